```python
import math
import jax, jax.numpy as jnp
from jax import lax
import numpy as np

D_MODEL = 1024
BATCH = 8
SEQ = 8192
DEPTH = 2

CHUNK = 64
Q_BLOCK = 128
SB_HEADS = 8
SB_HEAD_DIM = 64
SB_WIDTH = SB_HEADS * SB_HEAD_DIM
SSM_WIDTH = D_MODEL // 2
SSM_GROUP = 16
SSM_GROUPS = SSM_WIDTH // SSM_GROUP
SSM_STATE = 64
DT_MIN = 1e-3
DT_MAX = 1e-1
FFN_HIDDEN = ((8 * D_MODEL // 3 + 255) // 256) * 256
IN_SPLITS = (SB_WIDTH, 2 * SB_WIDTH, 3 * SB_WIDTH, 3 * SB_WIDTH + SSM_WIDTH,
             3 * SB_WIDTH + SSM_WIDTH + D_MODEL)
IN_COLS = 3 * SB_WIDTH + SSM_WIDTH + 2 * D_MODEL
N_MOD = 6
DEEPNORM_ALPHA = (2 * DEPTH) ** 0.25
DEEPNORM_BETA = (8 * DEPTH) ** -0.25
LN_EPS = 1e-5

kernel_name = "hybrid_sb_s5_deepnorm_adaln"


def _normalize(x):
    xf = x.astype(jnp.float32)
    mu = jnp.mean(xf, axis=-1, keepdims=True)
    var = jnp.mean(jnp.square(xf - mu), axis=-1, keepdims=True)
    return ((xf - mu) * lax.rsqrt(var + LN_EPS)).astype(x.dtype)


def _layer_norm(x, g, b):
    return _normalize(x) * g + b


def stick_breaking_attention(q, k, v):
    b, s, h, dh = q.shape
    nb = s // Q_BLOCK
    f32 = jnp.float32
    qb = q.astype(f32).reshape(b, nb, Q_BLOCK, h, dh).transpose(1, 0, 3, 2, 4)
    kt = k.astype(f32).transpose(0, 2, 1, 3)
    vt = v.astype(f32).transpose(0, 2, 1, 3)
    key_pos = jnp.arange(s, dtype=jnp.int32)
    scale = 1.0 / math.sqrt(dh)

    def one_block(args):
        q_blk, blk = args
        q_pos = blk * Q_BLOCK + jnp.arange(Q_BLOCK, dtype=jnp.int32)
        z = jnp.einsum('bhqd,bhkd->bhqk', q_blk, kt) * scale
        causal = key_pos[None, :] < q_pos[:, None]
        log_beta = jax.nn.log_sigmoid(z)
        log_one_minus = jnp.where(causal, log_beta - z, 0.0)
        after = lax.cumsum(log_one_minus, axis=3, reverse=True) - log_one_minus
        w = jnp.where(causal, jnp.exp(log_beta + after), 0.0)
        return jnp.einsum('bhqk,bhkd->bhqd', w, vt)

    out = lax.map(one_block, (qb, jnp.arange(nb, dtype=jnp.int32)))
    return out.transpose(1, 0, 3, 2, 4).reshape(b, s, h * dh).astype(v.dtype)


def s5_branch(u, a_re, a_im, log_dt, b_re, b_im, c_re, c_im, d_skip, w_glu, b_glu):
    f32 = jnp.float32
    c64 = jnp.complex64
    bsz, s, _ = u.shape
    n_chunks = s // CHUNK
    uf = u.astype(f32)
    lam = lax.complex(a_re.astype(f32), a_im.astype(f32))
    dt = jnp.exp(log_dt.astype(f32))[:, None]
    lam_dt = lam * dt
    lam_bar = jnp.exp(lam_dt)
    b_mat = lax.complex(b_re.astype(f32), b_im.astype(f32))
    b_bar = ((lam_bar - 1.0) / lam)[..., None] * b_mat
    c_mat = lax.complex(c_re.astype(f32), c_im.astype(f32))
    steps = jnp.arange(1, CHUNK + 1, dtype=f32)
    powers = jnp.exp(lam_dt[None] * steps[:, None, None].astype(c64))
    a_seq = jnp.broadcast_to(lam_bar[None, None], (CHUNK, bsz, SSM_GROUPS, SSM_STATE))

    u_chunks = uf.reshape(bsz, n_chunks, CHUNK, SSM_GROUPS, SSM_GROUP).transpose(1, 2, 0, 3, 4)

    def combine(left, right):
        a_l, b_l = left
        a_r, b_r = right
        return a_r * a_l, a_r * b_l + b_r

    def step(state, u_c):
        bu = jnp.einsum('gpc,lbgc->lbgp', b_bar, u_c.astype(c64))
        _, h_loc = lax.associative_scan(combine, (a_seq, bu), axis=0)
        h = h_loc + powers[:, None] * state[None]
        y = jnp.einsum('gcp,lbgp->lbgc', c_mat, h).real
        return h[-1], y

    state0 = jnp.zeros((bsz, SSM_GROUPS, SSM_STATE), c64)
    _, ys = lax.scan(step, state0, u_chunks)
    y = ys.transpose(2, 0, 1, 3, 4).reshape(bsz, s, SSM_WIDTH)
    y = y + d_skip.astype(f32) * uf
    y = jax.nn.gelu(y)
    y = y * jax.nn.sigmoid(y @ w_glu.astype(f32) + b_glu.astype(f32))
    return y.astype(u.dtype)


def token_mixer(h, w_in, w_sb_up, a_re, a_im, log_dt, b_re, b_im, c_re, c_im,
                d_skip, w_glu, b_glu, w_ssm_up, w_out):
    bsz, s, _ = h.shape
    proj = h @ w_in
    q, k, v, u, g_sb, g_ssm = jnp.split(proj, IN_SPLITS, axis=-1)
    shp = (bsz, s, SB_HEADS, SB_HEAD_DIM)
    y_sb = stick_breaking_attention(q.reshape(shp), k.reshape(shp), v.reshape(shp)) @ w_sb_up
    y_ssm = s5_branch(u, a_re, a_im, log_dt, b_re, b_im, c_re, c_im,
                      d_skip, w_glu, b_glu) @ w_ssm_up
    merged = jax.nn.sigmoid(g_sb) * y_sb + jax.nn.sigmoid(g_ssm) * y_ssm
    return merged @ w_out


def swiglu_ffn(h, w_ffn_in, w_ffn_out):
    gate, up = jnp.split(h @ w_ffn_in, 2, axis=-1)
    return (jax.nn.silu(gate) * up) @ w_ffn_out


def _fwd_setup_inputs(seed: int = 0) -> dict:
    key = jax.random.key(seed)
    ks = jax.random.split(key, 32)
    f32 = jnp.float32

    def nrm(k, shape, scale):
        return jax.random.normal(k, shape, f32) * scale

    G, P, Cg = SSM_GROUPS, SSM_STATE, SSM_GROUP
    n = jnp.arange(P, dtype=f32)
    return {
        "x": nrm(ks[0], (BATCH, SEQ, D_MODEL), 1.0),
        "c": nrm(ks[1], (BATCH, D_MODEL), 1.0),
        "w_ada": nrm(ks[2], (DEPTH, D_MODEL, N_MOD * D_MODEL), 0.5 * D_MODEL ** -0.5),
        "b_ada": nrm(ks[3], (DEPTH, N_MOD * D_MODEL), 0.02),
        "w_in": nrm(ks[4], (DEPTH, D_MODEL, IN_COLS), D_MODEL ** -0.5),
        "w_sb_up": nrm(ks[5], (DEPTH, SB_WIDTH, D_MODEL), SB_WIDTH ** -0.5),
        "ssm_a_re": -0.5 + nrm(ks[6], (DEPTH, G, P), 0.01),
        "ssm_a_im": math.pi * n + nrm(ks[7], (DEPTH, G, P), 0.01),
        "ssm_log_dt": jax.random.uniform(ks[8], (DEPTH, G), f32,
                                         math.log(DT_MIN), math.log(DT_MAX)),
        "ssm_b_re": nrm(ks[9], (DEPTH, G, P, Cg), (2 * Cg) ** -0.5),
        "ssm_b_im": nrm(ks[10], (DEPTH, G, P, Cg), (2 * Cg) ** -0.5),
        "ssm_c_re": nrm(ks[11], (DEPTH, G, Cg, P), P ** -0.5),
        "ssm_c_im": nrm(ks[12], (DEPTH, G, Cg, P), P ** -0.5),
        "ssm_d": 1.0 + nrm(ks[13], (DEPTH, SSM_WIDTH), 0.1),
        "w_glu": nrm(ks[14], (DEPTH, SSM_WIDTH, SSM_WIDTH), SSM_WIDTH ** -0.5),
        "b_glu": nrm(ks[15], (DEPTH, SSM_WIDTH), 0.02),
        "w_ssm_up": nrm(ks[16], (DEPTH, SSM_WIDTH, D_MODEL), SSM_WIDTH ** -0.5),
        "w_out": nrm(ks[17], (DEPTH, D_MODEL, D_MODEL), D_MODEL ** -0.5 * DEEPNORM_BETA),
        "ln1_g": 1.0 + nrm(ks[18], (DEPTH, D_MODEL), 0.02),
        "ln1_b": nrm(ks[19], (DEPTH, D_MODEL), 0.02),
        "w_ffn_in": nrm(ks[20], (DEPTH, D_MODEL, 2 * FFN_HIDDEN), D_MODEL ** -0.5),
        "w_ffn_out": nrm(ks[21], (DEPTH, FFN_HIDDEN, D_MODEL), FFN_HIDDEN ** -0.5 * DEEPNORM_BETA),
        "ln2_g": 1.0 + nrm(ks[22], (DEPTH, D_MODEL), 0.02),
        "ln2_b": nrm(ks[23], (DEPTH, D_MODEL), 0.02),
    }


def _fwd_reference(x, c, w_ada, b_ada, w_in, w_sb_up, ssm_a_re, ssm_a_im, ssm_log_dt,
              ssm_b_re, ssm_b_im, ssm_c_re, ssm_c_im, ssm_d, w_glu, b_glu,
              w_ssm_up, w_out, ln1_g, ln1_b, w_ffn_in, w_ffn_out, ln2_g, ln2_b):
    c_act = jax.nn.silu(c)
    for l in range(DEPTH):
        mod = c_act @ w_ada[l] + b_ada[l]
        sh_m, sc_m, g_m, sh_f, sc_f, g_f = [m[:, None, :] for m in jnp.split(mod, N_MOD, axis=-1)]
        h = _normalize(x) * (1.0 + sc_m) + sh_m
        y = token_mixer(h, w_in[l], w_sb_up[l], ssm_a_re[l], ssm_a_im[l], ssm_log_dt[l],
                        ssm_b_re[l], ssm_b_im[l], ssm_c_re[l], ssm_c_im[l], ssm_d[l],
                        w_glu[l], b_glu[l], w_ssm_up[l], w_out[l])
        x = _layer_norm(DEEPNORM_ALPHA * x + (1.0 + g_m) * y, ln1_g[l], ln1_b[l])
        h = _normalize(x) * (1.0 + sc_f) + sh_f
        y = swiglu_ffn(h, w_ffn_in[l], w_ffn_out[l])
        x = _layer_norm(DEEPNORM_ALPHA * x + (1.0 + g_f) * y, ln2_g[l], ln2_b[l])
    return x


import jax as _jax
import jax.numpy as _jnp

TWIN_FORMAT = 'train_step'
FWD_PARAMS = ['x', 'c', 'w_ada', 'b_ada', 'w_in', 'w_sb_up', 'ssm_a_re', 'ssm_a_im', 'ssm_log_dt', 'ssm_b_re', 'ssm_b_im', 'ssm_c_re', 'ssm_c_im', 'ssm_d', 'w_glu', 'b_glu', 'w_ssm_up', 'w_out', 'ln1_g', 'ln1_b', 'w_ffn_in', 'w_ffn_out', 'ln2_g', 'ln2_b']
TWIN_WEIGHTS = ['w_ada', 'b_ada', 'w_in', 'w_sb_up', 'ssm_a_re', 'ssm_a_im', 'ssm_log_dt', 'ssm_b_re', 'ssm_b_im', 'ssm_c_re', 'ssm_c_im', 'ssm_d', 'w_glu', 'b_glu', 'w_ssm_up', 'w_out', 'ln1_g', 'ln1_b', 'w_ffn_in', 'w_ffn_out', 'ln2_g', 'ln2_b']
TWIN_DIFF_INPUT = 'x'
TWIN_INPUTS = ['x', 'c', 'w_ada', 'b_ada', 'w_in', 'w_sb_up', 'ssm_a_re', 'ssm_a_im', 'ssm_log_dt', 'ssm_b_re', 'ssm_b_im', 'ssm_c_re', 'ssm_c_im', 'ssm_d', 'w_glu', 'b_glu', 'w_ssm_up', 'w_out', 'ln1_g', 'ln1_b', 'w_ffn_in', 'w_ffn_out', 'ln2_g', 'ln2_b', 'loss_target', 'm_w_ada', 'm_b_ada', 'm_w_in', 'm_w_sb_up', 'm_ssm_a_re', 'm_ssm_a_im', 'm_ssm_log_dt', 'm_ssm_b_re', 'm_ssm_b_im', 'm_ssm_c_re', 'm_ssm_c_im', 'm_ssm_d', 'm_w_glu', 'm_b_glu', 'm_w_ssm_up', 'm_w_out', 'm_ln1_g', 'm_ln1_b', 'm_w_ffn_in', 'm_w_ffn_out', 'm_ln2_g', 'm_ln2_b', 'v_w_ada', 'v_b_ada', 'v_w_in', 'v_w_sb_up', 'v_ssm_a_re', 'v_ssm_a_im', 'v_ssm_log_dt', 'v_ssm_b_re', 'v_ssm_b_im', 'v_ssm_c_re', 'v_ssm_c_im', 'v_ssm_d', 'v_w_glu', 'v_b_glu', 'v_w_ssm_up', 'v_w_out', 'v_ln1_g', 'v_ln1_b', 'v_w_ffn_in', 'v_w_ffn_out', 'v_ln2_g', 'v_ln2_b']
TWIN_OUTPUTS = ['loss', 'grad_x', 'grad_w_ada', 'grad_b_ada', 'grad_w_in', 'grad_w_sb_up', 'grad_ssm_a_re', 'grad_ssm_a_im', 'grad_ssm_log_dt', 'grad_ssm_b_re', 'grad_ssm_b_im', 'grad_ssm_c_re', 'grad_ssm_c_im', 'grad_ssm_d', 'grad_w_glu', 'grad_b_glu', 'grad_w_ssm_up', 'grad_w_out', 'grad_ln1_g', 'grad_ln1_b', 'grad_w_ffn_in', 'grad_w_ffn_out', 'grad_ln2_g', 'grad_ln2_b', 'delta_w_ada', 'delta_b_ada', 'delta_w_in', 'delta_w_sb_up', 'delta_ssm_a_re', 'delta_ssm_a_im', 'delta_ssm_log_dt', 'delta_ssm_b_re', 'delta_ssm_b_im', 'delta_ssm_c_re', 'delta_ssm_c_im', 'delta_ssm_d', 'delta_w_glu', 'delta_b_glu', 'delta_w_ssm_up', 'delta_w_out', 'delta_ln1_g', 'delta_ln1_b', 'delta_w_ffn_in', 'delta_w_ffn_out', 'delta_ln2_g', 'delta_ln2_b', 'new_m_w_ada', 'new_m_b_ada', 'new_m_w_in', 'new_m_w_sb_up', 'new_m_ssm_a_re', 'new_m_ssm_a_im', 'new_m_ssm_log_dt', 'new_m_ssm_b_re', 'new_m_ssm_b_im', 'new_m_ssm_c_re', 'new_m_ssm_c_im', 'new_m_ssm_d', 'new_m_w_glu', 'new_m_b_glu', 'new_m_w_ssm_up', 'new_m_w_out', 'new_m_ln1_g', 'new_m_ln1_b', 'new_m_w_ffn_in', 'new_m_w_ffn_out', 'new_m_ln2_g', 'new_m_ln2_b', 'new_v_w_ada', 'new_v_b_ada', 'new_v_w_in', 'new_v_w_sb_up', 'new_v_ssm_a_re', 'new_v_ssm_a_im', 'new_v_ssm_log_dt', 'new_v_ssm_b_re', 'new_v_ssm_b_im', 'new_v_ssm_c_re', 'new_v_ssm_c_im', 'new_v_ssm_d', 'new_v_w_glu', 'new_v_b_glu', 'new_v_w_ssm_up', 'new_v_w_out', 'new_v_ln1_g', 'new_v_ln1_b', 'new_v_w_ffn_in', 'new_v_w_ffn_out', 'new_v_ln2_g', 'new_v_ln2_b']
TWIN_LEAF_KINDS = {'loss': 'loss', 'grad_x': 'grad_x', 'grad_w_ada': 'grad_w', 'grad_b_ada': 'grad_w', 'grad_w_in': 'grad_w', 'grad_w_sb_up': 'grad_w', 'grad_ssm_a_re': 'grad_w', 'grad_ssm_a_im': 'grad_w', 'grad_ssm_log_dt': 'grad_w', 'grad_ssm_b_re': 'grad_w', 'grad_ssm_b_im': 'grad_w', 'grad_ssm_c_re': 'grad_w', 'grad_ssm_c_im': 'grad_w', 'grad_ssm_d': 'grad_w', 'grad_w_glu': 'grad_w', 'grad_b_glu': 'grad_w', 'grad_w_ssm_up': 'grad_w', 'grad_w_out': 'grad_w', 'grad_ln1_g': 'grad_w', 'grad_ln1_b': 'grad_w', 'grad_w_ffn_in': 'grad_w', 'grad_w_ffn_out': 'grad_w', 'grad_ln2_g': 'grad_w', 'grad_ln2_b': 'grad_w', 'delta_w_ada': 'delta_w', 'delta_b_ada': 'delta_w', 'delta_w_in': 'delta_w', 'delta_w_sb_up': 'delta_w', 'delta_ssm_a_re': 'delta_w', 'delta_ssm_a_im': 'delta_w', 'delta_ssm_log_dt': 'delta_w', 'delta_ssm_b_re': 'delta_w', 'delta_ssm_b_im': 'delta_w', 'delta_ssm_c_re': 'delta_w', 'delta_ssm_c_im': 'delta_w', 'delta_ssm_d': 'delta_w', 'delta_w_glu': 'delta_w', 'delta_b_glu': 'delta_w', 'delta_w_ssm_up': 'delta_w', 'delta_w_out': 'delta_w', 'delta_ln1_g': 'delta_w', 'delta_ln1_b': 'delta_w', 'delta_w_ffn_in': 'delta_w', 'delta_w_ffn_out': 'delta_w', 'delta_ln2_g': 'delta_w', 'delta_ln2_b': 'delta_w', 'new_m_w_ada': 'new_m', 'new_m_b_ada': 'new_m', 'new_m_w_in': 'new_m', 'new_m_w_sb_up': 'new_m', 'new_m_ssm_a_re': 'new_m', 'new_m_ssm_a_im': 'new_m', 'new_m_ssm_log_dt': 'new_m', 'new_m_ssm_b_re': 'new_m', 'new_m_ssm_b_im': 'new_m', 'new_m_ssm_c_re': 'new_m', 'new_m_ssm_c_im': 'new_m', 'new_m_ssm_d': 'new_m', 'new_m_w_glu': 'new_m', 'new_m_b_glu': 'new_m', 'new_m_w_ssm_up': 'new_m', 'new_m_w_out': 'new_m', 'new_m_ln1_g': 'new_m', 'new_m_ln1_b': 'new_m', 'new_m_w_ffn_in': 'new_m', 'new_m_w_ffn_out': 'new_m', 'new_m_ln2_g': 'new_m', 'new_m_ln2_b': 'new_m', 'new_v_w_ada': 'new_v', 'new_v_b_ada': 'new_v', 'new_v_w_in': 'new_v', 'new_v_w_sb_up': 'new_v', 'new_v_ssm_a_re': 'new_v', 'new_v_ssm_a_im': 'new_v', 'new_v_ssm_log_dt': 'new_v', 'new_v_ssm_b_re': 'new_v', 'new_v_ssm_b_im': 'new_v', 'new_v_ssm_c_re': 'new_v', 'new_v_ssm_c_im': 'new_v', 'new_v_ssm_d': 'new_v', 'new_v_w_glu': 'new_v', 'new_v_b_glu': 'new_v', 'new_v_w_ssm_up': 'new_v', 'new_v_w_out': 'new_v', 'new_v_ln1_g': 'new_v', 'new_v_ln1_b': 'new_v', 'new_v_w_ffn_in': 'new_v', 'new_v_w_ffn_out': 'new_v', 'new_v_ln2_g': 'new_v', 'new_v_ln2_b': 'new_v'}


def _forward(args):
    return _fwd_reference(*[args[k] for k in FWD_PARAMS])


def _output_shape():
    def fwd():
        inp = _fwd_setup_inputs(0)
        return _fwd_reference(*[inp[k] for k in FWD_PARAMS])
    out = _jax.eval_shape(fwd)
    return out.shape, out.dtype

N_MICROBATCH = 1
ADAM_LR = 0.001
ADAM_B1 = 0.9
ADAM_B2 = 0.999
ADAM_EPS = 1e-08
ADAM_WD = 0.01
ADAM_STEP = 10
PER_EXAMPLE_BATCH_AXIS = {'x': 0, 'c': 0, 'loss_target': 0}
SHARED_INPUTS = []
_WEIGHT_DTYPES = {'w_ada': _jnp.float32, 'b_ada': _jnp.float32, 'w_in': _jnp.float32, 'w_sb_up': _jnp.float32, 'ssm_a_re': _jnp.float32, 'ssm_a_im': _jnp.float32, 'ssm_log_dt': _jnp.float32, 'ssm_b_re': _jnp.float32, 'ssm_b_im': _jnp.float32, 'ssm_c_re': _jnp.float32, 'ssm_c_im': _jnp.float32, 'ssm_d': _jnp.float32, 'w_glu': _jnp.float32, 'b_glu': _jnp.float32, 'w_ssm_up': _jnp.float32, 'w_out': _jnp.float32, 'ln1_g': _jnp.float32, 'ln1_b': _jnp.float32, 'w_ffn_in': _jnp.float32, 'w_ffn_out': _jnp.float32, 'ln2_g': _jnp.float32, 'ln2_b': _jnp.float32}
MOMENT_SCALE = {'w_ada': 4.610784e-02, 'b_ada': 1.021407e-01, 'w_in': 2.959856e-02, 'w_sb_up': 4.420858e-02, 'ssm_a_re': 6.568846e-03, 'ssm_a_im': 3.974663e-03, 'ssm_log_dt': 1.549955e+00, 'ssm_b_re': 2.259950e-03, 'ssm_b_im': 2.276769e-03, 'ssm_c_re': 3.260860e-03, 'ssm_c_im': 2.919069e-03, 'ssm_d': 5.830259e-02, 'w_glu': 9.926769e-03, 'b_glu': 1.825318e-02, 'w_ssm_up': 3.004081e-02, 'w_out': 1.050699e-01, 'ln1_g': 1.999567e+00, 'ln1_b': 9.668173e-01, 'w_ffn_in': 4.046977e-02, 'w_ffn_out': 1.326415e-01, 'ln2_g': 4.538546e+01, 'ln2_b': 3.025939e+00}


def _to_microbatches(a, axis):
    t = _jnp.moveaxis(a, axis, 0)
    t = t.reshape((N_MICROBATCH, t.shape[0] // N_MICROBATCH) + t.shape[1:])
    return _jnp.moveaxis(t, 1, axis + 1)


def setup_inputs(seed: int = 0) -> dict:
    inp = _fwd_setup_inputs(seed)
    key = _jax.random.fold_in(_jax.random.key(seed), 7919)
    shape, _ = _output_shape()
    out = dict(inp)
    out["loss_target"] = _jax.random.normal(_jax.random.fold_in(key, 0), shape, _jnp.float32)
    for i, name in enumerate(TWIN_WEIGHTS):
        w = inp[name].astype(_jnp.float32)
        if MOMENT_SCALE is None:
            s = _jnp.sqrt(_jnp.mean(_jnp.square(w)) + 1e-30)
        else:
            s = MOMENT_SCALE[name]
        km, kv = _jax.random.split(_jax.random.fold_in(key, i + 1))
        out[name] = w
        out["m_" + name] = s * _jax.random.normal(km, w.shape, _jnp.float32)
        out["v_" + name] = (s * s) * _jax.random.uniform(kv, w.shape, _jnp.float32, 0.5, 1.5)
    if N_MICROBATCH > 1:
        for name, axis in PER_EXAMPLE_BATCH_AXIS.items():
            out[name] = _to_microbatches(out[name], axis)
    return {'x': out['x'], 'c': out['c'], 'w_ada': out['w_ada'], 'b_ada': out['b_ada'], 'w_in': out['w_in'], 'w_sb_up': out['w_sb_up'], 'ssm_a_re': out['ssm_a_re'], 'ssm_a_im': out['ssm_a_im'], 'ssm_log_dt': out['ssm_log_dt'], 'ssm_b_re': out['ssm_b_re'], 'ssm_b_im': out['ssm_b_im'], 'ssm_c_re': out['ssm_c_re'], 'ssm_c_im': out['ssm_c_im'], 'ssm_d': out['ssm_d'], 'w_glu': out['w_glu'], 'b_glu': out['b_glu'], 'w_ssm_up': out['w_ssm_up'], 'w_out': out['w_out'], 'ln1_g': out['ln1_g'], 'ln1_b': out['ln1_b'], 'w_ffn_in': out['w_ffn_in'], 'w_ffn_out': out['w_ffn_out'], 'ln2_g': out['ln2_g'], 'ln2_b': out['ln2_b'], 'loss_target': out['loss_target'], 'm_w_ada': out['m_w_ada'], 'm_b_ada': out['m_b_ada'], 'm_w_in': out['m_w_in'], 'm_w_sb_up': out['m_w_sb_up'], 'm_ssm_a_re': out['m_ssm_a_re'], 'm_ssm_a_im': out['m_ssm_a_im'], 'm_ssm_log_dt': out['m_ssm_log_dt'], 'm_ssm_b_re': out['m_ssm_b_re'], 'm_ssm_b_im': out['m_ssm_b_im'], 'm_ssm_c_re': out['m_ssm_c_re'], 'm_ssm_c_im': out['m_ssm_c_im'], 'm_ssm_d': out['m_ssm_d'], 'm_w_glu': out['m_w_glu'], 'm_b_glu': out['m_b_glu'], 'm_w_ssm_up': out['m_w_ssm_up'], 'm_w_out': out['m_w_out'], 'm_ln1_g': out['m_ln1_g'], 'm_ln1_b': out['m_ln1_b'], 'm_w_ffn_in': out['m_w_ffn_in'], 'm_w_ffn_out': out['m_w_ffn_out'], 'm_ln2_g': out['m_ln2_g'], 'm_ln2_b': out['m_ln2_b'], 'v_w_ada': out['v_w_ada'], 'v_b_ada': out['v_b_ada'], 'v_w_in': out['v_w_in'], 'v_w_sb_up': out['v_w_sb_up'], 'v_ssm_a_re': out['v_ssm_a_re'], 'v_ssm_a_im': out['v_ssm_a_im'], 'v_ssm_log_dt': out['v_ssm_log_dt'], 'v_ssm_b_re': out['v_ssm_b_re'], 'v_ssm_b_im': out['v_ssm_b_im'], 'v_ssm_c_re': out['v_ssm_c_re'], 'v_ssm_c_im': out['v_ssm_c_im'], 'v_ssm_d': out['v_ssm_d'], 'v_w_glu': out['v_w_glu'], 'v_b_glu': out['v_b_glu'], 'v_w_ssm_up': out['v_w_ssm_up'], 'v_w_out': out['v_w_out'], 'v_ln1_g': out['v_ln1_g'], 'v_ln1_b': out['v_ln1_b'], 'v_w_ffn_in': out['v_w_ffn_in'], 'v_w_ffn_out': out['v_w_ffn_out'], 'v_ln2_g': out['v_ln2_g'], 'v_ln2_b': out['v_ln2_b']}


def _loss(weights, diff, rest, loss_target):
    with _jax.named_scope("forward"):
        args = {**rest, TWIN_DIFF_INPUT: diff, **{k: w.astype(_WEIGHT_DTYPES[k]) for k, w in weights.items()}}
        y = _forward(args)
    with _jax.named_scope("loss_head"):
        err = _jnp.square(y.astype(_jnp.float32) - loss_target)
        return 0.5 * _jnp.sum(_jnp.mean(err, axis=-1)) if err.ndim else 0.5 * err


def _adamw(w, g, m, v):
    m = ADAM_B1 * m + (1.0 - ADAM_B1) * g
    v = ADAM_B2 * v + (1.0 - ADAM_B2) * _jnp.square(g)
    m_hat = m / (1.0 - ADAM_B1 ** ADAM_STEP)
    v_hat = v / (1.0 - ADAM_B2 ** ADAM_STEP)
    delta = -ADAM_LR * (m_hat / (_jnp.sqrt(v_hat) + ADAM_EPS) + ADAM_WD * w)
    return delta, m, v


def reference(x, c, w_ada, b_ada, w_in, w_sb_up, ssm_a_re, ssm_a_im, ssm_log_dt, ssm_b_re, ssm_b_im, ssm_c_re, ssm_c_im, ssm_d, w_glu, b_glu, w_ssm_up, w_out, ln1_g, ln1_b, w_ffn_in, w_ffn_out, ln2_g, ln2_b, loss_target, m_w_ada, m_b_ada, m_w_in, m_w_sb_up, m_ssm_a_re, m_ssm_a_im, m_ssm_log_dt, m_ssm_b_re, m_ssm_b_im, m_ssm_c_re, m_ssm_c_im, m_ssm_d, m_w_glu, m_b_glu, m_w_ssm_up, m_w_out, m_ln1_g, m_ln1_b, m_w_ffn_in, m_w_ffn_out, m_ln2_g, m_ln2_b, v_w_ada, v_b_ada, v_w_in, v_w_sb_up, v_ssm_a_re, v_ssm_a_im, v_ssm_log_dt, v_ssm_b_re, v_ssm_b_im, v_ssm_c_re, v_ssm_c_im, v_ssm_d, v_w_glu, v_b_glu, v_w_ssm_up, v_w_out, v_ln1_g, v_ln1_b, v_w_ffn_in, v_w_ffn_out, v_ln2_g, v_ln2_b):
    given = dict(x=x, c=c, w_ada=w_ada, b_ada=b_ada, w_in=w_in, w_sb_up=w_sb_up, ssm_a_re=ssm_a_re, ssm_a_im=ssm_a_im, ssm_log_dt=ssm_log_dt, ssm_b_re=ssm_b_re, ssm_b_im=ssm_b_im, ssm_c_re=ssm_c_re, ssm_c_im=ssm_c_im, ssm_d=ssm_d, w_glu=w_glu, b_glu=b_glu, w_ssm_up=w_ssm_up, w_out=w_out, ln1_g=ln1_g, ln1_b=ln1_b, w_ffn_in=w_ffn_in, w_ffn_out=w_ffn_out, ln2_g=ln2_g, ln2_b=ln2_b, loss_target=loss_target, m_w_ada=m_w_ada, m_b_ada=m_b_ada, m_w_in=m_w_in, m_w_sb_up=m_w_sb_up, m_ssm_a_re=m_ssm_a_re, m_ssm_a_im=m_ssm_a_im, m_ssm_log_dt=m_ssm_log_dt, m_ssm_b_re=m_ssm_b_re, m_ssm_b_im=m_ssm_b_im, m_ssm_c_re=m_ssm_c_re, m_ssm_c_im=m_ssm_c_im, m_ssm_d=m_ssm_d, m_w_glu=m_w_glu, m_b_glu=m_b_glu, m_w_ssm_up=m_w_ssm_up, m_w_out=m_w_out, m_ln1_g=m_ln1_g, m_ln1_b=m_ln1_b, m_w_ffn_in=m_w_ffn_in, m_w_ffn_out=m_w_ffn_out, m_ln2_g=m_ln2_g, m_ln2_b=m_ln2_b, v_w_ada=v_w_ada, v_b_ada=v_b_ada, v_w_in=v_w_in, v_w_sb_up=v_w_sb_up, v_ssm_a_re=v_ssm_a_re, v_ssm_a_im=v_ssm_a_im, v_ssm_log_dt=v_ssm_log_dt, v_ssm_b_re=v_ssm_b_re, v_ssm_b_im=v_ssm_b_im, v_ssm_c_re=v_ssm_c_re, v_ssm_c_im=v_ssm_c_im, v_ssm_d=v_ssm_d, v_w_glu=v_w_glu, v_b_glu=v_b_glu, v_w_ssm_up=v_w_ssm_up, v_w_out=v_w_out, v_ln1_g=v_ln1_g, v_ln1_b=v_ln1_b, v_w_ffn_in=v_w_ffn_in, v_w_ffn_out=v_w_ffn_out, v_ln2_g=v_ln2_g, v_ln2_b=v_ln2_b)
    weights = {n: given[n] for n in TWIN_WEIGHTS}
    shared = {n: given[n] for n in SHARED_INPUTS}
    per_example = {n: given[n] for n in ['x', 'c']}
    grad_fn = _jax.value_and_grad(_loss, argnums=(0, 1))

    def one_microbatch(ex, loss_target):
        ex = dict(ex)
        diff = ex.pop(TWIN_DIFF_INPUT)
        return grad_fn(weights, diff, {**shared, **ex}, loss_target)

    if N_MICROBATCH == 1:
        loss, (grad_w, grad_x) = one_microbatch(per_example, given["loss_target"])
    else:
        def body(carry, xs):
            loss_sum, grad_sum = carry
            l_k, (gw_k, gx_k) = one_microbatch(xs[0], xs[1])
            with _jax.named_scope("update"):
                return (loss_sum + l_k, _jax.tree.map(_jnp.add, grad_sum, gw_k)), gx_k

        init = (_jnp.zeros((), _jnp.float32), _jax.tree.map(_jnp.zeros_like, weights))
        (loss, grad_w), grad_x = _jax.lax.scan(body, init, (per_example, given["loss_target"]))
    with _jax.named_scope("update"):
        delta_w, new_m, new_v = {}, {}, {}
        for n in TWIN_WEIGHTS:
            delta_w[n], new_m[n], new_v[n] = _adamw(weights[n], grad_w[n], given["m_" + n], given["v_" + n])
    return (loss, grad_x, *[grad_w[n] for n in TWIN_WEIGHTS], *[delta_w[n] for n in TWIN_WEIGHTS],
            *[new_m[n] for n in TWIN_WEIGHTS], *[new_v[n] for n in TWIN_WEIGHTS])
```

```python
import functools
import math

import jax
import jax.numpy as jnp
from jax import lax
from jax.experimental import pallas as pl
from jax.experimental.pallas import tpu as pltpu

F32 = jnp.float32
BF16 = jnp.bfloat16
MESH = pl.DeviceIdType.MESH

LANES = 128
HEAD_DIM = 64
ATT_TILE = 256
SSM_GROUPS, SSM_STATE, SSM_GROUP = 32, 64, 16
N_STATE = SSM_GROUPS * SSM_STATE
LN_EPS = 1e-5
DEPTH = 2
ALPHA = (2 * DEPTH) ** 0.25
ADAM_LR, ADAM_B1, ADAM_B2, ADAM_EPS, ADAM_WD, ADAM_STEP = 0.001, 0.9, 0.999, 1e-08, 0.01, 10
VMEM_LIMIT = 56 * 1024 * 1024
GELU_K = math.sqrt(2.0 / math.pi)
GELU_C = 0.044715


def _params(**kw):
    return pltpu.CompilerParams(vmem_limit_bytes=VMEM_LIMIT, **kw)


def _pick(n, prefs):
    for p in prefs:
        if n % p == 0:
            return p
    return n


def _rowwise(name, fn, rows, vecs, outs, sums=(), tm=None):
    s = rows[0][0].shape[0]
    tm = tm or _pick(s, (256, 128, 64, 8))
    nin, no, ns = len(rows) + len(vecs), len(outs), len(sums)

    def body(*refs):
        res = fn(*[r[...] for r in refs[:nin]])
        res = res if isinstance(res, tuple) else (res,)
        for r, v in zip(refs[nin:nin + no], res[:no]):
            r[...] = v.astype(r.dtype)
        if ns:
            @pl.when(pl.program_id(0) == 0)
            def _():
                for r in refs[nin + no:]:
                    r[...] = jnp.zeros_like(r)
            for r, v in zip(refs[nin + no:], res[no:]):
                r[...] += v

    in_specs = [pl.BlockSpec((tm, w), lambda i, cb=cb: (i, cb)) for _, cb, w in rows]
    in_specs += [pl.BlockSpec(v.shape, lambda i: (0, 0)) for v in vecs]
    out_specs = [pl.BlockSpec((tm, w), lambda i: (i, 0)) for w, _ in outs]
    out_specs += [pl.BlockSpec((1, w), lambda i: (0, 0)) for w in sums]
    out_shape = [jax.ShapeDtypeStruct((s, w), dt) for w, dt in outs]
    out_shape += [jax.ShapeDtypeStruct((1, w), F32) for w in sums]
    res = pl.pallas_call(
        body, name=name, grid=(s // tm,), in_specs=in_specs, out_specs=out_specs, out_shape=out_shape,
        compiler_params=_params(dimension_semantics=("arbitrary",)),
    )(*[a for a, _, _ in rows], *vecs)
    return res[0] if len(res) == 1 else tuple(res)


def _mm(name, a, b, mode, out_dtype=F32, a_cols=None):
    a_off, a_w = a_cols if a_cols else (0, a.shape[1])
    if mode == "nn":
        m, k, n = a.shape[0], a_w, b.shape[1]
    elif mode == "nt":
        m, k, n = a.shape[0], a_w, b.shape[0]
    else:
        k, m, n = a.shape[0], a_w, b.shape[1]
    tm = _pick(m, (512, 256, 128))
    tn = _pick(n, (512, 256, 128))
    tk = _pick(k, (512, 256, 128))
    nk = k // tk
    dims = {"nn": ((1,), (0,)), "nt": ((1,), (1,)), "tn": ((0,), (0,))}[mode]

    def body(a_ref, b_ref, o_ref, acc_ref):
        kk = pl.program_id(2)

        @pl.when(kk == 0)
        def _():
            acc_ref[...] = jnp.zeros_like(acc_ref)

        acc_ref[...] += lax.dot_general(a_ref[...].astype(BF16), b_ref[...].astype(BF16), (dims, ((), ())),
                                        preferred_element_type=F32)

        @pl.when(kk == nk - 1)
        def _():
            o_ref[...] = acc_ref[...].astype(o_ref.dtype)

    if mode == "tn":
        a_spec = pl.BlockSpec((tk, tm), lambda i, j, kk: (kk, a_off // tm + i))
    else:
        a_spec = pl.BlockSpec((tm, tk), lambda i, j, kk: (i, a_off // tk + kk))
    if mode == "nt":
        b_spec = pl.BlockSpec((tn, tk), lambda i, j, kk: (j, kk))
    else:
        b_spec = pl.BlockSpec((tk, tn), lambda i, j, kk: (kk, j))
    return pl.pallas_call(
        body, name=name, grid=(m // tm, n // tn, nk), in_specs=[a_spec, b_spec],
        out_specs=pl.BlockSpec((tm, tn), lambda i, j, kk: (i, j)),
        out_shape=jax.ShapeDtypeStruct((m, n), out_dtype),
        scratch_shapes=[pltpu.VMEM((tm, tn), F32)],
        compiler_params=_params(dimension_semantics=("arbitrary", "arbitrary", "arbitrary")),
    )(a, b)


def _small_mm(name, a, b, mode):
    dims = {"nn": ((1,), (0,)), "tn": ((0,), (0,))}[mode]
    m = a.shape[0] if mode == "nn" else a.shape[1]

    def body(a_ref, b_ref, o_ref):
        o_ref[...] = lax.dot_general(a_ref[...], b_ref[...], (dims, ((), ())), precision=lax.Precision.HIGHEST,
                                     preferred_element_type=F32)

    return pl.pallas_call(body, name=name, out_shape=jax.ShapeDtypeStruct((m, b.shape[1]), F32),
                          compiler_params=_params())(a, b)


def _norm(x):
    mu = jnp.mean(x, axis=-1, keepdims=True)
    xc = x - mu
    rstd = lax.rsqrt(jnp.mean(xc * xc, axis=-1, keepdims=True) + LN_EPS)
    return xc * rstd, rstd


def _norm_bwd(dn, n, rstd):
    return rstd * (dn - jnp.mean(dn, axis=-1, keepdims=True) - n * jnp.mean(dn * n, axis=-1, keepdims=True))


def _colsum(v):
    return jnp.sum(v, axis=0, keepdims=True)


def _gelu(x):
    return 0.5 * x * (1.0 + jnp.tanh(GELU_K * (x + GELU_C * x * x * x)))


def _gelu_grad(x):
    t = jnp.tanh(GELU_K * (x + GELU_C * x * x * x))
    return 0.5 * (1.0 + t) + 0.5 * x * (1.0 - t * t) * GELU_K * (1.0 + 3.0 * GELU_C * x * x)


def _log_sigmoid_parts(z):
    lb = jnp.minimum(z, 0.0) - jnp.log(1.0 + jnp.exp(-jnp.abs(z)))
    return lb, lb - z


def _qkv_prep(proj, t):
    s = proj.shape[0]
    nb, nhp = s // t, 512 // LANES

    def body(q_ref, k_ref, v_ref, qs_ref, kb_ref, vb_ref, kt_ref, vt_ref):
        qs_ref[...] = (q_ref[...] * (1.0 / math.sqrt(HEAD_DIM))).astype(BF16)
        k, v = k_ref[...], v_ref[...]
        kb_ref[...] = k.astype(BF16)
        vb_ref[...] = v.astype(BF16)
        for hp in range(nhp):
            kt_ref[hp, 0] = k[:, hp * LANES:(hp + 1) * LANES].T.astype(BF16)
            vt_ref[hp, 0] = v[:, hp * LANES:(hp + 1) * LANES].T.astype(BF16)

    col = lambda cb: pl.BlockSpec((t, 512), lambda i, cb=cb: (i, cb))
    row_out = pl.BlockSpec((t, 512), lambda i: (i, 0))
    t_out = pl.BlockSpec((nhp, 1, LANES, t), lambda i: (0, i, 0, 0))
    return pl.pallas_call(
        body, name="qkv_prep", grid=(nb,), in_specs=[col(0), col(1), col(2)],
        out_specs=[row_out, row_out, row_out, t_out, t_out],
        out_shape=[jax.ShapeDtypeStruct((s, 512), BF16)] * 3 + [jax.ShapeDtypeStruct((nhp, nb, LANES, t), BF16)] * 2,
        compiler_params=_params(dimension_semantics=("arbitrary",)),
    )(proj, proj, proj)


def _tile_masks(t):
    row = lax.broadcasted_iota(jnp.int32, (t, t), 0)
    col = lax.broadcasted_iota(jnp.int32, (t, t), 1)
    return row, col


def _attn_fwd(qs, k, vt3, t):
    s = qs.shape[0]
    nb, nhp = s // t, qs.shape[1] // LANES

    def body(q_ref, k_ref, vt_ref, o_ref, car_ref):
        i = pl.program_id(1)
        q2 = q_ref[...]
        lane_q = lax.broadcasted_iota(jnp.int32, q2.shape, 1)
        row, col = _tile_masks(t)
        later = (col > row).astype(BF16)
        valid = row < col
        orow = lax.broadcasted_iota(jnp.int32, (LANES, t), 0)
        car_ref[...] = jnp.zeros_like(car_ref)
        accs = []
        for hh in range(2):
            qh = jnp.where((lane_q < HEAD_DIM) == (hh == 0), q2, jnp.zeros_like(q2))

            def block(j, c_after, acc, masked, hh=hh, qh=qh):
                kb = k_ref[pl.ds(pl.multiple_of(j * t, t), t), :]
                z = lax.dot_general(kb, qh, (((1,), (1,)), ((), ())), preferred_element_type=F32)
                lb, l1m = _log_sigmoid_parts(z)
                if masked:
                    l1m = jnp.where(valid, l1m, 0.0)
                aft = jnp.dot(later, l1m.astype(BF16), preferred_element_type=F32)
                w = jnp.exp(lb + aft + c_after)
                if masked:
                    w = jnp.where(valid, w, 0.0)
                car_ref[hh, pl.ds(j, 1), :] = c_after
                acc = acc + jnp.dot(vt_ref[0, j], w.astype(BF16), preferred_element_type=F32)
                return c_after + _colsum(l1m), acc

            c_after, acc = block(i, jnp.zeros((1, t), F32), jnp.zeros((LANES, t), F32), True)
            c_after, acc = lax.fori_loop(0, i, lambda jj, c: block(i - 1 - jj, c[0], c[1], False), (c_after, acc))
            accs.append(acc)
        o_ref[...] = jnp.where(orow < HEAD_DIM, accs[0], accs[1]).T.astype(o_ref.dtype)

    return pl.pallas_call(
        body, name="attn_fwd", grid=(nhp, nb),
        in_specs=[pl.BlockSpec((t, LANES), lambda hp, i: (i, hp)),
                  pl.BlockSpec((s, LANES), lambda hp, i: (0, hp)),
                  pl.BlockSpec((1, nb, LANES, t), lambda hp, i: (hp, 0, 0, 0))],
        out_specs=[pl.BlockSpec((t, LANES), lambda hp, i: (i, hp)),
                   pl.BlockSpec((2, nb, t), lambda hp, i: (hp, 0, i))],
        out_shape=[jax.ShapeDtypeStruct((s, nhp * LANES), BF16), jax.ShapeDtypeStruct((2 * nhp, nb, s), F32)],
        compiler_params=_params(dimension_semantics=("arbitrary", "arbitrary")),
    )(qs, k, vt3)


def _attn_bwd(qs, do, k, v, kt3, car, t):
    s = qs.shape[0]
    nb, nhp = s // t, qs.shape[1] // LANES

    def body(q_ref, do_ref, k_ref, v_ref, kt_ref, car_ref, dq_ref, dk_ref, dv_ref):
        i = pl.program_id(1)

        @pl.when(i == 0)
        def _():
            dk_ref[...] = jnp.zeros_like(dk_ref)
            dv_ref[...] = jnp.zeros_like(dv_ref)

        q2, do2 = q_ref[...], do_ref[...]
        lane_q = lax.broadcasted_iota(jnp.int32, q2.shape, 1)
        row, col = _tile_masks(t)
        later = (col > row).astype(BF16)
        earlier = (col < row).astype(BF16)
        valid = row < col
        orow = lax.broadcasted_iota(jnp.int32, (LANES, t), 0)
        dqs = []
        for hh in range(2):
            hm = (lane_q < HEAD_DIM) == (hh == 0)
            qh = jnp.where(hm, q2, jnp.zeros_like(q2))
            doh = jnp.where(hm, do2, jnp.zeros_like(do2))

            def block(j, c_g, dqt, masked, hh=hh, qh=qh, doh=doh):
                rows = pl.ds(pl.multiple_of(j * t, t), t)
                kb, vb = k_ref[rows, :], v_ref[rows, :]
                z = lax.dot_general(kb, qh, (((1,), (1,)), ((), ())), preferred_element_type=F32)
                lb, l1m = _log_sigmoid_parts(z)
                beta = jnp.exp(lb)
                if masked:
                    l1m = jnp.where(valid, l1m, 0.0)
                aft = jnp.dot(later, l1m.astype(BF16), preferred_element_type=F32)
                w = jnp.exp(lb + aft + car_ref[hh, pl.ds(j, 1), :])
                if masked:
                    w = jnp.where(valid, w, 0.0)
                dw = lax.dot_general(vb, doh, (((1,), (1,)), ((), ())), preferred_element_type=F32)
                g = dw * w
                g_before = jnp.dot(earlier, g.astype(BF16), preferred_element_type=F32) + c_g
                dz = g - beta * (g + g_before)
                if masked:
                    dz = jnp.where(valid, dz, 0.0)
                dzb, wb = dz.astype(BF16), w.astype(BF16)
                dk_ref[rows, :] += jnp.dot(dzb, qh, preferred_element_type=F32)
                dv_ref[rows, :] += jnp.dot(wb, doh, preferred_element_type=F32)
                dqt = dqt + jnp.dot(kt_ref[0, j], dzb, preferred_element_type=F32)
                return c_g + _colsum(g), dqt

            c_g, dqt = lax.fori_loop(0, i, lambda j, c: block(j, c[0], c[1], False),
                                     (jnp.zeros((1, t), F32), jnp.zeros((LANES, t), F32)))
            c_g, dqt = block(i, c_g, dqt, True)
            dqs.append(dqt)
        dq_ref[...] = jnp.where(orow < HEAD_DIM, dqs[0], dqs[1]).T

    tile = pl.BlockSpec((t, LANES), lambda hp, i: (i, hp))
    whole = pl.BlockSpec((s, LANES), lambda hp, i: (0, hp))
    return pl.pallas_call(
        body, name="attn_bwd", grid=(nhp, nb),
        in_specs=[tile, tile, whole, whole,
                  pl.BlockSpec((1, nb, LANES, t), lambda hp, i: (hp, 0, 0, 0)),
                  pl.BlockSpec((2, nb, t), lambda hp, i: (hp, 0, i))],
        out_specs=[tile, whole, whole],
        out_shape=[jax.ShapeDtypeStruct((s, nhp * LANES), F32)] * 3,
        compiler_params=_params(dimension_semantics=("arbitrary", "arbitrary")),
    )(qs, do, k, v, kt3, car)


SCAN_LANES = 2048
SCAN_ROWS = 8


def _scan_chunks(v):
    n = v.shape[1] // (2 * LANES)
    return [(v[:, c * 2 * LANES:c * 2 * LANES + LANES], v[:, c * 2 * LANES + LANES:(c + 1) * 2 * LANES]) for c in range(n)]


def _ssm_scan_fwd(bu, lam):
    s, w = bu.shape
    tt = _pick(s, (512, 256, 128, 8))
    nt = s // tt

    def body(x_ref, lam_ref, h_ref, st_ref):
        @pl.when(pl.program_id(1) == 0)
        def _():
            st_ref[...] = jnp.zeros_like(st_ref)

        lam_c = _scan_chunks(lam_ref[...])

        def tile(it, state):
            r0 = pl.multiple_of(it * SCAN_ROWS, SCAN_ROWS)
            x_c = _scan_chunks(x_ref[pl.ds(r0, SCAN_ROWS), :])
            state = list(state)
            out_rows = []
            for r in range(SCAN_ROWS):
                parts = []
                for c, ((lr, li), (xr, xi)) in enumerate(zip(lam_c, x_c)):
                    hr, hi = state[2 * c], state[2 * c + 1]
                    nhr = lr * hr - li * hi + xr[r:r + 1]
                    nhi = lr * hi + li * hr + xi[r:r + 1]
                    state[2 * c], state[2 * c + 1] = nhr, nhi
                    parts += [nhr, nhi]
                out_rows.append(jnp.concatenate(parts, axis=1))
            h_ref[pl.ds(r0, SCAN_ROWS), :] = jnp.concatenate(out_rows, axis=0)
            return tuple(state)

        st = st_ref[0:1, :]
        init = tuple(st[:, c * LANES:(c + 1) * LANES] for c in range(SCAN_LANES // LANES))
        fin = lax.fori_loop(0, tt // SCAN_ROWS, tile, init)
        st_ref[0:1, :] = jnp.concatenate(fin, axis=1)

    return pl.pallas_call(
        body, name="ssm_scan_fwd", grid=(w // SCAN_LANES, nt),
        in_specs=[pl.BlockSpec((tt, SCAN_LANES), lambda lc, i: (i, lc)),
                  pl.BlockSpec((1, SCAN_LANES), lambda lc, i: (0, lc))],
        out_specs=pl.BlockSpec((tt, SCAN_LANES), lambda lc, i: (i, lc)),
        out_shape=jax.ShapeDtypeStruct((s, w), F32),
        scratch_shapes=[pltpu.VMEM((SCAN_ROWS, SCAN_LANES), F32)],
        compiler_params=_params(dimension_semantics=("arbitrary", "arbitrary")),
    )(bu, lam)


def _ssm_scan_bwd(e, h, lam):
    s, w = e.shape
    tt = _pick(s, (512, 256, 128, 8))
    nt = s // tt

    def body(e_ref, h_ref, lam_ref, a_ref, dlam_ref, st_ref):
        @pl.when(pl.program_id(1) == 0)
        def _():
            st_ref[...] = jnp.zeros_like(st_ref)
            dlam_ref[...] = jnp.zeros_like(dlam_ref)

        lam_c = _scan_chunks(lam_ref[...])
        nch = len(lam_c)

        def tile(it, carry):
            r0 = pl.multiple_of((tt // SCAN_ROWS - 1 - it) * SCAN_ROWS, SCAN_ROWS)
            e_c = _scan_chunks(e_ref[pl.ds(r0, SCAN_ROWS), :])
            h_c = _scan_chunks(h_ref[pl.ds(r0, SCAN_ROWS), :])
            carry = list(carry)
            out_rows = [None] * SCAN_ROWS
            for r in reversed(range(SCAN_ROWS)):
                parts = []
                for c in range(nch):
                    (lr, li), (er, ei), (hr, hi) = lam_c[c], e_c[c], h_c[c]
                    ar, ai, dr, di = carry[4 * c:4 * c + 4]
                    hr, hi = hr[r:r + 1], hi[r:r + 1]
                    dr = dr + ar * hr + ai * hi
                    di = di + ai * hr - ar * hi
                    nar = lr * ar + li * ai + er[r:r + 1]
                    nai = lr * ai - li * ar + ei[r:r + 1]
                    carry[4 * c:4 * c + 4] = [nar, nai, dr, di]
                    parts += [nar, nai]
                out_rows[r] = jnp.concatenate(parts, axis=1)
            a_ref[pl.ds(r0, SCAN_ROWS), :] = jnp.concatenate(out_rows, axis=0)
            return tuple(carry)

        st, dl = st_ref[0:1, :], dlam_ref[0:1, :]
        init = []
        for c in range(nch):
            lo = c * 2 * LANES
            init += [st[:, lo:lo + LANES], st[:, lo + LANES:lo + 2 * LANES],
                     dl[:, lo:lo + LANES], dl[:, lo + LANES:lo + 2 * LANES]]
        fin = lax.fori_loop(0, tt // SCAN_ROWS, tile, tuple(init))
        st_ref[0:1, :] = jnp.concatenate([fin[4 * c + q] for c in range(nch) for q in (0, 1)], axis=1)
        dlam_ref[0:1, :] = jnp.concatenate([fin[4 * c + q] for c in range(nch) for q in (2, 3)], axis=1)

    rev = pl.BlockSpec((tt, SCAN_LANES), lambda lc, i: (nt - 1 - i, lc))
    vec = pl.BlockSpec((1, SCAN_LANES), lambda lc, i: (0, lc))
    return pl.pallas_call(
        body, name="ssm_scan_bwd", grid=(w // SCAN_LANES, nt),
        in_specs=[rev, rev, vec], out_specs=[rev, pl.BlockSpec((SCAN_ROWS, SCAN_LANES), lambda lc, i: (0, lc))],
        out_shape=[jax.ShapeDtypeStruct((s, w), F32), jax.ShapeDtypeStruct((SCAN_ROWS, w), F32)],
        scratch_shapes=[pltpu.VMEM((SCAN_ROWS, SCAN_LANES), F32)],
        compiler_params=_params(dimension_semantics=("arbitrary", "arbitrary")),
    )(e, h, lam)


def _ssm_params_fwd(a_re, a_im, log_dt, b_re, b_im):
    def body(ar_ref, ai_ref, ldt_ref, br_ref, bi_ref, lr_ref, li_ref, bbr_ref, bbi_ref):
        ar, ai, dt = ar_ref[...], ai_ref[...], jnp.exp(ldt_ref[...])
        mag = jnp.exp(ar * dt)
        lr, li = mag * jnp.cos(ai * dt), mag * jnp.sin(ai * dt)
        den = ar * ar + ai * ai
        cr = ((lr - 1.0) * ar + li * ai) / den
        ci = (li * ar - (lr - 1.0) * ai) / den
        br, bi = br_ref[...], bi_ref[...]
        lr_ref[...], li_ref[...] = lr, li
        bbr_ref[...] = cr * br - ci * bi
        bbi_ref[...] = cr * bi + ci * br

    n = a_re.shape[0]
    v1, v16 = jax.ShapeDtypeStruct((n, 1), F32), jax.ShapeDtypeStruct((n, SSM_GROUP), F32)
    return pl.pallas_call(body, name="ssm_params_fwd", out_shape=[v1, v1, v16, v16],
                          compiler_params=_params())(a_re, a_im, log_dt, b_re, b_im)


def _ssm_params_bwd(a_re, a_im, log_dt, b_re, b_im, g_lr, g_li, g_bbr, g_bbi):
    n = a_re.shape[0]

    def body(ar_ref, ai_ref, ldt_ref, br_ref, bi_ref, glr_ref, gli_ref, gbr_ref, gbi_ref,
             dar_ref, dai_ref, dldt_ref, dbr_ref, dbi_ref):
        ar, ai, dt = ar_ref[...], ai_ref[...], jnp.exp(ldt_ref[...])
        mag = jnp.exp(ar * dt)
        lr, li = mag * jnp.cos(ai * dt), mag * jnp.sin(ai * dt)
        den = ar * ar + ai * ai
        cr = ((lr - 1.0) * ar + li * ai) / den
        ci = (li * ar - (lr - 1.0) * ai) / den
        br, bi, gbr, gbi = br_ref[...], bi_ref[...], gbr_ref[...], gbi_ref[...]
        dbr_ref[...] = gbr * cr + gbi * ci
        dbi_ref[...] = gbi * cr - gbr * ci
        gcr = jnp.sum(gbr * br + gbi * bi, axis=1, keepdims=True)
        gci = jnp.sum(gbi * br - gbr * bi, axis=1, keepdims=True)
        ir, ii = ar / den, -ai / den
        glr = glr_ref[...] + gcr * ir + gci * ii
        gli = gli_ref[...] + gci * ir - gcr * ii
        qr, qi = cr * ir - ci * ii, cr * ii + ci * ir
        gar = -(gcr * qr + gci * qi)
        gai = -(gci * qr - gcr * qi)
        gxr = glr * lr + gli * li
        gxi = gli * lr - glr * li
        dar_ref[...] = gar + gxr * dt
        dai_ref[...] = gai + gxi * dt
        gdt = (gxr * ar + gxi * ai) * dt
        rowg = lax.broadcasted_iota(jnp.int32, (n, SSM_GROUPS), 0) // SSM_STATE
        colg = lax.broadcasted_iota(jnp.int32, (n, SSM_GROUPS), 1)
        dldt_ref[...] = jnp.sum(jnp.where(rowg == colg, gdt, 0.0), axis=0, keepdims=True)

    v1, v16 = jax.ShapeDtypeStruct((n, 1), F32), jax.ShapeDtypeStruct((n, SSM_GROUP), F32)
    return pl.pallas_call(body, name="ssm_params_bwd",
                          out_shape=[v1, v1, jax.ShapeDtypeStruct((1, SSM_GROUPS), F32), v16, v16],
                          compiler_params=_params())(a_re, a_im, log_dt, b_re, b_im, g_lr, g_li, g_bbr, g_bbi)


def _interleave(re, im, axis):
    shp = list(re.shape)
    new = shp[:axis] + [shp[axis] // LANES, LANES] + shp[axis + 1:]
    st = jnp.stack([re.reshape(new), im.reshape(new)], axis=axis + 1)
    return st.reshape(shp[:axis] + [2 * shp[axis]] + shp[axis + 1:])


def _deinterleave(v, axis):
    shp = list(v.shape)
    r = v.reshape(shp[:axis] + [shp[axis] // (2 * LANES), 2, LANES] + shp[axis + 1:])
    out = shp[:axis] + [shp[axis] // 2] + shp[axis + 1:]
    return (lax.index_in_dim(r, 0, axis + 1, keepdims=False).reshape(out),
            lax.index_in_dim(r, 1, axis + 1, keepdims=False).reshape(out))


def _b_matrix(bbr, bbi):
    eye = jnp.eye(SSM_GROUPS, dtype=F32)

    def blockdiag(v):
        x = v.reshape(SSM_GROUPS, SSM_STATE, SSM_GROUP).transpose(0, 2, 1)
        return (eye[:, None, :, None] * x[:, :, None, :]).reshape(SSM_GROUPS * SSM_GROUP, N_STATE)

    return _interleave(blockdiag(bbr), blockdiag(bbi), 1)


def _b_matrix_grad(d):
    def diag(v):
        x = v.reshape(SSM_GROUPS, SSM_GROUP, SSM_GROUPS, SSM_STATE)
        return jnp.diagonal(x, axis1=0, axis2=2).transpose(2, 1, 0).reshape(N_STATE, SSM_GROUP)

    dr, di = _deinterleave(d, 1)
    return diag(dr), diag(di)


def _c_matrix(c_re, c_im):
    eye = jnp.eye(SSM_GROUPS, dtype=F32)

    def blockdiag(v):
        x = v.transpose(0, 2, 1)
        return (x[:, :, None, :] * eye[:, None, :, None]).reshape(N_STATE, SSM_GROUPS * SSM_GROUP)

    return _interleave(blockdiag(c_re), blockdiag(-c_im), 0)


def _c_matrix_grad(d):
    def diag(v):
        x = v.reshape(SSM_GROUPS, SSM_STATE, SSM_GROUPS, SSM_GROUP)
        return jnp.diagonal(x, axis1=0, axis2=2).transpose(2, 1, 0)

    dr, di = _deinterleave(d, 0)
    return diag(dr), -diag(di)


def _row(v):
    return v.reshape(1, -1)


def _ssm_inputs(p):
    rows = lambda v: v.reshape(N_STATE, -1)
    ldt = jnp.repeat(p["ssm_log_dt"], SSM_STATE).reshape(N_STATE, 1)
    return rows(p["ssm_a_re"]), rows(p["ssm_a_im"]), ldt, rows(p["ssm_b_re"]), rows(p["ssm_b_im"])


def _layer_fwd(x, mod, p, tag):
    d = x.shape[1]
    sh_m, sc_m, g_m, sh_f, sc_f, g_f = [_row(mod[i]) for i in range(6)]
    nm = lambda s: f"{s}_{tag}"

    def lnmod(x, sc, sh):
        return _norm(x)[0] * (1.0 + sc) + sh

    h1 = _rowwise(nm("lnmod1"), lnmod, [(x, 0, d)], [sc_m, sh_m], [(d, BF16)])
    proj = _mm(nm("proj"), h1, p["w_in"], "nn")
    t = min(ATT_TILE, x.shape[0])
    qs, kb, vb, kt3, vt3 = _qkv_prep(proj, t)
    att, car = _attn_fwd(qs, kb, vt3, t)
    y_sb = _mm(nm("sb_up"), att, p["w_sb_up"], "nn")

    lam_r, lam_i, bbr, bbi = _ssm_params_fwd(*_ssm_inputs(p))
    lam = _interleave(lam_r.reshape(1, N_STATE), lam_i.reshape(1, N_STATE), 1)
    bmat = _b_matrix(bbr, bbi).astype(BF16)
    cmat = _c_matrix(p["ssm_c_re"], p["ssm_c_im"]).astype(BF16)
    bu = _mm(nm("ssm_b"), proj, bmat, "nn", a_cols=(1536, 512))
    hst = _ssm_scan_fwd(bu, lam)
    yc = _mm(nm("ssm_c"), hst, cmat, "nn")

    def ssm_act(yc, u, dsk):
        y0 = yc + dsk * u
        return y0, _gelu(y0)

    y0, y1 = _rowwise(nm("ssm_act"), ssm_act, [(yc, 0, 512), (proj, 3, 512)], [_row(p["ssm_d"])], [(512, F32), (512, F32)])
    gl = _mm(nm("glu"), y1, p["w_glu"], "nn")
    y2 = _rowwise(nm("glu_act"), lambda y1, gl, b: y1 * jax.nn.sigmoid(gl + b), [(y1, 0, 512), (gl, 0, 512)],
                  [_row(p["b_glu"])], [(512, BF16)])
    y_ssm = _mm(nm("ssm_up"), y2, p["w_ssm_up"], "nn")

    def merge(gsb, gss, ysb, yss):
        return jax.nn.sigmoid(gsb) * ysb + jax.nn.sigmoid(gss) * yss

    merged = _rowwise(nm("merge"), merge, [(proj, 2, d), (proj, 3, d), (y_sb, 0, d), (y_ssm, 0, d)], [], [(d, BF16)])
    y = _mm(nm("out"), merged, p["w_out"], "nn")

    def resid_ln(x, y, g, lg, lb):
        return _norm(ALPHA * x + (1.0 + g) * y)[0] * lg + lb

    x1 = _rowwise(nm("ln1"), resid_ln, [(x, 0, d), (y, 0, d)], [g_m, _row(p["ln1_g"]), _row(p["ln1_b"])], [(d, F32)])
    h2 = _rowwise(nm("lnmod2"), lnmod, [(x1, 0, d)], [sc_f, sh_f], [(d, BF16)])
    f = _mm(nm("ffn_in"), h2, p["w_ffn_in"], "nn")
    fh = f.shape[1] // 2
    act = _rowwise(nm("swiglu"), lambda g, u: g * jax.nn.sigmoid(g) * u, [(f, 0, fh), (f, 1, fh)], [], [(fh, BF16)])
    yf = _mm(nm("ffn_out"), act, p["w_ffn_out"], "nn")
    x2 = _rowwise(nm("ln2"), resid_ln, [(x1, 0, d), (yf, 0, d)], [g_f, _row(p["ln2_g"]), _row(p["ln2_b"])], [(d, F32)])
    saved = dict(x=x, h1=h1, proj=proj, qs=qs, kb=kb, vb=vb, kt3=kt3, car=car, att=att, y_sb=y_sb, lam=lam, bmat=bmat,
                 cmat=cmat, hst=hst, y0=y0, y1=y1, gl=gl, y2=y2, y_ssm=y_ssm, merged=merged, y=y, x1=x1, h2=h2, f=f,
                 act=act, yf=yf, t=t)
    return x2, saved


def _layer_bwd(dx2, mod, p, sv, tag):
    d = dx2.shape[1]
    sh_m, sc_m, g_m, sh_f, sc_f, g_f = [_row(mod[i]) for i in range(6)]
    nm = lambda s: f"{s}_{tag}"
    grads = {}

    def resid_ln_bwd(x, y, dxo, g, lg):
        n, rstd = _norm(ALPHA * x + (1.0 + g) * y)
        dr = _norm_bwd(dxo * lg, n, rstd)
        return ALPHA * dr, (1.0 + g) * dr, _colsum(dxo * n), _colsum(dxo), _colsum(dr * y)

    def lnmod_bwd(x, dh, dxa, sc):
        n, rstd = _norm(x)
        return dxa + _norm_bwd(dh * (1.0 + sc), n, rstd), _colsum(dh * n), _colsum(dh)

    dx1a, dyf, grads["ln2_g"], grads["ln2_b"], dg_f = _rowwise(
        nm("ln2_bwd"), resid_ln_bwd, [(sv["x1"], 0, d), (sv["yf"], 0, d), (dx2, 0, d)], [g_f, _row(p["ln2_g"])],
        [(d, F32), (d, BF16)], [d, d, d])
    dact = _mm(nm("d_act"), dyf, p["w_ffn_out"], "nt")
    grads["w_ffn_out"] = _mm(nm("dw_ffn_out"), sv["act"], dyf, "tn")
    fh = sv["f"].shape[1] // 2

    def swiglu_bwd(g, u, da):
        sg = jax.nn.sigmoid(g)
        return da * u * sg * (1.0 + g * (1.0 - sg)), da * g * sg

    dgate, dup = _rowwise(nm("swiglu_bwd"), swiglu_bwd, [(sv["f"], 0, fh), (sv["f"], 1, fh), (dact, 0, fh)], [],
                          [(fh, BF16), (fh, BF16)])
    df = jnp.concatenate([dgate, dup], axis=1)
    dh2 = _mm(nm("d_h2"), df, p["w_ffn_in"], "nt")
    grads["w_ffn_in"] = _mm(nm("dw_ffn_in"), sv["h2"], df, "tn")
    dx1, dsc_f, dsh_f = _rowwise(nm("lnmod2_bwd"), lnmod_bwd, [(sv["x1"], 0, d), (dh2, 0, d), (dx1a, 0, d)], [sc_f],
                                 [(d, F32)], [d, d])
    dxa, dy, grads["ln1_g"], grads["ln1_b"], dg_m = _rowwise(
        nm("ln1_bwd"), resid_ln_bwd, [(sv["x"], 0, d), (sv["y"], 0, d), (dx1, 0, d)], [g_m, _row(p["ln1_g"])],
        [(d, F32), (d, BF16)], [d, d, d])
    dmerged = _mm(nm("d_merged"), dy, p["w_out"], "nt")
    grads["w_out"] = _mm(nm("dw_out"), sv["merged"], dy, "tn")

    def merge_bwd(gsb, gss, ysb, yss, dm):
        s1, s2 = jax.nn.sigmoid(gsb), jax.nn.sigmoid(gss)
        return s1 * dm, s2 * dm, dm * ysb * s1 * (1.0 - s1), dm * yss * s2 * (1.0 - s2)

    dy_sb, dy_ssm, dg_sb, dg_ssm = _rowwise(
        nm("merge_bwd"), merge_bwd, [(sv["proj"], 2, d), (sv["proj"], 3, d), (sv["y_sb"], 0, d), (sv["y_ssm"], 0, d),
                                     (dmerged, 0, d)], [], [(d, BF16)] * 4)
    dy2 = _mm(nm("d_y2"), dy_ssm, p["w_ssm_up"], "nt")
    grads["w_ssm_up"] = _mm(nm("dw_ssm_up"), sv["y2"], dy_ssm, "tn")

    def glu_act_bwd(y1, gl, dy2, b):
        sg = jax.nn.sigmoid(gl + b)
        dgl = dy2 * y1 * sg * (1.0 - sg)
        return dy2 * sg, dgl, _colsum(dgl)

    dy1a, dgl, grads["b_glu"] = _rowwise(nm("glu_act_bwd"), glu_act_bwd, [(sv["y1"], 0, 512), (sv["gl"], 0, 512), (dy2, 0, 512)],
                                         [_row(p["b_glu"])], [(512, F32), (512, BF16)], [512])
    dy1b = _mm(nm("d_y1"), dgl, p["w_glu"], "nt")
    grads["w_glu"] = _mm(nm("dw_glu"), sv["y1"], dgl, "tn")

    def ssm_act_bwd(y0, u, dy1a, dy1b, dsk):
        dy0 = (dy1a + dy1b) * _gelu_grad(y0)
        return dy0, dsk * dy0, _colsum(dy0 * u)

    dy0, du_a, grads["ssm_d"] = _rowwise(nm("ssm_act_bwd"), ssm_act_bwd,
                                         [(sv["y0"], 0, 512), (sv["proj"], 3, 512), (dy1a, 0, 512), (dy1b, 0, 512)],
                                         [_row(p["ssm_d"])], [(512, BF16), (512, F32)], [512])
    e = _mm(nm("ssm_e"), dy0, sv["cmat"], "nt")
    grads["ssm_c_re"], grads["ssm_c_im"] = _c_matrix_grad(_mm(nm("dw_ssm_c"), sv["hst"], dy0, "tn"))
    adj, dlam = _ssm_scan_bwd(e, sv["hst"], sv["lam"])
    du_b = _mm(nm("d_u"), adj, sv["bmat"], "nt")
    g_bbr, g_bbi = _b_matrix_grad(_mm(nm("dw_ssm_b"), sv["proj"], adj, "tn", a_cols=(1536, 512)))
    g_lr, g_li = _deinterleave(dlam[0:1], 1)
    da_re, da_im, dldt, db_re, db_im = _ssm_params_bwd(*_ssm_inputs(p), g_lr.reshape(N_STATE, 1), g_li.reshape(N_STATE, 1),
                                                       g_bbr, g_bbi)
    grads["ssm_a_re"] = da_re.reshape(SSM_GROUPS, SSM_STATE)
    grads["ssm_a_im"] = da_im.reshape(SSM_GROUPS, SSM_STATE)
    grads["ssm_log_dt"] = dldt.reshape(SSM_GROUPS)
    grads["ssm_b_re"] = db_re.reshape(SSM_GROUPS, SSM_STATE, SSM_GROUP)
    grads["ssm_b_im"] = db_im.reshape(SSM_GROUPS, SSM_STATE, SSM_GROUP)
    datt = _mm(nm("d_att"), dy_sb, p["w_sb_up"], "nt", out_dtype=BF16)
    grads["w_sb_up"] = _mm(nm("dw_sb_up"), sv["att"], dy_sb, "tn")
    dqs, dk, dv = _attn_bwd(sv["qs"], datt, sv["kb"], sv["vb"], sv["kt3"], sv["car"], sv["t"])

    def dproj_qkvu(dqs, dk, dv, dua, dub):
        return dqs * (1.0 / math.sqrt(HEAD_DIM)), dk, dv, dua + dub

    dq, dkb, dvb, du = _rowwise(nm("dproj"), dproj_qkvu, [(dqs, 0, 512), (dk, 0, 512), (dv, 0, 512), (du_a, 0, 512), (du_b, 0, 512)],
                                [], [(512, BF16)] * 4)
    dproj = jnp.concatenate([dq, dkb, dvb, du, dg_sb, dg_ssm], axis=1)
    dh1 = _mm(nm("d_h1"), dproj, p["w_in"], "nt")
    grads["w_in"] = _mm(nm("dw_in"), sv["h1"], dproj, "tn")
    dx, dsc_m, dsh_m = _rowwise(nm("lnmod1_bwd"), lnmod_bwd, [(sv["x"], 0, d), (dh1, 0, d), (dxa, 0, d)], [sc_m],
                                [(d, F32)], [d, d])
    for k in ("ln1_g", "ln1_b", "ln2_g", "ln2_b", "ssm_d", "b_glu"):
        grads[k] = grads[k].reshape(-1)
    dmod = jnp.concatenate([dsh_m, dsc_m, dg_m, dsh_f, dsc_f, dg_f], axis=0)
    return dx, dmod, grads


def _loss_head(x, target):
    d = x.shape[1]

    def fn(x, tgt):
        err = x - tgt
        return err * (1.0 / d), _colsum(err * err) * (0.5 / d)

    return _rowwise("loss_head", fn, [(x, 0, d), (target, 0, d)], [], [(d, F32)], [d])


def _place():
    return lax.axis_index("x"), lax.axis_index("y"), lax.axis_index("c")


def _all_gather8(name, block):
    m_per, n = block.shape

    def body(x_ref, out_ref, send_sems, recv_sems, local_sem):
        x, y, c = _place()
        me, sibling = (x, y, c), (x, y, 1 - c)
        chips = [(1 - x, y), (x, 1 - y), (1 - x, 1 - y)]

        def rows(px, py, pc):
            return out_ref.at[pl.ds(pl.multiple_of((4 * px + 2 * py + pc) * m_per, 8), m_per), :]

        def copy(k, blk, to, src=None):
            return pltpu.make_async_remote_copy(src_ref=rows(*blk) if src is None else src, dst_ref=rows(*blk),
                                                send_sem=send_sems.at[k], recv_sem=recv_sems.at[k],
                                                device_id=to, device_id_type=MESH)

        mine = pltpu.make_async_copy(x_ref, rows(*me), local_sem)
        mine.start()
        first = [copy(0, me, sibling, src=x_ref)] + [copy(1 + j, me, (*chip, c), src=x_ref) for j, chip in enumerate(chips)]
        for cp in first:
            cp.start()
        passed = [copy(4 + j, (*chip, c), sibling) for j, chip in enumerate(chips)]
        for j, chip in enumerate(chips):
            copy(1 + j, (*chip, c), me).wait_recv()
            passed[j].start()
        copy(0, sibling, me).wait_recv()
        for j, chip in enumerate(chips):
            copy(4 + j, (*chip, 1 - c), me).wait_recv()
        for cp in first + passed:
            cp.wait_send()
        mine.wait()

    return pl.pallas_call(
        body, name=name, out_shape=jax.ShapeDtypeStruct((8 * m_per, n), block.dtype),
        in_specs=[pl.BlockSpec(memory_space=pltpu.VMEM)], out_specs=pl.BlockSpec(memory_space=pltpu.VMEM),
        scratch_shapes=[pltpu.SemaphoreType.DMA((7,)), pltpu.SemaphoreType.DMA((7,)), pltpu.SemaphoreType.DMA],
        compiler_params=_params(),
    )(block)


def _other_chips(x, y):
    return [(1 - x, y), (x, 1 - y), (1 - x, 1 - y)]


def _gather_weights(shards):
    n = len(shards)

    def body(*refs):
        src, dst = refs[:n], refs[n:2 * n]
        send_sems, recv_sems, local_sems = refs[2 * n:]
        x, y, c = _place()
        mine = 2 * x + y
        local = [pltpu.make_async_copy(src[k], dst[k].at[mine], local_sems.at[k]) for k in range(n)]
        for cp in local:
            cp.start()
        sends = []
        for k in range(n):
            for j, (px, py) in enumerate(_other_chips(x, y)):
                cp = pltpu.make_async_remote_copy(src_ref=src[k], dst_ref=dst[k].at[mine], send_sem=send_sems.at[k, j],
                                                  recv_sem=recv_sems.at[k, j], device_id=(px, py, c), device_id_type=MESH)
                cp.start()
                sends.append(cp)
        for k in range(n):
            for j, (px, py) in enumerate(_other_chips(x, y)):
                pltpu.make_async_remote_copy(src_ref=src[k], dst_ref=dst[k].at[2 * px + py], send_sem=send_sems.at[k, j],
                                             recv_sem=recv_sems.at[k, j], device_id=(px, py, c),
                                             device_id_type=MESH).wait_recv()
        for cp in sends:
            cp.wait_send()
        for cp in local:
            cp.wait()

    any_spec = pl.BlockSpec(memory_space=pl.ANY)
    return pl.pallas_call(
        body, name="gather_weights", in_specs=[any_spec] * n, out_specs=[any_spec] * n,
        out_shape=[jax.ShapeDtypeStruct((4, *s.shape), s.dtype) for s in shards],
        scratch_shapes=[pltpu.SemaphoreType.DMA((n, 3)), pltpu.SemaphoreType.DMA((n, 3)), pltpu.SemaphoreType.DMA((n,))],
        compiler_params=_params(),
    )(*shards)


def _peer(x, y, c, r):
    fx, fy, fc = (r >> 2) & 1, (r >> 1) & 1, r & 1
    return (x + fx - 2 * x * fx, y + fy - 2 * y * fy, c + fc - 2 * c * fc)


def _scatter_grads(grads, by_rows):
    n = len(grads)

    def half_shape(k):
        l, r, c = grads[k].shape
        return (l, r // 8, c) if by_rows[k] else (l, r // 2, c // 4)

    def body(*refs):
        src, dst = refs[:n], refs[n:2 * n]
        send_sems, recv_sems, local_sems = refs[2 * n:]
        x, y, c = _place()
        me = 4 * x + 2 * y + c

        def window(k, px, py, pc):
            _, hr, hc = half_shape(k)
            q = 2 * px + py
            if by_rows[k]:
                return src[k].at[:, pl.ds(pl.multiple_of((2 * q + pc) * hr, 8), hr), :]
            return src[k].at[:, pl.ds(pl.multiple_of(pc * hr, 8), hr), pl.ds(pl.multiple_of(q * hc, LANES), hc)]

        local = [pltpu.make_async_copy(window(k, x, y, c), dst[k].at[me], local_sems.at[k]) for k in range(n)]
        for cp in local:
            cp.start()
        sends = []
        for k in range(n):
            for r in range(1, 8):
                to = _peer(x, y, c, r)
                cp = pltpu.make_async_remote_copy(src_ref=window(k, *to), dst_ref=dst[k].at[me], send_sem=send_sems.at[k, r - 1],
                                                  recv_sem=recv_sems.at[k, r - 1], device_id=to, device_id_type=MESH)
                cp.start()
                sends.append(cp)
        for k in range(n):
            for r in range(1, 8):
                px, py, pc = _peer(x, y, c, r)
                pltpu.make_async_remote_copy(src_ref=window(k, x, y, c), dst_ref=dst[k].at[4 * px + 2 * py + pc],
                                             send_sem=send_sems.at[k, r - 1], recv_sem=recv_sems.at[k, r - 1],
                                             device_id=(px, py, pc), device_id_type=MESH).wait_recv()
        for cp in sends:
            cp.wait_send()
        for cp in local:
            cp.wait()

    any_spec = pl.BlockSpec(memory_space=pl.ANY)
    return pl.pallas_call(
        body, name="scatter_grads", in_specs=[any_spec] * n, out_specs=[any_spec] * n,
        out_shape=[jax.ShapeDtypeStruct((8, *half_shape(k)), F32) for k in range(n)],
        scratch_shapes=[pltpu.SemaphoreType.DMA((n, 7)), pltpu.SemaphoreType.DMA((n, 7)), pltpu.SemaphoreType.DMA((n,))],
        compiler_params=_params(),
    )(*grads)


def _sum_slots(name, parts):
    _, l, r, c = parts.shape
    tr = _pick(r, (256, 176, 128, 64, 32, 8))

    def body(p_ref, o_ref):
        acc = p_ref[0]
        for i in range(1, 8):
            acc = acc + p_ref[i]
        o_ref[...] = acc

    return pl.pallas_call(
        body, name=name, grid=(l, r // tr), in_specs=[pl.BlockSpec((8, 1, tr, c), lambda li, i: (0, li, i, 0))],
        out_specs=pl.BlockSpec((1, tr, c), lambda li, i: (li, i, 0)), out_shape=jax.ShapeDtypeStruct((l, r, c), F32),
        compiler_params=_params(dimension_semantics=("arbitrary", "arbitrary")),
    )(parts)


def _swap_halves(halves):
    n = len(halves)

    def body(*refs):
        src, dst = refs[:n], refs[n:2 * n]
        send_sems, recv_sems, local_sems = refs[2 * n:]
        x, y, c = _place()

        def place(k, pc):
            r = halves[k].shape[1]
            return dst[k].at[:, pl.ds(pl.multiple_of(pc * r, 8), r), :]

        local = [pltpu.make_async_copy(src[k], place(k, c), local_sems.at[k]) for k in range(n)]
        sends = [pltpu.make_async_remote_copy(src_ref=src[k], dst_ref=place(k, c), send_sem=send_sems.at[k],
                                              recv_sem=recv_sems.at[k], device_id=(x, y, 1 - c), device_id_type=MESH)
                 for k in range(n)]
        for cp in local + sends:
            cp.start()
        for k in range(n):
            pltpu.make_async_remote_copy(src_ref=src[k], dst_ref=place(k, 1 - c), send_sem=send_sems.at[k],
                                         recv_sem=recv_sems.at[k], device_id=(x, y, 1 - c), device_id_type=MESH).wait_recv()
        for cp in sends:
            cp.wait_send()
        for cp in local:
            cp.wait()

    any_spec = pl.BlockSpec(memory_space=pl.ANY)
    return pl.pallas_call(
        body, name="swap_halves", in_specs=[any_spec] * n, out_specs=[any_spec] * n,
        out_shape=[jax.ShapeDtypeStruct((h.shape[0], 2 * h.shape[1], h.shape[2]), F32) for h in halves],
        scratch_shapes=[pltpu.SemaphoreType.DMA((n,)), pltpu.SemaphoreType.DMA((n,)), pltpu.SemaphoreType.DMA((n,))],
        compiler_params=_params(),
    )(*halves)


def _adamw(name, w, g, m, v):
    shape = w.shape
    cols = shape[-1] if w.ndim > 1 and shape[-1] % LANES == 0 else w.size if w.size % LANES else LANES
    flat = lambda a: a.reshape(-1, cols)
    rows = w.size // cols
    tr = _pick(rows, [r for r in (512, 256, 128, 64, 32, 16, 8) if r * cols <= 256 * 1024]) if rows % 8 == 0 else rows

    def body(w_ref, g_ref, m_ref, v_ref, d_ref, nm_ref, nv_ref):
        gg = g_ref[...]
        nm = ADAM_B1 * m_ref[...] + (1.0 - ADAM_B1) * gg
        nv = ADAM_B2 * v_ref[...] + (1.0 - ADAM_B2) * (gg * gg)
        m_hat = nm / (1.0 - ADAM_B1 ** ADAM_STEP)
        v_hat = nv / (1.0 - ADAM_B2 ** ADAM_STEP)
        d_ref[...] = -ADAM_LR * (m_hat / (jnp.sqrt(v_hat) + ADAM_EPS) + ADAM_WD * w_ref[...])
        nm_ref[...] = nm
        nv_ref[...] = nv

    spec = pl.BlockSpec((tr, cols), lambda i: (i, 0))
    out = pl.pallas_call(
        body, name=name, grid=(rows // tr,), in_specs=[spec] * 4, out_specs=[spec] * 3,
        out_shape=[jax.ShapeDtypeStruct((rows, cols), F32)] * 3,
        compiler_params=_params(dimension_semantics=("arbitrary",)),
    )(flat(w), flat(g), flat(m), flat(v))
    return tuple(o.reshape(shape) for o in out)


WEIGHTS = ["w_ada", "b_ada", "w_in", "w_sb_up", "ssm_a_re", "ssm_a_im", "ssm_log_dt", "ssm_b_re", "ssm_b_im", "ssm_c_re",
           "ssm_c_im", "ssm_d", "w_glu", "b_glu", "w_ssm_up", "w_out", "ln1_g", "ln1_b", "w_ffn_in", "w_ffn_out", "ln2_g",
           "ln2_b"]
COL_SPLIT = ["w_in", "w_sb_up", "w_ssm_up", "w_ffn_in"]
ROW_SPLIT = ["w_glu", "w_out", "w_ffn_out"]
SMALL = ["ssm_a_re", "ssm_a_im", "ssm_log_dt", "ssm_b_re", "ssm_b_im", "ssm_c_re", "ssm_c_im", "ssm_d", "b_glu", "ln1_g",
         "ln1_b", "ln2_g", "ln2_b"]
SLAB_COLS = 1024


def _cast_bf16(name, w):
    shape = w.shape
    flat = w.reshape(-1, shape[-1])
    rows, cols = flat.shape
    tr = _pick(rows, (512, 256, 128, 64, 8))

    def body(w_ref, o_ref):
        o_ref[...] = w_ref[...].astype(BF16)

    spec = pl.BlockSpec((tr, cols), lambda i: (i, 0))
    return pl.pallas_call(body, name=name, grid=(rows // tr,), in_specs=[spec], out_specs=spec,
                          out_shape=jax.ShapeDtypeStruct((rows, cols), BF16),
                          compiler_params=_params(dimension_semantics=("arbitrary",)))(flat).reshape(shape)


def _silu_rows(name, c):
    def body(c_ref, o_ref):
        v = c_ref[...]
        o_ref[...] = v * jax.nn.sigmoid(v)

    return pl.pallas_call(body, name=name, out_shape=jax.ShapeDtypeStruct(c.shape, F32), compiler_params=_params())(c)


def _pad_rows(v, mult=8):
    flat = v.reshape(-1)
    per = mult * SLAB_COLS
    total = -(-flat.size // per) * per
    return jnp.pad(flat, (0, total - flat.size)).reshape(-1, SLAB_COLS)


def kernel(x, c, w_ada, b_ada, w_in, w_sb_up, ssm_a_re, ssm_a_im, ssm_log_dt, ssm_b_re, ssm_b_im, ssm_c_re, ssm_c_im, ssm_d, w_glu, b_glu, w_ssm_up, w_out, ln1_g, ln1_b, w_ffn_in, w_ffn_out, ln2_g, ln2_b, loss_target, m_w_ada, m_b_ada, m_w_in, m_w_sb_up, m_ssm_a_re, m_ssm_a_im, m_ssm_log_dt, m_ssm_b_re, m_ssm_b_im, m_ssm_c_re, m_ssm_c_im, m_ssm_d, m_w_glu, m_b_glu, m_w_ssm_up, m_w_out, m_ln1_g, m_ln1_b, m_w_ffn_in, m_w_ffn_out, m_ln2_g, m_ln2_b, v_w_ada, v_b_ada, v_w_in, v_w_sb_up, v_ssm_a_re, v_ssm_a_im, v_ssm_log_dt, v_ssm_b_re, v_ssm_b_im, v_ssm_c_re, v_ssm_c_im, v_ssm_d, v_w_glu, v_b_glu, v_w_ssm_up, v_w_out, v_ln1_g, v_ln1_b, v_w_ffn_in, v_w_ffn_out, v_ln2_g, v_ln2_b):
    args = dict(locals())
    w = {n: args[n] for n in WEIGHTS}
    mom = {n: args["m_" + n] for n in WEIGHTS}
    var = {n: args["v_" + n] for n in WEIGHTS}
    depth, d = w_ada.shape[0], x.shape[-1]
    xi, yi, ci = _place()
    me, chip = 4 * xi + 2 * yi + ci, 2 * xi + yi
    ada_cols = w_ada.shape[2]

    big = COL_SPLIT + ROW_SPLIT
    gathered = _gather_weights([_cast_bf16(f"cast_{n}", w[n]) for n in big])
    full = {}
    for n, g in zip(big, gathered):
        if n in COL_SPLIT:
            full[n] = g.transpose(1, 2, 0, 3).reshape(depth, g.shape[2], 4 * g.shape[3])
        else:
            full[n] = g.transpose(1, 0, 2, 3).reshape(depth, 4 * g.shape[2], g.shape[3])

    c_all = _all_gather8("gather_c", jnp.pad(c, ((0, 7), (0, 0))))[::8]
    c_act = _silu_rows("silu_c", c_all)
    b_cols = lax.dynamic_slice_in_dim(b_ada, chip * ada_cols, ada_cols, axis=1)
    mod_part = jnp.concatenate([_small_mm(f"mod_{l}", c_act, w_ada[l], "nn") + b_cols[l][None] for l in range(depth)], axis=0)
    mod_all = _all_gather8("gather_mod", mod_part).reshape(4, 2, depth, 8, ada_cols)[:, 0]
    mod_mine = lax.dynamic_index_in_dim(mod_all, me, axis=2, keepdims=False)
    mod = mod_mine.transpose(1, 0, 2).reshape(depth, 6, d)

    layer_w = [{**{n: full[n][l] for n in big}, **{n: w[n][l] for n in SMALL}} for l in range(depth)]
    h, saved = x[0], []
    for l in range(depth):
        h, sv = _layer_fwd(h, mod[l], layer_w[l], str(l))
        saved.append(sv)
    dh, loss_cols = _loss_head(h, loss_target[0])
    loss = lax.psum(jnp.sum(loss_cols), ("x", "y", "c"))
    dmods, lgrads = [None] * depth, [None] * depth
    for l in reversed(range(depth)):
        dh, dmods[l], lgrads[l] = _layer_bwd(dh, mod[l], layer_w[l], saved[l], str(l))
    grad_x = dh[None]

    parts = _scatter_grads([jnp.stack([lgrads[l][n] for l in range(depth)]) for n in big], [n in ROW_SPLIT for n in big])
    halves = [_sum_slots(f"sum_{n}", p) for n, p in zip(big, parts)]
    grad = dict(zip(big, _swap_halves(halves)))

    pieces = [jnp.stack(dmods)] + [jnp.stack([lgrads[l][n] for l in range(depth)]) for n in SMALL]
    slab = jnp.concatenate([_pad_rows(p) for p in pieces], axis=0)
    slabs = _all_gather8("gather_small", slab).reshape(8, 1, *slab.shape)
    total = _sum_slots("sum_small", slabs)[0]
    row = _pad_rows(pieces[0]).shape[0]
    for n, p in zip(SMALL, pieces[1:]):
        rows = _pad_rows(p).shape[0]
        grad[n] = total[row:row + rows].reshape(-1)[:p.size].reshape(p.shape)
        row += rows
    dmod_rows = _pad_rows(pieces[0]).shape[0]
    dmod_all = slabs[:, 0, :dmod_rows].reshape(8, -1)[:, :depth * 6 * d].reshape(8, depth, 4, ada_cols)
    dmod_cols = lax.dynamic_index_in_dim(dmod_all, chip, axis=2, keepdims=False)
    grad["w_ada"] = jnp.stack([_small_mm(f"dw_ada_{l}", c_act, dmod_cols[:, l], "tn") for l in range(depth)])
    dmod_sum = _sum_slots("sum_dmod", slabs[:, :, :dmod_rows])[0]
    grad["b_ada"] = dmod_sum.reshape(-1)[:depth * 6 * d].reshape(depth, 6 * d)

    delta, new_m, new_v = {}, {}, {}
    for n in WEIGHTS:
        delta[n], new_m[n], new_v[n] = _adamw(f"adamw_{n}", w[n], grad[n], mom[n], var[n])
    return (loss, grad_x, *[grad[n] for n in WEIGHTS], *[delta[n] for n in WEIGHTS], *[new_m[n] for n in WEIGHTS],
            *[new_v[n] for n in WEIGHTS])
```

```python
import functools
import math

import jax
import jax.numpy as jnp
from jax import lax
from jax.experimental import pallas as pl
from jax.experimental.pallas import tpu as pltpu

F32 = jnp.float32
BF16 = jnp.bfloat16
MESH = pl.DeviceIdType.MESH

LANES = 128
HEAD_DIM = 64
ATT_TILE = 256
SSM_GROUPS, SSM_STATE, SSM_GROUP = 32, 64, 16
N_STATE = SSM_GROUPS * SSM_STATE
LN_EPS = 1e-5
DEPTH = 2
ALPHA = (2 * DEPTH) ** 0.25
ADAM_LR, ADAM_B1, ADAM_B2, ADAM_EPS, ADAM_WD, ADAM_STEP = 0.001, 0.9, 0.999, 1e-08, 0.01, 10
VMEM_LIMIT = 56 * 1024 * 1024
GELU_K = math.sqrt(2.0 / math.pi)
GELU_C = 0.044715


def _params(**kw):
    return pltpu.CompilerParams(vmem_limit_bytes=VMEM_LIMIT, **kw)


def _pick(n, prefs):
    for p in prefs:
        if n % p == 0:
            return p
    return n


def _rowwise(name, fn, rows, vecs, outs, sums=(), tm=None):
    s = rows[0][0].shape[0]
    tm = tm or _pick(s, (256, 128, 64, 8))
    nin, no, ns = len(rows) + len(vecs), len(outs), len(sums)

    def body(*refs):
        res = fn(*[r[...] for r in refs[:nin]])
        res = res if isinstance(res, tuple) else (res,)
        for r, v in zip(refs[nin:nin + no], res[:no]):
            r[...] = v.astype(r.dtype)
        if ns:
            @pl.when(pl.program_id(0) == 0)
            def _():
                for r in refs[nin + no:]:
                    r[...] = jnp.zeros_like(r)
            for r, v in zip(refs[nin + no:], res[no:]):
                r[...] += v

    in_specs = [pl.BlockSpec((tm, w), lambda i, cb=cb: (i, cb)) for _, cb, w in rows]
    in_specs += [pl.BlockSpec(v.shape, lambda i: (0, 0)) for v in vecs]
    out_specs = [pl.BlockSpec((tm, w), lambda i: (i, 0)) for w, _ in outs]
    out_specs += [pl.BlockSpec((1, w), lambda i: (0, 0)) for w in sums]
    out_shape = [jax.ShapeDtypeStruct((s, w), dt) for w, dt in outs]
    out_shape += [jax.ShapeDtypeStruct((1, w), F32) for w in sums]
    res = pl.pallas_call(
        body, name=name, grid=(s // tm,), in_specs=in_specs, out_specs=out_specs, out_shape=out_shape,
        compiler_params=_params(dimension_semantics=("arbitrary",)),
    )(*[a for a, _, _ in rows], *vecs)
    return res[0] if len(res) == 1 else tuple(res)


MM_TILES = (1408, 1024, 512, 256, 128)


def _mm(name, a, b, mode, out_dtype=F32, a_cols=None, into=None):
    a_off, a_w = a_cols if a_cols else (0, a.shape[1])
    if mode == "nn":
        m, k, n = a.shape[0], a_w, b.shape[1]
    elif mode == "nt":
        m, k, n = a.shape[0], a_w, b.shape[0]
    else:
        k, m, n = a.shape[0], a_w, b.shape[1]
    tm = _pick(m, MM_TILES if mode == "tn" else MM_TILES[1:])
    tn = _pick(n, MM_TILES)
    tk = _pick(k, MM_TILES)
    nk = k // tk
    dims = {"nn": ((1,), (0,)), "nt": ((1,), (1,)), "tn": ((0,), (0,))}[mode]

    def body(a_ref, b_ref, *rest):
        o_ref = rest[-2] if nk > 1 else rest[-1]
        prod = lax.dot_general(a_ref[...].astype(BF16), b_ref[...].astype(BF16), (dims, ((), ())),
                               preferred_element_type=F32)
        if nk == 1:
            o_ref[...] = prod.astype(o_ref.dtype)
            return
        acc_ref = rest[-1]
        kk = pl.program_id(2)

        @pl.when(kk == 0)
        def _():
            acc_ref[...] = prod

        @pl.when(kk > 0)
        def _():
            acc_ref[...] += prod

        @pl.when(kk == nk - 1)
        def _():
            o_ref[...] = acc_ref[...].astype(o_ref.dtype)

    if mode == "tn":
        a_spec = pl.BlockSpec((tk, tm), lambda i, j, kk: (kk, a_off // tm + i))
    else:
        a_spec = pl.BlockSpec((tm, tk), lambda i, j, kk: (i, a_off // tk + kk))
    if mode == "nt":
        b_spec = pl.BlockSpec((tn, tk), lambda i, j, kk: (j, kk))
    else:
        b_spec = pl.BlockSpec((tk, tn), lambda i, j, kk: (kk, j))
    in_specs, operands, aliases = [a_spec, b_spec], [a, b], {}
    if into is None:
        out_spec = pl.BlockSpec((tm, tn), lambda i, j, kk: (i, j))
        out_shape = jax.ShapeDtypeStruct((m, n), out_dtype)
    else:
        buf, slab, count = into
        out_spec = pl.BlockSpec((None, tm, tn), lambda i, j, kk: (slab, i, j))
        out_shape = jax.ShapeDtypeStruct((count, m, n), out_dtype)
        if buf is not None:
            in_specs.append(pl.BlockSpec(memory_space=pl.ANY))
            operands.append(buf)
            aliases = {2: 0}
    return pl.pallas_call(
        body, name=name, grid=(m // tm, n // tn, nk), in_specs=in_specs, out_specs=out_spec, out_shape=out_shape,
        scratch_shapes=[pltpu.VMEM((tm, tn), F32)] if nk > 1 else [], input_output_aliases=aliases,
        compiler_params=_params(dimension_semantics=("arbitrary", "arbitrary", "arbitrary")),
    )(*operands)


def _small_mm(name, a, b, mode):
    dims = {"nn": ((1,), (0,)), "tn": ((0,), (0,))}[mode]
    m = a.shape[0] if mode == "nn" else a.shape[1]

    def body(a_ref, b_ref, o_ref):
        o_ref[...] = lax.dot_general(a_ref[...], b_ref[...], (dims, ((), ())), precision=lax.Precision.HIGHEST,
                                     preferred_element_type=F32)

    return pl.pallas_call(body, name=name, out_shape=jax.ShapeDtypeStruct((m, b.shape[1]), F32),
                          compiler_params=_params())(a, b)


def _norm(x):
    mu = jnp.mean(x, axis=-1, keepdims=True)
    xc = x - mu
    rstd = lax.rsqrt(jnp.mean(xc * xc, axis=-1, keepdims=True) + LN_EPS)
    return xc * rstd, rstd


def _norm_bwd(dn, n, rstd):
    return rstd * (dn - jnp.mean(dn, axis=-1, keepdims=True) - n * jnp.mean(dn * n, axis=-1, keepdims=True))


def _colsum(v):
    return jnp.sum(v, axis=0, keepdims=True)


def _gelu(x):
    return 0.5 * x * (1.0 + jnp.tanh(GELU_K * (x + GELU_C * x * x * x)))


def _gelu_grad(x):
    t = jnp.tanh(GELU_K * (x + GELU_C * x * x * x))
    return 0.5 * (1.0 + t) + 0.5 * x * (1.0 - t * t) * GELU_K * (1.0 + 3.0 * GELU_C * x * x)


def _log_sigmoid_parts(z):
    lb = jnp.minimum(z, 0.0) - jnp.log(1.0 + jnp.exp(-jnp.abs(z)))
    return lb, lb - z


def _qkv_prep(proj, t):
    s = proj.shape[0]
    nb, nhp = s // t, 512 // LANES

    def body(q_ref, k_ref, v_ref, qs_ref, kb_ref, vb_ref, kt_ref, vt_ref):
        qs_ref[...] = (q_ref[...] * (1.0 / math.sqrt(HEAD_DIM))).astype(BF16)
        k, v = k_ref[...], v_ref[...]
        kb_ref[...] = k.astype(BF16)
        vb_ref[...] = v.astype(BF16)
        for hp in range(nhp):
            kt_ref[hp, 0] = k[:, hp * LANES:(hp + 1) * LANES].T.astype(BF16)
            vt_ref[hp, 0] = v[:, hp * LANES:(hp + 1) * LANES].T.astype(BF16)

    col = lambda cb: pl.BlockSpec((t, 512), lambda i, cb=cb: (i, cb))
    row_out = pl.BlockSpec((t, 512), lambda i: (i, 0))
    t_out = pl.BlockSpec((nhp, 1, LANES, t), lambda i: (0, i, 0, 0))
    return pl.pallas_call(
        body, name="qkv_prep", grid=(nb,), in_specs=[col(0), col(1), col(2)],
        out_specs=[row_out, row_out, row_out, t_out, t_out],
        out_shape=[jax.ShapeDtypeStruct((s, 512), BF16)] * 3 + [jax.ShapeDtypeStruct((nhp, nb, LANES, t), BF16)] * 2,
        compiler_params=_params(dimension_semantics=("arbitrary",)),
    )(proj, proj, proj)


def _tile_masks(t):
    row = lax.broadcasted_iota(jnp.int32, (t, t), 0)
    col = lax.broadcasted_iota(jnp.int32, (t, t), 1)
    return row, col


def _walk_tiles(i, tile, state, descending):
    def pair(p, st):
        j = i - 1 - 2 * p if descending else 2 * p
        return tile(j - 1 if descending else j + 1, tile(j, st, False), False)

    def rest(st):
        return lax.cond(i % 2 == 1, lambda s_: tile(0 if descending else i - 1, s_, False), lambda s_: s_, st)

    if descending:
        return rest(lax.fori_loop(0, i // 2, pair, tile(i, state, True)))
    return tile(i, rest(lax.fori_loop(0, i // 2, pair, state)), True)


def _attn_fwd(qs, k, vt3, t):
    s = qs.shape[0]
    nb, nhp = s // t, qs.shape[1] // LANES

    def body(q_ref, k_ref, vt_ref, o_ref, car_ref):
        i = pl.program_id(1)
        q2 = q_ref[...]
        lane_q = lax.broadcasted_iota(jnp.int32, q2.shape, 1)
        row, col = _tile_masks(t)
        later = (col > row).astype(BF16)
        valid = row < col
        orow = lax.broadcasted_iota(jnp.int32, (LANES, t), 0)
        car_ref[...] = jnp.zeros_like(car_ref)
        qh = [jnp.where((lane_q < HEAD_DIM) == (hh == 0), q2, jnp.zeros_like(q2)) for hh in range(2)]

        def tile(j, state, masked):
            kb = k_ref[pl.ds(pl.multiple_of(j * t, t), t), :]
            vt = vt_ref[0, j]
            out = []
            for hh in range(2):
                c_after, acc = state[hh]
                z = lax.dot_general(kb, qh[hh], (((1,), (1,)), ((), ())), preferred_element_type=F32)
                lb, l1m = _log_sigmoid_parts(z)
                if masked:
                    l1m = jnp.where(valid, l1m, 0.0)
                aft = jnp.dot(later, l1m.astype(BF16), preferred_element_type=F32)
                w = jnp.exp(lb + aft + c_after)
                if masked:
                    w = jnp.where(valid, w, 0.0)
                car_ref[hh, pl.ds(j, 1), :] = c_after
                out.append((c_after + _colsum(l1m), acc + jnp.dot(vt, w.astype(BF16), preferred_element_type=F32)))
            return tuple(out)

        zero = (jnp.zeros((1, t), F32), jnp.zeros((LANES, t), F32))
        (_, acc0), (_, acc1) = _walk_tiles(i, tile, (zero, zero), descending=True)
        o_ref[...] = jnp.where(orow < HEAD_DIM, acc0, acc1).T.astype(o_ref.dtype)

    return pl.pallas_call(
        body, name="attn_fwd", grid=(nhp, nb),
        in_specs=[pl.BlockSpec((t, LANES), lambda hp, i: (i, hp)),
                  pl.BlockSpec((s, LANES), lambda hp, i: (0, hp)),
                  pl.BlockSpec((1, nb, LANES, t), lambda hp, i: (hp, 0, 0, 0))],
        out_specs=[pl.BlockSpec((t, LANES), lambda hp, i: (i, hp)),
                   pl.BlockSpec((2, nb, t), lambda hp, i: (hp, 0, i))],
        out_shape=[jax.ShapeDtypeStruct((s, nhp * LANES), BF16), jax.ShapeDtypeStruct((2 * nhp, nb, s), F32)],
        compiler_params=_params(dimension_semantics=("arbitrary", "arbitrary")),
    )(qs, k, vt3)


def _attn_bwd(qs, do, k, v, kt3, car, t):
    s = qs.shape[0]
    nb, nhp = s // t, qs.shape[1] // LANES

    def body(q_ref, do_ref, k_ref, v_ref, kt_ref, car_ref, dq_ref, dk_ref, dv_ref):
        i = pl.program_id(1)

        @pl.when(i == 0)
        def _():
            dk_ref[...] = jnp.zeros_like(dk_ref)
            dv_ref[...] = jnp.zeros_like(dv_ref)

        q2, do2 = q_ref[...], do_ref[...]
        lane_q = lax.broadcasted_iota(jnp.int32, q2.shape, 1)
        row, col = _tile_masks(t)
        later = (col > row).astype(BF16)
        earlier = (col < row).astype(BF16)
        valid = row < col
        orow = lax.broadcasted_iota(jnp.int32, (LANES, t), 0)
        head = [(lane_q < HEAD_DIM) == (hh == 0) for hh in range(2)]
        qh = [jnp.where(hm, q2, jnp.zeros_like(q2)) for hm in head]
        doh = [jnp.where(hm, do2, jnp.zeros_like(do2)) for hm in head]

        def tile(j, state, masked):
            rows = pl.ds(pl.multiple_of(j * t, t), t)
            kb, vb, kt = k_ref[rows, :], v_ref[rows, :], kt_ref[0, j]
            out, dk_t, dv_t = [], None, None
            for hh in range(2):
                c_g, dqt = state[hh]
                z = lax.dot_general(kb, qh[hh], (((1,), (1,)), ((), ())), preferred_element_type=F32)
                lb, l1m = _log_sigmoid_parts(z)
                beta = jnp.exp(lb)
                if masked:
                    l1m = jnp.where(valid, l1m, 0.0)
                aft = jnp.dot(later, l1m.astype(BF16), preferred_element_type=F32)
                w = jnp.exp(lb + aft + car_ref[hh, pl.ds(j, 1), :])
                if masked:
                    w = jnp.where(valid, w, 0.0)
                dw = lax.dot_general(vb, doh[hh], (((1,), (1,)), ((), ())), preferred_element_type=F32)
                g = dw * w
                g_before = jnp.dot(earlier, g.astype(BF16), preferred_element_type=F32) + c_g
                dz = g - beta * (g + g_before)
                if masked:
                    dz = jnp.where(valid, dz, 0.0)
                dzb, wb = dz.astype(BF16), w.astype(BF16)
                dk_h = jnp.dot(dzb, qh[hh], preferred_element_type=F32)
                dv_h = jnp.dot(wb, doh[hh], preferred_element_type=F32)
                dk_t = dk_h if dk_t is None else dk_t + dk_h
                dv_t = dv_h if dv_t is None else dv_t + dv_h
                out.append((c_g + _colsum(g), dqt + jnp.dot(kt, dzb, preferred_element_type=F32)))
            dk_ref[rows, :] += dk_t
            dv_ref[rows, :] += dv_t
            return tuple(out)

        zero = (jnp.zeros((1, t), F32), jnp.zeros((LANES, t), F32))
        (_, dq0), (_, dq1) = _walk_tiles(i, tile, (zero, zero), descending=False)
        dq_ref[...] = jnp.where(orow < HEAD_DIM, dq0, dq1).T

    tile_spec = pl.BlockSpec((t, LANES), lambda hp, i: (i, hp))
    whole = pl.BlockSpec((s, LANES), lambda hp, i: (0, hp))
    return pl.pallas_call(
        body, name="attn_bwd", grid=(nhp, nb),
        in_specs=[tile_spec, tile_spec, whole, whole,
                  pl.BlockSpec((1, nb, LANES, t), lambda hp, i: (hp, 0, 0, 0)),
                  pl.BlockSpec((2, nb, t), lambda hp, i: (hp, 0, i))],
        out_specs=[tile_spec, whole, whole],
        out_shape=[jax.ShapeDtypeStruct((s, nhp * LANES), F32)] * 3,
        compiler_params=_params(dimension_semantics=("arbitrary", "arbitrary")),
    )(qs, do, k, v, kt3, car)


SCAN_LANES = 2048
SCAN_ROWS = 8


def _scan_chunks(v):
    n = v.shape[1] // (2 * LANES)
    return [(v[:, c * 2 * LANES:c * 2 * LANES + LANES], v[:, c * 2 * LANES + LANES:(c + 1) * 2 * LANES]) for c in range(n)]


def _ssm_scan_fwd(bu, lam):
    s, w = bu.shape
    tt = _pick(s, (512, 256, 128, 8))
    nt = s // tt

    def body(x_ref, lam_ref, h_ref, st_ref):
        @pl.when(pl.program_id(1) == 0)
        def _():
            st_ref[...] = jnp.zeros_like(st_ref)

        lam_c = _scan_chunks(lam_ref[...])

        def tile(it, state):
            r0 = pl.multiple_of(it * SCAN_ROWS, SCAN_ROWS)
            x_c = _scan_chunks(x_ref[pl.ds(r0, SCAN_ROWS), :])
            state = list(state)
            out_rows = []
            for r in range(SCAN_ROWS):
                parts = []
                for c, ((lr, li), (xr, xi)) in enumerate(zip(lam_c, x_c)):
                    hr, hi = state[2 * c], state[2 * c + 1]
                    nhr = lr * hr - li * hi + xr[r:r + 1]
                    nhi = lr * hi + li * hr + xi[r:r + 1]
                    state[2 * c], state[2 * c + 1] = nhr, nhi
                    parts += [nhr, nhi]
                out_rows.append(jnp.concatenate(parts, axis=1))
            h_ref[pl.ds(r0, SCAN_ROWS), :] = jnp.concatenate(out_rows, axis=0)
            return tuple(state)

        st = st_ref[0:1, :]
        init = tuple(st[:, c * LANES:(c + 1) * LANES] for c in range(SCAN_LANES // LANES))
        fin = lax.fori_loop(0, tt // SCAN_ROWS, tile, init)
        st_ref[0:1, :] = jnp.concatenate(fin, axis=1)

    return pl.pallas_call(
        body, name="ssm_scan_fwd", grid=(w // SCAN_LANES, nt),
        in_specs=[pl.BlockSpec((tt, SCAN_LANES), lambda lc, i: (i, lc)),
                  pl.BlockSpec((1, SCAN_LANES), lambda lc, i: (0, lc))],
        out_specs=pl.BlockSpec((tt, SCAN_LANES), lambda lc, i: (i, lc)),
        out_shape=jax.ShapeDtypeStruct((s, w), F32),
        scratch_shapes=[pltpu.VMEM((SCAN_ROWS, SCAN_LANES), F32)],
        compiler_params=_params(dimension_semantics=("arbitrary", "arbitrary")),
    )(bu, lam)


def _ssm_scan_bwd(e, h, lam):
    s, w = e.shape
    tt = _pick(s, (512, 256, 128, 8))
    nt = s // tt

    def body(e_ref, h_ref, lam_ref, a_ref, dlam_ref, st_ref):
        @pl.when(pl.program_id(1) == 0)
        def _():
            st_ref[...] = jnp.zeros_like(st_ref)
            dlam_ref[...] = jnp.zeros_like(dlam_ref)

        lam_c = _scan_chunks(lam_ref[...])
        nch = len(lam_c)

        def tile(it, carry):
            r0 = pl.multiple_of((tt // SCAN_ROWS - 1 - it) * SCAN_ROWS, SCAN_ROWS)
            e_c = _scan_chunks(e_ref[pl.ds(r0, SCAN_ROWS), :])
            h_c = _scan_chunks(h_ref[pl.ds(r0, SCAN_ROWS), :])
            carry = list(carry)
            out_rows = [None] * SCAN_ROWS
            for r in reversed(range(SCAN_ROWS)):
                parts = []
                for c in range(nch):
                    (lr, li), (er, ei), (hr, hi) = lam_c[c], e_c[c], h_c[c]
                    ar, ai, dr, di = carry[4 * c:4 * c + 4]
                    hr, hi = hr[r:r + 1], hi[r:r + 1]
                    dr = dr + ar * hr + ai * hi
                    di = di + ai * hr - ar * hi
                    nar = lr * ar + li * ai + er[r:r + 1]
                    nai = lr * ai - li * ar + ei[r:r + 1]
                    carry[4 * c:4 * c + 4] = [nar, nai, dr, di]
                    parts += [nar, nai]
                out_rows[r] = jnp.concatenate(parts, axis=1)
            a_ref[pl.ds(r0, SCAN_ROWS), :] = jnp.concatenate(out_rows, axis=0)
            return tuple(carry)

        st, dl = st_ref[0:1, :], dlam_ref[0:1, :]
        init = []
        for c in range(nch):
            lo = c * 2 * LANES
            init += [st[:, lo:lo + LANES], st[:, lo + LANES:lo + 2 * LANES],
                     dl[:, lo:lo + LANES], dl[:, lo + LANES:lo + 2 * LANES]]
        fin = lax.fori_loop(0, tt // SCAN_ROWS, tile, tuple(init))
        st_ref[0:1, :] = jnp.concatenate([fin[4 * c + q] for c in range(nch) for q in (0, 1)], axis=1)
        dlam_ref[0:1, :] = jnp.concatenate([fin[4 * c + q] for c in range(nch) for q in (2, 3)], axis=1)

    rev = pl.BlockSpec((tt, SCAN_LANES), lambda lc, i: (nt - 1 - i, lc))
    vec = pl.BlockSpec((1, SCAN_LANES), lambda lc, i: (0, lc))
    return pl.pallas_call(
        body, name="ssm_scan_bwd", grid=(w // SCAN_LANES, nt),
        in_specs=[rev, rev, vec], out_specs=[rev, pl.BlockSpec((SCAN_ROWS, SCAN_LANES), lambda lc, i: (0, lc))],
        out_shape=[jax.ShapeDtypeStruct((s, w), F32), jax.ShapeDtypeStruct((SCAN_ROWS, w), F32)],
        scratch_shapes=[pltpu.VMEM((SCAN_ROWS, SCAN_LANES), F32)],
        compiler_params=_params(dimension_semantics=("arbitrary", "arbitrary")),
    )(e, h, lam)


def _ssm_params_fwd(a_re, a_im, log_dt, b_re, b_im):
    def body(ar_ref, ai_ref, ldt_ref, br_ref, bi_ref, lr_ref, li_ref, bbr_ref, bbi_ref):
        ar, ai, dt = ar_ref[...], ai_ref[...], jnp.exp(ldt_ref[...])
        mag = jnp.exp(ar * dt)
        lr, li = mag * jnp.cos(ai * dt), mag * jnp.sin(ai * dt)
        den = ar * ar + ai * ai
        cr = ((lr - 1.0) * ar + li * ai) / den
        ci = (li * ar - (lr - 1.0) * ai) / den
        br, bi = br_ref[...], bi_ref[...]
        lr_ref[...], li_ref[...] = lr, li
        bbr_ref[...] = cr * br - ci * bi
        bbi_ref[...] = cr * bi + ci * br

    n = a_re.shape[0]
    v1, v16 = jax.ShapeDtypeStruct((n, 1), F32), jax.ShapeDtypeStruct((n, SSM_GROUP), F32)
    return pl.pallas_call(body, name="ssm_params_fwd", out_shape=[v1, v1, v16, v16],
                          compiler_params=_params())(a_re, a_im, log_dt, b_re, b_im)


def _ssm_params_bwd(a_re, a_im, log_dt, b_re, b_im, g_lr, g_li, g_bbr, g_bbi):
    n = a_re.shape[0]

    def body(ar_ref, ai_ref, ldt_ref, br_ref, bi_ref, glr_ref, gli_ref, gbr_ref, gbi_ref,
             dar_ref, dai_ref, dldt_ref, dbr_ref, dbi_ref):
        ar, ai, dt = ar_ref[...], ai_ref[...], jnp.exp(ldt_ref[...])
        mag = jnp.exp(ar * dt)
        lr, li = mag * jnp.cos(ai * dt), mag * jnp.sin(ai * dt)
        den = ar * ar + ai * ai
        cr = ((lr - 1.0) * ar + li * ai) / den
        ci = (li * ar - (lr - 1.0) * ai) / den
        br, bi, gbr, gbi = br_ref[...], bi_ref[...], gbr_ref[...], gbi_ref[...]
        dbr_ref[...] = gbr * cr + gbi * ci
        dbi_ref[...] = gbi * cr - gbr * ci
        gcr = jnp.sum(gbr * br + gbi * bi, axis=1, keepdims=True)
        gci = jnp.sum(gbi * br - gbr * bi, axis=1, keepdims=True)
        ir, ii = ar / den, -ai / den
        glr = glr_ref[...] + gcr * ir + gci * ii
        gli = gli_ref[...] + gci * ir - gcr * ii
        qr, qi = cr * ir - ci * ii, cr * ii + ci * ir
        gar = -(gcr * qr + gci * qi)
        gai = -(gci * qr - gcr * qi)
        gxr = glr * lr + gli * li
        gxi = gli * lr - glr * li
        dar_ref[...] = gar + gxr * dt
        dai_ref[...] = gai + gxi * dt
        gdt = (gxr * ar + gxi * ai) * dt
        rowg = lax.broadcasted_iota(jnp.int32, (n, SSM_GROUPS), 0) // SSM_STATE
        colg = lax.broadcasted_iota(jnp.int32, (n, SSM_GROUPS), 1)
        dldt_ref[...] = jnp.sum(jnp.where(rowg == colg, gdt, 0.0), axis=0, keepdims=True)

    v1, v16 = jax.ShapeDtypeStruct((n, 1), F32), jax.ShapeDtypeStruct((n, SSM_GROUP), F32)
    return pl.pallas_call(body, name="ssm_params_bwd",
                          out_shape=[v1, v1, jax.ShapeDtypeStruct((1, SSM_GROUPS), F32), v16, v16],
                          compiler_params=_params())(a_re, a_im, log_dt, b_re, b_im, g_lr, g_li, g_bbr, g_bbi)


def _interleave(re, im, axis):
    shp = list(re.shape)
    new = shp[:axis] + [shp[axis] // LANES, LANES] + shp[axis + 1:]
    st = jnp.stack([re.reshape(new), im.reshape(new)], axis=axis + 1)
    return st.reshape(shp[:axis] + [2 * shp[axis]] + shp[axis + 1:])


def _deinterleave(v, axis):
    shp = list(v.shape)
    r = v.reshape(shp[:axis] + [shp[axis] // (2 * LANES), 2, LANES] + shp[axis + 1:])
    out = shp[:axis] + [shp[axis] // 2] + shp[axis + 1:]
    return (lax.index_in_dim(r, 0, axis + 1, keepdims=False).reshape(out),
            lax.index_in_dim(r, 1, axis + 1, keepdims=False).reshape(out))


def _b_matrix(bbr, bbi):
    eye = jnp.eye(SSM_GROUPS, dtype=F32)

    def blockdiag(v):
        x = v.reshape(SSM_GROUPS, SSM_STATE, SSM_GROUP).transpose(0, 2, 1)
        return (eye[:, None, :, None] * x[:, :, None, :]).reshape(SSM_GROUPS * SSM_GROUP, N_STATE)

    return _interleave(blockdiag(bbr), blockdiag(bbi), 1)


def _diag_blocks(v, rows, cols):
    return jnp.stack([v[g * rows:(g + 1) * rows, g * cols:(g + 1) * cols] for g in range(SSM_GROUPS)])


def _b_matrix_grad(d):
    def diag(v):
        return _diag_blocks(v, SSM_GROUP, SSM_STATE).transpose(0, 2, 1).reshape(N_STATE, SSM_GROUP)

    dr, di = _deinterleave(d, 1)
    return diag(dr), diag(di)


def _c_matrix(c_re, c_im):
    eye = jnp.eye(SSM_GROUPS, dtype=F32)

    def blockdiag(v):
        x = v.transpose(0, 2, 1)
        return (x[:, :, None, :] * eye[:, None, :, None]).reshape(N_STATE, SSM_GROUPS * SSM_GROUP)

    return _interleave(blockdiag(c_re), blockdiag(-c_im), 0)


def _c_matrix_grad(d):
    def diag(v):
        return _diag_blocks(v, SSM_STATE, SSM_GROUP).transpose(0, 2, 1)

    dr, di = _deinterleave(d, 0)
    return diag(dr), -diag(di)


def _row(v):
    return v.reshape(1, -1)


def _ssm_inputs(p):
    rows = lambda v: v.reshape(N_STATE, -1)
    ldt = jnp.repeat(p["ssm_log_dt"], SSM_STATE).reshape(N_STATE, 1)
    return rows(p["ssm_a_re"]), rows(p["ssm_a_im"]), ldt, rows(p["ssm_b_re"]), rows(p["ssm_b_im"])


def _layer_fwd(x, mod, p, tag):
    d = x.shape[1]
    sh_m, sc_m, g_m, sh_f, sc_f, g_f = [_row(mod[i]) for i in range(6)]
    nm = lambda s: f"{s}_{tag}"

    def lnmod(x, sc, sh):
        return _norm(x)[0] * (1.0 + sc) + sh

    h1 = _rowwise(nm("lnmod1"), lnmod, [(x, 0, d)], [sc_m, sh_m], [(d, BF16)])
    proj = _mm(nm("proj"), h1, p["w_in"], "nn")
    t = min(ATT_TILE, x.shape[0])
    qs, kb, vb, kt3, vt3 = _qkv_prep(proj, t)
    att, car = _attn_fwd(qs, kb, vt3, t)
    y_sb = _mm(nm("sb_up"), att, p["w_sb_up"], "nn")

    lam_r, lam_i, bbr, bbi = _ssm_params_fwd(*_ssm_inputs(p))
    lam = _interleave(lam_r.reshape(1, N_STATE), lam_i.reshape(1, N_STATE), 1)
    bmat = _b_matrix(bbr, bbi).astype(BF16)
    cmat = _c_matrix(p["ssm_c_re"], p["ssm_c_im"]).astype(BF16)
    bu = _mm(nm("ssm_b"), proj, bmat, "nn", a_cols=(1536, 512))
    hst = _ssm_scan_fwd(bu, lam)
    yc = _mm(nm("ssm_c"), hst, cmat, "nn")

    def ssm_act(yc, u, dsk):
        y0 = yc + dsk * u
        return y0, _gelu(y0)

    y0, y1 = _rowwise(nm("ssm_act"), ssm_act, [(yc, 0, 512), (proj, 3, 512)], [_row(p["ssm_d"])], [(512, F32), (512, F32)])
    gl = _mm(nm("glu"), y1, p["w_glu"], "nn")
    y2 = _rowwise(nm("glu_act"), lambda y1, gl, b: y1 * jax.nn.sigmoid(gl + b), [(y1, 0, 512), (gl, 0, 512)],
                  [_row(p["b_glu"])], [(512, BF16)])
    y_ssm = _mm(nm("ssm_up"), y2, p["w_ssm_up"], "nn")

    def merge(gsb, gss, ysb, yss):
        return jax.nn.sigmoid(gsb) * ysb + jax.nn.sigmoid(gss) * yss

    merged = _rowwise(nm("merge"), merge, [(proj, 2, d), (proj, 3, d), (y_sb, 0, d), (y_ssm, 0, d)], [], [(d, BF16)])
    y = _mm(nm("out"), merged, p["w_out"], "nn")

    def resid_ln(x, y, g, lg, lb):
        return _norm(ALPHA * x + (1.0 + g) * y)[0] * lg + lb

    x1 = _rowwise(nm("ln1"), resid_ln, [(x, 0, d), (y, 0, d)], [g_m, _row(p["ln1_g"]), _row(p["ln1_b"])], [(d, F32)])
    h2 = _rowwise(nm("lnmod2"), lnmod, [(x1, 0, d)], [sc_f, sh_f], [(d, BF16)])
    f = _mm(nm("ffn_in"), h2, p["w_ffn_in"], "nn", out_dtype=BF16)
    fh = f.shape[1] // 2

    def swiglu(g, u):
        g = g.astype(F32)
        return g * jax.nn.sigmoid(g) * u.astype(F32)

    act = _rowwise(nm("swiglu"), swiglu, [(f, 0, fh), (f, 1, fh)], [], [(fh, BF16)])
    yf = _mm(nm("ffn_out"), act, p["w_ffn_out"], "nn")
    x2 = _rowwise(nm("ln2"), resid_ln, [(x1, 0, d), (yf, 0, d)], [g_f, _row(p["ln2_g"]), _row(p["ln2_b"])], [(d, F32)])
    saved = dict(x=x, h1=h1, proj=proj, qs=qs, kb=kb, vb=vb, kt3=kt3, car=car, att=att, y_sb=y_sb, lam=lam, bmat=bmat,
                 cmat=cmat, hst=hst, y0=y0, y1=y1, gl=gl, y2=y2, y_ssm=y_ssm, merged=merged, y=y, x1=x1, h2=h2, f=f,
                 act=act, yf=yf, t=t)
    return x2, saved


def _layer_bwd(dx2, mod, p, sv, layer, depth, stacked):
    d = dx2.shape[1]
    sh_m, sc_m, g_m, sh_f, sc_f, g_f = [_row(mod[i]) for i in range(6)]
    nm = lambda s: f"{s}_{layer}"
    grads = {}

    def weight_grad(n, a, b, **kw):
        grads[n] = _mm(nm("d" + n), a, b, "tn", into=(stacked.get(n), layer, depth), **kw)

    def resid_ln_bwd(x, y, dxo, g, lg):
        n, rstd = _norm(ALPHA * x + (1.0 + g) * y)
        dr = _norm_bwd(dxo * lg, n, rstd)
        return ALPHA * dr, (1.0 + g) * dr, _colsum(dxo * n), _colsum(dxo), _colsum(dr * y)

    def lnmod_bwd(x, dh, dxa, sc):
        n, rstd = _norm(x)
        return dxa + _norm_bwd(dh * (1.0 + sc), n, rstd), _colsum(dh * n), _colsum(dh)

    dx1a, dyf, grads["ln2_g"], grads["ln2_b"], dg_f = _rowwise(
        nm("ln2_bwd"), resid_ln_bwd, [(sv["x1"], 0, d), (sv["yf"], 0, d), (dx2, 0, d)], [g_f, _row(p["ln2_g"])],
        [(d, F32), (d, BF16)], [d, d, d])
    dact = _mm(nm("d_act"), dyf, p["w_ffn_out"], "nt")
    weight_grad("w_ffn_out", sv["act"], dyf)
    fh = sv["f"].shape[1] // 2

    def swiglu_bwd(g, u, da):
        g, u = g.astype(F32), u.astype(F32)
        sg = jax.nn.sigmoid(g)
        return jnp.concatenate([da * u * sg * (1.0 + g * (1.0 - sg)), da * g * sg], axis=1)

    df = _rowwise(nm("swiglu_bwd"), swiglu_bwd, [(sv["f"], 0, fh), (sv["f"], 1, fh), (dact, 0, fh)], [], [(2 * fh, BF16)])
    dh2 = _mm(nm("d_h2"), df, p["w_ffn_in"], "nt")
    weight_grad("w_ffn_in", sv["h2"], df)
    dx1, dsc_f, dsh_f = _rowwise(nm("lnmod2_bwd"), lnmod_bwd, [(sv["x1"], 0, d), (dh2, 0, d), (dx1a, 0, d)], [sc_f],
                                 [(d, F32)], [d, d])
    dxa, dy, grads["ln1_g"], grads["ln1_b"], dg_m = _rowwise(
        nm("ln1_bwd"), resid_ln_bwd, [(sv["x"], 0, d), (sv["y"], 0, d), (dx1, 0, d)], [g_m, _row(p["ln1_g"])],
        [(d, F32), (d, BF16)], [d, d, d])
    dmerged = _mm(nm("d_merged"), dy, p["w_out"], "nt")
    weight_grad("w_out", sv["merged"], dy)

    def merge_bwd(gsb, gss, ysb, yss, dm):
        s1, s2 = jax.nn.sigmoid(gsb), jax.nn.sigmoid(gss)
        return s1 * dm, s2 * dm, dm * ysb * s1 * (1.0 - s1), dm * yss * s2 * (1.0 - s2)

    dy_sb, dy_ssm, dg_sb, dg_ssm = _rowwise(
        nm("merge_bwd"), merge_bwd, [(sv["proj"], 2, d), (sv["proj"], 3, d), (sv["y_sb"], 0, d), (sv["y_ssm"], 0, d),
                                     (dmerged, 0, d)], [], [(d, BF16)] * 4)
    dy2 = _mm(nm("d_y2"), dy_ssm, p["w_ssm_up"], "nt")
    weight_grad("w_ssm_up", sv["y2"], dy_ssm)

    def glu_act_bwd(y1, gl, dy2, b):
        sg = jax.nn.sigmoid(gl + b)
        dgl = dy2 * y1 * sg * (1.0 - sg)
        return dy2 * sg, dgl, _colsum(dgl)

    dy1a, dgl, grads["b_glu"] = _rowwise(nm("glu_act_bwd"), glu_act_bwd, [(sv["y1"], 0, 512), (sv["gl"], 0, 512), (dy2, 0, 512)],
                                         [_row(p["b_glu"])], [(512, F32), (512, BF16)], [512])
    dy1b = _mm(nm("d_y1"), dgl, p["w_glu"], "nt")
    weight_grad("w_glu", sv["y1"], dgl)

    def ssm_act_bwd(y0, u, dy1a, dy1b, dsk):
        dy0 = (dy1a + dy1b) * _gelu_grad(y0)
        return dy0, dsk * dy0, _colsum(dy0 * u)

    dy0, du_a, grads["ssm_d"] = _rowwise(nm("ssm_act_bwd"), ssm_act_bwd,
                                         [(sv["y0"], 0, 512), (sv["proj"], 3, 512), (dy1a, 0, 512), (dy1b, 0, 512)],
                                         [_row(p["ssm_d"])], [(512, BF16), (512, F32)], [512])
    e = _mm(nm("ssm_e"), dy0, sv["cmat"], "nt")
    grads["ssm_c_re"], grads["ssm_c_im"] = _c_matrix_grad(_mm(nm("dw_ssm_c"), sv["hst"], dy0, "tn"))
    adj, dlam = _ssm_scan_bwd(e, sv["hst"], sv["lam"])
    du_b = _mm(nm("d_u"), adj, sv["bmat"], "nt")
    g_bbr, g_bbi = _b_matrix_grad(_mm(nm("dw_ssm_b"), sv["proj"], adj, "tn", a_cols=(1536, 512)))
    g_lr, g_li = _deinterleave(dlam[0:1], 1)
    da_re, da_im, dldt, db_re, db_im = _ssm_params_bwd(*_ssm_inputs(p), g_lr.reshape(N_STATE, 1), g_li.reshape(N_STATE, 1),
                                                       g_bbr, g_bbi)
    grads["ssm_a_re"] = da_re.reshape(SSM_GROUPS, SSM_STATE)
    grads["ssm_a_im"] = da_im.reshape(SSM_GROUPS, SSM_STATE)
    grads["ssm_log_dt"] = dldt.reshape(SSM_GROUPS)
    grads["ssm_b_re"] = db_re.reshape(SSM_GROUPS, SSM_STATE, SSM_GROUP)
    grads["ssm_b_im"] = db_im.reshape(SSM_GROUPS, SSM_STATE, SSM_GROUP)
    datt = _mm(nm("d_att"), dy_sb, p["w_sb_up"], "nt", out_dtype=BF16)
    weight_grad("w_sb_up", sv["att"], dy_sb)
    dqs, dk, dv = _attn_bwd(sv["qs"], datt, sv["kb"], sv["vb"], sv["kt3"], sv["car"], sv["t"])

    def dproj_cols(dqs, dk, dv, dua, dub, dgsb, dgss):
        return jnp.concatenate([dqs * (1.0 / math.sqrt(HEAD_DIM)), dk, dv, dua + dub, dgsb.astype(F32), dgss.astype(F32)],
                               axis=1)

    dproj = _rowwise(nm("dproj"), dproj_cols, [(dqs, 0, 512), (dk, 0, 512), (dv, 0, 512), (du_a, 0, 512), (du_b, 0, 512),
                                               (dg_sb, 0, d), (dg_ssm, 0, d)], [], [(2048 + 2 * d, BF16)])
    dh1 = _mm(nm("d_h1"), dproj, p["w_in"], "nt")
    weight_grad("w_in", sv["h1"], dproj)
    dx, dsc_m, dsh_m = _rowwise(nm("lnmod1_bwd"), lnmod_bwd, [(sv["x"], 0, d), (dh1, 0, d), (dxa, 0, d)], [sc_m],
                                [(d, F32)], [d, d])
    for k in ("ln1_g", "ln1_b", "ln2_g", "ln2_b", "ssm_d", "b_glu"):
        grads[k] = grads[k].reshape(-1)
    dmod = jnp.concatenate([dsh_m, dsc_m, dg_m, dsh_f, dsc_f, dg_f], axis=0)
    return dx, dmod, grads


def _loss_head(x, target):
    d = x.shape[1]

    def fn(x, tgt):
        err = x - tgt
        return err * (1.0 / d), _colsum(err * err) * (0.5 / d)

    return _rowwise("loss_head", fn, [(x, 0, d), (target, 0, d)], [], [(d, F32)], [d])


def _place():
    return lax.axis_index("x"), lax.axis_index("y"), lax.axis_index("c")


def _all_gather8(name, block):
    m_per, n = block.shape

    def body(x_ref, out_ref, send_sems, recv_sems, local_sem):
        x, y, c = _place()
        me, sibling = (x, y, c), (x, y, 1 - c)
        chips = [(1 - x, y), (x, 1 - y), (1 - x, 1 - y)]

        def rows(px, py, pc):
            return out_ref.at[pl.ds(pl.multiple_of((4 * px + 2 * py + pc) * m_per, 8), m_per), :]

        def copy(k, blk, to, src=None):
            return pltpu.make_async_remote_copy(src_ref=rows(*blk) if src is None else src, dst_ref=rows(*blk),
                                                send_sem=send_sems.at[k], recv_sem=recv_sems.at[k],
                                                device_id=to, device_id_type=MESH)

        mine = pltpu.make_async_copy(x_ref, rows(*me), local_sem)
        mine.start()
        first = [copy(0, me, sibling, src=x_ref)] + [copy(1 + j, me, (*chip, c), src=x_ref) for j, chip in enumerate(chips)]
        for cp in first:
            cp.start()
        passed = [copy(4 + j, (*chip, c), sibling) for j, chip in enumerate(chips)]
        for j, chip in enumerate(chips):
            copy(1 + j, (*chip, c), me).wait_recv()
            passed[j].start()
        copy(0, sibling, me).wait_recv()
        for j, chip in enumerate(chips):
            copy(4 + j, (*chip, 1 - c), me).wait_recv()
        for cp in first + passed:
            cp.wait_send()
        mine.wait()

    return pl.pallas_call(
        body, name=name, out_shape=jax.ShapeDtypeStruct((8 * m_per, n), block.dtype),
        in_specs=[pl.BlockSpec(memory_space=pltpu.VMEM)], out_specs=pl.BlockSpec(memory_space=pltpu.VMEM),
        scratch_shapes=[pltpu.SemaphoreType.DMA((7,)), pltpu.SemaphoreType.DMA((7,)), pltpu.SemaphoreType.DMA],
        compiler_params=_params(),
    )(block)


def _other_chips(x, y):
    return [(1 - x, y), (x, 1 - y), (1 - x, 1 - y)]


def _gather_weights(shards, by_rows):
    n = len(shards)

    def body(*refs):
        src, dst = refs[:n], refs[n:2 * n]
        send_sems, recv_sems, local_sems = refs[2 * n:]
        x, y, c = _place()

        def block(k, px, py):
            _, r, cols = shards[k].shape
            q = 2 * px + py
            if by_rows[k]:
                return dst[k].at[:, pl.ds(pl.multiple_of(q * r, 16), r), :]
            return dst[k].at[:, :, pl.ds(pl.multiple_of(q * cols, LANES), cols)]

        local = [pltpu.make_async_copy(src[k], block(k, x, y), local_sems.at[k]) for k in range(n)]
        for cp in local:
            cp.start()
        sends = []
        for k in range(n):
            for j, (px, py) in enumerate(_other_chips(x, y)):
                cp = pltpu.make_async_remote_copy(src_ref=src[k], dst_ref=block(k, x, y), send_sem=send_sems.at[k, j],
                                                  recv_sem=recv_sems.at[k, j], device_id=(px, py, c), device_id_type=MESH)
                cp.start()
                sends.append(cp)
        for k in range(n):
            for j, (px, py) in enumerate(_other_chips(x, y)):
                pltpu.make_async_remote_copy(src_ref=src[k], dst_ref=block(k, px, py), send_sem=send_sems.at[k, j],
                                             recv_sem=recv_sems.at[k, j], device_id=(px, py, c),
                                             device_id_type=MESH).wait_recv()
        for cp in sends:
            cp.wait_send()
        for cp in local:
            cp.wait()

    def whole(s, rows):
        l, r, cols = s.shape
        return jax.ShapeDtypeStruct((l, 4 * r, cols) if rows else (l, r, 4 * cols), s.dtype)

    any_spec = pl.BlockSpec(memory_space=pl.ANY)
    return pl.pallas_call(
        body, name="gather_weights", in_specs=[any_spec] * n, out_specs=[any_spec] * n,
        out_shape=[whole(s, rows) for s, rows in zip(shards, by_rows)],
        scratch_shapes=[pltpu.SemaphoreType.DMA((n, 3)), pltpu.SemaphoreType.DMA((n, 3)), pltpu.SemaphoreType.DMA((n,))],
        compiler_params=_params(),
    )(*shards)


def _peer(x, y, c, r):
    fx, fy, fc = (r >> 2) & 1, (r >> 1) & 1, r & 1
    return (x + fx - 2 * x * fx, y + fy - 2 * y * fy, c + fc - 2 * c * fc)


def _scatter_grads(grads, by_rows):
    n = len(grads)

    def half_shape(k):
        l, r, c = grads[k].shape
        return (l, r // 8, c) if by_rows[k] else (l, r // 2, c // 4)

    def body(*refs):
        src, dst = refs[:n], refs[n:2 * n]
        send_sems, recv_sems, local_sems = refs[2 * n:]
        x, y, c = _place()
        me = 4 * x + 2 * y + c

        def window(k, px, py, pc):
            _, hr, hc = half_shape(k)
            q = 2 * px + py
            if by_rows[k]:
                return src[k].at[:, pl.ds(pl.multiple_of((2 * q + pc) * hr, 8), hr), :]
            return src[k].at[:, pl.ds(pl.multiple_of(pc * hr, 8), hr), pl.ds(pl.multiple_of(q * hc, LANES), hc)]

        local = [pltpu.make_async_copy(window(k, x, y, c), dst[k].at[me], local_sems.at[k]) for k in range(n)]
        for cp in local:
            cp.start()
        sends = []
        for k in range(n):
            for r in range(1, 8):
                to = _peer(x, y, c, r)
                cp = pltpu.make_async_remote_copy(src_ref=window(k, *to), dst_ref=dst[k].at[me], send_sem=send_sems.at[k, r - 1],
                                                  recv_sem=recv_sems.at[k, r - 1], device_id=to, device_id_type=MESH)
                cp.start()
                sends.append(cp)
        for k in range(n):
            for r in range(1, 8):
                px, py, pc = _peer(x, y, c, r)
                pltpu.make_async_remote_copy(src_ref=window(k, x, y, c), dst_ref=dst[k].at[4 * px + 2 * py + pc],
                                             send_sem=send_sems.at[k, r - 1], recv_sem=recv_sems.at[k, r - 1],
                                             device_id=(px, py, pc), device_id_type=MESH).wait_recv()
        for cp in sends:
            cp.wait_send()
        for cp in local:
            cp.wait()

    any_spec = pl.BlockSpec(memory_space=pl.ANY)
    return pl.pallas_call(
        body, name="scatter_grads", in_specs=[any_spec] * n, out_specs=[any_spec] * n,
        out_shape=[jax.ShapeDtypeStruct((8, *half_shape(k)), F32) for k in range(n)],
        scratch_shapes=[pltpu.SemaphoreType.DMA((n, 7)), pltpu.SemaphoreType.DMA((n, 7)), pltpu.SemaphoreType.DMA((n,))],
        compiler_params=_params(),
    )(*grads)


def _sum_slots(name, parts):
    _, l, r, c = parts.shape
    tr = _pick(r, (256, 176, 128, 64, 32, 8))

    def body(p_ref, o_ref):
        acc = p_ref[0]
        for i in range(1, 8):
            acc = acc + p_ref[i]
        o_ref[...] = acc

    return pl.pallas_call(
        body, name=name, grid=(l, r // tr), in_specs=[pl.BlockSpec((8, 1, tr, c), lambda li, i: (0, li, i, 0))],
        out_specs=pl.BlockSpec((1, tr, c), lambda li, i: (li, i, 0)), out_shape=jax.ShapeDtypeStruct((l, r, c), F32),
        compiler_params=_params(dimension_semantics=("arbitrary", "arbitrary")),
    )(parts)


def _swap_halves(halves):
    n = len(halves)

    def body(*refs):
        src, dst = refs[:n], refs[n:2 * n]
        send_sems, recv_sems, local_sems = refs[2 * n:]
        x, y, c = _place()

        def place(k, pc):
            r = halves[k].shape[1]
            return dst[k].at[:, pl.ds(pl.multiple_of(pc * r, 8), r), :]

        local = [pltpu.make_async_copy(src[k], place(k, c), local_sems.at[k]) for k in range(n)]
        sends = [pltpu.make_async_remote_copy(src_ref=src[k], dst_ref=place(k, c), send_sem=send_sems.at[k],
                                              recv_sem=recv_sems.at[k], device_id=(x, y, 1 - c), device_id_type=MESH)
                 for k in range(n)]
        for cp in local + sends:
            cp.start()
        for k in range(n):
            pltpu.make_async_remote_copy(src_ref=src[k], dst_ref=place(k, 1 - c), send_sem=send_sems.at[k],
                                         recv_sem=recv_sems.at[k], device_id=(x, y, 1 - c), device_id_type=MESH).wait_recv()
        for cp in sends:
            cp.wait_send()
        for cp in local:
            cp.wait()

    any_spec = pl.BlockSpec(memory_space=pl.ANY)
    return pl.pallas_call(
        body, name="swap_halves", in_specs=[any_spec] * n, out_specs=[any_spec] * n,
        out_shape=[jax.ShapeDtypeStruct((h.shape[0], 2 * h.shape[1], h.shape[2]), F32) for h in halves],
        scratch_shapes=[pltpu.SemaphoreType.DMA((n,)), pltpu.SemaphoreType.DMA((n,)), pltpu.SemaphoreType.DMA((n,))],
        compiler_params=_params(),
    )(*halves)


def _adamw(name, w, g, m, v):
    shape = w.shape
    cols = shape[-1] if w.ndim > 1 and shape[-1] % LANES == 0 else w.size if w.size % LANES else LANES
    flat = lambda a: a.reshape(-1, cols)
    rows = w.size // cols
    tr = _pick(rows, [r for r in (512, 256, 128, 64, 32, 16, 8) if r * cols <= 256 * 1024]) if rows % 8 == 0 else rows

    def body(w_ref, g_ref, m_ref, v_ref, d_ref, nm_ref, nv_ref):
        gg = g_ref[...]
        nm = ADAM_B1 * m_ref[...] + (1.0 - ADAM_B1) * gg
        nv = ADAM_B2 * v_ref[...] + (1.0 - ADAM_B2) * (gg * gg)
        m_hat = nm / (1.0 - ADAM_B1 ** ADAM_STEP)
        v_hat = nv / (1.0 - ADAM_B2 ** ADAM_STEP)
        d_ref[...] = -ADAM_LR * (m_hat / (jnp.sqrt(v_hat) + ADAM_EPS) + ADAM_WD * w_ref[...])
        nm_ref[...] = nm
        nv_ref[...] = nv

    spec = pl.BlockSpec((tr, cols), lambda i: (i, 0))
    out = pl.pallas_call(
        body, name=name, grid=(rows // tr,), in_specs=[spec] * 4, out_specs=[spec] * 3,
        out_shape=[jax.ShapeDtypeStruct((rows, cols), F32)] * 3,
        compiler_params=_params(dimension_semantics=("arbitrary",)),
    )(flat(w), flat(g), flat(m), flat(v))
    return tuple(o.reshape(shape) for o in out)


WEIGHTS = ["w_ada", "b_ada", "w_in", "w_sb_up", "ssm_a_re", "ssm_a_im", "ssm_log_dt", "ssm_b_re", "ssm_b_im", "ssm_c_re",
           "ssm_c_im", "ssm_d", "w_glu", "b_glu", "w_ssm_up", "w_out", "ln1_g", "ln1_b", "w_ffn_in", "w_ffn_out", "ln2_g",
           "ln2_b"]
COL_SPLIT = ["w_in", "w_sb_up", "w_ssm_up", "w_ffn_in"]
ROW_SPLIT = ["w_glu", "w_out", "w_ffn_out"]
SMALL = ["ssm_a_re", "ssm_a_im", "ssm_log_dt", "ssm_b_re", "ssm_b_im", "ssm_c_re", "ssm_c_im", "ssm_d", "b_glu", "ln1_g",
         "ln1_b", "ln2_g", "ln2_b"]
SLAB_COLS = 1024


def _cast_bf16(name, w):
    shape = w.shape
    flat = w.reshape(-1, shape[-1])
    rows, cols = flat.shape
    tr = _pick(rows, (512, 256, 128, 64, 8))

    def body(w_ref, o_ref):
        o_ref[...] = w_ref[...].astype(BF16)

    spec = pl.BlockSpec((tr, cols), lambda i: (i, 0))
    return pl.pallas_call(body, name=name, grid=(rows // tr,), in_specs=[spec], out_specs=spec,
                          out_shape=jax.ShapeDtypeStruct((rows, cols), BF16),
                          compiler_params=_params(dimension_semantics=("arbitrary",)))(flat).reshape(shape)


def _silu_rows(name, c):
    def body(c_ref, o_ref):
        v = c_ref[...]
        o_ref[...] = v * jax.nn.sigmoid(v)

    return pl.pallas_call(body, name=name, out_shape=jax.ShapeDtypeStruct(c.shape, F32), compiler_params=_params())(c)


def _pad_rows(v, mult=8):
    flat = v.reshape(-1)
    per = mult * SLAB_COLS
    total = -(-flat.size // per) * per
    return jnp.pad(flat, (0, total - flat.size)).reshape(-1, SLAB_COLS)


def kernel(x, c, w_ada, b_ada, w_in, w_sb_up, ssm_a_re, ssm_a_im, ssm_log_dt, ssm_b_re, ssm_b_im, ssm_c_re, ssm_c_im, ssm_d, w_glu, b_glu, w_ssm_up, w_out, ln1_g, ln1_b, w_ffn_in, w_ffn_out, ln2_g, ln2_b, loss_target, m_w_ada, m_b_ada, m_w_in, m_w_sb_up, m_ssm_a_re, m_ssm_a_im, m_ssm_log_dt, m_ssm_b_re, m_ssm_b_im, m_ssm_c_re, m_ssm_c_im, m_ssm_d, m_w_glu, m_b_glu, m_w_ssm_up, m_w_out, m_ln1_g, m_ln1_b, m_w_ffn_in, m_w_ffn_out, m_ln2_g, m_ln2_b, v_w_ada, v_b_ada, v_w_in, v_w_sb_up, v_ssm_a_re, v_ssm_a_im, v_ssm_log_dt, v_ssm_b_re, v_ssm_b_im, v_ssm_c_re, v_ssm_c_im, v_ssm_d, v_w_glu, v_b_glu, v_w_ssm_up, v_w_out, v_ln1_g, v_ln1_b, v_w_ffn_in, v_w_ffn_out, v_ln2_g, v_ln2_b):
    args = dict(locals())
    w = {n: args[n] for n in WEIGHTS}
    mom = {n: args["m_" + n] for n in WEIGHTS}
    var = {n: args["v_" + n] for n in WEIGHTS}
    depth, d = w_ada.shape[0], x.shape[-1]
    xi, yi, ci = _place()
    me, chip = 4 * xi + 2 * yi + ci, 2 * xi + yi
    ada_cols = w_ada.shape[2]

    big = COL_SPLIT + ROW_SPLIT
    by_rows = [n in ROW_SPLIT for n in big]
    full = dict(zip(big, _gather_weights([_cast_bf16(f"cast_{n}", w[n]) for n in big], by_rows)))

    c_all = _all_gather8("gather_c", jnp.pad(c, ((0, 7), (0, 0))))[::8]
    c_act = _silu_rows("silu_c", c_all)
    b_cols = lax.dynamic_slice_in_dim(b_ada, chip * ada_cols, ada_cols, axis=1)
    mod_part = jnp.concatenate([_small_mm(f"mod_{l}", c_act, w_ada[l], "nn") + b_cols[l][None] for l in range(depth)], axis=0)
    mod_all = _all_gather8("gather_mod", mod_part).reshape(4, 2, depth, 8, ada_cols)[:, 0]
    mod_mine = lax.dynamic_index_in_dim(mod_all, me, axis=2, keepdims=False)
    mod = mod_mine.transpose(1, 0, 2).reshape(depth, 6, d)

    layer_w = [{**{n: full[n][l] for n in big}, **{n: w[n][l] for n in SMALL}} for l in range(depth)]
    h, saved = x[0], []
    for l in range(depth):
        h, sv = _layer_fwd(h, mod[l], layer_w[l], str(l))
        saved.append(sv)
    dh, loss_cols = _loss_head(h, loss_target[0])
    loss = lax.psum(jnp.sum(loss_cols), ("x", "y", "c"))
    dmods, lgrads, stacked = [None] * depth, [None] * depth, {}
    for l in reversed(range(depth)):
        dh, dmods[l], lgrads[l] = _layer_bwd(dh, mod[l], layer_w[l], saved[l], l, depth, stacked)
        stacked = {n: lgrads[l][n] for n in big}
    grad_x = dh[None]

    parts = _scatter_grads([stacked[n] for n in big], by_rows)
    halves = [_sum_slots(f"sum_{n}", p) for n, p in zip(big, parts)]
    grad = dict(zip(big, _swap_halves(halves)))

    pieces = [jnp.stack(dmods)] + [jnp.stack([lgrads[l][n] for l in range(depth)]) for n in SMALL]
    slab = jnp.concatenate([_pad_rows(p) for p in pieces], axis=0)
    slabs = _all_gather8("gather_small", slab).reshape(8, 1, *slab.shape)
    total = _sum_slots("sum_small", slabs)[0]
    row = _pad_rows(pieces[0]).shape[0]
    for n, p in zip(SMALL, pieces[1:]):
        rows = _pad_rows(p).shape[0]
        grad[n] = total[row:row + rows].reshape(-1)[:p.size].reshape(p.shape)
        row += rows
    dmod_rows = _pad_rows(pieces[0]).shape[0]
    dmod_all = slabs[:, 0, :dmod_rows].reshape(8, -1)[:, :depth * 6 * d].reshape(8, depth, 4, ada_cols)
    dmod_cols = lax.dynamic_index_in_dim(dmod_all, chip, axis=2, keepdims=False)
    grad["w_ada"] = jnp.stack([_small_mm(f"dw_ada_{l}", c_act, dmod_cols[:, l], "tn") for l in range(depth)])
    dmod_sum = _sum_slots("sum_dmod", slabs[:, :, :dmod_rows])[0]
    grad["b_ada"] = dmod_sum.reshape(-1)[:depth * 6 * d].reshape(depth, 6 * d)

    delta, new_m, new_v = {}, {}, {}
    for n in WEIGHTS:
        delta[n], new_m[n], new_v[n] = _adamw(f"adamw_{n}", w[n], grad[n], mom[n], var[n])
    return (loss, grad_x, *[grad[n] for n in WEIGHTS], *[delta[n] for n in WEIGHTS], *[new_m[n] for n in WEIGHTS],
            *[new_v[n] for n in WEIGHTS])
```

```python
import functools
import math

import jax
import jax.numpy as jnp
from jax import lax
from jax.experimental import pallas as pl
from jax.experimental.pallas import tpu as pltpu

F32 = jnp.float32
BF16 = jnp.bfloat16
MESH = pl.DeviceIdType.MESH

LANES = 128
HEAD_DIM = 64
ATT_TILE = 256
SSM_GROUPS, SSM_STATE, SSM_GROUP = 32, 64, 16
N_STATE = SSM_GROUPS * SSM_STATE
LN_EPS = 1e-5
DEPTH = 2
ALPHA = (2 * DEPTH) ** 0.25
ADAM_LR, ADAM_B1, ADAM_B2, ADAM_EPS, ADAM_WD, ADAM_STEP = 0.001, 0.9, 0.999, 1e-08, 0.01, 10
VMEM_LIMIT = 56 * 1024 * 1024
GELU_K = math.sqrt(2.0 / math.pi)
GELU_C = 0.044715


def _params(**kw):
    return pltpu.CompilerParams(vmem_limit_bytes=VMEM_LIMIT, **kw)


def _pick(n, prefs):
    for p in prefs:
        if n % p == 0:
            return p
    return n


def _rowwise(name, fn, rows, vecs, outs, sums=(), tm=None):
    s = rows[0][0].shape[0]
    tm = tm or _pick(s, (256, 128, 64, 8))
    nin, no, ns = len(rows) + len(vecs), len(outs), len(sums)

    def body(*refs):
        res = fn(*[r[...] for r in refs[:nin]])
        res = res if isinstance(res, tuple) else (res,)
        for r, v in zip(refs[nin:nin + no], res[:no]):
            r[...] = v.astype(r.dtype)
        if ns:
            @pl.when(pl.program_id(0) == 0)
            def _():
                for r in refs[nin + no:]:
                    r[...] = jnp.zeros_like(r)
            for r, v in zip(refs[nin + no:], res[no:]):
                r[...] += v

    in_specs = [pl.BlockSpec((tm, w), lambda i, cb=cb: (i, cb)) for _, cb, w in rows]
    in_specs += [pl.BlockSpec(v.shape, lambda i: (0, 0)) for v in vecs]
    out_specs = [pl.BlockSpec((tm, w), lambda i: (i, 0)) for w, _ in outs]
    out_specs += [pl.BlockSpec((1, w), lambda i: (0, 0)) for w in sums]
    out_shape = [jax.ShapeDtypeStruct((s, w), dt) for w, dt in outs]
    out_shape += [jax.ShapeDtypeStruct((1, w), F32) for w in sums]
    res = pl.pallas_call(
        body, name=name, grid=(s // tm,), in_specs=in_specs, out_specs=out_specs, out_shape=out_shape,
        compiler_params=_params(dimension_semantics=("arbitrary",)),
    )(*[a for a, _, _ in rows], *vecs)
    return res[0] if len(res) == 1 else tuple(res)


MM_TILES = (1408, 1024, 512, 256, 128)


def _mm(name, a, b, mode, out_dtype=F32, a_cols=None, into=None):
    a_off, a_w = a_cols if a_cols else (0, a.shape[1])
    if mode == "nn":
        m, k, n = a.shape[0], a_w, b.shape[1]
    elif mode == "nt":
        m, k, n = a.shape[0], a_w, b.shape[0]
    else:
        k, m, n = a.shape[0], a_w, b.shape[1]
    tm = _pick(m, MM_TILES if mode == "tn" else MM_TILES[1:])
    tn = _pick(n, MM_TILES)
    tk = _pick(k, MM_TILES)
    nk = k // tk
    dims = {"nn": ((1,), (0,)), "nt": ((1,), (1,)), "tn": ((0,), (0,))}[mode]

    def body(a_ref, b_ref, *rest):
        o_ref = rest[-2] if nk > 1 else rest[-1]
        prod = lax.dot_general(a_ref[...].astype(BF16), b_ref[...].astype(BF16), (dims, ((), ())),
                               preferred_element_type=F32)
        if nk == 1:
            o_ref[...] = prod.astype(o_ref.dtype)
            return
        acc_ref = rest[-1]
        kk = pl.program_id(2)

        @pl.when(kk == 0)
        def _():
            acc_ref[...] = prod

        @pl.when(kk > 0)
        def _():
            acc_ref[...] += prod

        @pl.when(kk == nk - 1)
        def _():
            o_ref[...] = acc_ref[...].astype(o_ref.dtype)

    if mode == "tn":
        a_spec = pl.BlockSpec((tk, tm), lambda i, j, kk: (kk, a_off // tm + i))
    else:
        a_spec = pl.BlockSpec((tm, tk), lambda i, j, kk: (i, a_off // tk + kk))
    if mode == "nt":
        b_spec = pl.BlockSpec((tn, tk), lambda i, j, kk: (j, kk))
    else:
        b_spec = pl.BlockSpec((tk, tn), lambda i, j, kk: (kk, j))
    in_specs, operands, aliases = [a_spec, b_spec], [a, b], {}
    if into is None:
        out_spec = pl.BlockSpec((tm, tn), lambda i, j, kk: (i, j))
        out_shape = jax.ShapeDtypeStruct((m, n), out_dtype)
    else:
        buf, slab, count = into
        out_spec = pl.BlockSpec((None, tm, tn), lambda i, j, kk: (slab, i, j))
        out_shape = jax.ShapeDtypeStruct((count, m, n), out_dtype)
        if buf is not None:
            in_specs.append(pl.BlockSpec(memory_space=pl.ANY))
            operands.append(buf)
            aliases = {2: 0}
    return pl.pallas_call(
        body, name=name, grid=(m // tm, n // tn, nk), in_specs=in_specs, out_specs=out_spec, out_shape=out_shape,
        scratch_shapes=[pltpu.VMEM((tm, tn), F32)] if nk > 1 else [], input_output_aliases=aliases,
        compiler_params=_params(dimension_semantics=("arbitrary", "arbitrary", "arbitrary")),
    )(*operands)


def _small_mm(name, a, b, mode):
    dims = {"nn": ((1,), (0,)), "tn": ((0,), (0,))}[mode]
    m = a.shape[0] if mode == "nn" else a.shape[1]

    def body(a_ref, b_ref, o_ref):
        o_ref[...] = lax.dot_general(a_ref[...], b_ref[...], (dims, ((), ())), precision=lax.Precision.HIGHEST,
                                     preferred_element_type=F32)

    return pl.pallas_call(body, name=name, out_shape=jax.ShapeDtypeStruct((m, b.shape[1]), F32),
                          compiler_params=_params())(a, b)


def _norm(x):
    mu = jnp.mean(x, axis=-1, keepdims=True)
    xc = x - mu
    rstd = lax.rsqrt(jnp.mean(xc * xc, axis=-1, keepdims=True) + LN_EPS)
    return xc * rstd, rstd


def _norm_bwd(dn, n, rstd):
    return rstd * (dn - jnp.mean(dn, axis=-1, keepdims=True) - n * jnp.mean(dn * n, axis=-1, keepdims=True))


def _colsum(v):
    return jnp.sum(v, axis=0, keepdims=True)


def _gelu(x):
    return 0.5 * x * (1.0 + jnp.tanh(GELU_K * (x + GELU_C * x * x * x)))


def _gelu_grad(x):
    t = jnp.tanh(GELU_K * (x + GELU_C * x * x * x))
    return 0.5 * (1.0 + t) + 0.5 * x * (1.0 - t * t) * GELU_K * (1.0 + 3.0 * GELU_C * x * x)


def _log_sigmoid_parts(z):
    lb = jnp.minimum(z, 0.0) - jnp.log(1.0 + jnp.exp(-jnp.abs(z)))
    return lb, lb - z


def _qkv_prep(proj, t):
    s = proj.shape[0]
    nb, nhp = s // t, 512 // LANES

    def body(q_ref, k_ref, v_ref, qs_ref, kb_ref, vb_ref, kt_ref, vt_ref):
        qs_ref[...] = (q_ref[...] * (1.0 / math.sqrt(HEAD_DIM))).astype(BF16)
        k, v = k_ref[...], v_ref[...]
        kb_ref[...] = k.astype(BF16)
        vb_ref[...] = v.astype(BF16)
        for hp in range(nhp):
            kt_ref[hp, 0] = k[:, hp * LANES:(hp + 1) * LANES].T.astype(BF16)
            vt_ref[hp, 0] = v[:, hp * LANES:(hp + 1) * LANES].T.astype(BF16)

    col = lambda cb: pl.BlockSpec((t, 512), lambda i, cb=cb: (i, cb))
    row_out = pl.BlockSpec((t, 512), lambda i: (i, 0))
    t_out = pl.BlockSpec((nhp, 1, LANES, t), lambda i: (0, i, 0, 0))
    return pl.pallas_call(
        body, name="qkv_prep", grid=(nb,), in_specs=[col(0), col(1), col(2)],
        out_specs=[row_out, row_out, row_out, t_out, t_out],
        out_shape=[jax.ShapeDtypeStruct((s, 512), BF16)] * 3 + [jax.ShapeDtypeStruct((nhp, nb, LANES, t), BF16)] * 2,
        compiler_params=_params(dimension_semantics=("arbitrary",)),
    )(proj, proj, proj)


def _tile_masks(t):
    row = lax.broadcasted_iota(jnp.int32, (t, t), 0)
    col = lax.broadcasted_iota(jnp.int32, (t, t), 1)
    return row, col


def _walk_tiles(i, tiles, state, descending):
    def pair(p, st):
        j = i - 1 - 2 * p if descending else 2 * p
        return tiles([j, j - 1 if descending else j + 1], st, False)

    def rest(st):
        return lax.cond(i % 2 == 1, lambda s_: tiles([0 if descending else i - 1], s_, False), lambda s_: s_, st)

    if descending:
        return rest(lax.fori_loop(0, i // 2, pair, tiles([i], state, True)))
    return tiles([i], rest(lax.fori_loop(0, i // 2, pair, state)), True)


def _nt(a, b):
    return lax.dot_general(a, b, (((1,), (1,)), ((), ())), preferred_element_type=F32)


def _nn(a, b):
    return jnp.dot(a, b, preferred_element_type=F32)


def _attn_fwd(qs, k, vt3, t):
    s = qs.shape[0]
    nb, nhp = s // t, qs.shape[1] // LANES

    def body(q_ref, k_ref, vt_ref, o_ref, car_ref):
        i = pl.program_id(1)
        q2 = q_ref[...]
        lane_q = lax.broadcasted_iota(jnp.int32, q2.shape, 1)
        row, col = _tile_masks(t)
        later = (col > row).astype(BF16)
        valid = row < col
        orow = lax.broadcasted_iota(jnp.int32, (LANES, t), 0)
        car_ref[...] = jnp.zeros_like(car_ref)
        qh = [jnp.where((lane_q < HEAD_DIM) == (hh == 0), q2, jnp.zeros_like(q2)) for hh in range(2)]

        def tiles(js, state, masked):
            chains = [(n, hh) for n in range(len(js)) for hh in range(2)]
            kb = [k_ref[pl.ds(pl.multiple_of(j * t, t), t), :] for j in js]
            z = {ch: _nt(kb[ch[0]], qh[ch[1]]) for ch in chains}
            lb, aft, csum = {}, {}, {}
            for ch in chains:
                lb[ch], l1m = _log_sigmoid_parts(z[ch])
                if masked:
                    l1m = jnp.where(valid, l1m, 0.0)
                aft[ch] = _nn(later, l1m.astype(BF16))
                csum[ch] = _colsum(l1m)
            state = list(state)
            for ch in chains:
                n, hh = ch
                c_after, acc = state[hh]
                w = jnp.exp(lb[ch] + aft[ch] + c_after)
                if masked:
                    w = jnp.where(valid, w, 0.0)
                car_ref[hh, pl.ds(js[n], 1), :] = c_after
                state[hh] = (c_after + csum[ch], acc + _nn(vt_ref[0, js[n]], w.astype(BF16)))
            return tuple(state)

        zero = (jnp.zeros((1, t), F32), jnp.zeros((LANES, t), F32))
        (_, acc0), (_, acc1) = _walk_tiles(i, tiles, (zero, zero), descending=True)
        o_ref[...] = jnp.where(orow < HEAD_DIM, acc0, acc1).T.astype(o_ref.dtype)

    return pl.pallas_call(
        body, name="attn_fwd", grid=(nhp, nb),
        in_specs=[pl.BlockSpec((t, LANES), lambda hp, i: (i, hp)),
                  pl.BlockSpec((s, LANES), lambda hp, i: (0, hp)),
                  pl.BlockSpec((1, nb, LANES, t), lambda hp, i: (hp, 0, 0, 0))],
        out_specs=[pl.BlockSpec((t, LANES), lambda hp, i: (i, hp)),
                   pl.BlockSpec((2, nb, t), lambda hp, i: (hp, 0, i))],
        out_shape=[jax.ShapeDtypeStruct((s, nhp * LANES), BF16), jax.ShapeDtypeStruct((2 * nhp, nb, s), F32)],
        compiler_params=_params(dimension_semantics=("arbitrary", "arbitrary")),
    )(qs, k, vt3)


def _attn_bwd(qs, do, k, v, kt3, car, t):
    s = qs.shape[0]
    nb, nhp = s // t, qs.shape[1] // LANES

    def body(q_ref, do_ref, k_ref, v_ref, kt_ref, car_ref, dq_ref, dk_ref, dv_ref):
        i = pl.program_id(1)

        @pl.when(i == 0)
        def _():
            dk_ref[...] = jnp.zeros_like(dk_ref)
            dv_ref[...] = jnp.zeros_like(dv_ref)

        q2, do2 = q_ref[...], do_ref[...]
        lane_q = lax.broadcasted_iota(jnp.int32, q2.shape, 1)
        row, col = _tile_masks(t)
        later = (col > row).astype(BF16)
        earlier = (col < row).astype(BF16)
        valid = row < col
        orow = lax.broadcasted_iota(jnp.int32, (LANES, t), 0)
        head = [(lane_q < HEAD_DIM) == (hh == 0) for hh in range(2)]
        qh = [jnp.where(hm, q2, jnp.zeros_like(q2)) for hm in head]
        doh = [jnp.where(hm, do2, jnp.zeros_like(do2)) for hm in head]

        def tiles(js, state, masked):
            chains = [(n, hh) for n in range(len(js)) for hh in range(2)]
            rows = [pl.ds(pl.multiple_of(j * t, t), t) for j in js]
            kb = [k_ref[r, :] for r in rows]
            vb = [v_ref[r, :] for r in rows]
            z = {ch: _nt(kb[ch[0]], qh[ch[1]]) for ch in chains}
            dw = {ch: _nt(vb[ch[0]], doh[ch[1]]) for ch in chains}
            lb, beta, aft = {}, {}, {}
            for ch in chains:
                lb[ch], l1m = _log_sigmoid_parts(z[ch])
                beta[ch] = jnp.exp(lb[ch])
                if masked:
                    l1m = jnp.where(valid, l1m, 0.0)
                aft[ch] = _nn(later, l1m.astype(BF16))
            w, g, gsum, g_in = {}, {}, {}, {}
            for ch in chains:
                n, hh = ch
                w[ch] = jnp.exp(lb[ch] + aft[ch] + car_ref[hh, pl.ds(js[n], 1), :])
                if masked:
                    w[ch] = jnp.where(valid, w[ch], 0.0)
                g[ch] = dw[ch] * w[ch]
                g_in[ch] = _nn(earlier, g[ch].astype(BF16))
                gsum[ch] = _colsum(g[ch])
            state = list(state)
            dk_t, dv_t = [None] * len(js), [None] * len(js)
            for ch in chains:
                n, hh = ch
                c_g, dqt = state[hh]
                dz = g[ch] - beta[ch] * (g[ch] + g_in[ch] + c_g)
                if masked:
                    dz = jnp.where(valid, dz, 0.0)
                dzb, wb = dz.astype(BF16), w[ch].astype(BF16)
                dk_h, dv_h = _nn(dzb, qh[hh]), _nn(wb, doh[hh])
                dk_t[n] = dk_h if dk_t[n] is None else dk_t[n] + dk_h
                dv_t[n] = dv_h if dv_t[n] is None else dv_t[n] + dv_h
                state[hh] = (c_g + gsum[ch], dqt + _nn(kt_ref[0, js[n]], dzb))
            for n in range(len(js)):
                dk_ref[rows[n], :] += dk_t[n]
                dv_ref[rows[n], :] += dv_t[n]
            return tuple(state)

        zero = (jnp.zeros((1, t), F32), jnp.zeros((LANES, t), F32))
        (_, dq0), (_, dq1) = _walk_tiles(i, tiles, (zero, zero), descending=False)
        dq_ref[...] = jnp.where(orow < HEAD_DIM, dq0, dq1).T

    tile_spec = pl.BlockSpec((t, LANES), lambda hp, i: (i, hp))
    whole = pl.BlockSpec((s, LANES), lambda hp, i: (0, hp))
    return pl.pallas_call(
        body, name="attn_bwd", grid=(nhp, nb),
        in_specs=[tile_spec, tile_spec, whole, whole,
                  pl.BlockSpec((1, nb, LANES, t), lambda hp, i: (hp, 0, 0, 0)),
                  pl.BlockSpec((2, nb, t), lambda hp, i: (hp, 0, i))],
        out_specs=[tile_spec, whole, whole],
        out_shape=[jax.ShapeDtypeStruct((s, nhp * LANES), F32)] * 3,
        compiler_params=_params(dimension_semantics=("arbitrary", "arbitrary")),
    )(qs, do, k, v, kt3, car)


SCAN_LANES = 2048
SCAN_ROWS = 8


def _scan_chunks(v):
    n = v.shape[1] // (2 * LANES)
    return [(v[:, c * 2 * LANES:c * 2 * LANES + LANES], v[:, c * 2 * LANES + LANES:(c + 1) * 2 * LANES]) for c in range(n)]


def _ssm_scan_fwd(bu, lam):
    s, w = bu.shape
    tt = _pick(s, (512, 256, 128, 8))
    nt = s // tt

    def body(x_ref, lam_ref, h_ref, st_ref):
        @pl.when(pl.program_id(1) == 0)
        def _():
            st_ref[...] = jnp.zeros_like(st_ref)

        lam_c = _scan_chunks(lam_ref[...])

        def tile(it, state):
            r0 = pl.multiple_of(it * SCAN_ROWS, SCAN_ROWS)
            x_c = _scan_chunks(x_ref[pl.ds(r0, SCAN_ROWS), :])
            state = list(state)
            out_rows = []
            for r in range(SCAN_ROWS):
                parts = []
                for c, ((lr, li), (xr, xi)) in enumerate(zip(lam_c, x_c)):
                    hr, hi = state[2 * c], state[2 * c + 1]
                    nhr = lr * hr - li * hi + xr[r:r + 1]
                    nhi = lr * hi + li * hr + xi[r:r + 1]
                    state[2 * c], state[2 * c + 1] = nhr, nhi
                    parts += [nhr, nhi]
                out_rows.append(jnp.concatenate(parts, axis=1))
            h_ref[pl.ds(r0, SCAN_ROWS), :] = jnp.concatenate(out_rows, axis=0)
            return tuple(state)

        st = st_ref[0:1, :]
        init = tuple(st[:, c * LANES:(c + 1) * LANES] for c in range(SCAN_LANES // LANES))
        fin = lax.fori_loop(0, tt // SCAN_ROWS, tile, init)
        st_ref[0:1, :] = jnp.concatenate(fin, axis=1)

    return pl.pallas_call(
        body, name="ssm_scan_fwd", grid=(w // SCAN_LANES, nt),
        in_specs=[pl.BlockSpec((tt, SCAN_LANES), lambda lc, i: (i, lc)),
                  pl.BlockSpec((1, SCAN_LANES), lambda lc, i: (0, lc))],
        out_specs=pl.BlockSpec((tt, SCAN_LANES), lambda lc, i: (i, lc)),
        out_shape=jax.ShapeDtypeStruct((s, w), F32),
        scratch_shapes=[pltpu.VMEM((SCAN_ROWS, SCAN_LANES), F32)],
        compiler_params=_params(dimension_semantics=("arbitrary", "arbitrary")),
    )(bu, lam)


def _ssm_scan_bwd(e, h, lam):
    s, w = e.shape
    tt = _pick(s, (512, 256, 128, 8))
    nt = s // tt

    def body(e_ref, h_ref, lam_ref, a_ref, dlam_ref, st_ref):
        @pl.when(pl.program_id(1) == 0)
        def _():
            st_ref[...] = jnp.zeros_like(st_ref)
            dlam_ref[...] = jnp.zeros_like(dlam_ref)

        lam_c = _scan_chunks(lam_ref[...])
        nch = len(lam_c)

        def tile(it, carry):
            r0 = pl.multiple_of((tt // SCAN_ROWS - 1 - it) * SCAN_ROWS, SCAN_ROWS)
            e_c = _scan_chunks(e_ref[pl.ds(r0, SCAN_ROWS), :])
            h_c = _scan_chunks(h_ref[pl.ds(r0, SCAN_ROWS), :])
            carry = list(carry)
            out_rows = [None] * SCAN_ROWS
            for r in reversed(range(SCAN_ROWS)):
                parts = []
                for c in range(nch):
                    (lr, li), (er, ei), (hr, hi) = lam_c[c], e_c[c], h_c[c]
                    ar, ai, dr, di = carry[4 * c:4 * c + 4]
                    hr, hi = hr[r:r + 1], hi[r:r + 1]
                    dr = dr + ar * hr + ai * hi
                    di = di + ai * hr - ar * hi
                    nar = lr * ar + li * ai + er[r:r + 1]
                    nai = lr * ai - li * ar + ei[r:r + 1]
                    carry[4 * c:4 * c + 4] = [nar, nai, dr, di]
                    parts += [nar, nai]
                out_rows[r] = jnp.concatenate(parts, axis=1)
            a_ref[pl.ds(r0, SCAN_ROWS), :] = jnp.concatenate(out_rows, axis=0)
            return tuple(carry)

        st, dl = st_ref[0:1, :], dlam_ref[0:1, :]
        init = []
        for c in range(nch):
            lo = c * 2 * LANES
            init += [st[:, lo:lo + LANES], st[:, lo + LANES:lo + 2 * LANES],
                     dl[:, lo:lo + LANES], dl[:, lo + LANES:lo + 2 * LANES]]
        fin = lax.fori_loop(0, tt // SCAN_ROWS, tile, tuple(init))
        st_ref[0:1, :] = jnp.concatenate([fin[4 * c + q] for c in range(nch) for q in (0, 1)], axis=1)
        dlam_ref[0:1, :] = jnp.concatenate([fin[4 * c + q] for c in range(nch) for q in (2, 3)], axis=1)

    rev = pl.BlockSpec((tt, SCAN_LANES), lambda lc, i: (nt - 1 - i, lc))
    vec = pl.BlockSpec((1, SCAN_LANES), lambda lc, i: (0, lc))
    return pl.pallas_call(
        body, name="ssm_scan_bwd", grid=(w // SCAN_LANES, nt),
        in_specs=[rev, rev, vec], out_specs=[rev, pl.BlockSpec((SCAN_ROWS, SCAN_LANES), lambda lc, i: (0, lc))],
        out_shape=[jax.ShapeDtypeStruct((s, w), F32), jax.ShapeDtypeStruct((SCAN_ROWS, w), F32)],
        scratch_shapes=[pltpu.VMEM((SCAN_ROWS, SCAN_LANES), F32)],
        compiler_params=_params(dimension_semantics=("arbitrary", "arbitrary")),
    )(e, h, lam)


def _ssm_params_fwd(a_re, a_im, log_dt, b_re, b_im):
    def body(ar_ref, ai_ref, ldt_ref, br_ref, bi_ref, lr_ref, li_ref, bbr_ref, bbi_ref):
        ar, ai, dt = ar_ref[...], ai_ref[...], jnp.exp(ldt_ref[...])
        mag = jnp.exp(ar * dt)
        lr, li = mag * jnp.cos(ai * dt), mag * jnp.sin(ai * dt)
        den = ar * ar + ai * ai
        cr = ((lr - 1.0) * ar + li * ai) / den
        ci = (li * ar - (lr - 1.0) * ai) / den
        br, bi = br_ref[...], bi_ref[...]
        lr_ref[...], li_ref[...] = lr, li
        bbr_ref[...] = cr * br - ci * bi
        bbi_ref[...] = cr * bi + ci * br

    n = a_re.shape[0]
    v1, v16 = jax.ShapeDtypeStruct((n, 1), F32), jax.ShapeDtypeStruct((n, SSM_GROUP), F32)
    return pl.pallas_call(body, name="ssm_params_fwd", out_shape=[v1, v1, v16, v16],
                          compiler_params=_params())(a_re, a_im, log_dt, b_re, b_im)


def _ssm_params_bwd(a_re, a_im, log_dt, b_re, b_im, g_lr, g_li, g_bbr, g_bbi):
    n = a_re.shape[0]

    def body(ar_ref, ai_ref, ldt_ref, br_ref, bi_ref, glr_ref, gli_ref, gbr_ref, gbi_ref,
             dar_ref, dai_ref, dldt_ref, dbr_ref, dbi_ref):
        ar, ai, dt = ar_ref[...], ai_ref[...], jnp.exp(ldt_ref[...])
        mag = jnp.exp(ar * dt)
        lr, li = mag * jnp.cos(ai * dt), mag * jnp.sin(ai * dt)
        den = ar * ar + ai * ai
        cr = ((lr - 1.0) * ar + li * ai) / den
        ci = (li * ar - (lr - 1.0) * ai) / den
        br, bi, gbr, gbi = br_ref[...], bi_ref[...], gbr_ref[...], gbi_ref[...]
        dbr_ref[...] = gbr * cr + gbi * ci
        dbi_ref[...] = gbi * cr - gbr * ci
        gcr = jnp.sum(gbr * br + gbi * bi, axis=1, keepdims=True)
        gci = jnp.sum(gbi * br - gbr * bi, axis=1, keepdims=True)
        ir, ii = ar / den, -ai / den
        glr = glr_ref[...] + gcr * ir + gci * ii
        gli = gli_ref[...] + gci * ir - gcr * ii
        qr, qi = cr * ir - ci * ii, cr * ii + ci * ir
        gar = -(gcr * qr + gci * qi)
        gai = -(gci * qr - gcr * qi)
        gxr = glr * lr + gli * li
        gxi = gli * lr - glr * li
        dar_ref[...] = gar + gxr * dt
        dai_ref[...] = gai + gxi * dt
        gdt = (gxr * ar + gxi * ai) * dt
        rowg = lax.broadcasted_iota(jnp.int32, (n, SSM_GROUPS), 0) // SSM_STATE
        colg = lax.broadcasted_iota(jnp.int32, (n, SSM_GROUPS), 1)
        dldt_ref[...] = jnp.sum(jnp.where(rowg == colg, gdt, 0.0), axis=0, keepdims=True)

    v1, v16 = jax.ShapeDtypeStruct((n, 1), F32), jax.ShapeDtypeStruct((n, SSM_GROUP), F32)
    return pl.pallas_call(body, name="ssm_params_bwd",
                          out_shape=[v1, v1, jax.ShapeDtypeStruct((1, SSM_GROUPS), F32), v16, v16],
                          compiler_params=_params())(a_re, a_im, log_dt, b_re, b_im, g_lr, g_li, g_bbr, g_bbi)


def _interleave(re, im, axis):
    shp = list(re.shape)
    new = shp[:axis] + [shp[axis] // LANES, LANES] + shp[axis + 1:]
    st = jnp.stack([re.reshape(new), im.reshape(new)], axis=axis + 1)
    return st.reshape(shp[:axis] + [2 * shp[axis]] + shp[axis + 1:])


def _deinterleave(v, axis):
    shp = list(v.shape)
    r = v.reshape(shp[:axis] + [shp[axis] // (2 * LANES), 2, LANES] + shp[axis + 1:])
    out = shp[:axis] + [shp[axis] // 2] + shp[axis + 1:]
    return (lax.index_in_dim(r, 0, axis + 1, keepdims=False).reshape(out),
            lax.index_in_dim(r, 1, axis + 1, keepdims=False).reshape(out))


def _b_matrix(bbr, bbi):
    eye = jnp.eye(SSM_GROUPS, dtype=F32)

    def blockdiag(v):
        x = v.reshape(SSM_GROUPS, SSM_STATE, SSM_GROUP).transpose(0, 2, 1)
        return (eye[:, None, :, None] * x[:, :, None, :]).reshape(SSM_GROUPS * SSM_GROUP, N_STATE)

    return _interleave(blockdiag(bbr), blockdiag(bbi), 1)


def _diag_blocks(v, rows, cols):
    return jnp.stack([v[g * rows:(g + 1) * rows, g * cols:(g + 1) * cols] for g in range(SSM_GROUPS)])


def _b_matrix_grad(d):
    def diag(v):
        return _diag_blocks(v, SSM_GROUP, SSM_STATE).transpose(0, 2, 1).reshape(N_STATE, SSM_GROUP)

    dr, di = _deinterleave(d, 1)
    return diag(dr), diag(di)


def _c_matrix(c_re, c_im):
    eye = jnp.eye(SSM_GROUPS, dtype=F32)

    def blockdiag(v):
        x = v.transpose(0, 2, 1)
        return (x[:, :, None, :] * eye[:, None, :, None]).reshape(N_STATE, SSM_GROUPS * SSM_GROUP)

    return _interleave(blockdiag(c_re), blockdiag(-c_im), 0)


def _c_matrix_grad(d):
    def diag(v):
        return _diag_blocks(v, SSM_STATE, SSM_GROUP).transpose(0, 2, 1)

    dr, di = _deinterleave(d, 0)
    return diag(dr), -diag(di)


def _row(v):
    return v.reshape(1, -1)


def _ssm_inputs(p):
    rows = lambda v: v.reshape(N_STATE, -1)
    ldt = jnp.repeat(p["ssm_log_dt"], SSM_STATE).reshape(N_STATE, 1)
    return rows(p["ssm_a_re"]), rows(p["ssm_a_im"]), ldt, rows(p["ssm_b_re"]), rows(p["ssm_b_im"])


def _layer_fwd(x, mod, p, tag):
    d = x.shape[1]
    sh_m, sc_m, g_m, sh_f, sc_f, g_f = [_row(mod[i]) for i in range(6)]
    nm = lambda s: f"{s}_{tag}"

    def lnmod(x, sc, sh):
        return _norm(x)[0] * (1.0 + sc) + sh

    h1 = _rowwise(nm("lnmod1"), lnmod, [(x, 0, d)], [sc_m, sh_m], [(d, BF16)])
    proj = _mm(nm("proj"), h1, p["w_in"], "nn")
    t = min(ATT_TILE, x.shape[0])
    qs, kb, vb, kt3, vt3 = _qkv_prep(proj, t)
    att, car = _attn_fwd(qs, kb, vt3, t)
    y_sb = _mm(nm("sb_up"), att, p["w_sb_up"], "nn")

    lam_r, lam_i, bbr, bbi = _ssm_params_fwd(*_ssm_inputs(p))
    lam = _interleave(lam_r.reshape(1, N_STATE), lam_i.reshape(1, N_STATE), 1)
    bmat = _b_matrix(bbr, bbi).astype(BF16)
    cmat = _c_matrix(p["ssm_c_re"], p["ssm_c_im"]).astype(BF16)
    bu = _mm(nm("ssm_b"), proj, bmat, "nn", a_cols=(1536, 512))
    hst = _ssm_scan_fwd(bu, lam)
    yc = _mm(nm("ssm_c"), hst, cmat, "nn")

    def ssm_act(yc, u, dsk):
        y0 = yc + dsk * u
        return y0, _gelu(y0)

    y0, y1 = _rowwise(nm("ssm_act"), ssm_act, [(yc, 0, 512), (proj, 3, 512)], [_row(p["ssm_d"])], [(512, F32), (512, F32)])
    gl = _mm(nm("glu"), y1, p["w_glu"], "nn")
    y2 = _rowwise(nm("glu_act"), lambda y1, gl, b: y1 * jax.nn.sigmoid(gl + b), [(y1, 0, 512), (gl, 0, 512)],
                  [_row(p["b_glu"])], [(512, BF16)])
    y_ssm = _mm(nm("ssm_up"), y2, p["w_ssm_up"], "nn")

    def merge(gsb, gss, ysb, yss):
        return jax.nn.sigmoid(gsb) * ysb + jax.nn.sigmoid(gss) * yss

    merged = _rowwise(nm("merge"), merge, [(proj, 2, d), (proj, 3, d), (y_sb, 0, d), (y_ssm, 0, d)], [], [(d, BF16)])
    y = _mm(nm("out"), merged, p["w_out"], "nn")

    def resid_ln(x, y, g, lg, lb):
        return _norm(ALPHA * x + (1.0 + g) * y)[0] * lg + lb

    x1 = _rowwise(nm("ln1"), resid_ln, [(x, 0, d), (y, 0, d)], [g_m, _row(p["ln1_g"]), _row(p["ln1_b"])], [(d, F32)])
    h2 = _rowwise(nm("lnmod2"), lnmod, [(x1, 0, d)], [sc_f, sh_f], [(d, BF16)])
    f = _mm(nm("ffn_in"), h2, p["w_ffn_in"], "nn", out_dtype=BF16)
    fh = f.shape[1] // 2

    def swiglu(g, u):
        g = g.astype(F32)
        return g * jax.nn.sigmoid(g) * u.astype(F32)

    act = _rowwise(nm("swiglu"), swiglu, [(f, 0, fh), (f, 1, fh)], [], [(fh, BF16)])
    yf = _mm(nm("ffn_out"), act, p["w_ffn_out"], "nn")
    x2 = _rowwise(nm("ln2"), resid_ln, [(x1, 0, d), (yf, 0, d)], [g_f, _row(p["ln2_g"]), _row(p["ln2_b"])], [(d, F32)])
    saved = dict(x=x, h1=h1, proj=proj, qs=qs, kb=kb, vb=vb, kt3=kt3, car=car, att=att, y_sb=y_sb, lam=lam, bmat=bmat,
                 cmat=cmat, hst=hst, y0=y0, y1=y1, gl=gl, y2=y2, y_ssm=y_ssm, merged=merged, y=y, x1=x1, h2=h2, f=f,
                 act=act, yf=yf, t=t)
    return x2, saved


def _layer_bwd(dx2, mod, p, sv, layer, depth, stacked):
    d = dx2.shape[1]
    sh_m, sc_m, g_m, sh_f, sc_f, g_f = [_row(mod[i]) for i in range(6)]
    nm = lambda s: f"{s}_{layer}"
    grads = {}

    def weight_grad(n, a, b, **kw):
        grads[n] = _mm(nm("d" + n), a, b, "tn", out_dtype=BF16, into=(stacked.get(n), layer, depth), **kw)

    def resid_ln_bwd(x, y, dxo, g, lg):
        n, rstd = _norm(ALPHA * x + (1.0 + g) * y)
        dr = _norm_bwd(dxo * lg, n, rstd)
        return ALPHA * dr, (1.0 + g) * dr, _colsum(dxo * n), _colsum(dxo), _colsum(dr * y)

    def lnmod_bwd(x, dh, dxa, sc):
        n, rstd = _norm(x)
        return dxa + _norm_bwd(dh * (1.0 + sc), n, rstd), _colsum(dh * n), _colsum(dh)

    dx1a, dyf, grads["ln2_g"], grads["ln2_b"], dg_f = _rowwise(
        nm("ln2_bwd"), resid_ln_bwd, [(sv["x1"], 0, d), (sv["yf"], 0, d), (dx2, 0, d)], [g_f, _row(p["ln2_g"])],
        [(d, F32), (d, BF16)], [d, d, d])
    dact = _mm(nm("d_act"), dyf, p["w_ffn_out"], "nt")
    weight_grad("w_ffn_out", sv["act"], dyf)
    fh = sv["f"].shape[1] // 2

    def swiglu_bwd(g, u, da):
        g, u = g.astype(F32), u.astype(F32)
        sg = jax.nn.sigmoid(g)
        return jnp.concatenate([da * u * sg * (1.0 + g * (1.0 - sg)), da * g * sg], axis=1)

    df = _rowwise(nm("swiglu_bwd"), swiglu_bwd, [(sv["f"], 0, fh), (sv["f"], 1, fh), (dact, 0, fh)], [], [(2 * fh, BF16)])
    dh2 = _mm(nm("d_h2"), df, p["w_ffn_in"], "nt")
    weight_grad("w_ffn_in", sv["h2"], df)
    dx1, dsc_f, dsh_f = _rowwise(nm("lnmod2_bwd"), lnmod_bwd, [(sv["x1"], 0, d), (dh2, 0, d), (dx1a, 0, d)], [sc_f],
                                 [(d, F32)], [d, d])
    dxa, dy, grads["ln1_g"], grads["ln1_b"], dg_m = _rowwise(
        nm("ln1_bwd"), resid_ln_bwd, [(sv["x"], 0, d), (sv["y"], 0, d), (dx1, 0, d)], [g_m, _row(p["ln1_g"])],
        [(d, F32), (d, BF16)], [d, d, d])
    dmerged = _mm(nm("d_merged"), dy, p["w_out"], "nt")
    weight_grad("w_out", sv["merged"], dy)

    def merge_bwd(gsb, gss, ysb, yss, dm):
        s1, s2 = jax.nn.sigmoid(gsb), jax.nn.sigmoid(gss)
        return s1 * dm, s2 * dm, dm * ysb * s1 * (1.0 - s1), dm * yss * s2 * (1.0 - s2)

    dy_sb, dy_ssm, dg_sb, dg_ssm = _rowwise(
        nm("merge_bwd"), merge_bwd, [(sv["proj"], 2, d), (sv["proj"], 3, d), (sv["y_sb"], 0, d), (sv["y_ssm"], 0, d),
                                     (dmerged, 0, d)], [], [(d, BF16)] * 4)
    dy2 = _mm(nm("d_y2"), dy_ssm, p["w_ssm_up"], "nt")
    weight_grad("w_ssm_up", sv["y2"], dy_ssm)

    def glu_act_bwd(y1, gl, dy2, b):
        sg = jax.nn.sigmoid(gl + b)
        dgl = dy2 * y1 * sg * (1.0 - sg)
        return dy2 * sg, dgl, _colsum(dgl)

    dy1a, dgl, grads["b_glu"] = _rowwise(nm("glu_act_bwd"), glu_act_bwd, [(sv["y1"], 0, 512), (sv["gl"], 0, 512), (dy2, 0, 512)],
                                         [_row(p["b_glu"])], [(512, F32), (512, BF16)], [512])
    dy1b = _mm(nm("d_y1"), dgl, p["w_glu"], "nt")
    weight_grad("w_glu", sv["y1"], dgl)

    def ssm_act_bwd(y0, u, dy1a, dy1b, dsk):
        dy0 = (dy1a + dy1b) * _gelu_grad(y0)
        return dy0, dsk * dy0, _colsum(dy0 * u)

    dy0, du_a, grads["ssm_d"] = _rowwise(nm("ssm_act_bwd"), ssm_act_bwd,
                                         [(sv["y0"], 0, 512), (sv["proj"], 3, 512), (dy1a, 0, 512), (dy1b, 0, 512)],
                                         [_row(p["ssm_d"])], [(512, BF16), (512, F32)], [512])
    e = _mm(nm("ssm_e"), dy0, sv["cmat"], "nt")
    grads["ssm_c_re"], grads["ssm_c_im"] = _c_matrix_grad(_mm(nm("dw_ssm_c"), sv["hst"], dy0, "tn"))
    adj, dlam = _ssm_scan_bwd(e, sv["hst"], sv["lam"])
    du_b = _mm(nm("d_u"), adj, sv["bmat"], "nt")
    g_bbr, g_bbi = _b_matrix_grad(_mm(nm("dw_ssm_b"), sv["proj"], adj, "tn", a_cols=(1536, 512)))
    g_lr, g_li = _deinterleave(dlam[0:1], 1)
    da_re, da_im, dldt, db_re, db_im = _ssm_params_bwd(*_ssm_inputs(p), g_lr.reshape(N_STATE, 1), g_li.reshape(N_STATE, 1),
                                                       g_bbr, g_bbi)
    grads["ssm_a_re"] = da_re.reshape(SSM_GROUPS, SSM_STATE)
    grads["ssm_a_im"] = da_im.reshape(SSM_GROUPS, SSM_STATE)
    grads["ssm_log_dt"] = dldt.reshape(SSM_GROUPS)
    grads["ssm_b_re"] = db_re.reshape(SSM_GROUPS, SSM_STATE, SSM_GROUP)
    grads["ssm_b_im"] = db_im.reshape(SSM_GROUPS, SSM_STATE, SSM_GROUP)
    datt = _mm(nm("d_att"), dy_sb, p["w_sb_up"], "nt", out_dtype=BF16)
    weight_grad("w_sb_up", sv["att"], dy_sb)
    dqs, dk, dv = _attn_bwd(sv["qs"], datt, sv["kb"], sv["vb"], sv["kt3"], sv["car"], sv["t"])

    def dproj_cols(dqs, dk, dv, dua, dub, dgsb, dgss):
        return jnp.concatenate([dqs * (1.0 / math.sqrt(HEAD_DIM)), dk, dv, dua + dub, dgsb.astype(F32), dgss.astype(F32)],
                               axis=1)

    dproj = _rowwise(nm("dproj"), dproj_cols, [(dqs, 0, 512), (dk, 0, 512), (dv, 0, 512), (du_a, 0, 512), (du_b, 0, 512),
                                               (dg_sb, 0, d), (dg_ssm, 0, d)], [], [(2048 + 2 * d, BF16)])
    dh1 = _mm(nm("d_h1"), dproj, p["w_in"], "nt")
    weight_grad("w_in", sv["h1"], dproj)
    dx, dsc_m, dsh_m = _rowwise(nm("lnmod1_bwd"), lnmod_bwd, [(sv["x"], 0, d), (dh1, 0, d), (dxa, 0, d)], [sc_m],
                                [(d, F32)], [d, d])
    for k in ("ln1_g", "ln1_b", "ln2_g", "ln2_b", "ssm_d", "b_glu"):
        grads[k] = grads[k].reshape(-1)
    dmod = jnp.concatenate([dsh_m, dsc_m, dg_m, dsh_f, dsc_f, dg_f], axis=0)
    return dx, dmod, grads


def _loss_head(x, target):
    d = x.shape[1]

    def fn(x, tgt):
        err = x - tgt
        return err * (1.0 / d), _colsum(err * err) * (0.5 / d)

    return _rowwise("loss_head", fn, [(x, 0, d), (target, 0, d)], [], [(d, F32)], [d])


def _place():
    return lax.axis_index("x"), lax.axis_index("y"), lax.axis_index("c")


def _all_gather8(name, block):
    m_per, n = block.shape

    def body(x_ref, out_ref, send_sems, recv_sems, local_sem):
        x, y, c = _place()
        me, sibling = (x, y, c), (x, y, 1 - c)
        chips = [(1 - x, y), (x, 1 - y), (1 - x, 1 - y)]

        def rows(px, py, pc):
            return out_ref.at[pl.ds(pl.multiple_of((4 * px + 2 * py + pc) * m_per, 8), m_per), :]

        def copy(k, blk, to, src=None):
            return pltpu.make_async_remote_copy(src_ref=rows(*blk) if src is None else src, dst_ref=rows(*blk),
                                                send_sem=send_sems.at[k], recv_sem=recv_sems.at[k],
                                                device_id=to, device_id_type=MESH)

        mine = pltpu.make_async_copy(x_ref, rows(*me), local_sem)
        mine.start()
        first = [copy(0, me, sibling, src=x_ref)] + [copy(1 + j, me, (*chip, c), src=x_ref) for j, chip in enumerate(chips)]
        for cp in first:
            cp.start()
        passed = [copy(4 + j, (*chip, c), sibling) for j, chip in enumerate(chips)]
        for j, chip in enumerate(chips):
            copy(1 + j, (*chip, c), me).wait_recv()
            passed[j].start()
        copy(0, sibling, me).wait_recv()
        for j, chip in enumerate(chips):
            copy(4 + j, (*chip, 1 - c), me).wait_recv()
        for cp in first + passed:
            cp.wait_send()
        mine.wait()

    return pl.pallas_call(
        body, name=name, out_shape=jax.ShapeDtypeStruct((8 * m_per, n), block.dtype),
        in_specs=[pl.BlockSpec(memory_space=pltpu.VMEM)], out_specs=pl.BlockSpec(memory_space=pltpu.VMEM),
        scratch_shapes=[pltpu.SemaphoreType.DMA((7,)), pltpu.SemaphoreType.DMA((7,)), pltpu.SemaphoreType.DMA],
        compiler_params=_params(),
    )(block)


def _other_chips(x, y):
    return [(1 - x, y), (x, 1 - y), (1 - x, 1 - y)]


def _gather_weights(shards, by_rows):
    n = len(shards)

    def body(*refs):
        src, dst = refs[:n], refs[n:2 * n]
        send_sems, recv_sems, local_sems = refs[2 * n:]
        x, y, c = _place()

        def block(k, px, py):
            _, r, cols = shards[k].shape
            q = 2 * px + py
            if by_rows[k]:
                return dst[k].at[:, pl.ds(pl.multiple_of(q * r, 16), r), :]
            return dst[k].at[:, :, pl.ds(pl.multiple_of(q * cols, LANES), cols)]

        local = [pltpu.make_async_copy(src[k], block(k, x, y), local_sems.at[k]) for k in range(n)]
        for cp in local:
            cp.start()
        sends = []
        for k in range(n):
            for j, (px, py) in enumerate(_other_chips(x, y)):
                cp = pltpu.make_async_remote_copy(src_ref=src[k], dst_ref=block(k, x, y), send_sem=send_sems.at[k, j],
                                                  recv_sem=recv_sems.at[k, j], device_id=(px, py, c), device_id_type=MESH)
                cp.start()
                sends.append(cp)
        for k in range(n):
            for j, (px, py) in enumerate(_other_chips(x, y)):
                pltpu.make_async_remote_copy(src_ref=src[k], dst_ref=block(k, px, py), send_sem=send_sems.at[k, j],
                                             recv_sem=recv_sems.at[k, j], device_id=(px, py, c),
                                             device_id_type=MESH).wait_recv()
        for cp in sends:
            cp.wait_send()
        for cp in local:
            cp.wait()

    def whole(s, rows):
        l, r, cols = s.shape
        return jax.ShapeDtypeStruct((l, 4 * r, cols) if rows else (l, r, 4 * cols), s.dtype)

    any_spec = pl.BlockSpec(memory_space=pl.ANY)
    return pl.pallas_call(
        body, name="gather_weights", in_specs=[any_spec] * n, out_specs=[any_spec] * n,
        out_shape=[whole(s, rows) for s, rows in zip(shards, by_rows)],
        scratch_shapes=[pltpu.SemaphoreType.DMA((n, 3)), pltpu.SemaphoreType.DMA((n, 3)), pltpu.SemaphoreType.DMA((n,))],
        compiler_params=_params(),
    )(*shards)


def _peer(x, y, c, r):
    fx, fy, fc = (r >> 2) & 1, (r >> 1) & 1, r & 1
    return (x + fx - 2 * x * fx, y + fy - 2 * y * fy, c + fc - 2 * c * fc)


def _scatter_grads(grads, by_rows):
    n = len(grads)

    def half_shape(k):
        l, r, c = grads[k].shape
        return (l, r // 8, c) if by_rows[k] else (l, r // 2, c // 4)

    def body(*refs):
        src, dst = refs[:n], refs[n:2 * n]
        send_sems, recv_sems, local_sems = refs[2 * n:]
        x, y, c = _place()
        me = 4 * x + 2 * y + c

        def window(k, px, py, pc):
            _, hr, hc = half_shape(k)
            q = 2 * px + py
            if by_rows[k]:
                return src[k].at[:, pl.ds(pl.multiple_of((2 * q + pc) * hr, 16), hr), :]
            return src[k].at[:, pl.ds(pl.multiple_of(pc * hr, 16), hr), pl.ds(pl.multiple_of(q * hc, LANES), hc)]

        local = [pltpu.make_async_copy(window(k, x, y, c), dst[k].at[me], local_sems.at[k]) for k in range(n)]
        for cp in local:
            cp.start()
        sends = []
        for k in range(n):
            for r in range(1, 8):
                to = _peer(x, y, c, r)
                cp = pltpu.make_async_remote_copy(src_ref=window(k, *to), dst_ref=dst[k].at[me], send_sem=send_sems.at[k, r - 1],
                                                  recv_sem=recv_sems.at[k, r - 1], device_id=to, device_id_type=MESH)
                cp.start()
                sends.append(cp)
        for k in range(n):
            for r in range(1, 8):
                px, py, pc = _peer(x, y, c, r)
                pltpu.make_async_remote_copy(src_ref=window(k, x, y, c), dst_ref=dst[k].at[4 * px + 2 * py + pc],
                                             send_sem=send_sems.at[k, r - 1], recv_sem=recv_sems.at[k, r - 1],
                                             device_id=(px, py, pc), device_id_type=MESH).wait_recv()
        for cp in sends:
            cp.wait_send()
        for cp in local:
            cp.wait()

    any_spec = pl.BlockSpec(memory_space=pl.ANY)
    return pl.pallas_call(
        body, name="scatter_grads", in_specs=[any_spec] * n, out_specs=[any_spec] * n,
        out_shape=[jax.ShapeDtypeStruct((8, *half_shape(k)), grads[k].dtype) for k in range(n)],
        scratch_shapes=[pltpu.SemaphoreType.DMA((n, 7)), pltpu.SemaphoreType.DMA((n, 7)), pltpu.SemaphoreType.DMA((n,))],
        compiler_params=_params(),
    )(*grads)


def _sum_slots(name, parts):
    _, l, r, c = parts.shape
    tr = _pick(r, (256, 176, 128, 64, 32, 8))

    def body(p_ref, o_ref):
        acc = p_ref[0].astype(F32)
        for i in range(1, 8):
            acc = acc + p_ref[i].astype(F32)
        o_ref[...] = acc

    return pl.pallas_call(
        body, name=name, grid=(l, r // tr), in_specs=[pl.BlockSpec((8, 1, tr, c), lambda li, i: (0, li, i, 0))],
        out_specs=pl.BlockSpec((1, tr, c), lambda li, i: (li, i, 0)), out_shape=jax.ShapeDtypeStruct((l, r, c), F32),
        compiler_params=_params(dimension_semantics=("arbitrary", "arbitrary")),
    )(parts)


def _swap_halves(halves):
    n = len(halves)

    def body(*refs):
        src, dst = refs[:n], refs[n:2 * n]
        send_sems, recv_sems, local_sems = refs[2 * n:]
        x, y, c = _place()

        def place(k, pc):
            r = halves[k].shape[1]
            return dst[k].at[:, pl.ds(pl.multiple_of(pc * r, 8), r), :]

        local = [pltpu.make_async_copy(src[k], place(k, c), local_sems.at[k]) for k in range(n)]
        sends = [pltpu.make_async_remote_copy(src_ref=src[k], dst_ref=place(k, c), send_sem=send_sems.at[k],
                                              recv_sem=recv_sems.at[k], device_id=(x, y, 1 - c), device_id_type=MESH)
                 for k in range(n)]
        for cp in local + sends:
            cp.start()
        for k in range(n):
            pltpu.make_async_remote_copy(src_ref=src[k], dst_ref=place(k, 1 - c), send_sem=send_sems.at[k],
                                         recv_sem=recv_sems.at[k], device_id=(x, y, 1 - c), device_id_type=MESH).wait_recv()
        for cp in sends:
            cp.wait_send()
        for cp in local:
            cp.wait()

    any_spec = pl.BlockSpec(memory_space=pl.ANY)
    return pl.pallas_call(
        body, name="swap_halves", in_specs=[any_spec] * n, out_specs=[any_spec] * n,
        out_shape=[jax.ShapeDtypeStruct((h.shape[0], 2 * h.shape[1], h.shape[2]), F32) for h in halves],
        scratch_shapes=[pltpu.SemaphoreType.DMA((n,)), pltpu.SemaphoreType.DMA((n,)), pltpu.SemaphoreType.DMA((n,))],
        compiler_params=_params(),
    )(*halves)


def _adamw(name, w, g, m, v):
    shape = w.shape
    cols = shape[-1] if w.ndim > 1 and shape[-1] % LANES == 0 else w.size if w.size % LANES else LANES
    flat = lambda a: a.reshape(-1, cols)
    rows = w.size // cols
    tr = _pick(rows, [r for r in (512, 256, 128, 64, 32, 16, 8) if r * cols <= 256 * 1024]) if rows % 8 == 0 else rows

    def body(w_ref, g_ref, m_ref, v_ref, d_ref, nm_ref, nv_ref):
        gg = g_ref[...]
        nm = ADAM_B1 * m_ref[...] + (1.0 - ADAM_B1) * gg
        nv = ADAM_B2 * v_ref[...] + (1.0 - ADAM_B2) * (gg * gg)
        m_hat = nm / (1.0 - ADAM_B1 ** ADAM_STEP)
        v_hat = nv / (1.0 - ADAM_B2 ** ADAM_STEP)
        d_ref[...] = -ADAM_LR * (m_hat / (jnp.sqrt(v_hat) + ADAM_EPS) + ADAM_WD * w_ref[...])
        nm_ref[...] = nm
        nv_ref[...] = nv

    spec = pl.BlockSpec((tr, cols), lambda i: (i, 0))
    out = pl.pallas_call(
        body, name=name, grid=(rows // tr,), in_specs=[spec] * 4, out_specs=[spec] * 3,
        out_shape=[jax.ShapeDtypeStruct((rows, cols), F32)] * 3,
        compiler_params=_params(dimension_semantics=("arbitrary",)),
    )(flat(w), flat(g), flat(m), flat(v))
    return tuple(o.reshape(shape) for o in out)


WEIGHTS = ["w_ada", "b_ada", "w_in", "w_sb_up", "ssm_a_re", "ssm_a_im", "ssm_log_dt", "ssm_b_re", "ssm_b_im", "ssm_c_re",
           "ssm_c_im", "ssm_d", "w_glu", "b_glu", "w_ssm_up", "w_out", "ln1_g", "ln1_b", "w_ffn_in", "w_ffn_out", "ln2_g",
           "ln2_b"]
COL_SPLIT = ["w_in", "w_sb_up", "w_ssm_up", "w_ffn_in"]
ROW_SPLIT = ["w_glu", "w_out", "w_ffn_out"]
SMALL = ["ssm_a_re", "ssm_a_im", "ssm_log_dt", "ssm_b_re", "ssm_b_im", "ssm_c_re", "ssm_c_im", "ssm_d", "b_glu", "ln1_g",
         "ln1_b", "ln2_g", "ln2_b"]
SLAB_COLS = 1024


def _cast_bf16(name, w):
    shape = w.shape
    flat = w.reshape(-1, shape[-1])
    rows, cols = flat.shape
    tr = _pick(rows, (512, 256, 128, 64, 8))

    def body(w_ref, o_ref):
        o_ref[...] = w_ref[...].astype(BF16)

    spec = pl.BlockSpec((tr, cols), lambda i: (i, 0))
    return pl.pallas_call(body, name=name, grid=(rows // tr,), in_specs=[spec], out_specs=spec,
                          out_shape=jax.ShapeDtypeStruct((rows, cols), BF16),
                          compiler_params=_params(dimension_semantics=("arbitrary",)))(flat).reshape(shape)


def _silu_rows(name, c):
    def body(c_ref, o_ref):
        v = c_ref[...]
        o_ref[...] = v * jax.nn.sigmoid(v)

    return pl.pallas_call(body, name=name, out_shape=jax.ShapeDtypeStruct(c.shape, F32), compiler_params=_params())(c)


def _pad_rows(v, mult=8):
    flat = v.reshape(-1)
    per = mult * SLAB_COLS
    total = -(-flat.size // per) * per
    return jnp.pad(flat, (0, total - flat.size)).reshape(-1, SLAB_COLS)


def kernel(x, c, w_ada, b_ada, w_in, w_sb_up, ssm_a_re, ssm_a_im, ssm_log_dt, ssm_b_re, ssm_b_im, ssm_c_re, ssm_c_im, ssm_d, w_glu, b_glu, w_ssm_up, w_out, ln1_g, ln1_b, w_ffn_in, w_ffn_out, ln2_g, ln2_b, loss_target, m_w_ada, m_b_ada, m_w_in, m_w_sb_up, m_ssm_a_re, m_ssm_a_im, m_ssm_log_dt, m_ssm_b_re, m_ssm_b_im, m_ssm_c_re, m_ssm_c_im, m_ssm_d, m_w_glu, m_b_glu, m_w_ssm_up, m_w_out, m_ln1_g, m_ln1_b, m_w_ffn_in, m_w_ffn_out, m_ln2_g, m_ln2_b, v_w_ada, v_b_ada, v_w_in, v_w_sb_up, v_ssm_a_re, v_ssm_a_im, v_ssm_log_dt, v_ssm_b_re, v_ssm_b_im, v_ssm_c_re, v_ssm_c_im, v_ssm_d, v_w_glu, v_b_glu, v_w_ssm_up, v_w_out, v_ln1_g, v_ln1_b, v_w_ffn_in, v_w_ffn_out, v_ln2_g, v_ln2_b):
    args = dict(locals())
    w = {n: args[n] for n in WEIGHTS}
    mom = {n: args["m_" + n] for n in WEIGHTS}
    var = {n: args["v_" + n] for n in WEIGHTS}
    depth, d = w_ada.shape[0], x.shape[-1]
    xi, yi, ci = _place()
    me, chip = 4 * xi + 2 * yi + ci, 2 * xi + yi
    ada_cols = w_ada.shape[2]

    big = COL_SPLIT + ROW_SPLIT
    by_rows = [n in ROW_SPLIT for n in big]
    full = dict(zip(big, _gather_weights([_cast_bf16(f"cast_{n}", w[n]) for n in big], by_rows)))

    c_all = _all_gather8("gather_c", jnp.pad(c, ((0, 7), (0, 0))))[::8]
    c_act = _silu_rows("silu_c", c_all)
    b_cols = lax.dynamic_slice_in_dim(b_ada, chip * ada_cols, ada_cols, axis=1)
    mod_part = jnp.concatenate([_small_mm(f"mod_{l}", c_act, w_ada[l], "nn") + b_cols[l][None] for l in range(depth)], axis=0)
    mod_all = _all_gather8("gather_mod", mod_part).reshape(4, 2, depth, 8, ada_cols)[:, 0]
    mod_mine = lax.dynamic_index_in_dim(mod_all, me, axis=2, keepdims=False)
    mod = mod_mine.transpose(1, 0, 2).reshape(depth, 6, d)

    layer_w = [{**{n: full[n][l] for n in big}, **{n: w[n][l] for n in SMALL}} for l in range(depth)]
    h, saved = x[0], []
    for l in range(depth):
        h, sv = _layer_fwd(h, mod[l], layer_w[l], str(l))
        saved.append(sv)
    dh, loss_cols = _loss_head(h, loss_target[0])
    loss = lax.psum(jnp.sum(loss_cols), ("x", "y", "c"))
    dmods, lgrads, stacked = [None] * depth, [None] * depth, {}
    for l in reversed(range(depth)):
        dh, dmods[l], lgrads[l] = _layer_bwd(dh, mod[l], layer_w[l], saved[l], l, depth, stacked)
        stacked = {n: lgrads[l][n] for n in big}
    grad_x = dh[None]

    parts = _scatter_grads([stacked[n] for n in big], by_rows)
    halves = [_sum_slots(f"sum_{n}", p) for n, p in zip(big, parts)]
    grad = dict(zip(big, _swap_halves(halves)))

    pieces = [jnp.stack(dmods)] + [jnp.stack([lgrads[l][n] for l in range(depth)]) for n in SMALL]
    slab = jnp.concatenate([_pad_rows(p) for p in pieces], axis=0)
    slabs = _all_gather8("gather_small", slab).reshape(8, 1, *slab.shape)
    total = _sum_slots("sum_small", slabs)[0]
    row = _pad_rows(pieces[0]).shape[0]
    for n, p in zip(SMALL, pieces[1:]):
        rows = _pad_rows(p).shape[0]
        grad[n] = total[row:row + rows].reshape(-1)[:p.size].reshape(p.shape)
        row += rows
    dmod_rows = _pad_rows(pieces[0]).shape[0]
    dmod_all = slabs[:, 0, :dmod_rows].reshape(8, -1)[:, :depth * 6 * d].reshape(8, depth, 4, ada_cols)
    dmod_cols = lax.dynamic_index_in_dim(dmod_all, chip, axis=2, keepdims=False)
    grad["w_ada"] = jnp.stack([_small_mm(f"dw_ada_{l}", c_act, dmod_cols[:, l], "tn") for l in range(depth)])
    dmod_sum = _sum_slots("sum_dmod", slabs[:, :, :dmod_rows])[0]
    grad["b_ada"] = dmod_sum.reshape(-1)[:depth * 6 * d].reshape(depth, 6 * d)

    delta, new_m, new_v = {}, {}, {}
    for n in WEIGHTS:
        delta[n], new_m[n], new_v[n] = _adamw(f"adamw_{n}", w[n], grad[n], mom[n], var[n])
    return (loss, grad_x, *[grad[n] for n in WEIGHTS], *[delta[n] for n in WEIGHTS], *[new_m[n] for n in WEIGHTS],
            *[new_v[n] for n in WEIGHTS])
```

```python
import functools
import math

import jax
import jax.numpy as jnp
from jax import lax
from jax.experimental import pallas as pl
from jax.experimental.pallas import tpu as pltpu

F32 = jnp.float32
BF16 = jnp.bfloat16
MESH = pl.DeviceIdType.MESH

LANES = 128
HEAD_DIM = 64
ATT_TILE = 256
SSM_GROUPS, SSM_STATE, SSM_GROUP = 32, 64, 16
N_STATE = SSM_GROUPS * SSM_STATE
LN_EPS = 1e-5
DEPTH = 2
ALPHA = (2 * DEPTH) ** 0.25
ADAM_LR, ADAM_B1, ADAM_B2, ADAM_EPS, ADAM_WD, ADAM_STEP = 0.001, 0.9, 0.999, 1e-08, 0.01, 10
VMEM_LIMIT = 56 * 1024 * 1024
GELU_K = math.sqrt(2.0 / math.pi)
GELU_C = 0.044715


def _params(**kw):
    return pltpu.CompilerParams(vmem_limit_bytes=VMEM_LIMIT, **kw)


def _pick(n, prefs):
    for p in prefs:
        if n % p == 0:
            return p
    return n


def _rowwise(name, fn, rows, vecs, outs, sums=(), tm=None):
    s = rows[0][0].shape[0]
    tm = tm or _pick(s, (256, 128, 64, 8))
    nin, no, ns = len(rows) + len(vecs), len(outs), len(sums)

    def body(*refs):
        res = fn(*[r[...] for r in refs[:nin]])
        res = res if isinstance(res, tuple) else (res,)
        for r, v in zip(refs[nin:nin + no], res[:no]):
            r[...] = v.astype(r.dtype)
        if ns:
            @pl.when(pl.program_id(0) == 0)
            def _():
                for r in refs[nin + no:]:
                    r[...] = jnp.zeros_like(r)
            for r, v in zip(refs[nin + no:], res[no:]):
                r[...] += v

    in_specs = [pl.BlockSpec((tm, w), lambda i, cb=cb: (i, cb)) for _, cb, w in rows]
    in_specs += [pl.BlockSpec(v.shape, lambda i: (0, 0)) for v in vecs]
    out_specs = [pl.BlockSpec((tm, w), lambda i: (i, 0)) for w, _ in outs]
    out_specs += [pl.BlockSpec((1, w), lambda i: (0, 0)) for w in sums]
    out_shape = [jax.ShapeDtypeStruct((s, w), dt) for w, dt in outs]
    out_shape += [jax.ShapeDtypeStruct((1, w), F32) for w in sums]
    res = pl.pallas_call(
        body, name=name, grid=(s // tm,), in_specs=in_specs, out_specs=out_specs, out_shape=out_shape,
        compiler_params=_params(dimension_semantics=("arbitrary",)),
    )(*[a for a, _, _ in rows], *vecs)
    return res[0] if len(res) == 1 else tuple(res)


MM_TILES = (1408, 1024, 512, 256, 128)


def _mm(name, a, b, mode, out_dtype=F32, a_cols=None, into=None):
    a_off, a_w = a_cols if a_cols else (0, a.shape[1])
    if mode == "nn":
        m, k, n = a.shape[0], a_w, b.shape[1]
    elif mode == "nt":
        m, k, n = a.shape[0], a_w, b.shape[0]
    else:
        k, m, n = a.shape[0], a_w, b.shape[1]
    tm = _pick(m, MM_TILES if mode == "tn" else MM_TILES[1:])
    tn = _pick(n, MM_TILES)
    tk = _pick(k, MM_TILES)
    nk = k // tk
    dims = {"nn": ((1,), (0,)), "nt": ((1,), (1,)), "tn": ((0,), (0,))}[mode]

    def body(a_ref, b_ref, *rest):
        o_ref = rest[-2] if nk > 1 else rest[-1]
        prod = lax.dot_general(a_ref[...].astype(BF16), b_ref[...].astype(BF16), (dims, ((), ())),
                               preferred_element_type=F32)
        if nk == 1:
            o_ref[...] = prod.astype(o_ref.dtype)
            return
        acc_ref = rest[-1]
        kk = pl.program_id(2)

        @pl.when(kk == 0)
        def _():
            acc_ref[...] = prod

        @pl.when(kk > 0)
        def _():
            acc_ref[...] += prod

        @pl.when(kk == nk - 1)
        def _():
            o_ref[...] = acc_ref[...].astype(o_ref.dtype)

    if mode == "tn":
        a_spec = pl.BlockSpec((tk, tm), lambda i, j, kk: (kk, a_off // tm + i))
    else:
        a_spec = pl.BlockSpec((tm, tk), lambda i, j, kk: (i, a_off // tk + kk))
    if mode == "nt":
        b_spec = pl.BlockSpec((tn, tk), lambda i, j, kk: (j, kk))
    else:
        b_spec = pl.BlockSpec((tk, tn), lambda i, j, kk: (kk, j))
    in_specs, operands, aliases = [a_spec, b_spec], [a, b], {}
    if into is None:
        out_spec = pl.BlockSpec((tm, tn), lambda i, j, kk: (i, j))
        out_shape = jax.ShapeDtypeStruct((m, n), out_dtype)
    else:
        buf, slab, count = into
        out_spec = pl.BlockSpec((None, tm, tn), lambda i, j, kk: (slab, i, j))
        out_shape = jax.ShapeDtypeStruct((count, m, n), out_dtype)
        if buf is not None:
            in_specs.append(pl.BlockSpec(memory_space=pl.ANY))
            operands.append(buf)
            aliases = {2: 0}
    return pl.pallas_call(
        body, name=name, grid=(m // tm, n // tn, nk), in_specs=in_specs, out_specs=out_spec, out_shape=out_shape,
        scratch_shapes=[pltpu.VMEM((tm, tn), F32)] if nk > 1 else [], input_output_aliases=aliases,
        compiler_params=_params(dimension_semantics=("arbitrary", "arbitrary", "arbitrary")),
    )(*operands)


def _small_mm(name, a, b, mode):
    dims = {"nn": ((1,), (0,)), "tn": ((0,), (0,))}[mode]
    m = a.shape[0] if mode == "nn" else a.shape[1]

    def body(a_ref, b_ref, o_ref):
        o_ref[...] = lax.dot_general(a_ref[...], b_ref[...], (dims, ((), ())), precision=lax.Precision.HIGHEST,
                                     preferred_element_type=F32)

    return pl.pallas_call(body, name=name, out_shape=jax.ShapeDtypeStruct((m, b.shape[1]), F32),
                          compiler_params=_params())(a, b)


def _norm(x):
    mu = jnp.mean(x, axis=-1, keepdims=True)
    xc = x - mu
    rstd = lax.rsqrt(jnp.mean(xc * xc, axis=-1, keepdims=True) + LN_EPS)
    return xc * rstd, rstd


def _norm_bwd(dn, n, rstd):
    return rstd * (dn - jnp.mean(dn, axis=-1, keepdims=True) - n * jnp.mean(dn * n, axis=-1, keepdims=True))


def _colsum(v):
    return jnp.sum(v, axis=0, keepdims=True)


def _gelu(x):
    return 0.5 * x * (1.0 + jnp.tanh(GELU_K * (x + GELU_C * x * x * x)))


def _gelu_grad(x):
    t = jnp.tanh(GELU_K * (x + GELU_C * x * x * x))
    return 0.5 * (1.0 + t) + 0.5 * x * (1.0 - t * t) * GELU_K * (1.0 + 3.0 * GELU_C * x * x)


def _log_sigmoid_parts(z):
    lb = jnp.minimum(z, 0.0) - jnp.log(1.0 + jnp.exp(-jnp.abs(z)))
    return lb, lb - z


def _qkv_prep(proj, t):
    s = proj.shape[0]
    nb, nhp = s // t, 512 // LANES

    def body(q_ref, k_ref, v_ref, qs_ref, kb_ref, vb_ref, kt_ref, vt_ref):
        qs_ref[...] = (q_ref[...] * (1.0 / math.sqrt(HEAD_DIM))).astype(BF16)
        k, v = k_ref[...], v_ref[...]
        kb_ref[...] = k.astype(BF16)
        vb_ref[...] = v.astype(BF16)
        for hp in range(nhp):
            kt_ref[hp, 0] = k[:, hp * LANES:(hp + 1) * LANES].T.astype(BF16)
            vt_ref[hp, 0] = v[:, hp * LANES:(hp + 1) * LANES].T.astype(BF16)

    col = lambda cb: pl.BlockSpec((t, 512), lambda i, cb=cb: (i, cb))
    row_out = pl.BlockSpec((t, 512), lambda i: (i, 0))
    t_out = pl.BlockSpec((nhp, 1, LANES, t), lambda i: (0, i, 0, 0))
    return pl.pallas_call(
        body, name="qkv_prep", grid=(nb,), in_specs=[col(0), col(1), col(2)],
        out_specs=[row_out, row_out, row_out, t_out, t_out],
        out_shape=[jax.ShapeDtypeStruct((s, 512), BF16)] * 3 + [jax.ShapeDtypeStruct((nhp, nb, LANES, t), BF16)] * 2,
        compiler_params=_params(dimension_semantics=("arbitrary",)),
    )(proj, proj, proj)


def _tile_masks(t):
    row = lax.broadcasted_iota(jnp.int32, (t, t), 0)
    col = lax.broadcasted_iota(jnp.int32, (t, t), 1)
    return row, col


DEAD_LOG_WEIGHT = -110.0


def _walk_down(i, tiles, state, alive):
    def pair(c):
        return c[0] + 1, tiles([i - 1 - 2 * c[0], i - 2 - 2 * c[0]], c[1], False)

    p, st = lax.while_loop(lambda c: (c[0] < i // 2) & alive(c[1]), pair, (jnp.int32(0), tiles([i], state, True)))
    return lax.cond((i % 2 == 1) & (p == i // 2) & alive(st), lambda s_: tiles([0], s_, False), lambda s_: s_, st)


def _walk_up(i, first, tiles, state):
    n = i - first
    st = lax.fori_loop(0, n // 2, lambda p, s_: tiles([first + 2 * p, first + 2 * p + 1], s_, False), state)
    st = lax.cond(n % 2 == 1, lambda s_: tiles([i - 1], s_, False), lambda s_: s_, st)
    return tiles([i], st, True)


def _nt(a, b):
    return lax.dot_general(a, b, (((1,), (1,)), ((), ())), preferred_element_type=F32)


def _nn(a, b):
    return jnp.dot(a, b, preferred_element_type=F32)


def _attn_fwd(qs, k, vt3, t):
    s = qs.shape[0]
    nb, nhp = s // t, qs.shape[1] // LANES

    def body(q_ref, k_ref, vt_ref, o_ref, car_ref):
        i = pl.program_id(1)
        q2 = q_ref[...]
        lane_q = lax.broadcasted_iota(jnp.int32, q2.shape, 1)
        row, col = _tile_masks(t)
        later = (col > row).astype(BF16)
        valid = row < col
        orow = lax.broadcasted_iota(jnp.int32, (LANES, t), 0)
        car_ref[...] = jnp.full(car_ref.shape, 2.0 * DEAD_LOG_WEIGHT, F32)
        qh = [jnp.where((lane_q < HEAD_DIM) == (hh == 0), q2, jnp.zeros_like(q2)) for hh in range(2)]

        def tiles(js, state, masked):
            chains = [(n, hh) for n in range(len(js)) for hh in range(2)]
            kb = [k_ref[pl.ds(pl.multiple_of(j * t, t), t), :] for j in js]
            z = {ch: _nt(kb[ch[0]], qh[ch[1]]) for ch in chains}
            lb, aft, csum = {}, {}, {}
            for ch in chains:
                lb[ch], l1m = _log_sigmoid_parts(z[ch])
                if masked:
                    l1m = jnp.where(valid, l1m, 0.0)
                aft[ch] = _nn(later, l1m.astype(BF16))
                csum[ch] = _colsum(l1m)
            state = list(state)
            for ch in chains:
                n, hh = ch
                c_after, acc = state[hh]
                w = jnp.exp(lb[ch] + aft[ch] + c_after)
                if masked:
                    w = jnp.where(valid, w, 0.0)
                car_ref[hh, pl.ds(js[n], 1), :] = c_after
                state[hh] = (c_after + csum[ch], acc + _nn(vt_ref[0, js[n]], w.astype(BF16)))
            return tuple(state)

        def alive(state):
            return jnp.max(jnp.maximum(state[0][0], state[1][0])) >= DEAD_LOG_WEIGHT

        zero = (jnp.zeros((1, t), F32), jnp.zeros((LANES, t), F32))
        (_, acc0), (_, acc1) = _walk_down(i, tiles, (zero, zero), alive)
        o_ref[...] = jnp.where(orow < HEAD_DIM, acc0, acc1).T.astype(o_ref.dtype)

    return pl.pallas_call(
        body, name="attn_fwd", grid=(nhp, nb),
        in_specs=[pl.BlockSpec((t, LANES), lambda hp, i: (i, hp)),
                  pl.BlockSpec((s, LANES), lambda hp, i: (0, hp)),
                  pl.BlockSpec((1, nb, LANES, t), lambda hp, i: (hp, 0, 0, 0))],
        out_specs=[pl.BlockSpec((t, LANES), lambda hp, i: (i, hp)),
                   pl.BlockSpec((2, nb, t), lambda hp, i: (hp, 0, i))],
        out_shape=[jax.ShapeDtypeStruct((s, nhp * LANES), BF16), jax.ShapeDtypeStruct((2 * nhp, nb, s), F32)],
        compiler_params=_params(dimension_semantics=("arbitrary", "arbitrary")),
    )(qs, k, vt3)


def _attn_bwd(qs, do, k, v, kt3, car, t):
    s = qs.shape[0]
    nb, nhp = s // t, qs.shape[1] // LANES

    def body(q_ref, do_ref, k_ref, v_ref, kt_ref, car_ref, dq_ref, dk_ref, dv_ref):
        i = pl.program_id(1)

        @pl.when(i == 0)
        def _():
            dk_ref[...] = jnp.zeros_like(dk_ref)
            dv_ref[...] = jnp.zeros_like(dv_ref)

        q2, do2 = q_ref[...], do_ref[...]
        lane_q = lax.broadcasted_iota(jnp.int32, q2.shape, 1)
        row, col = _tile_masks(t)
        later = (col > row).astype(BF16)
        earlier = (col < row).astype(BF16)
        valid = row < col
        orow = lax.broadcasted_iota(jnp.int32, (LANES, t), 0)
        head = [(lane_q < HEAD_DIM) == (hh == 0) for hh in range(2)]
        qh = [jnp.where(hm, q2, jnp.zeros_like(q2)) for hm in head]
        doh = [jnp.where(hm, do2, jnp.zeros_like(do2)) for hm in head]

        def tiles(js, state, masked):
            chains = [(n, hh) for n in range(len(js)) for hh in range(2)]
            rows = [pl.ds(pl.multiple_of(j * t, t), t) for j in js]
            kb = [k_ref[r, :] for r in rows]
            vb = [v_ref[r, :] for r in rows]
            z = {ch: _nt(kb[ch[0]], qh[ch[1]]) for ch in chains}
            dw = {ch: _nt(vb[ch[0]], doh[ch[1]]) for ch in chains}
            lb, beta, aft = {}, {}, {}
            for ch in chains:
                lb[ch], l1m = _log_sigmoid_parts(z[ch])
                beta[ch] = jnp.exp(lb[ch])
                if masked:
                    l1m = jnp.where(valid, l1m, 0.0)
                aft[ch] = _nn(later, l1m.astype(BF16))
            w, g, gsum, g_in = {}, {}, {}, {}
            for ch in chains:
                n, hh = ch
                w[ch] = jnp.exp(lb[ch] + aft[ch] + car_ref[hh, pl.ds(js[n], 1), :])
                if masked:
                    w[ch] = jnp.where(valid, w[ch], 0.0)
                g[ch] = dw[ch] * w[ch]
                g_in[ch] = _nn(earlier, g[ch].astype(BF16))
                gsum[ch] = _colsum(g[ch])
            state = list(state)
            dk_t, dv_t = [None] * len(js), [None] * len(js)
            for ch in chains:
                n, hh = ch
                c_g, dqt = state[hh]
                dz = g[ch] - beta[ch] * (g[ch] + g_in[ch] + c_g)
                if masked:
                    dz = jnp.where(valid, dz, 0.0)
                dzb, wb = dz.astype(BF16), w[ch].astype(BF16)
                dk_h, dv_h = _nn(dzb, qh[hh]), _nn(wb, doh[hh])
                dk_t[n] = dk_h if dk_t[n] is None else dk_t[n] + dk_h
                dv_t[n] = dv_h if dv_t[n] is None else dv_t[n] + dv_h
                state[hh] = (c_g + gsum[ch], dqt + _nn(kt_ref[0, js[n]], dzb))
            for n in range(len(js)):
                dk_ref[rows[n], :] += dk_t[n]
                dv_ref[rows[n], :] += dv_t[n]
            return tuple(state)

        reach = jnp.max(jnp.max(car_ref[...], axis=2, keepdims=True), axis=0)
        dead = (reach < DEAD_LOG_WEIGHT) & (lax.broadcasted_iota(jnp.int32, reach.shape, 0) < i)
        first = jnp.sum(jnp.where(dead, 1.0, 0.0)).astype(jnp.int32)
        zero = (jnp.zeros((1, t), F32), jnp.zeros((LANES, t), F32))
        (_, dq0), (_, dq1) = _walk_up(i, first, tiles, (zero, zero))
        dq_ref[...] = jnp.where(orow < HEAD_DIM, dq0, dq1).T

    tile_spec = pl.BlockSpec((t, LANES), lambda hp, i: (i, hp))
    whole = pl.BlockSpec((s, LANES), lambda hp, i: (0, hp))
    return pl.pallas_call(
        body, name="attn_bwd", grid=(nhp, nb),
        in_specs=[tile_spec, tile_spec, whole, whole,
                  pl.BlockSpec((1, nb, LANES, t), lambda hp, i: (hp, 0, 0, 0)),
                  pl.BlockSpec((2, nb, t), lambda hp, i: (hp, 0, i))],
        out_specs=[tile_spec, whole, whole],
        out_shape=[jax.ShapeDtypeStruct((s, nhp * LANES), F32)] * 3,
        compiler_params=_params(dimension_semantics=("arbitrary", "arbitrary")),
    )(qs, do, k, v, kt3, car)


SCAN_LANES = 1024
SCAN_ROWS = 8


def _scan_chunks(v):
    n = v.shape[1] // (2 * LANES)
    return [(v[:, c * 2 * LANES:c * 2 * LANES + LANES], v[:, c * 2 * LANES + LANES:(c + 1) * 2 * LANES]) for c in range(n)]


def _scan_tables(lr, li, reverse):
    if reverse:
        li = -li
    row = lax.broadcasted_iota(jnp.int32, (SCAN_ROWS, LANES), 0)
    powers = [(lr, li)]
    for _ in range(SCAN_ROWS - 1):
        pr, pi = powers[-1]
        powers.append((pr * lr - pi * li, pr * li + pi * lr))
    levels = []
    for d in (1, 2, 4):
        keep = (row < SCAN_ROWS - d) if reverse else (row >= d)
        levels.append((SCAN_ROWS - d if reverse else d,
                       (jnp.where(keep, powers[d - 1][0], 0.0), jnp.where(keep, powers[d - 1][1], 0.0))))
    pr = pi = jnp.zeros((SCAN_ROWS, LANES), F32)
    for r in range(SCAN_ROWS):
        steps = SCAN_ROWS - r if reverse else r + 1
        pr = jnp.where(row == r, powers[steps - 1][0], pr)
        pi = jnp.where(row == r, powers[steps - 1][1], pi)
    return levels, (pr, pi)


def _ssm_scan_fwd(bu, lam):
    s, w = bu.shape
    tt = _pick(s, (512, 256, 128, 8))
    nt = s // tt

    def body(x_ref, lam_ref, h_ref, st_ref):
        @pl.when(pl.program_id(1) == 0)
        def _():
            st_ref[...] = jnp.zeros_like(st_ref)

        tables = [_scan_tables(lr, li, reverse=False) for lr, li in _scan_chunks(lam_ref[...])]

        def tile(it, last):
            r0 = pl.multiple_of(it * SCAN_ROWS, SCAN_ROWS)
            last, parts = list(last), []
            for c, (xr, xi) in enumerate(_scan_chunks(x_ref[pl.ds(r0, SCAN_ROWS), :])):
                levels, (pr, pi) = tables[c]
                for d, (ar, ai) in levels:
                    sr, si = pltpu.roll(xr, d, 0), pltpu.roll(xi, d, 0)
                    xr, xi = xr + ar * sr - ai * si, xi + ar * si + ai * sr
                br, bi = last[2 * c], last[2 * c + 1]
                hr = xr + pr * br - pi * bi
                hi = xi + pr * bi + pi * br
                last[2 * c], last[2 * c + 1] = hr[SCAN_ROWS - 1:], hi[SCAN_ROWS - 1:]
                parts += [hr, hi]
            h_ref[pl.ds(r0, SCAN_ROWS), :] = jnp.concatenate(parts, axis=1)
            return tuple(last)

        st = st_ref[0:1, :]
        init = tuple(st[:, c * LANES:(c + 1) * LANES] for c in range(SCAN_LANES // LANES))
        fin = lax.fori_loop(0, tt // SCAN_ROWS, tile, init)
        st_ref[0:1, :] = jnp.concatenate(fin, axis=1)

    return pl.pallas_call(
        body, name="ssm_scan_fwd", grid=(w // SCAN_LANES, nt),
        in_specs=[pl.BlockSpec((tt, SCAN_LANES), lambda lc, i: (i, lc)),
                  pl.BlockSpec((1, SCAN_LANES), lambda lc, i: (0, lc))],
        out_specs=pl.BlockSpec((tt, SCAN_LANES), lambda lc, i: (i, lc)),
        out_shape=jax.ShapeDtypeStruct((s, w), F32),
        scratch_shapes=[pltpu.VMEM((SCAN_ROWS, SCAN_LANES), F32)],
        compiler_params=_params(dimension_semantics=("arbitrary", "arbitrary")),
    )(bu, lam)


def _ssm_scan_bwd(e, h, lam):
    s, w = e.shape
    tt = _pick(s, (512, 256, 128, 8))
    nt = s // tt

    def body(e_ref, h_ref, lam_ref, a_ref, dlam_ref, st_ref):
        @pl.when(pl.program_id(1) == 0)
        def _():
            st_ref[...] = jnp.zeros_like(st_ref)
            dlam_ref[...] = jnp.zeros_like(dlam_ref)

        tables = [_scan_tables(lr, li, reverse=True) for lr, li in _scan_chunks(lam_ref[...])]
        nch = len(tables)
        row = lax.broadcasted_iota(jnp.int32, (SCAN_ROWS, LANES), 0)

        def tile(it, carry):
            r0 = pl.multiple_of((tt // SCAN_ROWS - 1 - it) * SCAN_ROWS, SCAN_ROWS)
            e_c = _scan_chunks(e_ref[pl.ds(r0, SCAN_ROWS), :])
            h_c = _scan_chunks(h_ref[pl.ds(r0, SCAN_ROWS), :])
            carry, parts = list(carry), []
            for c in range(nch):
                (yr, yi), (hr, hi) = e_c[c], h_c[c]
                levels, (pr, pi) = tables[c]
                for shift, (lr, li) in levels:
                    sr, si = pltpu.roll(yr, shift, 0), pltpu.roll(yi, shift, 0)
                    yr, yi = yr + lr * sr - li * si, yi + lr * si + li * sr
                nr, ni, dr, di = carry[4 * c:4 * c + 4]
                ar = yr + pr * nr - pi * ni
                ai = yi + pr * ni + pi * nr
                nxr = jnp.where(row == SCAN_ROWS - 1, nr, pltpu.roll(ar, SCAN_ROWS - 1, 0))
                nxi = jnp.where(row == SCAN_ROWS - 1, ni, pltpu.roll(ai, SCAN_ROWS - 1, 0))
                carry[4 * c:4 * c + 4] = [ar[0:1], ai[0:1], dr + nxr * hr + nxi * hi, di + nxi * hr - nxr * hi]
                parts += [ar, ai]
            a_ref[pl.ds(r0, SCAN_ROWS), :] = jnp.concatenate(parts, axis=1)
            return tuple(carry)

        st, dl = st_ref[0:1, :], dlam_ref[...]
        init = []
        for c in range(nch):
            lo = c * 2 * LANES
            init += [st[:, lo:lo + LANES], st[:, lo + LANES:lo + 2 * LANES],
                     dl[:, lo:lo + LANES], dl[:, lo + LANES:lo + 2 * LANES]]
        fin = lax.fori_loop(0, tt // SCAN_ROWS, tile, tuple(init))
        st_ref[0:1, :] = jnp.concatenate([fin[4 * c + q] for c in range(nch) for q in (0, 1)], axis=1)
        dlam_ref[...] = jnp.concatenate([fin[4 * c + q] for c in range(nch) for q in (2, 3)], axis=1)

        @pl.when(pl.program_id(1) == nt - 1)
        def _():
            dlam_ref[0:1, :] = jnp.sum(dlam_ref[...], axis=0, keepdims=True)

    rev = pl.BlockSpec((tt, SCAN_LANES), lambda lc, i: (nt - 1 - i, lc))
    vec = pl.BlockSpec((1, SCAN_LANES), lambda lc, i: (0, lc))
    return pl.pallas_call(
        body, name="ssm_scan_bwd", grid=(w // SCAN_LANES, nt),
        in_specs=[rev, rev, vec], out_specs=[rev, pl.BlockSpec((SCAN_ROWS, SCAN_LANES), lambda lc, i: (0, lc))],
        out_shape=[jax.ShapeDtypeStruct((s, w), F32), jax.ShapeDtypeStruct((SCAN_ROWS, w), F32)],
        scratch_shapes=[pltpu.VMEM((SCAN_ROWS, SCAN_LANES), F32)],
        compiler_params=_params(dimension_semantics=("arbitrary", "arbitrary")),
    )(e, h, lam)


def _ssm_params_fwd(a_re, a_im, log_dt, b_re, b_im):
    def body(ar_ref, ai_ref, ldt_ref, br_ref, bi_ref, lr_ref, li_ref, bbr_ref, bbi_ref):
        ar, ai, dt = ar_ref[...], ai_ref[...], jnp.exp(ldt_ref[...])
        mag = jnp.exp(ar * dt)
        lr, li = mag * jnp.cos(ai * dt), mag * jnp.sin(ai * dt)
        den = ar * ar + ai * ai
        cr = ((lr - 1.0) * ar + li * ai) / den
        ci = (li * ar - (lr - 1.0) * ai) / den
        br, bi = br_ref[...], bi_ref[...]
        lr_ref[...], li_ref[...] = lr, li
        bbr_ref[...] = cr * br - ci * bi
        bbi_ref[...] = cr * bi + ci * br

    n = a_re.shape[0]
    v1, v16 = jax.ShapeDtypeStruct((n, 1), F32), jax.ShapeDtypeStruct((n, SSM_GROUP), F32)
    return pl.pallas_call(body, name="ssm_params_fwd", out_shape=[v1, v1, v16, v16],
                          compiler_params=_params())(a_re, a_im, log_dt, b_re, b_im)


def _ssm_params_bwd(a_re, a_im, log_dt, b_re, b_im, g_lr, g_li, g_bbr, g_bbi):
    n = a_re.shape[0]

    def body(ar_ref, ai_ref, ldt_ref, br_ref, bi_ref, glr_ref, gli_ref, gbr_ref, gbi_ref,
             dar_ref, dai_ref, dldt_ref, dbr_ref, dbi_ref):
        ar, ai, dt = ar_ref[...], ai_ref[...], jnp.exp(ldt_ref[...])
        mag = jnp.exp(ar * dt)
        lr, li = mag * jnp.cos(ai * dt), mag * jnp.sin(ai * dt)
        den = ar * ar + ai * ai
        cr = ((lr - 1.0) * ar + li * ai) / den
        ci = (li * ar - (lr - 1.0) * ai) / den
        br, bi, gbr, gbi = br_ref[...], bi_ref[...], gbr_ref[...], gbi_ref[...]
        dbr_ref[...] = gbr * cr + gbi * ci
        dbi_ref[...] = gbi * cr - gbr * ci
        gcr = jnp.sum(gbr * br + gbi * bi, axis=1, keepdims=True)
        gci = jnp.sum(gbi * br - gbr * bi, axis=1, keepdims=True)
        ir, ii = ar / den, -ai / den
        glr = glr_ref[...] + gcr * ir + gci * ii
        gli = gli_ref[...] + gci * ir - gcr * ii
        qr, qi = cr * ir - ci * ii, cr * ii + ci * ir
        gar = -(gcr * qr + gci * qi)
        gai = -(gci * qr - gcr * qi)
        gxr = glr * lr + gli * li
        gxi = gli * lr - glr * li
        dar_ref[...] = gar + gxr * dt
        dai_ref[...] = gai + gxi * dt
        gdt = (gxr * ar + gxi * ai) * dt
        rowg = lax.broadcasted_iota(jnp.int32, (n, SSM_GROUPS), 0) // SSM_STATE
        colg = lax.broadcasted_iota(jnp.int32, (n, SSM_GROUPS), 1)
        dldt_ref[...] = jnp.sum(jnp.where(rowg == colg, gdt, 0.0), axis=0, keepdims=True)

    v1, v16 = jax.ShapeDtypeStruct((n, 1), F32), jax.ShapeDtypeStruct((n, SSM_GROUP), F32)
    return pl.pallas_call(body, name="ssm_params_bwd",
                          out_shape=[v1, v1, jax.ShapeDtypeStruct((1, SSM_GROUPS), F32), v16, v16],
                          compiler_params=_params())(a_re, a_im, log_dt, b_re, b_im, g_lr, g_li, g_bbr, g_bbi)


def _interleave(re, im, axis):
    shp = list(re.shape)
    new = shp[:axis] + [shp[axis] // LANES, LANES] + shp[axis + 1:]
    st = jnp.stack([re.reshape(new), im.reshape(new)], axis=axis + 1)
    return st.reshape(shp[:axis] + [2 * shp[axis]] + shp[axis + 1:])


def _deinterleave(v, axis):
    shp = list(v.shape)
    r = v.reshape(shp[:axis] + [shp[axis] // (2 * LANES), 2, LANES] + shp[axis + 1:])
    out = shp[:axis] + [shp[axis] // 2] + shp[axis + 1:]
    return (lax.index_in_dim(r, 0, axis + 1, keepdims=False).reshape(out),
            lax.index_in_dim(r, 1, axis + 1, keepdims=False).reshape(out))


def _b_matrix(bbr, bbi):
    eye = jnp.eye(SSM_GROUPS, dtype=F32)

    def blockdiag(v):
        x = v.reshape(SSM_GROUPS, SSM_STATE, SSM_GROUP).transpose(0, 2, 1)
        return (eye[:, None, :, None] * x[:, :, None, :]).reshape(SSM_GROUPS * SSM_GROUP, N_STATE)

    return _interleave(blockdiag(bbr), blockdiag(bbi), 1)


def _diag_blocks(v, rows, cols):
    return jnp.stack([v[g * rows:(g + 1) * rows, g * cols:(g + 1) * cols] for g in range(SSM_GROUPS)])


def _b_matrix_grad(d):
    def diag(v):
        return _diag_blocks(v, SSM_GROUP, SSM_STATE).transpose(0, 2, 1).reshape(N_STATE, SSM_GROUP)

    dr, di = _deinterleave(d, 1)
    return diag(dr), diag(di)


def _c_matrix(c_re, c_im):
    eye = jnp.eye(SSM_GROUPS, dtype=F32)

    def blockdiag(v):
        x = v.transpose(0, 2, 1)
        return (x[:, :, None, :] * eye[:, None, :, None]).reshape(N_STATE, SSM_GROUPS * SSM_GROUP)

    return _interleave(blockdiag(c_re), blockdiag(-c_im), 0)


def _c_matrix_grad(d):
    def diag(v):
        return _diag_blocks(v, SSM_STATE, SSM_GROUP).transpose(0, 2, 1)

    dr, di = _deinterleave(d, 0)
    return diag(dr), -diag(di)


def _row(v):
    return v.reshape(1, -1)


def _ssm_inputs(p):
    rows = lambda v: v.reshape(N_STATE, -1)
    ldt = jnp.repeat(p["ssm_log_dt"], SSM_STATE).reshape(N_STATE, 1)
    return rows(p["ssm_a_re"]), rows(p["ssm_a_im"]), ldt, rows(p["ssm_b_re"]), rows(p["ssm_b_im"])


def _layer_fwd(x, mod, p, tag):
    d = x.shape[1]
    sh_m, sc_m, g_m, sh_f, sc_f, g_f = [_row(mod[i]) for i in range(6)]
    nm = lambda s: f"{s}_{tag}"

    def lnmod(x, sc, sh):
        return _norm(x)[0] * (1.0 + sc) + sh

    h1 = _rowwise(nm("lnmod1"), lnmod, [(x, 0, d)], [sc_m, sh_m], [(d, BF16)])
    proj = _mm(nm("proj"), h1, p["w_in"], "nn")
    t = min(ATT_TILE, x.shape[0])
    qs, kb, vb, kt3, vt3 = _qkv_prep(proj, t)
    att, car = _attn_fwd(qs, kb, vt3, t)
    y_sb = _mm(nm("sb_up"), att, p["w_sb_up"], "nn")

    lam_r, lam_i, bbr, bbi = _ssm_params_fwd(*_ssm_inputs(p))
    lam = _interleave(lam_r.reshape(1, N_STATE), lam_i.reshape(1, N_STATE), 1)
    bmat = _b_matrix(bbr, bbi).astype(BF16)
    cmat = _c_matrix(p["ssm_c_re"], p["ssm_c_im"]).astype(BF16)
    bu = _mm(nm("ssm_b"), proj, bmat, "nn", a_cols=(1536, 512))
    hst = _ssm_scan_fwd(bu, lam)
    yc = _mm(nm("ssm_c"), hst, cmat, "nn")

    def ssm_act(yc, u, dsk):
        y0 = yc + dsk * u
        return y0, _gelu(y0)

    y0, y1 = _rowwise(nm("ssm_act"), ssm_act, [(yc, 0, 512), (proj, 3, 512)], [_row(p["ssm_d"])], [(512, F32), (512, F32)])
    gl = _mm(nm("glu"), y1, p["w_glu"], "nn")
    y2 = _rowwise(nm("glu_act"), lambda y1, gl, b: y1 * jax.nn.sigmoid(gl + b), [(y1, 0, 512), (gl, 0, 512)],
                  [_row(p["b_glu"])], [(512, BF16)])
    y_ssm = _mm(nm("ssm_up"), y2, p["w_ssm_up"], "nn")

    def merge(gsb, gss, ysb, yss):
        return jax.nn.sigmoid(gsb) * ysb + jax.nn.sigmoid(gss) * yss

    merged = _rowwise(nm("merge"), merge, [(proj, 2, d), (proj, 3, d), (y_sb, 0, d), (y_ssm, 0, d)], [], [(d, BF16)])
    y = _mm(nm("out"), merged, p["w_out"], "nn")

    def resid_ln(x, y, g, lg, lb):
        return _norm(ALPHA * x + (1.0 + g) * y)[0] * lg + lb

    x1 = _rowwise(nm("ln1"), resid_ln, [(x, 0, d), (y, 0, d)], [g_m, _row(p["ln1_g"]), _row(p["ln1_b"])], [(d, F32)])
    h2 = _rowwise(nm("lnmod2"), lnmod, [(x1, 0, d)], [sc_f, sh_f], [(d, BF16)])
    f = _mm(nm("ffn_in"), h2, p["w_ffn_in"], "nn", out_dtype=BF16)
    fh = f.shape[1] // 2

    def swiglu(g, u):
        g = g.astype(F32)
        return g * jax.nn.sigmoid(g) * u.astype(F32)

    act = _rowwise(nm("swiglu"), swiglu, [(f, 0, fh), (f, 1, fh)], [], [(fh, BF16)])
    yf = _mm(nm("ffn_out"), act, p["w_ffn_out"], "nn")
    x2 = _rowwise(nm("ln2"), resid_ln, [(x1, 0, d), (yf, 0, d)], [g_f, _row(p["ln2_g"]), _row(p["ln2_b"])], [(d, F32)])
    saved = dict(x=x, h1=h1, proj=proj, qs=qs, kb=kb, vb=vb, kt3=kt3, car=car, att=att, y_sb=y_sb, lam=lam, bmat=bmat,
                 cmat=cmat, hst=hst, y0=y0, y1=y1, gl=gl, y2=y2, y_ssm=y_ssm, merged=merged, y=y, x1=x1, h2=h2, f=f,
                 act=act, yf=yf, t=t)
    return x2, saved


def _layer_bwd(dx2, mod, p, sv, layer, depth, stacked):
    d = dx2.shape[1]
    sh_m, sc_m, g_m, sh_f, sc_f, g_f = [_row(mod[i]) for i in range(6)]
    nm = lambda s: f"{s}_{layer}"
    grads = {}

    def weight_grad(n, a, b, **kw):
        grads[n] = _mm(nm("d" + n), a, b, "tn", out_dtype=BF16, into=(stacked.get(n), layer, depth), **kw)

    def resid_ln_bwd(x, y, dxo, g, lg):
        n, rstd = _norm(ALPHA * x + (1.0 + g) * y)
        dr = _norm_bwd(dxo * lg, n, rstd)
        return ALPHA * dr, (1.0 + g) * dr, _colsum(dxo * n), _colsum(dxo), _colsum(dr * y)

    def lnmod_bwd(x, dh, dxa, sc):
        n, rstd = _norm(x)
        return dxa + _norm_bwd(dh * (1.0 + sc), n, rstd), _colsum(dh * n), _colsum(dh)

    dx1a, dyf, grads["ln2_g"], grads["ln2_b"], dg_f = _rowwise(
        nm("ln2_bwd"), resid_ln_bwd, [(sv["x1"], 0, d), (sv["yf"], 0, d), (dx2, 0, d)], [g_f, _row(p["ln2_g"])],
        [(d, F32), (d, BF16)], [d, d, d])
    dact = _mm(nm("d_act"), dyf, p["w_ffn_out"], "nt")
    weight_grad("w_ffn_out", sv["act"], dyf)
    fh = sv["f"].shape[1] // 2

    def swiglu_bwd(g, u, da):
        g, u = g.astype(F32), u.astype(F32)
        sg = jax.nn.sigmoid(g)
        return jnp.concatenate([da * u * sg * (1.0 + g * (1.0 - sg)), da * g * sg], axis=1)

    df = _rowwise(nm("swiglu_bwd"), swiglu_bwd, [(sv["f"], 0, fh), (sv["f"], 1, fh), (dact, 0, fh)], [], [(2 * fh, BF16)])
    dh2 = _mm(nm("d_h2"), df, p["w_ffn_in"], "nt")
    weight_grad("w_ffn_in", sv["h2"], df)
    dx1, dsc_f, dsh_f = _rowwise(nm("lnmod2_bwd"), lnmod_bwd, [(sv["x1"], 0, d), (dh2, 0, d), (dx1a, 0, d)], [sc_f],
                                 [(d, F32)], [d, d])
    dxa, dy, grads["ln1_g"], grads["ln1_b"], dg_m = _rowwise(
        nm("ln1_bwd"), resid_ln_bwd, [(sv["x"], 0, d), (sv["y"], 0, d), (dx1, 0, d)], [g_m, _row(p["ln1_g"])],
        [(d, F32), (d, BF16)], [d, d, d])
    dmerged = _mm(nm("d_merged"), dy, p["w_out"], "nt")
    weight_grad("w_out", sv["merged"], dy)

    def merge_bwd(gsb, gss, ysb, yss, dm):
        s1, s2 = jax.nn.sigmoid(gsb), jax.nn.sigmoid(gss)
        return s1 * dm, s2 * dm, dm * ysb * s1 * (1.0 - s1), dm * yss * s2 * (1.0 - s2)

    dy_sb, dy_ssm, dg_sb, dg_ssm = _rowwise(
        nm("merge_bwd"), merge_bwd, [(sv["proj"], 2, d), (sv["proj"], 3, d), (sv["y_sb"], 0, d), (sv["y_ssm"], 0, d),
                                     (dmerged, 0, d)], [], [(d, BF16)] * 4)
    dy2 = _mm(nm("d_y2"), dy_ssm, p["w_ssm_up"], "nt")
    weight_grad("w_ssm_up", sv["y2"], dy_ssm)

    def glu_act_bwd(y1, gl, dy2, b):
        sg = jax.nn.sigmoid(gl + b)
        dgl = dy2 * y1 * sg * (1.0 - sg)
        return dy2 * sg, dgl, _colsum(dgl)

    dy1a, dgl, grads["b_glu"] = _rowwise(nm("glu_act_bwd"), glu_act_bwd, [(sv["y1"], 0, 512), (sv["gl"], 0, 512), (dy2, 0, 512)],
                                         [_row(p["b_glu"])], [(512, F32), (512, BF16)], [512])
    dy1b = _mm(nm("d_y1"), dgl, p["w_glu"], "nt")
    weight_grad("w_glu", sv["y1"], dgl)

    def ssm_act_bwd(y0, u, dy1a, dy1b, dsk):
        dy0 = (dy1a + dy1b) * _gelu_grad(y0)
        return dy0, dsk * dy0, _colsum(dy0 * u)

    dy0, du_a, grads["ssm_d"] = _rowwise(nm("ssm_act_bwd"), ssm_act_bwd,
                                         [(sv["y0"], 0, 512), (sv["proj"], 3, 512), (dy1a, 0, 512), (dy1b, 0, 512)],
                                         [_row(p["ssm_d"])], [(512, BF16), (512, F32)], [512])
    e = _mm(nm("ssm_e"), dy0, sv["cmat"], "nt")
    grads["ssm_c_re"], grads["ssm_c_im"] = _c_matrix_grad(_mm(nm("dw_ssm_c"), sv["hst"], dy0, "tn"))
    adj, dlam = _ssm_scan_bwd(e, sv["hst"], sv["lam"])
    du_b = _mm(nm("d_u"), adj, sv["bmat"], "nt")
    g_bbr, g_bbi = _b_matrix_grad(_mm(nm("dw_ssm_b"), sv["proj"], adj, "tn", a_cols=(1536, 512)))
    g_lr, g_li = _deinterleave(dlam[0:1], 1)
    da_re, da_im, dldt, db_re, db_im = _ssm_params_bwd(*_ssm_inputs(p), g_lr.reshape(N_STATE, 1), g_li.reshape(N_STATE, 1),
                                                       g_bbr, g_bbi)
    grads["ssm_a_re"] = da_re.reshape(SSM_GROUPS, SSM_STATE)
    grads["ssm_a_im"] = da_im.reshape(SSM_GROUPS, SSM_STATE)
    grads["ssm_log_dt"] = dldt.reshape(SSM_GROUPS)
    grads["ssm_b_re"] = db_re.reshape(SSM_GROUPS, SSM_STATE, SSM_GROUP)
    grads["ssm_b_im"] = db_im.reshape(SSM_GROUPS, SSM_STATE, SSM_GROUP)
    datt = _mm(nm("d_att"), dy_sb, p["w_sb_up"], "nt", out_dtype=BF16)
    weight_grad("w_sb_up", sv["att"], dy_sb)
    dqs, dk, dv = _attn_bwd(sv["qs"], datt, sv["kb"], sv["vb"], sv["kt3"], sv["car"], sv["t"])

    def dproj_cols(dqs, dk, dv, dua, dub, dgsb, dgss):
        return jnp.concatenate([dqs * (1.0 / math.sqrt(HEAD_DIM)), dk, dv, dua + dub, dgsb.astype(F32), dgss.astype(F32)],
                               axis=1)

    dproj = _rowwise(nm("dproj"), dproj_cols, [(dqs, 0, 512), (dk, 0, 512), (dv, 0, 512), (du_a, 0, 512), (du_b, 0, 512),
                                               (dg_sb, 0, d), (dg_ssm, 0, d)], [], [(2048 + 2 * d, BF16)])
    dh1 = _mm(nm("d_h1"), dproj, p["w_in"], "nt")
    weight_grad("w_in", sv["h1"], dproj)
    dx, dsc_m, dsh_m = _rowwise(nm("lnmod1_bwd"), lnmod_bwd, [(sv["x"], 0, d), (dh1, 0, d), (dxa, 0, d)], [sc_m],
                                [(d, F32)], [d, d])
    for k in ("ln1_g", "ln1_b", "ln2_g", "ln2_b", "ssm_d", "b_glu"):
        grads[k] = grads[k].reshape(-1)
    dmod = jnp.concatenate([dsh_m, dsc_m, dg_m, dsh_f, dsc_f, dg_f], axis=0)
    return dx, dmod, grads


def _loss_head(x, target):
    d = x.shape[1]

    def fn(x, tgt):
        err = x - tgt
        return err * (1.0 / d), _colsum(err * err) * (0.5 / d)

    return _rowwise("loss_head", fn, [(x, 0, d), (target, 0, d)], [], [(d, F32)], [d])


def _place():
    return lax.axis_index("x"), lax.axis_index("y"), lax.axis_index("c")


def _all_gather8(name, block):
    m_per, n = block.shape

    def body(x_ref, out_ref, send_sems, recv_sems, local_sem):
        x, y, c = _place()
        me, sibling = (x, y, c), (x, y, 1 - c)
        chips = [(1 - x, y), (x, 1 - y), (1 - x, 1 - y)]

        def rows(px, py, pc):
            return out_ref.at[pl.ds(pl.multiple_of((4 * px + 2 * py + pc) * m_per, 8), m_per), :]

        def copy(k, blk, to, src=None):
            return pltpu.make_async_remote_copy(src_ref=rows(*blk) if src is None else src, dst_ref=rows(*blk),
                                                send_sem=send_sems.at[k], recv_sem=recv_sems.at[k],
                                                device_id=to, device_id_type=MESH)

        mine = pltpu.make_async_copy(x_ref, rows(*me), local_sem)
        mine.start()
        first = [copy(0, me, sibling, src=x_ref)] + [copy(1 + j, me, (*chip, c), src=x_ref) for j, chip in enumerate(chips)]
        for cp in first:
            cp.start()
        passed = [copy(4 + j, (*chip, c), sibling) for j, chip in enumerate(chips)]
        for j, chip in enumerate(chips):
            copy(1 + j, (*chip, c), me).wait_recv()
            passed[j].start()
        copy(0, sibling, me).wait_recv()
        for j, chip in enumerate(chips):
            copy(4 + j, (*chip, 1 - c), me).wait_recv()
        for cp in first + passed:
            cp.wait_send()
        mine.wait()

    return pl.pallas_call(
        body, name=name, out_shape=jax.ShapeDtypeStruct((8 * m_per, n), block.dtype),
        in_specs=[pl.BlockSpec(memory_space=pltpu.VMEM)], out_specs=pl.BlockSpec(memory_space=pltpu.VMEM),
        scratch_shapes=[pltpu.SemaphoreType.DMA((7,)), pltpu.SemaphoreType.DMA((7,)), pltpu.SemaphoreType.DMA],
        compiler_params=_params(),
    )(block)


def _other_chips(x, y):
    return [(1 - x, y), (x, 1 - y), (1 - x, 1 - y)]


def _gather_weights(shards, by_rows):
    n = len(shards)

    def body(*refs):
        src, dst = refs[:n], refs[n:2 * n]
        send_sems, recv_sems, local_sems = refs[2 * n:]
        x, y, c = _place()

        def block(k, px, py):
            _, r, cols = shards[k].shape
            q = 2 * px + py
            if by_rows[k]:
                return dst[k].at[:, pl.ds(pl.multiple_of(q * r, 16), r), :]
            return dst[k].at[:, :, pl.ds(pl.multiple_of(q * cols, LANES), cols)]

        local = [pltpu.make_async_copy(src[k], block(k, x, y), local_sems.at[k]) for k in range(n)]
        for cp in local:
            cp.start()
        sends = []
        for k in range(n):
            for j, (px, py) in enumerate(_other_chips(x, y)):
                cp = pltpu.make_async_remote_copy(src_ref=src[k], dst_ref=block(k, x, y), send_sem=send_sems.at[k, j],
                                                  recv_sem=recv_sems.at[k, j], device_id=(px, py, c), device_id_type=MESH)
                cp.start()
                sends.append(cp)
        for k in range(n):
            for j, (px, py) in enumerate(_other_chips(x, y)):
                pltpu.make_async_remote_copy(src_ref=src[k], dst_ref=block(k, px, py), send_sem=send_sems.at[k, j],
                                             recv_sem=recv_sems.at[k, j], device_id=(px, py, c),
                                             device_id_type=MESH).wait_recv()
        for cp in sends:
            cp.wait_send()
        for cp in local:
            cp.wait()

    def whole(s, rows):
        l, r, cols = s.shape
        return jax.ShapeDtypeStruct((l, 4 * r, cols) if rows else (l, r, 4 * cols), s.dtype)

    any_spec = pl.BlockSpec(memory_space=pl.ANY)
    return pl.pallas_call(
        body, name="gather_weights", in_specs=[any_spec] * n, out_specs=[any_spec] * n,
        out_shape=[whole(s, rows) for s, rows in zip(shards, by_rows)],
        scratch_shapes=[pltpu.SemaphoreType.DMA((n, 3)), pltpu.SemaphoreType.DMA((n, 3)), pltpu.SemaphoreType.DMA((n,))],
        compiler_params=_params(),
    )(*shards)


def _peer(x, y, c, r):
    fx, fy, fc = (r >> 2) & 1, (r >> 1) & 1, r & 1
    return (x + fx - 2 * x * fx, y + fy - 2 * y * fy, c + fc - 2 * c * fc)


def _scatter_grads(grads, by_rows):
    n = len(grads)

    def half_shape(k):
        l, r, c = grads[k].shape
        return (l, r // 8, c) if by_rows[k] else (l, r // 2, c // 4)

    def body(*refs):
        src, dst = refs[:n], refs[n:2 * n]
        send_sems, recv_sems, local_sems = refs[2 * n:]
        x, y, c = _place()
        me = 4 * x + 2 * y + c

        def window(k, px, py, pc):
            _, hr, hc = half_shape(k)
            q = 2 * px + py
            if by_rows[k]:
                return src[k].at[:, pl.ds(pl.multiple_of((2 * q + pc) * hr, 16), hr), :]
            return src[k].at[:, pl.ds(pl.multiple_of(pc * hr, 16), hr), pl.ds(pl.multiple_of(q * hc, LANES), hc)]

        local = [pltpu.make_async_copy(window(k, x, y, c), dst[k].at[me], local_sems.at[k]) for k in range(n)]
        for cp in local:
            cp.start()
        sends = []
        for k in range(n):
            for r in range(1, 8):
                to = _peer(x, y, c, r)
                cp = pltpu.make_async_remote_copy(src_ref=window(k, *to), dst_ref=dst[k].at[me], send_sem=send_sems.at[k, r - 1],
                                                  recv_sem=recv_sems.at[k, r - 1], device_id=to, device_id_type=MESH)
                cp.start()
                sends.append(cp)
        for k in range(n):
            for r in range(1, 8):
                px, py, pc = _peer(x, y, c, r)
                pltpu.make_async_remote_copy(src_ref=window(k, x, y, c), dst_ref=dst[k].at[4 * px + 2 * py + pc],
                                             send_sem=send_sems.at[k, r - 1], recv_sem=recv_sems.at[k, r - 1],
                                             device_id=(px, py, pc), device_id_type=MESH).wait_recv()
        for cp in sends:
            cp.wait_send()
        for cp in local:
            cp.wait()

    any_spec = pl.BlockSpec(memory_space=pl.ANY)
    return pl.pallas_call(
        body, name="scatter_grads", in_specs=[any_spec] * n, out_specs=[any_spec] * n,
        out_shape=[jax.ShapeDtypeStruct((8, *half_shape(k)), grads[k].dtype) for k in range(n)],
        scratch_shapes=[pltpu.SemaphoreType.DMA((n, 7)), pltpu.SemaphoreType.DMA((n, 7)), pltpu.SemaphoreType.DMA((n,))],
        compiler_params=_params(),
    )(*grads)


def _sum_slots(name, parts):
    _, l, r, c = parts.shape
    tr = _pick(r, (256, 176, 128, 64, 32, 8))

    def body(p_ref, o_ref):
        acc = p_ref[0].astype(F32)
        for i in range(1, 8):
            acc = acc + p_ref[i].astype(F32)
        o_ref[...] = acc

    return pl.pallas_call(
        body, name=name, grid=(l, r // tr), in_specs=[pl.BlockSpec((8, 1, tr, c), lambda li, i: (0, li, i, 0))],
        out_specs=pl.BlockSpec((1, tr, c), lambda li, i: (li, i, 0)), out_shape=jax.ShapeDtypeStruct((l, r, c), F32),
        compiler_params=_params(dimension_semantics=("arbitrary", "arbitrary")),
    )(parts)


def _swap_halves(halves):
    n = len(halves)

    def body(*refs):
        src, dst = refs[:n], refs[n:2 * n]
        send_sems, recv_sems, local_sems = refs[2 * n:]
        x, y, c = _place()

        def place(k, pc):
            r = halves[k].shape[1]
            return dst[k].at[:, pl.ds(pl.multiple_of(pc * r, 8), r), :]

        local = [pltpu.make_async_copy(src[k], place(k, c), local_sems.at[k]) for k in range(n)]
        sends = [pltpu.make_async_remote_copy(src_ref=src[k], dst_ref=place(k, c), send_sem=send_sems.at[k],
                                              recv_sem=recv_sems.at[k], device_id=(x, y, 1 - c), device_id_type=MESH)
                 for k in range(n)]
        for cp in local + sends:
            cp.start()
        for k in range(n):
            pltpu.make_async_remote_copy(src_ref=src[k], dst_ref=place(k, 1 - c), send_sem=send_sems.at[k],
                                         recv_sem=recv_sems.at[k], device_id=(x, y, 1 - c), device_id_type=MESH).wait_recv()
        for cp in sends:
            cp.wait_send()
        for cp in local:
            cp.wait()

    any_spec = pl.BlockSpec(memory_space=pl.ANY)
    return pl.pallas_call(
        body, name="swap_halves", in_specs=[any_spec] * n, out_specs=[any_spec] * n,
        out_shape=[jax.ShapeDtypeStruct((h.shape[0], 2 * h.shape[1], h.shape[2]), F32) for h in halves],
        scratch_shapes=[pltpu.SemaphoreType.DMA((n,)), pltpu.SemaphoreType.DMA((n,)), pltpu.SemaphoreType.DMA((n,))],
        compiler_params=_params(),
    )(*halves)


def _adamw(name, w, g, m, v):
    shape = w.shape
    cols = shape[-1] if w.ndim > 1 and shape[-1] % LANES == 0 else w.size if w.size % LANES else LANES
    flat = lambda a: a.reshape(-1, cols)
    rows = w.size // cols
    tr = _pick(rows, [r for r in (512, 256, 128, 64, 32, 16, 8) if r * cols <= 256 * 1024]) if rows % 8 == 0 else rows

    def body(w_ref, g_ref, m_ref, v_ref, d_ref, nm_ref, nv_ref):
        gg = g_ref[...]
        nm = ADAM_B1 * m_ref[...] + (1.0 - ADAM_B1) * gg
        nv = ADAM_B2 * v_ref[...] + (1.0 - ADAM_B2) * (gg * gg)
        m_hat = nm / (1.0 - ADAM_B1 ** ADAM_STEP)
        v_hat = nv / (1.0 - ADAM_B2 ** ADAM_STEP)
        d_ref[...] = -ADAM_LR * (m_hat / (jnp.sqrt(v_hat) + ADAM_EPS) + ADAM_WD * w_ref[...])
        nm_ref[...] = nm
        nv_ref[...] = nv

    spec = pl.BlockSpec((tr, cols), lambda i: (i, 0))
    out = pl.pallas_call(
        body, name=name, grid=(rows // tr,), in_specs=[spec] * 4, out_specs=[spec] * 3,
        out_shape=[jax.ShapeDtypeStruct((rows, cols), F32)] * 3,
        compiler_params=_params(dimension_semantics=("arbitrary",)),
    )(flat(w), flat(g), flat(m), flat(v))
    return tuple(o.reshape(shape) for o in out)


WEIGHTS = ["w_ada", "b_ada", "w_in", "w_sb_up", "ssm_a_re", "ssm_a_im", "ssm_log_dt", "ssm_b_re", "ssm_b_im", "ssm_c_re",
           "ssm_c_im", "ssm_d", "w_glu", "b_glu", "w_ssm_up", "w_out", "ln1_g", "ln1_b", "w_ffn_in", "w_ffn_out", "ln2_g",
           "ln2_b"]
COL_SPLIT = ["w_in", "w_sb_up", "w_ssm_up", "w_ffn_in"]
ROW_SPLIT = ["w_glu", "w_out", "w_ffn_out"]
SMALL = ["ssm_a_re", "ssm_a_im", "ssm_log_dt", "ssm_b_re", "ssm_b_im", "ssm_c_re", "ssm_c_im", "ssm_d", "b_glu", "ln1_g",
         "ln1_b", "ln2_g", "ln2_b"]
SLAB_COLS = 1024


def _cast_bf16(name, w):
    shape = w.shape
    flat = w.reshape(-1, shape[-1])
    rows, cols = flat.shape
    tr = _pick(rows, (512, 256, 128, 64, 8))

    def body(w_ref, o_ref):
        o_ref[...] = w_ref[...].astype(BF16)

    spec = pl.BlockSpec((tr, cols), lambda i: (i, 0))
    return pl.pallas_call(body, name=name, grid=(rows // tr,), in_specs=[spec], out_specs=spec,
                          out_shape=jax.ShapeDtypeStruct((rows, cols), BF16),
                          compiler_params=_params(dimension_semantics=("arbitrary",)))(flat).reshape(shape)


def _silu_rows(name, c):
    def body(c_ref, o_ref):
        v = c_ref[...]
        o_ref[...] = v * jax.nn.sigmoid(v)

    return pl.pallas_call(body, name=name, out_shape=jax.ShapeDtypeStruct(c.shape, F32), compiler_params=_params())(c)


def _pad_rows(v, mult=8):
    flat = v.reshape(-1)
    per = mult * SLAB_COLS
    total = -(-flat.size // per) * per
    return jnp.pad(flat, (0, total - flat.size)).reshape(-1, SLAB_COLS)


def kernel(x, c, w_ada, b_ada, w_in, w_sb_up, ssm_a_re, ssm_a_im, ssm_log_dt, ssm_b_re, ssm_b_im, ssm_c_re, ssm_c_im, ssm_d, w_glu, b_glu, w_ssm_up, w_out, ln1_g, ln1_b, w_ffn_in, w_ffn_out, ln2_g, ln2_b, loss_target, m_w_ada, m_b_ada, m_w_in, m_w_sb_up, m_ssm_a_re, m_ssm_a_im, m_ssm_log_dt, m_ssm_b_re, m_ssm_b_im, m_ssm_c_re, m_ssm_c_im, m_ssm_d, m_w_glu, m_b_glu, m_w_ssm_up, m_w_out, m_ln1_g, m_ln1_b, m_w_ffn_in, m_w_ffn_out, m_ln2_g, m_ln2_b, v_w_ada, v_b_ada, v_w_in, v_w_sb_up, v_ssm_a_re, v_ssm_a_im, v_ssm_log_dt, v_ssm_b_re, v_ssm_b_im, v_ssm_c_re, v_ssm_c_im, v_ssm_d, v_w_glu, v_b_glu, v_w_ssm_up, v_w_out, v_ln1_g, v_ln1_b, v_w_ffn_in, v_w_ffn_out, v_ln2_g, v_ln2_b):
    args = dict(locals())
    w = {n: args[n] for n in WEIGHTS}
    mom = {n: args["m_" + n] for n in WEIGHTS}
    var = {n: args["v_" + n] for n in WEIGHTS}
    depth, d = w_ada.shape[0], x.shape[-1]
    xi, yi, ci = _place()
    me, chip = 4 * xi + 2 * yi + ci, 2 * xi + yi
    ada_cols = w_ada.shape[2]

    big = COL_SPLIT + ROW_SPLIT
    by_rows = [n in ROW_SPLIT for n in big]
    full = dict(zip(big, _gather_weights([_cast_bf16(f"cast_{n}", w[n]) for n in big], by_rows)))

    c_all = _all_gather8("gather_c", jnp.pad(c, ((0, 7), (0, 0))))[::8]
    c_act = _silu_rows("silu_c", c_all)
    b_cols = lax.dynamic_slice_in_dim(b_ada, chip * ada_cols, ada_cols, axis=1)
    mod_part = jnp.concatenate([_small_mm(f"mod_{l}", c_act, w_ada[l], "nn") + b_cols[l][None] for l in range(depth)], axis=0)
    mod_all = _all_gather8("gather_mod", mod_part).reshape(4, 2, depth, 8, ada_cols)[:, 0]
    mod_mine = lax.dynamic_index_in_dim(mod_all, me, axis=2, keepdims=False)
    mod = mod_mine.transpose(1, 0, 2).reshape(depth, 6, d)

    layer_w = [{**{n: full[n][l] for n in big}, **{n: w[n][l] for n in SMALL}} for l in range(depth)]
    h, saved = x[0], []
    for l in range(depth):
        h, sv = _layer_fwd(h, mod[l], layer_w[l], str(l))
        saved.append(sv)
    dh, loss_cols = _loss_head(h, loss_target[0])
    loss = lax.psum(jnp.sum(loss_cols), ("x", "y", "c"))
    dmods, lgrads, stacked = [None] * depth, [None] * depth, {}
    for l in reversed(range(depth)):
        dh, dmods[l], lgrads[l] = _layer_bwd(dh, mod[l], layer_w[l], saved[l], l, depth, stacked)
        stacked = {n: lgrads[l][n] for n in big}
    grad_x = dh[None]

    parts = _scatter_grads([stacked[n] for n in big], by_rows)
    halves = [_sum_slots(f"sum_{n}", p) for n, p in zip(big, parts)]
    grad = dict(zip(big, _swap_halves(halves)))

    pieces = [jnp.stack(dmods)] + [jnp.stack([lgrads[l][n] for l in range(depth)]) for n in SMALL]
    slab = jnp.concatenate([_pad_rows(p) for p in pieces], axis=0)
    slabs = _all_gather8("gather_small", slab).reshape(8, 1, *slab.shape)
    total = _sum_slots("sum_small", slabs)[0]
    row = _pad_rows(pieces[0]).shape[0]
    for n, p in zip(SMALL, pieces[1:]):
        rows = _pad_rows(p).shape[0]
        grad[n] = total[row:row + rows].reshape(-1)[:p.size].reshape(p.shape)
        row += rows
    dmod_rows = _pad_rows(pieces[0]).shape[0]
    dmod_all = slabs[:, 0, :dmod_rows].reshape(8, -1)[:, :depth * 6 * d].reshape(8, depth, 4, ada_cols)
    dmod_cols = lax.dynamic_index_in_dim(dmod_all, chip, axis=2, keepdims=False)
    grad["w_ada"] = jnp.stack([_small_mm(f"dw_ada_{l}", c_act, dmod_cols[:, l], "tn") for l in range(depth)])
    dmod_sum = _sum_slots("sum_dmod", slabs[:, :, :dmod_rows])[0]
    grad["b_ada"] = dmod_sum.reshape(-1)[:depth * 6 * d].reshape(depth, 6 * d)

    delta, new_m, new_v = {}, {}, {}
    for n in WEIGHTS:
        delta[n], new_m[n], new_v[n] = _adamw(f"adamw_{n}", w[n], grad[n], mom[n], var[n])
    return (loss, grad_x, *[grad[n] for n in WEIGHTS], *[delta[n] for n in WEIGHTS], *[new_m[n] for n in WEIGHTS],
            *[new_v[n] for n in WEIGHTS])
```

```python
import functools
import math

import jax
import jax.numpy as jnp
from jax import lax
from jax.experimental import pallas as pl
from jax.experimental.pallas import tpu as pltpu

F32 = jnp.float32
BF16 = jnp.bfloat16
MESH = pl.DeviceIdType.MESH

LANES = 128
HEAD_DIM = 64
ATT_TILE = 256
SSM_GROUPS, SSM_STATE, SSM_GROUP = 32, 64, 16
N_STATE = SSM_GROUPS * SSM_STATE
SSM_BLOCKS = SSM_GROUPS * SSM_GROUP // LANES
LN_EPS = 1e-5
DEPTH = 2
ALPHA = (2 * DEPTH) ** 0.25
ADAM_LR, ADAM_B1, ADAM_B2, ADAM_EPS, ADAM_WD, ADAM_STEP = 0.001, 0.9, 0.999, 1e-08, 0.01, 10
VMEM_LIMIT = 56 * 1024 * 1024
GELU_K = math.sqrt(2.0 / math.pi)
GELU_C = 0.044715


def _params(**kw):
    return pltpu.CompilerParams(vmem_limit_bytes=VMEM_LIMIT, **kw)


def _pick(n, prefs):
    for p in prefs:
        if n % p == 0:
            return p
    return n


def _rowwise(name, fn, rows, vecs, outs, sums=(), tm=None):
    s = rows[0][0].shape[0]
    tm = tm or _pick(s, (256, 128, 64, 8))
    nin, no, ns = len(rows) + len(vecs), len(outs), len(sums)

    def body(*refs):
        res = fn(*[r[...] for r in refs[:nin]])
        res = res if isinstance(res, tuple) else (res,)
        for r, v in zip(refs[nin:nin + no], res[:no]):
            r[...] = v.astype(r.dtype)
        if ns:
            @pl.when(pl.program_id(0) == 0)
            def _():
                for r in refs[nin + no:]:
                    r[...] = jnp.zeros_like(r)
            for r, v in zip(refs[nin + no:], res[no:]):
                r[...] += v

    in_specs = [pl.BlockSpec((tm, w), lambda i, cb=cb: (i, cb)) for _, cb, w in rows]
    in_specs += [pl.BlockSpec(v.shape, lambda i: (0, 0)) for v in vecs]
    out_specs = [pl.BlockSpec((tm, w), lambda i: (i, 0)) for w, _ in outs]
    out_specs += [pl.BlockSpec((1, w), lambda i: (0, 0)) for w in sums]
    out_shape = [jax.ShapeDtypeStruct((s, w), dt) for w, dt in outs]
    out_shape += [jax.ShapeDtypeStruct((1, w), F32) for w in sums]
    res = pl.pallas_call(
        body, name=name, grid=(s // tm,), in_specs=in_specs, out_specs=out_specs, out_shape=out_shape,
        compiler_params=_params(dimension_semantics=("arbitrary",)),
    )(*[a for a, _, _ in rows], *vecs)
    return res[0] if len(res) == 1 else tuple(res)


MM_TILES = (1408, 1024, 512, 256, 128)


def _mm(name, a, b, mode, out_dtype=F32, a_cols=None, into=None, diag=0):
    a_off, a_w = a_cols if a_cols else (0, a.shape[1])
    if mode == "nn":
        m, k, n = a.shape[0], a_w, b.shape[1]
    elif mode == "nt":
        m, k, n = a.shape[0], a_w, b.shape[0]
    else:
        k, m, n = a.shape[0], a_w, b.shape[1]
    tm = _pick(m, MM_TILES if mode == "tn" else MM_TILES[1:])
    tn = _pick(n, MM_TILES)
    tk = _pick(k, MM_TILES)
    if diag and mode == "tn":
        tm, tn = m // diag, n // diag
    elif diag:
        tk, tn = k // diag, n // diag
    nk = 1 if diag and mode != "tn" else k // tk
    dims = {"nn": ((1,), (0,)), "nt": ((1,), (1,)), "tn": ((0,), (0,))}[mode]

    def body(a_ref, b_ref, *rest):
        o_ref = rest[-2] if nk > 1 else rest[-1]
        prod = lax.dot_general(a_ref[...].astype(BF16), b_ref[...].astype(BF16), (dims, ((), ())),
                               preferred_element_type=F32)
        if nk == 1:
            o_ref[...] = prod.astype(o_ref.dtype)
            return
        acc_ref = rest[-1]
        kk = pl.program_id(2)

        @pl.when(kk == 0)
        def _():
            acc_ref[...] = prod

        @pl.when(kk > 0)
        def _():
            acc_ref[...] += prod

        @pl.when(kk == nk - 1)
        def _():
            o_ref[...] = acc_ref[...].astype(o_ref.dtype)

    kb = (lambda i, j, kk: j) if diag and mode != "tn" else (lambda i, j, kk: kk)
    nb = (lambda i, j, kk: i) if diag and mode == "tn" else (lambda i, j, kk: j)
    if mode == "tn":
        a_spec = pl.BlockSpec((tk, tm), lambda i, j, kk: (kb(i, j, kk), a_off // tm + i))
    else:
        a_spec = pl.BlockSpec((tm, tk), lambda i, j, kk: (i, a_off // tk + kb(i, j, kk)))
    if mode == "nt":
        b_spec = pl.BlockSpec((tn, tk), lambda i, j, kk: (nb(i, j, kk), kb(i, j, kk)))
    else:
        b_spec = pl.BlockSpec((tk, tn), lambda i, j, kk: (kb(i, j, kk), nb(i, j, kk)))
    in_specs, operands, aliases = [a_spec, b_spec], [a, b], {}
    n_out = tn if diag and mode == "tn" else n
    if into is None:
        out_spec = pl.BlockSpec((tm, tn), lambda i, j, kk: (i, j))
        out_shape = jax.ShapeDtypeStruct((m, n_out), out_dtype)
    else:
        buf, slab, count = into
        out_spec = pl.BlockSpec((None, tm, tn), lambda i, j, kk: (slab, i, j))
        out_shape = jax.ShapeDtypeStruct((count, m, n_out), out_dtype)
        if buf is not None:
            in_specs.append(pl.BlockSpec(memory_space=pl.ANY))
            operands.append(buf)
            aliases = {2: 0}
    return pl.pallas_call(
        body, name=name, grid=(m // tm, n_out // tn, nk), in_specs=in_specs, out_specs=out_spec, out_shape=out_shape,
        scratch_shapes=[pltpu.VMEM((tm, tn), F32)] if nk > 1 else [], input_output_aliases=aliases,
        compiler_params=_params(dimension_semantics=("arbitrary", "arbitrary", "arbitrary")),
    )(*operands)


def _small_mm(name, a, b, mode):
    dims = {"nn": ((1,), (0,)), "tn": ((0,), (0,))}[mode]
    m = a.shape[0] if mode == "nn" else a.shape[1]

    def body(a_ref, b_ref, o_ref):
        o_ref[...] = lax.dot_general(a_ref[...], b_ref[...], (dims, ((), ())), precision=lax.Precision.HIGHEST,
                                     preferred_element_type=F32)

    return pl.pallas_call(body, name=name, out_shape=jax.ShapeDtypeStruct((m, b.shape[1]), F32),
                          compiler_params=_params())(a, b)


def _norm(x):
    mu = jnp.mean(x, axis=-1, keepdims=True)
    xc = x - mu
    rstd = lax.rsqrt(jnp.mean(xc * xc, axis=-1, keepdims=True) + LN_EPS)
    return xc * rstd, rstd


def _norm_bwd(dn, n, rstd):
    return rstd * (dn - jnp.mean(dn, axis=-1, keepdims=True) - n * jnp.mean(dn * n, axis=-1, keepdims=True))


def _colsum(v):
    return jnp.sum(v, axis=0, keepdims=True)


def _gelu(x):
    return 0.5 * x * (1.0 + jnp.tanh(GELU_K * (x + GELU_C * x * x * x)))


def _gelu_grad(x):
    t = jnp.tanh(GELU_K * (x + GELU_C * x * x * x))
    return 0.5 * (1.0 + t) + 0.5 * x * (1.0 - t * t) * GELU_K * (1.0 + 3.0 * GELU_C * x * x)


def _log_sigmoid_parts(z):
    lb = jnp.minimum(z, 0.0) - jnp.log(1.0 + jnp.exp(-jnp.abs(z)))
    return lb, lb - z


def _qkv_prep(proj, t):
    s = proj.shape[0]
    nb, nhp = s // t, 512 // LANES

    def body(q_ref, k_ref, v_ref, qs_ref, kb_ref, vb_ref, kt_ref, vt_ref):
        qs_ref[...] = (q_ref[...] * (1.0 / math.sqrt(HEAD_DIM))).astype(BF16)
        k, v = k_ref[...], v_ref[...]
        kb_ref[...] = k.astype(BF16)
        vb_ref[...] = v.astype(BF16)
        for hp in range(nhp):
            kt_ref[hp, 0] = k[:, hp * LANES:(hp + 1) * LANES].T.astype(BF16)
            vt_ref[hp, 0] = v[:, hp * LANES:(hp + 1) * LANES].T.astype(BF16)

    col = lambda cb: pl.BlockSpec((t, 512), lambda i, cb=cb: (i, cb))
    row_out = pl.BlockSpec((t, 512), lambda i: (i, 0))
    t_out = pl.BlockSpec((nhp, 1, LANES, t), lambda i: (0, i, 0, 0))
    return pl.pallas_call(
        body, name="qkv_prep", grid=(nb,), in_specs=[col(0), col(1), col(2)],
        out_specs=[row_out, row_out, row_out, t_out, t_out],
        out_shape=[jax.ShapeDtypeStruct((s, 512), BF16)] * 3 + [jax.ShapeDtypeStruct((nhp, nb, LANES, t), BF16)] * 2,
        compiler_params=_params(dimension_semantics=("arbitrary",)),
    )(proj, proj, proj)


def _tile_masks(t):
    row = lax.broadcasted_iota(jnp.int32, (t, t), 0)
    col = lax.broadcasted_iota(jnp.int32, (t, t), 1)
    return row, col


DEAD_LOG_WEIGHT = -110.0


def _walk_down(i, tiles, state, alive):
    st = lax.cond(i == 0, lambda s_: tiles([i], s_, [True]), lambda s_: tiles([i, i - 1], s_, [True, False]), state)
    n = jnp.maximum(i - 1, 0)

    def pair(c):
        return c[0] + 1, tiles([i - 2 - 2 * c[0], i - 3 - 2 * c[0]], c[1], [False, False])

    p, st = lax.while_loop(lambda c: (c[0] < n // 2) & alive(c[1]), pair, (jnp.int32(0), st))
    return lax.cond((n % 2 == 1) & (p == n // 2) & alive(st), lambda s_: tiles([0], s_, [False]), lambda s_: s_, st)


def _walk_up(i, first, tiles, state):
    n = jnp.maximum(i - 1 - first, 0)
    st = lax.fori_loop(0, n // 2, lambda p, s_: tiles([first + 2 * p, first + 2 * p + 1], s_, [False, False]), state)
    st = lax.cond(n % 2 == 1, lambda s_: tiles([i - 2], s_, [False]), lambda s_: s_, st)
    return lax.cond(i == 0, lambda s_: tiles([i], s_, [True]), lambda s_: tiles([i - 1, i], s_, [False, True]), st)


def _nt(a, b):
    return lax.dot_general(a, b, (((1,), (1,)), ((), ())), preferred_element_type=F32)


def _nn(a, b):
    return jnp.dot(a, b, preferred_element_type=F32)


def _attn_fwd(qs, k, vt3, t):
    s = qs.shape[0]
    nb, nhp = s // t, qs.shape[1] // LANES

    def body(q_ref, k_ref, vt_ref, o_ref, car_ref):
        i = pl.program_id(1)
        q2 = q_ref[...]
        lane_q = lax.broadcasted_iota(jnp.int32, q2.shape, 1)
        row, col = _tile_masks(t)
        later = (col > row).astype(BF16)
        valid = row < col
        orow = lax.broadcasted_iota(jnp.int32, (LANES, t), 0)
        car_ref[...] = jnp.full(car_ref.shape, 2.0 * DEAD_LOG_WEIGHT, F32)
        qh = [jnp.where((lane_q < HEAD_DIM) == (hh == 0), q2, jnp.zeros_like(q2)) for hh in range(2)]

        def tiles(js, state, diagonal):
            chains = [(n, hh) for n in range(len(js)) for hh in range(2)]
            kb = [k_ref[pl.ds(pl.multiple_of(j * t, t), t), :] for j in js]
            z = {ch: _nt(kb[ch[0]], qh[ch[1]]) for ch in chains}
            lb, aft, csum = {}, {}, {}
            for ch in chains:
                lb[ch], l1m = _log_sigmoid_parts(z[ch])
                if diagonal[ch[0]]:
                    l1m = jnp.where(valid, l1m, 0.0)
                aft[ch] = _nn(later, l1m.astype(BF16))
                csum[ch] = _colsum(l1m)
            state = list(state)
            for ch in chains:
                n, hh = ch
                c_after, acc = state[hh]
                w = jnp.exp(lb[ch] + aft[ch] + c_after)
                if diagonal[ch[0]]:
                    w = jnp.where(valid, w, 0.0)
                car_ref[hh, pl.ds(js[n], 1), :] = c_after
                state[hh] = (c_after + csum[ch], acc + _nn(vt_ref[0, js[n]], w.astype(BF16)))
            return tuple(state)

        def alive(state):
            return jnp.max(jnp.maximum(state[0][0], state[1][0])) >= DEAD_LOG_WEIGHT

        zero = (jnp.zeros((1, t), F32), jnp.zeros((LANES, t), F32))
        (_, acc0), (_, acc1) = _walk_down(i, tiles, (zero, zero), alive)
        o_ref[...] = jnp.where(orow < HEAD_DIM, acc0, acc1).T.astype(o_ref.dtype)

    return pl.pallas_call(
        body, name="attn_fwd", grid=(nhp, nb),
        in_specs=[pl.BlockSpec((t, LANES), lambda hp, i: (i, hp)),
                  pl.BlockSpec((s, LANES), lambda hp, i: (0, hp)),
                  pl.BlockSpec((1, nb, LANES, t), lambda hp, i: (hp, 0, 0, 0))],
        out_specs=[pl.BlockSpec((t, LANES), lambda hp, i: (i, hp)),
                   pl.BlockSpec((2, nb, t), lambda hp, i: (hp, 0, i))],
        out_shape=[jax.ShapeDtypeStruct((s, nhp * LANES), BF16), jax.ShapeDtypeStruct((2 * nhp, nb, s), F32)],
        compiler_params=_params(dimension_semantics=("arbitrary", "arbitrary")),
    )(qs, k, vt3)


def _attn_bwd(qs, do, k, v, kt3, car, t):
    s = qs.shape[0]
    nb, nhp = s // t, qs.shape[1] // LANES

    def body(q_ref, do_ref, k_ref, v_ref, kt_ref, car_ref, dq_ref, dk_ref, dv_ref):
        i = pl.program_id(1)

        @pl.when(i == 0)
        def _():
            dk_ref[...] = jnp.zeros_like(dk_ref)
            dv_ref[...] = jnp.zeros_like(dv_ref)

        q2, do2 = q_ref[...], do_ref[...]
        lane_q = lax.broadcasted_iota(jnp.int32, q2.shape, 1)
        row, col = _tile_masks(t)
        later = (col > row).astype(BF16)
        earlier = (col < row).astype(BF16)
        valid = row < col
        orow = lax.broadcasted_iota(jnp.int32, (LANES, t), 0)
        head = [(lane_q < HEAD_DIM) == (hh == 0) for hh in range(2)]
        qh = [jnp.where(hm, q2, jnp.zeros_like(q2)) for hm in head]
        doh = [jnp.where(hm, do2, jnp.zeros_like(do2)) for hm in head]

        def tiles(js, state, diagonal):
            chains = [(n, hh) for n in range(len(js)) for hh in range(2)]
            rows = [pl.ds(pl.multiple_of(j * t, t), t) for j in js]
            kb = [k_ref[r, :] for r in rows]
            vb = [v_ref[r, :] for r in rows]
            z = {ch: _nt(kb[ch[0]], qh[ch[1]]) for ch in chains}
            dw = {ch: _nt(vb[ch[0]], doh[ch[1]]) for ch in chains}
            lb, beta, aft = {}, {}, {}
            for ch in chains:
                lb[ch], l1m = _log_sigmoid_parts(z[ch])
                beta[ch] = jnp.exp(lb[ch])
                if diagonal[ch[0]]:
                    l1m = jnp.where(valid, l1m, 0.0)
                aft[ch] = _nn(later, l1m.astype(BF16))
            w, g, gsum, g_in = {}, {}, {}, {}
            for ch in chains:
                n, hh = ch
                w[ch] = jnp.exp(lb[ch] + aft[ch] + car_ref[hh, pl.ds(js[n], 1), :])
                if diagonal[ch[0]]:
                    w[ch] = jnp.where(valid, w[ch], 0.0)
                g[ch] = dw[ch] * w[ch]
                g_in[ch] = _nn(earlier, g[ch].astype(BF16))
                gsum[ch] = _colsum(g[ch])
            state = list(state)
            dk_t, dv_t = [None] * len(js), [None] * len(js)
            for ch in chains:
                n, hh = ch
                c_g, dqt = state[hh]
                dz = g[ch] - beta[ch] * (g[ch] + g_in[ch] + c_g)
                if diagonal[ch[0]]:
                    dz = jnp.where(valid, dz, 0.0)
                dzb, wb = dz.astype(BF16), w[ch].astype(BF16)
                dk_h, dv_h = _nn(dzb, qh[hh]), _nn(wb, doh[hh])
                dk_t[n] = dk_h if dk_t[n] is None else dk_t[n] + dk_h
                dv_t[n] = dv_h if dv_t[n] is None else dv_t[n] + dv_h
                state[hh] = (c_g + gsum[ch], dqt + _nn(kt_ref[0, js[n]], dzb))
            for n in range(len(js)):
                dk_ref[rows[n], :] += dk_t[n]
                dv_ref[rows[n], :] += dv_t[n]
            return tuple(state)

        reach = jnp.max(jnp.max(car_ref[...], axis=2, keepdims=True), axis=0)
        dead = (reach < DEAD_LOG_WEIGHT) & (lax.broadcasted_iota(jnp.int32, reach.shape, 0) < i)
        first = jnp.sum(jnp.where(dead, 1.0, 0.0)).astype(jnp.int32)
        zero = (jnp.zeros((1, t), F32), jnp.zeros((LANES, t), F32))
        (_, dq0), (_, dq1) = _walk_up(i, first, tiles, (zero, zero))
        dq_ref[...] = jnp.where(orow < HEAD_DIM, dq0, dq1).T

    tile_spec = pl.BlockSpec((t, LANES), lambda hp, i: (i, hp))
    whole = pl.BlockSpec((s, LANES), lambda hp, i: (0, hp))
    return pl.pallas_call(
        body, name="attn_bwd", grid=(nhp, nb),
        in_specs=[tile_spec, tile_spec, whole, whole,
                  pl.BlockSpec((1, nb, LANES, t), lambda hp, i: (hp, 0, 0, 0)),
                  pl.BlockSpec((2, nb, t), lambda hp, i: (hp, 0, i))],
        out_specs=[tile_spec, whole, whole],
        out_shape=[jax.ShapeDtypeStruct((s, nhp * LANES), F32)] * 3,
        compiler_params=_params(dimension_semantics=("arbitrary", "arbitrary")),
    )(qs, do, k, v, kt3, car)


SCAN_LANES = 1024
SCAN_ROWS = 8


def _scan_chunks(v):
    n = v.shape[1] // (2 * LANES)
    return [(v[:, c * 2 * LANES:c * 2 * LANES + LANES], v[:, c * 2 * LANES + LANES:(c + 1) * 2 * LANES]) for c in range(n)]


def _scan_tables(lr, li, reverse):
    if reverse:
        li = -li
    row = lax.broadcasted_iota(jnp.int32, (SCAN_ROWS, LANES), 0)
    powers = [(lr, li)]
    for _ in range(SCAN_ROWS - 1):
        pr, pi = powers[-1]
        powers.append((pr * lr - pi * li, pr * li + pi * lr))
    levels = []
    for d in (1, 2, 4):
        keep = (row < SCAN_ROWS - d) if reverse else (row >= d)
        levels.append((SCAN_ROWS - d if reverse else d,
                       (jnp.where(keep, powers[d - 1][0], 0.0), jnp.where(keep, powers[d - 1][1], 0.0))))
    pr = pi = jnp.zeros((SCAN_ROWS, LANES), F32)
    for r in range(SCAN_ROWS):
        steps = SCAN_ROWS - r if reverse else r + 1
        pr = jnp.where(row == r, powers[steps - 1][0], pr)
        pi = jnp.where(row == r, powers[steps - 1][1], pi)
    return levels, (pr, pi)


def _ssm_scan_fwd(bu, lam):
    s, w = bu.shape
    tt = _pick(s, (512, 256, 128, 8))
    nt = s // tt

    def body(x_ref, lam_ref, h_ref, st_ref):
        @pl.when(pl.program_id(1) == 0)
        def _():
            st_ref[...] = jnp.zeros_like(st_ref)

        tables = [_scan_tables(lr, li, reverse=False) for lr, li in _scan_chunks(lam_ref[...])]

        def tile(it, last):
            r0 = pl.multiple_of(it * SCAN_ROWS, SCAN_ROWS)
            last, parts = list(last), []
            for c, (xr, xi) in enumerate(_scan_chunks(x_ref[pl.ds(r0, SCAN_ROWS), :])):
                levels, (pr, pi) = tables[c]
                for d, (ar, ai) in levels:
                    sr, si = pltpu.roll(xr, d, 0), pltpu.roll(xi, d, 0)
                    xr, xi = xr + ar * sr - ai * si, xi + ar * si + ai * sr
                br, bi = last[2 * c], last[2 * c + 1]
                hr = xr + pr * br - pi * bi
                hi = xi + pr * bi + pi * br
                last[2 * c], last[2 * c + 1] = hr[SCAN_ROWS - 1:], hi[SCAN_ROWS - 1:]
                parts += [hr, hi]
            h_ref[pl.ds(r0, SCAN_ROWS), :] = jnp.concatenate(parts, axis=1)
            return tuple(last)

        st = st_ref[0:1, :]
        init = tuple(st[:, c * LANES:(c + 1) * LANES] for c in range(SCAN_LANES // LANES))
        fin = lax.fori_loop(0, tt // SCAN_ROWS, tile, init)
        st_ref[0:1, :] = jnp.concatenate(fin, axis=1)

    return pl.pallas_call(
        body, name="ssm_scan_fwd", grid=(w // SCAN_LANES, nt),
        in_specs=[pl.BlockSpec((tt, SCAN_LANES), lambda lc, i: (i, lc)),
                  pl.BlockSpec((1, SCAN_LANES), lambda lc, i: (0, lc))],
        out_specs=pl.BlockSpec((tt, SCAN_LANES), lambda lc, i: (i, lc)),
        out_shape=jax.ShapeDtypeStruct((s, w), F32),
        scratch_shapes=[pltpu.VMEM((SCAN_ROWS, SCAN_LANES), F32)],
        compiler_params=_params(dimension_semantics=("arbitrary", "arbitrary")),
    )(bu, lam)


def _ssm_scan_bwd(e, h, lam):
    s, w = e.shape
    tt = _pick(s, (512, 256, 128, 8))
    nt = s // tt

    def body(e_ref, h_ref, lam_ref, a_ref, dlam_ref, st_ref):
        @pl.when(pl.program_id(1) == 0)
        def _():
            st_ref[...] = jnp.zeros_like(st_ref)
            dlam_ref[...] = jnp.zeros_like(dlam_ref)

        tables = [_scan_tables(lr, li, reverse=True) for lr, li in _scan_chunks(lam_ref[...])]
        nch = len(tables)
        row = lax.broadcasted_iota(jnp.int32, (SCAN_ROWS, LANES), 0)

        def tile(it, carry):
            r0 = pl.multiple_of((tt // SCAN_ROWS - 1 - it) * SCAN_ROWS, SCAN_ROWS)
            e_c = _scan_chunks(e_ref[pl.ds(r0, SCAN_ROWS), :])
            h_c = _scan_chunks(h_ref[pl.ds(r0, SCAN_ROWS), :])
            carry, parts = list(carry), []
            for c in range(nch):
                (yr, yi), (hr, hi) = e_c[c], h_c[c]
                levels, (pr, pi) = tables[c]
                for shift, (lr, li) in levels:
                    sr, si = pltpu.roll(yr, shift, 0), pltpu.roll(yi, shift, 0)
                    yr, yi = yr + lr * sr - li * si, yi + lr * si + li * sr
                nr, ni, dr, di = carry[4 * c:4 * c + 4]
                ar = yr + pr * nr - pi * ni
                ai = yi + pr * ni + pi * nr
                nxr = jnp.where(row == SCAN_ROWS - 1, nr, pltpu.roll(ar, SCAN_ROWS - 1, 0))
                nxi = jnp.where(row == SCAN_ROWS - 1, ni, pltpu.roll(ai, SCAN_ROWS - 1, 0))
                carry[4 * c:4 * c + 4] = [ar[0:1], ai[0:1], dr + nxr * hr + nxi * hi, di + nxi * hr - nxr * hi]
                parts += [ar, ai]
            a_ref[pl.ds(r0, SCAN_ROWS), :] = jnp.concatenate(parts, axis=1)
            return tuple(carry)

        st, dl = st_ref[0:1, :], dlam_ref[...]
        init = []
        for c in range(nch):
            lo = c * 2 * LANES
            init += [st[:, lo:lo + LANES], st[:, lo + LANES:lo + 2 * LANES],
                     dl[:, lo:lo + LANES], dl[:, lo + LANES:lo + 2 * LANES]]
        fin = lax.fori_loop(0, tt // SCAN_ROWS, tile, tuple(init))
        st_ref[0:1, :] = jnp.concatenate([fin[4 * c + q] for c in range(nch) for q in (0, 1)], axis=1)
        dlam_ref[...] = jnp.concatenate([fin[4 * c + q] for c in range(nch) for q in (2, 3)], axis=1)

        @pl.when(pl.program_id(1) == nt - 1)
        def _():
            dlam_ref[0:1, :] = jnp.sum(dlam_ref[...], axis=0, keepdims=True)

    rev = pl.BlockSpec((tt, SCAN_LANES), lambda lc, i: (nt - 1 - i, lc))
    vec = pl.BlockSpec((1, SCAN_LANES), lambda lc, i: (0, lc))
    return pl.pallas_call(
        body, name="ssm_scan_bwd", grid=(w // SCAN_LANES, nt),
        in_specs=[rev, rev, vec], out_specs=[rev, pl.BlockSpec((SCAN_ROWS, SCAN_LANES), lambda lc, i: (0, lc))],
        out_shape=[jax.ShapeDtypeStruct((s, w), F32), jax.ShapeDtypeStruct((SCAN_ROWS, w), F32)],
        scratch_shapes=[pltpu.VMEM((SCAN_ROWS, SCAN_LANES), F32)],
        compiler_params=_params(dimension_semantics=("arbitrary", "arbitrary")),
    )(e, h, lam)


def _ssm_params_fwd(a_re, a_im, log_dt, b_re, b_im):
    def body(ar_ref, ai_ref, ldt_ref, br_ref, bi_ref, lr_ref, li_ref, bbr_ref, bbi_ref):
        ar, ai, dt = ar_ref[...], ai_ref[...], jnp.exp(ldt_ref[...])
        mag = jnp.exp(ar * dt)
        lr, li = mag * jnp.cos(ai * dt), mag * jnp.sin(ai * dt)
        den = ar * ar + ai * ai
        cr = ((lr - 1.0) * ar + li * ai) / den
        ci = (li * ar - (lr - 1.0) * ai) / den
        br, bi = br_ref[...], bi_ref[...]
        lr_ref[...], li_ref[...] = lr, li
        bbr_ref[...] = cr * br - ci * bi
        bbi_ref[...] = cr * bi + ci * br

    n = a_re.shape[0]
    v1, v16 = jax.ShapeDtypeStruct((n, 1), F32), jax.ShapeDtypeStruct((n, SSM_GROUP), F32)
    return pl.pallas_call(body, name="ssm_params_fwd", out_shape=[v1, v1, v16, v16],
                          compiler_params=_params())(a_re, a_im, log_dt, b_re, b_im)


def _ssm_params_bwd(a_re, a_im, log_dt, b_re, b_im, g_lr, g_li, g_bbr, g_bbi):
    n = a_re.shape[0]

    def body(ar_ref, ai_ref, ldt_ref, br_ref, bi_ref, glr_ref, gli_ref, gbr_ref, gbi_ref,
             dar_ref, dai_ref, dldt_ref, dbr_ref, dbi_ref):
        ar, ai, dt = ar_ref[...], ai_ref[...], jnp.exp(ldt_ref[...])
        mag = jnp.exp(ar * dt)
        lr, li = mag * jnp.cos(ai * dt), mag * jnp.sin(ai * dt)
        den = ar * ar + ai * ai
        cr = ((lr - 1.0) * ar + li * ai) / den
        ci = (li * ar - (lr - 1.0) * ai) / den
        br, bi, gbr, gbi = br_ref[...], bi_ref[...], gbr_ref[...], gbi_ref[...]
        dbr_ref[...] = gbr * cr + gbi * ci
        dbi_ref[...] = gbi * cr - gbr * ci
        gcr = jnp.sum(gbr * br + gbi * bi, axis=1, keepdims=True)
        gci = jnp.sum(gbi * br - gbr * bi, axis=1, keepdims=True)
        ir, ii = ar / den, -ai / den
        glr = glr_ref[...] + gcr * ir + gci * ii
        gli = gli_ref[...] + gci * ir - gcr * ii
        qr, qi = cr * ir - ci * ii, cr * ii + ci * ir
        gar = -(gcr * qr + gci * qi)
        gai = -(gci * qr - gcr * qi)
        gxr = glr * lr + gli * li
        gxi = gli * lr - glr * li
        dar_ref[...] = gar + gxr * dt
        dai_ref[...] = gai + gxi * dt
        gdt = (gxr * ar + gxi * ai) * dt
        rowg = lax.broadcasted_iota(jnp.int32, (n, SSM_GROUPS), 0) // SSM_STATE
        colg = lax.broadcasted_iota(jnp.int32, (n, SSM_GROUPS), 1)
        dldt_ref[...] = jnp.sum(jnp.where(rowg == colg, gdt, 0.0), axis=0, keepdims=True)

    v1, v16 = jax.ShapeDtypeStruct((n, 1), F32), jax.ShapeDtypeStruct((n, SSM_GROUP), F32)
    return pl.pallas_call(body, name="ssm_params_bwd",
                          out_shape=[v1, v1, jax.ShapeDtypeStruct((1, SSM_GROUPS), F32), v16, v16],
                          compiler_params=_params())(a_re, a_im, log_dt, b_re, b_im, g_lr, g_li, g_bbr, g_bbi)


def _interleave(re, im, axis):
    shp = list(re.shape)
    new = shp[:axis] + [shp[axis] // LANES, LANES] + shp[axis + 1:]
    st = jnp.stack([re.reshape(new), im.reshape(new)], axis=axis + 1)
    return st.reshape(shp[:axis] + [2 * shp[axis]] + shp[axis + 1:])


def _deinterleave(v, axis):
    shp = list(v.shape)
    r = v.reshape(shp[:axis] + [shp[axis] // (2 * LANES), 2, LANES] + shp[axis + 1:])
    out = shp[:axis] + [shp[axis] // 2] + shp[axis + 1:]
    return (lax.index_in_dim(r, 0, axis + 1, keepdims=False).reshape(out),
            lax.index_in_dim(r, 1, axis + 1, keepdims=False).reshape(out))


def _b_matrix(bbr, bbi):
    eye = jnp.eye(SSM_GROUPS, dtype=F32)

    def blockdiag(v):
        x = v.reshape(SSM_GROUPS, SSM_STATE, SSM_GROUP).transpose(0, 2, 1)
        return (eye[:, None, :, None] * x[:, :, None, :]).reshape(SSM_GROUPS * SSM_GROUP, N_STATE)

    return _interleave(blockdiag(bbr), blockdiag(bbi), 1)


def _diag_blocks(v, rows, cols):
    per = SSM_GROUPS // SSM_BLOCKS
    return jnp.stack([v[g * rows:(g + 1) * rows, (g % per) * cols:(g % per + 1) * cols] for g in range(SSM_GROUPS)])


def _b_matrix_grad(d):
    def diag(v):
        return _diag_blocks(v, SSM_GROUP, SSM_STATE).transpose(0, 2, 1).reshape(N_STATE, SSM_GROUP)

    dr, di = _deinterleave(d, 1)
    return diag(dr), diag(di)


def _c_matrix(c_re, c_im):
    eye = jnp.eye(SSM_GROUPS, dtype=F32)

    def blockdiag(v):
        x = v.transpose(0, 2, 1)
        return (x[:, :, None, :] * eye[:, None, :, None]).reshape(N_STATE, SSM_GROUPS * SSM_GROUP)

    return _interleave(blockdiag(c_re), blockdiag(-c_im), 0)


def _c_matrix_grad(d):
    def diag(v):
        return _diag_blocks(v, SSM_STATE, SSM_GROUP).transpose(0, 2, 1)

    dr, di = _deinterleave(d, 0)
    return diag(dr), -diag(di)


def _row(v):
    return v.reshape(1, -1)


def _ssm_inputs(p):
    rows = lambda v: v.reshape(N_STATE, -1)
    ldt = jnp.repeat(p["ssm_log_dt"], SSM_STATE).reshape(N_STATE, 1)
    return rows(p["ssm_a_re"]), rows(p["ssm_a_im"]), ldt, rows(p["ssm_b_re"]), rows(p["ssm_b_im"])


def _layer_fwd(x, mod, p, tag):
    d = x.shape[1]
    sh_m, sc_m, g_m, sh_f, sc_f, g_f = [_row(mod[i]) for i in range(6)]
    nm = lambda s: f"{s}_{tag}"

    def lnmod(x, sc, sh):
        return _norm(x)[0] * (1.0 + sc) + sh

    h1 = _rowwise(nm("lnmod1"), lnmod, [(x, 0, d)], [sc_m, sh_m], [(d, BF16)])
    proj = _mm(nm("proj"), h1, p["w_in"], "nn")
    t = min(ATT_TILE, x.shape[0])
    qs, kb, vb, kt3, vt3 = _qkv_prep(proj, t)
    att, car = _attn_fwd(qs, kb, vt3, t)
    y_sb = _mm(nm("sb_up"), att, p["w_sb_up"], "nn")

    lam_r, lam_i, bbr, bbi = _ssm_params_fwd(*_ssm_inputs(p))
    lam = _interleave(lam_r.reshape(1, N_STATE), lam_i.reshape(1, N_STATE), 1)
    bmat = _b_matrix(bbr, bbi).astype(BF16)
    cmat = _c_matrix(p["ssm_c_re"], p["ssm_c_im"]).astype(BF16)
    bu = _mm(nm("ssm_b"), proj, bmat, "nn", a_cols=(1536, 512), diag=SSM_BLOCKS)
    hst = _ssm_scan_fwd(bu, lam)
    yc = _mm(nm("ssm_c"), hst, cmat, "nn", diag=SSM_BLOCKS)

    def ssm_act(yc, u, dsk):
        y0 = yc + dsk * u
        return y0, _gelu(y0)

    y0, y1 = _rowwise(nm("ssm_act"), ssm_act, [(yc, 0, 512), (proj, 3, 512)], [_row(p["ssm_d"])], [(512, F32), (512, F32)])
    gl = _mm(nm("glu"), y1, p["w_glu"], "nn")
    y2 = _rowwise(nm("glu_act"), lambda y1, gl, b: y1 * jax.nn.sigmoid(gl + b), [(y1, 0, 512), (gl, 0, 512)],
                  [_row(p["b_glu"])], [(512, BF16)])
    y_ssm = _mm(nm("ssm_up"), y2, p["w_ssm_up"], "nn")

    def merge(gsb, gss, ysb, yss):
        return jax.nn.sigmoid(gsb) * ysb + jax.nn.sigmoid(gss) * yss

    merged = _rowwise(nm("merge"), merge, [(proj, 2, d), (proj, 3, d), (y_sb, 0, d), (y_ssm, 0, d)], [], [(d, BF16)])
    y = _mm(nm("out"), merged, p["w_out"], "nn")

    def resid_ln(x, y, g, lg, lb):
        return _norm(ALPHA * x + (1.0 + g) * y)[0] * lg + lb

    x1 = _rowwise(nm("ln1"), resid_ln, [(x, 0, d), (y, 0, d)], [g_m, _row(p["ln1_g"]), _row(p["ln1_b"])], [(d, F32)])
    h2 = _rowwise(nm("lnmod2"), lnmod, [(x1, 0, d)], [sc_f, sh_f], [(d, BF16)])
    f = _mm(nm("ffn_in"), h2, p["w_ffn_in"], "nn", out_dtype=BF16)
    fh = f.shape[1] // 2

    def swiglu(g, u):
        g = g.astype(F32)
        return g * jax.nn.sigmoid(g) * u.astype(F32)

    act = _rowwise(nm("swiglu"), swiglu, [(f, 0, fh), (f, 1, fh)], [], [(fh, BF16)])
    yf = _mm(nm("ffn_out"), act, p["w_ffn_out"], "nn")
    x2 = _rowwise(nm("ln2"), resid_ln, [(x1, 0, d), (yf, 0, d)], [g_f, _row(p["ln2_g"]), _row(p["ln2_b"])], [(d, F32)])
    saved = dict(x=x, h1=h1, proj=proj, qs=qs, kb=kb, vb=vb, kt3=kt3, car=car, att=att, y_sb=y_sb, lam=lam, bmat=bmat,
                 cmat=cmat, hst=hst, y0=y0, y1=y1, gl=gl, y2=y2, y_ssm=y_ssm, merged=merged, y=y, x1=x1, h2=h2, f=f,
                 act=act, yf=yf, t=t)
    return x2, saved


def _layer_bwd(dx2, mod, p, sv, layer, depth, stacked):
    d = dx2.shape[1]
    sh_m, sc_m, g_m, sh_f, sc_f, g_f = [_row(mod[i]) for i in range(6)]
    nm = lambda s: f"{s}_{layer}"
    grads = {}

    def weight_grad(n, a, b, **kw):
        grads[n] = _mm(nm("d" + n), a, b, "tn", out_dtype=BF16, into=(stacked.get(n), layer, depth), **kw)

    def resid_ln_bwd(x, y, dxo, g, lg):
        n, rstd = _norm(ALPHA * x + (1.0 + g) * y)
        dr = _norm_bwd(dxo * lg, n, rstd)
        return ALPHA * dr, (1.0 + g) * dr, _colsum(dxo * n), _colsum(dxo), _colsum(dr * y)

    def lnmod_bwd(x, dh, dxa, sc):
        n, rstd = _norm(x)
        return dxa + _norm_bwd(dh * (1.0 + sc), n, rstd), _colsum(dh * n), _colsum(dh)

    dx1a, dyf, grads["ln2_g"], grads["ln2_b"], dg_f = _rowwise(
        nm("ln2_bwd"), resid_ln_bwd, [(sv["x1"], 0, d), (sv["yf"], 0, d), (dx2, 0, d)], [g_f, _row(p["ln2_g"])],
        [(d, F32), (d, BF16)], [d, d, d])
    dact = _mm(nm("d_act"), dyf, p["w_ffn_out"], "nt")
    weight_grad("w_ffn_out", sv["act"], dyf)
    fh = sv["f"].shape[1] // 2

    def swiglu_bwd(g, u, da):
        g, u = g.astype(F32), u.astype(F32)
        sg = jax.nn.sigmoid(g)
        return jnp.concatenate([da * u * sg * (1.0 + g * (1.0 - sg)), da * g * sg], axis=1)

    df = _rowwise(nm("swiglu_bwd"), swiglu_bwd, [(sv["f"], 0, fh), (sv["f"], 1, fh), (dact, 0, fh)], [], [(2 * fh, BF16)])
    dh2 = _mm(nm("d_h2"), df, p["w_ffn_in"], "nt")
    weight_grad("w_ffn_in", sv["h2"], df)
    dx1, dsc_f, dsh_f = _rowwise(nm("lnmod2_bwd"), lnmod_bwd, [(sv["x1"], 0, d), (dh2, 0, d), (dx1a, 0, d)], [sc_f],
                                 [(d, F32)], [d, d])
    dxa, dy, grads["ln1_g"], grads["ln1_b"], dg_m = _rowwise(
        nm("ln1_bwd"), resid_ln_bwd, [(sv["x"], 0, d), (sv["y"], 0, d), (dx1, 0, d)], [g_m, _row(p["ln1_g"])],
        [(d, F32), (d, BF16)], [d, d, d])
    dmerged = _mm(nm("d_merged"), dy, p["w_out"], "nt")
    weight_grad("w_out", sv["merged"], dy)

    def merge_bwd(gsb, gss, ysb, yss, dm):
        s1, s2 = jax.nn.sigmoid(gsb), jax.nn.sigmoid(gss)
        return s1 * dm, s2 * dm, dm * ysb * s1 * (1.0 - s1), dm * yss * s2 * (1.0 - s2)

    dy_sb, dy_ssm, dg_sb, dg_ssm = _rowwise(
        nm("merge_bwd"), merge_bwd, [(sv["proj"], 2, d), (sv["proj"], 3, d), (sv["y_sb"], 0, d), (sv["y_ssm"], 0, d),
                                     (dmerged, 0, d)], [], [(d, BF16)] * 4)
    dy2 = _mm(nm("d_y2"), dy_ssm, p["w_ssm_up"], "nt")
    weight_grad("w_ssm_up", sv["y2"], dy_ssm)

    def glu_act_bwd(y1, gl, dy2, b):
        sg = jax.nn.sigmoid(gl + b)
        dgl = dy2 * y1 * sg * (1.0 - sg)
        return dy2 * sg, dgl, _colsum(dgl)

    dy1a, dgl, grads["b_glu"] = _rowwise(nm("glu_act_bwd"), glu_act_bwd, [(sv["y1"], 0, 512), (sv["gl"], 0, 512), (dy2, 0, 512)],
                                         [_row(p["b_glu"])], [(512, F32), (512, BF16)], [512])
    dy1b = _mm(nm("d_y1"), dgl, p["w_glu"], "nt")
    weight_grad("w_glu", sv["y1"], dgl)

    def ssm_act_bwd(y0, u, dy1a, dy1b, dsk):
        dy0 = (dy1a + dy1b) * _gelu_grad(y0)
        return dy0, dsk * dy0, _colsum(dy0 * u)

    dy0, du_a, grads["ssm_d"] = _rowwise(nm("ssm_act_bwd"), ssm_act_bwd,
                                         [(sv["y0"], 0, 512), (sv["proj"], 3, 512), (dy1a, 0, 512), (dy1b, 0, 512)],
                                         [_row(p["ssm_d"])], [(512, BF16), (512, F32)], [512])
    e = _mm(nm("ssm_e"), dy0, sv["cmat"], "nt", diag=SSM_BLOCKS)
    grads["ssm_c_re"], grads["ssm_c_im"] = _c_matrix_grad(_mm(nm("dw_ssm_c"), sv["hst"], dy0, "tn", diag=SSM_BLOCKS))
    adj, dlam = _ssm_scan_bwd(e, sv["hst"], sv["lam"])
    du_b = _mm(nm("d_u"), adj, sv["bmat"], "nt", diag=SSM_BLOCKS)
    g_bbr, g_bbi = _b_matrix_grad(_mm(nm("dw_ssm_b"), sv["proj"], adj, "tn", a_cols=(1536, 512), diag=SSM_BLOCKS))
    g_lr, g_li = _deinterleave(dlam[0:1], 1)
    da_re, da_im, dldt, db_re, db_im = _ssm_params_bwd(*_ssm_inputs(p), g_lr.reshape(N_STATE, 1), g_li.reshape(N_STATE, 1),
                                                       g_bbr, g_bbi)
    grads["ssm_a_re"] = da_re.reshape(SSM_GROUPS, SSM_STATE)
    grads["ssm_a_im"] = da_im.reshape(SSM_GROUPS, SSM_STATE)
    grads["ssm_log_dt"] = dldt.reshape(SSM_GROUPS)
    grads["ssm_b_re"] = db_re.reshape(SSM_GROUPS, SSM_STATE, SSM_GROUP)
    grads["ssm_b_im"] = db_im.reshape(SSM_GROUPS, SSM_STATE, SSM_GROUP)
    datt = _mm(nm("d_att"), dy_sb, p["w_sb_up"], "nt", out_dtype=BF16)
    weight_grad("w_sb_up", sv["att"], dy_sb)
    dqs, dk, dv = _attn_bwd(sv["qs"], datt, sv["kb"], sv["vb"], sv["kt3"], sv["car"], sv["t"])

    def dproj_cols(dqs, dk, dv, dua, dub, dgsb, dgss):
        return jnp.concatenate([dqs * (1.0 / math.sqrt(HEAD_DIM)), dk, dv, dua + dub, dgsb.astype(F32), dgss.astype(F32)],
                               axis=1)

    dproj = _rowwise(nm("dproj"), dproj_cols, [(dqs, 0, 512), (dk, 0, 512), (dv, 0, 512), (du_a, 0, 512), (du_b, 0, 512),
                                               (dg_sb, 0, d), (dg_ssm, 0, d)], [], [(2048 + 2 * d, BF16)])
    dh1 = _mm(nm("d_h1"), dproj, p["w_in"], "nt")
    weight_grad("w_in", sv["h1"], dproj)
    dx, dsc_m, dsh_m = _rowwise(nm("lnmod1_bwd"), lnmod_bwd, [(sv["x"], 0, d), (dh1, 0, d), (dxa, 0, d)], [sc_m],
                                [(d, F32)], [d, d])
    for k in ("ln1_g", "ln1_b", "ln2_g", "ln2_b", "ssm_d", "b_glu"):
        grads[k] = grads[k].reshape(-1)
    dmod = jnp.concatenate([dsh_m, dsc_m, dg_m, dsh_f, dsc_f, dg_f], axis=0)
    return dx, dmod, grads


def _loss_head(x, target):
    d = x.shape[1]

    def fn(x, tgt):
        err = x - tgt
        return err * (1.0 / d), _colsum(err * err) * (0.5 / d)

    return _rowwise("loss_head", fn, [(x, 0, d), (target, 0, d)], [], [(d, F32)], [d])


def _place():
    return lax.axis_index("x"), lax.axis_index("y"), lax.axis_index("c")


def _all_gather8(name, block):
    m_per, n = block.shape

    def body(x_ref, out_ref, send_sems, recv_sems, local_sem):
        x, y, c = _place()
        me, sibling = (x, y, c), (x, y, 1 - c)
        chips = [(1 - x, y), (x, 1 - y), (1 - x, 1 - y)]

        def rows(px, py, pc):
            return out_ref.at[pl.ds(pl.multiple_of((4 * px + 2 * py + pc) * m_per, 8), m_per), :]

        def copy(k, blk, to, src=None):
            return pltpu.make_async_remote_copy(src_ref=rows(*blk) if src is None else src, dst_ref=rows(*blk),
                                                send_sem=send_sems.at[k], recv_sem=recv_sems.at[k],
                                                device_id=to, device_id_type=MESH)

        mine = pltpu.make_async_copy(x_ref, rows(*me), local_sem)
        mine.start()
        first = [copy(0, me, sibling, src=x_ref)] + [copy(1 + j, me, (*chip, c), src=x_ref) for j, chip in enumerate(chips)]
        for cp in first:
            cp.start()
        passed = [copy(4 + j, (*chip, c), sibling) for j, chip in enumerate(chips)]
        for j, chip in enumerate(chips):
            copy(1 + j, (*chip, c), me).wait_recv()
            passed[j].start()
        copy(0, sibling, me).wait_recv()
        for j, chip in enumerate(chips):
            copy(4 + j, (*chip, 1 - c), me).wait_recv()
        for cp in first + passed:
            cp.wait_send()
        mine.wait()

    return pl.pallas_call(
        body, name=name, out_shape=jax.ShapeDtypeStruct((8 * m_per, n), block.dtype),
        in_specs=[pl.BlockSpec(memory_space=pltpu.VMEM)], out_specs=pl.BlockSpec(memory_space=pltpu.VMEM),
        scratch_shapes=[pltpu.SemaphoreType.DMA((7,)), pltpu.SemaphoreType.DMA((7,)), pltpu.SemaphoreType.DMA],
        compiler_params=_params(),
    )(block)


def _other_chips(x, y):
    return [(1 - x, y), (x, 1 - y), (1 - x, 1 - y)]


def _gather_weights(shards, by_rows):
    n = len(shards)

    def body(*refs):
        src, dst = refs[:n], refs[n:2 * n]
        send_sems, recv_sems, local_sems = refs[2 * n:]
        x, y, c = _place()

        def block(k, px, py):
            _, r, cols = shards[k].shape
            q = 2 * px + py
            if by_rows[k]:
                return dst[k].at[:, pl.ds(pl.multiple_of(q * r, 16), r), :]
            return dst[k].at[:, :, pl.ds(pl.multiple_of(q * cols, LANES), cols)]

        local = [pltpu.make_async_copy(src[k], block(k, x, y), local_sems.at[k]) for k in range(n)]
        for cp in local:
            cp.start()
        sends = []
        for k in range(n):
            for j, (px, py) in enumerate(_other_chips(x, y)):
                cp = pltpu.make_async_remote_copy(src_ref=src[k], dst_ref=block(k, x, y), send_sem=send_sems.at[k, j],
                                                  recv_sem=recv_sems.at[k, j], device_id=(px, py, c), device_id_type=MESH)
                cp.start()
                sends.append(cp)
        for k in range(n):
            for j, (px, py) in enumerate(_other_chips(x, y)):
                pltpu.make_async_remote_copy(src_ref=src[k], dst_ref=block(k, px, py), send_sem=send_sems.at[k, j],
                                             recv_sem=recv_sems.at[k, j], device_id=(px, py, c),
                                             device_id_type=MESH).wait_recv()
        for cp in sends:
            cp.wait_send()
        for cp in local:
            cp.wait()

    def whole(s, rows):
        l, r, cols = s.shape
        return jax.ShapeDtypeStruct((l, 4 * r, cols) if rows else (l, r, 4 * cols), s.dtype)

    any_spec = pl.BlockSpec(memory_space=pl.ANY)
    return pl.pallas_call(
        body, name="gather_weights", in_specs=[any_spec] * n, out_specs=[any_spec] * n,
        out_shape=[whole(s, rows) for s, rows in zip(shards, by_rows)],
        scratch_shapes=[pltpu.SemaphoreType.DMA((n, 3)), pltpu.SemaphoreType.DMA((n, 3)), pltpu.SemaphoreType.DMA((n,))],
        compiler_params=_params(),
    )(*shards)


def _peer(x, y, c, r):
    fx, fy, fc = (r >> 2) & 1, (r >> 1) & 1, r & 1
    return (x + fx - 2 * x * fx, y + fy - 2 * y * fy, c + fc - 2 * c * fc)


def _scatter_grads(grads, by_rows):
    n = len(grads)

    def half_shape(k):
        l, r, c = grads[k].shape
        return (l, r // 8, c) if by_rows[k] else (l, r // 2, c // 4)

    def body(*refs):
        src, dst = refs[:n], refs[n:2 * n]
        send_sems, recv_sems, local_sems = refs[2 * n:]
        x, y, c = _place()
        me = 4 * x + 2 * y + c

        def window(k, px, py, pc):
            _, hr, hc = half_shape(k)
            q = 2 * px + py
            if by_rows[k]:
                return src[k].at[:, pl.ds(pl.multiple_of((2 * q + pc) * hr, 16), hr), :]
            return src[k].at[:, pl.ds(pl.multiple_of(pc * hr, 16), hr), pl.ds(pl.multiple_of(q * hc, LANES), hc)]

        local = [pltpu.make_async_copy(window(k, x, y, c), dst[k].at[me], local_sems.at[k]) for k in range(n)]
        for cp in local:
            cp.start()
        sends = []
        for k in range(n):
            for r in range(1, 8):
                to = _peer(x, y, c, r)
                cp = pltpu.make_async_remote_copy(src_ref=window(k, *to), dst_ref=dst[k].at[me], send_sem=send_sems.at[k, r - 1],
                                                  recv_sem=recv_sems.at[k, r - 1], device_id=to, device_id_type=MESH)
                cp.start()
                sends.append(cp)
        for k in range(n):
            for r in range(1, 8):
                px, py, pc = _peer(x, y, c, r)
                pltpu.make_async_remote_copy(src_ref=window(k, x, y, c), dst_ref=dst[k].at[4 * px + 2 * py + pc],
                                             send_sem=send_sems.at[k, r - 1], recv_sem=recv_sems.at[k, r - 1],
                                             device_id=(px, py, pc), device_id_type=MESH).wait_recv()
        for cp in sends:
            cp.wait_send()
        for cp in local:
            cp.wait()

    any_spec = pl.BlockSpec(memory_space=pl.ANY)
    return pl.pallas_call(
        body, name="scatter_grads", in_specs=[any_spec] * n, out_specs=[any_spec] * n,
        out_shape=[jax.ShapeDtypeStruct((8, *half_shape(k)), grads[k].dtype) for k in range(n)],
        scratch_shapes=[pltpu.SemaphoreType.DMA((n, 7)), pltpu.SemaphoreType.DMA((n, 7)), pltpu.SemaphoreType.DMA((n,))],
        compiler_params=_params(),
    )(*grads)


def _sum_slots(name, parts, half=None):
    _, l, r, c = parts.shape
    tr = _pick(r, (256, 176, 128, 64, 32, 8))

    def body(*refs):
        p_ref, o_ref = refs[-2:]
        acc = p_ref[0].astype(F32)
        for i in range(1, 8):
            acc = acc + p_ref[i].astype(F32)
        o_ref[...] = acc

    if half is None:
        return pl.pallas_call(
            body, name=name, grid=(l, r // tr), in_specs=[pl.BlockSpec((8, 1, tr, c), lambda li, i: (0, li, i, 0))],
            out_specs=pl.BlockSpec((1, tr, c), lambda li, i: (li, i, 0)), out_shape=jax.ShapeDtypeStruct((l, r, c), F32),
            compiler_params=_params(dimension_semantics=("arbitrary", "arbitrary")),
        )(parts)
    grid_spec = pltpu.PrefetchScalarGridSpec(
        num_scalar_prefetch=1, grid=(l, r // tr),
        in_specs=[pl.BlockSpec((8, 1, tr, c), lambda li, i, h: (0, li, i, 0))],
        out_specs=pl.BlockSpec((1, tr, c), lambda li, i, h: (li, h[0] * (r // tr) + i, 0)))
    return pl.pallas_call(
        body, name=name, grid_spec=grid_spec, out_shape=jax.ShapeDtypeStruct((l, 2 * r, c), F32),
        compiler_params=_params(dimension_semantics=("arbitrary", "arbitrary")),
    )(jnp.reshape(half, (1,)).astype(jnp.int32), parts)


def _swap_halves(blocks):
    n = len(blocks)

    def body(*refs):
        src, dst = refs[:n], refs[n:2 * n]
        send_sems, recv_sems = refs[2 * n:]
        x, y, c = _place()

        def half(ref, k, pc):
            r = blocks[k].shape[1] // 2
            return ref[k].at[:, pl.ds(pl.multiple_of(pc * r, 8), r), :]

        def copy(k, pc):
            return pltpu.make_async_remote_copy(src_ref=half(src, k, pc), dst_ref=half(dst, k, pc), send_sem=send_sems.at[k],
                                                recv_sem=recv_sems.at[k], device_id=(x, y, 1 - c), device_id_type=MESH)

        for k in range(n):
            copy(k, c).start()
        for k in range(n):
            copy(k, 1 - c).wait_recv()
        for k in range(n):
            copy(k, c).wait_send()

    any_spec = pl.BlockSpec(memory_space=pl.ANY)
    return pl.pallas_call(
        body, name="swap_halves", in_specs=[any_spec] * n, out_specs=[any_spec] * n,
        out_shape=[jax.ShapeDtypeStruct(b.shape, b.dtype) for b in blocks], input_output_aliases={k: k for k in range(n)},
        scratch_shapes=[pltpu.SemaphoreType.DMA((n,)), pltpu.SemaphoreType.DMA((n,))],
        compiler_params=_params(),
    )(*blocks)


def _adamw(name, w, g, m, v):
    shape = w.shape
    cols = shape[-1] if w.ndim > 1 and shape[-1] % LANES == 0 else w.size if w.size % LANES else LANES
    flat = lambda a: a.reshape(-1, cols)
    rows = w.size // cols
    tr = _pick(rows, [r for r in (512, 256, 128, 64, 32, 16, 8) if r * cols <= 256 * 1024]) if rows % 8 == 0 else rows

    def body(w_ref, g_ref, m_ref, v_ref, d_ref, nm_ref, nv_ref):
        gg = g_ref[...]
        nm = ADAM_B1 * m_ref[...] + (1.0 - ADAM_B1) * gg
        nv = ADAM_B2 * v_ref[...] + (1.0 - ADAM_B2) * (gg * gg)
        m_hat = nm / (1.0 - ADAM_B1 ** ADAM_STEP)
        v_hat = nv / (1.0 - ADAM_B2 ** ADAM_STEP)
        d_ref[...] = -ADAM_LR * (m_hat / (jnp.sqrt(v_hat) + ADAM_EPS) + ADAM_WD * w_ref[...])
        nm_ref[...] = nm
        nv_ref[...] = nv

    spec = pl.BlockSpec((tr, cols), lambda i: (i, 0))
    out = pl.pallas_call(
        body, name=name, grid=(rows // tr,), in_specs=[spec] * 4, out_specs=[spec] * 3,
        out_shape=[jax.ShapeDtypeStruct((rows, cols), F32)] * 3,
        compiler_params=_params(dimension_semantics=("arbitrary",)),
    )(flat(w), flat(g), flat(m), flat(v))
    return tuple(o.reshape(shape) for o in out)


WEIGHTS = ["w_ada", "b_ada", "w_in", "w_sb_up", "ssm_a_re", "ssm_a_im", "ssm_log_dt", "ssm_b_re", "ssm_b_im", "ssm_c_re",
           "ssm_c_im", "ssm_d", "w_glu", "b_glu", "w_ssm_up", "w_out", "ln1_g", "ln1_b", "w_ffn_in", "w_ffn_out", "ln2_g",
           "ln2_b"]
COL_SPLIT = ["w_in", "w_sb_up", "w_ssm_up", "w_ffn_in"]
ROW_SPLIT = ["w_glu", "w_out", "w_ffn_out"]
SMALL = ["ssm_a_re", "ssm_a_im", "ssm_log_dt", "ssm_b_re", "ssm_b_im", "ssm_c_re", "ssm_c_im", "ssm_d", "b_glu", "ln1_g",
         "ln1_b", "ln2_g", "ln2_b"]
SLAB_COLS = 1024


def _cast_bf16(name, w):
    shape = w.shape
    flat = w.reshape(-1, shape[-1])
    rows, cols = flat.shape
    tr = _pick(rows, (512, 256, 128, 64, 8))

    def body(w_ref, o_ref):
        o_ref[...] = w_ref[...].astype(BF16)

    spec = pl.BlockSpec((tr, cols), lambda i: (i, 0))
    return pl.pallas_call(body, name=name, grid=(rows // tr,), in_specs=[spec], out_specs=spec,
                          out_shape=jax.ShapeDtypeStruct((rows, cols), BF16),
                          compiler_params=_params(dimension_semantics=("arbitrary",)))(flat).reshape(shape)


def _silu_rows(name, c):
    def body(c_ref, o_ref):
        v = c_ref[...]
        o_ref[...] = v * jax.nn.sigmoid(v)

    return pl.pallas_call(body, name=name, out_shape=jax.ShapeDtypeStruct(c.shape, F32), compiler_params=_params())(c)


def _pad_rows(v, mult=8):
    flat = v.reshape(-1)
    per = mult * SLAB_COLS
    total = -(-flat.size // per) * per
    return jnp.pad(flat, (0, total - flat.size)).reshape(-1, SLAB_COLS)


def kernel(x, c, w_ada, b_ada, w_in, w_sb_up, ssm_a_re, ssm_a_im, ssm_log_dt, ssm_b_re, ssm_b_im, ssm_c_re, ssm_c_im, ssm_d, w_glu, b_glu, w_ssm_up, w_out, ln1_g, ln1_b, w_ffn_in, w_ffn_out, ln2_g, ln2_b, loss_target, m_w_ada, m_b_ada, m_w_in, m_w_sb_up, m_ssm_a_re, m_ssm_a_im, m_ssm_log_dt, m_ssm_b_re, m_ssm_b_im, m_ssm_c_re, m_ssm_c_im, m_ssm_d, m_w_glu, m_b_glu, m_w_ssm_up, m_w_out, m_ln1_g, m_ln1_b, m_w_ffn_in, m_w_ffn_out, m_ln2_g, m_ln2_b, v_w_ada, v_b_ada, v_w_in, v_w_sb_up, v_ssm_a_re, v_ssm_a_im, v_ssm_log_dt, v_ssm_b_re, v_ssm_b_im, v_ssm_c_re, v_ssm_c_im, v_ssm_d, v_w_glu, v_b_glu, v_w_ssm_up, v_w_out, v_ln1_g, v_ln1_b, v_w_ffn_in, v_w_ffn_out, v_ln2_g, v_ln2_b):
    args = dict(locals())
    w = {n: args[n] for n in WEIGHTS}
    mom = {n: args["m_" + n] for n in WEIGHTS}
    var = {n: args["v_" + n] for n in WEIGHTS}
    depth, d = w_ada.shape[0], x.shape[-1]
    xi, yi, ci = _place()
    me, chip = 4 * xi + 2 * yi + ci, 2 * xi + yi
    ada_cols = w_ada.shape[2]

    big = COL_SPLIT + ROW_SPLIT
    by_rows = [n in ROW_SPLIT for n in big]
    full = dict(zip(big, _gather_weights([_cast_bf16(f"cast_{n}", w[n]) for n in big], by_rows)))

    c_all = _all_gather8("gather_c", jnp.pad(c, ((0, 7), (0, 0))))[::8]
    c_act = _silu_rows("silu_c", c_all)
    b_cols = lax.dynamic_slice_in_dim(b_ada, chip * ada_cols, ada_cols, axis=1)
    mod_part = jnp.concatenate([_small_mm(f"mod_{l}", c_act, w_ada[l], "nn") + b_cols[l][None] for l in range(depth)], axis=0)
    mod_all = _all_gather8("gather_mod", mod_part).reshape(4, 2, depth, 8, ada_cols)[:, 0]
    mod_mine = lax.dynamic_index_in_dim(mod_all, me, axis=2, keepdims=False)
    mod = mod_mine.transpose(1, 0, 2).reshape(depth, 6, d)

    layer_w = [{**{n: full[n][l] for n in big}, **{n: w[n][l] for n in SMALL}} for l in range(depth)]
    h, saved = x[0], []
    for l in range(depth):
        h, sv = _layer_fwd(h, mod[l], layer_w[l], str(l))
        saved.append(sv)
    dh, loss_cols = _loss_head(h, loss_target[0])
    loss = lax.psum(jnp.sum(loss_cols), ("x", "y", "c"))
    dmods, lgrads, stacked = [None] * depth, [None] * depth, {}
    for l in reversed(range(depth)):
        dh, dmods[l], lgrads[l] = _layer_bwd(dh, mod[l], layer_w[l], saved[l], l, depth, stacked)
        stacked = {n: lgrads[l][n] for n in big}
    grad_x = dh[None]

    parts = _scatter_grads([stacked[n] for n in big], by_rows)
    halves = [_sum_slots(f"sum_{n}", p, half=ci) for n, p in zip(big, parts)]
    grad = dict(zip(big, _swap_halves(halves)))

    pieces = [jnp.stack(dmods)] + [jnp.stack([lgrads[l][n] for l in range(depth)]) for n in SMALL]
    slab = jnp.concatenate([_pad_rows(p) for p in pieces], axis=0)
    slabs = _all_gather8("gather_small", slab).reshape(8, 1, *slab.shape)
    total = _sum_slots("sum_small", slabs)[0]
    row = _pad_rows(pieces[0]).shape[0]
    for n, p in zip(SMALL, pieces[1:]):
        rows = _pad_rows(p).shape[0]
        grad[n] = total[row:row + rows].reshape(-1)[:p.size].reshape(p.shape)
        row += rows
    dmod_rows = _pad_rows(pieces[0]).shape[0]
    dmod_all = slabs[:, 0, :dmod_rows].reshape(8, -1)[:, :depth * 6 * d].reshape(8, depth, 4, ada_cols)
    dmod_cols = lax.dynamic_index_in_dim(dmod_all, chip, axis=2, keepdims=False)
    grad["w_ada"] = jnp.stack([_small_mm(f"dw_ada_{l}", c_act, dmod_cols[:, l], "tn") for l in range(depth)])
    dmod_sum = _sum_slots("sum_dmod", slabs[:, :, :dmod_rows])[0]
    grad["b_ada"] = dmod_sum.reshape(-1)[:depth * 6 * d].reshape(depth, 6 * d)

    delta, new_m, new_v = {}, {}, {}
    for n in WEIGHTS:
        delta[n], new_m[n], new_v[n] = _adamw(f"adamw_{n}", w[n], grad[n], mom[n], var[n])
    return (loss, grad_x, *[grad[n] for n in WEIGHTS], *[delta[n] for n in WEIGHTS], *[new_m[n] for n in WEIGHTS],
            *[new_v[n] for n in WEIGHTS])
```

```python
import functools
import math

import jax
import jax.numpy as jnp
from jax import lax
from jax.experimental import pallas as pl
from jax.experimental.pallas import tpu as pltpu

F32 = jnp.float32
BF16 = jnp.bfloat16
MESH = pl.DeviceIdType.MESH

LANES = 128
HEAD_DIM = 64
ATT_TILE = 256
SSM_GROUPS, SSM_STATE, SSM_GROUP = 32, 64, 16
N_STATE = SSM_GROUPS * SSM_STATE
SSM_BLOCKS = SSM_GROUPS * SSM_GROUP // LANES
U_OFFSET = 3 * 512
LN_EPS = 1e-5
DEPTH = 2
ALPHA = (2 * DEPTH) ** 0.25
ADAM_LR, ADAM_B1, ADAM_B2, ADAM_EPS, ADAM_WD, ADAM_STEP = 0.001, 0.9, 0.999, 1e-08, 0.01, 10
VMEM_LIMIT = 56 * 1024 * 1024
GELU_K = math.sqrt(2.0 / math.pi)
GELU_C = 0.044715


def _params(**kw):
    return pltpu.CompilerParams(vmem_limit_bytes=VMEM_LIMIT, **kw)


def _pick(n, prefs):
    for p in prefs:
        if n % p == 0:
            return p
    return n


def _rowwise(name, fn, rows, vecs, outs, sums=(), tm=None):
    s = rows[0][0].shape[0]
    tm = tm or _pick(s, (256, 128, 64, 8))
    nin, no, ns = len(rows) + len(vecs), len(outs), len(sums)

    def body(*refs):
        res = fn(*[r[...] for r in refs[:nin]])
        res = res if isinstance(res, tuple) else (res,)
        for r, v in zip(refs[nin:nin + no], res[:no]):
            r[...] = v.astype(r.dtype)
        if ns:
            @pl.when(pl.program_id(0) == 0)
            def _():
                for r in refs[nin + no:]:
                    r[...] = jnp.zeros_like(r)
            for r, v in zip(refs[nin + no:], res[no:]):
                r[...] += v

    in_specs = [pl.BlockSpec((tm, w), lambda i, cb=cb: (i, cb)) for _, cb, w in rows]
    in_specs += [pl.BlockSpec(v.shape, lambda i: (0, 0)) for v in vecs]
    out_specs = [pl.BlockSpec((tm, w), lambda i: (i, 0)) for w, _ in outs]
    out_specs += [pl.BlockSpec((1, w), lambda i: (0, 0)) for w in sums]
    out_shape = [jax.ShapeDtypeStruct((s, w), dt) for w, dt in outs]
    out_shape += [jax.ShapeDtypeStruct((1, w), F32) for w in sums]
    res = pl.pallas_call(
        body, name=name, grid=(s // tm,), in_specs=in_specs, out_specs=out_specs, out_shape=out_shape,
        compiler_params=_params(dimension_semantics=("arbitrary",)),
    )(*[a for a, _, _ in rows], *vecs)
    return res[0] if len(res) == 1 else tuple(res)


MM_TILES = (1408, 1024, 512, 256, 128)


def _mm(name, a, b, mode, out_dtype=F32, into=None):
    if mode == "nn":
        m, k, n = a.shape[0], a.shape[1], b.shape[1]
    elif mode == "nt":
        m, k, n = a.shape[0], a.shape[1], b.shape[0]
    else:
        k, m, n = a.shape[0], a.shape[1], b.shape[1]
    tm = _pick(m, MM_TILES if mode == "tn" else MM_TILES[1:])
    tn = _pick(n, MM_TILES)
    tk = _pick(k, MM_TILES)
    nk = k // tk
    dims = {"nn": ((1,), (0,)), "nt": ((1,), (1,)), "tn": ((0,), (0,))}[mode]

    def body(a_ref, b_ref, *rest):
        o_ref = rest[-2] if nk > 1 else rest[-1]
        prod = lax.dot_general(a_ref[...].astype(BF16), b_ref[...].astype(BF16), (dims, ((), ())),
                               preferred_element_type=F32)
        if nk == 1:
            o_ref[...] = prod.astype(o_ref.dtype)
            return
        acc_ref = rest[-1]
        kk = pl.program_id(2)

        @pl.when(kk == 0)
        def _():
            acc_ref[...] = prod

        @pl.when(kk > 0)
        def _():
            acc_ref[...] += prod

        @pl.when(kk == nk - 1)
        def _():
            o_ref[...] = acc_ref[...].astype(o_ref.dtype)

    if mode == "tn":
        a_spec = pl.BlockSpec((tk, tm), lambda i, j, kk: (kk, i))
    else:
        a_spec = pl.BlockSpec((tm, tk), lambda i, j, kk: (i, kk))
    if mode == "nt":
        b_spec = pl.BlockSpec((tn, tk), lambda i, j, kk: (j, kk))
    else:
        b_spec = pl.BlockSpec((tk, tn), lambda i, j, kk: (kk, j))
    in_specs, operands, aliases = [a_spec, b_spec], [a, b], {}
    if into is None:
        out_spec = pl.BlockSpec((tm, tn), lambda i, j, kk: (i, j))
        out_shape = jax.ShapeDtypeStruct((m, n), out_dtype)
    else:
        buf, slab, count = into
        out_spec = pl.BlockSpec((None, tm, tn), lambda i, j, kk: (slab, i, j))
        out_shape = jax.ShapeDtypeStruct((count, m, n), out_dtype)
        if buf is not None:
            in_specs.append(pl.BlockSpec(memory_space=pl.ANY))
            operands.append(buf)
            aliases = {2: 0}
    return pl.pallas_call(
        body, name=name, grid=(m // tm, n // tn, nk), in_specs=in_specs, out_specs=out_spec, out_shape=out_shape,
        scratch_shapes=[pltpu.VMEM((tm, tn), F32)] if nk > 1 else [], input_output_aliases=aliases,
        compiler_params=_params(dimension_semantics=("arbitrary", "arbitrary", "arbitrary")),
    )(*operands)


def _small_mm(name, a, b, mode):
    dims = {"nn": ((1,), (0,)), "tn": ((0,), (0,))}[mode]
    m = a.shape[0] if mode == "nn" else a.shape[1]

    def body(a_ref, b_ref, o_ref):
        o_ref[...] = lax.dot_general(a_ref[...], b_ref[...], (dims, ((), ())), precision=lax.Precision.HIGHEST,
                                     preferred_element_type=F32)

    return pl.pallas_call(body, name=name, out_shape=jax.ShapeDtypeStruct((m, b.shape[1]), F32),
                          compiler_params=_params())(a, b)


def _norm(x):
    mu = jnp.mean(x, axis=-1, keepdims=True)
    xc = x - mu
    rstd = lax.rsqrt(jnp.mean(xc * xc, axis=-1, keepdims=True) + LN_EPS)
    return xc * rstd, rstd


def _norm_bwd(dn, n, rstd):
    return rstd * (dn - jnp.mean(dn, axis=-1, keepdims=True) - n * jnp.mean(dn * n, axis=-1, keepdims=True))


def _colsum(v):
    return jnp.sum(v, axis=0, keepdims=True)


def _gelu(x):
    return 0.5 * x * (1.0 + jnp.tanh(GELU_K * (x + GELU_C * x * x * x)))


def _gelu_grad(x):
    t = jnp.tanh(GELU_K * (x + GELU_C * x * x * x))
    return 0.5 * (1.0 + t) + 0.5 * x * (1.0 - t * t) * GELU_K * (1.0 + 3.0 * GELU_C * x * x)


def _log_sigmoid_parts(z):
    lb = jnp.minimum(z, 0.0) - jnp.log(1.0 + jnp.exp(-jnp.abs(z)))
    return lb, lb - z


def _qkv_prep(proj, t):
    s = proj.shape[0]
    nb, nhp = s // t, 512 // LANES

    def body(q_ref, k_ref, v_ref, qs_ref, kb_ref, vb_ref, kt_ref, vt_ref):
        qs_ref[...] = (q_ref[...] * (1.0 / math.sqrt(HEAD_DIM))).astype(BF16)
        k, v = k_ref[...], v_ref[...]
        kb_ref[...] = k.astype(BF16)
        vb_ref[...] = v.astype(BF16)
        for hp in range(nhp):
            kt_ref[hp, 0] = k[:, hp * LANES:(hp + 1) * LANES].T.astype(BF16)
            vt_ref[hp, 0] = v[:, hp * LANES:(hp + 1) * LANES].T.astype(BF16)

    col = lambda cb: pl.BlockSpec((t, 512), lambda i, cb=cb: (i, cb))
    row_out = pl.BlockSpec((t, 512), lambda i: (i, 0))
    t_out = pl.BlockSpec((nhp, 1, LANES, t), lambda i: (0, i, 0, 0))
    return pl.pallas_call(
        body, name="qkv_prep", grid=(nb,), in_specs=[col(0), col(1), col(2)],
        out_specs=[row_out, row_out, row_out, t_out, t_out],
        out_shape=[jax.ShapeDtypeStruct((s, 512), BF16)] * 3 + [jax.ShapeDtypeStruct((nhp, nb, LANES, t), BF16)] * 2,
        compiler_params=_params(dimension_semantics=("arbitrary",)),
    )(proj, proj, proj)


def _tile_masks(t):
    row = lax.broadcasted_iota(jnp.int32, (t, t), 0)
    col = lax.broadcasted_iota(jnp.int32, (t, t), 1)
    return row, col


DEAD_LOG_WEIGHT = -110.0


def _walk_down(i, tiles, state, alive):
    st = lax.cond(i == 0, lambda s_: tiles([i], s_, [True]), lambda s_: tiles([i, i - 1], s_, [True, False]), state)
    n = jnp.maximum(i - 1, 0)

    def pair(c):
        return c[0] + 1, tiles([i - 2 - 2 * c[0], i - 3 - 2 * c[0]], c[1], [False, False])

    p, st = lax.while_loop(lambda c: (c[0] < n // 2) & alive(c[1]), pair, (jnp.int32(0), st))
    return lax.cond((n % 2 == 1) & (p == n // 2) & alive(st), lambda s_: tiles([0], s_, [False]), lambda s_: s_, st)


def _walk_up(i, first, tiles, state):
    n = jnp.maximum(i - 1 - first, 0)
    st = lax.fori_loop(0, n // 2, lambda p, s_: tiles([first + 2 * p, first + 2 * p + 1], s_, [False, False]), state)
    st = lax.cond(n % 2 == 1, lambda s_: tiles([i - 2], s_, [False]), lambda s_: s_, st)
    return lax.cond(i == 0, lambda s_: tiles([i], s_, [True]), lambda s_: tiles([i - 1, i], s_, [False, True]), st)


def _nt(a, b):
    return lax.dot_general(a, b, (((1,), (1,)), ((), ())), preferred_element_type=F32)


def _nn(a, b):
    return jnp.dot(a, b, preferred_element_type=F32)


def _attn_fwd(qs, k, vt3, t):
    s = qs.shape[0]
    nb, nhp = s // t, qs.shape[1] // LANES

    def body(q_ref, k_ref, vt_ref, o_ref, car_ref):
        i = pl.program_id(1)
        q2 = q_ref[...]
        lane_q = lax.broadcasted_iota(jnp.int32, q2.shape, 1)
        row, col = _tile_masks(t)
        later = (col > row).astype(BF16)
        valid = row < col
        orow = lax.broadcasted_iota(jnp.int32, (LANES, t), 0)
        car_ref[...] = jnp.full(car_ref.shape, 2.0 * DEAD_LOG_WEIGHT, F32)
        qh = [jnp.where((lane_q < HEAD_DIM) == (hh == 0), q2, jnp.zeros_like(q2)) for hh in range(2)]

        def tiles(js, state, diagonal):
            chains = [(n, hh) for n in range(len(js)) for hh in range(2)]
            kb = [k_ref[pl.ds(pl.multiple_of(j * t, t), t), :] for j in js]
            z = {ch: _nt(kb[ch[0]], qh[ch[1]]) for ch in chains}
            lb, aft, csum = {}, {}, {}
            for ch in chains:
                lb[ch], l1m = _log_sigmoid_parts(z[ch])
                if diagonal[ch[0]]:
                    l1m = jnp.where(valid, l1m, 0.0)
                aft[ch] = _nn(later, l1m.astype(BF16))
                csum[ch] = _colsum(l1m)
            state = list(state)
            for ch in chains:
                n, hh = ch
                c_after, acc = state[hh]
                w = jnp.exp(lb[ch] + aft[ch] + c_after)
                if diagonal[ch[0]]:
                    w = jnp.where(valid, w, 0.0)
                car_ref[hh, pl.ds(js[n], 1), :] = c_after
                state[hh] = (c_after + csum[ch], acc + _nn(vt_ref[0, js[n]], w.astype(BF16)))
            return tuple(state)

        def alive(state):
            return jnp.max(jnp.maximum(state[0][0], state[1][0])) >= DEAD_LOG_WEIGHT

        zero = (jnp.zeros((1, t), F32), jnp.zeros((LANES, t), F32))
        (_, acc0), (_, acc1) = _walk_down(i, tiles, (zero, zero), alive)
        o_ref[...] = jnp.where(orow < HEAD_DIM, acc0, acc1).T.astype(o_ref.dtype)

    return pl.pallas_call(
        body, name="attn_fwd", grid=(nhp, nb),
        in_specs=[pl.BlockSpec((t, LANES), lambda hp, i: (i, hp)),
                  pl.BlockSpec((s, LANES), lambda hp, i: (0, hp)),
                  pl.BlockSpec((1, nb, LANES, t), lambda hp, i: (hp, 0, 0, 0))],
        out_specs=[pl.BlockSpec((t, LANES), lambda hp, i: (i, hp)),
                   pl.BlockSpec((2, nb, t), lambda hp, i: (hp, 0, i))],
        out_shape=[jax.ShapeDtypeStruct((s, nhp * LANES), BF16), jax.ShapeDtypeStruct((2 * nhp, nb, s), F32)],
        compiler_params=_params(dimension_semantics=("arbitrary", "arbitrary")),
    )(qs, k, vt3)


def _attn_bwd(qs, do, k, v, kt3, car, t):
    s = qs.shape[0]
    nb, nhp = s // t, qs.shape[1] // LANES

    def body(q_ref, do_ref, k_ref, v_ref, kt_ref, car_ref, dq_ref, dk_ref, dv_ref):
        i = pl.program_id(1)

        @pl.when(i == 0)
        def _():
            dk_ref[...] = jnp.zeros_like(dk_ref)
            dv_ref[...] = jnp.zeros_like(dv_ref)

        q2, do2 = q_ref[...], do_ref[...]
        lane_q = lax.broadcasted_iota(jnp.int32, q2.shape, 1)
        row, col = _tile_masks(t)
        later = (col > row).astype(BF16)
        earlier = (col < row).astype(BF16)
        valid = row < col
        orow = lax.broadcasted_iota(jnp.int32, (LANES, t), 0)
        head = [(lane_q < HEAD_DIM) == (hh == 0) for hh in range(2)]
        qh = [jnp.where(hm, q2, jnp.zeros_like(q2)) for hm in head]
        doh = [jnp.where(hm, do2, jnp.zeros_like(do2)) for hm in head]

        def tiles(js, state, diagonal):
            chains = [(n, hh) for n in range(len(js)) for hh in range(2)]
            rows = [pl.ds(pl.multiple_of(j * t, t), t) for j in js]
            kb = [k_ref[r, :] for r in rows]
            vb = [v_ref[r, :] for r in rows]
            z = {ch: _nt(kb[ch[0]], qh[ch[1]]) for ch in chains}
            dw = {ch: _nt(vb[ch[0]], doh[ch[1]]) for ch in chains}
            lb, beta, aft = {}, {}, {}
            for ch in chains:
                lb[ch], l1m = _log_sigmoid_parts(z[ch])
                beta[ch] = jnp.exp(lb[ch])
                if diagonal[ch[0]]:
                    l1m = jnp.where(valid, l1m, 0.0)
                aft[ch] = _nn(later, l1m.astype(BF16))
            w, g, gsum, g_in = {}, {}, {}, {}
            for ch in chains:
                n, hh = ch
                w[ch] = jnp.exp(lb[ch] + aft[ch] + car_ref[hh, pl.ds(js[n], 1), :])
                if diagonal[ch[0]]:
                    w[ch] = jnp.where(valid, w[ch], 0.0)
                g[ch] = dw[ch] * w[ch]
                g_in[ch] = _nn(earlier, g[ch].astype(BF16))
                gsum[ch] = _colsum(g[ch])
            state = list(state)
            dk_t, dv_t = [None] * len(js), [None] * len(js)
            for ch in chains:
                n, hh = ch
                c_g, dqt = state[hh]
                dz = g[ch] - beta[ch] * (g[ch] + g_in[ch] + c_g)
                if diagonal[ch[0]]:
                    dz = jnp.where(valid, dz, 0.0)
                dzb, wb = dz.astype(BF16), w[ch].astype(BF16)
                dk_h, dv_h = _nn(dzb, qh[hh]), _nn(wb, doh[hh])
                dk_t[n] = dk_h if dk_t[n] is None else dk_t[n] + dk_h
                dv_t[n] = dv_h if dv_t[n] is None else dv_t[n] + dv_h
                state[hh] = (c_g + gsum[ch], dqt + _nn(kt_ref[0, js[n]], dzb))
            for n in range(len(js)):
                dk_ref[rows[n], :] += dk_t[n]
                dv_ref[rows[n], :] += dv_t[n]
            return tuple(state)

        reach = jnp.max(jnp.max(car_ref[...], axis=2, keepdims=True), axis=0)
        dead = (reach < DEAD_LOG_WEIGHT) & (lax.broadcasted_iota(jnp.int32, reach.shape, 0) < i)
        first = jnp.sum(jnp.where(dead, 1.0, 0.0)).astype(jnp.int32)
        zero = (jnp.zeros((1, t), F32), jnp.zeros((LANES, t), F32))
        (_, dq0), (_, dq1) = _walk_up(i, first, tiles, (zero, zero))
        dq_ref[...] = jnp.where(orow < HEAD_DIM, dq0, dq1).T

    tile_spec = pl.BlockSpec((t, LANES), lambda hp, i: (i, hp))
    whole = pl.BlockSpec((s, LANES), lambda hp, i: (0, hp))
    return pl.pallas_call(
        body, name="attn_bwd", grid=(nhp, nb),
        in_specs=[tile_spec, tile_spec, whole, whole,
                  pl.BlockSpec((1, nb, LANES, t), lambda hp, i: (hp, 0, 0, 0)),
                  pl.BlockSpec((2, nb, t), lambda hp, i: (hp, 0, i))],
        out_specs=[tile_spec, whole, whole],
        out_shape=[jax.ShapeDtypeStruct((s, nhp * LANES), F32)] * 3,
        compiler_params=_params(dimension_semantics=("arbitrary", "arbitrary")),
    )(qs, do, k, v, kt3, car)


SCAN_LANES = 1024
SCAN_ROWS = 8


def _scan_chunks(v):
    n = v.shape[1] // (2 * LANES)
    return [(v[:, c * 2 * LANES:c * 2 * LANES + LANES], v[:, c * 2 * LANES + LANES:(c + 1) * 2 * LANES]) for c in range(n)]


def _scan_tables(lr, li, reverse):
    if reverse:
        li = -li
    row = lax.broadcasted_iota(jnp.int32, (SCAN_ROWS, LANES), 0)
    powers = [(lr, li)]
    for _ in range(SCAN_ROWS - 1):
        pr, pi = powers[-1]
        powers.append((pr * lr - pi * li, pr * li + pi * lr))
    levels = []
    for d in (1, 2, 4):
        keep = (row < SCAN_ROWS - d) if reverse else (row >= d)
        levels.append((SCAN_ROWS - d if reverse else d,
                       (jnp.where(keep, powers[d - 1][0], 0.0), jnp.where(keep, powers[d - 1][1], 0.0))))
    pr = pi = jnp.zeros((SCAN_ROWS, LANES), F32)
    for r in range(SCAN_ROWS):
        steps = SCAN_ROWS - r if reverse else r + 1
        pr = jnp.where(row == r, powers[steps - 1][0], pr)
        pi = jnp.where(row == r, powers[steps - 1][1], pi)
    return levels, (pr, pi)


def _s5_fwd(proj, u_off, bmat, cmat, lam):
    s, w = proj.shape[0], bmat.shape[1]
    tt = _pick(s, (512, 256, 128, 8))
    nt = s // tt
    cin = bmat.shape[0] // SSM_BLOCKS

    def body(u_ref, b_ref, c_ref, lam_ref, h_ref, y_ref, x_ref, st_ref):
        @pl.when(pl.program_id(1) == 0)
        def _():
            st_ref[...] = jnp.zeros_like(st_ref)

        tables = [_scan_tables(lr, li, reverse=False) for lr, li in _scan_chunks(lam_ref[...])]
        x_ref[...] = _nn(u_ref[...].astype(BF16), b_ref[...])

        def tile(it, last):
            r0 = pl.multiple_of(it * SCAN_ROWS, SCAN_ROWS)
            last, parts = list(last), []
            for c, (xr, xi) in enumerate(_scan_chunks(x_ref[pl.ds(r0, SCAN_ROWS), :])):
                levels, (pr, pi) = tables[c]
                for d, (ar, ai) in levels:
                    sr, si = pltpu.roll(xr, d, 0), pltpu.roll(xi, d, 0)
                    xr, xi = xr + ar * sr - ai * si, xi + ar * si + ai * sr
                br, bi = last[2 * c], last[2 * c + 1]
                hr = xr + pr * br - pi * bi
                hi = xi + pr * bi + pi * br
                last[2 * c], last[2 * c + 1] = hr[SCAN_ROWS - 1:], hi[SCAN_ROWS - 1:]
                parts += [hr, hi]
            h_ref[pl.ds(r0, SCAN_ROWS), :] = jnp.concatenate(parts, axis=1)
            return tuple(last)

        st = st_ref[0:1, :]
        init = tuple(st[:, c * LANES:(c + 1) * LANES] for c in range(SCAN_LANES // LANES))
        fin = lax.fori_loop(0, tt // SCAN_ROWS, tile, init)
        st_ref[0:1, :] = jnp.concatenate(fin, axis=1)
        y_ref[...] = _nn(h_ref[...].astype(BF16), c_ref[...])

    return pl.pallas_call(
        body, name="s5_fwd", grid=(SSM_BLOCKS, nt),
        in_specs=[pl.BlockSpec((tt, cin), lambda kb, i: (i, u_off // cin + kb)),
                  pl.BlockSpec((cin, SCAN_LANES), lambda kb, i: (kb, kb)),
                  pl.BlockSpec((SCAN_LANES, cin), lambda kb, i: (kb, kb)),
                  pl.BlockSpec((1, SCAN_LANES), lambda kb, i: (0, kb))],
        out_specs=[pl.BlockSpec((tt, SCAN_LANES), lambda kb, i: (i, kb)), pl.BlockSpec((tt, cin), lambda kb, i: (i, kb))],
        out_shape=[jax.ShapeDtypeStruct((s, w), F32), jax.ShapeDtypeStruct((s, bmat.shape[0]), F32)],
        scratch_shapes=[pltpu.VMEM((tt, SCAN_LANES), F32), pltpu.VMEM((SCAN_ROWS, SCAN_LANES), F32)],
        compiler_params=_params(dimension_semantics=("arbitrary", "arbitrary")),
    )(proj, bmat, cmat, lam)


def _s5_bwd(dy, h, proj, u_off, bmat, cmat, lam):
    s, w = h.shape
    tt = _pick(s, (512, 256, 128, 8))
    nt = s // tt
    cin = bmat.shape[0] // SSM_BLOCKS

    def body(dy_ref, h_ref, u_ref, b_ref, c_ref, lam_ref, du_ref, dlam_ref, db_ref, dc_ref, e_ref, a_ref, st_ref):
        @pl.when(pl.program_id(1) == 0)
        def _():
            st_ref[...] = jnp.zeros_like(st_ref)
            dlam_ref[...] = jnp.zeros_like(dlam_ref)
            db_ref[...] = jnp.zeros_like(db_ref)
            dc_ref[...] = jnp.zeros_like(dc_ref)

        tables = [_scan_tables(lr, li, reverse=True) for lr, li in _scan_chunks(lam_ref[...])]
        nch = len(tables)
        row = lax.broadcasted_iota(jnp.int32, (SCAN_ROWS, LANES), 0)
        e_ref[...] = _nt(dy_ref[...], c_ref[...])

        def tile(it, carry):
            r0 = pl.multiple_of((tt // SCAN_ROWS - 1 - it) * SCAN_ROWS, SCAN_ROWS)
            e_c = _scan_chunks(e_ref[pl.ds(r0, SCAN_ROWS), :])
            h_c = _scan_chunks(h_ref[pl.ds(r0, SCAN_ROWS), :])
            carry, parts = list(carry), []
            for c in range(nch):
                (yr, yi), (hr, hi) = e_c[c], h_c[c]
                levels, (pr, pi) = tables[c]
                for shift, (lr, li) in levels:
                    sr, si = pltpu.roll(yr, shift, 0), pltpu.roll(yi, shift, 0)
                    yr, yi = yr + lr * sr - li * si, yi + lr * si + li * sr
                nr, ni, dr, di = carry[4 * c:4 * c + 4]
                ar = yr + pr * nr - pi * ni
                ai = yi + pr * ni + pi * nr
                nxr = jnp.where(row == SCAN_ROWS - 1, nr, pltpu.roll(ar, SCAN_ROWS - 1, 0))
                nxi = jnp.where(row == SCAN_ROWS - 1, ni, pltpu.roll(ai, SCAN_ROWS - 1, 0))
                carry[4 * c:4 * c + 4] = [ar[0:1], ai[0:1], dr + nxr * hr + nxi * hi, di + nxi * hr - nxr * hi]
                parts += [ar, ai]
            a_ref[pl.ds(r0, SCAN_ROWS), :] = jnp.concatenate(parts, axis=1)
            return tuple(carry)

        st, dl = st_ref[0:1, :], dlam_ref[...]
        init = []
        for c in range(nch):
            lo = c * 2 * LANES
            init += [st[:, lo:lo + LANES], st[:, lo + LANES:lo + 2 * LANES],
                     dl[:, lo:lo + LANES], dl[:, lo + LANES:lo + 2 * LANES]]
        fin = lax.fori_loop(0, tt // SCAN_ROWS, tile, tuple(init))
        st_ref[0:1, :] = jnp.concatenate([fin[4 * c + q] for c in range(nch) for q in (0, 1)], axis=1)
        dlam_ref[...] = jnp.concatenate([fin[4 * c + q] for c in range(nch) for q in (2, 3)], axis=1)

        @pl.when(pl.program_id(1) == nt - 1)
        def _():
            dlam_ref[0:1, :] = jnp.sum(dlam_ref[...], axis=0, keepdims=True)

        adj = a_ref[...].astype(BF16)
        du_ref[...] = _nt(adj, b_ref[...])
        rows_first = (((0,), (0,)), ((), ()))
        db_ref[...] += lax.dot_general(u_ref[...].astype(BF16), adj, rows_first, preferred_element_type=F32)
        dc_ref[...] += lax.dot_general(h_ref[...].astype(BF16), dy_ref[...], rows_first, preferred_element_type=F32)

    def rev(width, col):
        return pl.BlockSpec((tt, width), lambda kb, i: (nt - 1 - i, col(kb)))

    return pl.pallas_call(
        body, name="s5_bwd", grid=(SSM_BLOCKS, nt),
        in_specs=[rev(cin, lambda kb: kb), rev(SCAN_LANES, lambda kb: kb), rev(cin, lambda kb: u_off // cin + kb),
                  pl.BlockSpec((cin, SCAN_LANES), lambda kb, i: (kb, kb)),
                  pl.BlockSpec((SCAN_LANES, cin), lambda kb, i: (kb, kb)),
                  pl.BlockSpec((1, SCAN_LANES), lambda kb, i: (0, kb))],
        out_specs=[rev(cin, lambda kb: kb), pl.BlockSpec((SCAN_ROWS, SCAN_LANES), lambda kb, i: (0, kb)),
                   pl.BlockSpec((cin, SCAN_LANES), lambda kb, i: (kb, 0)), pl.BlockSpec((SCAN_LANES, cin), lambda kb, i: (kb, 0))],
        out_shape=[jax.ShapeDtypeStruct((s, bmat.shape[0]), F32), jax.ShapeDtypeStruct((SCAN_ROWS, w), F32),
                   jax.ShapeDtypeStruct((bmat.shape[0], SCAN_LANES), F32), jax.ShapeDtypeStruct((w, cin), F32)],
        scratch_shapes=[pltpu.VMEM((tt, SCAN_LANES), F32), pltpu.VMEM((tt, SCAN_LANES), F32),
                        pltpu.VMEM((SCAN_ROWS, SCAN_LANES), F32)],
        compiler_params=_params(dimension_semantics=("arbitrary", "arbitrary")),
    )(dy, h, proj, bmat, cmat, lam)


def _ssm_params_fwd(a_re, a_im, log_dt, b_re, b_im):
    def body(ar_ref, ai_ref, ldt_ref, br_ref, bi_ref, lr_ref, li_ref, bbr_ref, bbi_ref):
        ar, ai, dt = ar_ref[...], ai_ref[...], jnp.exp(ldt_ref[...])
        mag = jnp.exp(ar * dt)
        lr, li = mag * jnp.cos(ai * dt), mag * jnp.sin(ai * dt)
        den = ar * ar + ai * ai
        cr = ((lr - 1.0) * ar + li * ai) / den
        ci = (li * ar - (lr - 1.0) * ai) / den
        br, bi = br_ref[...], bi_ref[...]
        lr_ref[...], li_ref[...] = lr, li
        bbr_ref[...] = cr * br - ci * bi
        bbi_ref[...] = cr * bi + ci * br

    n = a_re.shape[0]
    v1, v16 = jax.ShapeDtypeStruct((n, 1), F32), jax.ShapeDtypeStruct((n, SSM_GROUP), F32)
    return pl.pallas_call(body, name="ssm_params_fwd", out_shape=[v1, v1, v16, v16],
                          compiler_params=_params())(a_re, a_im, log_dt, b_re, b_im)


def _ssm_params_bwd(a_re, a_im, log_dt, b_re, b_im, g_lr, g_li, g_bbr, g_bbi):
    n = a_re.shape[0]

    def body(ar_ref, ai_ref, ldt_ref, br_ref, bi_ref, glr_ref, gli_ref, gbr_ref, gbi_ref,
             dar_ref, dai_ref, dldt_ref, dbr_ref, dbi_ref):
        ar, ai, dt = ar_ref[...], ai_ref[...], jnp.exp(ldt_ref[...])
        mag = jnp.exp(ar * dt)
        lr, li = mag * jnp.cos(ai * dt), mag * jnp.sin(ai * dt)
        den = ar * ar + ai * ai
        cr = ((lr - 1.0) * ar + li * ai) / den
        ci = (li * ar - (lr - 1.0) * ai) / den
        br, bi, gbr, gbi = br_ref[...], bi_ref[...], gbr_ref[...], gbi_ref[...]
        dbr_ref[...] = gbr * cr + gbi * ci
        dbi_ref[...] = gbi * cr - gbr * ci
        gcr = jnp.sum(gbr * br + gbi * bi, axis=1, keepdims=True)
        gci = jnp.sum(gbi * br - gbr * bi, axis=1, keepdims=True)
        ir, ii = ar / den, -ai / den
        glr = glr_ref[...] + gcr * ir + gci * ii
        gli = gli_ref[...] + gci * ir - gcr * ii
        qr, qi = cr * ir - ci * ii, cr * ii + ci * ir
        gar = -(gcr * qr + gci * qi)
        gai = -(gci * qr - gcr * qi)
        gxr = glr * lr + gli * li
        gxi = gli * lr - glr * li
        dar_ref[...] = gar + gxr * dt
        dai_ref[...] = gai + gxi * dt
        gdt = (gxr * ar + gxi * ai) * dt
        rowg = lax.broadcasted_iota(jnp.int32, (n, SSM_GROUPS), 0) // SSM_STATE
        colg = lax.broadcasted_iota(jnp.int32, (n, SSM_GROUPS), 1)
        dldt_ref[...] = jnp.sum(jnp.where(rowg == colg, gdt, 0.0), axis=0, keepdims=True)

    v1, v16 = jax.ShapeDtypeStruct((n, 1), F32), jax.ShapeDtypeStruct((n, SSM_GROUP), F32)
    return pl.pallas_call(body, name="ssm_params_bwd",
                          out_shape=[v1, v1, jax.ShapeDtypeStruct((1, SSM_GROUPS), F32), v16, v16],
                          compiler_params=_params())(a_re, a_im, log_dt, b_re, b_im, g_lr, g_li, g_bbr, g_bbi)


def _interleave(re, im, axis):
    shp = list(re.shape)
    new = shp[:axis] + [shp[axis] // LANES, LANES] + shp[axis + 1:]
    st = jnp.stack([re.reshape(new), im.reshape(new)], axis=axis + 1)
    return st.reshape(shp[:axis] + [2 * shp[axis]] + shp[axis + 1:])


def _deinterleave(v, axis):
    shp = list(v.shape)
    r = v.reshape(shp[:axis] + [shp[axis] // (2 * LANES), 2, LANES] + shp[axis + 1:])
    out = shp[:axis] + [shp[axis] // 2] + shp[axis + 1:]
    return (lax.index_in_dim(r, 0, axis + 1, keepdims=False).reshape(out),
            lax.index_in_dim(r, 1, axis + 1, keepdims=False).reshape(out))


def _b_matrix(bbr, bbi):
    eye = jnp.eye(SSM_GROUPS, dtype=F32)

    def blockdiag(v):
        x = v.reshape(SSM_GROUPS, SSM_STATE, SSM_GROUP).transpose(0, 2, 1)
        return (eye[:, None, :, None] * x[:, :, None, :]).reshape(SSM_GROUPS * SSM_GROUP, N_STATE)

    return _interleave(blockdiag(bbr), blockdiag(bbi), 1)


def _diag_blocks(v, rows, cols):
    per = SSM_GROUPS // SSM_BLOCKS
    return jnp.stack([v[g * rows:(g + 1) * rows, (g % per) * cols:(g % per + 1) * cols] for g in range(SSM_GROUPS)])


def _b_matrix_grad(d):
    def diag(v):
        return _diag_blocks(v, SSM_GROUP, SSM_STATE).transpose(0, 2, 1).reshape(N_STATE, SSM_GROUP)

    dr, di = _deinterleave(d, 1)
    return diag(dr), diag(di)


def _c_matrix(c_re, c_im):
    eye = jnp.eye(SSM_GROUPS, dtype=F32)

    def blockdiag(v):
        x = v.transpose(0, 2, 1)
        return (x[:, :, None, :] * eye[:, None, :, None]).reshape(N_STATE, SSM_GROUPS * SSM_GROUP)

    return _interleave(blockdiag(c_re), blockdiag(-c_im), 0)


def _c_matrix_grad(d):
    def diag(v):
        return _diag_blocks(v, SSM_STATE, SSM_GROUP).transpose(0, 2, 1)

    dr, di = _deinterleave(d, 0)
    return diag(dr), -diag(di)


def _row(v):
    return v.reshape(1, -1)


def _ssm_inputs(p):
    rows = lambda v: v.reshape(N_STATE, -1)
    ldt = jnp.repeat(p["ssm_log_dt"], SSM_STATE).reshape(N_STATE, 1)
    return rows(p["ssm_a_re"]), rows(p["ssm_a_im"]), ldt, rows(p["ssm_b_re"]), rows(p["ssm_b_im"])


def _layer_fwd(x, mod, p, tag):
    d = x.shape[1]
    sh_m, sc_m, g_m, sh_f, sc_f, g_f = [_row(mod[i]) for i in range(6)]
    nm = lambda s: f"{s}_{tag}"

    def lnmod(x, sc, sh):
        return _norm(x)[0] * (1.0 + sc) + sh

    h1 = _rowwise(nm("lnmod1"), lnmod, [(x, 0, d)], [sc_m, sh_m], [(d, BF16)])
    proj = _mm(nm("proj"), h1, p["w_in"], "nn")
    t = min(ATT_TILE, x.shape[0])
    qs, kb, vb, kt3, vt3 = _qkv_prep(proj, t)
    att, car = _attn_fwd(qs, kb, vt3, t)
    y_sb = _mm(nm("sb_up"), att, p["w_sb_up"], "nn")

    lam_r, lam_i, bbr, bbi = _ssm_params_fwd(*_ssm_inputs(p))
    lam = _interleave(lam_r.reshape(1, N_STATE), lam_i.reshape(1, N_STATE), 1)
    bmat = _b_matrix(bbr, bbi).astype(BF16)
    cmat = _c_matrix(p["ssm_c_re"], p["ssm_c_im"]).astype(BF16)
    hst, yc = _s5_fwd(proj, U_OFFSET, bmat, cmat, lam)

    def ssm_act(yc, u, dsk):
        y0 = yc + dsk * u
        return y0, _gelu(y0)

    y0, y1 = _rowwise(nm("ssm_act"), ssm_act, [(yc, 0, 512), (proj, 3, 512)], [_row(p["ssm_d"])], [(512, F32), (512, F32)])
    gl = _mm(nm("glu"), y1, p["w_glu"], "nn")
    y2 = _rowwise(nm("glu_act"), lambda y1, gl, b: y1 * jax.nn.sigmoid(gl + b), [(y1, 0, 512), (gl, 0, 512)],
                  [_row(p["b_glu"])], [(512, BF16)])
    y_ssm = _mm(nm("ssm_up"), y2, p["w_ssm_up"], "nn")

    def merge(gsb, gss, ysb, yss):
        return jax.nn.sigmoid(gsb) * ysb + jax.nn.sigmoid(gss) * yss

    merged = _rowwise(nm("merge"), merge, [(proj, 2, d), (proj, 3, d), (y_sb, 0, d), (y_ssm, 0, d)], [], [(d, BF16)])
    y = _mm(nm("out"), merged, p["w_out"], "nn")

    def resid_ln(x, y, g, lg, lb):
        return _norm(ALPHA * x + (1.0 + g) * y)[0] * lg + lb

    x1 = _rowwise(nm("ln1"), resid_ln, [(x, 0, d), (y, 0, d)], [g_m, _row(p["ln1_g"]), _row(p["ln1_b"])], [(d, F32)])
    h2 = _rowwise(nm("lnmod2"), lnmod, [(x1, 0, d)], [sc_f, sh_f], [(d, BF16)])
    f = _mm(nm("ffn_in"), h2, p["w_ffn_in"], "nn", out_dtype=BF16)
    fh = f.shape[1] // 2

    def swiglu(g, u):
        g = g.astype(F32)
        return g * jax.nn.sigmoid(g) * u.astype(F32)

    act = _rowwise(nm("swiglu"), swiglu, [(f, 0, fh), (f, 1, fh)], [], [(fh, BF16)])
    yf = _mm(nm("ffn_out"), act, p["w_ffn_out"], "nn")
    x2 = _rowwise(nm("ln2"), resid_ln, [(x1, 0, d), (yf, 0, d)], [g_f, _row(p["ln2_g"]), _row(p["ln2_b"])], [(d, F32)])
    saved = dict(x=x, h1=h1, proj=proj, qs=qs, kb=kb, vb=vb, kt3=kt3, car=car, att=att, y_sb=y_sb, lam=lam, bmat=bmat,
                 cmat=cmat, hst=hst, y0=y0, y1=y1, gl=gl, y2=y2, y_ssm=y_ssm, merged=merged, y=y, x1=x1, h2=h2, f=f,
                 act=act, yf=yf, t=t)
    return x2, saved


def _layer_bwd(dx2, mod, p, sv, layer, depth, stacked):
    d = dx2.shape[1]
    sh_m, sc_m, g_m, sh_f, sc_f, g_f = [_row(mod[i]) for i in range(6)]
    nm = lambda s: f"{s}_{layer}"
    grads = {}

    def weight_grad(n, a, b, **kw):
        grads[n] = _mm(nm("d" + n), a, b, "tn", out_dtype=BF16, into=(stacked.get(n), layer, depth), **kw)

    def resid_ln_bwd(x, y, dxo, g, lg):
        n, rstd = _norm(ALPHA * x + (1.0 + g) * y)
        dr = _norm_bwd(dxo * lg, n, rstd)
        return ALPHA * dr, (1.0 + g) * dr, _colsum(dxo * n), _colsum(dxo), _colsum(dr * y)

    def lnmod_bwd(x, dh, dxa, sc):
        n, rstd = _norm(x)
        return dxa + _norm_bwd(dh * (1.0 + sc), n, rstd), _colsum(dh * n), _colsum(dh)

    dx1a, dyf, grads["ln2_g"], grads["ln2_b"], dg_f = _rowwise(
        nm("ln2_bwd"), resid_ln_bwd, [(sv["x1"], 0, d), (sv["yf"], 0, d), (dx2, 0, d)], [g_f, _row(p["ln2_g"])],
        [(d, F32), (d, BF16)], [d, d, d])
    dact = _mm(nm("d_act"), dyf, p["w_ffn_out"], "nt")
    weight_grad("w_ffn_out", sv["act"], dyf)
    fh = sv["f"].shape[1] // 2

    def swiglu_bwd(g, u, da):
        g, u = g.astype(F32), u.astype(F32)
        sg = jax.nn.sigmoid(g)
        return jnp.concatenate([da * u * sg * (1.0 + g * (1.0 - sg)), da * g * sg], axis=1)

    df = _rowwise(nm("swiglu_bwd"), swiglu_bwd, [(sv["f"], 0, fh), (sv["f"], 1, fh), (dact, 0, fh)], [], [(2 * fh, BF16)])
    dh2 = _mm(nm("d_h2"), df, p["w_ffn_in"], "nt")
    weight_grad("w_ffn_in", sv["h2"], df)
    dx1, dsc_f, dsh_f = _rowwise(nm("lnmod2_bwd"), lnmod_bwd, [(sv["x1"], 0, d), (dh2, 0, d), (dx1a, 0, d)], [sc_f],
                                 [(d, F32)], [d, d])
    dxa, dy, grads["ln1_g"], grads["ln1_b"], dg_m = _rowwise(
        nm("ln1_bwd"), resid_ln_bwd, [(sv["x"], 0, d), (sv["y"], 0, d), (dx1, 0, d)], [g_m, _row(p["ln1_g"])],
        [(d, F32), (d, BF16)], [d, d, d])
    dmerged = _mm(nm("d_merged"), dy, p["w_out"], "nt")
    weight_grad("w_out", sv["merged"], dy)

    def merge_bwd(gsb, gss, ysb, yss, dm):
        s1, s2 = jax.nn.sigmoid(gsb), jax.nn.sigmoid(gss)
        return s1 * dm, s2 * dm, dm * ysb * s1 * (1.0 - s1), dm * yss * s2 * (1.0 - s2)

    dy_sb, dy_ssm, dg_sb, dg_ssm = _rowwise(
        nm("merge_bwd"), merge_bwd, [(sv["proj"], 2, d), (sv["proj"], 3, d), (sv["y_sb"], 0, d), (sv["y_ssm"], 0, d),
                                     (dmerged, 0, d)], [], [(d, BF16)] * 4)
    dy2 = _mm(nm("d_y2"), dy_ssm, p["w_ssm_up"], "nt")
    weight_grad("w_ssm_up", sv["y2"], dy_ssm)

    def glu_act_bwd(y1, gl, dy2, b):
        sg = jax.nn.sigmoid(gl + b)
        dgl = dy2 * y1 * sg * (1.0 - sg)
        return dy2 * sg, dgl, _colsum(dgl)

    dy1a, dgl, grads["b_glu"] = _rowwise(nm("glu_act_bwd"), glu_act_bwd, [(sv["y1"], 0, 512), (sv["gl"], 0, 512), (dy2, 0, 512)],
                                         [_row(p["b_glu"])], [(512, F32), (512, BF16)], [512])
    dy1b = _mm(nm("d_y1"), dgl, p["w_glu"], "nt")
    weight_grad("w_glu", sv["y1"], dgl)

    def ssm_act_bwd(y0, u, dy1a, dy1b, dsk):
        dy0 = (dy1a + dy1b) * _gelu_grad(y0)
        return dy0, dsk * dy0, _colsum(dy0 * u)

    dy0, du_a, grads["ssm_d"] = _rowwise(nm("ssm_act_bwd"), ssm_act_bwd,
                                         [(sv["y0"], 0, 512), (sv["proj"], 3, 512), (dy1a, 0, 512), (dy1b, 0, 512)],
                                         [_row(p["ssm_d"])], [(512, BF16), (512, F32)], [512])
    du_b, dlam, d_bmat, d_cmat = _s5_bwd(dy0, sv["hst"], sv["proj"], U_OFFSET, sv["bmat"], sv["cmat"], sv["lam"])
    grads["ssm_c_re"], grads["ssm_c_im"] = _c_matrix_grad(d_cmat)
    g_bbr, g_bbi = _b_matrix_grad(d_bmat)
    g_lr, g_li = _deinterleave(dlam[0:1], 1)
    da_re, da_im, dldt, db_re, db_im = _ssm_params_bwd(*_ssm_inputs(p), g_lr.reshape(N_STATE, 1), g_li.reshape(N_STATE, 1),
                                                       g_bbr, g_bbi)
    grads["ssm_a_re"] = da_re.reshape(SSM_GROUPS, SSM_STATE)
    grads["ssm_a_im"] = da_im.reshape(SSM_GROUPS, SSM_STATE)
    grads["ssm_log_dt"] = dldt.reshape(SSM_GROUPS)
    grads["ssm_b_re"] = db_re.reshape(SSM_GROUPS, SSM_STATE, SSM_GROUP)
    grads["ssm_b_im"] = db_im.reshape(SSM_GROUPS, SSM_STATE, SSM_GROUP)
    datt = _mm(nm("d_att"), dy_sb, p["w_sb_up"], "nt", out_dtype=BF16)
    weight_grad("w_sb_up", sv["att"], dy_sb)
    dqs, dk, dv = _attn_bwd(sv["qs"], datt, sv["kb"], sv["vb"], sv["kt3"], sv["car"], sv["t"])

    def dproj_cols(dqs, dk, dv, dua, dub, dgsb, dgss):
        return jnp.concatenate([dqs * (1.0 / math.sqrt(HEAD_DIM)), dk, dv, dua + dub, dgsb.astype(F32), dgss.astype(F32)],
                               axis=1)

    dproj = _rowwise(nm("dproj"), dproj_cols, [(dqs, 0, 512), (dk, 0, 512), (dv, 0, 512), (du_a, 0, 512), (du_b, 0, 512),
                                               (dg_sb, 0, d), (dg_ssm, 0, d)], [], [(2048 + 2 * d, BF16)])
    dh1 = _mm(nm("d_h1"), dproj, p["w_in"], "nt")
    weight_grad("w_in", sv["h1"], dproj)
    dx, dsc_m, dsh_m = _rowwise(nm("lnmod1_bwd"), lnmod_bwd, [(sv["x"], 0, d), (dh1, 0, d), (dxa, 0, d)], [sc_m],
                                [(d, F32)], [d, d])
    for k in ("ln1_g", "ln1_b", "ln2_g", "ln2_b", "ssm_d", "b_glu"):
        grads[k] = grads[k].reshape(-1)
    dmod = jnp.concatenate([dsh_m, dsc_m, dg_m, dsh_f, dsc_f, dg_f], axis=0)
    return dx, dmod, grads


def _loss_head(x, target):
    d = x.shape[1]

    def fn(x, tgt):
        err = x - tgt
        return err * (1.0 / d), _colsum(err * err) * (0.5 / d)

    return _rowwise("loss_head", fn, [(x, 0, d), (target, 0, d)], [], [(d, F32)], [d])


def _place():
    return lax.axis_index("x"), lax.axis_index("y"), lax.axis_index("c")


def _all_gather8(name, block):
    m_per, n = block.shape

    def body(x_ref, out_ref, send_sems, recv_sems, local_sem):
        x, y, c = _place()
        me, sibling = (x, y, c), (x, y, 1 - c)
        chips = [(1 - x, y), (x, 1 - y), (1 - x, 1 - y)]

        def rows(px, py, pc):
            return out_ref.at[pl.ds(pl.multiple_of((4 * px + 2 * py + pc) * m_per, 8), m_per), :]

        def copy(k, blk, to, src=None):
            return pltpu.make_async_remote_copy(src_ref=rows(*blk) if src is None else src, dst_ref=rows(*blk),
                                                send_sem=send_sems.at[k], recv_sem=recv_sems.at[k],
                                                device_id=to, device_id_type=MESH)

        mine = pltpu.make_async_copy(x_ref, rows(*me), local_sem)
        mine.start()
        first = [copy(0, me, sibling, src=x_ref)] + [copy(1 + j, me, (*chip, c), src=x_ref) for j, chip in enumerate(chips)]
        for cp in first:
            cp.start()
        passed = [copy(4 + j, (*chip, c), sibling) for j, chip in enumerate(chips)]
        for j, chip in enumerate(chips):
            copy(1 + j, (*chip, c), me).wait_recv()
            passed[j].start()
        copy(0, sibling, me).wait_recv()
        for j, chip in enumerate(chips):
            copy(4 + j, (*chip, 1 - c), me).wait_recv()
        for cp in first + passed:
            cp.wait_send()
        mine.wait()

    return pl.pallas_call(
        body, name=name, out_shape=jax.ShapeDtypeStruct((8 * m_per, n), block.dtype),
        in_specs=[pl.BlockSpec(memory_space=pltpu.VMEM)], out_specs=pl.BlockSpec(memory_space=pltpu.VMEM),
        scratch_shapes=[pltpu.SemaphoreType.DMA((7,)), pltpu.SemaphoreType.DMA((7,)), pltpu.SemaphoreType.DMA],
        compiler_params=_params(),
    )(block)


def _other_chips(x, y):
    return [(1 - x, y), (x, 1 - y), (1 - x, 1 - y)]


def _gather_weights(shards, by_rows):
    n = len(shards)

    def body(*refs):
        src, dst = refs[:n], refs[n:2 * n]
        send_sems, recv_sems, local_sems = refs[2 * n:]
        x, y, c = _place()

        def block(k, px, py):
            _, r, cols = shards[k].shape
            q = 2 * px + py
            if by_rows[k]:
                return dst[k].at[:, pl.ds(pl.multiple_of(q * r, 16), r), :]
            return dst[k].at[:, :, pl.ds(pl.multiple_of(q * cols, LANES), cols)]

        local = [pltpu.make_async_copy(src[k], block(k, x, y), local_sems.at[k]) for k in range(n)]
        for cp in local:
            cp.start()
        sends = []
        for k in range(n):
            for j, (px, py) in enumerate(_other_chips(x, y)):
                cp = pltpu.make_async_remote_copy(src_ref=src[k], dst_ref=block(k, x, y), send_sem=send_sems.at[k, j],
                                                  recv_sem=recv_sems.at[k, j], device_id=(px, py, c), device_id_type=MESH)
                cp.start()
                sends.append(cp)
        for k in range(n):
            for j, (px, py) in enumerate(_other_chips(x, y)):
                pltpu.make_async_remote_copy(src_ref=src[k], dst_ref=block(k, px, py), send_sem=send_sems.at[k, j],
                                             recv_sem=recv_sems.at[k, j], device_id=(px, py, c),
                                             device_id_type=MESH).wait_recv()
        for cp in sends:
            cp.wait_send()
        for cp in local:
            cp.wait()

    def whole(s, rows):
        l, r, cols = s.shape
        return jax.ShapeDtypeStruct((l, 4 * r, cols) if rows else (l, r, 4 * cols), s.dtype)

    any_spec = pl.BlockSpec(memory_space=pl.ANY)
    return pl.pallas_call(
        body, name="gather_weights", in_specs=[any_spec] * n, out_specs=[any_spec] * n,
        out_shape=[whole(s, rows) for s, rows in zip(shards, by_rows)],
        scratch_shapes=[pltpu.SemaphoreType.DMA((n, 3)), pltpu.SemaphoreType.DMA((n, 3)), pltpu.SemaphoreType.DMA((n,))],
        compiler_params=_params(),
    )(*shards)


def _peer(x, y, c, r):
    fx, fy, fc = (r >> 2) & 1, (r >> 1) & 1, r & 1
    return (x + fx - 2 * x * fx, y + fy - 2 * y * fy, c + fc - 2 * c * fc)


def _scatter_grads(grads, by_rows):
    n = len(grads)

    def half_shape(k):
        l, r, c = grads[k].shape
        return (l, r // 8, c) if by_rows[k] else (l, r // 2, c // 4)

    def body(*refs):
        src, dst = refs[:n], refs[n:2 * n]
        send_sems, recv_sems, local_sems = refs[2 * n:]
        x, y, c = _place()
        me = 4 * x + 2 * y + c

        def window(k, px, py, pc):
            _, hr, hc = half_shape(k)
            q = 2 * px + py
            if by_rows[k]:
                return src[k].at[:, pl.ds(pl.multiple_of((2 * q + pc) * hr, 16), hr), :]
            return src[k].at[:, pl.ds(pl.multiple_of(pc * hr, 16), hr), pl.ds(pl.multiple_of(q * hc, LANES), hc)]

        local = [pltpu.make_async_copy(window(k, x, y, c), dst[k].at[me], local_sems.at[k]) for k in range(n)]
        for cp in local:
            cp.start()
        sends = []
        for k in range(n):
            for r in range(1, 8):
                to = _peer(x, y, c, r)
                cp = pltpu.make_async_remote_copy(src_ref=window(k, *to), dst_ref=dst[k].at[me], send_sem=send_sems.at[k, r - 1],
                                                  recv_sem=recv_sems.at[k, r - 1], device_id=to, device_id_type=MESH)
                cp.start()
                sends.append(cp)
        for k in range(n):
            for r in range(1, 8):
                px, py, pc = _peer(x, y, c, r)
                pltpu.make_async_remote_copy(src_ref=window(k, x, y, c), dst_ref=dst[k].at[4 * px + 2 * py + pc],
                                             send_sem=send_sems.at[k, r - 1], recv_sem=recv_sems.at[k, r - 1],
                                             device_id=(px, py, pc), device_id_type=MESH).wait_recv()
        for cp in sends:
            cp.wait_send()
        for cp in local:
            cp.wait()

    any_spec = pl.BlockSpec(memory_space=pl.ANY)
    return pl.pallas_call(
        body, name="scatter_grads", in_specs=[any_spec] * n, out_specs=[any_spec] * n,
        out_shape=[jax.ShapeDtypeStruct((8, *half_shape(k)), grads[k].dtype) for k in range(n)],
        scratch_shapes=[pltpu.SemaphoreType.DMA((n, 7)), pltpu.SemaphoreType.DMA((n, 7)), pltpu.SemaphoreType.DMA((n,))],
        compiler_params=_params(),
    )(*grads)


def _sum_slots(name, parts, half=None):
    _, l, r, c = parts.shape
    tr = _pick(r, (256, 176, 128, 64, 32, 8))

    def body(*refs):
        p_ref, o_ref = refs[-2:]
        acc = p_ref[0].astype(F32)
        for i in range(1, 8):
            acc = acc + p_ref[i].astype(F32)
        o_ref[...] = acc

    if half is None:
        return pl.pallas_call(
            body, name=name, grid=(l, r // tr), in_specs=[pl.BlockSpec((8, 1, tr, c), lambda li, i: (0, li, i, 0))],
            out_specs=pl.BlockSpec((1, tr, c), lambda li, i: (li, i, 0)), out_shape=jax.ShapeDtypeStruct((l, r, c), F32),
            compiler_params=_params(dimension_semantics=("arbitrary", "arbitrary")),
        )(parts)
    grid_spec = pltpu.PrefetchScalarGridSpec(
        num_scalar_prefetch=1, grid=(l, r // tr),
        in_specs=[pl.BlockSpec((8, 1, tr, c), lambda li, i, h: (0, li, i, 0))],
        out_specs=pl.BlockSpec((1, tr, c), lambda li, i, h: (li, h[0] * (r // tr) + i, 0)))
    return pl.pallas_call(
        body, name=name, grid_spec=grid_spec, out_shape=jax.ShapeDtypeStruct((l, 2 * r, c), F32),
        compiler_params=_params(dimension_semantics=("arbitrary", "arbitrary")),
    )(jnp.reshape(half, (1,)).astype(jnp.int32), parts)


def _swap_halves(blocks):
    n = len(blocks)

    def body(*refs):
        src, dst = refs[:n], refs[n:2 * n]
        send_sems, recv_sems = refs[2 * n:]
        x, y, c = _place()

        def half(ref, k, pc):
            r = blocks[k].shape[1] // 2
            return ref[k].at[:, pl.ds(pl.multiple_of(pc * r, 8), r), :]

        def copy(k, pc):
            return pltpu.make_async_remote_copy(src_ref=half(src, k, pc), dst_ref=half(dst, k, pc), send_sem=send_sems.at[k],
                                                recv_sem=recv_sems.at[k], device_id=(x, y, 1 - c), device_id_type=MESH)

        for k in range(n):
            copy(k, c).start()
        for k in range(n):
            copy(k, 1 - c).wait_recv()
        for k in range(n):
            copy(k, c).wait_send()

    any_spec = pl.BlockSpec(memory_space=pl.ANY)
    return pl.pallas_call(
        body, name="swap_halves", in_specs=[any_spec] * n, out_specs=[any_spec] * n,
        out_shape=[jax.ShapeDtypeStruct(b.shape, b.dtype) for b in blocks], input_output_aliases={k: k for k in range(n)},
        scratch_shapes=[pltpu.SemaphoreType.DMA((n,)), pltpu.SemaphoreType.DMA((n,))],
        compiler_params=_params(),
    )(*blocks)


def _adamw(name, w, g, m, v):
    shape = w.shape
    cols = shape[-1] if w.ndim > 1 and shape[-1] % LANES == 0 else w.size if w.size % LANES else LANES
    flat = lambda a: a.reshape(-1, cols)
    rows = w.size // cols
    tr = _pick(rows, [r for r in (512, 256, 128, 64, 32, 16, 8) if r * cols <= 256 * 1024]) if rows % 8 == 0 else rows

    def body(w_ref, g_ref, m_ref, v_ref, go_ref, d_ref, nm_ref, nv_ref):
        gg = g_ref[...]
        go_ref[...] = gg
        nm = ADAM_B1 * m_ref[...] + (1.0 - ADAM_B1) * gg
        nv = ADAM_B2 * v_ref[...] + (1.0 - ADAM_B2) * (gg * gg)
        m_hat = nm / (1.0 - ADAM_B1 ** ADAM_STEP)
        v_hat = nv / (1.0 - ADAM_B2 ** ADAM_STEP)
        d_ref[...] = -ADAM_LR * (m_hat / (jnp.sqrt(v_hat) + ADAM_EPS) + ADAM_WD * w_ref[...])
        nm_ref[...] = nm
        nv_ref[...] = nv

    spec = pl.BlockSpec((tr, cols), lambda i: (i, 0))
    out = pl.pallas_call(
        body, name=name, grid=(rows // tr,), in_specs=[spec] * 4, out_specs=[spec] * 4,
        out_shape=[jax.ShapeDtypeStruct((rows, cols), F32)] * 4,
        compiler_params=_params(dimension_semantics=("arbitrary",)),
    )(flat(w), flat(g), flat(m), flat(v))
    return tuple(o.reshape(shape) for o in out)


WEIGHTS = ["w_ada", "b_ada", "w_in", "w_sb_up", "ssm_a_re", "ssm_a_im", "ssm_log_dt", "ssm_b_re", "ssm_b_im", "ssm_c_re",
           "ssm_c_im", "ssm_d", "w_glu", "b_glu", "w_ssm_up", "w_out", "ln1_g", "ln1_b", "w_ffn_in", "w_ffn_out", "ln2_g",
           "ln2_b"]
COL_SPLIT = ["w_in", "w_sb_up", "w_ssm_up", "w_ffn_in"]
ROW_SPLIT = ["w_glu", "w_out", "w_ffn_out"]
SMALL = ["ssm_a_re", "ssm_a_im", "ssm_log_dt", "ssm_b_re", "ssm_b_im", "ssm_c_re", "ssm_c_im", "ssm_d", "b_glu", "ln1_g",
         "ln1_b", "ln2_g", "ln2_b"]
SLAB_COLS = 1024


def _cast_bf16(name, w):
    shape = w.shape
    flat = w.reshape(-1, shape[-1])
    rows, cols = flat.shape
    tr = _pick(rows, (512, 256, 128, 64, 8))

    def body(w_ref, o_ref):
        o_ref[...] = w_ref[...].astype(BF16)

    spec = pl.BlockSpec((tr, cols), lambda i: (i, 0))
    return pl.pallas_call(body, name=name, grid=(rows // tr,), in_specs=[spec], out_specs=spec,
                          out_shape=jax.ShapeDtypeStruct((rows, cols), BF16),
                          compiler_params=_params(dimension_semantics=("arbitrary",)))(flat).reshape(shape)


def _silu_rows(name, c):
    def body(c_ref, o_ref):
        v = c_ref[...]
        o_ref[...] = v * jax.nn.sigmoid(v)

    return pl.pallas_call(body, name=name, out_shape=jax.ShapeDtypeStruct(c.shape, F32), compiler_params=_params())(c)


def _pad_rows(v, mult=8):
    flat = v.reshape(-1)
    per = mult * SLAB_COLS
    total = -(-flat.size // per) * per
    return jnp.pad(flat, (0, total - flat.size)).reshape(-1, SLAB_COLS)


def kernel(x, c, w_ada, b_ada, w_in, w_sb_up, ssm_a_re, ssm_a_im, ssm_log_dt, ssm_b_re, ssm_b_im, ssm_c_re, ssm_c_im, ssm_d, w_glu, b_glu, w_ssm_up, w_out, ln1_g, ln1_b, w_ffn_in, w_ffn_out, ln2_g, ln2_b, loss_target, m_w_ada, m_b_ada, m_w_in, m_w_sb_up, m_ssm_a_re, m_ssm_a_im, m_ssm_log_dt, m_ssm_b_re, m_ssm_b_im, m_ssm_c_re, m_ssm_c_im, m_ssm_d, m_w_glu, m_b_glu, m_w_ssm_up, m_w_out, m_ln1_g, m_ln1_b, m_w_ffn_in, m_w_ffn_out, m_ln2_g, m_ln2_b, v_w_ada, v_b_ada, v_w_in, v_w_sb_up, v_ssm_a_re, v_ssm_a_im, v_ssm_log_dt, v_ssm_b_re, v_ssm_b_im, v_ssm_c_re, v_ssm_c_im, v_ssm_d, v_w_glu, v_b_glu, v_w_ssm_up, v_w_out, v_ln1_g, v_ln1_b, v_w_ffn_in, v_w_ffn_out, v_ln2_g, v_ln2_b):
    args = dict(locals())
    w = {n: args[n] for n in WEIGHTS}
    mom = {n: args["m_" + n] for n in WEIGHTS}
    var = {n: args["v_" + n] for n in WEIGHTS}
    depth, d = w_ada.shape[0], x.shape[-1]
    xi, yi, ci = _place()
    me, chip = 4 * xi + 2 * yi + ci, 2 * xi + yi
    ada_cols = w_ada.shape[2]

    big = COL_SPLIT + ROW_SPLIT
    by_rows = [n in ROW_SPLIT for n in big]
    full = dict(zip(big, _gather_weights([_cast_bf16(f"cast_{n}", w[n]) for n in big], by_rows)))

    c_all = _all_gather8("gather_c", jnp.pad(c, ((0, 7), (0, 0))))[::8]
    c_act = _silu_rows("silu_c", c_all)
    b_cols = lax.dynamic_slice_in_dim(b_ada, chip * ada_cols, ada_cols, axis=1)
    mod_part = jnp.concatenate([_small_mm(f"mod_{l}", c_act, w_ada[l], "nn") + b_cols[l][None] for l in range(depth)], axis=0)
    mod_all = _all_gather8("gather_mod", mod_part).reshape(4, 2, depth, 8, ada_cols)[:, 0]
    mod_mine = lax.dynamic_index_in_dim(mod_all, me, axis=2, keepdims=False)
    mod = mod_mine.transpose(1, 0, 2).reshape(depth, 6, d)

    layer_w = [{**{n: full[n][l] for n in big}, **{n: w[n][l] for n in SMALL}} for l in range(depth)]
    h, saved = x[0], []
    for l in range(depth):
        h, sv = _layer_fwd(h, mod[l], layer_w[l], str(l))
        saved.append(sv)
    dh, loss_cols = _loss_head(h, loss_target[0])
    loss = lax.psum(jnp.sum(loss_cols), ("x", "y", "c"))
    dmods, lgrads, stacked = [None] * depth, [None] * depth, {}
    for l in reversed(range(depth)):
        dh, dmods[l], lgrads[l] = _layer_bwd(dh, mod[l], layer_w[l], saved[l], l, depth, stacked)
        stacked = {n: lgrads[l][n] for n in big}
    grad_x = dh[None]

    parts = _scatter_grads([stacked[n] for n in big], by_rows)
    halves = [_sum_slots(f"sum_{n}", p, half=ci) for n, p in zip(big, parts)]
    grad = dict(zip(big, _swap_halves(halves)))

    pieces = [jnp.stack(dmods)] + [jnp.stack([lgrads[l][n] for l in range(depth)]) for n in SMALL]
    slab = jnp.concatenate([_pad_rows(p) for p in pieces], axis=0)
    slabs = _all_gather8("gather_small", slab).reshape(8, 1, *slab.shape)
    total = _sum_slots("sum_small", slabs)[0]
    row = _pad_rows(pieces[0]).shape[0]
    for n, p in zip(SMALL, pieces[1:]):
        rows = _pad_rows(p).shape[0]
        grad[n] = total[row:row + rows].reshape(-1)[:p.size].reshape(p.shape)
        row += rows
    dmod_rows = _pad_rows(pieces[0]).shape[0]
    dmod_all = slabs[:, 0, :dmod_rows].reshape(8, -1)[:, :depth * 6 * d].reshape(8, depth, 4, ada_cols)
    dmod_cols = lax.dynamic_index_in_dim(dmod_all, chip, axis=2, keepdims=False)
    grad["w_ada"] = jnp.stack([_small_mm(f"dw_ada_{l}", c_act, dmod_cols[:, l], "tn") for l in range(depth)])
    dmod_sum = _sum_slots("sum_dmod", slabs[:, :, :dmod_rows])[0]
    grad["b_ada"] = dmod_sum.reshape(-1)[:depth * 6 * d].reshape(depth, 6 * d)

    delta, new_m, new_v = {}, {}, {}
    for n in WEIGHTS:
        grad[n], delta[n], new_m[n], new_v[n] = _adamw(f"adamw_{n}", w[n], grad[n], mom[n], var[n])
    return (loss, grad_x, *[grad[n] for n in WEIGHTS], *[delta[n] for n in WEIGHTS], *[new_m[n] for n in WEIGHTS],
            *[new_v[n] for n in WEIGHTS])
```

```python
import functools
import math

import jax
import jax.numpy as jnp
from jax import lax
from jax.experimental import pallas as pl
from jax.experimental.pallas import tpu as pltpu

F32 = jnp.float32
BF16 = jnp.bfloat16
MESH = pl.DeviceIdType.MESH

LANES = 128
HEAD_DIM = 64
ATT_TILE = 256
SSM_GROUPS, SSM_STATE, SSM_GROUP = 32, 64, 16
N_STATE = SSM_GROUPS * SSM_STATE
SSM_BLOCKS = SSM_GROUPS * SSM_GROUP // LANES
U_OFFSET = 3 * 512
LN_EPS = 1e-5
DEPTH = 2
ALPHA = (2 * DEPTH) ** 0.25
ADAM_LR, ADAM_B1, ADAM_B2, ADAM_EPS, ADAM_WD, ADAM_STEP = 0.001, 0.9, 0.999, 1e-08, 0.01, 10
VMEM_LIMIT = 56 * 1024 * 1024
GELU_K = math.sqrt(2.0 / math.pi)
GELU_C = 0.044715


def _params(**kw):
    return pltpu.CompilerParams(vmem_limit_bytes=VMEM_LIMIT, **kw)


def _pick(n, prefs):
    for p in prefs:
        if n % p == 0:
            return p
    return n


def _rowwise(name, fn, rows, vecs, outs, sums=(), tm=None):
    s = rows[0][0].shape[0]
    tm = tm or _pick(s, (256, 128, 64, 8))
    nin, no, ns = len(rows) + len(vecs), len(outs), len(sums)

    def body(*refs):
        res = fn(*[r[...] for r in refs[:nin]])
        res = res if isinstance(res, tuple) else (res,)
        for r, v in zip(refs[nin:nin + no], res[:no]):
            r[...] = v.astype(r.dtype)
        if ns:
            @pl.when(pl.program_id(0) == 0)
            def _():
                for r in refs[nin + no:]:
                    r[...] = jnp.zeros_like(r)
            for r, v in zip(refs[nin + no:], res[no:]):
                r[...] += v

    in_specs = [pl.BlockSpec((tm, w), lambda i, cb=cb: (i, cb)) for _, cb, w in rows]
    in_specs += [pl.BlockSpec(v.shape, lambda i: (0, 0)) for v in vecs]
    out_specs = [pl.BlockSpec((tm, w), lambda i: (i, 0)) for w, _ in outs]
    out_specs += [pl.BlockSpec((1, w), lambda i: (0, 0)) for w in sums]
    out_shape = [jax.ShapeDtypeStruct((s, w), dt) for w, dt in outs]
    out_shape += [jax.ShapeDtypeStruct((1, w), F32) for w in sums]
    res = pl.pallas_call(
        body, name=name, grid=(s // tm,), in_specs=in_specs, out_specs=out_specs, out_shape=out_shape,
        compiler_params=_params(dimension_semantics=("arbitrary",)),
    )(*[a for a, _, _ in rows], *vecs)
    return res[0] if len(res) == 1 else tuple(res)


MM_TILES = (1408, 1024, 512, 256, 128)


def _mm(name, a, b, mode, out_dtype=F32, into=None):
    if mode == "nn":
        m, k, n = a.shape[0], a.shape[1], b.shape[1]
    elif mode == "nt":
        m, k, n = a.shape[0], a.shape[1], b.shape[0]
    else:
        k, m, n = a.shape[0], a.shape[1], b.shape[1]
    tm = _pick(m, MM_TILES if mode == "tn" else MM_TILES[1:])
    tn = _pick(n, MM_TILES)
    tk = _pick(k, MM_TILES)
    nk = k // tk
    dims = {"nn": ((1,), (0,)), "nt": ((1,), (1,)), "tn": ((0,), (0,))}[mode]

    def body(a_ref, b_ref, *rest):
        o_ref = rest[-2] if nk > 1 else rest[-1]
        prod = lax.dot_general(a_ref[...].astype(BF16), b_ref[...].astype(BF16), (dims, ((), ())),
                               preferred_element_type=F32)
        if nk == 1:
            o_ref[...] = prod.astype(o_ref.dtype)
            return
        acc_ref = rest[-1]
        kk = pl.program_id(2)

        @pl.when(kk == 0)
        def _():
            acc_ref[...] = prod

        @pl.when(kk > 0)
        def _():
            acc_ref[...] += prod

        @pl.when(kk == nk - 1)
        def _():
            o_ref[...] = acc_ref[...].astype(o_ref.dtype)

    if mode == "tn":
        a_spec = pl.BlockSpec((tk, tm), lambda i, j, kk: (kk, i))
    else:
        a_spec = pl.BlockSpec((tm, tk), lambda i, j, kk: (i, kk))
    if mode == "nt":
        b_spec = pl.BlockSpec((tn, tk), lambda i, j, kk: (j, kk))
    else:
        b_spec = pl.BlockSpec((tk, tn), lambda i, j, kk: (kk, j))
    in_specs, operands, aliases = [a_spec, b_spec], [a, b], {}
    if into is None:
        out_spec = pl.BlockSpec((tm, tn), lambda i, j, kk: (i, j))
        out_shape = jax.ShapeDtypeStruct((m, n), out_dtype)
    else:
        buf, slab, count = into
        out_spec = pl.BlockSpec((None, tm, tn), lambda i, j, kk: (slab, i, j))
        out_shape = jax.ShapeDtypeStruct((count, m, n), out_dtype)
        if buf is not None:
            in_specs.append(pl.BlockSpec(memory_space=pl.ANY))
            operands.append(buf)
            aliases = {2: 0}
    return pl.pallas_call(
        body, name=name, grid=(m // tm, n // tn, nk), in_specs=in_specs, out_specs=out_spec, out_shape=out_shape,
        scratch_shapes=[pltpu.VMEM((tm, tn), F32)] if nk > 1 else [], input_output_aliases=aliases,
        compiler_params=_params(dimension_semantics=("arbitrary", "arbitrary", "arbitrary")),
    )(*operands)


def _small_mm(name, a, b, mode):
    dims = {"nn": ((1,), (0,)), "tn": ((0,), (0,))}[mode]
    m = a.shape[0] if mode == "nn" else a.shape[1]

    def body(a_ref, b_ref, o_ref):
        o_ref[...] = lax.dot_general(a_ref[...], b_ref[...], (dims, ((), ())), precision=lax.Precision.HIGHEST,
                                     preferred_element_type=F32)

    return pl.pallas_call(body, name=name, out_shape=jax.ShapeDtypeStruct((m, b.shape[1]), F32),
                          compiler_params=_params())(a, b)


def _norm(x):
    mu = jnp.mean(x, axis=-1, keepdims=True)
    xc = x - mu
    rstd = lax.rsqrt(jnp.mean(xc * xc, axis=-1, keepdims=True) + LN_EPS)
    return xc * rstd, rstd


def _norm_bwd(dn, n, rstd):
    return rstd * (dn - jnp.mean(dn, axis=-1, keepdims=True) - n * jnp.mean(dn * n, axis=-1, keepdims=True))


def _colsum(v):
    return jnp.sum(v, axis=0, keepdims=True)


def _gelu(x):
    return 0.5 * x * (1.0 + jnp.tanh(GELU_K * (x + GELU_C * x * x * x)))


def _gelu_grad(x):
    t = jnp.tanh(GELU_K * (x + GELU_C * x * x * x))
    return 0.5 * (1.0 + t) + 0.5 * x * (1.0 - t * t) * GELU_K * (1.0 + 3.0 * GELU_C * x * x)


def _log_sigmoid_parts(z):
    lb = jnp.minimum(z, 0.0) - jnp.log(1.0 + jnp.exp(-jnp.abs(z)))
    return lb, lb - z


def _qkv_prep(proj, t):
    s = proj.shape[0]
    nb, nhp = s // t, 512 // LANES

    def body(q_ref, k_ref, v_ref, qs_ref, kb_ref, vb_ref, kt_ref, vt_ref):
        qs_ref[...] = (q_ref[...] * (1.0 / math.sqrt(HEAD_DIM))).astype(BF16)
        k, v = k_ref[...], v_ref[...]
        kb_ref[...] = k.astype(BF16)
        vb_ref[...] = v.astype(BF16)
        for hp in range(nhp):
            kt_ref[hp, 0] = k[:, hp * LANES:(hp + 1) * LANES].T.astype(BF16)
            vt_ref[hp, 0] = v[:, hp * LANES:(hp + 1) * LANES].T.astype(BF16)

    col = lambda cb: pl.BlockSpec((t, 512), lambda i, cb=cb: (i, cb))
    row_out = pl.BlockSpec((t, 512), lambda i: (i, 0))
    t_out = pl.BlockSpec((nhp, 1, LANES, t), lambda i: (0, i, 0, 0))
    return pl.pallas_call(
        body, name="qkv_prep", grid=(nb,), in_specs=[col(0), col(1), col(2)],
        out_specs=[row_out, row_out, row_out, t_out, t_out],
        out_shape=[jax.ShapeDtypeStruct((s, 512), BF16)] * 3 + [jax.ShapeDtypeStruct((nhp, nb, LANES, t), BF16)] * 2,
        compiler_params=_params(dimension_semantics=("arbitrary",)),
    )(proj, proj, proj)


def _tile_masks(t):
    row = lax.broadcasted_iota(jnp.int32, (t, t), 0)
    col = lax.broadcasted_iota(jnp.int32, (t, t), 1)
    return row, col


DEAD_LOG_WEIGHT = -110.0


def _walk_down(i, tiles, state, alive):
    st = lax.cond(i == 0, lambda s_: tiles([i], s_, [True]), lambda s_: tiles([i, i - 1], s_, [True, False]), state)
    n = jnp.maximum(i - 1, 0)

    def pair(c):
        return c[0] + 1, tiles([i - 2 - 2 * c[0], i - 3 - 2 * c[0]], c[1], [False, False])

    p, st = lax.while_loop(lambda c: (c[0] < n // 2) & alive(c[1]), pair, (jnp.int32(0), st))
    return lax.cond((n % 2 == 1) & (p == n // 2) & alive(st), lambda s_: tiles([0], s_, [False]), lambda s_: s_, st)


def _walk_up(i, first, tiles, state):
    n = jnp.maximum(i - 1 - first, 0)
    st = lax.fori_loop(0, n // 2, lambda p, s_: tiles([first + 2 * p, first + 2 * p + 1], s_, [False, False]), state)
    st = lax.cond(n % 2 == 1, lambda s_: tiles([i - 2], s_, [False]), lambda s_: s_, st)
    return lax.cond(i == 0, lambda s_: tiles([i], s_, [True]), lambda s_: tiles([i - 1, i], s_, [False, True]), st)


def _nt(a, b):
    return lax.dot_general(a, b, (((1,), (1,)), ((), ())), preferred_element_type=F32)


def _nn(a, b):
    return jnp.dot(a, b, preferred_element_type=F32)


def _attn_fwd(qs, k, vt3, t):
    s = qs.shape[0]
    nb, nhp = s // t, qs.shape[1] // LANES

    def body(q_ref, k_ref, vt_ref, o_ref, car_ref):
        i = pl.program_id(1)
        q2 = q_ref[...]
        lane_q = lax.broadcasted_iota(jnp.int32, q2.shape, 1)
        row, col = _tile_masks(t)
        later = (col > row).astype(BF16)
        valid = row < col
        orow = lax.broadcasted_iota(jnp.int32, (LANES, t), 0)
        car_ref[...] = jnp.full(car_ref.shape, 2.0 * DEAD_LOG_WEIGHT, F32)
        qh = [jnp.where((lane_q < HEAD_DIM) == (hh == 0), q2, jnp.zeros_like(q2)) for hh in range(2)]

        def tiles(js, state, diagonal):
            chains = [(n, hh) for n in range(len(js)) for hh in range(2)]
            kb = [k_ref[pl.ds(pl.multiple_of(j * t, t), t), :] for j in js]
            z = {ch: _nt(kb[ch[0]], qh[ch[1]]) for ch in chains}
            lb, aft, csum = {}, {}, {}
            for ch in chains:
                lb[ch], l1m = _log_sigmoid_parts(z[ch])
                if diagonal[ch[0]]:
                    l1m = jnp.where(valid, l1m, 0.0)
                aft[ch] = _nn(later, l1m.astype(BF16))
                csum[ch] = _colsum(l1m)
            state = list(state)
            for ch in chains:
                n, hh = ch
                c_after, acc = state[hh]
                w = jnp.exp(lb[ch] + aft[ch] + c_after)
                if diagonal[ch[0]]:
                    w = jnp.where(valid, w, 0.0)
                car_ref[hh, pl.ds(js[n], 1), :] = c_after
                state[hh] = (c_after + csum[ch], acc + _nn(vt_ref[0, js[n]], w.astype(BF16)))
            return tuple(state)

        def alive(state):
            return jnp.max(jnp.maximum(state[0][0], state[1][0])) >= DEAD_LOG_WEIGHT

        zero = (jnp.zeros((1, t), F32), jnp.zeros((LANES, t), F32))
        (_, acc0), (_, acc1) = _walk_down(i, tiles, (zero, zero), alive)
        o_ref[...] = jnp.where(orow < HEAD_DIM, acc0, acc1).T.astype(o_ref.dtype)

    return pl.pallas_call(
        body, name="attn_fwd", grid=(nhp, nb),
        in_specs=[pl.BlockSpec((t, LANES), lambda hp, i: (i, hp)),
                  pl.BlockSpec((s, LANES), lambda hp, i: (0, hp)),
                  pl.BlockSpec((1, nb, LANES, t), lambda hp, i: (hp, 0, 0, 0))],
        out_specs=[pl.BlockSpec((t, LANES), lambda hp, i: (i, hp)),
                   pl.BlockSpec((2, nb, t), lambda hp, i: (hp, 0, i))],
        out_shape=[jax.ShapeDtypeStruct((s, nhp * LANES), BF16), jax.ShapeDtypeStruct((2 * nhp, nb, s), F32)],
        compiler_params=_params(dimension_semantics=("arbitrary", "arbitrary")),
    )(qs, k, vt3)


def _attn_bwd(qs, do, k, v, kt3, car, t):
    s = qs.shape[0]
    nb, nhp = s // t, qs.shape[1] // LANES

    def body(q_ref, do_ref, k_ref, v_ref, kt_ref, car_ref, dq_ref, dk_ref, dv_ref):
        i = pl.program_id(1)

        @pl.when(i == 0)
        def _():
            dk_ref[...] = jnp.zeros_like(dk_ref)
            dv_ref[...] = jnp.zeros_like(dv_ref)

        q2, do2 = q_ref[...], do_ref[...]
        lane_q = lax.broadcasted_iota(jnp.int32, q2.shape, 1)
        row, col = _tile_masks(t)
        later = (col > row).astype(BF16)
        earlier = (col < row).astype(BF16)
        valid = row < col
        orow = lax.broadcasted_iota(jnp.int32, (LANES, t), 0)
        head = [(lane_q < HEAD_DIM) == (hh == 0) for hh in range(2)]
        qh = [jnp.where(hm, q2, jnp.zeros_like(q2)) for hm in head]
        doh = [jnp.where(hm, do2, jnp.zeros_like(do2)) for hm in head]

        def tiles(js, state, diagonal):
            chains = [(n, hh) for n in range(len(js)) for hh in range(2)]
            rows = [pl.ds(pl.multiple_of(j * t, t), t) for j in js]
            kb = [k_ref[r, :] for r in rows]
            vb = [v_ref[r, :] for r in rows]
            z = {ch: _nt(kb[ch[0]], qh[ch[1]]) for ch in chains}
            dw = {ch: _nt(vb[ch[0]], doh[ch[1]]) for ch in chains}
            lb, beta, aft = {}, {}, {}
            for ch in chains:
                lb[ch], l1m = _log_sigmoid_parts(z[ch])
                beta[ch] = jnp.exp(lb[ch])
                if diagonal[ch[0]]:
                    l1m = jnp.where(valid, l1m, 0.0)
                aft[ch] = _nn(later, l1m.astype(BF16))
            w, g, gsum, g_in = {}, {}, {}, {}
            for ch in chains:
                n, hh = ch
                w[ch] = jnp.exp(lb[ch] + aft[ch] + car_ref[hh, pl.ds(js[n], 1), :])
                if diagonal[ch[0]]:
                    w[ch] = jnp.where(valid, w[ch], 0.0)
                g[ch] = dw[ch] * w[ch]
                g_in[ch] = _nn(earlier, g[ch].astype(BF16))
                gsum[ch] = _colsum(g[ch])
            state = list(state)
            dk_t, dv_t = [None] * len(js), [None] * len(js)
            for ch in chains:
                n, hh = ch
                c_g, dqt = state[hh]
                dz = g[ch] - beta[ch] * (g[ch] + g_in[ch] + c_g)
                if diagonal[ch[0]]:
                    dz = jnp.where(valid, dz, 0.0)
                dzb, wb = dz.astype(BF16), w[ch].astype(BF16)
                dk_h, dv_h = _nn(dzb, qh[hh]), _nn(wb, doh[hh])
                dk_t[n] = dk_h if dk_t[n] is None else dk_t[n] + dk_h
                dv_t[n] = dv_h if dv_t[n] is None else dv_t[n] + dv_h
                state[hh] = (c_g + gsum[ch], dqt + _nn(kt_ref[0, js[n]], dzb))
            for n in range(len(js)):
                dk_ref[rows[n], :] += dk_t[n]
                dv_ref[rows[n], :] += dv_t[n]
            return tuple(state)

        reach = jnp.max(jnp.max(car_ref[...], axis=2, keepdims=True), axis=0)
        dead = (reach < DEAD_LOG_WEIGHT) & (lax.broadcasted_iota(jnp.int32, reach.shape, 0) < i)
        first = jnp.sum(jnp.where(dead, 1.0, 0.0)).astype(jnp.int32)
        zero = (jnp.zeros((1, t), F32), jnp.zeros((LANES, t), F32))
        (_, dq0), (_, dq1) = _walk_up(i, first, tiles, (zero, zero))
        dq_ref[...] = jnp.where(orow < HEAD_DIM, dq0, dq1).T

    tile_spec = pl.BlockSpec((t, LANES), lambda hp, i: (i, hp))
    whole = pl.BlockSpec((s, LANES), lambda hp, i: (0, hp))
    return pl.pallas_call(
        body, name="attn_bwd", grid=(nhp, nb),
        in_specs=[tile_spec, tile_spec, whole, whole,
                  pl.BlockSpec((1, nb, LANES, t), lambda hp, i: (hp, 0, 0, 0)),
                  pl.BlockSpec((2, nb, t), lambda hp, i: (hp, 0, i))],
        out_specs=[tile_spec, whole, whole],
        out_shape=[jax.ShapeDtypeStruct((s, nhp * LANES), F32)] * 3,
        compiler_params=_params(dimension_semantics=("arbitrary", "arbitrary")),
    )(qs, do, k, v, kt3, car)


SCAN_LANES = 1024
SCAN_ROWS = 8


def _scan_chunks(v):
    n = v.shape[1] // (2 * LANES)
    return [(v[:, c * 2 * LANES:c * 2 * LANES + LANES], v[:, c * 2 * LANES + LANES:(c + 1) * 2 * LANES]) for c in range(n)]


def _scan_tables(lr, li, reverse):
    if reverse:
        li = -li
    row = lax.broadcasted_iota(jnp.int32, (SCAN_ROWS, LANES), 0)
    powers = [(lr, li)]
    for _ in range(SCAN_ROWS - 1):
        pr, pi = powers[-1]
        powers.append((pr * lr - pi * li, pr * li + pi * lr))
    levels = []
    for d in (1, 2, 4):
        keep = (row < SCAN_ROWS - d) if reverse else (row >= d)
        levels.append((SCAN_ROWS - d if reverse else d,
                       (jnp.where(keep, powers[d - 1][0], 0.0), jnp.where(keep, powers[d - 1][1], 0.0))))
    pr = pi = jnp.zeros((SCAN_ROWS, LANES), F32)
    for r in range(SCAN_ROWS):
        steps = SCAN_ROWS - r if reverse else r + 1
        pr = jnp.where(row == r, powers[steps - 1][0], pr)
        pi = jnp.where(row == r, powers[steps - 1][1], pi)
    return levels, (pr, pi)


def _s5_fwd(proj, u_off, bmat, cmat, lam):
    s, w = proj.shape[0], bmat.shape[1]
    tt = _pick(s, (512, 256, 128, 8))
    nt = s // tt
    cin = bmat.shape[0] // SSM_BLOCKS

    def body(u_ref, b_ref, c_ref, lam_ref, h_ref, y_ref, x_ref, st_ref):
        @pl.when(pl.program_id(1) == 0)
        def _():
            st_ref[...] = jnp.zeros_like(st_ref)

        tables = [_scan_tables(lr, li, reverse=False) for lr, li in _scan_chunks(lam_ref[...])]
        x_ref[...] = _nn(u_ref[...].astype(BF16), b_ref[...])

        def tile(it, last):
            r0 = pl.multiple_of(it * SCAN_ROWS, SCAN_ROWS)
            last, parts = list(last), []
            for c, (xr, xi) in enumerate(_scan_chunks(x_ref[pl.ds(r0, SCAN_ROWS), :])):
                levels, (pr, pi) = tables[c]
                for d, (ar, ai) in levels:
                    sr, si = pltpu.roll(xr, d, 0), pltpu.roll(xi, d, 0)
                    xr, xi = xr + ar * sr - ai * si, xi + ar * si + ai * sr
                br, bi = last[2 * c], last[2 * c + 1]
                hr = xr + pr * br - pi * bi
                hi = xi + pr * bi + pi * br
                last[2 * c], last[2 * c + 1] = hr[SCAN_ROWS - 1:], hi[SCAN_ROWS - 1:]
                parts += [hr, hi]
            h_ref[pl.ds(r0, SCAN_ROWS), :] = jnp.concatenate(parts, axis=1)
            return tuple(last)

        st = st_ref[0:1, :]
        init = tuple(st[:, c * LANES:(c + 1) * LANES] for c in range(SCAN_LANES // LANES))
        fin = lax.fori_loop(0, tt // SCAN_ROWS, tile, init)
        st_ref[0:1, :] = jnp.concatenate(fin, axis=1)
        y_ref[...] = _nn(h_ref[...].astype(BF16), c_ref[...])

    return pl.pallas_call(
        body, name="s5_fwd", grid=(SSM_BLOCKS, nt),
        in_specs=[pl.BlockSpec((tt, cin), lambda kb, i: (i, u_off // cin + kb)),
                  pl.BlockSpec((cin, SCAN_LANES), lambda kb, i: (kb, kb)),
                  pl.BlockSpec((SCAN_LANES, cin), lambda kb, i: (kb, kb)),
                  pl.BlockSpec((1, SCAN_LANES), lambda kb, i: (0, kb))],
        out_specs=[pl.BlockSpec((tt, SCAN_LANES), lambda kb, i: (i, kb)), pl.BlockSpec((tt, cin), lambda kb, i: (i, kb))],
        out_shape=[jax.ShapeDtypeStruct((s, w), F32), jax.ShapeDtypeStruct((s, bmat.shape[0]), F32)],
        scratch_shapes=[pltpu.VMEM((tt, SCAN_LANES), F32), pltpu.VMEM((SCAN_ROWS, SCAN_LANES), F32)],
        compiler_params=_params(dimension_semantics=("arbitrary", "arbitrary")),
    )(proj, bmat, cmat, lam)


def _s5_bwd(dy, h, proj, u_off, bmat, cmat, lam):
    s, w = h.shape
    tt = _pick(s, (512, 256, 128, 8))
    nt = s // tt
    cin = bmat.shape[0] // SSM_BLOCKS

    def body(dy_ref, h_ref, u_ref, b_ref, c_ref, lam_ref, du_ref, dlam_ref, db_ref, dc_ref, e_ref, a_ref, st_ref):
        @pl.when(pl.program_id(1) == 0)
        def _():
            st_ref[...] = jnp.zeros_like(st_ref)
            dlam_ref[...] = jnp.zeros_like(dlam_ref)
            db_ref[...] = jnp.zeros_like(db_ref)
            dc_ref[...] = jnp.zeros_like(dc_ref)

        tables = [_scan_tables(lr, li, reverse=True) for lr, li in _scan_chunks(lam_ref[...])]
        nch = len(tables)
        row = lax.broadcasted_iota(jnp.int32, (SCAN_ROWS, LANES), 0)
        e_ref[...] = _nt(dy_ref[...], c_ref[...])

        def tile(it, carry):
            r0 = pl.multiple_of((tt // SCAN_ROWS - 1 - it) * SCAN_ROWS, SCAN_ROWS)
            e_c = _scan_chunks(e_ref[pl.ds(r0, SCAN_ROWS), :])
            h_c = _scan_chunks(h_ref[pl.ds(r0, SCAN_ROWS), :])
            carry, parts = list(carry), []
            for c in range(nch):
                (yr, yi), (hr, hi) = e_c[c], h_c[c]
                levels, (pr, pi) = tables[c]
                for shift, (lr, li) in levels:
                    sr, si = pltpu.roll(yr, shift, 0), pltpu.roll(yi, shift, 0)
                    yr, yi = yr + lr * sr - li * si, yi + lr * si + li * sr
                nr, ni, dr, di = carry[4 * c:4 * c + 4]
                ar = yr + pr * nr - pi * ni
                ai = yi + pr * ni + pi * nr
                nxr = jnp.where(row == SCAN_ROWS - 1, nr, pltpu.roll(ar, SCAN_ROWS - 1, 0))
                nxi = jnp.where(row == SCAN_ROWS - 1, ni, pltpu.roll(ai, SCAN_ROWS - 1, 0))
                carry[4 * c:4 * c + 4] = [ar[0:1], ai[0:1], dr + nxr * hr + nxi * hi, di + nxi * hr - nxr * hi]
                parts += [ar, ai]
            a_ref[pl.ds(r0, SCAN_ROWS), :] = jnp.concatenate(parts, axis=1)
            return tuple(carry)

        st, dl = st_ref[0:1, :], dlam_ref[...]
        init = []
        for c in range(nch):
            lo = c * 2 * LANES
            init += [st[:, lo:lo + LANES], st[:, lo + LANES:lo + 2 * LANES],
                     dl[:, lo:lo + LANES], dl[:, lo + LANES:lo + 2 * LANES]]
        fin = lax.fori_loop(0, tt // SCAN_ROWS, tile, tuple(init))
        st_ref[0:1, :] = jnp.concatenate([fin[4 * c + q] for c in range(nch) for q in (0, 1)], axis=1)
        dlam_ref[...] = jnp.concatenate([fin[4 * c + q] for c in range(nch) for q in (2, 3)], axis=1)

        @pl.when(pl.program_id(1) == nt - 1)
        def _():
            dlam_ref[0:1, :] = jnp.sum(dlam_ref[...], axis=0, keepdims=True)

        adj = a_ref[...].astype(BF16)
        du_ref[...] = _nt(adj, b_ref[...])
        rows_first = (((0,), (0,)), ((), ()))
        db_ref[...] += lax.dot_general(u_ref[...].astype(BF16), adj, rows_first, preferred_element_type=F32)
        dc_ref[...] += lax.dot_general(h_ref[...].astype(BF16), dy_ref[...], rows_first, preferred_element_type=F32)

    def rev(width, col):
        return pl.BlockSpec((tt, width), lambda kb, i: (nt - 1 - i, col(kb)))

    return pl.pallas_call(
        body, name="s5_bwd", grid=(SSM_BLOCKS, nt),
        in_specs=[rev(cin, lambda kb: kb), rev(SCAN_LANES, lambda kb: kb), rev(cin, lambda kb: u_off // cin + kb),
                  pl.BlockSpec((cin, SCAN_LANES), lambda kb, i: (kb, kb)),
                  pl.BlockSpec((SCAN_LANES, cin), lambda kb, i: (kb, kb)),
                  pl.BlockSpec((1, SCAN_LANES), lambda kb, i: (0, kb))],
        out_specs=[rev(cin, lambda kb: kb), pl.BlockSpec((SCAN_ROWS, SCAN_LANES), lambda kb, i: (0, kb)),
                   pl.BlockSpec((cin, SCAN_LANES), lambda kb, i: (kb, 0)), pl.BlockSpec((SCAN_LANES, cin), lambda kb, i: (kb, 0))],
        out_shape=[jax.ShapeDtypeStruct((s, bmat.shape[0]), F32), jax.ShapeDtypeStruct((SCAN_ROWS, w), F32),
                   jax.ShapeDtypeStruct((bmat.shape[0], SCAN_LANES), F32), jax.ShapeDtypeStruct((w, cin), F32)],
        scratch_shapes=[pltpu.VMEM((tt, SCAN_LANES), F32), pltpu.VMEM((tt, SCAN_LANES), F32),
                        pltpu.VMEM((SCAN_ROWS, SCAN_LANES), F32)],
        compiler_params=_params(dimension_semantics=("arbitrary", "arbitrary")),
    )(dy, h, proj, bmat, cmat, lam)


def _ssm_params_fwd(a_re, a_im, log_dt, b_re, b_im):
    def body(ar_ref, ai_ref, ldt_ref, br_ref, bi_ref, lr_ref, li_ref, bbr_ref, bbi_ref):
        ar, ai, dt = ar_ref[...], ai_ref[...], jnp.exp(ldt_ref[...])
        mag = jnp.exp(ar * dt)
        lr, li = mag * jnp.cos(ai * dt), mag * jnp.sin(ai * dt)
        den = ar * ar + ai * ai
        cr = ((lr - 1.0) * ar + li * ai) / den
        ci = (li * ar - (lr - 1.0) * ai) / den
        br, bi = br_ref[...], bi_ref[...]
        lr_ref[...], li_ref[...] = lr, li
        bbr_ref[...] = cr * br - ci * bi
        bbi_ref[...] = cr * bi + ci * br

    n = a_re.shape[0]
    v1, v16 = jax.ShapeDtypeStruct((n, 1), F32), jax.ShapeDtypeStruct((n, SSM_GROUP), F32)
    return pl.pallas_call(body, name="ssm_params_fwd", out_shape=[v1, v1, v16, v16],
                          compiler_params=_params())(a_re, a_im, log_dt, b_re, b_im)


def _ssm_params_bwd(a_re, a_im, log_dt, b_re, b_im, g_lr, g_li, g_bbr, g_bbi):
    n = a_re.shape[0]

    def body(ar_ref, ai_ref, ldt_ref, br_ref, bi_ref, glr_ref, gli_ref, gbr_ref, gbi_ref,
             dar_ref, dai_ref, dldt_ref, dbr_ref, dbi_ref):
        ar, ai, dt = ar_ref[...], ai_ref[...], jnp.exp(ldt_ref[...])
        mag = jnp.exp(ar * dt)
        lr, li = mag * jnp.cos(ai * dt), mag * jnp.sin(ai * dt)
        den = ar * ar + ai * ai
        cr = ((lr - 1.0) * ar + li * ai) / den
        ci = (li * ar - (lr - 1.0) * ai) / den
        br, bi, gbr, gbi = br_ref[...], bi_ref[...], gbr_ref[...], gbi_ref[...]
        dbr_ref[...] = gbr * cr + gbi * ci
        dbi_ref[...] = gbi * cr - gbr * ci
        gcr = jnp.sum(gbr * br + gbi * bi, axis=1, keepdims=True)
        gci = jnp.sum(gbi * br - gbr * bi, axis=1, keepdims=True)
        ir, ii = ar / den, -ai / den
        glr = glr_ref[...] + gcr * ir + gci * ii
        gli = gli_ref[...] + gci * ir - gcr * ii
        qr, qi = cr * ir - ci * ii, cr * ii + ci * ir
        gar = -(gcr * qr + gci * qi)
        gai = -(gci * qr - gcr * qi)
        gxr = glr * lr + gli * li
        gxi = gli * lr - glr * li
        dar_ref[...] = gar + gxr * dt
        dai_ref[...] = gai + gxi * dt
        gdt = (gxr * ar + gxi * ai) * dt
        rowg = lax.broadcasted_iota(jnp.int32, (n, SSM_GROUPS), 0) // SSM_STATE
        colg = lax.broadcasted_iota(jnp.int32, (n, SSM_GROUPS), 1)
        dldt_ref[...] = jnp.sum(jnp.where(rowg == colg, gdt, 0.0), axis=0, keepdims=True)

    v1, v16 = jax.ShapeDtypeStruct((n, 1), F32), jax.ShapeDtypeStruct((n, SSM_GROUP), F32)
    return pl.pallas_call(body, name="ssm_params_bwd",
                          out_shape=[v1, v1, jax.ShapeDtypeStruct((1, SSM_GROUPS), F32), v16, v16],
                          compiler_params=_params())(a_re, a_im, log_dt, b_re, b_im, g_lr, g_li, g_bbr, g_bbi)


def _interleave(re, im, axis):
    shp = list(re.shape)
    new = shp[:axis] + [shp[axis] // LANES, LANES] + shp[axis + 1:]
    st = jnp.stack([re.reshape(new), im.reshape(new)], axis=axis + 1)
    return st.reshape(shp[:axis] + [2 * shp[axis]] + shp[axis + 1:])


def _deinterleave(v, axis):
    shp = list(v.shape)
    r = v.reshape(shp[:axis] + [shp[axis] // (2 * LANES), 2, LANES] + shp[axis + 1:])
    out = shp[:axis] + [shp[axis] // 2] + shp[axis + 1:]
    return (lax.index_in_dim(r, 0, axis + 1, keepdims=False).reshape(out),
            lax.index_in_dim(r, 1, axis + 1, keepdims=False).reshape(out))


def _b_matrix(bbr, bbi):
    eye = jnp.eye(SSM_GROUPS, dtype=F32)

    def blockdiag(v):
        x = v.reshape(SSM_GROUPS, SSM_STATE, SSM_GROUP).transpose(0, 2, 1)
        return (eye[:, None, :, None] * x[:, :, None, :]).reshape(SSM_GROUPS * SSM_GROUP, N_STATE)

    return _interleave(blockdiag(bbr), blockdiag(bbi), 1)


def _diag_blocks(v, rows, cols):
    per = SSM_GROUPS // SSM_BLOCKS
    return jnp.stack([v[g * rows:(g + 1) * rows, (g % per) * cols:(g % per + 1) * cols] for g in range(SSM_GROUPS)])


def _b_matrix_grad(d):
    def diag(v):
        return _diag_blocks(v, SSM_GROUP, SSM_STATE).transpose(0, 2, 1).reshape(N_STATE, SSM_GROUP)

    dr, di = _deinterleave(d, 1)
    return diag(dr), diag(di)


def _c_matrix(c_re, c_im):
    eye = jnp.eye(SSM_GROUPS, dtype=F32)

    def blockdiag(v):
        x = v.transpose(0, 2, 1)
        return (x[:, :, None, :] * eye[:, None, :, None]).reshape(N_STATE, SSM_GROUPS * SSM_GROUP)

    return _interleave(blockdiag(c_re), blockdiag(-c_im), 0)


def _c_matrix_grad(d):
    def diag(v):
        return _diag_blocks(v, SSM_STATE, SSM_GROUP).transpose(0, 2, 1)

    dr, di = _deinterleave(d, 0)
    return diag(dr), -diag(di)


def _row(v):
    return v.reshape(1, -1)


def _ssm_inputs(p):
    rows = lambda v: v.reshape(N_STATE, -1)
    ldt = jnp.repeat(p["ssm_log_dt"], SSM_STATE).reshape(N_STATE, 1)
    return rows(p["ssm_a_re"]), rows(p["ssm_a_im"]), ldt, rows(p["ssm_b_re"]), rows(p["ssm_b_im"])


def _layer_fwd(x, mod, p, tag):
    d = x.shape[1]
    sh_m, sc_m, g_m, sh_f, sc_f, g_f = [_row(mod[i]) for i in range(6)]
    nm = lambda s: f"{s}_{tag}"

    def lnmod(x, sc, sh):
        return _norm(x)[0] * (1.0 + sc) + sh

    h1 = _rowwise(nm("lnmod1"), lnmod, [(x, 0, d)], [sc_m, sh_m], [(d, BF16)])
    proj = _mm(nm("proj"), h1, p["w_in"], "nn")
    t = min(ATT_TILE, x.shape[0])
    qs, kb, vb, kt3, vt3 = _qkv_prep(proj, t)
    att, car = _attn_fwd(qs, kb, vt3, t)
    y_sb = _mm(nm("sb_up"), att, p["w_sb_up"], "nn")

    lam_r, lam_i, bbr, bbi = _ssm_params_fwd(*_ssm_inputs(p))
    lam = _interleave(lam_r.reshape(1, N_STATE), lam_i.reshape(1, N_STATE), 1)
    bmat = _b_matrix(bbr, bbi).astype(BF16)
    cmat = _c_matrix(p["ssm_c_re"], p["ssm_c_im"]).astype(BF16)
    hst, yc = _s5_fwd(proj, U_OFFSET, bmat, cmat, lam)

    def ssm_act(yc, u, dsk):
        y0 = yc + dsk * u
        return y0, _gelu(y0)

    y0, y1 = _rowwise(nm("ssm_act"), ssm_act, [(yc, 0, 512), (proj, 3, 512)], [_row(p["ssm_d"])], [(512, F32), (512, F32)])
    gl = _mm(nm("glu"), y1, p["w_glu"], "nn")
    y2 = _rowwise(nm("glu_act"), lambda y1, gl, b: y1 * jax.nn.sigmoid(gl + b), [(y1, 0, 512), (gl, 0, 512)],
                  [_row(p["b_glu"])], [(512, BF16)])
    y_ssm = _mm(nm("ssm_up"), y2, p["w_ssm_up"], "nn")

    def merge(gsb, gss, ysb, yss):
        return jax.nn.sigmoid(gsb) * ysb + jax.nn.sigmoid(gss) * yss

    merged = _rowwise(nm("merge"), merge, [(proj, 2, d), (proj, 3, d), (y_sb, 0, d), (y_ssm, 0, d)], [], [(d, BF16)])
    y = _mm(nm("out"), merged, p["w_out"], "nn")

    def resid_ln(x, y, g, lg, lb):
        return _norm(ALPHA * x + (1.0 + g) * y)[0] * lg + lb

    x1 = _rowwise(nm("ln1"), resid_ln, [(x, 0, d), (y, 0, d)], [g_m, _row(p["ln1_g"]), _row(p["ln1_b"])], [(d, F32)])
    h2 = _rowwise(nm("lnmod2"), lnmod, [(x1, 0, d)], [sc_f, sh_f], [(d, BF16)])
    f = _mm(nm("ffn_in"), h2, p["w_ffn_in"], "nn", out_dtype=BF16)
    fh = f.shape[1] // 2

    def swiglu(g, u):
        g = g.astype(F32)
        return g * jax.nn.sigmoid(g) * u.astype(F32)

    act = _rowwise(nm("swiglu"), swiglu, [(f, 0, fh), (f, 1, fh)], [], [(fh, BF16)])
    yf = _mm(nm("ffn_out"), act, p["w_ffn_out"], "nn")
    x2 = _rowwise(nm("ln2"), resid_ln, [(x1, 0, d), (yf, 0, d)], [g_f, _row(p["ln2_g"]), _row(p["ln2_b"])], [(d, F32)])
    saved = dict(x=x, h1=h1, proj=proj, qs=qs, kb=kb, vb=vb, kt3=kt3, car=car, att=att, y_sb=y_sb, lam=lam, bmat=bmat,
                 cmat=cmat, hst=hst, y0=y0, y1=y1, gl=gl, y2=y2, y_ssm=y_ssm, merged=merged, y=y, x1=x1, h2=h2, f=f,
                 act=act, yf=yf, t=t)
    return x2, saved


def _layer_bwd(dx2, mod, p, sv, layer, depth, stacked):
    d = dx2.shape[1]
    sh_m, sc_m, g_m, sh_f, sc_f, g_f = [_row(mod[i]) for i in range(6)]
    nm = lambda s: f"{s}_{layer}"
    grads = {}

    def weight_grad(n, a, b, **kw):
        grads[n] = _mm(nm("d" + n), a, b, "tn", out_dtype=BF16, into=(stacked.get(n), layer, depth), **kw)

    def resid_ln_bwd(x, y, dxo, g, lg):
        n, rstd = _norm(ALPHA * x + (1.0 + g) * y)
        dr = _norm_bwd(dxo * lg, n, rstd)
        return ALPHA * dr, (1.0 + g) * dr, _colsum(dxo * n), _colsum(dxo), _colsum(dr * y)

    def lnmod_bwd(x, dh, dxa, sc):
        n, rstd = _norm(x)
        return dxa + _norm_bwd(dh * (1.0 + sc), n, rstd), _colsum(dh * n), _colsum(dh)

    dx1a, dyf, grads["ln2_g"], grads["ln2_b"], dg_f = _rowwise(
        nm("ln2_bwd"), resid_ln_bwd, [(sv["x1"], 0, d), (sv["yf"], 0, d), (dx2, 0, d)], [g_f, _row(p["ln2_g"])],
        [(d, F32), (d, BF16)], [d, d, d])
    dact = _mm(nm("d_act"), dyf, p["w_ffn_out"], "nt")
    weight_grad("w_ffn_out", sv["act"], dyf)
    fh = sv["f"].shape[1] // 2

    def swiglu_bwd(g, u, da):
        g, u = g.astype(F32), u.astype(F32)
        sg = jax.nn.sigmoid(g)
        return jnp.concatenate([da * u * sg * (1.0 + g * (1.0 - sg)), da * g * sg], axis=1)

    df = _rowwise(nm("swiglu_bwd"), swiglu_bwd, [(sv["f"], 0, fh), (sv["f"], 1, fh), (dact, 0, fh)], [], [(2 * fh, BF16)])
    dh2 = _mm(nm("d_h2"), df, p["w_ffn_in"], "nt")
    weight_grad("w_ffn_in", sv["h2"], df)
    dx1, dsc_f, dsh_f = _rowwise(nm("lnmod2_bwd"), lnmod_bwd, [(sv["x1"], 0, d), (dh2, 0, d), (dx1a, 0, d)], [sc_f],
                                 [(d, F32)], [d, d])
    dxa, dy, grads["ln1_g"], grads["ln1_b"], dg_m = _rowwise(
        nm("ln1_bwd"), resid_ln_bwd, [(sv["x"], 0, d), (sv["y"], 0, d), (dx1, 0, d)], [g_m, _row(p["ln1_g"])],
        [(d, F32), (d, BF16)], [d, d, d])
    dmerged = _mm(nm("d_merged"), dy, p["w_out"], "nt")
    weight_grad("w_out", sv["merged"], dy)

    def merge_bwd(gsb, gss, ysb, yss, dm):
        s1, s2 = jax.nn.sigmoid(gsb), jax.nn.sigmoid(gss)
        return s1 * dm, s2 * dm, dm * ysb * s1 * (1.0 - s1), dm * yss * s2 * (1.0 - s2)

    dy_sb, dy_ssm, dg_sb, dg_ssm = _rowwise(
        nm("merge_bwd"), merge_bwd, [(sv["proj"], 2, d), (sv["proj"], 3, d), (sv["y_sb"], 0, d), (sv["y_ssm"], 0, d),
                                     (dmerged, 0, d)], [], [(d, BF16)] * 4)
    dy2 = _mm(nm("d_y2"), dy_ssm, p["w_ssm_up"], "nt")
    weight_grad("w_ssm_up", sv["y2"], dy_ssm)

    def glu_act_bwd(y1, gl, dy2, b):
        sg = jax.nn.sigmoid(gl + b)
        dgl = dy2 * y1 * sg * (1.0 - sg)
        return dy2 * sg, dgl, _colsum(dgl)

    dy1a, dgl, grads["b_glu"] = _rowwise(nm("glu_act_bwd"), glu_act_bwd, [(sv["y1"], 0, 512), (sv["gl"], 0, 512), (dy2, 0, 512)],
                                         [_row(p["b_glu"])], [(512, F32), (512, BF16)], [512])
    dy1b = _mm(nm("d_y1"), dgl, p["w_glu"], "nt")
    weight_grad("w_glu", sv["y1"], dgl)

    def ssm_act_bwd(y0, u, dy1a, dy1b, dsk):
        dy0 = (dy1a + dy1b) * _gelu_grad(y0)
        return dy0, dsk * dy0, _colsum(dy0 * u)

    dy0, du_a, grads["ssm_d"] = _rowwise(nm("ssm_act_bwd"), ssm_act_bwd,
                                         [(sv["y0"], 0, 512), (sv["proj"], 3, 512), (dy1a, 0, 512), (dy1b, 0, 512)],
                                         [_row(p["ssm_d"])], [(512, BF16), (512, F32)], [512])
    du_b, dlam, d_bmat, d_cmat = _s5_bwd(dy0, sv["hst"], sv["proj"], U_OFFSET, sv["bmat"], sv["cmat"], sv["lam"])
    grads["ssm_c_re"], grads["ssm_c_im"] = _c_matrix_grad(d_cmat)
    g_bbr, g_bbi = _b_matrix_grad(d_bmat)
    g_lr, g_li = _deinterleave(dlam[0:1], 1)
    da_re, da_im, dldt, db_re, db_im = _ssm_params_bwd(*_ssm_inputs(p), g_lr.reshape(N_STATE, 1), g_li.reshape(N_STATE, 1),
                                                       g_bbr, g_bbi)
    grads["ssm_a_re"] = da_re.reshape(SSM_GROUPS, SSM_STATE)
    grads["ssm_a_im"] = da_im.reshape(SSM_GROUPS, SSM_STATE)
    grads["ssm_log_dt"] = dldt.reshape(SSM_GROUPS)
    grads["ssm_b_re"] = db_re.reshape(SSM_GROUPS, SSM_STATE, SSM_GROUP)
    grads["ssm_b_im"] = db_im.reshape(SSM_GROUPS, SSM_STATE, SSM_GROUP)
    datt = _mm(nm("d_att"), dy_sb, p["w_sb_up"], "nt", out_dtype=BF16)
    weight_grad("w_sb_up", sv["att"], dy_sb)
    dqs, dk, dv = _attn_bwd(sv["qs"], datt, sv["kb"], sv["vb"], sv["kt3"], sv["car"], sv["t"])

    def dproj_cols(dqs, dk, dv, dua, dub, dgsb, dgss):
        return jnp.concatenate([dqs * (1.0 / math.sqrt(HEAD_DIM)), dk, dv, dua + dub, dgsb.astype(F32), dgss.astype(F32)],
                               axis=1)

    dproj = _rowwise(nm("dproj"), dproj_cols, [(dqs, 0, 512), (dk, 0, 512), (dv, 0, 512), (du_a, 0, 512), (du_b, 0, 512),
                                               (dg_sb, 0, d), (dg_ssm, 0, d)], [], [(2048 + 2 * d, BF16)])
    dh1 = _mm(nm("d_h1"), dproj, p["w_in"], "nt")
    weight_grad("w_in", sv["h1"], dproj)
    dx, dsc_m, dsh_m = _rowwise(nm("lnmod1_bwd"), lnmod_bwd, [(sv["x"], 0, d), (dh1, 0, d), (dxa, 0, d)], [sc_m],
                                [(d, F32)], [d, d])
    for k in ("ln1_g", "ln1_b", "ln2_g", "ln2_b", "ssm_d", "b_glu"):
        grads[k] = grads[k].reshape(-1)
    dmod = jnp.concatenate([dsh_m, dsc_m, dg_m, dsh_f, dsc_f, dg_f], axis=0)
    return dx, dmod, grads


def _loss_head(x, target):
    d = x.shape[1]

    def fn(x, tgt):
        err = x - tgt
        return err * (1.0 / d), _colsum(err * err) * (0.5 / d)

    return _rowwise("loss_head", fn, [(x, 0, d), (target, 0, d)], [], [(d, F32)], [d])


def _place():
    return lax.axis_index("x"), lax.axis_index("y"), lax.axis_index("c")


def _all_gather8(name, block):
    m_per, n = block.shape

    def body(x_ref, out_ref, send_sems, recv_sems, local_sem):
        x, y, c = _place()
        me, sibling = (x, y, c), (x, y, 1 - c)
        chips = [(1 - x, y), (x, 1 - y), (1 - x, 1 - y)]

        def rows(px, py, pc):
            return out_ref.at[pl.ds(pl.multiple_of((4 * px + 2 * py + pc) * m_per, 8), m_per), :]

        def copy(k, blk, to, src=None):
            return pltpu.make_async_remote_copy(src_ref=rows(*blk) if src is None else src, dst_ref=rows(*blk),
                                                send_sem=send_sems.at[k], recv_sem=recv_sems.at[k],
                                                device_id=to, device_id_type=MESH)

        mine = pltpu.make_async_copy(x_ref, rows(*me), local_sem)
        mine.start()
        first = [copy(0, me, sibling, src=x_ref)] + [copy(1 + j, me, (*chip, c), src=x_ref) for j, chip in enumerate(chips)]
        for cp in first:
            cp.start()
        passed = [copy(4 + j, (*chip, c), sibling) for j, chip in enumerate(chips)]
        for j, chip in enumerate(chips):
            copy(1 + j, (*chip, c), me).wait_recv()
            passed[j].start()
        copy(0, sibling, me).wait_recv()
        for j, chip in enumerate(chips):
            copy(4 + j, (*chip, 1 - c), me).wait_recv()
        for cp in first + passed:
            cp.wait_send()
        mine.wait()

    return pl.pallas_call(
        body, name=name, out_shape=jax.ShapeDtypeStruct((8 * m_per, n), block.dtype),
        in_specs=[pl.BlockSpec(memory_space=pltpu.VMEM)], out_specs=pl.BlockSpec(memory_space=pltpu.VMEM),
        scratch_shapes=[pltpu.SemaphoreType.DMA((7,)), pltpu.SemaphoreType.DMA((7,)), pltpu.SemaphoreType.DMA],
        compiler_params=_params(),
    )(block)


def _other_chips(x, y):
    return [(1 - x, y), (x, 1 - y), (1 - x, 1 - y)]


def _gather_weights(whole, by_rows):
    n = len(whole)

    def body(*refs):
        dst = refs[n:2 * n]
        ici_send, ici_recv, d2d_send, d2d_recv = refs[2 * n:]
        x, y, c = _place()
        chips = _other_chips(x, y)

        def part(ref, k, px, py, pc):
            _, r, cols = whole[k].shape
            q = 2 * px + py
            if by_rows[k]:
                return ref[k].at[:, pl.ds(pl.multiple_of((2 * q + pc) * (r // 8), 16), r // 8), :]
            return ref[k].at[:, pl.ds(pl.multiple_of(pc * (r // 2), 16), r // 2),
                             pl.ds(pl.multiple_of(q * (cols // 4), LANES), cols // 4)]

        def ici(k, j, px, py, to):
            return pltpu.make_async_remote_copy(src_ref=part(dst, k, px, py, c), dst_ref=part(dst, k, px, py, c),
                                                send_sem=ici_send.at[k, j], recv_sem=ici_recv.at[k, j],
                                                device_id=(*to, c), device_id_type=MESH)

        def d2d(k, j, px, py, pc):
            return pltpu.make_async_remote_copy(src_ref=part(dst, k, px, py, pc), dst_ref=part(dst, k, px, py, pc),
                                                send_sem=d2d_send.at[k, j], recv_sem=d2d_recv.at[k, j],
                                                device_id=(x, y, 1 - c), device_id_type=MESH)

        for k in range(n):
            for j, chip in enumerate(chips):
                ici(k, j, x, y, chip).start()
        for k in range(n):
            for j, chip in enumerate(chips):
                ici(k, j, *chip, chip).wait_recv()
                d2d(k, j, *chip, c).start()
        for k in range(n):
            for j, chip in enumerate(chips):
                d2d(k, j, *chip, 1 - c).wait_recv()
        for k in range(n):
            for j, chip in enumerate(chips):
                ici(k, j, x, y, chip).wait_send()
                d2d(k, j, *chip, c).wait_send()

    any_spec = pl.BlockSpec(memory_space=pl.ANY)
    return pl.pallas_call(
        body, name="gather_weights", in_specs=[any_spec] * n, out_specs=[any_spec] * n,
        out_shape=[jax.ShapeDtypeStruct(a.shape, a.dtype) for a in whole], input_output_aliases={k: k for k in range(n)},
        scratch_shapes=[pltpu.SemaphoreType.DMA((n, 3))] * 4,
        compiler_params=_params(),
    )(*whole)


def _part_shape(shape, by_rows):
    l, r, c = shape
    return (l, r // 8, c) if by_rows else (l, r // 2, c // 4)


def _pair_exchange(grads, by_rows):
    n = len(grads)

    def body(*refs):
        src, dst = refs[:n], refs[n:2 * n]
        send_sems, recv_sems = refs[2 * n:]
        x, y, c = _place()

        def window(k, q, pc):
            _, hr, hc = _part_shape(grads[k].shape, by_rows[k])
            if by_rows[k]:
                return src[k].at[:, pl.ds(pl.multiple_of((2 * q + pc) * hr, 16), hr), :]
            return src[k].at[:, pl.ds(pl.multiple_of(pc * hr, 16), hr), pl.ds(q * hc, hc)]

        def copy(k, q, pc):
            return pltpu.make_async_remote_copy(src_ref=window(k, q, pc), dst_ref=dst[k].at[q], send_sem=send_sems.at[k, q],
                                                recv_sem=recv_sems.at[k, q], device_id=(x, y, 1 - c), device_id_type=MESH)

        for k in range(n):
            for q in range(4):
                copy(k, q, 1 - c).start()
        for k in range(n):
            for q in range(4):
                copy(k, q, c).wait_recv()
        for k in range(n):
            for q in range(4):
                copy(k, q, 1 - c).wait_send()

    any_spec = pl.BlockSpec(memory_space=pl.ANY)
    return pl.pallas_call(
        body, name="pair_exchange", in_specs=[any_spec] * n, out_specs=[any_spec] * n,
        out_shape=[jax.ShapeDtypeStruct((4, *_part_shape(g.shape, rows)), g.dtype) for g, rows in zip(grads, by_rows)],
        scratch_shapes=[pltpu.SemaphoreType.DMA((n, 4)), pltpu.SemaphoreType.DMA((n, 4))],
        compiler_params=_params(),
    )(*grads)


def _pair_sum(name, g, theirs, by_rows, c, chip):
    _, l, hr, hc = theirs.shape
    tr = _pick(hr, (256, 176, 128, 64, 32))

    def body(s_ref, g_ref, t_ref, p_ref, own_ref):
        v = (g_ref[...].astype(F32) + t_ref[0].astype(F32)).astype(BF16)
        p_ref[0] = v

        @pl.when(pl.program_id(2) == s_ref[1])
        def _():
            own_ref[0] = v

    if by_rows:
        g_spec = pl.BlockSpec((1, tr, hc), lambda li, i, q, s: (li, (2 * q + s[0]) * (hr // tr) + i, 0))
    else:
        g_spec = pl.BlockSpec((1, tr, hc), lambda li, i, q, s: (li, s[0] * (hr // tr) + i, q))
    slot = pl.BlockSpec((1, 1, tr, hc), lambda li, i, q, s: (q, li, i, 0))
    grid_spec = pltpu.PrefetchScalarGridSpec(
        num_scalar_prefetch=1, grid=(l, hr // tr, 4), in_specs=[g_spec, slot],
        out_specs=[slot, pl.BlockSpec((1, 1, tr, hc), lambda li, i, q, s: (s[1], li, i, 0))])
    return pl.pallas_call(
        body, name=name, grid_spec=grid_spec, out_shape=[jax.ShapeDtypeStruct(theirs.shape, BF16)] * 2,
        compiler_params=_params(dimension_semantics=("arbitrary", "arbitrary", "arbitrary")),
    )(jnp.stack([c, chip]).astype(jnp.int32), g, theirs)


def _chip_scatter(sums, landing):
    n = len(sums)

    def body(*refs):
        src, dst = refs[:n], refs[2 * n:3 * n]
        send_sems, recv_sems = refs[3 * n:]
        x, y, c = _place()
        mine = 2 * x + y

        def copy(k, j, src_slot, dst_slot, to):
            return pltpu.make_async_remote_copy(src_ref=src[k].at[src_slot], dst_ref=dst[k].at[dst_slot],
                                                send_sem=send_sems.at[k, j], recv_sem=recv_sems.at[k, j],
                                                device_id=(*to, c), device_id_type=MESH)

        chips = _other_chips(x, y)
        for k in range(n):
            for j, (px, py) in enumerate(chips):
                copy(k, j, 2 * px + py, mine, (px, py)).start()
        for k in range(n):
            for j, (px, py) in enumerate(chips):
                copy(k, j, mine, 2 * px + py, (px, py)).wait_recv()
        for k in range(n):
            for j, (px, py) in enumerate(chips):
                copy(k, j, 2 * px + py, mine, (px, py)).wait_send()

    any_spec = pl.BlockSpec(memory_space=pl.ANY)
    return pl.pallas_call(
        body, name="chip_scatter", in_specs=[any_spec] * (2 * n), out_specs=[any_spec] * n,
        out_shape=[jax.ShapeDtypeStruct(a.shape, a.dtype) for a in landing],
        input_output_aliases={n + k: k for k in range(n)},
        scratch_shapes=[pltpu.SemaphoreType.DMA((n, 3)), pltpu.SemaphoreType.DMA((n, 3))],
        compiler_params=_params(),
    )(*sums, *landing)


def _sum_slots(name, parts, half=None):
    slots, l, r, c = parts.shape
    tr = _pick(r, (256, 176, 128, 64, 32, 8))

    def body(*refs):
        p_ref, o_ref = refs[-2:]
        acc = p_ref[0].astype(F32)
        for i in range(1, slots):
            acc = acc + p_ref[i].astype(F32)
        o_ref[...] = acc

    if half is None:
        return pl.pallas_call(
            body, name=name, grid=(l, r // tr), in_specs=[pl.BlockSpec((slots, 1, tr, c), lambda li, i: (0, li, i, 0))],
            out_specs=pl.BlockSpec((1, tr, c), lambda li, i: (li, i, 0)), out_shape=jax.ShapeDtypeStruct((l, r, c), F32),
            compiler_params=_params(dimension_semantics=("arbitrary", "arbitrary")),
        )(parts)
    grid_spec = pltpu.PrefetchScalarGridSpec(
        num_scalar_prefetch=1, grid=(l, r // tr),
        in_specs=[pl.BlockSpec((slots, 1, tr, c), lambda li, i, h: (0, li, i, 0))],
        out_specs=pl.BlockSpec((1, tr, c), lambda li, i, h: (li, h[0] * (r // tr) + i, 0)))
    return pl.pallas_call(
        body, name=name, grid_spec=grid_spec, out_shape=jax.ShapeDtypeStruct((l, 2 * r, c), F32),
        compiler_params=_params(dimension_semantics=("arbitrary", "arbitrary")),
    )(jnp.reshape(half, (1,)).astype(jnp.int32), parts)


def _swap_halves(blocks):
    n = len(blocks)

    def body(*refs):
        src, dst = refs[:n], refs[n:2 * n]
        send_sems, recv_sems = refs[2 * n:]
        x, y, c = _place()

        def half(ref, k, pc):
            r = blocks[k].shape[1] // 2
            return ref[k].at[:, pl.ds(pl.multiple_of(pc * r, 8), r), :]

        def copy(k, pc):
            return pltpu.make_async_remote_copy(src_ref=half(src, k, pc), dst_ref=half(dst, k, pc), send_sem=send_sems.at[k],
                                                recv_sem=recv_sems.at[k], device_id=(x, y, 1 - c), device_id_type=MESH)

        for k in range(n):
            copy(k, c).start()
        for k in range(n):
            copy(k, 1 - c).wait_recv()
        for k in range(n):
            copy(k, c).wait_send()

    any_spec = pl.BlockSpec(memory_space=pl.ANY)
    return pl.pallas_call(
        body, name="swap_halves", in_specs=[any_spec] * n, out_specs=[any_spec] * n,
        out_shape=[jax.ShapeDtypeStruct(b.shape, b.dtype) for b in blocks], input_output_aliases={k: k for k in range(n)},
        scratch_shapes=[pltpu.SemaphoreType.DMA((n,)), pltpu.SemaphoreType.DMA((n,))],
        compiler_params=_params(),
    )(*blocks)


def _adamw(name, w, g, m, v):
    shape = w.shape
    cols = shape[-1] if w.ndim > 1 and shape[-1] % LANES == 0 else w.size if w.size % LANES else LANES
    flat = lambda a: a.reshape(-1, cols)
    rows = w.size // cols
    tr = _pick(rows, [r for r in (512, 256, 128, 64, 32, 16, 8) if r * cols <= 256 * 1024]) if rows % 8 == 0 else rows

    def body(w_ref, g_ref, m_ref, v_ref, go_ref, d_ref, nm_ref, nv_ref):
        gg = g_ref[...]
        go_ref[...] = gg
        nm = ADAM_B1 * m_ref[...] + (1.0 - ADAM_B1) * gg
        nv = ADAM_B2 * v_ref[...] + (1.0 - ADAM_B2) * (gg * gg)
        m_hat = nm / (1.0 - ADAM_B1 ** ADAM_STEP)
        v_hat = nv / (1.0 - ADAM_B2 ** ADAM_STEP)
        d_ref[...] = -ADAM_LR * (m_hat / (jnp.sqrt(v_hat) + ADAM_EPS) + ADAM_WD * w_ref[...])
        nm_ref[...] = nm
        nv_ref[...] = nv

    spec = pl.BlockSpec((tr, cols), lambda i: (i, 0))
    out = pl.pallas_call(
        body, name=name, grid=(rows // tr,), in_specs=[spec] * 4, out_specs=[spec] * 4,
        out_shape=[jax.ShapeDtypeStruct((rows, cols), F32)] * 4,
        compiler_params=_params(dimension_semantics=("arbitrary",)),
    )(flat(w), flat(g), flat(m), flat(v))
    return tuple(o.reshape(shape) for o in out)


WEIGHTS = ["w_ada", "b_ada", "w_in", "w_sb_up", "ssm_a_re", "ssm_a_im", "ssm_log_dt", "ssm_b_re", "ssm_b_im", "ssm_c_re",
           "ssm_c_im", "ssm_d", "w_glu", "b_glu", "w_ssm_up", "w_out", "ln1_g", "ln1_b", "w_ffn_in", "w_ffn_out", "ln2_g",
           "ln2_b"]
COL_SPLIT = ["w_in", "w_sb_up", "w_ssm_up", "w_ffn_in"]
ROW_SPLIT = ["w_glu", "w_out", "w_ffn_out"]
SMALL = ["ssm_a_re", "ssm_a_im", "ssm_log_dt", "ssm_b_re", "ssm_b_im", "ssm_c_re", "ssm_c_im", "ssm_d", "b_glu", "ln1_g",
         "ln1_b", "ln2_g", "ln2_b"]
SLAB_COLS = 1024


def _cast_into_whole(name, w, by_rows, chip):
    l, r, cols = w.shape
    tr = _pick(r, (512, 256, 128, 64, 16))

    def body(q_ref, w_ref, o_ref):
        o_ref[...] = w_ref[...].astype(BF16)

    if by_rows:
        out_map, shape = (lambda li, i, q: (li, q[0] * (r // tr) + i, 0)), (l, 4 * r, cols)
    else:
        out_map, shape = (lambda li, i, q: (li, i, q[0])), (l, r, 4 * cols)
    grid_spec = pltpu.PrefetchScalarGridSpec(
        num_scalar_prefetch=1, grid=(l, r // tr), in_specs=[pl.BlockSpec((1, tr, cols), lambda li, i, q: (li, i, 0))],
        out_specs=pl.BlockSpec((1, tr, cols), out_map))
    return pl.pallas_call(body, name=name, grid_spec=grid_spec, out_shape=jax.ShapeDtypeStruct(shape, BF16),
                          compiler_params=_params(dimension_semantics=("arbitrary", "arbitrary")),
                          )(jnp.reshape(chip, (1,)).astype(jnp.int32), w)


def _silu_rows(name, c):
    def body(c_ref, o_ref):
        v = c_ref[...]
        o_ref[...] = v * jax.nn.sigmoid(v)

    return pl.pallas_call(body, name=name, out_shape=jax.ShapeDtypeStruct(c.shape, F32), compiler_params=_params())(c)


def _pad_rows(v, mult=8):
    flat = v.reshape(-1)
    per = mult * SLAB_COLS
    total = -(-flat.size // per) * per
    return jnp.pad(flat, (0, total - flat.size)).reshape(-1, SLAB_COLS)


def kernel(x, c, w_ada, b_ada, w_in, w_sb_up, ssm_a_re, ssm_a_im, ssm_log_dt, ssm_b_re, ssm_b_im, ssm_c_re, ssm_c_im, ssm_d, w_glu, b_glu, w_ssm_up, w_out, ln1_g, ln1_b, w_ffn_in, w_ffn_out, ln2_g, ln2_b, loss_target, m_w_ada, m_b_ada, m_w_in, m_w_sb_up, m_ssm_a_re, m_ssm_a_im, m_ssm_log_dt, m_ssm_b_re, m_ssm_b_im, m_ssm_c_re, m_ssm_c_im, m_ssm_d, m_w_glu, m_b_glu, m_w_ssm_up, m_w_out, m_ln1_g, m_ln1_b, m_w_ffn_in, m_w_ffn_out, m_ln2_g, m_ln2_b, v_w_ada, v_b_ada, v_w_in, v_w_sb_up, v_ssm_a_re, v_ssm_a_im, v_ssm_log_dt, v_ssm_b_re, v_ssm_b_im, v_ssm_c_re, v_ssm_c_im, v_ssm_d, v_w_glu, v_b_glu, v_w_ssm_up, v_w_out, v_ln1_g, v_ln1_b, v_w_ffn_in, v_w_ffn_out, v_ln2_g, v_ln2_b):
    args = dict(locals())
    w = {n: args[n] for n in WEIGHTS}
    mom = {n: args["m_" + n] for n in WEIGHTS}
    var = {n: args["v_" + n] for n in WEIGHTS}
    depth, d = w_ada.shape[0], x.shape[-1]
    xi, yi, ci = _place()
    me, chip = 4 * xi + 2 * yi + ci, 2 * xi + yi
    ada_cols = w_ada.shape[2]

    big = COL_SPLIT + ROW_SPLIT
    by_rows = [n in ROW_SPLIT for n in big]
    full = dict(zip(big, _gather_weights([_cast_into_whole(f"cast_{n}", w[n], n in ROW_SPLIT, chip) for n in big], by_rows)))

    c_all = _all_gather8("gather_c", jnp.pad(c, ((0, 7), (0, 0))))[::8]
    c_act = _silu_rows("silu_c", c_all)
    b_cols = lax.dynamic_slice_in_dim(b_ada, chip * ada_cols, ada_cols, axis=1)
    mod_part = jnp.concatenate([_small_mm(f"mod_{l}", c_act, w_ada[l], "nn") + b_cols[l][None] for l in range(depth)], axis=0)
    mod_all = _all_gather8("gather_mod", mod_part).reshape(4, 2, depth, 8, ada_cols)[:, 0]
    mod_mine = lax.dynamic_index_in_dim(mod_all, me, axis=2, keepdims=False)
    mod = mod_mine.transpose(1, 0, 2).reshape(depth, 6, d)

    layer_w = [{**{n: full[n][l] for n in big}, **{n: w[n][l] for n in SMALL}} for l in range(depth)]
    h, saved = x[0], []
    for l in range(depth):
        h, sv = _layer_fwd(h, mod[l], layer_w[l], str(l))
        saved.append(sv)
    dh, loss_cols = _loss_head(h, loss_target[0])
    loss = lax.psum(jnp.sum(loss_cols), ("x", "y", "c"))
    dmods, lgrads, stacked = [None] * depth, [None] * depth, {}
    for l in reversed(range(depth)):
        dh, dmods[l], lgrads[l] = _layer_bwd(dh, mod[l], layer_w[l], saved[l], l, depth, stacked)
        stacked = {n: lgrads[l][n] for n in big}
    grad_x = dh[None]

    theirs = _pair_exchange([stacked[n] for n in big], by_rows)
    pairs = [_pair_sum(f"pair_{n}", stacked[n], t, n in ROW_SPLIT, ci, chip) for n, t in zip(big, theirs)]
    landed = _chip_scatter([p[0] for p in pairs], [p[1] for p in pairs])
    halves = [_sum_slots(f"sum_{n}", p, half=ci) for n, p in zip(big, landed)]
    grad = dict(zip(big, _swap_halves(halves)))

    pieces = [jnp.stack(dmods)] + [jnp.stack([lgrads[l][n] for l in range(depth)]) for n in SMALL]
    slab = jnp.concatenate([_pad_rows(p) for p in pieces], axis=0)
    slabs = _all_gather8("gather_small", slab).reshape(8, 1, *slab.shape)
    total = _sum_slots("sum_small", slabs)[0]
    row = _pad_rows(pieces[0]).shape[0]
    for n, p in zip(SMALL, pieces[1:]):
        rows = _pad_rows(p).shape[0]
        grad[n] = total[row:row + rows].reshape(-1)[:p.size].reshape(p.shape)
        row += rows
    dmod_rows = _pad_rows(pieces[0]).shape[0]
    dmod_all = slabs[:, 0, :dmod_rows].reshape(8, -1)[:, :depth * 6 * d].reshape(8, depth, 4, ada_cols)
    dmod_cols = lax.dynamic_index_in_dim(dmod_all, chip, axis=2, keepdims=False)
    grad["w_ada"] = jnp.stack([_small_mm(f"dw_ada_{l}", c_act, dmod_cols[:, l], "tn") for l in range(depth)])
    dmod_sum = _sum_slots("sum_dmod", slabs[:, :, :dmod_rows])[0]
    grad["b_ada"] = dmod_sum.reshape(-1)[:depth * 6 * d].reshape(depth, 6 * d)

    delta, new_m, new_v = {}, {}, {}
    for n in WEIGHTS:
        grad[n], delta[n], new_m[n], new_v[n] = _adamw(f"adamw_{n}", w[n], grad[n], mom[n], var[n])
    return (loss, grad_x, *[grad[n] for n in WEIGHTS], *[delta[n] for n in WEIGHTS], *[new_m[n] for n in WEIGHTS],
            *[new_v[n] for n in WEIGHTS])
```

```python
import functools
import math

import jax
import jax.numpy as jnp
from jax import lax
from jax.experimental import pallas as pl
from jax.experimental.pallas import tpu as pltpu

F32 = jnp.float32
BF16 = jnp.bfloat16
MESH = pl.DeviceIdType.MESH

LANES = 128
HEAD_DIM = 64
ATT_TILE = 256
SSM_GROUPS, SSM_STATE, SSM_GROUP = 32, 64, 16
N_STATE = SSM_GROUPS * SSM_STATE
SSM_BLOCKS = SSM_GROUPS * SSM_GROUP // LANES
U_OFFSET = 3 * 512
LN_EPS = 1e-5
DEPTH = 2
ALPHA = (2 * DEPTH) ** 0.25
ADAM_LR, ADAM_B1, ADAM_B2, ADAM_EPS, ADAM_WD, ADAM_STEP = 0.001, 0.9, 0.999, 1e-08, 0.01, 10
VMEM_LIMIT = 56 * 1024 * 1024
GELU_K = math.sqrt(2.0 / math.pi)
GELU_C = 0.044715


def _params(**kw):
    return pltpu.CompilerParams(vmem_limit_bytes=VMEM_LIMIT, **kw)


def _pick(n, prefs):
    for p in prefs:
        if n % p == 0:
            return p
    return n


def _rowwise(name, fn, rows, vecs, outs, sums=(), tm=None):
    s = rows[0][0].shape[0]
    tm = tm or _pick(s, (256, 128, 64, 8))
    nin, no, ns = len(rows) + len(vecs), len(outs), len(sums)

    def body(*refs):
        res = fn(*[r[...].astype(F32) for r in refs[:nin]])
        res = res if isinstance(res, tuple) else (res,)
        for r, v in zip(refs[nin:nin + no], res[:no]):
            r[...] = v.astype(r.dtype)
        if ns:
            @pl.when(pl.program_id(0) == 0)
            def _():
                for r in refs[nin + no:]:
                    r[...] = jnp.zeros_like(r)
            for r, v in zip(refs[nin + no:], res[no:]):
                r[...] += v

    in_specs = [pl.BlockSpec((tm, w), lambda i, cb=cb: (i, cb)) for _, cb, w in rows]
    in_specs += [pl.BlockSpec(v.shape, lambda i: (0, 0)) for v in vecs]
    out_specs = [pl.BlockSpec((tm, w), lambda i: (i, 0)) for w, _ in outs]
    out_specs += [pl.BlockSpec((1, w), lambda i: (0, 0)) for w in sums]
    out_shape = [jax.ShapeDtypeStruct((s, w), dt) for w, dt in outs]
    out_shape += [jax.ShapeDtypeStruct((1, w), F32) for w in sums]
    res = pl.pallas_call(
        body, name=name, grid=(s // tm,), in_specs=in_specs, out_specs=out_specs, out_shape=out_shape,
        compiler_params=_params(dimension_semantics=("arbitrary",)),
    )(*[a for a, _, _ in rows], *vecs)
    return res[0] if len(res) == 1 else tuple(res)


MM_TILES = (1408, 1024, 512, 256, 128)


def _mm(name, a, b, mode, out_dtype=F32, into=None):
    if mode == "nn":
        m, k, n = a.shape[0], a.shape[1], b.shape[1]
    elif mode == "nt":
        m, k, n = a.shape[0], a.shape[1], b.shape[0]
    else:
        k, m, n = a.shape[0], a.shape[1], b.shape[1]
    tm = _pick(m, MM_TILES if mode == "tn" else MM_TILES[1:])
    tn = _pick(n, MM_TILES)
    tk = _pick(k, MM_TILES)
    nk = k // tk
    dims = {"nn": ((1,), (0,)), "nt": ((1,), (1,)), "tn": ((0,), (0,))}[mode]

    def body(a_ref, b_ref, *rest):
        o_ref = rest[-2] if nk > 1 else rest[-1]
        prod = lax.dot_general(a_ref[...].astype(BF16), b_ref[...].astype(BF16), (dims, ((), ())),
                               preferred_element_type=F32)
        if nk == 1:
            o_ref[...] = prod.astype(o_ref.dtype)
            return
        acc_ref = rest[-1]
        kk = pl.program_id(2)

        @pl.when(kk == 0)
        def _():
            acc_ref[...] = prod

        @pl.when(kk > 0)
        def _():
            acc_ref[...] += prod

        @pl.when(kk == nk - 1)
        def _():
            o_ref[...] = acc_ref[...].astype(o_ref.dtype)

    if mode == "tn":
        a_spec = pl.BlockSpec((tk, tm), lambda i, j, kk: (kk, i))
    else:
        a_spec = pl.BlockSpec((tm, tk), lambda i, j, kk: (i, kk))
    if mode == "nt":
        b_spec = pl.BlockSpec((tn, tk), lambda i, j, kk: (j, kk))
    else:
        b_spec = pl.BlockSpec((tk, tn), lambda i, j, kk: (kk, j))
    in_specs, operands, aliases = [a_spec, b_spec], [a, b], {}
    if into is None:
        out_spec = pl.BlockSpec((tm, tn), lambda i, j, kk: (i, j))
        out_shape = jax.ShapeDtypeStruct((m, n), out_dtype)
    else:
        buf, slab, count = into
        out_spec = pl.BlockSpec((None, tm, tn), lambda i, j, kk: (slab, i, j))
        out_shape = jax.ShapeDtypeStruct((count, m, n), out_dtype)
        if buf is not None:
            in_specs.append(pl.BlockSpec(memory_space=pl.ANY))
            operands.append(buf)
            aliases = {2: 0}
    return pl.pallas_call(
        body, name=name, grid=(m // tm, n // tn, nk), in_specs=in_specs, out_specs=out_spec, out_shape=out_shape,
        scratch_shapes=[pltpu.VMEM((tm, tn), F32)] if nk > 1 else [], input_output_aliases=aliases,
        compiler_params=_params(dimension_semantics=("arbitrary", "arbitrary", "arbitrary")),
    )(*operands)


def _small_mm(name, a, b, mode):
    dims = {"nn": ((1,), (0,)), "tn": ((0,), (0,))}[mode]
    m = a.shape[0] if mode == "nn" else a.shape[1]

    def body(a_ref, b_ref, o_ref):
        o_ref[...] = lax.dot_general(a_ref[...], b_ref[...], (dims, ((), ())), precision=lax.Precision.HIGHEST,
                                     preferred_element_type=F32)

    return pl.pallas_call(body, name=name, out_shape=jax.ShapeDtypeStruct((m, b.shape[1]), F32),
                          compiler_params=_params())(a, b)


def _norm(x):
    mu = jnp.mean(x, axis=-1, keepdims=True)
    xc = x - mu
    rstd = lax.rsqrt(jnp.mean(xc * xc, axis=-1, keepdims=True) + LN_EPS)
    return xc * rstd, rstd


def _norm_bwd(dn, n, rstd):
    return rstd * (dn - jnp.mean(dn, axis=-1, keepdims=True) - n * jnp.mean(dn * n, axis=-1, keepdims=True))


def _colsum(v):
    return jnp.sum(v, axis=0, keepdims=True)


def _gelu(x):
    return 0.5 * x * (1.0 + jnp.tanh(GELU_K * (x + GELU_C * x * x * x)))


def _gelu_grad(x):
    t = jnp.tanh(GELU_K * (x + GELU_C * x * x * x))
    return 0.5 * (1.0 + t) + 0.5 * x * (1.0 - t * t) * GELU_K * (1.0 + 3.0 * GELU_C * x * x)


def _log_sigmoid_parts(z):
    lb = jnp.minimum(z, 0.0) - jnp.log(1.0 + jnp.exp(-jnp.abs(z)))
    return lb, lb - z


def _qkv_prep(proj, t):
    s = proj.shape[0]
    nb, nhp = s // t, 512 // LANES

    def body(q_ref, k_ref, v_ref, qs_ref, kb_ref, vb_ref, kt_ref, vt_ref):
        qs_ref[...] = (q_ref[...] * (1.0 / math.sqrt(HEAD_DIM))).astype(BF16)
        k, v = k_ref[...], v_ref[...]
        kb_ref[...] = k.astype(BF16)
        vb_ref[...] = v.astype(BF16)
        for hp in range(nhp):
            kt_ref[hp, 0] = k[:, hp * LANES:(hp + 1) * LANES].T.astype(BF16)
            vt_ref[hp, 0] = v[:, hp * LANES:(hp + 1) * LANES].T.astype(BF16)

    col = lambda cb: pl.BlockSpec((t, 512), lambda i, cb=cb: (i, cb))
    row_out = pl.BlockSpec((t, 512), lambda i: (i, 0))
    t_out = pl.BlockSpec((nhp, 1, LANES, t), lambda i: (0, i, 0, 0))
    return pl.pallas_call(
        body, name="qkv_prep", grid=(nb,), in_specs=[col(0), col(1), col(2)],
        out_specs=[row_out, row_out, row_out, t_out, t_out],
        out_shape=[jax.ShapeDtypeStruct((s, 512), BF16)] * 3 + [jax.ShapeDtypeStruct((nhp, nb, LANES, t), BF16)] * 2,
        compiler_params=_params(dimension_semantics=("arbitrary",)),
    )(proj, proj, proj)


def _tile_masks(t):
    row = lax.broadcasted_iota(jnp.int32, (t, t), 0)
    col = lax.broadcasted_iota(jnp.int32, (t, t), 1)
    return row, col


DEAD_LOG_WEIGHT = -110.0


def _walk_down(i, tiles, state, alive):
    st = lax.cond(i == 0, lambda s_: tiles([i], s_, [True]), lambda s_: tiles([i, i - 1], s_, [True, False]), state)
    n = jnp.maximum(i - 1, 0)

    def pair(c):
        return c[0] + 1, tiles([i - 2 - 2 * c[0], i - 3 - 2 * c[0]], c[1], [False, False])

    p, st = lax.while_loop(lambda c: (c[0] < n // 2) & alive(c[1]), pair, (jnp.int32(0), st))
    return lax.cond((n % 2 == 1) & (p == n // 2) & alive(st), lambda s_: tiles([0], s_, [False]), lambda s_: s_, st)


def _walk_up(i, first, tiles, state):
    n = jnp.maximum(i - 1 - first, 0)
    st = lax.fori_loop(0, n // 2, lambda p, s_: tiles([first + 2 * p, first + 2 * p + 1], s_, [False, False]), state)
    st = lax.cond(n % 2 == 1, lambda s_: tiles([i - 2], s_, [False]), lambda s_: s_, st)
    return lax.cond(i == 0, lambda s_: tiles([i], s_, [True]), lambda s_: tiles([i - 1, i], s_, [False, True]), st)


def _nt(a, b):
    return lax.dot_general(a, b, (((1,), (1,)), ((), ())), preferred_element_type=F32)


def _nn(a, b):
    return jnp.dot(a, b, preferred_element_type=F32)


def _attn_fwd(qs, k, vt3, t):
    s = qs.shape[0]
    nb, nhp = s // t, qs.shape[1] // LANES

    def body(q_ref, k_ref, vt_ref, o_ref, car_ref):
        i = pl.program_id(1)
        q2 = q_ref[...]
        lane_q = lax.broadcasted_iota(jnp.int32, q2.shape, 1)
        row, col = _tile_masks(t)
        later = (col > row).astype(BF16)
        valid = row < col
        orow = lax.broadcasted_iota(jnp.int32, (LANES, t), 0)
        car_ref[...] = jnp.full(car_ref.shape, 2.0 * DEAD_LOG_WEIGHT, F32)
        qh = [jnp.where((lane_q < HEAD_DIM) == (hh == 0), q2, jnp.zeros_like(q2)) for hh in range(2)]

        def tiles(js, state, diagonal):
            chains = [(n, hh) for n in range(len(js)) for hh in range(2)]
            kb = [k_ref[pl.ds(pl.multiple_of(j * t, t), t), :] for j in js]
            z = {ch: _nt(kb[ch[0]], qh[ch[1]]) for ch in chains}
            lb, aft, csum = {}, {}, {}
            for ch in chains:
                lb[ch], l1m = _log_sigmoid_parts(z[ch])
                if diagonal[ch[0]]:
                    l1m = jnp.where(valid, l1m, 0.0)
                aft[ch] = _nn(later, l1m.astype(BF16))
                csum[ch] = _colsum(l1m)
            state = list(state)
            for ch in chains:
                n, hh = ch
                c_after, acc = state[hh]
                w = jnp.exp(lb[ch] + aft[ch] + c_after)
                if diagonal[ch[0]]:
                    w = jnp.where(valid, w, 0.0)
                car_ref[hh, pl.ds(js[n], 1), :] = c_after
                state[hh] = (c_after + csum[ch], acc + _nn(vt_ref[0, js[n]], w.astype(BF16)))
            return tuple(state)

        def alive(state):
            return jnp.max(jnp.maximum(state[0][0], state[1][0])) >= DEAD_LOG_WEIGHT

        zero = (jnp.zeros((1, t), F32), jnp.zeros((LANES, t), F32))
        (_, acc0), (_, acc1) = _walk_down(i, tiles, (zero, zero), alive)
        o_ref[...] = jnp.where(orow < HEAD_DIM, acc0, acc1).T.astype(o_ref.dtype)

    return pl.pallas_call(
        body, name="attn_fwd", grid=(nhp, nb),
        in_specs=[pl.BlockSpec((t, LANES), lambda hp, i: (i, hp)),
                  pl.BlockSpec((s, LANES), lambda hp, i: (0, hp)),
                  pl.BlockSpec((1, nb, LANES, t), lambda hp, i: (hp, 0, 0, 0))],
        out_specs=[pl.BlockSpec((t, LANES), lambda hp, i: (i, hp)),
                   pl.BlockSpec((2, nb, t), lambda hp, i: (hp, 0, i))],
        out_shape=[jax.ShapeDtypeStruct((s, nhp * LANES), BF16), jax.ShapeDtypeStruct((2 * nhp, nb, s), F32)],
        compiler_params=_params(dimension_semantics=("arbitrary", "arbitrary")),
    )(qs, k, vt3)


def _attn_bwd(qs, do, k, v, kt3, car, t):
    s = qs.shape[0]
    nb, nhp = s // t, qs.shape[1] // LANES

    def body(q_ref, do_ref, k_ref, v_ref, kt_ref, car_ref, dq_ref, dk_ref, dv_ref):
        i = pl.program_id(1)

        @pl.when(i == 0)
        def _():
            dk_ref[...] = jnp.zeros_like(dk_ref)
            dv_ref[...] = jnp.zeros_like(dv_ref)

        q2, do2 = q_ref[...], do_ref[...]
        lane_q = lax.broadcasted_iota(jnp.int32, q2.shape, 1)
        row, col = _tile_masks(t)
        later = (col > row).astype(BF16)
        earlier = (col < row).astype(BF16)
        valid = row < col
        orow = lax.broadcasted_iota(jnp.int32, (LANES, t), 0)
        head = [(lane_q < HEAD_DIM) == (hh == 0) for hh in range(2)]
        qh = [jnp.where(hm, q2, jnp.zeros_like(q2)) for hm in head]
        doh = [jnp.where(hm, do2, jnp.zeros_like(do2)) for hm in head]

        def tiles(js, state, diagonal):
            chains = [(n, hh) for n in range(len(js)) for hh in range(2)]
            rows = [pl.ds(pl.multiple_of(j * t, t), t) for j in js]
            kb = [k_ref[r, :] for r in rows]
            vb = [v_ref[r, :] for r in rows]
            z = {ch: _nt(kb[ch[0]], qh[ch[1]]) for ch in chains}
            dw = {ch: _nt(vb[ch[0]], doh[ch[1]]) for ch in chains}
            lb, beta, aft = {}, {}, {}
            for ch in chains:
                lb[ch], l1m = _log_sigmoid_parts(z[ch])
                beta[ch] = jnp.exp(lb[ch])
                if diagonal[ch[0]]:
                    l1m = jnp.where(valid, l1m, 0.0)
                aft[ch] = _nn(later, l1m.astype(BF16))
            w, g, gsum, g_in = {}, {}, {}, {}
            for ch in chains:
                n, hh = ch
                w[ch] = jnp.exp(lb[ch] + aft[ch] + car_ref[hh, pl.ds(js[n], 1), :])
                if diagonal[ch[0]]:
                    w[ch] = jnp.where(valid, w[ch], 0.0)
                g[ch] = dw[ch] * w[ch]
                g_in[ch] = _nn(earlier, g[ch].astype(BF16))
                gsum[ch] = _colsum(g[ch])
            state = list(state)
            dk_t, dv_t = [None] * len(js), [None] * len(js)
            for ch in chains:
                n, hh = ch
                c_g, dqt = state[hh]
                dz = g[ch] - beta[ch] * (g[ch] + g_in[ch] + c_g)
                if diagonal[ch[0]]:
                    dz = jnp.where(valid, dz, 0.0)
                dzb, wb = dz.astype(BF16), w[ch].astype(BF16)
                dk_h, dv_h = _nn(dzb, qh[hh]), _nn(wb, doh[hh])
                dk_t[n] = dk_h if dk_t[n] is None else dk_t[n] + dk_h
                dv_t[n] = dv_h if dv_t[n] is None else dv_t[n] + dv_h
                state[hh] = (c_g + gsum[ch], dqt + _nn(kt_ref[0, js[n]], dzb))
            for n in range(len(js)):
                dk_ref[rows[n], :] += dk_t[n]
                dv_ref[rows[n], :] += dv_t[n]
            return tuple(state)

        reach = jnp.max(jnp.max(car_ref[...], axis=2, keepdims=True), axis=0)
        dead = (reach < DEAD_LOG_WEIGHT) & (lax.broadcasted_iota(jnp.int32, reach.shape, 0) < i)
        first = jnp.sum(jnp.where(dead, 1.0, 0.0)).astype(jnp.int32)
        zero = (jnp.zeros((1, t), F32), jnp.zeros((LANES, t), F32))
        (_, dq0), (_, dq1) = _walk_up(i, first, tiles, (zero, zero))
        dq_ref[...] = jnp.where(orow < HEAD_DIM, dq0, dq1).T

    tile_spec = pl.BlockSpec((t, LANES), lambda hp, i: (i, hp))
    whole = pl.BlockSpec((s, LANES), lambda hp, i: (0, hp))
    return pl.pallas_call(
        body, name="attn_bwd", grid=(nhp, nb),
        in_specs=[tile_spec, tile_spec, whole, whole,
                  pl.BlockSpec((1, nb, LANES, t), lambda hp, i: (hp, 0, 0, 0)),
                  pl.BlockSpec((2, nb, t), lambda hp, i: (hp, 0, i))],
        out_specs=[tile_spec, whole, whole],
        out_shape=[jax.ShapeDtypeStruct((s, nhp * LANES), F32)] * 3,
        compiler_params=_params(dimension_semantics=("arbitrary", "arbitrary")),
    )(qs, do, k, v, kt3, car)


SCAN_LANES = 1024
SCAN_ROWS = 8


def _scan_chunks(v):
    n = v.shape[1] // (2 * LANES)
    return [(v[:, c * 2 * LANES:c * 2 * LANES + LANES], v[:, c * 2 * LANES + LANES:(c + 1) * 2 * LANES]) for c in range(n)]


def _scan_tables(lr, li, reverse):
    if reverse:
        li = -li
    row = lax.broadcasted_iota(jnp.int32, (SCAN_ROWS, LANES), 0)
    powers = [(lr, li)]
    for _ in range(SCAN_ROWS - 1):
        pr, pi = powers[-1]
        powers.append((pr * lr - pi * li, pr * li + pi * lr))
    levels = []
    for d in (1, 2, 4):
        keep = (row < SCAN_ROWS - d) if reverse else (row >= d)
        levels.append((SCAN_ROWS - d if reverse else d,
                       (jnp.where(keep, powers[d - 1][0], 0.0), jnp.where(keep, powers[d - 1][1], 0.0))))
    pr = pi = jnp.zeros((SCAN_ROWS, LANES), F32)
    for r in range(SCAN_ROWS):
        steps = SCAN_ROWS - r if reverse else r + 1
        pr = jnp.where(row == r, powers[steps - 1][0], pr)
        pi = jnp.where(row == r, powers[steps - 1][1], pi)
    return levels, (pr, pi)


def _s5_fwd(proj, u_off, bmat, cmat, lam):
    s, w = proj.shape[0], bmat.shape[1]
    tt = _pick(s, (512, 256, 128, 8))
    nt = s // tt
    cin = bmat.shape[0] // SSM_BLOCKS

    def body(u_ref, b_ref, c_ref, lam_ref, h_ref, y_ref, x_ref, st_ref):
        @pl.when(pl.program_id(1) == 0)
        def _():
            st_ref[...] = jnp.zeros_like(st_ref)

        tables = [_scan_tables(lr, li, reverse=False) for lr, li in _scan_chunks(lam_ref[...])]
        x_ref[...] = _nn(u_ref[...].astype(BF16), b_ref[...])

        def tile(it, last):
            r0 = pl.multiple_of(it * SCAN_ROWS, SCAN_ROWS)
            last, parts = list(last), []
            for c, (xr, xi) in enumerate(_scan_chunks(x_ref[pl.ds(r0, SCAN_ROWS), :])):
                levels, (pr, pi) = tables[c]
                for d, (ar, ai) in levels:
                    sr, si = pltpu.roll(xr, d, 0), pltpu.roll(xi, d, 0)
                    xr, xi = xr + ar * sr - ai * si, xi + ar * si + ai * sr
                br, bi = last[2 * c], last[2 * c + 1]
                hr = xr + pr * br - pi * bi
                hi = xi + pr * bi + pi * br
                last[2 * c], last[2 * c + 1] = hr[SCAN_ROWS - 1:], hi[SCAN_ROWS - 1:]
                parts += [hr, hi]
            h_ref[pl.ds(r0, SCAN_ROWS), :] = jnp.concatenate(parts, axis=1)
            return tuple(last)

        st = st_ref[0:1, :]
        init = tuple(st[:, c * LANES:(c + 1) * LANES] for c in range(SCAN_LANES // LANES))
        fin = lax.fori_loop(0, tt // SCAN_ROWS, tile, init)
        st_ref[0:1, :] = jnp.concatenate(fin, axis=1)
        y_ref[...] = _nn(h_ref[...].astype(BF16), c_ref[...])

    return pl.pallas_call(
        body, name="s5_fwd", grid=(SSM_BLOCKS, nt),
        in_specs=[pl.BlockSpec((tt, cin), lambda kb, i: (i, u_off // cin + kb)),
                  pl.BlockSpec((cin, SCAN_LANES), lambda kb, i: (kb, kb)),
                  pl.BlockSpec((SCAN_LANES, cin), lambda kb, i: (kb, kb)),
                  pl.BlockSpec((1, SCAN_LANES), lambda kb, i: (0, kb))],
        out_specs=[pl.BlockSpec((tt, SCAN_LANES), lambda kb, i: (i, kb)), pl.BlockSpec((tt, cin), lambda kb, i: (i, kb))],
        out_shape=[jax.ShapeDtypeStruct((s, w), F32), jax.ShapeDtypeStruct((s, bmat.shape[0]), F32)],
        scratch_shapes=[pltpu.VMEM((tt, SCAN_LANES), F32), pltpu.VMEM((SCAN_ROWS, SCAN_LANES), F32)],
        compiler_params=_params(dimension_semantics=("arbitrary", "arbitrary")),
    )(proj, bmat, cmat, lam)


def _s5_bwd(dy, h, proj, u_off, bmat, cmat, lam):
    s, w = h.shape
    tt = _pick(s, (512, 256, 128, 8))
    nt = s // tt
    cin = bmat.shape[0] // SSM_BLOCKS

    def body(dy_ref, h_ref, u_ref, b_ref, c_ref, lam_ref, du_ref, dlam_ref, db_ref, dc_ref, e_ref, a_ref, st_ref):
        @pl.when(pl.program_id(1) == 0)
        def _():
            st_ref[...] = jnp.zeros_like(st_ref)
            dlam_ref[...] = jnp.zeros_like(dlam_ref)
            db_ref[...] = jnp.zeros_like(db_ref)
            dc_ref[...] = jnp.zeros_like(dc_ref)

        tables = [_scan_tables(lr, li, reverse=True) for lr, li in _scan_chunks(lam_ref[...])]
        nch = len(tables)
        row = lax.broadcasted_iota(jnp.int32, (SCAN_ROWS, LANES), 0)
        e_ref[...] = _nt(dy_ref[...], c_ref[...])

        def tile(it, carry):
            r0 = pl.multiple_of((tt // SCAN_ROWS - 1 - it) * SCAN_ROWS, SCAN_ROWS)
            e_c = _scan_chunks(e_ref[pl.ds(r0, SCAN_ROWS), :])
            h_c = _scan_chunks(h_ref[pl.ds(r0, SCAN_ROWS), :])
            carry, parts = list(carry), []
            for c in range(nch):
                (yr, yi), (hr, hi) = e_c[c], h_c[c]
                levels, (pr, pi) = tables[c]
                for shift, (lr, li) in levels:
                    sr, si = pltpu.roll(yr, shift, 0), pltpu.roll(yi, shift, 0)
                    yr, yi = yr + lr * sr - li * si, yi + lr * si + li * sr
                nr, ni, dr, di = carry[4 * c:4 * c + 4]
                ar = yr + pr * nr - pi * ni
                ai = yi + pr * ni + pi * nr
                nxr = jnp.where(row == SCAN_ROWS - 1, nr, pltpu.roll(ar, SCAN_ROWS - 1, 0))
                nxi = jnp.where(row == SCAN_ROWS - 1, ni, pltpu.roll(ai, SCAN_ROWS - 1, 0))
                carry[4 * c:4 * c + 4] = [ar[0:1], ai[0:1], dr + nxr * hr + nxi * hi, di + nxi * hr - nxr * hi]
                parts += [ar, ai]
            a_ref[pl.ds(r0, SCAN_ROWS), :] = jnp.concatenate(parts, axis=1)
            return tuple(carry)

        st, dl = st_ref[0:1, :], dlam_ref[...]
        init = []
        for c in range(nch):
            lo = c * 2 * LANES
            init += [st[:, lo:lo + LANES], st[:, lo + LANES:lo + 2 * LANES],
                     dl[:, lo:lo + LANES], dl[:, lo + LANES:lo + 2 * LANES]]
        fin = lax.fori_loop(0, tt // SCAN_ROWS, tile, tuple(init))
        st_ref[0:1, :] = jnp.concatenate([fin[4 * c + q] for c in range(nch) for q in (0, 1)], axis=1)
        dlam_ref[...] = jnp.concatenate([fin[4 * c + q] for c in range(nch) for q in (2, 3)], axis=1)

        @pl.when(pl.program_id(1) == nt - 1)
        def _():
            dlam_ref[0:1, :] = jnp.sum(dlam_ref[...], axis=0, keepdims=True)

        adj = a_ref[...].astype(BF16)
        du_ref[...] = _nt(adj, b_ref[...])
        rows_first = (((0,), (0,)), ((), ()))
        db_ref[...] += lax.dot_general(u_ref[...].astype(BF16), adj, rows_first, preferred_element_type=F32)
        dc_ref[...] += lax.dot_general(h_ref[...].astype(BF16), dy_ref[...], rows_first, preferred_element_type=F32)

    def rev(width, col):
        return pl.BlockSpec((tt, width), lambda kb, i: (nt - 1 - i, col(kb)))

    return pl.pallas_call(
        body, name="s5_bwd", grid=(SSM_BLOCKS, nt),
        in_specs=[rev(cin, lambda kb: kb), rev(SCAN_LANES, lambda kb: kb), rev(cin, lambda kb: u_off // cin + kb),
                  pl.BlockSpec((cin, SCAN_LANES), lambda kb, i: (kb, kb)),
                  pl.BlockSpec((SCAN_LANES, cin), lambda kb, i: (kb, kb)),
                  pl.BlockSpec((1, SCAN_LANES), lambda kb, i: (0, kb))],
        out_specs=[rev(cin, lambda kb: kb), pl.BlockSpec((SCAN_ROWS, SCAN_LANES), lambda kb, i: (0, kb)),
                   pl.BlockSpec((cin, SCAN_LANES), lambda kb, i: (kb, 0)), pl.BlockSpec((SCAN_LANES, cin), lambda kb, i: (kb, 0))],
        out_shape=[jax.ShapeDtypeStruct((s, bmat.shape[0]), F32), jax.ShapeDtypeStruct((SCAN_ROWS, w), F32),
                   jax.ShapeDtypeStruct((bmat.shape[0], SCAN_LANES), F32), jax.ShapeDtypeStruct((w, cin), F32)],
        scratch_shapes=[pltpu.VMEM((tt, SCAN_LANES), F32), pltpu.VMEM((tt, SCAN_LANES), F32),
                        pltpu.VMEM((SCAN_ROWS, SCAN_LANES), F32)],
        compiler_params=_params(dimension_semantics=("arbitrary", "arbitrary")),
    )(dy, h, proj, bmat, cmat, lam)


def _ssm_params_fwd(a_re, a_im, log_dt, b_re, b_im):
    def body(ar_ref, ai_ref, ldt_ref, br_ref, bi_ref, lr_ref, li_ref, bbr_ref, bbi_ref):
        ar, ai, dt = ar_ref[...], ai_ref[...], jnp.exp(ldt_ref[...])
        mag = jnp.exp(ar * dt)
        lr, li = mag * jnp.cos(ai * dt), mag * jnp.sin(ai * dt)
        den = ar * ar + ai * ai
        cr = ((lr - 1.0) * ar + li * ai) / den
        ci = (li * ar - (lr - 1.0) * ai) / den
        br, bi = br_ref[...], bi_ref[...]
        lr_ref[...], li_ref[...] = lr, li
        bbr_ref[...] = cr * br - ci * bi
        bbi_ref[...] = cr * bi + ci * br

    n = a_re.shape[0]
    v1, v16 = jax.ShapeDtypeStruct((n, 1), F32), jax.ShapeDtypeStruct((n, SSM_GROUP), F32)
    return pl.pallas_call(body, name="ssm_params_fwd", out_shape=[v1, v1, v16, v16],
                          compiler_params=_params())(a_re, a_im, log_dt, b_re, b_im)


def _ssm_params_bwd(a_re, a_im, log_dt, b_re, b_im, g_lr, g_li, g_bbr, g_bbi):
    n = a_re.shape[0]

    def body(ar_ref, ai_ref, ldt_ref, br_ref, bi_ref, glr_ref, gli_ref, gbr_ref, gbi_ref,
             dar_ref, dai_ref, dldt_ref, dbr_ref, dbi_ref):
        ar, ai, dt = ar_ref[...], ai_ref[...], jnp.exp(ldt_ref[...])
        mag = jnp.exp(ar * dt)
        lr, li = mag * jnp.cos(ai * dt), mag * jnp.sin(ai * dt)
        den = ar * ar + ai * ai
        cr = ((lr - 1.0) * ar + li * ai) / den
        ci = (li * ar - (lr - 1.0) * ai) / den
        br, bi, gbr, gbi = br_ref[...], bi_ref[...], gbr_ref[...], gbi_ref[...]
        dbr_ref[...] = gbr * cr + gbi * ci
        dbi_ref[...] = gbi * cr - gbr * ci
        gcr = jnp.sum(gbr * br + gbi * bi, axis=1, keepdims=True)
        gci = jnp.sum(gbi * br - gbr * bi, axis=1, keepdims=True)
        ir, ii = ar / den, -ai / den
        glr = glr_ref[...] + gcr * ir + gci * ii
        gli = gli_ref[...] + gci * ir - gcr * ii
        qr, qi = cr * ir - ci * ii, cr * ii + ci * ir
        gar = -(gcr * qr + gci * qi)
        gai = -(gci * qr - gcr * qi)
        gxr = glr * lr + gli * li
        gxi = gli * lr - glr * li
        dar_ref[...] = gar + gxr * dt
        dai_ref[...] = gai + gxi * dt
        gdt = (gxr * ar + gxi * ai) * dt
        rowg = lax.broadcasted_iota(jnp.int32, (n, SSM_GROUPS), 0) // SSM_STATE
        colg = lax.broadcasted_iota(jnp.int32, (n, SSM_GROUPS), 1)
        dldt_ref[...] = jnp.sum(jnp.where(rowg == colg, gdt, 0.0), axis=0, keepdims=True)

    v1, v16 = jax.ShapeDtypeStruct((n, 1), F32), jax.ShapeDtypeStruct((n, SSM_GROUP), F32)
    return pl.pallas_call(body, name="ssm_params_bwd",
                          out_shape=[v1, v1, jax.ShapeDtypeStruct((1, SSM_GROUPS), F32), v16, v16],
                          compiler_params=_params())(a_re, a_im, log_dt, b_re, b_im, g_lr, g_li, g_bbr, g_bbi)


def _interleave(re, im, axis):
    shp = list(re.shape)
    new = shp[:axis] + [shp[axis] // LANES, LANES] + shp[axis + 1:]
    st = jnp.stack([re.reshape(new), im.reshape(new)], axis=axis + 1)
    return st.reshape(shp[:axis] + [2 * shp[axis]] + shp[axis + 1:])


def _deinterleave(v, axis):
    shp = list(v.shape)
    r = v.reshape(shp[:axis] + [shp[axis] // (2 * LANES), 2, LANES] + shp[axis + 1:])
    out = shp[:axis] + [shp[axis] // 2] + shp[axis + 1:]
    return (lax.index_in_dim(r, 0, axis + 1, keepdims=False).reshape(out),
            lax.index_in_dim(r, 1, axis + 1, keepdims=False).reshape(out))


def _b_matrix(bbr, bbi):
    eye = jnp.eye(SSM_GROUPS, dtype=F32)

    def blockdiag(v):
        x = v.reshape(SSM_GROUPS, SSM_STATE, SSM_GROUP).transpose(0, 2, 1)
        return (eye[:, None, :, None] * x[:, :, None, :]).reshape(SSM_GROUPS * SSM_GROUP, N_STATE)

    return _interleave(blockdiag(bbr), blockdiag(bbi), 1)


def _diag_blocks(v, rows, cols):
    per = SSM_GROUPS // SSM_BLOCKS
    return jnp.stack([v[g * rows:(g + 1) * rows, (g % per) * cols:(g % per + 1) * cols] for g in range(SSM_GROUPS)])


def _b_matrix_grad(d):
    def diag(v):
        return _diag_blocks(v, SSM_GROUP, SSM_STATE).transpose(0, 2, 1).reshape(N_STATE, SSM_GROUP)

    dr, di = _deinterleave(d, 1)
    return diag(dr), diag(di)


def _c_matrix(c_re, c_im):
    eye = jnp.eye(SSM_GROUPS, dtype=F32)

    def blockdiag(v):
        x = v.transpose(0, 2, 1)
        return (x[:, :, None, :] * eye[:, None, :, None]).reshape(N_STATE, SSM_GROUPS * SSM_GROUP)

    return _interleave(blockdiag(c_re), blockdiag(-c_im), 0)


def _c_matrix_grad(d):
    def diag(v):
        return _diag_blocks(v, SSM_STATE, SSM_GROUP).transpose(0, 2, 1)

    dr, di = _deinterleave(d, 0)
    return diag(dr), -diag(di)


def _row(v):
    return v.reshape(1, -1)


def _ssm_inputs(p):
    rows = lambda v: v.reshape(N_STATE, -1)
    ldt = jnp.repeat(p["ssm_log_dt"], SSM_STATE).reshape(N_STATE, 1)
    return rows(p["ssm_a_re"]), rows(p["ssm_a_im"]), ldt, rows(p["ssm_b_re"]), rows(p["ssm_b_im"])


def _lnmod(x, sc, sh):
    return _norm(x)[0] * (1.0 + sc) + sh


def _resid_ln(x, y, g, lg, lb):
    return _norm(ALPHA * x + (1.0 + g) * y)[0] * lg + lb


def _resid_ln_lnmod(x, y, g, lg, lb, sc, sh):
    xo = _resid_ln(x, y, g, lg, lb)
    return xo, _lnmod(xo, sc, sh)


def _layer_fwd(x, h1, mod, p, tag, next_mod):
    d = x.shape[1]
    sh_m, sc_m, g_m, sh_f, sc_f, g_f = [_row(mod[i]) for i in range(6)]
    nm = lambda s: f"{s}_{tag}"
    proj = _mm(nm("proj"), h1, p["w_in"], "nn")
    t = min(ATT_TILE, x.shape[0])
    qs, kb, vb, kt3, vt3 = _qkv_prep(proj, t)
    att, car = _attn_fwd(qs, kb, vt3, t)
    y_sb = _mm(nm("sb_up"), att, p["w_sb_up"], "nn", out_dtype=BF16)

    lam_r, lam_i, bbr, bbi = _ssm_params_fwd(*_ssm_inputs(p))
    lam = _interleave(lam_r.reshape(1, N_STATE), lam_i.reshape(1, N_STATE), 1)
    bmat = _b_matrix(bbr, bbi).astype(BF16)
    cmat = _c_matrix(p["ssm_c_re"], p["ssm_c_im"]).astype(BF16)
    hst, yc = _s5_fwd(proj, U_OFFSET, bmat, cmat, lam)

    def ssm_act(yc, u, dsk):
        y0 = yc + dsk * u
        return y0, _gelu(y0)

    y0, y1 = _rowwise(nm("ssm_act"), ssm_act, [(yc, 0, 512), (proj, 3, 512)], [_row(p["ssm_d"])], [(512, F32), (512, F32)])
    gl = _mm(nm("glu"), y1, p["w_glu"], "nn")
    y2 = _rowwise(nm("glu_act"), lambda y1, gl, b: y1 * jax.nn.sigmoid(gl + b), [(y1, 0, 512), (gl, 0, 512)],
                  [_row(p["b_glu"])], [(512, BF16)])
    y_ssm = _mm(nm("ssm_up"), y2, p["w_ssm_up"], "nn", out_dtype=BF16)

    def merge(gsb, gss, ysb, yss):
        return jax.nn.sigmoid(gsb) * ysb + jax.nn.sigmoid(gss) * yss

    merged = _rowwise(nm("merge"), merge, [(proj, 2, d), (proj, 3, d), (y_sb, 0, d), (y_ssm, 0, d)], [], [(d, BF16)])
    y = _mm(nm("out"), merged, p["w_out"], "nn")

    x1, h2 = _rowwise(nm("ln1"), _resid_ln_lnmod, [(x, 0, d), (y, 0, d)],
                      [g_m, _row(p["ln1_g"]), _row(p["ln1_b"]), sc_f, sh_f], [(d, F32), (d, BF16)])
    f = _mm(nm("ffn_in"), h2, p["w_ffn_in"], "nn", out_dtype=BF16)
    fh = f.shape[1] // 2
    act = _rowwise(nm("swiglu"), lambda g, u: g * jax.nn.sigmoid(g) * u, [(f, 0, fh), (f, 1, fh)], [], [(fh, BF16)])
    yf = _mm(nm("ffn_out"), act, p["w_ffn_out"], "nn")
    x2 = h1_next = None
    if next_mod is not None:
        x2, h1_next = _rowwise(nm("ln2"), _resid_ln_lnmod, [(x1, 0, d), (yf, 0, d)],
                               [g_f, _row(p["ln2_g"]), _row(p["ln2_b"]), next_mod[1], next_mod[0]], [(d, F32), (d, BF16)])
    saved = dict(x=x, h1=h1, proj=proj, qs=qs, kb=kb, vb=vb, kt3=kt3, car=car, att=att, y_sb=y_sb, lam=lam, bmat=bmat,
                 cmat=cmat, hst=hst, y0=y0, y1=y1, gl=gl, y2=y2, y_ssm=y_ssm, merged=merged, y=y, x1=x1, h2=h2, f=f,
                 act=act, yf=yf, t=t)
    return x2, h1_next, saved


def _resid_ln_bwd(x, y, dxo, g, lg):
    n, rstd = _norm(ALPHA * x + (1.0 + g) * y)
    dr = _norm_bwd(dxo * lg, n, rstd)
    return ALPHA * dr, (1.0 + g) * dr, _colsum(dxo * n), _colsum(dxo), _colsum(dr * y)


def _lnmod_bwd(x, dh, dxa, sc):
    n, rstd = _norm(x)
    return dxa + _norm_bwd(dh * (1.0 + sc), n, rstd), _colsum(dh * n), _colsum(dh)


def _lnmod_resid_ln_bwd(xo, dh, dxa, x, y, sc, g, lg):
    dxo, dsc, dsh = _lnmod_bwd(xo, dh, dxa, sc)
    dx, dy, dlg, dlb, dg = _resid_ln_bwd(x, y, dxo, g, lg)
    return dx, dy, dsc, dsh, dlg, dlb, dg


def _layer_bwd(dx1a, dyf, mod, p, sv, layer, depth, stacked):
    d = dx1a.shape[1]
    sh_m, sc_m, g_m, sh_f, sc_f, g_f = [_row(mod[i]) for i in range(6)]
    nm = lambda s: f"{s}_{layer}"
    grads = {}

    def weight_grad(n, a, b, **kw):
        grads[n] = _mm(nm("d" + n), a, b, "tn", out_dtype=BF16, into=(stacked.get(n), layer, depth), **kw)

    dact = _mm(nm("d_act"), dyf, p["w_ffn_out"], "nt", out_dtype=BF16)
    weight_grad("w_ffn_out", sv["act"], dyf)
    fh = sv["f"].shape[1] // 2

    def swiglu_bwd(g, u, da):
        sg = jax.nn.sigmoid(g)
        return jnp.concatenate([da * u * sg * (1.0 + g * (1.0 - sg)), da * g * sg], axis=1)

    df = _rowwise(nm("swiglu_bwd"), swiglu_bwd, [(sv["f"], 0, fh), (sv["f"], 1, fh), (dact, 0, fh)], [], [(2 * fh, BF16)])
    dh2 = _mm(nm("d_h2"), df, p["w_ffn_in"], "nt")
    weight_grad("w_ffn_in", sv["h2"], df)
    dxa, dy, dsc_f, dsh_f, grads["ln1_g"], grads["ln1_b"], dg_m = _rowwise(
        nm("ln1_bwd"), _lnmod_resid_ln_bwd, [(sv["x1"], 0, d), (dh2, 0, d), (dx1a, 0, d), (sv["x"], 0, d), (sv["y"], 0, d)],
        [sc_f, g_m, _row(p["ln1_g"])], [(d, F32), (d, BF16)], [d] * 5, tm=128)
    dmerged = _mm(nm("d_merged"), dy, p["w_out"], "nt", out_dtype=BF16)
    weight_grad("w_out", sv["merged"], dy)

    def merge_bwd(gsb, gss, ysb, yss, dm):
        s1, s2 = jax.nn.sigmoid(gsb), jax.nn.sigmoid(gss)
        return s1 * dm, s2 * dm, dm * ysb * s1 * (1.0 - s1), dm * yss * s2 * (1.0 - s2)

    dy_sb, dy_ssm, dg_sb, dg_ssm = _rowwise(
        nm("merge_bwd"), merge_bwd, [(sv["proj"], 2, d), (sv["proj"], 3, d), (sv["y_sb"], 0, d), (sv["y_ssm"], 0, d),
                                     (dmerged, 0, d)], [], [(d, BF16)] * 4)
    dy2 = _mm(nm("d_y2"), dy_ssm, p["w_ssm_up"], "nt")
    weight_grad("w_ssm_up", sv["y2"], dy_ssm)

    def glu_act_bwd(y1, gl, dy2, b):
        sg = jax.nn.sigmoid(gl + b)
        dgl = dy2 * y1 * sg * (1.0 - sg)
        return dy2 * sg, dgl, _colsum(dgl)

    dy1a, dgl, grads["b_glu"] = _rowwise(nm("glu_act_bwd"), glu_act_bwd, [(sv["y1"], 0, 512), (sv["gl"], 0, 512), (dy2, 0, 512)],
                                         [_row(p["b_glu"])], [(512, F32), (512, BF16)], [512])
    dy1b = _mm(nm("d_y1"), dgl, p["w_glu"], "nt")
    weight_grad("w_glu", sv["y1"], dgl)

    def ssm_act_bwd(y0, u, dy1a, dy1b, dsk):
        dy0 = (dy1a + dy1b) * _gelu_grad(y0)
        return dy0, dsk * dy0, _colsum(dy0 * u)

    dy0, du_a, grads["ssm_d"] = _rowwise(nm("ssm_act_bwd"), ssm_act_bwd,
                                         [(sv["y0"], 0, 512), (sv["proj"], 3, 512), (dy1a, 0, 512), (dy1b, 0, 512)],
                                         [_row(p["ssm_d"])], [(512, BF16), (512, F32)], [512])
    du_b, dlam, d_bmat, d_cmat = _s5_bwd(dy0, sv["hst"], sv["proj"], U_OFFSET, sv["bmat"], sv["cmat"], sv["lam"])
    grads["ssm_c_re"], grads["ssm_c_im"] = _c_matrix_grad(d_cmat)
    g_bbr, g_bbi = _b_matrix_grad(d_bmat)
    g_lr, g_li = _deinterleave(dlam[0:1], 1)
    da_re, da_im, dldt, db_re, db_im = _ssm_params_bwd(*_ssm_inputs(p), g_lr.reshape(N_STATE, 1), g_li.reshape(N_STATE, 1),
                                                       g_bbr, g_bbi)
    grads["ssm_a_re"] = da_re.reshape(SSM_GROUPS, SSM_STATE)
    grads["ssm_a_im"] = da_im.reshape(SSM_GROUPS, SSM_STATE)
    grads["ssm_log_dt"] = dldt.reshape(SSM_GROUPS)
    grads["ssm_b_re"] = db_re.reshape(SSM_GROUPS, SSM_STATE, SSM_GROUP)
    grads["ssm_b_im"] = db_im.reshape(SSM_GROUPS, SSM_STATE, SSM_GROUP)
    datt = _mm(nm("d_att"), dy_sb, p["w_sb_up"], "nt", out_dtype=BF16)
    weight_grad("w_sb_up", sv["att"], dy_sb)
    dqs, dk, dv = _attn_bwd(sv["qs"], datt, sv["kb"], sv["vb"], sv["kt3"], sv["car"], sv["t"])

    def dproj_cols(dqs, dk, dv, dua, dub, dgsb, dgss):
        return jnp.concatenate([dqs * (1.0 / math.sqrt(HEAD_DIM)), dk, dv, dua + dub, dgsb.astype(F32), dgss.astype(F32)],
                               axis=1)

    dproj = _rowwise(nm("dproj"), dproj_cols, [(dqs, 0, 512), (dk, 0, 512), (dv, 0, 512), (du_a, 0, 512), (du_b, 0, 512),
                                               (dg_sb, 0, d), (dg_ssm, 0, d)], [], [(2048 + 2 * d, BF16)])
    dh1 = _mm(nm("d_h1"), dproj, p["w_in"], "nt")
    weight_grad("w_in", sv["h1"], dproj)
    for k in ("ln1_g", "ln1_b", "ssm_d", "b_glu"):
        grads[k] = grads[k].reshape(-1)
    return dh1, dxa, grads, (dg_m, dsh_f, dsc_f)


def _local_step(x, target, mod, layer_w):
    depth, d = len(layer_w), x.shape[1]
    rows = lambda l: [_row(mod[l][i]) for i in range(6)]
    h1 = _rowwise("lnmod1_0", _lnmod, [(x, 0, d)], [rows(0)[1], rows(0)[0]], [(d, BF16)])
    xs, saved = x, []
    for l in range(depth):
        xs, h1, sv = _layer_fwd(xs, h1, mod[l], layer_w[l], str(l), rows(l + 1)[:2] if l + 1 < depth else None)
        saved.append(sv)

    def head_bwd(x1, yf, tgt, g, lg, lb):
        err = _resid_ln(x1, yf, g, lg, lb) - tgt
        return _resid_ln_bwd(x1, yf, err * (1.0 / d), g, lg) + (_colsum(err * err) * (0.5 / d),)

    def boundary_bwd(dh, dxa, x1, yf, sc, g, lg, lb):
        dxo, dsc, dsh = _lnmod_bwd(_resid_ln(x1, yf, g, lg, lb), dh, dxa, sc)
        return _resid_ln_bwd(x1, yf, dxo, g, lg) + (dsc, dsh)

    lgrads, sums, stacked = [None] * depth, [dict() for _ in range(depth)], {}
    last, p = saved[-1], layer_w[-1]
    dx1a, dyf, dlg, dlb, dg_f, loss_cols = _rowwise(
        "head_bwd", head_bwd, [(last["x1"], 0, d), (last["yf"], 0, d), (target, 0, d)],
        [rows(depth - 1)[5], _row(p["ln2_g"]), _row(p["ln2_b"])], [(d, F32), (d, BF16)], [d] * 4)
    for l in reversed(range(depth)):
        sums[l]["g_f"] = dg_f
        dh1, dxa, lgrads[l], (sums[l]["g_m"], sums[l]["sh_f"], sums[l]["sc_f"]) = _layer_bwd(
            dx1a, dyf, mod[l], layer_w[l], saved[l], l, depth, stacked)
        lgrads[l]["ln2_g"], lgrads[l]["ln2_b"] = dlg.reshape(-1), dlb.reshape(-1)
        stacked = {n: lgrads[l][n] for n in COL_SPLIT + ROW_SPLIT}
        if l > 0:
            prev, p = saved[l - 1], layer_w[l - 1]
            dx1a, dyf, dlg, dlb, dg_f, sums[l]["sc_m"], sums[l]["sh_m"] = _rowwise(
                f"boundary_bwd_{l}", boundary_bwd, [(dh1, 0, d), (dxa, 0, d), (prev["x1"], 0, d), (prev["yf"], 0, d)],
                [rows(l)[1], rows(l - 1)[5], _row(p["ln2_g"]), _row(p["ln2_b"])], [(d, F32), (d, BF16)], [d] * 5, tm=128)
        else:
            dx, sums[l]["sc_m"], sums[l]["sh_m"] = _rowwise("lnmod1_bwd", _lnmod_bwd, [(x, 0, d), (dh1, 0, d), (dxa, 0, d)],
                                                            [rows(0)[1]], [(d, F32)], [d, d])
    dmod = jnp.stack([jnp.concatenate([sums[l][k] for k in ("sh_m", "sc_m", "g_m", "sh_f", "sc_f", "g_f")], axis=0)
                      for l in range(depth)])
    return loss_cols, dx, dmod, lgrads, stacked


def _place():
    return lax.axis_index("x"), lax.axis_index("y"), lax.axis_index("c")


def _all_gather8(name, block):
    m_per, n = block.shape

    def body(x_ref, out_ref, send_sems, recv_sems, local_sem):
        x, y, c = _place()
        me, sibling = (x, y, c), (x, y, 1 - c)
        chips = [(1 - x, y), (x, 1 - y), (1 - x, 1 - y)]

        def rows(px, py, pc):
            return out_ref.at[pl.ds(pl.multiple_of((4 * px + 2 * py + pc) * m_per, 8), m_per), :]

        def copy(k, blk, to, src=None):
            return pltpu.make_async_remote_copy(src_ref=rows(*blk) if src is None else src, dst_ref=rows(*blk),
                                                send_sem=send_sems.at[k], recv_sem=recv_sems.at[k],
                                                device_id=to, device_id_type=MESH)

        mine = pltpu.make_async_copy(x_ref, rows(*me), local_sem)
        mine.start()
        first = [copy(0, me, sibling, src=x_ref)] + [copy(1 + j, me, (*chip, c), src=x_ref) for j, chip in enumerate(chips)]
        for cp in first:
            cp.start()
        passed = [copy(4 + j, (*chip, c), sibling) for j, chip in enumerate(chips)]
        for j, chip in enumerate(chips):
            copy(1 + j, (*chip, c), me).wait_recv()
            passed[j].start()
        copy(0, sibling, me).wait_recv()
        for j, chip in enumerate(chips):
            copy(4 + j, (*chip, 1 - c), me).wait_recv()
        for cp in first + passed:
            cp.wait_send()
        mine.wait()

    return pl.pallas_call(
        body, name=name, out_shape=jax.ShapeDtypeStruct((8 * m_per, n), block.dtype),
        in_specs=[pl.BlockSpec(memory_space=pltpu.VMEM)], out_specs=pl.BlockSpec(memory_space=pltpu.VMEM),
        scratch_shapes=[pltpu.SemaphoreType.DMA((7,)), pltpu.SemaphoreType.DMA((7,)), pltpu.SemaphoreType.DMA],
        compiler_params=_params(),
    )(block)


def _other_chips(x, y):
    return [(1 - x, y), (x, 1 - y), (1 - x, 1 - y)]


def _gather_weights(whole, by_rows):
    n = len(whole)

    def body(*refs):
        dst = refs[n:2 * n]
        ici_send, ici_recv, d2d_send, d2d_recv = refs[2 * n:]
        x, y, c = _place()
        chips = _other_chips(x, y)

        def part(ref, k, px, py, pc):
            _, r, cols = whole[k].shape
            q = 2 * px + py
            if by_rows[k]:
                return ref[k].at[:, pl.ds(pl.multiple_of((2 * q + pc) * (r // 8), 16), r // 8), :]
            return ref[k].at[:, pl.ds(pl.multiple_of(pc * (r // 2), 16), r // 2),
                             pl.ds(pl.multiple_of(q * (cols // 4), LANES), cols // 4)]

        def ici(k, j, px, py, to):
            return pltpu.make_async_remote_copy(src_ref=part(dst, k, px, py, c), dst_ref=part(dst, k, px, py, c),
                                                send_sem=ici_send.at[k, j], recv_sem=ici_recv.at[k, j],
                                                device_id=(*to, c), device_id_type=MESH)

        def d2d(k, j, px, py, pc):
            return pltpu.make_async_remote_copy(src_ref=part(dst, k, px, py, pc), dst_ref=part(dst, k, px, py, pc),
                                                send_sem=d2d_send.at[k, j], recv_sem=d2d_recv.at[k, j],
                                                device_id=(x, y, 1 - c), device_id_type=MESH)

        for k in range(n):
            for j, chip in enumerate(chips):
                ici(k, j, x, y, chip).start()
        for k in range(n):
            for j, chip in enumerate(chips):
                ici(k, j, *chip, chip).wait_recv()
                d2d(k, j, *chip, c).start()
        for k in range(n):
            for j, chip in enumerate(chips):
                d2d(k, j, *chip, 1 - c).wait_recv()
        for k in range(n):
            for j, chip in enumerate(chips):
                ici(k, j, x, y, chip).wait_send()
                d2d(k, j, *chip, c).wait_send()

    any_spec = pl.BlockSpec(memory_space=pl.ANY)
    return pl.pallas_call(
        body, name="gather_weights", in_specs=[any_spec] * n, out_specs=[any_spec] * n,
        out_shape=[jax.ShapeDtypeStruct(a.shape, a.dtype) for a in whole], input_output_aliases={k: k for k in range(n)},
        scratch_shapes=[pltpu.SemaphoreType.DMA((n, 3))] * 4,
        compiler_params=_params(),
    )(*whole)


def _part_shape(shape, by_rows):
    l, r, c = shape
    return (l, r // 8, c) if by_rows else (l, r // 2, c // 4)


def _pair_exchange(grads, by_rows):
    n = len(grads)

    def body(*refs):
        src, dst = refs[:n], refs[n:2 * n]
        send_sems, recv_sems = refs[2 * n:]
        x, y, c = _place()

        def window(k, q, pc):
            _, hr, hc = _part_shape(grads[k].shape, by_rows[k])
            if by_rows[k]:
                return src[k].at[:, pl.ds(pl.multiple_of((2 * q + pc) * hr, 16), hr), :]
            return src[k].at[:, pl.ds(pl.multiple_of(pc * hr, 16), hr), pl.ds(q * hc, hc)]

        def copy(k, q, pc):
            return pltpu.make_async_remote_copy(src_ref=window(k, q, pc), dst_ref=dst[k].at[q], send_sem=send_sems.at[k, q],
                                                recv_sem=recv_sems.at[k, q], device_id=(x, y, 1 - c), device_id_type=MESH)

        for k in range(n):
            for q in range(4):
                copy(k, q, 1 - c).start()
        for k in range(n):
            for q in range(4):
                copy(k, q, c).wait_recv()
        for k in range(n):
            for q in range(4):
                copy(k, q, 1 - c).wait_send()

    any_spec = pl.BlockSpec(memory_space=pl.ANY)
    return pl.pallas_call(
        body, name="pair_exchange", in_specs=[any_spec] * n, out_specs=[any_spec] * n,
        out_shape=[jax.ShapeDtypeStruct((4, *_part_shape(g.shape, rows)), g.dtype) for g, rows in zip(grads, by_rows)],
        scratch_shapes=[pltpu.SemaphoreType.DMA((n, 4)), pltpu.SemaphoreType.DMA((n, 4))],
        compiler_params=_params(),
    )(*grads)


def _pair_sum(name, g, theirs, by_rows, c, chip):
    _, l, hr, hc = theirs.shape
    tr = _pick(hr, (256, 176, 128, 64, 32))

    def body(s_ref, g_ref, t_ref, p_ref, own_ref):
        v = (g_ref[...].astype(F32) + t_ref[0].astype(F32)).astype(BF16)
        p_ref[0] = v

        @pl.when(pl.program_id(2) == s_ref[1])
        def _():
            own_ref[0] = v

    if by_rows:
        g_spec = pl.BlockSpec((1, tr, hc), lambda li, i, q, s: (li, (2 * q + s[0]) * (hr // tr) + i, 0))
    else:
        g_spec = pl.BlockSpec((1, tr, hc), lambda li, i, q, s: (li, s[0] * (hr // tr) + i, q))
    slot = pl.BlockSpec((1, 1, tr, hc), lambda li, i, q, s: (q, li, i, 0))
    grid_spec = pltpu.PrefetchScalarGridSpec(
        num_scalar_prefetch=1, grid=(l, hr // tr, 4), in_specs=[g_spec, slot],
        out_specs=[slot, pl.BlockSpec((1, 1, tr, hc), lambda li, i, q, s: (s[1], li, i, 0))])
    return pl.pallas_call(
        body, name=name, grid_spec=grid_spec, out_shape=[jax.ShapeDtypeStruct(theirs.shape, BF16)] * 2,
        compiler_params=_params(dimension_semantics=("arbitrary", "arbitrary", "arbitrary")),
    )(jnp.stack([c, chip]).astype(jnp.int32), g, theirs)


def _chip_scatter(sums, landing):
    n = len(sums)

    def body(*refs):
        src, dst = refs[:n], refs[2 * n:3 * n]
        send_sems, recv_sems = refs[3 * n:]
        x, y, c = _place()
        mine = 2 * x + y

        def copy(k, j, src_slot, dst_slot, to):
            return pltpu.make_async_remote_copy(src_ref=src[k].at[src_slot], dst_ref=dst[k].at[dst_slot],
                                                send_sem=send_sems.at[k, j], recv_sem=recv_sems.at[k, j],
                                                device_id=(*to, c), device_id_type=MESH)

        chips = _other_chips(x, y)
        for k in range(n):
            for j, (px, py) in enumerate(chips):
                copy(k, j, 2 * px + py, mine, (px, py)).start()
        for k in range(n):
            for j, (px, py) in enumerate(chips):
                copy(k, j, mine, 2 * px + py, (px, py)).wait_recv()
        for k in range(n):
            for j, (px, py) in enumerate(chips):
                copy(k, j, 2 * px + py, mine, (px, py)).wait_send()

    any_spec = pl.BlockSpec(memory_space=pl.ANY)
    return pl.pallas_call(
        body, name="chip_scatter", in_specs=[any_spec] * (2 * n), out_specs=[any_spec] * n,
        out_shape=[jax.ShapeDtypeStruct(a.shape, a.dtype) for a in landing],
        input_output_aliases={n + k: k for k in range(n)},
        scratch_shapes=[pltpu.SemaphoreType.DMA((n, 3)), pltpu.SemaphoreType.DMA((n, 3))],
        compiler_params=_params(),
    )(*sums, *landing)


def _sum_slots(name, parts, half=None):
    slots, l, r, c = parts.shape
    tr = _pick(r, (256, 176, 128, 64, 32, 8))

    def body(*refs):
        p_ref, o_ref = refs[-2:]
        acc = p_ref[0].astype(F32)
        for i in range(1, slots):
            acc = acc + p_ref[i].astype(F32)
        o_ref[...] = acc

    if half is None:
        return pl.pallas_call(
            body, name=name, grid=(l, r // tr), in_specs=[pl.BlockSpec((slots, 1, tr, c), lambda li, i: (0, li, i, 0))],
            out_specs=pl.BlockSpec((1, tr, c), lambda li, i: (li, i, 0)), out_shape=jax.ShapeDtypeStruct((l, r, c), F32),
            compiler_params=_params(dimension_semantics=("arbitrary", "arbitrary")),
        )(parts)
    grid_spec = pltpu.PrefetchScalarGridSpec(
        num_scalar_prefetch=1, grid=(l, r // tr),
        in_specs=[pl.BlockSpec((slots, 1, tr, c), lambda li, i, h: (0, li, i, 0))],
        out_specs=pl.BlockSpec((1, tr, c), lambda li, i, h: (li, h[0] * (r // tr) + i, 0)))
    return pl.pallas_call(
        body, name=name, grid_spec=grid_spec, out_shape=jax.ShapeDtypeStruct((l, 2 * r, c), F32),
        compiler_params=_params(dimension_semantics=("arbitrary", "arbitrary")),
    )(jnp.reshape(half, (1,)).astype(jnp.int32), parts)


def _swap_halves(blocks):
    n = len(blocks)

    def body(*refs):
        src, dst = refs[:n], refs[n:2 * n]
        send_sems, recv_sems = refs[2 * n:]
        x, y, c = _place()

        def half(ref, k, pc):
            r = blocks[k].shape[1] // 2
            return ref[k].at[:, pl.ds(pl.multiple_of(pc * r, 8), r), :]

        def copy(k, pc):
            return pltpu.make_async_remote_copy(src_ref=half(src, k, pc), dst_ref=half(dst, k, pc), send_sem=send_sems.at[k],
                                                recv_sem=recv_sems.at[k], device_id=(x, y, 1 - c), device_id_type=MESH)

        for k in range(n):
            copy(k, c).start()
        for k in range(n):
            copy(k, 1 - c).wait_recv()
        for k in range(n):
            copy(k, c).wait_send()

    any_spec = pl.BlockSpec(memory_space=pl.ANY)
    return pl.pallas_call(
        body, name="swap_halves", in_specs=[any_spec] * n, out_specs=[any_spec] * n,
        out_shape=[jax.ShapeDtypeStruct(b.shape, b.dtype) for b in blocks], input_output_aliases={k: k for k in range(n)},
        scratch_shapes=[pltpu.SemaphoreType.DMA((n,)), pltpu.SemaphoreType.DMA((n,))],
        compiler_params=_params(),
    )(*blocks)


def _adamw(name, w, g, m, v):
    shape = w.shape
    cols = shape[-1] if w.ndim > 1 and shape[-1] % LANES == 0 else w.size if w.size % LANES else LANES
    flat = lambda a: a.reshape(-1, cols)
    rows = w.size // cols
    tr = _pick(rows, [r for r in (512, 256, 128, 64, 32, 16, 8) if r * cols <= 256 * 1024]) if rows % 8 == 0 else rows

    def body(w_ref, g_ref, m_ref, v_ref, go_ref, d_ref, nm_ref, nv_ref):
        gg = g_ref[...]
        go_ref[...] = gg
        nm = ADAM_B1 * m_ref[...] + (1.0 - ADAM_B1) * gg
        nv = ADAM_B2 * v_ref[...] + (1.0 - ADAM_B2) * (gg * gg)
        m_hat = nm / (1.0 - ADAM_B1 ** ADAM_STEP)
        v_hat = nv / (1.0 - ADAM_B2 ** ADAM_STEP)
        d_ref[...] = -ADAM_LR * (m_hat / (jnp.sqrt(v_hat) + ADAM_EPS) + ADAM_WD * w_ref[...])
        nm_ref[...] = nm
        nv_ref[...] = nv

    spec = pl.BlockSpec((tr, cols), lambda i: (i, 0))
    out = pl.pallas_call(
        body, name=name, grid=(rows // tr,), in_specs=[spec] * 4, out_specs=[spec] * 4,
        out_shape=[jax.ShapeDtypeStruct((rows, cols), F32)] * 4,
        compiler_params=_params(dimension_semantics=("arbitrary",)),
    )(flat(w), flat(g), flat(m), flat(v))
    return tuple(o.reshape(shape) for o in out)


WEIGHTS = ["w_ada", "b_ada", "w_in", "w_sb_up", "ssm_a_re", "ssm_a_im", "ssm_log_dt", "ssm_b_re", "ssm_b_im", "ssm_c_re",
           "ssm_c_im", "ssm_d", "w_glu", "b_glu", "w_ssm_up", "w_out", "ln1_g", "ln1_b", "w_ffn_in", "w_ffn_out", "ln2_g",
           "ln2_b"]
COL_SPLIT = ["w_in", "w_sb_up", "w_ssm_up", "w_ffn_in"]
ROW_SPLIT = ["w_glu", "w_out", "w_ffn_out"]
SMALL = ["ssm_a_re", "ssm_a_im", "ssm_log_dt", "ssm_b_re", "ssm_b_im", "ssm_c_re", "ssm_c_im", "ssm_d", "b_glu", "ln1_g",
         "ln1_b", "ln2_g", "ln2_b"]
SLAB_COLS = 1024


def _cast_into_whole(name, w, by_rows, chip):
    l, r, cols = w.shape
    tr = _pick(r, (512, 256, 128, 64, 16))

    def body(q_ref, w_ref, o_ref):
        o_ref[...] = w_ref[...].astype(BF16)

    if by_rows:
        out_map, shape = (lambda li, i, q: (li, q[0] * (r // tr) + i, 0)), (l, 4 * r, cols)
    else:
        out_map, shape = (lambda li, i, q: (li, i, q[0])), (l, r, 4 * cols)
    grid_spec = pltpu.PrefetchScalarGridSpec(
        num_scalar_prefetch=1, grid=(l, r // tr), in_specs=[pl.BlockSpec((1, tr, cols), lambda li, i, q: (li, i, 0))],
        out_specs=pl.BlockSpec((1, tr, cols), out_map))
    return pl.pallas_call(body, name=name, grid_spec=grid_spec, out_shape=jax.ShapeDtypeStruct(shape, BF16),
                          compiler_params=_params(dimension_semantics=("arbitrary", "arbitrary")),
                          )(jnp.reshape(chip, (1,)).astype(jnp.int32), w)


def _silu_rows(name, c):
    def body(c_ref, o_ref):
        v = c_ref[...]
        o_ref[...] = v * jax.nn.sigmoid(v)

    return pl.pallas_call(body, name=name, out_shape=jax.ShapeDtypeStruct(c.shape, F32), compiler_params=_params())(c)


def _pad_rows(v, mult=8):
    flat = v.reshape(-1)
    per = mult * SLAB_COLS
    total = -(-flat.size // per) * per
    return jnp.pad(flat, (0, total - flat.size)).reshape(-1, SLAB_COLS)


def kernel(x, c, w_ada, b_ada, w_in, w_sb_up, ssm_a_re, ssm_a_im, ssm_log_dt, ssm_b_re, ssm_b_im, ssm_c_re, ssm_c_im, ssm_d, w_glu, b_glu, w_ssm_up, w_out, ln1_g, ln1_b, w_ffn_in, w_ffn_out, ln2_g, ln2_b, loss_target, m_w_ada, m_b_ada, m_w_in, m_w_sb_up, m_ssm_a_re, m_ssm_a_im, m_ssm_log_dt, m_ssm_b_re, m_ssm_b_im, m_ssm_c_re, m_ssm_c_im, m_ssm_d, m_w_glu, m_b_glu, m_w_ssm_up, m_w_out, m_ln1_g, m_ln1_b, m_w_ffn_in, m_w_ffn_out, m_ln2_g, m_ln2_b, v_w_ada, v_b_ada, v_w_in, v_w_sb_up, v_ssm_a_re, v_ssm_a_im, v_ssm_log_dt, v_ssm_b_re, v_ssm_b_im, v_ssm_c_re, v_ssm_c_im, v_ssm_d, v_w_glu, v_b_glu, v_w_ssm_up, v_w_out, v_ln1_g, v_ln1_b, v_w_ffn_in, v_w_ffn_out, v_ln2_g, v_ln2_b):
    args = dict(locals())
    w = {n: args[n] for n in WEIGHTS}
    mom = {n: args["m_" + n] for n in WEIGHTS}
    var = {n: args["v_" + n] for n in WEIGHTS}
    depth, d = w_ada.shape[0], x.shape[-1]
    xi, yi, ci = _place()
    me, chip = 4 * xi + 2 * yi + ci, 2 * xi + yi
    ada_cols = w_ada.shape[2]

    big = COL_SPLIT + ROW_SPLIT
    by_rows = [n in ROW_SPLIT for n in big]
    full = dict(zip(big, _gather_weights([_cast_into_whole(f"cast_{n}", w[n], n in ROW_SPLIT, chip) for n in big], by_rows)))

    c_all = _all_gather8("gather_c", jnp.pad(c, ((0, 7), (0, 0))))[::8]
    c_act = _silu_rows("silu_c", c_all)
    b_cols = lax.dynamic_slice_in_dim(b_ada, chip * ada_cols, ada_cols, axis=1)
    mod_part = jnp.concatenate([_small_mm(f"mod_{l}", c_act, w_ada[l], "nn") + b_cols[l][None] for l in range(depth)], axis=0)
    mod_all = _all_gather8("gather_mod", mod_part).reshape(4, 2, depth, 8, ada_cols)[:, 0]
    mod_mine = lax.dynamic_index_in_dim(mod_all, me, axis=2, keepdims=False)
    mod = mod_mine.transpose(1, 0, 2).reshape(depth, 6, d)

    layer_w = [{**{n: full[n][l] for n in big}, **{n: w[n][l] for n in SMALL}} for l in range(depth)]
    loss_cols, dx, dmods, lgrads, stacked = _local_step(x[0], loss_target[0], mod, layer_w)
    loss = lax.psum(jnp.sum(loss_cols), ("x", "y", "c"))
    grad_x = dx[None]

    theirs = _pair_exchange([stacked[n] for n in big], by_rows)
    pairs = [_pair_sum(f"pair_{n}", stacked[n], t, n in ROW_SPLIT, ci, chip) for n, t in zip(big, theirs)]
    landed = _chip_scatter([p[0] for p in pairs], [p[1] for p in pairs])
    halves = [_sum_slots(f"sum_{n}", p, half=ci) for n, p in zip(big, landed)]
    grad = dict(zip(big, _swap_halves(halves)))

    pieces = [dmods] + [jnp.stack([lgrads[l][n] for l in range(depth)]) for n in SMALL]
    slab = jnp.concatenate([_pad_rows(p) for p in pieces], axis=0)
    slabs = _all_gather8("gather_small", slab).reshape(8, 1, *slab.shape)
    total = _sum_slots("sum_small", slabs)[0]
    row = _pad_rows(pieces[0]).shape[0]
    for n, p in zip(SMALL, pieces[1:]):
        rows = _pad_rows(p).shape[0]
        grad[n] = total[row:row + rows].reshape(-1)[:p.size].reshape(p.shape)
        row += rows
    dmod_rows = _pad_rows(pieces[0]).shape[0]
    dmod_all = slabs[:, 0, :dmod_rows].reshape(8, -1)[:, :depth * 6 * d].reshape(8, depth, 4, ada_cols)
    dmod_cols = lax.dynamic_index_in_dim(dmod_all, chip, axis=2, keepdims=False)
    grad["w_ada"] = jnp.stack([_small_mm(f"dw_ada_{l}", c_act, dmod_cols[:, l], "tn") for l in range(depth)])
    dmod_sum = _sum_slots("sum_dmod", slabs[:, :, :dmod_rows])[0]
    grad["b_ada"] = dmod_sum.reshape(-1)[:depth * 6 * d].reshape(depth, 6 * d)

    delta, new_m, new_v = {}, {}, {}
    for n in WEIGHTS:
        grad[n], delta[n], new_m[n], new_v[n] = _adamw(f"adamw_{n}", w[n], grad[n], mom[n], var[n])
    return (loss, grad_x, *[grad[n] for n in WEIGHTS], *[delta[n] for n in WEIGHTS], *[new_m[n] for n in WEIGHTS],
            *[new_v[n] for n in WEIGHTS])
```

```python
import functools
import math

import jax
import jax.numpy as jnp
from jax import lax
from jax.experimental import pallas as pl
from jax.experimental.pallas import tpu as pltpu

F32 = jnp.float32
BF16 = jnp.bfloat16
MESH = pl.DeviceIdType.MESH

LANES = 128
HEAD_DIM = 64
ATT_TILE = 256
SSM_GROUPS, SSM_STATE, SSM_GROUP = 32, 64, 16
N_STATE = SSM_GROUPS * SSM_STATE
SSM_BLOCKS = SSM_GROUPS * SSM_GROUP // LANES
U_OFFSET = 3 * 512
LN_EPS = 1e-5
DEPTH = 2
ALPHA = (2 * DEPTH) ** 0.25
ADAM_LR, ADAM_B1, ADAM_B2, ADAM_EPS, ADAM_WD, ADAM_STEP = 0.001, 0.9, 0.999, 1e-08, 0.01, 10
VMEM_LIMIT = 56 * 1024 * 1024
GELU_K = math.sqrt(2.0 / math.pi)
GELU_C = 0.044715


def _params(**kw):
    return pltpu.CompilerParams(vmem_limit_bytes=VMEM_LIMIT, **kw)


def _pick(n, prefs):
    for p in prefs:
        if n % p == 0:
            return p
    return n


def _rowwise(name, fn, rows, vecs, outs, sums=(), tm=None):
    s = rows[0][0].shape[0]
    tm = tm or _pick(s, (256, 128, 64, 8))
    nin, no, ns = len(rows) + len(vecs), len(outs), len(sums)

    def body(*refs):
        res = fn(*[r[...].astype(F32) for r in refs[:nin]])
        res = res if isinstance(res, tuple) else (res,)
        for r, v in zip(refs[nin:nin + no], res[:no]):
            r[...] = v.astype(r.dtype)
        if ns:
            @pl.when(pl.program_id(0) == 0)
            def _():
                for r in refs[nin + no:]:
                    r[...] = jnp.zeros_like(r)
            for r, v in zip(refs[nin + no:], res[no:]):
                r[...] += v

    in_specs = [pl.BlockSpec((tm, w), lambda i, cb=cb: (i, cb)) for _, cb, w in rows]
    in_specs += [pl.BlockSpec(v.shape, lambda i: (0, 0)) for v in vecs]
    out_specs = [pl.BlockSpec((tm, w), lambda i: (i, 0)) for w, _ in outs]
    out_specs += [pl.BlockSpec((1, w), lambda i: (0, 0)) for w in sums]
    out_shape = [jax.ShapeDtypeStruct((s, w), dt) for w, dt in outs]
    out_shape += [jax.ShapeDtypeStruct((1, w), F32) for w in sums]
    res = pl.pallas_call(
        body, name=name, grid=(s // tm,), in_specs=in_specs, out_specs=out_specs, out_shape=out_shape,
        compiler_params=_params(dimension_semantics=("arbitrary",)),
    )(*[a for a, _, _ in rows], *vecs)
    return res[0] if len(res) == 1 else tuple(res)


MM_TILES = (1408, 1024, 512, 256, 128)


def _mm(name, a, b, mode, out_dtype=F32, into=None):
    if mode == "nn":
        m, k, n = a.shape[0], a.shape[1], b.shape[1]
    elif mode == "nt":
        m, k, n = a.shape[0], a.shape[1], b.shape[0]
    else:
        k, m, n = a.shape[0], a.shape[1], b.shape[1]
    tm = _pick(m, MM_TILES if mode == "tn" else MM_TILES[1:])
    tn = _pick(n, MM_TILES)
    tk = _pick(k, MM_TILES)
    nk = k // tk
    dims = {"nn": ((1,), (0,)), "nt": ((1,), (1,)), "tn": ((0,), (0,))}[mode]

    def body(a_ref, b_ref, *rest):
        o_ref = rest[-2] if nk > 1 else rest[-1]
        prod = lax.dot_general(a_ref[...].astype(BF16), b_ref[...].astype(BF16), (dims, ((), ())),
                               preferred_element_type=F32)
        if nk == 1:
            o_ref[...] = prod.astype(o_ref.dtype)
            return
        acc_ref = rest[-1]
        kk = pl.program_id(2)

        @pl.when(kk == 0)
        def _():
            acc_ref[...] = prod

        @pl.when(kk > 0)
        def _():
            acc_ref[...] += prod

        @pl.when(kk == nk - 1)
        def _():
            o_ref[...] = acc_ref[...].astype(o_ref.dtype)

    if mode == "tn":
        a_spec = pl.BlockSpec((tk, tm), lambda i, j, kk: (kk, i))
    else:
        a_spec = pl.BlockSpec((tm, tk), lambda i, j, kk: (i, kk))
    if mode == "nt":
        b_spec = pl.BlockSpec((tn, tk), lambda i, j, kk: (j, kk))
    else:
        b_spec = pl.BlockSpec((tk, tn), lambda i, j, kk: (kk, j))
    in_specs, operands, aliases = [a_spec, b_spec], [a, b], {}
    if into is None:
        out_spec = pl.BlockSpec((tm, tn), lambda i, j, kk: (i, j))
        out_shape = jax.ShapeDtypeStruct((m, n), out_dtype)
    else:
        buf, slab, count = into
        out_spec = pl.BlockSpec((None, tm, tn), lambda i, j, kk: (slab, i, j))
        out_shape = jax.ShapeDtypeStruct((count, m, n), out_dtype)
        if buf is not None:
            in_specs.append(pl.BlockSpec(memory_space=pl.ANY))
            operands.append(buf)
            aliases = {2: 0}
    return pl.pallas_call(
        body, name=name, grid=(m // tm, n // tn, nk), in_specs=in_specs, out_specs=out_spec, out_shape=out_shape,
        scratch_shapes=[pltpu.VMEM((tm, tn), F32)] if nk > 1 else [], input_output_aliases=aliases,
        compiler_params=_params(dimension_semantics=("arbitrary", "arbitrary", "arbitrary")),
    )(*operands)


def _small_mm(name, a, b, mode):
    dims = {"nn": ((1,), (0,)), "tn": ((0,), (0,))}[mode]
    m = a.shape[0] if mode == "nn" else a.shape[1]

    def body(a_ref, b_ref, o_ref):
        o_ref[...] = lax.dot_general(a_ref[...], b_ref[...], (dims, ((), ())), precision=lax.Precision.HIGHEST,
                                     preferred_element_type=F32)

    return pl.pallas_call(body, name=name, out_shape=jax.ShapeDtypeStruct((m, b.shape[1]), F32),
                          compiler_params=_params())(a, b)


def _norm(x):
    mu = jnp.mean(x, axis=-1, keepdims=True)
    xc = x - mu
    rstd = lax.rsqrt(jnp.mean(xc * xc, axis=-1, keepdims=True) + LN_EPS)
    return xc * rstd, rstd


def _norm_bwd(dn, n, rstd):
    return rstd * (dn - jnp.mean(dn, axis=-1, keepdims=True) - n * jnp.mean(dn * n, axis=-1, keepdims=True))


def _colsum(v):
    return jnp.sum(v, axis=0, keepdims=True)


def _gelu(x):
    return 0.5 * x * (1.0 + jnp.tanh(GELU_K * (x + GELU_C * x * x * x)))


def _gelu_grad(x):
    t = jnp.tanh(GELU_K * (x + GELU_C * x * x * x))
    return 0.5 * (1.0 + t) + 0.5 * x * (1.0 - t * t) * GELU_K * (1.0 + 3.0 * GELU_C * x * x)


def _log_sigmoid_parts(z):
    lb = jnp.minimum(z, 0.0) - jnp.log(1.0 + jnp.exp(-jnp.abs(z)))
    return lb, lb - z


def _qkv_prep(proj, t):
    s = proj.shape[0]
    nb, nhp = s // t, 512 // LANES

    def body(q_ref, k_ref, v_ref, qs_ref, kb_ref, vb_ref, kt_ref, vt_ref):
        qs_ref[...] = (q_ref[...] * (1.0 / math.sqrt(HEAD_DIM))).astype(BF16)
        k, v = k_ref[...], v_ref[...]
        kb_ref[...] = k.astype(BF16)
        vb_ref[...] = v.astype(BF16)
        for hp in range(nhp):
            kt_ref[hp, 0] = k[:, hp * LANES:(hp + 1) * LANES].T.astype(BF16)
            vt_ref[hp, 0] = v[:, hp * LANES:(hp + 1) * LANES].T.astype(BF16)

    col = lambda cb: pl.BlockSpec((t, 512), lambda i, cb=cb: (i, cb))
    row_out = pl.BlockSpec((t, 512), lambda i: (i, 0))
    t_out = pl.BlockSpec((nhp, 1, LANES, t), lambda i: (0, i, 0, 0))
    return pl.pallas_call(
        body, name="qkv_prep", grid=(nb,), in_specs=[col(0), col(1), col(2)],
        out_specs=[row_out, row_out, row_out, t_out, t_out],
        out_shape=[jax.ShapeDtypeStruct((s, 512), BF16)] * 3 + [jax.ShapeDtypeStruct((nhp, nb, LANES, t), BF16)] * 2,
        compiler_params=_params(dimension_semantics=("arbitrary",)),
    )(proj, proj, proj)


def _tile_masks(t):
    row = lax.broadcasted_iota(jnp.int32, (t, t), 0)
    col = lax.broadcasted_iota(jnp.int32, (t, t), 1)
    return row, col


DEAD_LOG_WEIGHT = -110.0


def _walk_down(i, tiles, state, alive):
    st = lax.cond(i == 0, lambda s_: tiles([i], s_, [True]), lambda s_: tiles([i, i - 1], s_, [True, False]), state)
    n = jnp.maximum(i - 1, 0)

    def pair(c):
        return c[0] + 1, tiles([i - 2 - 2 * c[0], i - 3 - 2 * c[0]], c[1], [False, False])

    p, st = lax.while_loop(lambda c: (c[0] < n // 2) & alive(c[1]), pair, (jnp.int32(0), st))
    return lax.cond((n % 2 == 1) & (p == n // 2) & alive(st), lambda s_: tiles([0], s_, [False]), lambda s_: s_, st)


def _walk_up(i, first, tiles, state):
    n = jnp.maximum(i - 1 - first, 0)
    st = lax.fori_loop(0, n // 2, lambda p, s_: tiles([first + 2 * p, first + 2 * p + 1], s_, [False, False]), state)
    st = lax.cond(n % 2 == 1, lambda s_: tiles([i - 2], s_, [False]), lambda s_: s_, st)
    return lax.cond(i == 0, lambda s_: tiles([i], s_, [True]), lambda s_: tiles([i - 1, i], s_, [False, True]), st)


def _nt(a, b):
    return lax.dot_general(a, b, (((1,), (1,)), ((), ())), preferred_element_type=F32)


def _nn(a, b):
    return jnp.dot(a, b, preferred_element_type=F32)


def _attn_fwd(qs, k, vt3, t):
    s = qs.shape[0]
    nb, nhp = s // t, qs.shape[1] // LANES

    def body(q_ref, k_ref, vt_ref, o_ref, car_ref):
        i = pl.program_id(1)
        q2 = q_ref[...]
        lane_q = lax.broadcasted_iota(jnp.int32, q2.shape, 1)
        row, col = _tile_masks(t)
        later = (col > row).astype(BF16)
        valid = row < col
        orow = lax.broadcasted_iota(jnp.int32, (LANES, t), 0)
        car_ref[...] = jnp.full(car_ref.shape, 2.0 * DEAD_LOG_WEIGHT, F32)
        qh = [jnp.where((lane_q < HEAD_DIM) == (hh == 0), q2, jnp.zeros_like(q2)) for hh in range(2)]

        def tiles(js, state, diagonal):
            chains = [(n, hh) for n in range(len(js)) for hh in range(2)]
            kb = [k_ref[pl.ds(pl.multiple_of(j * t, t), t), :] for j in js]
            z = {ch: _nt(kb[ch[0]], qh[ch[1]]) for ch in chains}
            lb, aft, csum = {}, {}, {}
            for ch in chains:
                lb[ch], l1m = _log_sigmoid_parts(z[ch])
                if diagonal[ch[0]]:
                    l1m = jnp.where(valid, l1m, 0.0)
                aft[ch] = _nn(later, l1m.astype(BF16))
                csum[ch] = _colsum(l1m)
            state = list(state)
            for ch in chains:
                n, hh = ch
                c_after, acc = state[hh]
                w = jnp.exp(lb[ch] + aft[ch] + c_after)
                if diagonal[ch[0]]:
                    w = jnp.where(valid, w, 0.0)
                car_ref[hh, pl.ds(js[n], 1), :] = c_after
                state[hh] = (c_after + csum[ch], acc + _nn(vt_ref[0, js[n]], w.astype(BF16)))
            return tuple(state)

        def alive(state):
            return jnp.max(jnp.maximum(state[0][0], state[1][0])) >= DEAD_LOG_WEIGHT

        zero = (jnp.zeros((1, t), F32), jnp.zeros((LANES, t), F32))
        (_, acc0), (_, acc1) = _walk_down(i, tiles, (zero, zero), alive)
        o_ref[...] = jnp.where(orow < HEAD_DIM, acc0, acc1).T.astype(o_ref.dtype)

    return pl.pallas_call(
        body, name="attn_fwd", grid=(nhp, nb),
        in_specs=[pl.BlockSpec((t, LANES), lambda hp, i: (i, hp)),
                  pl.BlockSpec((s, LANES), lambda hp, i: (0, hp)),
                  pl.BlockSpec((1, nb, LANES, t), lambda hp, i: (hp, 0, 0, 0))],
        out_specs=[pl.BlockSpec((t, LANES), lambda hp, i: (i, hp)),
                   pl.BlockSpec((2, nb, t), lambda hp, i: (hp, 0, i))],
        out_shape=[jax.ShapeDtypeStruct((s, nhp * LANES), BF16), jax.ShapeDtypeStruct((2 * nhp, nb, s), F32)],
        compiler_params=_params(dimension_semantics=("arbitrary", "arbitrary")),
    )(qs, k, vt3)


def _attn_bwd(qs, do, k, v, kt3, car, t):
    s = qs.shape[0]
    nb, nhp = s // t, qs.shape[1] // LANES

    def body(q_ref, do_ref, k_ref, v_ref, kt_ref, car_ref, dq_ref, dk_ref, dv_ref):
        i = pl.program_id(1)

        @pl.when(i == 0)
        def _():
            dk_ref[...] = jnp.zeros_like(dk_ref)
            dv_ref[...] = jnp.zeros_like(dv_ref)

        q2, do2 = q_ref[...], do_ref[...]
        lane_q = lax.broadcasted_iota(jnp.int32, q2.shape, 1)
        row, col = _tile_masks(t)
        later = (col > row).astype(BF16)
        earlier = (col < row).astype(BF16)
        valid = row < col
        orow = lax.broadcasted_iota(jnp.int32, (LANES, t), 0)
        head = [(lane_q < HEAD_DIM) == (hh == 0) for hh in range(2)]
        qh = [jnp.where(hm, q2, jnp.zeros_like(q2)) for hm in head]
        doh = [jnp.where(hm, do2, jnp.zeros_like(do2)) for hm in head]

        def tiles(js, state, diagonal):
            chains = [(n, hh) for n in range(len(js)) for hh in range(2)]
            rows = [pl.ds(pl.multiple_of(j * t, t), t) for j in js]
            kb = [k_ref[r, :] for r in rows]
            vb = [v_ref[r, :] for r in rows]
            z = {ch: _nt(kb[ch[0]], qh[ch[1]]) for ch in chains}
            dw = {ch: _nt(vb[ch[0]], doh[ch[1]]) for ch in chains}
            lb, beta, aft = {}, {}, {}
            for ch in chains:
                lb[ch], l1m = _log_sigmoid_parts(z[ch])
                beta[ch] = jnp.exp(lb[ch])
                if diagonal[ch[0]]:
                    l1m = jnp.where(valid, l1m, 0.0)
                aft[ch] = _nn(later, l1m.astype(BF16))
            w, g, gsum, g_in = {}, {}, {}, {}
            for ch in chains:
                n, hh = ch
                w[ch] = jnp.exp(lb[ch] + aft[ch] + car_ref[hh, pl.ds(js[n], 1), :])
                if diagonal[ch[0]]:
                    w[ch] = jnp.where(valid, w[ch], 0.0)
                g[ch] = dw[ch] * w[ch]
                g_in[ch] = _nn(earlier, g[ch].astype(BF16))
                gsum[ch] = _colsum(g[ch])
            state = list(state)
            dk_t, dv_t = [None] * len(js), [None] * len(js)
            for ch in chains:
                n, hh = ch
                c_g, dqt = state[hh]
                dz = g[ch] - beta[ch] * (g[ch] + g_in[ch] + c_g)
                if diagonal[ch[0]]:
                    dz = jnp.where(valid, dz, 0.0)
                dzb, wb = dz.astype(BF16), w[ch].astype(BF16)
                dk_h, dv_h = _nn(dzb, qh[hh]), _nn(wb, doh[hh])
                dk_t[n] = dk_h if dk_t[n] is None else dk_t[n] + dk_h
                dv_t[n] = dv_h if dv_t[n] is None else dv_t[n] + dv_h
                state[hh] = (c_g + gsum[ch], dqt + _nn(kt_ref[0, js[n]], dzb))
            for n in range(len(js)):
                dk_ref[rows[n], :] += dk_t[n]
                dv_ref[rows[n], :] += dv_t[n]
            return tuple(state)

        reach = jnp.max(jnp.max(car_ref[...], axis=2, keepdims=True), axis=0)
        dead = (reach < DEAD_LOG_WEIGHT) & (lax.broadcasted_iota(jnp.int32, reach.shape, 0) < i)
        first = jnp.sum(jnp.where(dead, 1.0, 0.0)).astype(jnp.int32)
        zero = (jnp.zeros((1, t), F32), jnp.zeros((LANES, t), F32))
        (_, dq0), (_, dq1) = _walk_up(i, first, tiles, (zero, zero))
        dq_ref[...] = jnp.where(orow < HEAD_DIM, dq0, dq1).T

    tile_spec = pl.BlockSpec((t, LANES), lambda hp, i: (i, hp))
    whole = pl.BlockSpec((s, LANES), lambda hp, i: (0, hp))
    return pl.pallas_call(
        body, name="attn_bwd", grid=(nhp, nb),
        in_specs=[tile_spec, tile_spec, whole, whole,
                  pl.BlockSpec((1, nb, LANES, t), lambda hp, i: (hp, 0, 0, 0)),
                  pl.BlockSpec((2, nb, t), lambda hp, i: (hp, 0, i))],
        out_specs=[tile_spec, whole, whole],
        out_shape=[jax.ShapeDtypeStruct((s, nhp * LANES), F32)] * 3,
        compiler_params=_params(dimension_semantics=("arbitrary", "arbitrary")),
    )(qs, do, k, v, kt3, car)


SCAN_LANES = 1024
SCAN_ROWS = 8
S5_CHUNKS = 4


def _scan_chunks(v):
    n = v.shape[1] // (2 * LANES)
    return [(v[:, c * 2 * LANES:c * 2 * LANES + LANES], v[:, c * 2 * LANES + LANES:(c + 1) * 2 * LANES]) for c in range(n)]


def _scan_tables(lr, li, reverse):
    if reverse:
        li = -li
    row = lax.broadcasted_iota(jnp.int32, (SCAN_ROWS, LANES), 0)
    powers = [(lr, li)]
    for _ in range(SCAN_ROWS - 1):
        pr, pi = powers[-1]
        powers.append((pr * lr - pi * li, pr * li + pi * lr))
    levels = []
    for d in (1, 2, 4):
        keep = (row < SCAN_ROWS - d) if reverse else (row >= d)
        levels.append((SCAN_ROWS - d if reverse else d,
                       (jnp.where(keep, powers[d - 1][0], 0.0), jnp.where(keep, powers[d - 1][1], 0.0))))
    pr = pi = jnp.zeros((SCAN_ROWS, LANES), F32)
    for r in range(SCAN_ROWS):
        steps = SCAN_ROWS - r if reverse else r + 1
        pr = jnp.where(row == r, powers[steps - 1][0], pr)
        pi = jnp.where(row == r, powers[steps - 1][1], pi)
    return levels, (pr, pi)


def _s5_fwd(proj, u_off, bmat, cmat, lam):
    s, w = proj.shape[0], bmat.shape[1]
    tt = _pick(s, (512, 256, 128, 8))
    nt = s // tt
    cin = bmat.shape[0] // SSM_BLOCKS
    chunk = tt // S5_CHUNKS

    def body(u_ref, b_ref, c_ref, lam_ref, h_ref, y_ref, x_ref, st_ref):
        @pl.when(pl.program_id(1) == 0)
        def _():
            st_ref[...] = jnp.zeros_like(st_ref)

        tables = [_scan_tables(lr, li, reverse=False) for lr, li in _scan_chunks(lam_ref[...])]

        def project(k):
            x_ref[k * chunk:(k + 1) * chunk, :] = _nn(u_ref[k * chunk:(k + 1) * chunk, :].astype(BF16), b_ref[...])

        def tile(r0, last):
            last, parts = list(last), []
            for c, (xr, xi) in enumerate(_scan_chunks(x_ref[r0:r0 + SCAN_ROWS, :])):
                levels, (pr, pi) = tables[c]
                for d, (ar, ai) in levels:
                    sr, si = pltpu.roll(xr, d, 0), pltpu.roll(xi, d, 0)
                    xr, xi = xr + ar * sr - ai * si, xi + ar * si + ai * sr
                br, bi = last[2 * c], last[2 * c + 1]
                hr = xr + pr * br - pi * bi
                hi = xi + pr * bi + pi * br
                last[2 * c], last[2 * c + 1] = hr[SCAN_ROWS - 1:], hi[SCAN_ROWS - 1:]
                parts += [hr, hi]
            h_ref[r0:r0 + SCAN_ROWS, :] = jnp.concatenate(parts, axis=1)
            return tuple(last)

        st = st_ref[0:1, :]
        last = tuple(st[:, c * LANES:(c + 1) * LANES] for c in range(SCAN_LANES // LANES))
        project(0)
        for k in range(S5_CHUNKS):
            if k + 1 < S5_CHUNKS:
                project(k + 1)
            for r0 in range(k * chunk, (k + 1) * chunk, SCAN_ROWS):
                last = tile(r0, last)
            y_ref[k * chunk:(k + 1) * chunk, :] = _nn(h_ref[k * chunk:(k + 1) * chunk, :].astype(BF16), c_ref[...])
        st_ref[0:1, :] = jnp.concatenate(last, axis=1)

    return pl.pallas_call(
        body, name="s5_fwd", grid=(SSM_BLOCKS, nt),
        in_specs=[pl.BlockSpec((tt, cin), lambda kb, i: (i, u_off // cin + kb)),
                  pl.BlockSpec((cin, SCAN_LANES), lambda kb, i: (kb, kb)),
                  pl.BlockSpec((SCAN_LANES, cin), lambda kb, i: (kb, kb)),
                  pl.BlockSpec((1, SCAN_LANES), lambda kb, i: (0, kb))],
        out_specs=[pl.BlockSpec((tt, SCAN_LANES), lambda kb, i: (i, kb)), pl.BlockSpec((tt, cin), lambda kb, i: (i, kb))],
        out_shape=[jax.ShapeDtypeStruct((s, w), F32), jax.ShapeDtypeStruct((s, bmat.shape[0]), F32)],
        scratch_shapes=[pltpu.VMEM((tt, SCAN_LANES), F32), pltpu.VMEM((SCAN_ROWS, SCAN_LANES), F32)],
        compiler_params=_params(dimension_semantics=("arbitrary", "arbitrary")),
    )(proj, bmat, cmat, lam)


def _s5_bwd(dy, h, proj, u_off, bmat, cmat, lam):
    s, w = h.shape
    tt = _pick(s, (512, 256, 128, 8))
    nt = s // tt
    cin = bmat.shape[0] // SSM_BLOCKS
    chunk = tt // S5_CHUNKS

    def body(dy_ref, h_ref, u_ref, b_ref, c_ref, lam_ref, du_ref, dlam_ref, db_ref, dc_ref, e_ref, a_ref, st_ref):
        @pl.when(pl.program_id(1) == 0)
        def _():
            st_ref[...] = jnp.zeros_like(st_ref)
            dlam_ref[...] = jnp.zeros_like(dlam_ref)
            db_ref[...] = jnp.zeros_like(db_ref)
            dc_ref[...] = jnp.zeros_like(dc_ref)

        tables = [_scan_tables(lr, li, reverse=True) for lr, li in _scan_chunks(lam_ref[...])]
        nch = len(tables)
        row = lax.broadcasted_iota(jnp.int32, (SCAN_ROWS, LANES), 0)
        rows_first = (((0,), (0,)), ((), ()))

        def project(k):
            e_ref[k * chunk:(k + 1) * chunk, :] = _nt(dy_ref[k * chunk:(k + 1) * chunk, :], c_ref[...])

        def finish(k):
            rows = slice(k * chunk, (k + 1) * chunk)
            adj = a_ref[rows, :].astype(BF16)
            du_ref[rows, :] = _nt(adj, b_ref[...])
            db_ref[...] += lax.dot_general(u_ref[rows, :].astype(BF16), adj, rows_first, preferred_element_type=F32)
            dc_ref[...] += lax.dot_general(h_ref[rows, :].astype(BF16), dy_ref[rows, :], rows_first, preferred_element_type=F32)

        def tile(r0, carry):
            e_c = _scan_chunks(e_ref[r0:r0 + SCAN_ROWS, :])
            h_c = _scan_chunks(h_ref[r0:r0 + SCAN_ROWS, :])
            carry, parts = list(carry), []
            for c in range(nch):
                (yr, yi), (hr, hi) = e_c[c], h_c[c]
                levels, (pr, pi) = tables[c]
                for shift, (lr, li) in levels:
                    sr, si = pltpu.roll(yr, shift, 0), pltpu.roll(yi, shift, 0)
                    yr, yi = yr + lr * sr - li * si, yi + lr * si + li * sr
                nr, ni, dr, di = carry[4 * c:4 * c + 4]
                ar = yr + pr * nr - pi * ni
                ai = yi + pr * ni + pi * nr
                nxr = jnp.where(row == SCAN_ROWS - 1, nr, pltpu.roll(ar, SCAN_ROWS - 1, 0))
                nxi = jnp.where(row == SCAN_ROWS - 1, ni, pltpu.roll(ai, SCAN_ROWS - 1, 0))
                carry[4 * c:4 * c + 4] = [ar[0:1], ai[0:1], dr + nxr * hr + nxi * hi, di + nxi * hr - nxr * hi]
                parts += [ar, ai]
            a_ref[r0:r0 + SCAN_ROWS, :] = jnp.concatenate(parts, axis=1)
            return tuple(carry)

        st, dl = st_ref[0:1, :], dlam_ref[...]
        init = []
        for c in range(nch):
            lo = c * 2 * LANES
            init += [st[:, lo:lo + LANES], st[:, lo + LANES:lo + 2 * LANES],
                     dl[:, lo:lo + LANES], dl[:, lo + LANES:lo + 2 * LANES]]
        fin = tuple(init)
        project(S5_CHUNKS - 1)
        for k in reversed(range(S5_CHUNKS)):
            if k > 0:
                project(k - 1)
            for r0 in reversed(range(k * chunk, (k + 1) * chunk, SCAN_ROWS)):
                fin = tile(r0, fin)
            finish(k)
        st_ref[0:1, :] = jnp.concatenate([fin[4 * c + q] for c in range(nch) for q in (0, 1)], axis=1)
        dlam_ref[...] = jnp.concatenate([fin[4 * c + q] for c in range(nch) for q in (2, 3)], axis=1)

        @pl.when(pl.program_id(1) == nt - 1)
        def _():
            dlam_ref[0:1, :] = jnp.sum(dlam_ref[...], axis=0, keepdims=True)

    def rev(width, col):
        return pl.BlockSpec((tt, width), lambda kb, i: (nt - 1 - i, col(kb)))

    return pl.pallas_call(
        body, name="s5_bwd", grid=(SSM_BLOCKS, nt),
        in_specs=[rev(cin, lambda kb: kb), rev(SCAN_LANES, lambda kb: kb), rev(cin, lambda kb: u_off // cin + kb),
                  pl.BlockSpec((cin, SCAN_LANES), lambda kb, i: (kb, kb)),
                  pl.BlockSpec((SCAN_LANES, cin), lambda kb, i: (kb, kb)),
                  pl.BlockSpec((1, SCAN_LANES), lambda kb, i: (0, kb))],
        out_specs=[rev(cin, lambda kb: kb), pl.BlockSpec((SCAN_ROWS, SCAN_LANES), lambda kb, i: (0, kb)),
                   pl.BlockSpec((cin, SCAN_LANES), lambda kb, i: (kb, 0)), pl.BlockSpec((SCAN_LANES, cin), lambda kb, i: (kb, 0))],
        out_shape=[jax.ShapeDtypeStruct((s, bmat.shape[0]), F32), jax.ShapeDtypeStruct((SCAN_ROWS, w), F32),
                   jax.ShapeDtypeStruct((bmat.shape[0], SCAN_LANES), F32), jax.ShapeDtypeStruct((w, cin), F32)],
        scratch_shapes=[pltpu.VMEM((tt, SCAN_LANES), F32), pltpu.VMEM((tt, SCAN_LANES), F32),
                        pltpu.VMEM((SCAN_ROWS, SCAN_LANES), F32)],
        compiler_params=_params(dimension_semantics=("arbitrary", "arbitrary")),
    )(dy, h, proj, bmat, cmat, lam)


def _ssm_params_fwd(a_re, a_im, log_dt, b_re, b_im):
    def body(ar_ref, ai_ref, ldt_ref, br_ref, bi_ref, lr_ref, li_ref, bbr_ref, bbi_ref):
        ar, ai, dt = ar_ref[...], ai_ref[...], jnp.exp(ldt_ref[...])
        mag = jnp.exp(ar * dt)
        lr, li = mag * jnp.cos(ai * dt), mag * jnp.sin(ai * dt)
        den = ar * ar + ai * ai
        cr = ((lr - 1.0) * ar + li * ai) / den
        ci = (li * ar - (lr - 1.0) * ai) / den
        br, bi = br_ref[...], bi_ref[...]
        lr_ref[...], li_ref[...] = lr, li
        bbr_ref[...] = cr * br - ci * bi
        bbi_ref[...] = cr * bi + ci * br

    n = a_re.shape[0]
    v1, v16 = jax.ShapeDtypeStruct((n, 1), F32), jax.ShapeDtypeStruct((n, SSM_GROUP), F32)
    return pl.pallas_call(body, name="ssm_params_fwd", out_shape=[v1, v1, v16, v16],
                          compiler_params=_params())(a_re, a_im, log_dt, b_re, b_im)


def _ssm_params_bwd(a_re, a_im, log_dt, b_re, b_im, g_lr, g_li, g_bbr, g_bbi):
    n = a_re.shape[0]

    def body(ar_ref, ai_ref, ldt_ref, br_ref, bi_ref, glr_ref, gli_ref, gbr_ref, gbi_ref,
             dar_ref, dai_ref, dldt_ref, dbr_ref, dbi_ref):
        ar, ai, dt = ar_ref[...], ai_ref[...], jnp.exp(ldt_ref[...])
        mag = jnp.exp(ar * dt)
        lr, li = mag * jnp.cos(ai * dt), mag * jnp.sin(ai * dt)
        den = ar * ar + ai * ai
        cr = ((lr - 1.0) * ar + li * ai) / den
        ci = (li * ar - (lr - 1.0) * ai) / den
        br, bi, gbr, gbi = br_ref[...], bi_ref[...], gbr_ref[...], gbi_ref[...]
        dbr_ref[...] = gbr * cr + gbi * ci
        dbi_ref[...] = gbi * cr - gbr * ci
        gcr = jnp.sum(gbr * br + gbi * bi, axis=1, keepdims=True)
        gci = jnp.sum(gbi * br - gbr * bi, axis=1, keepdims=True)
        ir, ii = ar / den, -ai / den
        glr = glr_ref[...] + gcr * ir + gci * ii
        gli = gli_ref[...] + gci * ir - gcr * ii
        qr, qi = cr * ir - ci * ii, cr * ii + ci * ir
        gar = -(gcr * qr + gci * qi)
        gai = -(gci * qr - gcr * qi)
        gxr = glr * lr + gli * li
        gxi = gli * lr - glr * li
        dar_ref[...] = gar + gxr * dt
        dai_ref[...] = gai + gxi * dt
        gdt = (gxr * ar + gxi * ai) * dt
        rowg = lax.broadcasted_iota(jnp.int32, (n, SSM_GROUPS), 0) // SSM_STATE
        colg = lax.broadcasted_iota(jnp.int32, (n, SSM_GROUPS), 1)
        dldt_ref[...] = jnp.sum(jnp.where(rowg == colg, gdt, 0.0), axis=0, keepdims=True)

    v1, v16 = jax.ShapeDtypeStruct((n, 1), F32), jax.ShapeDtypeStruct((n, SSM_GROUP), F32)
    return pl.pallas_call(body, name="ssm_params_bwd",
                          out_shape=[v1, v1, jax.ShapeDtypeStruct((1, SSM_GROUPS), F32), v16, v16],
                          compiler_params=_params())(a_re, a_im, log_dt, b_re, b_im, g_lr, g_li, g_bbr, g_bbi)


def _interleave(re, im, axis):
    shp = list(re.shape)
    new = shp[:axis] + [shp[axis] // LANES, LANES] + shp[axis + 1:]
    st = jnp.stack([re.reshape(new), im.reshape(new)], axis=axis + 1)
    return st.reshape(shp[:axis] + [2 * shp[axis]] + shp[axis + 1:])


def _deinterleave(v, axis):
    shp = list(v.shape)
    r = v.reshape(shp[:axis] + [shp[axis] // (2 * LANES), 2, LANES] + shp[axis + 1:])
    out = shp[:axis] + [shp[axis] // 2] + shp[axis + 1:]
    return (lax.index_in_dim(r, 0, axis + 1, keepdims=False).reshape(out),
            lax.index_in_dim(r, 1, axis + 1, keepdims=False).reshape(out))


def _b_matrix(bbr, bbi):
    eye = jnp.eye(SSM_GROUPS, dtype=F32)

    def blockdiag(v):
        x = v.reshape(SSM_GROUPS, SSM_STATE, SSM_GROUP).transpose(0, 2, 1)
        return (eye[:, None, :, None] * x[:, :, None, :]).reshape(SSM_GROUPS * SSM_GROUP, N_STATE)

    return _interleave(blockdiag(bbr), blockdiag(bbi), 1)


def _diag_blocks(v, rows, cols):
    per = SSM_GROUPS // SSM_BLOCKS
    return jnp.stack([v[g * rows:(g + 1) * rows, (g % per) * cols:(g % per + 1) * cols] for g in range(SSM_GROUPS)])


def _b_matrix_grad(d):
    def diag(v):
        return _diag_blocks(v, SSM_GROUP, SSM_STATE).transpose(0, 2, 1).reshape(N_STATE, SSM_GROUP)

    dr, di = _deinterleave(d, 1)
    return diag(dr), diag(di)


def _c_matrix(c_re, c_im):
    eye = jnp.eye(SSM_GROUPS, dtype=F32)

    def blockdiag(v):
        x = v.transpose(0, 2, 1)
        return (x[:, :, None, :] * eye[:, None, :, None]).reshape(N_STATE, SSM_GROUPS * SSM_GROUP)

    return _interleave(blockdiag(c_re), blockdiag(-c_im), 0)


def _c_matrix_grad(d):
    def diag(v):
        return _diag_blocks(v, SSM_STATE, SSM_GROUP).transpose(0, 2, 1)

    dr, di = _deinterleave(d, 0)
    return diag(dr), -diag(di)


def _row(v):
    return v.reshape(1, -1)


def _ssm_inputs(p):
    rows = lambda v: v.reshape(N_STATE, -1)
    ldt = jnp.repeat(p["ssm_log_dt"], SSM_STATE).reshape(N_STATE, 1)
    return rows(p["ssm_a_re"]), rows(p["ssm_a_im"]), ldt, rows(p["ssm_b_re"]), rows(p["ssm_b_im"])


def _lnmod(x, sc, sh):
    return _norm(x)[0] * (1.0 + sc) + sh


def _resid_ln(x, y, g, lg, lb):
    return _norm(ALPHA * x + (1.0 + g) * y)[0] * lg + lb


def _resid_ln_lnmod(x, y, g, lg, lb, sc, sh):
    xo = _resid_ln(x, y, g, lg, lb)
    return xo, _lnmod(xo, sc, sh)


def _layer_fwd(x, h1, mod, p, tag, next_mod):
    d = x.shape[1]
    sh_m, sc_m, g_m, sh_f, sc_f, g_f = [_row(mod[i]) for i in range(6)]
    nm = lambda s: f"{s}_{tag}"
    proj = _mm(nm("proj"), h1, p["w_in"], "nn")
    t = min(ATT_TILE, x.shape[0])
    qs, kb, vb, kt3, vt3 = _qkv_prep(proj, t)
    att, car = _attn_fwd(qs, kb, vt3, t)
    y_sb = _mm(nm("sb_up"), att, p["w_sb_up"], "nn", out_dtype=BF16)

    lam_r, lam_i, bbr, bbi = _ssm_params_fwd(*_ssm_inputs(p))
    lam = _interleave(lam_r.reshape(1, N_STATE), lam_i.reshape(1, N_STATE), 1)
    bmat = _b_matrix(bbr, bbi).astype(BF16)
    cmat = _c_matrix(p["ssm_c_re"], p["ssm_c_im"]).astype(BF16)
    hst, yc = _s5_fwd(proj, U_OFFSET, bmat, cmat, lam)

    def ssm_act(yc, u, dsk):
        y0 = yc + dsk * u
        return y0, _gelu(y0)

    y0, y1 = _rowwise(nm("ssm_act"), ssm_act, [(yc, 0, 512), (proj, 3, 512)], [_row(p["ssm_d"])], [(512, F32), (512, F32)])
    gl = _mm(nm("glu"), y1, p["w_glu"], "nn")
    y2 = _rowwise(nm("glu_act"), lambda y1, gl, b: y1 * jax.nn.sigmoid(gl + b), [(y1, 0, 512), (gl, 0, 512)],
                  [_row(p["b_glu"])], [(512, BF16)])
    y_ssm = _mm(nm("ssm_up"), y2, p["w_ssm_up"], "nn", out_dtype=BF16)

    def merge(gsb, gss, ysb, yss):
        return jax.nn.sigmoid(gsb) * ysb + jax.nn.sigmoid(gss) * yss

    merged = _rowwise(nm("merge"), merge, [(proj, 2, d), (proj, 3, d), (y_sb, 0, d), (y_ssm, 0, d)], [], [(d, BF16)])
    y = _mm(nm("out"), merged, p["w_out"], "nn")

    x1, h2 = _rowwise(nm("ln1"), _resid_ln_lnmod, [(x, 0, d), (y, 0, d)],
                      [g_m, _row(p["ln1_g"]), _row(p["ln1_b"]), sc_f, sh_f], [(d, F32), (d, BF16)])
    f = _mm(nm("ffn_in"), h2, p["w_ffn_in"], "nn", out_dtype=BF16)
    fh = f.shape[1] // 2
    act = _rowwise(nm("swiglu"), lambda g, u: g * jax.nn.sigmoid(g) * u, [(f, 0, fh), (f, 1, fh)], [], [(fh, BF16)])
    yf = _mm(nm("ffn_out"), act, p["w_ffn_out"], "nn")
    x2 = h1_next = None
    if next_mod is not None:
        x2, h1_next = _rowwise(nm("ln2"), _resid_ln_lnmod, [(x1, 0, d), (yf, 0, d)],
                               [g_f, _row(p["ln2_g"]), _row(p["ln2_b"]), next_mod[1], next_mod[0]], [(d, F32), (d, BF16)])
    saved = dict(x=x, h1=h1, proj=proj, qs=qs, kb=kb, vb=vb, kt3=kt3, car=car, att=att, y_sb=y_sb, lam=lam, bmat=bmat,
                 cmat=cmat, hst=hst, y0=y0, y1=y1, gl=gl, y2=y2, y_ssm=y_ssm, merged=merged, y=y, x1=x1, h2=h2, f=f,
                 act=act, yf=yf, t=t)
    return x2, h1_next, saved


def _resid_ln_bwd(x, y, dxo, g, lg):
    n, rstd = _norm(ALPHA * x + (1.0 + g) * y)
    dr = _norm_bwd(dxo * lg, n, rstd)
    return ALPHA * dr, (1.0 + g) * dr, _colsum(dxo * n), _colsum(dxo), _colsum(dr * y)


def _lnmod_bwd(x, dh, dxa, sc):
    n, rstd = _norm(x)
    return dxa + _norm_bwd(dh * (1.0 + sc), n, rstd), _colsum(dh * n), _colsum(dh)


def _lnmod_resid_ln_bwd(xo, dh, dxa, x, y, sc, g, lg):
    dxo, dsc, dsh = _lnmod_bwd(xo, dh, dxa, sc)
    dx, dy, dlg, dlb, dg = _resid_ln_bwd(x, y, dxo, g, lg)
    return dx, dy, dsc, dsh, dlg, dlb, dg


def _layer_bwd(dx1a, dyf, mod, p, sv, layer, depth, stacked):
    d = dx1a.shape[1]
    sh_m, sc_m, g_m, sh_f, sc_f, g_f = [_row(mod[i]) for i in range(6)]
    nm = lambda s: f"{s}_{layer}"
    grads = {}

    def weight_grad(n, a, b, **kw):
        grads[n] = _mm(nm("d" + n), a, b, "tn", out_dtype=BF16, into=(stacked.get(n), layer, depth), **kw)

    dact = _mm(nm("d_act"), dyf, p["w_ffn_out"], "nt", out_dtype=BF16)
    weight_grad("w_ffn_out", sv["act"], dyf)
    fh = sv["f"].shape[1] // 2

    def swiglu_bwd(g, u, da):
        sg = jax.nn.sigmoid(g)
        return jnp.concatenate([da * u * sg * (1.0 + g * (1.0 - sg)), da * g * sg], axis=1)

    df = _rowwise(nm("swiglu_bwd"), swiglu_bwd, [(sv["f"], 0, fh), (sv["f"], 1, fh), (dact, 0, fh)], [], [(2 * fh, BF16)])
    dh2 = _mm(nm("d_h2"), df, p["w_ffn_in"], "nt")
    weight_grad("w_ffn_in", sv["h2"], df)
    dxa, dy, dsc_f, dsh_f, grads["ln1_g"], grads["ln1_b"], dg_m = _rowwise(
        nm("ln1_bwd"), _lnmod_resid_ln_bwd, [(sv["x1"], 0, d), (dh2, 0, d), (dx1a, 0, d), (sv["x"], 0, d), (sv["y"], 0, d)],
        [sc_f, g_m, _row(p["ln1_g"])], [(d, F32), (d, BF16)], [d] * 5, tm=128)
    dmerged = _mm(nm("d_merged"), dy, p["w_out"], "nt", out_dtype=BF16)
    weight_grad("w_out", sv["merged"], dy)

    def merge_bwd(gsb, gss, ysb, yss, dm):
        s1, s2 = jax.nn.sigmoid(gsb), jax.nn.sigmoid(gss)
        return s1 * dm, s2 * dm, dm * ysb * s1 * (1.0 - s1), dm * yss * s2 * (1.0 - s2)

    dy_sb, dy_ssm, dg_sb, dg_ssm = _rowwise(
        nm("merge_bwd"), merge_bwd, [(sv["proj"], 2, d), (sv["proj"], 3, d), (sv["y_sb"], 0, d), (sv["y_ssm"], 0, d),
                                     (dmerged, 0, d)], [], [(d, BF16)] * 4)
    dy2 = _mm(nm("d_y2"), dy_ssm, p["w_ssm_up"], "nt")
    weight_grad("w_ssm_up", sv["y2"], dy_ssm)

    def glu_act_bwd(y1, gl, dy2, b):
        sg = jax.nn.sigmoid(gl + b)
        dgl = dy2 * y1 * sg * (1.0 - sg)
        return dy2 * sg, dgl, _colsum(dgl)

    dy1a, dgl, grads["b_glu"] = _rowwise(nm("glu_act_bwd"), glu_act_bwd, [(sv["y1"], 0, 512), (sv["gl"], 0, 512), (dy2, 0, 512)],
                                         [_row(p["b_glu"])], [(512, F32), (512, BF16)], [512])
    dy1b = _mm(nm("d_y1"), dgl, p["w_glu"], "nt")
    weight_grad("w_glu", sv["y1"], dgl)

    def ssm_act_bwd(y0, u, dy1a, dy1b, dsk):
        dy0 = (dy1a + dy1b) * _gelu_grad(y0)
        return dy0, dsk * dy0, _colsum(dy0 * u)

    dy0, du_a, grads["ssm_d"] = _rowwise(nm("ssm_act_bwd"), ssm_act_bwd,
                                         [(sv["y0"], 0, 512), (sv["proj"], 3, 512), (dy1a, 0, 512), (dy1b, 0, 512)],
                                         [_row(p["ssm_d"])], [(512, BF16), (512, F32)], [512])
    du_b, dlam, d_bmat, d_cmat = _s5_bwd(dy0, sv["hst"], sv["proj"], U_OFFSET, sv["bmat"], sv["cmat"], sv["lam"])
    grads["ssm_c_re"], grads["ssm_c_im"] = _c_matrix_grad(d_cmat)
    g_bbr, g_bbi = _b_matrix_grad(d_bmat)
    g_lr, g_li = _deinterleave(dlam[0:1], 1)
    da_re, da_im, dldt, db_re, db_im = _ssm_params_bwd(*_ssm_inputs(p), g_lr.reshape(N_STATE, 1), g_li.reshape(N_STATE, 1),
                                                       g_bbr, g_bbi)
    grads["ssm_a_re"] = da_re.reshape(SSM_GROUPS, SSM_STATE)
    grads["ssm_a_im"] = da_im.reshape(SSM_GROUPS, SSM_STATE)
    grads["ssm_log_dt"] = dldt.reshape(SSM_GROUPS)
    grads["ssm_b_re"] = db_re.reshape(SSM_GROUPS, SSM_STATE, SSM_GROUP)
    grads["ssm_b_im"] = db_im.reshape(SSM_GROUPS, SSM_STATE, SSM_GROUP)
    datt = _mm(nm("d_att"), dy_sb, p["w_sb_up"], "nt", out_dtype=BF16)
    weight_grad("w_sb_up", sv["att"], dy_sb)
    dqs, dk, dv = _attn_bwd(sv["qs"], datt, sv["kb"], sv["vb"], sv["kt3"], sv["car"], sv["t"])

    def dproj_cols(dqs, dk, dv, dua, dub, dgsb, dgss):
        return jnp.concatenate([dqs * (1.0 / math.sqrt(HEAD_DIM)), dk, dv, dua + dub, dgsb.astype(F32), dgss.astype(F32)],
                               axis=1)

    dproj = _rowwise(nm("dproj"), dproj_cols, [(dqs, 0, 512), (dk, 0, 512), (dv, 0, 512), (du_a, 0, 512), (du_b, 0, 512),
                                               (dg_sb, 0, d), (dg_ssm, 0, d)], [], [(2048 + 2 * d, BF16)])
    dh1 = _mm(nm("d_h1"), dproj, p["w_in"], "nt")
    weight_grad("w_in", sv["h1"], dproj)
    for k in ("ln1_g", "ln1_b", "ssm_d", "b_glu"):
        grads[k] = grads[k].reshape(-1)
    return dh1, dxa, grads, (dg_m, dsh_f, dsc_f)


def _local_step(x, target, mod, layer_w):
    depth, d = len(layer_w), x.shape[1]
    rows = lambda l: [_row(mod[l][i]) for i in range(6)]
    h1 = _rowwise("lnmod1_0", _lnmod, [(x, 0, d)], [rows(0)[1], rows(0)[0]], [(d, BF16)])
    xs, saved = x, []
    for l in range(depth):
        xs, h1, sv = _layer_fwd(xs, h1, mod[l], layer_w[l], str(l), rows(l + 1)[:2] if l + 1 < depth else None)
        saved.append(sv)

    def head_bwd(x1, yf, tgt, g, lg, lb):
        err = _resid_ln(x1, yf, g, lg, lb) - tgt
        return _resid_ln_bwd(x1, yf, err * (1.0 / d), g, lg) + (_colsum(err * err) * (0.5 / d),)

    def boundary_bwd(dh, dxa, x1, yf, sc, g, lg, lb):
        dxo, dsc, dsh = _lnmod_bwd(_resid_ln(x1, yf, g, lg, lb), dh, dxa, sc)
        return _resid_ln_bwd(x1, yf, dxo, g, lg) + (dsc, dsh)

    lgrads, sums, stacked = [None] * depth, [dict() for _ in range(depth)], {}
    last, p = saved[-1], layer_w[-1]
    dx1a, dyf, dlg, dlb, dg_f, loss_cols = _rowwise(
        "head_bwd", head_bwd, [(last["x1"], 0, d), (last["yf"], 0, d), (target, 0, d)],
        [rows(depth - 1)[5], _row(p["ln2_g"]), _row(p["ln2_b"])], [(d, F32), (d, BF16)], [d] * 4)
    for l in reversed(range(depth)):
        sums[l]["g_f"] = dg_f
        dh1, dxa, lgrads[l], (sums[l]["g_m"], sums[l]["sh_f"], sums[l]["sc_f"]) = _layer_bwd(
            dx1a, dyf, mod[l], layer_w[l], saved[l], l, depth, stacked)
        lgrads[l]["ln2_g"], lgrads[l]["ln2_b"] = dlg.reshape(-1), dlb.reshape(-1)
        stacked = {n: lgrads[l][n] for n in COL_SPLIT + ROW_SPLIT}
        if l > 0:
            prev, p = saved[l - 1], layer_w[l - 1]
            dx1a, dyf, dlg, dlb, dg_f, sums[l]["sc_m"], sums[l]["sh_m"] = _rowwise(
                f"boundary_bwd_{l}", boundary_bwd, [(dh1, 0, d), (dxa, 0, d), (prev["x1"], 0, d), (prev["yf"], 0, d)],
                [rows(l)[1], rows(l - 1)[5], _row(p["ln2_g"]), _row(p["ln2_b"])], [(d, F32), (d, BF16)], [d] * 5, tm=128)
        else:
            dx, sums[l]["sc_m"], sums[l]["sh_m"] = _rowwise("lnmod1_bwd", _lnmod_bwd, [(x, 0, d), (dh1, 0, d), (dxa, 0, d)],
                                                            [rows(0)[1]], [(d, F32)], [d, d])
    dmod = jnp.stack([jnp.concatenate([sums[l][k] for k in ("sh_m", "sc_m", "g_m", "sh_f", "sc_f", "g_f")], axis=0)
                      for l in range(depth)])
    return loss_cols, dx, dmod, lgrads, stacked


def _place():
    return lax.axis_index("x"), lax.axis_index("y"), lax.axis_index("c")


def _all_gather8(name, block):
    m_per, n = block.shape

    def body(x_ref, out_ref, send_sems, recv_sems, local_sem):
        x, y, c = _place()
        me, sibling = (x, y, c), (x, y, 1 - c)
        chips = [(1 - x, y), (x, 1 - y), (1 - x, 1 - y)]

        def rows(px, py, pc):
            return out_ref.at[pl.ds(pl.multiple_of((4 * px + 2 * py + pc) * m_per, 8), m_per), :]

        def copy(k, blk, to, src=None):
            return pltpu.make_async_remote_copy(src_ref=rows(*blk) if src is None else src, dst_ref=rows(*blk),
                                                send_sem=send_sems.at[k], recv_sem=recv_sems.at[k],
                                                device_id=to, device_id_type=MESH)

        mine = pltpu.make_async_copy(x_ref, rows(*me), local_sem)
        mine.start()
        first = [copy(0, me, sibling, src=x_ref)] + [copy(1 + j, me, (*chip, c), src=x_ref) for j, chip in enumerate(chips)]
        for cp in first:
            cp.start()
        passed = [copy(4 + j, (*chip, c), sibling) for j, chip in enumerate(chips)]
        for j, chip in enumerate(chips):
            copy(1 + j, (*chip, c), me).wait_recv()
            passed[j].start()
        copy(0, sibling, me).wait_recv()
        for j, chip in enumerate(chips):
            copy(4 + j, (*chip, 1 - c), me).wait_recv()
        for cp in first + passed:
            cp.wait_send()
        mine.wait()

    return pl.pallas_call(
        body, name=name, out_shape=jax.ShapeDtypeStruct((8 * m_per, n), block.dtype),
        in_specs=[pl.BlockSpec(memory_space=pltpu.VMEM)], out_specs=pl.BlockSpec(memory_space=pltpu.VMEM),
        scratch_shapes=[pltpu.SemaphoreType.DMA((7,)), pltpu.SemaphoreType.DMA((7,)), pltpu.SemaphoreType.DMA],
        compiler_params=_params(),
    )(block)


def _other_chips(x, y):
    return [(1 - x, y), (x, 1 - y), (1 - x, 1 - y)]


def _gather_weights(whole, by_rows):
    n = len(whole)

    def body(*refs):
        dst = refs[n:2 * n]
        ici_send, ici_recv, d2d_send, d2d_recv = refs[2 * n:]
        x, y, c = _place()
        chips = _other_chips(x, y)

        def part(ref, k, px, py, pc):
            _, r, cols = whole[k].shape
            q = 2 * px + py
            if by_rows[k]:
                return ref[k].at[:, pl.ds(pl.multiple_of((2 * q + pc) * (r // 8), 16), r // 8), :]
            return ref[k].at[:, pl.ds(pl.multiple_of(pc * (r // 2), 16), r // 2),
                             pl.ds(pl.multiple_of(q * (cols // 4), LANES), cols // 4)]

        def ici(k, j, px, py, to):
            return pltpu.make_async_remote_copy(src_ref=part(dst, k, px, py, c), dst_ref=part(dst, k, px, py, c),
                                                send_sem=ici_send.at[k, j], recv_sem=ici_recv.at[k, j],
                                                device_id=(*to, c), device_id_type=MESH)

        def d2d(k, j, px, py, pc):
            return pltpu.make_async_remote_copy(src_ref=part(dst, k, px, py, pc), dst_ref=part(dst, k, px, py, pc),
                                                send_sem=d2d_send.at[k, j], recv_sem=d2d_recv.at[k, j],
                                                device_id=(x, y, 1 - c), device_id_type=MESH)

        for k in range(n):
            for j, chip in enumerate(chips):
                ici(k, j, x, y, chip).start()
        for k in range(n):
            for j, chip in enumerate(chips):
                ici(k, j, *chip, chip).wait_recv()
                d2d(k, j, *chip, c).start()
        for k in range(n):
            for j, chip in enumerate(chips):
                d2d(k, j, *chip, 1 - c).wait_recv()
        for k in range(n):
            for j, chip in enumerate(chips):
                ici(k, j, x, y, chip).wait_send()
                d2d(k, j, *chip, c).wait_send()

    any_spec = pl.BlockSpec(memory_space=pl.ANY)
    return pl.pallas_call(
        body, name="gather_weights", in_specs=[any_spec] * n, out_specs=[any_spec] * n,
        out_shape=[jax.ShapeDtypeStruct(a.shape, a.dtype) for a in whole], input_output_aliases={k: k for k in range(n)},
        scratch_shapes=[pltpu.SemaphoreType.DMA((n, 3))] * 4,
        compiler_params=_params(),
    )(*whole)


def _part_shape(shape, by_rows):
    l, r, c = shape
    return (l, r // 8, c) if by_rows else (l, r // 2, c // 4)


def _pair_exchange(grads, by_rows):
    n = len(grads)

    def body(*refs):
        src, dst = refs[:n], refs[n:2 * n]
        send_sems, recv_sems = refs[2 * n:]
        x, y, c = _place()

        def window(k, q, pc):
            _, hr, hc = _part_shape(grads[k].shape, by_rows[k])
            if by_rows[k]:
                return src[k].at[:, pl.ds(pl.multiple_of((2 * q + pc) * hr, 16), hr), :]
            return src[k].at[:, pl.ds(pl.multiple_of(pc * hr, 16), hr), pl.ds(q * hc, hc)]

        def copy(k, q, pc):
            return pltpu.make_async_remote_copy(src_ref=window(k, q, pc), dst_ref=dst[k].at[q], send_sem=send_sems.at[k, q],
                                                recv_sem=recv_sems.at[k, q], device_id=(x, y, 1 - c), device_id_type=MESH)

        for k in range(n):
            for q in range(4):
                copy(k, q, 1 - c).start()
        for k in range(n):
            for q in range(4):
                copy(k, q, c).wait_recv()
        for k in range(n):
            for q in range(4):
                copy(k, q, 1 - c).wait_send()

    any_spec = pl.BlockSpec(memory_space=pl.ANY)
    return pl.pallas_call(
        body, name="pair_exchange", in_specs=[any_spec] * n, out_specs=[any_spec] * n,
        out_shape=[jax.ShapeDtypeStruct((4, *_part_shape(g.shape, rows)), g.dtype) for g, rows in zip(grads, by_rows)],
        scratch_shapes=[pltpu.SemaphoreType.DMA((n, 4)), pltpu.SemaphoreType.DMA((n, 4))],
        compiler_params=_params(),
    )(*grads)


def _pair_sum(name, g, theirs, by_rows, c, chip):
    _, l, hr, hc = theirs.shape
    tr = _pick(hr, (256, 176, 128, 64, 32))

    def body(s_ref, g_ref, t_ref, p_ref, own_ref):
        v = (g_ref[...].astype(F32) + t_ref[0].astype(F32)).astype(BF16)
        p_ref[0] = v

        @pl.when(pl.program_id(2) == s_ref[1])
        def _():
            own_ref[0] = v

    if by_rows:
        g_spec = pl.BlockSpec((1, tr, hc), lambda li, i, q, s: (li, (2 * q + s[0]) * (hr // tr) + i, 0))
    else:
        g_spec = pl.BlockSpec((1, tr, hc), lambda li, i, q, s: (li, s[0] * (hr // tr) + i, q))
    slot = pl.BlockSpec((1, 1, tr, hc), lambda li, i, q, s: (q, li, i, 0))
    grid_spec = pltpu.PrefetchScalarGridSpec(
        num_scalar_prefetch=1, grid=(l, hr // tr, 4), in_specs=[g_spec, slot],
        out_specs=[slot, pl.BlockSpec((1, 1, tr, hc), lambda li, i, q, s: (s[1], li, i, 0))])
    return pl.pallas_call(
        body, name=name, grid_spec=grid_spec, out_shape=[jax.ShapeDtypeStruct(theirs.shape, BF16)] * 2,
        compiler_params=_params(dimension_semantics=("arbitrary", "arbitrary", "arbitrary")),
    )(jnp.stack([c, chip]).astype(jnp.int32), g, theirs)


def _chip_scatter(sums, landing):
    n = len(sums)

    def body(*refs):
        src, dst = refs[:n], refs[2 * n:3 * n]
        send_sems, recv_sems = refs[3 * n:]
        x, y, c = _place()
        mine = 2 * x + y

        def copy(k, j, src_slot, dst_slot, to):
            return pltpu.make_async_remote_copy(src_ref=src[k].at[src_slot], dst_ref=dst[k].at[dst_slot],
                                                send_sem=send_sems.at[k, j], recv_sem=recv_sems.at[k, j],
                                                device_id=(*to, c), device_id_type=MESH)

        chips = _other_chips(x, y)
        for k in range(n):
            for j, (px, py) in enumerate(chips):
                copy(k, j, 2 * px + py, mine, (px, py)).start()
        for k in range(n):
            for j, (px, py) in enumerate(chips):
                copy(k, j, mine, 2 * px + py, (px, py)).wait_recv()
        for k in range(n):
            for j, (px, py) in enumerate(chips):
                copy(k, j, 2 * px + py, mine, (px, py)).wait_send()

    any_spec = pl.BlockSpec(memory_space=pl.ANY)
    return pl.pallas_call(
        body, name="chip_scatter", in_specs=[any_spec] * (2 * n), out_specs=[any_spec] * n,
        out_shape=[jax.ShapeDtypeStruct(a.shape, a.dtype) for a in landing],
        input_output_aliases={n + k: k for k in range(n)},
        scratch_shapes=[pltpu.SemaphoreType.DMA((n, 3)), pltpu.SemaphoreType.DMA((n, 3))],
        compiler_params=_params(),
    )(*sums, *landing)


def _sum_slots(name, parts, half=None):
    slots, l, r, c = parts.shape
    tr = _pick(r, (256, 176, 128, 64, 32, 8))

    def body(*refs):
        p_ref, o_ref = refs[-2:]
        acc = p_ref[0].astype(F32)
        for i in range(1, slots):
            acc = acc + p_ref[i].astype(F32)
        o_ref[...] = acc

    if half is None:
        return pl.pallas_call(
            body, name=name, grid=(l, r // tr), in_specs=[pl.BlockSpec((slots, 1, tr, c), lambda li, i: (0, li, i, 0))],
            out_specs=pl.BlockSpec((1, tr, c), lambda li, i: (li, i, 0)), out_shape=jax.ShapeDtypeStruct((l, r, c), F32),
            compiler_params=_params(dimension_semantics=("arbitrary", "arbitrary")),
        )(parts)
    grid_spec = pltpu.PrefetchScalarGridSpec(
        num_scalar_prefetch=1, grid=(l, r // tr),
        in_specs=[pl.BlockSpec((slots, 1, tr, c), lambda li, i, h: (0, li, i, 0))],
        out_specs=pl.BlockSpec((1, tr, c), lambda li, i, h: (li, h[0] * (r // tr) + i, 0)))
    return pl.pallas_call(
        body, name=name, grid_spec=grid_spec, out_shape=jax.ShapeDtypeStruct((l, 2 * r, c), F32),
        compiler_params=_params(dimension_semantics=("arbitrary", "arbitrary")),
    )(jnp.reshape(half, (1,)).astype(jnp.int32), parts)


def _swap_halves(blocks):
    n = len(blocks)

    def body(*refs):
        src, dst = refs[:n], refs[n:2 * n]
        send_sems, recv_sems = refs[2 * n:]
        x, y, c = _place()

        def half(ref, k, pc):
            r = blocks[k].shape[1] // 2
            return ref[k].at[:, pl.ds(pl.multiple_of(pc * r, 8), r), :]

        def copy(k, pc):
            return pltpu.make_async_remote_copy(src_ref=half(src, k, pc), dst_ref=half(dst, k, pc), send_sem=send_sems.at[k],
                                                recv_sem=recv_sems.at[k], device_id=(x, y, 1 - c), device_id_type=MESH)

        for k in range(n):
            copy(k, c).start()
        for k in range(n):
            copy(k, 1 - c).wait_recv()
        for k in range(n):
            copy(k, c).wait_send()

    any_spec = pl.BlockSpec(memory_space=pl.ANY)
    return pl.pallas_call(
        body, name="swap_halves", in_specs=[any_spec] * n, out_specs=[any_spec] * n,
        out_shape=[jax.ShapeDtypeStruct(b.shape, b.dtype) for b in blocks], input_output_aliases={k: k for k in range(n)},
        scratch_shapes=[pltpu.SemaphoreType.DMA((n,)), pltpu.SemaphoreType.DMA((n,))],
        compiler_params=_params(),
    )(*blocks)


def _adamw(name, w, g, m, v):
    shape = w.shape
    cols = shape[-1] if w.ndim > 1 and shape[-1] % LANES == 0 else w.size if w.size % LANES else LANES
    flat = lambda a: a.reshape(-1, cols)
    rows = w.size // cols
    tr = _pick(rows, [r for r in (512, 256, 128, 64, 32, 16, 8) if r * cols <= 256 * 1024]) if rows % 8 == 0 else rows

    def body(w_ref, g_ref, m_ref, v_ref, go_ref, d_ref, nm_ref, nv_ref):
        gg = g_ref[...]
        go_ref[...] = gg
        nm = ADAM_B1 * m_ref[...] + (1.0 - ADAM_B1) * gg
        nv = ADAM_B2 * v_ref[...] + (1.0 - ADAM_B2) * (gg * gg)
        m_hat = nm / (1.0 - ADAM_B1 ** ADAM_STEP)
        v_hat = nv / (1.0 - ADAM_B2 ** ADAM_STEP)
        d_ref[...] = -ADAM_LR * (m_hat / (jnp.sqrt(v_hat) + ADAM_EPS) + ADAM_WD * w_ref[...])
        nm_ref[...] = nm
        nv_ref[...] = nv

    spec = pl.BlockSpec((tr, cols), lambda i: (i, 0))
    out = pl.pallas_call(
        body, name=name, grid=(rows // tr,), in_specs=[spec] * 4, out_specs=[spec] * 4,
        out_shape=[jax.ShapeDtypeStruct((rows, cols), F32)] * 4,
        compiler_params=_params(dimension_semantics=("arbitrary",)),
    )(flat(w), flat(g), flat(m), flat(v))
    return tuple(o.reshape(shape) for o in out)


WEIGHTS = ["w_ada", "b_ada", "w_in", "w_sb_up", "ssm_a_re", "ssm_a_im", "ssm_log_dt", "ssm_b_re", "ssm_b_im", "ssm_c_re",
           "ssm_c_im", "ssm_d", "w_glu", "b_glu", "w_ssm_up", "w_out", "ln1_g", "ln1_b", "w_ffn_in", "w_ffn_out", "ln2_g",
           "ln2_b"]
COL_SPLIT = ["w_in", "w_sb_up", "w_ssm_up", "w_ffn_in"]
ROW_SPLIT = ["w_glu", "w_out", "w_ffn_out"]
SMALL = ["ssm_a_re", "ssm_a_im", "ssm_log_dt", "ssm_b_re", "ssm_b_im", "ssm_c_re", "ssm_c_im", "ssm_d", "b_glu", "ln1_g",
         "ln1_b", "ln2_g", "ln2_b"]
SLAB_COLS = 1024


def _cast_into_whole(name, w, by_rows, chip):
    l, r, cols = w.shape
    tr = _pick(r, (512, 256, 128, 64, 16))

    def body(q_ref, w_ref, o_ref):
        o_ref[...] = w_ref[...].astype(BF16)

    if by_rows:
        out_map, shape = (lambda li, i, q: (li, q[0] * (r // tr) + i, 0)), (l, 4 * r, cols)
    else:
        out_map, shape = (lambda li, i, q: (li, i, q[0])), (l, r, 4 * cols)
    grid_spec = pltpu.PrefetchScalarGridSpec(
        num_scalar_prefetch=1, grid=(l, r // tr), in_specs=[pl.BlockSpec((1, tr, cols), lambda li, i, q: (li, i, 0))],
        out_specs=pl.BlockSpec((1, tr, cols), out_map))
    return pl.pallas_call(body, name=name, grid_spec=grid_spec, out_shape=jax.ShapeDtypeStruct(shape, BF16),
                          compiler_params=_params(dimension_semantics=("arbitrary", "arbitrary")),
                          )(jnp.reshape(chip, (1,)).astype(jnp.int32), w)


def _silu_rows(name, c):
    def body(c_ref, o_ref):
        v = c_ref[...]
        o_ref[...] = v * jax.nn.sigmoid(v)

    return pl.pallas_call(body, name=name, out_shape=jax.ShapeDtypeStruct(c.shape, F32), compiler_params=_params())(c)


def _pad_rows(v, mult=8):
    flat = v.reshape(-1)
    per = mult * SLAB_COLS
    total = -(-flat.size // per) * per
    return jnp.pad(flat, (0, total - flat.size)).reshape(-1, SLAB_COLS)


def kernel(x, c, w_ada, b_ada, w_in, w_sb_up, ssm_a_re, ssm_a_im, ssm_log_dt, ssm_b_re, ssm_b_im, ssm_c_re, ssm_c_im, ssm_d, w_glu, b_glu, w_ssm_up, w_out, ln1_g, ln1_b, w_ffn_in, w_ffn_out, ln2_g, ln2_b, loss_target, m_w_ada, m_b_ada, m_w_in, m_w_sb_up, m_ssm_a_re, m_ssm_a_im, m_ssm_log_dt, m_ssm_b_re, m_ssm_b_im, m_ssm_c_re, m_ssm_c_im, m_ssm_d, m_w_glu, m_b_glu, m_w_ssm_up, m_w_out, m_ln1_g, m_ln1_b, m_w_ffn_in, m_w_ffn_out, m_ln2_g, m_ln2_b, v_w_ada, v_b_ada, v_w_in, v_w_sb_up, v_ssm_a_re, v_ssm_a_im, v_ssm_log_dt, v_ssm_b_re, v_ssm_b_im, v_ssm_c_re, v_ssm_c_im, v_ssm_d, v_w_glu, v_b_glu, v_w_ssm_up, v_w_out, v_ln1_g, v_ln1_b, v_w_ffn_in, v_w_ffn_out, v_ln2_g, v_ln2_b):
    args = dict(locals())
    w = {n: args[n] for n in WEIGHTS}
    mom = {n: args["m_" + n] for n in WEIGHTS}
    var = {n: args["v_" + n] for n in WEIGHTS}
    depth, d = w_ada.shape[0], x.shape[-1]
    xi, yi, ci = _place()
    me, chip = 4 * xi + 2 * yi + ci, 2 * xi + yi
    ada_cols = w_ada.shape[2]

    big = COL_SPLIT + ROW_SPLIT
    by_rows = [n in ROW_SPLIT for n in big]
    full = dict(zip(big, _gather_weights([_cast_into_whole(f"cast_{n}", w[n], n in ROW_SPLIT, chip) for n in big], by_rows)))

    c_all = _all_gather8("gather_c", jnp.pad(c, ((0, 7), (0, 0))))[::8]
    c_act = _silu_rows("silu_c", c_all)
    b_cols = lax.dynamic_slice_in_dim(b_ada, chip * ada_cols, ada_cols, axis=1)
    mod_part = jnp.concatenate([_small_mm(f"mod_{l}", c_act, w_ada[l], "nn") + b_cols[l][None] for l in range(depth)], axis=0)
    mod_all = _all_gather8("gather_mod", mod_part).reshape(4, 2, depth, 8, ada_cols)[:, 0]
    mod_mine = lax.dynamic_index_in_dim(mod_all, me, axis=2, keepdims=False)
    mod = mod_mine.transpose(1, 0, 2).reshape(depth, 6, d)

    layer_w = [{**{n: full[n][l] for n in big}, **{n: w[n][l] for n in SMALL}} for l in range(depth)]
    loss_cols, dx, dmods, lgrads, stacked = _local_step(x[0], loss_target[0], mod, layer_w)
    loss = lax.psum(jnp.sum(loss_cols), ("x", "y", "c"))
    grad_x = dx[None]

    theirs = _pair_exchange([stacked[n] for n in big], by_rows)
    pairs = [_pair_sum(f"pair_{n}", stacked[n], t, n in ROW_SPLIT, ci, chip) for n, t in zip(big, theirs)]
    landed = _chip_scatter([p[0] for p in pairs], [p[1] for p in pairs])
    halves = [_sum_slots(f"sum_{n}", p, half=ci) for n, p in zip(big, landed)]
    grad = dict(zip(big, _swap_halves(halves)))

    pieces = [dmods] + [jnp.stack([lgrads[l][n] for l in range(depth)]) for n in SMALL]
    slab = jnp.concatenate([_pad_rows(p) for p in pieces], axis=0)
    slabs = _all_gather8("gather_small", slab).reshape(8, 1, *slab.shape)
    total = _sum_slots("sum_small", slabs)[0]
    row = _pad_rows(pieces[0]).shape[0]
    for n, p in zip(SMALL, pieces[1:]):
        rows = _pad_rows(p).shape[0]
        grad[n] = total[row:row + rows].reshape(-1)[:p.size].reshape(p.shape)
        row += rows
    dmod_rows = _pad_rows(pieces[0]).shape[0]
    dmod_all = slabs[:, 0, :dmod_rows].reshape(8, -1)[:, :depth * 6 * d].reshape(8, depth, 4, ada_cols)
    dmod_cols = lax.dynamic_index_in_dim(dmod_all, chip, axis=2, keepdims=False)
    grad["w_ada"] = jnp.stack([_small_mm(f"dw_ada_{l}", c_act, dmod_cols[:, l], "tn") for l in range(depth)])
    dmod_sum = _sum_slots("sum_dmod", slabs[:, :, :dmod_rows])[0]
    grad["b_ada"] = dmod_sum.reshape(-1)[:depth * 6 * d].reshape(depth, 6 * d)

    delta, new_m, new_v = {}, {}, {}
    for n in WEIGHTS:
        grad[n], delta[n], new_m[n], new_v[n] = _adamw(f"adamw_{n}", w[n], grad[n], mom[n], var[n])
    return (loss, grad_x, *[grad[n] for n in WEIGHTS], *[delta[n] for n in WEIGHTS], *[new_m[n] for n in WEIGHTS],
            *[new_v[n] for n in WEIGHTS])
```

```python
import functools
import math

import jax
import jax.numpy as jnp
from jax import lax
from jax.experimental import pallas as pl
from jax.experimental.pallas import tpu as pltpu

F32 = jnp.float32
BF16 = jnp.bfloat16
MESH = pl.DeviceIdType.MESH

LANES = 128
HEAD_DIM = 64
SB_WIDTH = 512
ATT_TILE = 256
SSM_GROUPS, SSM_STATE, SSM_GROUP = 32, 64, 16
N_STATE = SSM_GROUPS * SSM_STATE
SSM_BLOCKS = SSM_GROUPS * SSM_GROUP // LANES
U_OFFSET = 3 * 512
LN_EPS = 1e-5
DEPTH = 2
ALPHA = (2 * DEPTH) ** 0.25
ADAM_LR, ADAM_B1, ADAM_B2, ADAM_EPS, ADAM_WD, ADAM_STEP = 0.001, 0.9, 0.999, 1e-08, 0.01, 10
VMEM_LIMIT = 56 * 1024 * 1024
GELU_K = math.sqrt(2.0 / math.pi)
GELU_C = 0.044715


def _params(**kw):
    return pltpu.CompilerParams(vmem_limit_bytes=VMEM_LIMIT, **kw)


def _pick(n, prefs):
    for p in prefs:
        if n % p == 0:
            return p
    return n


def _rowwise(name, fn, rows, vecs, outs, sums=(), tm=None):
    s = rows[0][0].shape[0]
    tm = tm or _pick(s, (256, 128, 64, 8))
    nin, no, ns = len(rows) + len(vecs), len(outs), len(sums)

    def body(*refs):
        res = fn(*[r[...].astype(F32) for r in refs[:nin]])
        res = res if isinstance(res, tuple) else (res,)
        for r, v in zip(refs[nin:nin + no], res[:no]):
            r[...] = v.astype(r.dtype)
        if ns:
            @pl.when(pl.program_id(0) == 0)
            def _():
                for r in refs[nin + no:]:
                    r[...] = jnp.zeros_like(r)
            for r, v in zip(refs[nin + no:], res[no:]):
                r[...] += v

    in_specs = [pl.BlockSpec((tm, w), lambda i, cb=cb: (i, cb)) for _, cb, w in rows]
    in_specs += [pl.BlockSpec(v.shape, lambda i: (0, 0)) for v in vecs]
    out_specs = [pl.BlockSpec((tm, w), lambda i: (i, 0)) for w, _ in outs]
    out_specs += [pl.BlockSpec((1, w), lambda i: (0, 0)) for w in sums]
    out_shape = [jax.ShapeDtypeStruct((s, w), dt) for w, dt in outs]
    out_shape += [jax.ShapeDtypeStruct((1, w), F32) for w in sums]
    res = pl.pallas_call(
        body, name=name, grid=(s // tm,), in_specs=in_specs, out_specs=out_specs, out_shape=out_shape,
        compiler_params=_params(dimension_semantics=("arbitrary",)),
    )(*[a for a, _, _ in rows], *vecs)
    return res[0] if len(res) == 1 else tuple(res)


MM_TILES = (1408, 1024, 512, 256, 128)


def _mm(name, a, b, mode, out_dtype=F32, into=None):
    if mode == "nn":
        m, k, n = a.shape[0], a.shape[1], b.shape[1]
    elif mode == "nt":
        m, k, n = a.shape[0], a.shape[1], b.shape[0]
    else:
        k, m, n = a.shape[0], a.shape[1], b.shape[1]
    tm = _pick(m, MM_TILES if mode == "tn" else (2048,) + MM_TILES[1:])
    tn = _pick(n, MM_TILES)
    tk = _pick(k, (2048,) + MM_TILES if mode == "tn" else MM_TILES)
    nk = k // tk
    dims = {"nn": ((1,), (0,)), "nt": ((1,), (1,)), "tn": ((0,), (0,))}[mode]

    def body(a_ref, b_ref, *rest):
        o_ref = rest[-2] if nk > 1 else rest[-1]
        prod = lax.dot_general(a_ref[...].astype(BF16), b_ref[...].astype(BF16), (dims, ((), ())),
                               preferred_element_type=F32)
        if nk == 1:
            o_ref[...] = prod.astype(o_ref.dtype)
            return
        acc_ref = rest[-1]
        kk = pl.program_id(2)

        @pl.when(kk == 0)
        def _():
            acc_ref[...] = prod

        @pl.when(kk > 0)
        def _():
            acc_ref[...] += prod

        @pl.when(kk == nk - 1)
        def _():
            o_ref[...] = acc_ref[...].astype(o_ref.dtype)

    if mode == "tn":
        a_spec = pl.BlockSpec((tk, tm), lambda i, j, kk: (kk, i))
    else:
        a_spec = pl.BlockSpec((tm, tk), lambda i, j, kk: (i, kk))
    if mode == "nt":
        b_spec = pl.BlockSpec((tn, tk), lambda i, j, kk: (j, kk))
    else:
        b_spec = pl.BlockSpec((tk, tn), lambda i, j, kk: (kk, j))
    in_specs, operands, aliases = [a_spec, b_spec], [a, b], {}
    if into is None:
        out_spec = pl.BlockSpec((tm, tn), lambda i, j, kk: (i, j))
        out_shape = jax.ShapeDtypeStruct((m, n), out_dtype)
    else:
        buf, slab, count = into
        out_spec = pl.BlockSpec((None, tm, tn), lambda i, j, kk: (slab, i, j))
        out_shape = jax.ShapeDtypeStruct((count, m, n), out_dtype)
        if buf is not None:
            in_specs.append(pl.BlockSpec(memory_space=pl.ANY))
            operands.append(buf)
            aliases = {2: 0}
    return pl.pallas_call(
        body, name=name, grid=(m // tm, n // tn, nk), in_specs=in_specs, out_specs=out_spec, out_shape=out_shape,
        scratch_shapes=[pltpu.VMEM((tm, tn), F32)] if nk > 1 else [], input_output_aliases=aliases,
        compiler_params=_params(dimension_semantics=("arbitrary", "arbitrary", "arbitrary")),
    )(*operands)


def _small_mm(name, a, b, mode):
    dims = {"nn": ((1,), (0,)), "tn": ((0,), (0,))}[mode]
    m = a.shape[0] if mode == "nn" else a.shape[1]

    def body(a_ref, b_ref, o_ref):
        o_ref[...] = lax.dot_general(a_ref[...], b_ref[...], (dims, ((), ())), precision=lax.Precision.HIGHEST,
                                     preferred_element_type=F32)

    return pl.pallas_call(body, name=name, out_shape=jax.ShapeDtypeStruct((m, b.shape[1]), F32),
                          compiler_params=_params())(a, b)


def _norm(x):
    mu = jnp.mean(x, axis=-1, keepdims=True)
    xc = x - mu
    rstd = lax.rsqrt(jnp.mean(xc * xc, axis=-1, keepdims=True) + LN_EPS)
    return xc * rstd, rstd


def _norm_bwd(dn, n, rstd):
    return rstd * (dn - jnp.mean(dn, axis=-1, keepdims=True) - n * jnp.mean(dn * n, axis=-1, keepdims=True))


def _colsum(v):
    return jnp.sum(v, axis=0, keepdims=True)


def _gelu(x):
    return 0.5 * x * (1.0 + jnp.tanh(GELU_K * (x + GELU_C * x * x * x)))


def _gelu_grad(x):
    t = jnp.tanh(GELU_K * (x + GELU_C * x * x * x))
    return 0.5 * (1.0 + t) + 0.5 * x * (1.0 - t * t) * GELU_K * (1.0 + 3.0 * GELU_C * x * x)


def _log_sigmoid_parts(z):
    lb = jnp.minimum(z, 0.0) - jnp.log(1.0 + jnp.exp(-jnp.abs(z)))
    return lb, lb - z


def _kv_transposed(proj, t):
    s = proj.shape[0]
    nb, nhp = s // t, SB_WIDTH // LANES

    def body(k_ref, v_ref, kt_ref, vt_ref):
        k, v = k_ref[...].astype(F32), v_ref[...].astype(F32)
        for hp in range(nhp):
            kt_ref[hp, 0] = k[:, hp * LANES:(hp + 1) * LANES].T.astype(BF16)
            vt_ref[hp, 0] = v[:, hp * LANES:(hp + 1) * LANES].T.astype(BF16)

    col = lambda cb: pl.BlockSpec((t, SB_WIDTH), lambda i, cb=cb: (i, cb))
    t_out = pl.BlockSpec((nhp, 1, LANES, t), lambda i: (0, i, 0, 0))
    return pl.pallas_call(
        body, name="kv_transposed", grid=(nb,), in_specs=[col(1), col(2)], out_specs=[t_out, t_out],
        out_shape=[jax.ShapeDtypeStruct((nhp, nb, LANES, t), BF16)] * 2,
        compiler_params=_params(dimension_semantics=("arbitrary",)),
    )(proj, proj)


def _tile_masks(t):
    row = lax.broadcasted_iota(jnp.int32, (t, t), 0)
    col = lax.broadcasted_iota(jnp.int32, (t, t), 1)
    return row, col


DEAD_LOG_WEIGHT = -110.0


def _walk_down(i, tiles, state, alive):
    st = lax.cond(i == 0, lambda s_: tiles([i], s_, [True]), lambda s_: tiles([i, i - 1], s_, [True, False]), state)
    n = jnp.maximum(i - 1, 0)

    def pair(c):
        return c[0] + 1, tiles([i - 2 - 2 * c[0], i - 3 - 2 * c[0]], c[1], [False, False])

    p, st = lax.while_loop(lambda c: (c[0] < n // 2) & alive(c[1]), pair, (jnp.int32(0), st))
    return lax.cond((n % 2 == 1) & (p == n // 2) & alive(st), lambda s_: tiles([0], s_, [False]), lambda s_: s_, st)


def _walk_up(i, first, tiles, state):
    n = jnp.maximum(i - 1 - first, 0)
    st = lax.fori_loop(0, n // 2, lambda p, s_: tiles([first + 2 * p, first + 2 * p + 1], s_, [False, False]), state)
    st = lax.cond(n % 2 == 1, lambda s_: tiles([i - 2], s_, [False]), lambda s_: s_, st)
    return lax.cond(i == 0, lambda s_: tiles([i], s_, [True]), lambda s_: tiles([i - 1, i], s_, [False, True]), st)


def _nt(a, b):
    return lax.dot_general(a, b, (((1,), (1,)), ((), ())), preferred_element_type=F32)


def _nn(a, b):
    return jnp.dot(a, b, preferred_element_type=F32)


def _attn_fwd(proj, vt3, t):
    s = proj.shape[0]
    nb, nhp = s // t, SB_WIDTH // LANES

    def body(q_ref, k_ref, vt_ref, o_ref, car_ref):
        i = pl.program_id(1)
        q2 = q_ref[...] * (1.0 / math.sqrt(HEAD_DIM))
        lane_q = lax.broadcasted_iota(jnp.int32, q2.shape, 1)
        row, col = _tile_masks(t)
        later = (col > row).astype(BF16)
        valid = row < col
        orow = lax.broadcasted_iota(jnp.int32, (LANES, t), 0)
        car_ref[...] = jnp.full(car_ref.shape, 2.0 * DEAD_LOG_WEIGHT, F32)
        qh = [jnp.where((lane_q < HEAD_DIM) == (hh == 0), q2, jnp.zeros_like(q2)) for hh in range(2)]

        def tiles(js, state, diagonal):
            chains = [(n, hh) for n in range(len(js)) for hh in range(2)]
            kb = [k_ref[pl.ds(pl.multiple_of(j * t, t), t), :] for j in js]
            z = {ch: _nt(kb[ch[0]], qh[ch[1]]) for ch in chains}
            lb, aft, csum = {}, {}, {}
            for ch in chains:
                lb[ch], l1m = _log_sigmoid_parts(z[ch])
                if diagonal[ch[0]]:
                    l1m = jnp.where(valid, l1m, 0.0)
                aft[ch] = _nn(later, l1m.astype(BF16))
                csum[ch] = _colsum(l1m)
            state = list(state)
            for ch in chains:
                n, hh = ch
                c_after, acc = state[hh]
                w = jnp.exp(lb[ch] + aft[ch] + c_after)
                if diagonal[ch[0]]:
                    w = jnp.where(valid, w, 0.0)
                car_ref[hh, pl.ds(js[n], 1), :] = c_after
                state[hh] = (c_after + csum[ch], acc + _nn(vt_ref[0, js[n]], w.astype(BF16)))
            return tuple(state)

        def alive(state):
            return jnp.max(jnp.maximum(state[0][0], state[1][0])) >= DEAD_LOG_WEIGHT

        zero = (jnp.zeros((1, t), F32), jnp.zeros((LANES, t), F32))
        (_, acc0), (_, acc1) = _walk_down(i, tiles, (zero, zero), alive)
        o_ref[...] = jnp.where(orow < HEAD_DIM, acc0, acc1).T.astype(o_ref.dtype)

    return pl.pallas_call(
        body, name="attn_fwd", grid=(nhp, nb),
        in_specs=[pl.BlockSpec((t, LANES), lambda hp, i: (i, hp)),
                  pl.BlockSpec((s, LANES), lambda hp, i: (0, nhp + hp)),
                  pl.BlockSpec((1, nb, LANES, t), lambda hp, i: (hp, 0, 0, 0))],
        out_specs=[pl.BlockSpec((t, LANES), lambda hp, i: (i, hp)),
                   pl.BlockSpec((2, nb, t), lambda hp, i: (hp, 0, i))],
        out_shape=[jax.ShapeDtypeStruct((s, nhp * LANES), BF16), jax.ShapeDtypeStruct((2 * nhp, nb, s), F32)],
        compiler_params=_params(dimension_semantics=("arbitrary", "arbitrary")),
    )(proj, proj, vt3)


def _attn_bwd(proj, do, kt3, car, t):
    s = proj.shape[0]
    nb, nhp = s // t, SB_WIDTH // LANES

    def body(q_ref, do_ref, k_ref, v_ref, kt_ref, car_ref, dq_ref, dk_ref, dv_ref):
        i = pl.program_id(1)

        @pl.when(i == 0)
        def _():
            dk_ref[...] = jnp.zeros_like(dk_ref)
            dv_ref[...] = jnp.zeros_like(dv_ref)

        q2, do2 = q_ref[...] * (1.0 / math.sqrt(HEAD_DIM)), do_ref[...]
        lane_q = lax.broadcasted_iota(jnp.int32, q2.shape, 1)
        row, col = _tile_masks(t)
        later = (col > row).astype(BF16)
        earlier = (col < row).astype(BF16)
        valid = row < col
        orow = lax.broadcasted_iota(jnp.int32, (LANES, t), 0)
        head = [(lane_q < HEAD_DIM) == (hh == 0) for hh in range(2)]
        qh = [jnp.where(hm, q2, jnp.zeros_like(q2)) for hm in head]
        doh = [jnp.where(hm, do2, jnp.zeros_like(do2)) for hm in head]

        def tiles(js, state, diagonal):
            chains = [(n, hh) for n in range(len(js)) for hh in range(2)]
            rows = [pl.ds(pl.multiple_of(j * t, t), t) for j in js]
            kb = [k_ref[r, :] for r in rows]
            vb = [v_ref[r, :] for r in rows]
            z = {ch: _nt(kb[ch[0]], qh[ch[1]]) for ch in chains}
            dw = {ch: _nt(vb[ch[0]], doh[ch[1]]) for ch in chains}
            lb, beta, aft = {}, {}, {}
            for ch in chains:
                lb[ch], l1m = _log_sigmoid_parts(z[ch])
                beta[ch] = jnp.exp(lb[ch])
                if diagonal[ch[0]]:
                    l1m = jnp.where(valid, l1m, 0.0)
                aft[ch] = _nn(later, l1m.astype(BF16))
            w, g, gsum, g_in = {}, {}, {}, {}
            for ch in chains:
                n, hh = ch
                w[ch] = jnp.exp(lb[ch] + aft[ch] + car_ref[hh, pl.ds(js[n], 1), :])
                if diagonal[ch[0]]:
                    w[ch] = jnp.where(valid, w[ch], 0.0)
                g[ch] = dw[ch] * w[ch]
                g_in[ch] = _nn(earlier, g[ch].astype(BF16))
                gsum[ch] = _colsum(g[ch])
            state = list(state)
            dk_t, dv_t = [None] * len(js), [None] * len(js)
            for ch in chains:
                n, hh = ch
                c_g, dqt = state[hh]
                dz = g[ch] - beta[ch] * (g[ch] + g_in[ch] + c_g)
                if diagonal[ch[0]]:
                    dz = jnp.where(valid, dz, 0.0)
                dzb, wb = dz.astype(BF16), w[ch].astype(BF16)
                dk_h, dv_h = _nn(dzb, qh[hh]), _nn(wb, doh[hh])
                dk_t[n] = dk_h if dk_t[n] is None else dk_t[n] + dk_h
                dv_t[n] = dv_h if dv_t[n] is None else dv_t[n] + dv_h
                state[hh] = (c_g + gsum[ch], dqt + _nn(kt_ref[0, js[n]], dzb))
            for n in range(len(js)):
                dk_ref[rows[n], :] += dk_t[n]
                dv_ref[rows[n], :] += dv_t[n]
            return tuple(state)

        reach = jnp.max(jnp.max(car_ref[...], axis=2, keepdims=True), axis=0)
        dead = (reach < DEAD_LOG_WEIGHT) & (lax.broadcasted_iota(jnp.int32, reach.shape, 0) < i)
        first = jnp.sum(jnp.where(dead, 1.0, 0.0)).astype(jnp.int32)
        zero = (jnp.zeros((1, t), F32), jnp.zeros((LANES, t), F32))
        (_, dq0), (_, dq1) = _walk_up(i, first, tiles, (zero, zero))
        dq_ref[...] = jnp.where(orow < HEAD_DIM, dq0, dq1).T

    tile_spec = pl.BlockSpec((t, LANES), lambda hp, i: (i, hp))
    whole = pl.BlockSpec((s, LANES), lambda hp, i: (0, hp))
    return pl.pallas_call(
        body, name="attn_bwd", grid=(nhp, nb),
        in_specs=[tile_spec, tile_spec, pl.BlockSpec((s, LANES), lambda hp, i: (0, nhp + hp)),
                  pl.BlockSpec((s, LANES), lambda hp, i: (0, 2 * nhp + hp)),
                  pl.BlockSpec((1, nb, LANES, t), lambda hp, i: (hp, 0, 0, 0)),
                  pl.BlockSpec((2, nb, t), lambda hp, i: (hp, 0, i))],
        out_specs=[tile_spec, whole, whole],
        out_shape=[jax.ShapeDtypeStruct((s, nhp * LANES), F32)] * 3,
        compiler_params=_params(dimension_semantics=("arbitrary", "arbitrary")),
    )(proj, do, proj, proj, kt3, car)


SCAN_LANES = 1024
SCAN_ROWS = 8
S5_CHUNKS = 4


def _scan_chunks(v):
    n = v.shape[1] // (2 * LANES)
    return [(v[:, c * 2 * LANES:c * 2 * LANES + LANES], v[:, c * 2 * LANES + LANES:(c + 1) * 2 * LANES]) for c in range(n)]


def _scan_tables(lr, li, reverse):
    if reverse:
        li = -li
    row = lax.broadcasted_iota(jnp.int32, (SCAN_ROWS, LANES), 0)
    powers = [(lr, li)]
    for _ in range(SCAN_ROWS - 1):
        pr, pi = powers[-1]
        powers.append((pr * lr - pi * li, pr * li + pi * lr))
    levels = []
    for d in (1, 2, 4):
        keep = (row < SCAN_ROWS - d) if reverse else (row >= d)
        levels.append((SCAN_ROWS - d if reverse else d,
                       (jnp.where(keep, powers[d - 1][0], 0.0), jnp.where(keep, powers[d - 1][1], 0.0))))
    pr = pi = jnp.zeros((SCAN_ROWS, LANES), F32)
    for r in range(SCAN_ROWS):
        steps = SCAN_ROWS - r if reverse else r + 1
        pr = jnp.where(row == r, powers[steps - 1][0], pr)
        pi = jnp.where(row == r, powers[steps - 1][1], pi)
    return levels, (pr, pi)


def _s5_fwd(proj, u_off, bmat, cmat, lam):
    s, w = proj.shape[0], SSM_BLOCKS * bmat.shape[1]
    tt = _pick(s, (512, 256, 128, 8))
    nt = s // tt
    cin = bmat.shape[0] // SSM_BLOCKS
    chunk = tt // S5_CHUNKS

    def body(u_ref, b_ref, c_ref, lam_ref, h_ref, y_ref, x_ref, st_ref):
        @pl.when(pl.program_id(1) == 0)
        def _():
            st_ref[...] = jnp.zeros_like(st_ref)

        tables = [_scan_tables(lr, li, reverse=False) for lr, li in _scan_chunks(lam_ref[...])]

        def project(k):
            x_ref[k * chunk:(k + 1) * chunk, :] = _nn(u_ref[k * chunk:(k + 1) * chunk, :].astype(BF16), b_ref[...])

        def tile(r0, last):
            last, parts = list(last), []
            for c, (xr, xi) in enumerate(_scan_chunks(x_ref[r0:r0 + SCAN_ROWS, :])):
                levels, (pr, pi) = tables[c]
                for d, (ar, ai) in levels:
                    sr, si = pltpu.roll(xr, d, 0), pltpu.roll(xi, d, 0)
                    xr, xi = xr + ar * sr - ai * si, xi + ar * si + ai * sr
                br, bi = last[2 * c], last[2 * c + 1]
                hr = xr + pr * br - pi * bi
                hi = xi + pr * bi + pi * br
                last[2 * c], last[2 * c + 1] = hr[SCAN_ROWS - 1:], hi[SCAN_ROWS - 1:]
                parts += [hr, hi]
            h_ref[r0:r0 + SCAN_ROWS, :] = jnp.concatenate(parts, axis=1)
            return tuple(last)

        st = st_ref[0:1, :]
        last = tuple(st[:, c * LANES:(c + 1) * LANES] for c in range(SCAN_LANES // LANES))
        project(0)
        for k in range(S5_CHUNKS):
            if k + 1 < S5_CHUNKS:
                project(k + 1)
            for r0 in range(k * chunk, (k + 1) * chunk, SCAN_ROWS):
                last = tile(r0, last)
            y_ref[k * chunk:(k + 1) * chunk, :] = _nn(h_ref[k * chunk:(k + 1) * chunk, :].astype(BF16), c_ref[...])
        st_ref[0:1, :] = jnp.concatenate(last, axis=1)

    return pl.pallas_call(
        body, name="s5_fwd", grid=(SSM_BLOCKS, nt),
        in_specs=[pl.BlockSpec((tt, cin), lambda kb, i: (i, u_off // cin + kb)),
                  pl.BlockSpec((cin, SCAN_LANES), lambda kb, i: (kb, 0)),
                  pl.BlockSpec((SCAN_LANES, cin), lambda kb, i: (kb, 0)),
                  pl.BlockSpec((1, SCAN_LANES), lambda kb, i: (0, kb))],
        out_specs=[pl.BlockSpec((tt, SCAN_LANES), lambda kb, i: (i, kb)), pl.BlockSpec((tt, cin), lambda kb, i: (i, kb))],
        out_shape=[jax.ShapeDtypeStruct((s, w), F32), jax.ShapeDtypeStruct((s, bmat.shape[0]), F32)],
        scratch_shapes=[pltpu.VMEM((tt, SCAN_LANES), F32), pltpu.VMEM((SCAN_ROWS, SCAN_LANES), F32)],
        compiler_params=_params(dimension_semantics=("arbitrary", "arbitrary")),
    )(proj, bmat, cmat, lam)


def _s5_bwd(dy, h, proj, u_off, bmat, cmat, lam):
    s, w = h.shape
    tt = _pick(s, (512, 256, 128, 8))
    nt = s // tt
    cin = bmat.shape[0] // SSM_BLOCKS
    chunk = tt // S5_CHUNKS

    def body(dy_ref, h_ref, u_ref, b_ref, c_ref, lam_ref, du_ref, dlam_ref, db_ref, dc_ref, e_ref, a_ref, st_ref):
        @pl.when(pl.program_id(1) == 0)
        def _():
            st_ref[...] = jnp.zeros_like(st_ref)
            dlam_ref[...] = jnp.zeros_like(dlam_ref)
            db_ref[...] = jnp.zeros_like(db_ref)
            dc_ref[...] = jnp.zeros_like(dc_ref)

        tables = [_scan_tables(lr, li, reverse=True) for lr, li in _scan_chunks(lam_ref[...])]
        nch = len(tables)
        row = lax.broadcasted_iota(jnp.int32, (SCAN_ROWS, LANES), 0)
        rows_first = (((0,), (0,)), ((), ()))

        def project(k):
            e_ref[k * chunk:(k + 1) * chunk, :] = _nt(dy_ref[k * chunk:(k + 1) * chunk, :], c_ref[...])

        def finish(k):
            rows = slice(k * chunk, (k + 1) * chunk)
            adj = a_ref[rows, :].astype(BF16)
            du_ref[rows, :] = _nt(adj, b_ref[...])
            db_ref[...] += lax.dot_general(u_ref[rows, :].astype(BF16), adj, rows_first, preferred_element_type=F32)
            dc_ref[...] += lax.dot_general(h_ref[rows, :].astype(BF16), dy_ref[rows, :], rows_first, preferred_element_type=F32)

        def tile(r0, carry):
            e_c = _scan_chunks(e_ref[r0:r0 + SCAN_ROWS, :])
            h_c = _scan_chunks(h_ref[r0:r0 + SCAN_ROWS, :])
            carry, parts = list(carry), []
            for c in range(nch):
                (yr, yi), (hr, hi) = e_c[c], h_c[c]
                levels, (pr, pi) = tables[c]
                for shift, (lr, li) in levels:
                    sr, si = pltpu.roll(yr, shift, 0), pltpu.roll(yi, shift, 0)
                    yr, yi = yr + lr * sr - li * si, yi + lr * si + li * sr
                nr, ni, dr, di = carry[4 * c:4 * c + 4]
                ar = yr + pr * nr - pi * ni
                ai = yi + pr * ni + pi * nr
                nxr = jnp.where(row == SCAN_ROWS - 1, nr, pltpu.roll(ar, SCAN_ROWS - 1, 0))
                nxi = jnp.where(row == SCAN_ROWS - 1, ni, pltpu.roll(ai, SCAN_ROWS - 1, 0))
                carry[4 * c:4 * c + 4] = [ar[0:1], ai[0:1], dr + nxr * hr + nxi * hi, di + nxi * hr - nxr * hi]
                parts += [ar, ai]
            a_ref[r0:r0 + SCAN_ROWS, :] = jnp.concatenate(parts, axis=1)
            return tuple(carry)

        st, dl = st_ref[0:1, :], dlam_ref[...]
        init = []
        for c in range(nch):
            lo = c * 2 * LANES
            init += [st[:, lo:lo + LANES], st[:, lo + LANES:lo + 2 * LANES],
                     dl[:, lo:lo + LANES], dl[:, lo + LANES:lo + 2 * LANES]]
        fin = tuple(init)
        project(S5_CHUNKS - 1)
        for k in reversed(range(S5_CHUNKS)):
            if k > 0:
                project(k - 1)
            for r0 in reversed(range(k * chunk, (k + 1) * chunk, SCAN_ROWS)):
                fin = tile(r0, fin)
            finish(k)
        st_ref[0:1, :] = jnp.concatenate([fin[4 * c + q] for c in range(nch) for q in (0, 1)], axis=1)
        dlam_ref[...] = jnp.concatenate([fin[4 * c + q] for c in range(nch) for q in (2, 3)], axis=1)

        @pl.when(pl.program_id(1) == nt - 1)
        def _():
            dlam_ref[0:1, :] = jnp.sum(dlam_ref[...], axis=0, keepdims=True)

    def rev(width, col):
        return pl.BlockSpec((tt, width), lambda kb, i: (nt - 1 - i, col(kb)))

    return pl.pallas_call(
        body, name="s5_bwd", grid=(SSM_BLOCKS, nt),
        in_specs=[rev(cin, lambda kb: kb), rev(SCAN_LANES, lambda kb: kb), rev(cin, lambda kb: u_off // cin + kb),
                  pl.BlockSpec((cin, SCAN_LANES), lambda kb, i: (kb, 0)),
                  pl.BlockSpec((SCAN_LANES, cin), lambda kb, i: (kb, 0)),
                  pl.BlockSpec((1, SCAN_LANES), lambda kb, i: (0, kb))],
        out_specs=[rev(cin, lambda kb: kb), pl.BlockSpec((SCAN_ROWS, SCAN_LANES), lambda kb, i: (0, kb)),
                   pl.BlockSpec((cin, SCAN_LANES), lambda kb, i: (kb, 0)), pl.BlockSpec((SCAN_LANES, cin), lambda kb, i: (kb, 0))],
        out_shape=[jax.ShapeDtypeStruct((s, bmat.shape[0]), F32), jax.ShapeDtypeStruct((SCAN_ROWS, w), F32),
                   jax.ShapeDtypeStruct((bmat.shape[0], SCAN_LANES), F32), jax.ShapeDtypeStruct((w, cin), F32)],
        scratch_shapes=[pltpu.VMEM((tt, SCAN_LANES), F32), pltpu.VMEM((tt, SCAN_LANES), F32),
                        pltpu.VMEM((SCAN_ROWS, SCAN_LANES), F32)],
        compiler_params=_params(dimension_semantics=("arbitrary", "arbitrary")),
    )(dy, h, proj, bmat, cmat, lam)


def _ssm_params_fwd(a_re, a_im, log_dt, b_re, b_im):
    def body(ar_ref, ai_ref, ldt_ref, br_ref, bi_ref, lr_ref, li_ref, bbr_ref, bbi_ref):
        ar, ai, dt = ar_ref[...], ai_ref[...], jnp.exp(ldt_ref[...])
        mag = jnp.exp(ar * dt)
        lr, li = mag * jnp.cos(ai * dt), mag * jnp.sin(ai * dt)
        den = ar * ar + ai * ai
        cr = ((lr - 1.0) * ar + li * ai) / den
        ci = (li * ar - (lr - 1.0) * ai) / den
        br, bi = br_ref[...], bi_ref[...]
        lr_ref[...], li_ref[...] = lr, li
        bbr_ref[...] = cr * br - ci * bi
        bbi_ref[...] = cr * bi + ci * br

    n = a_re.shape[0]
    v1, v16 = jax.ShapeDtypeStruct((n, 1), F32), jax.ShapeDtypeStruct((n, SSM_GROUP), F32)
    return pl.pallas_call(body, name="ssm_params_fwd", out_shape=[v1, v1, v16, v16],
                          compiler_params=_params())(a_re, a_im, log_dt, b_re, b_im)


def _ssm_params_bwd(a_re, a_im, log_dt, b_re, b_im, g_lr, g_li, g_bbr, g_bbi):
    n = a_re.shape[0]

    def body(ar_ref, ai_ref, ldt_ref, br_ref, bi_ref, glr_ref, gli_ref, gbr_ref, gbi_ref,
             dar_ref, dai_ref, dldt_ref, dbr_ref, dbi_ref):
        ar, ai, dt = ar_ref[...], ai_ref[...], jnp.exp(ldt_ref[...])
        mag = jnp.exp(ar * dt)
        lr, li = mag * jnp.cos(ai * dt), mag * jnp.sin(ai * dt)
        den = ar * ar + ai * ai
        cr = ((lr - 1.0) * ar + li * ai) / den
        ci = (li * ar - (lr - 1.0) * ai) / den
        br, bi, gbr, gbi = br_ref[...], bi_ref[...], gbr_ref[...], gbi_ref[...]
        dbr_ref[...] = gbr * cr + gbi * ci
        dbi_ref[...] = gbi * cr - gbr * ci
        gcr = jnp.sum(gbr * br + gbi * bi, axis=1, keepdims=True)
        gci = jnp.sum(gbi * br - gbr * bi, axis=1, keepdims=True)
        ir, ii = ar / den, -ai / den
        glr = glr_ref[...] + gcr * ir + gci * ii
        gli = gli_ref[...] + gci * ir - gcr * ii
        qr, qi = cr * ir - ci * ii, cr * ii + ci * ir
        gar = -(gcr * qr + gci * qi)
        gai = -(gci * qr - gcr * qi)
        gxr = glr * lr + gli * li
        gxi = gli * lr - glr * li
        dar_ref[...] = gar + gxr * dt
        dai_ref[...] = gai + gxi * dt
        gdt = (gxr * ar + gxi * ai) * dt
        rowg = lax.broadcasted_iota(jnp.int32, (n, SSM_GROUPS), 0) // SSM_STATE
        colg = lax.broadcasted_iota(jnp.int32, (n, SSM_GROUPS), 1)
        dldt_ref[...] = jnp.sum(jnp.where(rowg == colg, gdt, 0.0), axis=0, keepdims=True)

    v1, v16 = jax.ShapeDtypeStruct((n, 1), F32), jax.ShapeDtypeStruct((n, SSM_GROUP), F32)
    return pl.pallas_call(body, name="ssm_params_bwd",
                          out_shape=[v1, v1, jax.ShapeDtypeStruct((1, SSM_GROUPS), F32), v16, v16],
                          compiler_params=_params())(a_re, a_im, log_dt, b_re, b_im, g_lr, g_li, g_bbr, g_bbi)


def _interleave(re, im, axis):
    shp = list(re.shape)
    new = shp[:axis] + [shp[axis] // LANES, LANES] + shp[axis + 1:]
    st = jnp.stack([re.reshape(new), im.reshape(new)], axis=axis + 1)
    return st.reshape(shp[:axis] + [2 * shp[axis]] + shp[axis + 1:])


def _deinterleave(v, axis):
    shp = list(v.shape)
    r = v.reshape(shp[:axis] + [shp[axis] // (2 * LANES), 2, LANES] + shp[axis + 1:])
    out = shp[:axis] + [shp[axis] // 2] + shp[axis + 1:]
    return (lax.index_in_dim(r, 0, axis + 1, keepdims=False).reshape(out),
            lax.index_in_dim(r, 1, axis + 1, keepdims=False).reshape(out))


def _b_matrix(bbr, bbi):
    per = SSM_GROUPS // SSM_BLOCKS
    eye = jnp.eye(per, dtype=F32)

    def blockdiag(v):
        x = v.reshape(SSM_BLOCKS, per, SSM_STATE, SSM_GROUP).transpose(0, 1, 3, 2)
        return (eye[None, :, None, :, None] * x[:, :, :, None, :]).reshape(SSM_GROUPS * SSM_GROUP, per * SSM_STATE)

    return _interleave(blockdiag(bbr), blockdiag(bbi), 1)


def _diag_blocks(v, rows, cols):
    per = SSM_GROUPS // SSM_BLOCKS
    return jnp.stack([v[g * rows:(g + 1) * rows, (g % per) * cols:(g % per + 1) * cols] for g in range(SSM_GROUPS)])


def _b_matrix_grad(d):
    def diag(v):
        return _diag_blocks(v, SSM_GROUP, SSM_STATE).transpose(0, 2, 1).reshape(N_STATE, SSM_GROUP)

    dr, di = _deinterleave(d, 1)
    return diag(dr), diag(di)


def _c_matrix(c_re, c_im):
    per = SSM_GROUPS // SSM_BLOCKS
    eye = jnp.eye(per, dtype=F32)

    def blockdiag(v):
        x = v.reshape(SSM_BLOCKS, per, SSM_GROUP, SSM_STATE).transpose(0, 1, 3, 2)
        return (x[:, :, :, None, :] * eye[None, :, None, :, None]).reshape(N_STATE, per * SSM_GROUP)

    return _interleave(blockdiag(c_re), blockdiag(-c_im), 0)


def _c_matrix_grad(d):
    def diag(v):
        return _diag_blocks(v, SSM_STATE, SSM_GROUP).transpose(0, 2, 1)

    dr, di = _deinterleave(d, 0)
    return diag(dr), -diag(di)


def _row(v):
    return v.reshape(1, -1)


def _ssm_inputs(p):
    rows = lambda v: v.reshape(N_STATE, -1)
    ldt = jnp.repeat(p["ssm_log_dt"], SSM_STATE).reshape(N_STATE, 1)
    return rows(p["ssm_a_re"]), rows(p["ssm_a_im"]), ldt, rows(p["ssm_b_re"]), rows(p["ssm_b_im"])


def _lnmod(x, sc, sh):
    return _norm(x)[0] * (1.0 + sc) + sh


def _resid_ln(x, y, g, lg, lb):
    return _norm(ALPHA * x + (1.0 + g) * y)[0] * lg + lb


def _resid_ln_lnmod(x, y, g, lg, lb, sc, sh):
    xo = _resid_ln(x, y, g, lg, lb)
    return xo, _lnmod(xo, sc, sh)


def _layer_fwd(x, h1, mod, p, tag, next_mod):
    d = x.shape[1]
    sh_m, sc_m, g_m, sh_f, sc_f, g_f = [_row(mod[i]) for i in range(6)]
    nm = lambda s: f"{s}_{tag}"
    proj = _mm(nm("proj"), h1, p["w_in"], "nn", out_dtype=BF16)
    t = min(ATT_TILE, x.shape[0])
    kt3, vt3 = _kv_transposed(proj, t)
    att, car = _attn_fwd(proj, vt3, t)
    y_sb = _mm(nm("sb_up"), att, p["w_sb_up"], "nn", out_dtype=BF16)

    lam_r, lam_i, bbr, bbi = _ssm_params_fwd(*_ssm_inputs(p))
    lam = _interleave(lam_r.reshape(1, N_STATE), lam_i.reshape(1, N_STATE), 1)
    bmat = _b_matrix(bbr, bbi).astype(BF16)
    cmat = _c_matrix(p["ssm_c_re"], p["ssm_c_im"]).astype(BF16)
    hst, yc = _s5_fwd(proj, U_OFFSET, bmat, cmat, lam)

    def ssm_act(yc, u, dsk):
        y0 = yc + dsk * u
        return y0, _gelu(y0)

    y0, y1 = _rowwise(nm("ssm_act"), ssm_act, [(yc, 0, 512), (proj, 3, 512)], [_row(p["ssm_d"])], [(512, F32), (512, F32)])
    gl = _mm(nm("glu"), y1, p["w_glu"], "nn")
    y2 = _rowwise(nm("glu_act"), lambda y1, gl, b: y1 * jax.nn.sigmoid(gl + b), [(y1, 0, 512), (gl, 0, 512)],
                  [_row(p["b_glu"])], [(512, BF16)])
    y_ssm = _mm(nm("ssm_up"), y2, p["w_ssm_up"], "nn", out_dtype=BF16)

    def merge(gsb, gss, ysb, yss):
        return jax.nn.sigmoid(gsb) * ysb + jax.nn.sigmoid(gss) * yss

    merged = _rowwise(nm("merge"), merge, [(proj, 2, d), (proj, 3, d), (y_sb, 0, d), (y_ssm, 0, d)], [], [(d, BF16)])
    y = _mm(nm("out"), merged, p["w_out"], "nn")

    x1, h2 = _rowwise(nm("ln1"), _resid_ln_lnmod, [(x, 0, d), (y, 0, d)],
                      [g_m, _row(p["ln1_g"]), _row(p["ln1_b"]), sc_f, sh_f], [(d, F32), (d, BF16)])
    f = _mm(nm("ffn_in"), h2, p["w_ffn_in"], "nn", out_dtype=BF16)
    fh = f.shape[1] // 2
    act = _rowwise(nm("swiglu"), lambda g, u: g * jax.nn.sigmoid(g) * u, [(f, 0, fh), (f, 1, fh)], [], [(fh, BF16)])
    yf = _mm(nm("ffn_out"), act, p["w_ffn_out"], "nn")
    x2 = h1_next = None
    if next_mod is not None:
        x2, h1_next = _rowwise(nm("ln2"), _resid_ln_lnmod, [(x1, 0, d), (yf, 0, d)],
                               [g_f, _row(p["ln2_g"]), _row(p["ln2_b"]), next_mod[1], next_mod[0]], [(d, F32), (d, BF16)])
    saved = dict(x=x, h1=h1, proj=proj, kt3=kt3, car=car, att=att, y_sb=y_sb, lam=lam, bmat=bmat,
                 cmat=cmat, hst=hst, y0=y0, y1=y1, gl=gl, y2=y2, y_ssm=y_ssm, merged=merged, y=y, x1=x1, h2=h2, f=f,
                 act=act, yf=yf, t=t)
    return x2, h1_next, saved


def _resid_ln_bwd(x, y, dxo, g, lg):
    n, rstd = _norm(ALPHA * x + (1.0 + g) * y)
    dr = _norm_bwd(dxo * lg, n, rstd)
    return ALPHA * dr, (1.0 + g) * dr, _colsum(dxo * n), _colsum(dxo), _colsum(dr * y)


def _lnmod_bwd(x, dh, dxa, sc):
    n, rstd = _norm(x)
    return dxa + _norm_bwd(dh * (1.0 + sc), n, rstd), _colsum(dh * n), _colsum(dh)


def _lnmod_resid_ln_bwd(xo, dh, dxa, x, y, sc, g, lg):
    dxo, dsc, dsh = _lnmod_bwd(xo, dh, dxa, sc)
    dx, dy, dlg, dlb, dg = _resid_ln_bwd(x, y, dxo, g, lg)
    return dx, dy, dsc, dsh, dlg, dlb, dg


def _layer_bwd(dx1a, dyf, mod, p, sv, layer, depth, stacked):
    d = dx1a.shape[1]
    sh_m, sc_m, g_m, sh_f, sc_f, g_f = [_row(mod[i]) for i in range(6)]
    nm = lambda s: f"{s}_{layer}"
    grads = {}

    def weight_grad(n, a, b, **kw):
        grads[n] = _mm(nm("d" + n), a, b, "tn", out_dtype=BF16, into=(stacked.get(n), layer, depth), **kw)

    dact = _mm(nm("d_act"), dyf, p["w_ffn_out"], "nt", out_dtype=BF16)
    weight_grad("w_ffn_out", sv["act"], dyf)
    fh = sv["f"].shape[1] // 2

    def swiglu_bwd(g, u, da):
        sg = jax.nn.sigmoid(g)
        return jnp.concatenate([da * u * sg * (1.0 + g * (1.0 - sg)), da * g * sg], axis=1)

    df = _rowwise(nm("swiglu_bwd"), swiglu_bwd, [(sv["f"], 0, fh), (sv["f"], 1, fh), (dact, 0, fh)], [], [(2 * fh, BF16)])
    dh2 = _mm(nm("d_h2"), df, p["w_ffn_in"], "nt")
    weight_grad("w_ffn_in", sv["h2"], df)
    dxa, dy, dsc_f, dsh_f, grads["ln1_g"], grads["ln1_b"], dg_m = _rowwise(
        nm("ln1_bwd"), _lnmod_resid_ln_bwd, [(sv["x1"], 0, d), (dh2, 0, d), (dx1a, 0, d), (sv["x"], 0, d), (sv["y"], 0, d)],
        [sc_f, g_m, _row(p["ln1_g"])], [(d, F32), (d, BF16)], [d] * 5, tm=128)
    dmerged = _mm(nm("d_merged"), dy, p["w_out"], "nt", out_dtype=BF16)
    weight_grad("w_out", sv["merged"], dy)

    def merge_bwd(gsb, gss, ysb, yss, dm):
        s1, s2 = jax.nn.sigmoid(gsb), jax.nn.sigmoid(gss)
        return s1 * dm, s2 * dm, dm * ysb * s1 * (1.0 - s1), dm * yss * s2 * (1.0 - s2)

    dy_sb, dy_ssm, dg_sb, dg_ssm = _rowwise(
        nm("merge_bwd"), merge_bwd, [(sv["proj"], 2, d), (sv["proj"], 3, d), (sv["y_sb"], 0, d), (sv["y_ssm"], 0, d),
                                     (dmerged, 0, d)], [], [(d, BF16)] * 4)
    dy2 = _mm(nm("d_y2"), dy_ssm, p["w_ssm_up"], "nt")
    weight_grad("w_ssm_up", sv["y2"], dy_ssm)

    def glu_act_bwd(y1, gl, dy2, b):
        sg = jax.nn.sigmoid(gl + b)
        dgl = dy2 * y1 * sg * (1.0 - sg)
        return dy2 * sg, dgl, _colsum(dgl)

    dy1a, dgl, grads["b_glu"] = _rowwise(nm("glu_act_bwd"), glu_act_bwd, [(sv["y1"], 0, 512), (sv["gl"], 0, 512), (dy2, 0, 512)],
                                         [_row(p["b_glu"])], [(512, F32), (512, BF16)], [512])
    dy1b = _mm(nm("d_y1"), dgl, p["w_glu"], "nt")
    weight_grad("w_glu", sv["y1"], dgl)

    def ssm_act_bwd(y0, u, dy1a, dy1b, dsk):
        dy0 = (dy1a + dy1b) * _gelu_grad(y0)
        return dy0, dsk * dy0, _colsum(dy0 * u)

    dy0, du_a, grads["ssm_d"] = _rowwise(nm("ssm_act_bwd"), ssm_act_bwd,
                                         [(sv["y0"], 0, 512), (sv["proj"], 3, 512), (dy1a, 0, 512), (dy1b, 0, 512)],
                                         [_row(p["ssm_d"])], [(512, BF16), (512, F32)], [512])
    du_b, dlam, d_bmat, d_cmat = _s5_bwd(dy0, sv["hst"], sv["proj"], U_OFFSET, sv["bmat"], sv["cmat"], sv["lam"])
    grads["ssm_c_re"], grads["ssm_c_im"] = _c_matrix_grad(d_cmat)
    g_bbr, g_bbi = _b_matrix_grad(d_bmat)
    g_lr, g_li = _deinterleave(dlam[0:1], 1)
    da_re, da_im, dldt, db_re, db_im = _ssm_params_bwd(*_ssm_inputs(p), g_lr.reshape(N_STATE, 1), g_li.reshape(N_STATE, 1),
                                                       g_bbr, g_bbi)
    grads["ssm_a_re"] = da_re.reshape(SSM_GROUPS, SSM_STATE)
    grads["ssm_a_im"] = da_im.reshape(SSM_GROUPS, SSM_STATE)
    grads["ssm_log_dt"] = dldt.reshape(SSM_GROUPS)
    grads["ssm_b_re"] = db_re.reshape(SSM_GROUPS, SSM_STATE, SSM_GROUP)
    grads["ssm_b_im"] = db_im.reshape(SSM_GROUPS, SSM_STATE, SSM_GROUP)
    datt = _mm(nm("d_att"), dy_sb, p["w_sb_up"], "nt", out_dtype=BF16)
    weight_grad("w_sb_up", sv["att"], dy_sb)
    dqs, dk, dv = _attn_bwd(sv["proj"], datt, sv["kt3"], sv["car"], sv["t"])

    def dproj_cols(dqs, dk, dv, dua, dub, dgsb, dgss):
        return jnp.concatenate([dqs * (1.0 / math.sqrt(HEAD_DIM)), dk, dv, dua + dub, dgsb.astype(F32), dgss.astype(F32)],
                               axis=1)

    dproj = _rowwise(nm("dproj"), dproj_cols, [(dqs, 0, 512), (dk, 0, 512), (dv, 0, 512), (du_a, 0, 512), (du_b, 0, 512),
                                               (dg_sb, 0, d), (dg_ssm, 0, d)], [], [(2048 + 2 * d, BF16)])
    dh1 = _mm(nm("d_h1"), dproj, p["w_in"], "nt")
    weight_grad("w_in", sv["h1"], dproj)
    for k in ("ln1_g", "ln1_b", "ssm_d", "b_glu"):
        grads[k] = grads[k].reshape(-1)
    return dh1, dxa, grads, (dg_m, dsh_f, dsc_f)


def _local_step(x, target, mod, layer_w):
    depth, d = len(layer_w), x.shape[1]
    rows = lambda l: [_row(mod[l][i]) for i in range(6)]
    h1 = _rowwise("lnmod1_0", _lnmod, [(x, 0, d)], [rows(0)[1], rows(0)[0]], [(d, BF16)])
    xs, saved = x, []
    for l in range(depth):
        xs, h1, sv = _layer_fwd(xs, h1, mod[l], layer_w[l], str(l), rows(l + 1)[:2] if l + 1 < depth else None)
        saved.append(sv)

    def head_bwd(x1, yf, tgt, g, lg, lb):
        err = _resid_ln(x1, yf, g, lg, lb) - tgt
        return _resid_ln_bwd(x1, yf, err * (1.0 / d), g, lg) + (_colsum(err * err) * (0.5 / d),)

    def boundary_bwd(dh, dxa, x1, yf, sc, g, lg, lb):
        dxo, dsc, dsh = _lnmod_bwd(_resid_ln(x1, yf, g, lg, lb), dh, dxa, sc)
        return _resid_ln_bwd(x1, yf, dxo, g, lg) + (dsc, dsh)

    lgrads, sums, stacked = [None] * depth, [dict() for _ in range(depth)], {}
    last, p = saved[-1], layer_w[-1]
    dx1a, dyf, dlg, dlb, dg_f, loss_cols = _rowwise(
        "head_bwd", head_bwd, [(last["x1"], 0, d), (last["yf"], 0, d), (target, 0, d)],
        [rows(depth - 1)[5], _row(p["ln2_g"]), _row(p["ln2_b"])], [(d, F32), (d, BF16)], [d] * 4)
    for l in reversed(range(depth)):
        sums[l]["g_f"] = dg_f
        dh1, dxa, lgrads[l], (sums[l]["g_m"], sums[l]["sh_f"], sums[l]["sc_f"]) = _layer_bwd(
            dx1a, dyf, mod[l], layer_w[l], saved[l], l, depth, stacked)
        lgrads[l]["ln2_g"], lgrads[l]["ln2_b"] = dlg.reshape(-1), dlb.reshape(-1)
        stacked = {n: lgrads[l][n] for n in COL_SPLIT + ROW_SPLIT}
        if l > 0:
            prev, p = saved[l - 1], layer_w[l - 1]
            dx1a, dyf, dlg, dlb, dg_f, sums[l]["sc_m"], sums[l]["sh_m"] = _rowwise(
                f"boundary_bwd_{l}", boundary_bwd, [(dh1, 0, d), (dxa, 0, d), (prev["x1"], 0, d), (prev["yf"], 0, d)],
                [rows(l)[1], rows(l - 1)[5], _row(p["ln2_g"]), _row(p["ln2_b"])], [(d, F32), (d, BF16)], [d] * 5, tm=128)
        else:
            dx, sums[l]["sc_m"], sums[l]["sh_m"] = _rowwise("lnmod1_bwd", _lnmod_bwd, [(x, 0, d), (dh1, 0, d), (dxa, 0, d)],
                                                            [rows(0)[1]], [(d, F32)], [d, d])
    dmod = jnp.stack([jnp.concatenate([sums[l][k] for k in ("sh_m", "sc_m", "g_m", "sh_f", "sc_f", "g_f")], axis=0)
                      for l in range(depth)])
    return loss_cols, dx, dmod, lgrads, stacked


def _place():
    return lax.axis_index("x"), lax.axis_index("y"), lax.axis_index("c")


def _all_gather8(name, block):
    m_per, n = block.shape

    def body(x_ref, out_ref, send_sems, recv_sems, local_sem):
        x, y, c = _place()
        me, sibling = (x, y, c), (x, y, 1 - c)
        chips = [(1 - x, y), (x, 1 - y), (1 - x, 1 - y)]

        def rows(px, py, pc):
            return out_ref.at[pl.ds(pl.multiple_of((4 * px + 2 * py + pc) * m_per, 8), m_per), :]

        def copy(k, blk, to, src=None):
            return pltpu.make_async_remote_copy(src_ref=rows(*blk) if src is None else src, dst_ref=rows(*blk),
                                                send_sem=send_sems.at[k], recv_sem=recv_sems.at[k],
                                                device_id=to, device_id_type=MESH)

        mine = pltpu.make_async_copy(x_ref, rows(*me), local_sem)
        mine.start()
        first = [copy(0, me, sibling, src=x_ref)] + [copy(1 + j, me, (*chip, c), src=x_ref) for j, chip in enumerate(chips)]
        for cp in first:
            cp.start()
        passed = [copy(4 + j, (*chip, c), sibling) for j, chip in enumerate(chips)]
        for j, chip in enumerate(chips):
            copy(1 + j, (*chip, c), me).wait_recv()
            passed[j].start()
        copy(0, sibling, me).wait_recv()
        for j, chip in enumerate(chips):
            copy(4 + j, (*chip, 1 - c), me).wait_recv()
        for cp in first + passed:
            cp.wait_send()
        mine.wait()

    return pl.pallas_call(
        body, name=name, out_shape=jax.ShapeDtypeStruct((8 * m_per, n), block.dtype),
        in_specs=[pl.BlockSpec(memory_space=pltpu.VMEM)], out_specs=pl.BlockSpec(memory_space=pltpu.VMEM),
        scratch_shapes=[pltpu.SemaphoreType.DMA((7,)), pltpu.SemaphoreType.DMA((7,)), pltpu.SemaphoreType.DMA],
        compiler_params=_params(),
    )(block)


def _other_chips(x, y):
    return [(1 - x, y), (x, 1 - y), (1 - x, 1 - y)]


def _gather_weights(whole, by_rows):
    n = len(whole)

    def body(*refs):
        dst = refs[n:2 * n]
        ici_send, ici_recv, d2d_send, d2d_recv = refs[2 * n:]
        x, y, c = _place()
        chips = _other_chips(x, y)

        def part(ref, k, px, py, pc):
            _, r, cols = whole[k].shape
            q = 2 * px + py
            if by_rows[k]:
                return ref[k].at[:, pl.ds(pl.multiple_of((2 * q + pc) * (r // 8), 16), r // 8), :]
            return ref[k].at[:, pl.ds(pl.multiple_of(pc * (r // 2), 16), r // 2),
                             pl.ds(pl.multiple_of(q * (cols // 4), LANES), cols // 4)]

        def ici(k, j, px, py, to):
            return pltpu.make_async_remote_copy(src_ref=part(dst, k, px, py, c), dst_ref=part(dst, k, px, py, c),
                                                send_sem=ici_send.at[k, j], recv_sem=ici_recv.at[k, j],
                                                device_id=(*to, c), device_id_type=MESH)

        def d2d(k, j, px, py, pc):
            return pltpu.make_async_remote_copy(src_ref=part(dst, k, px, py, pc), dst_ref=part(dst, k, px, py, pc),
                                                send_sem=d2d_send.at[k, j], recv_sem=d2d_recv.at[k, j],
                                                device_id=(x, y, 1 - c), device_id_type=MESH)

        for k in range(n):
            for j, chip in enumerate(chips):
                ici(k, j, x, y, chip).start()
        for k in range(n):
            for j, chip in enumerate(chips):
                ici(k, j, *chip, chip).wait_recv()
                d2d(k, j, *chip, c).start()
        for k in range(n):
            for j, chip in enumerate(chips):
                d2d(k, j, *chip, 1 - c).wait_recv()
        for k in range(n):
            for j, chip in enumerate(chips):
                ici(k, j, x, y, chip).wait_send()
                d2d(k, j, *chip, c).wait_send()

    any_spec = pl.BlockSpec(memory_space=pl.ANY)
    return pl.pallas_call(
        body, name="gather_weights", in_specs=[any_spec] * n, out_specs=[any_spec] * n,
        out_shape=[jax.ShapeDtypeStruct(a.shape, a.dtype) for a in whole], input_output_aliases={k: k for k in range(n)},
        scratch_shapes=[pltpu.SemaphoreType.DMA((n, 3))] * 4,
        compiler_params=_params(),
    )(*whole)


def _part_shape(shape, by_rows):
    l, r, c = shape
    return (l, r // 8, c) if by_rows else (l, r // 2, c // 4)


def _pair_exchange(grads, by_rows):
    n = len(grads)

    def body(*refs):
        src, dst = refs[:n], refs[n:2 * n]
        send_sems, recv_sems = refs[2 * n:]
        x, y, c = _place()

        def window(k, q, pc):
            _, hr, hc = _part_shape(grads[k].shape, by_rows[k])
            if by_rows[k]:
                return src[k].at[:, pl.ds(pl.multiple_of((2 * q + pc) * hr, 16), hr), :]
            return src[k].at[:, pl.ds(pl.multiple_of(pc * hr, 16), hr), pl.ds(q * hc, hc)]

        def copy(k, q, pc):
            return pltpu.make_async_remote_copy(src_ref=window(k, q, pc), dst_ref=dst[k].at[q], send_sem=send_sems.at[k, q],
                                                recv_sem=recv_sems.at[k, q], device_id=(x, y, 1 - c), device_id_type=MESH)

        for k in range(n):
            for q in range(4):
                copy(k, q, 1 - c).start()
        for k in range(n):
            for q in range(4):
                copy(k, q, c).wait_recv()
        for k in range(n):
            for q in range(4):
                copy(k, q, 1 - c).wait_send()

    any_spec = pl.BlockSpec(memory_space=pl.ANY)
    return pl.pallas_call(
        body, name="pair_exchange", in_specs=[any_spec] * n, out_specs=[any_spec] * n,
        out_shape=[jax.ShapeDtypeStruct((4, *_part_shape(g.shape, rows)), g.dtype) for g, rows in zip(grads, by_rows)],
        scratch_shapes=[pltpu.SemaphoreType.DMA((n, 4)), pltpu.SemaphoreType.DMA((n, 4))],
        compiler_params=_params(),
    )(*grads)


def _pair_sum(name, g, theirs, by_rows, c, chip):
    _, l, hr, hc = theirs.shape
    tr = _pick(hr, (256, 176, 128, 64, 32))

    def body(s_ref, g_ref, t_ref, p_ref, own_ref):
        v = (g_ref[...].astype(F32) + t_ref[0].astype(F32)).astype(BF16)
        p_ref[0] = v

        @pl.when(pl.program_id(2) == s_ref[1])
        def _():
            own_ref[0] = v

    if by_rows:
        g_spec = pl.BlockSpec((1, tr, hc), lambda li, i, q, s: (li, (2 * q + s[0]) * (hr // tr) + i, 0))
    else:
        g_spec = pl.BlockSpec((1, tr, hc), lambda li, i, q, s: (li, s[0] * (hr // tr) + i, q))
    slot = pl.BlockSpec((1, 1, tr, hc), lambda li, i, q, s: (q, li, i, 0))
    grid_spec = pltpu.PrefetchScalarGridSpec(
        num_scalar_prefetch=1, grid=(l, hr // tr, 4), in_specs=[g_spec, slot],
        out_specs=[slot, pl.BlockSpec((1, 1, tr, hc), lambda li, i, q, s: (s[1], li, i, 0))])
    return pl.pallas_call(
        body, name=name, grid_spec=grid_spec, out_shape=[jax.ShapeDtypeStruct(theirs.shape, BF16)] * 2,
        compiler_params=_params(dimension_semantics=("arbitrary", "arbitrary", "arbitrary")),
    )(jnp.stack([c, chip]).astype(jnp.int32), g, theirs)


def _chip_scatter(sums, landing):
    n = len(sums)

    def body(*refs):
        src, dst = refs[:n], refs[2 * n:3 * n]
        send_sems, recv_sems = refs[3 * n:]
        x, y, c = _place()
        mine = 2 * x + y

        def copy(k, j, src_slot, dst_slot, to):
            return pltpu.make_async_remote_copy(src_ref=src[k].at[src_slot], dst_ref=dst[k].at[dst_slot],
                                                send_sem=send_sems.at[k, j], recv_sem=recv_sems.at[k, j],
                                                device_id=(*to, c), device_id_type=MESH)

        chips = _other_chips(x, y)
        for k in range(n):
            for j, (px, py) in enumerate(chips):
                copy(k, j, 2 * px + py, mine, (px, py)).start()
        for k in range(n):
            for j, (px, py) in enumerate(chips):
                copy(k, j, mine, 2 * px + py, (px, py)).wait_recv()
        for k in range(n):
            for j, (px, py) in enumerate(chips):
                copy(k, j, 2 * px + py, mine, (px, py)).wait_send()

    any_spec = pl.BlockSpec(memory_space=pl.ANY)
    return pl.pallas_call(
        body, name="chip_scatter", in_specs=[any_spec] * (2 * n), out_specs=[any_spec] * n,
        out_shape=[jax.ShapeDtypeStruct(a.shape, a.dtype) for a in landing],
        input_output_aliases={n + k: k for k in range(n)},
        scratch_shapes=[pltpu.SemaphoreType.DMA((n, 3)), pltpu.SemaphoreType.DMA((n, 3))],
        compiler_params=_params(),
    )(*sums, *landing)


def _sum_slots(name, parts, half=None):
    slots, l, r, c = parts.shape
    tr = _pick(r, (256, 176, 128, 64, 32, 8))

    def body(*refs):
        p_ref, o_ref = refs[-2:]
        acc = p_ref[0].astype(F32)
        for i in range(1, slots):
            acc = acc + p_ref[i].astype(F32)
        o_ref[...] = acc

    if half is None:
        return pl.pallas_call(
            body, name=name, grid=(l, r // tr), in_specs=[pl.BlockSpec((slots, 1, tr, c), lambda li, i: (0, li, i, 0))],
            out_specs=pl.BlockSpec((1, tr, c), lambda li, i: (li, i, 0)), out_shape=jax.ShapeDtypeStruct((l, r, c), F32),
            compiler_params=_params(dimension_semantics=("arbitrary", "arbitrary")),
        )(parts)
    grid_spec = pltpu.PrefetchScalarGridSpec(
        num_scalar_prefetch=1, grid=(l, r // tr),
        in_specs=[pl.BlockSpec((slots, 1, tr, c), lambda li, i, h: (0, li, i, 0))],
        out_specs=pl.BlockSpec((1, tr, c), lambda li, i, h: (li, h[0] * (r // tr) + i, 0)))
    return pl.pallas_call(
        body, name=name, grid_spec=grid_spec, out_shape=jax.ShapeDtypeStruct((l, 2 * r, c), F32),
        compiler_params=_params(dimension_semantics=("arbitrary", "arbitrary")),
    )(jnp.reshape(half, (1,)).astype(jnp.int32), parts)


def _swap_halves(blocks):
    n = len(blocks)

    def body(*refs):
        src, dst = refs[:n], refs[n:2 * n]
        send_sems, recv_sems = refs[2 * n:]
        x, y, c = _place()

        def half(ref, k, pc):
            r = blocks[k].shape[1] // 2
            return ref[k].at[:, pl.ds(pl.multiple_of(pc * r, 8), r), :]

        def copy(k, pc):
            return pltpu.make_async_remote_copy(src_ref=half(src, k, pc), dst_ref=half(dst, k, pc), send_sem=send_sems.at[k],
                                                recv_sem=recv_sems.at[k], device_id=(x, y, 1 - c), device_id_type=MESH)

        for k in range(n):
            copy(k, c).start()
        for k in range(n):
            copy(k, 1 - c).wait_recv()
        for k in range(n):
            copy(k, c).wait_send()

    any_spec = pl.BlockSpec(memory_space=pl.ANY)
    return pl.pallas_call(
        body, name="swap_halves", in_specs=[any_spec] * n, out_specs=[any_spec] * n,
        out_shape=[jax.ShapeDtypeStruct(b.shape, b.dtype) for b in blocks], input_output_aliases={k: k for k in range(n)},
        scratch_shapes=[pltpu.SemaphoreType.DMA((n,)), pltpu.SemaphoreType.DMA((n,))],
        compiler_params=_params(),
    )(*blocks)


def _adamw(name, w, g, m, v):
    shape = w.shape
    cols = shape[-1] if w.ndim > 1 and shape[-1] % LANES == 0 else w.size if w.size % LANES else LANES
    flat = lambda a: a.reshape(-1, cols)
    rows = w.size // cols
    tr = _pick(rows, [r for r in (512, 256, 128, 64, 32, 16, 8) if r * cols <= 256 * 1024]) if rows % 8 == 0 else rows

    def body(w_ref, g_ref, m_ref, v_ref, go_ref, d_ref, nm_ref, nv_ref):
        gg = g_ref[...]
        go_ref[...] = gg
        nm = ADAM_B1 * m_ref[...] + (1.0 - ADAM_B1) * gg
        nv = ADAM_B2 * v_ref[...] + (1.0 - ADAM_B2) * (gg * gg)
        m_hat = nm / (1.0 - ADAM_B1 ** ADAM_STEP)
        v_hat = nv / (1.0 - ADAM_B2 ** ADAM_STEP)
        d_ref[...] = -ADAM_LR * (m_hat / (jnp.sqrt(v_hat) + ADAM_EPS) + ADAM_WD * w_ref[...])
        nm_ref[...] = nm
        nv_ref[...] = nv

    spec = pl.BlockSpec((tr, cols), lambda i: (i, 0))
    out = pl.pallas_call(
        body, name=name, grid=(rows // tr,), in_specs=[spec] * 4, out_specs=[spec] * 4,
        out_shape=[jax.ShapeDtypeStruct((rows, cols), F32)] * 4,
        compiler_params=_params(dimension_semantics=("arbitrary",)),
    )(flat(w), flat(g), flat(m), flat(v))
    return tuple(o.reshape(shape) for o in out)


WEIGHTS = ["w_ada", "b_ada", "w_in", "w_sb_up", "ssm_a_re", "ssm_a_im", "ssm_log_dt", "ssm_b_re", "ssm_b_im", "ssm_c_re",
           "ssm_c_im", "ssm_d", "w_glu", "b_glu", "w_ssm_up", "w_out", "ln1_g", "ln1_b", "w_ffn_in", "w_ffn_out", "ln2_g",
           "ln2_b"]
COL_SPLIT = ["w_in", "w_sb_up", "w_ssm_up", "w_ffn_in"]
ROW_SPLIT = ["w_glu", "w_out", "w_ffn_out"]
SMALL = ["ssm_a_re", "ssm_a_im", "ssm_log_dt", "ssm_b_re", "ssm_b_im", "ssm_c_re", "ssm_c_im", "ssm_d", "b_glu", "ln1_g",
         "ln1_b", "ln2_g", "ln2_b"]
SLAB_COLS = 1024


def _cast_into_whole(name, w, by_rows, chip):
    l, r, cols = w.shape
    tr = _pick(r, (512, 256, 128, 64, 16))

    def body(q_ref, w_ref, o_ref):
        o_ref[...] = w_ref[...].astype(BF16)

    if by_rows:
        out_map, shape = (lambda li, i, q: (li, q[0] * (r // tr) + i, 0)), (l, 4 * r, cols)
    else:
        out_map, shape = (lambda li, i, q: (li, i, q[0])), (l, r, 4 * cols)
    grid_spec = pltpu.PrefetchScalarGridSpec(
        num_scalar_prefetch=1, grid=(l, r // tr), in_specs=[pl.BlockSpec((1, tr, cols), lambda li, i, q: (li, i, 0))],
        out_specs=pl.BlockSpec((1, tr, cols), out_map))
    return pl.pallas_call(body, name=name, grid_spec=grid_spec, out_shape=jax.ShapeDtypeStruct(shape, BF16),
                          compiler_params=_params(dimension_semantics=("arbitrary", "arbitrary")),
                          )(jnp.reshape(chip, (1,)).astype(jnp.int32), w)


def _silu_rows(name, c):
    def body(c_ref, o_ref):
        v = c_ref[...]
        o_ref[...] = v * jax.nn.sigmoid(v)

    return pl.pallas_call(body, name=name, out_shape=jax.ShapeDtypeStruct(c.shape, F32), compiler_params=_params())(c)


def _pad_rows(v, mult=8):
    flat = v.reshape(-1)
    per = mult * SLAB_COLS
    total = -(-flat.size // per) * per
    return jnp.pad(flat, (0, total - flat.size)).reshape(-1, SLAB_COLS)


def kernel(x, c, w_ada, b_ada, w_in, w_sb_up, ssm_a_re, ssm_a_im, ssm_log_dt, ssm_b_re, ssm_b_im, ssm_c_re, ssm_c_im, ssm_d, w_glu, b_glu, w_ssm_up, w_out, ln1_g, ln1_b, w_ffn_in, w_ffn_out, ln2_g, ln2_b, loss_target, m_w_ada, m_b_ada, m_w_in, m_w_sb_up, m_ssm_a_re, m_ssm_a_im, m_ssm_log_dt, m_ssm_b_re, m_ssm_b_im, m_ssm_c_re, m_ssm_c_im, m_ssm_d, m_w_glu, m_b_glu, m_w_ssm_up, m_w_out, m_ln1_g, m_ln1_b, m_w_ffn_in, m_w_ffn_out, m_ln2_g, m_ln2_b, v_w_ada, v_b_ada, v_w_in, v_w_sb_up, v_ssm_a_re, v_ssm_a_im, v_ssm_log_dt, v_ssm_b_re, v_ssm_b_im, v_ssm_c_re, v_ssm_c_im, v_ssm_d, v_w_glu, v_b_glu, v_w_ssm_up, v_w_out, v_ln1_g, v_ln1_b, v_w_ffn_in, v_w_ffn_out, v_ln2_g, v_ln2_b):
    args = dict(locals())
    w = {n: args[n] for n in WEIGHTS}
    mom = {n: args["m_" + n] for n in WEIGHTS}
    var = {n: args["v_" + n] for n in WEIGHTS}
    depth, d = w_ada.shape[0], x.shape[-1]
    xi, yi, ci = _place()
    me, chip = 4 * xi + 2 * yi + ci, 2 * xi + yi
    ada_cols = w_ada.shape[2]

    big = COL_SPLIT + ROW_SPLIT
    by_rows = [n in ROW_SPLIT for n in big]
    full = dict(zip(big, _gather_weights([_cast_into_whole(f"cast_{n}", w[n], n in ROW_SPLIT, chip) for n in big], by_rows)))

    c_all = _all_gather8("gather_c", jnp.pad(c, ((0, 7), (0, 0))))[::8]
    c_act = _silu_rows("silu_c", c_all)
    b_cols = lax.dynamic_slice_in_dim(b_ada, chip * ada_cols, ada_cols, axis=1)
    mod_part = jnp.concatenate([_small_mm(f"mod_{l}", c_act, w_ada[l], "nn") + b_cols[l][None] for l in range(depth)], axis=0)
    mod_all = _all_gather8("gather_mod", mod_part).reshape(4, 2, depth, 8, ada_cols)[:, 0]
    mod_mine = lax.dynamic_index_in_dim(mod_all, me, axis=2, keepdims=False)
    mod = mod_mine.transpose(1, 0, 2).reshape(depth, 6, d)

    layer_w = [{**{n: full[n][l] for n in big}, **{n: w[n][l] for n in SMALL}} for l in range(depth)]
    loss_cols, dx, dmods, lgrads, stacked = _local_step(x[0], loss_target[0], mod, layer_w)
    loss = lax.psum(jnp.sum(loss_cols), ("x", "y", "c"))
    grad_x = dx[None]

    theirs = _pair_exchange([stacked[n] for n in big], by_rows)
    pairs = [_pair_sum(f"pair_{n}", stacked[n], t, n in ROW_SPLIT, ci, chip) for n, t in zip(big, theirs)]
    landed = _chip_scatter([p[0] for p in pairs], [p[1] for p in pairs])
    halves = [_sum_slots(f"sum_{n}", p, half=ci) for n, p in zip(big, landed)]
    grad = dict(zip(big, _swap_halves(halves)))

    pieces = [dmods] + [jnp.stack([lgrads[l][n] for l in range(depth)]) for n in SMALL]
    slab = jnp.concatenate([_pad_rows(p) for p in pieces], axis=0)
    slabs = _all_gather8("gather_small", slab).reshape(8, 1, *slab.shape)
    total = _sum_slots("sum_small", slabs)[0]
    row = _pad_rows(pieces[0]).shape[0]
    for n, p in zip(SMALL, pieces[1:]):
        rows = _pad_rows(p).shape[0]
        grad[n] = total[row:row + rows].reshape(-1)[:p.size].reshape(p.shape)
        row += rows
    dmod_rows = _pad_rows(pieces[0]).shape[0]
    dmod_all = slabs[:, 0, :dmod_rows].reshape(8, -1)[:, :depth * 6 * d].reshape(8, depth, 4, ada_cols)
    dmod_cols = lax.dynamic_index_in_dim(dmod_all, chip, axis=2, keepdims=False)
    grad["w_ada"] = jnp.stack([_small_mm(f"dw_ada_{l}", c_act, dmod_cols[:, l], "tn") for l in range(depth)])
    dmod_sum = _sum_slots("sum_dmod", slabs[:, :, :dmod_rows])[0]
    grad["b_ada"] = dmod_sum.reshape(-1)[:depth * 6 * d].reshape(depth, 6 * d)

    delta, new_m, new_v = {}, {}, {}
    for n in WEIGHTS:
        grad[n], delta[n], new_m[n], new_v[n] = _adamw(f"adamw_{n}", w[n], grad[n], mom[n], var[n])
    return (loss, grad_x, *[grad[n] for n in WEIGHTS], *[delta[n] for n in WEIGHTS], *[new_m[n] for n in WEIGHTS],
            *[new_v[n] for n in WEIGHTS])
```

```python
import functools
import math

import jax
import jax.numpy as jnp
from jax import lax
from jax.experimental import pallas as pl
from jax.experimental.pallas import tpu as pltpu

F32 = jnp.float32
BF16 = jnp.bfloat16
MESH = pl.DeviceIdType.MESH

LANES = 128
HEAD_DIM = 64
SB_WIDTH = 512
ATT_TILE = 256
SSM_GROUPS, SSM_STATE, SSM_GROUP = 32, 64, 16
N_STATE = SSM_GROUPS * SSM_STATE
SSM_BLOCKS = SSM_GROUPS * SSM_GROUP // LANES
U_OFFSET = 3 * 512
LN_EPS = 1e-5
DEPTH = 2
ALPHA = (2 * DEPTH) ** 0.25
ADAM_LR, ADAM_B1, ADAM_B2, ADAM_EPS, ADAM_WD, ADAM_STEP = 0.001, 0.9, 0.999, 1e-08, 0.01, 10
VMEM_LIMIT = 56 * 1024 * 1024
GELU_K = math.sqrt(2.0 / math.pi)
GELU_C = 0.044715


def _params(**kw):
    return pltpu.CompilerParams(vmem_limit_bytes=VMEM_LIMIT, **kw)


def _pick(n, prefs):
    for p in prefs:
        if n % p == 0:
            return p
    return n


def _rowwise(name, fn, rows, vecs, outs, sums=(), tm=None):
    s = rows[0][0].shape[0]
    tm = tm or _pick(s, (256, 128, 64, 8))
    nin, no, ns = len(rows) + len(vecs), len(outs), len(sums)

    def body(*refs):
        res = fn(*[r[...].astype(F32) for r in refs[:nin]])
        res = res if isinstance(res, tuple) else (res,)
        for r, v in zip(refs[nin:nin + no], res[:no]):
            r[...] = v.astype(r.dtype)
        if ns:
            @pl.when(pl.program_id(0) == 0)
            def _():
                for r in refs[nin + no:]:
                    r[...] = jnp.zeros_like(r)
            for r, v in zip(refs[nin + no:], res[no:]):
                r[...] += v

    in_specs = [pl.BlockSpec((tm, w), lambda i, cb=cb: (i, cb)) for _, cb, w in rows]
    in_specs += [pl.BlockSpec(v.shape, lambda i: (0, 0)) for v in vecs]
    out_specs = [pl.BlockSpec((tm, w), lambda i: (i, 0)) for w, _ in outs]
    out_specs += [pl.BlockSpec((1, w), lambda i: (0, 0)) for w in sums]
    out_shape = [jax.ShapeDtypeStruct((s, w), dt) for w, dt in outs]
    out_shape += [jax.ShapeDtypeStruct((1, w), F32) for w in sums]
    res = pl.pallas_call(
        body, name=name, grid=(s // tm,), in_specs=in_specs, out_specs=out_specs, out_shape=out_shape,
        compiler_params=_params(dimension_semantics=("arbitrary",)),
    )(*[a for a, _, _ in rows], *vecs)
    return res[0] if len(res) == 1 else tuple(res)


MM_TILES = (1408, 1024, 512, 256, 128)


def _slab_spec(block, index, slab):
    if slab is None:
        return pl.BlockSpec(block, index)
    return pl.BlockSpec((None, *block), lambda *g: (slab, *index(*g)))


def _mm(name, a, b, mode, out_dtype=F32, into=None):
    b, b_slab = b if isinstance(b, tuple) else (b, None)
    if mode == "nn":
        m, k, n = a.shape[0], a.shape[1], b.shape[-1]
    elif mode == "nt":
        m, k, n = a.shape[0], a.shape[1], b.shape[-2]
    else:
        k, m, n = a.shape[0], a.shape[1], b.shape[-1]
    tm = _pick(m, MM_TILES if mode == "tn" else (2048,) + MM_TILES[1:])
    tn = _pick(n, MM_TILES)
    tk = _pick(k, (2048,) + MM_TILES if mode == "tn" else MM_TILES)
    nk = k // tk
    dims = {"nn": ((1,), (0,)), "nt": ((1,), (1,)), "tn": ((0,), (0,))}[mode]

    def body(a_ref, b_ref, *rest):
        o_ref = rest[-2] if nk > 1 else rest[-1]
        prod = lax.dot_general(a_ref[...].astype(BF16), b_ref[...].astype(BF16), (dims, ((), ())),
                               preferred_element_type=F32)
        if nk == 1:
            o_ref[...] = prod.astype(o_ref.dtype)
            return
        acc_ref = rest[-1]
        kk = pl.program_id(2)

        @pl.when(kk == 0)
        def _():
            acc_ref[...] = prod

        @pl.when(kk > 0)
        def _():
            acc_ref[...] += prod

        @pl.when(kk == nk - 1)
        def _():
            o_ref[...] = acc_ref[...].astype(o_ref.dtype)

    if mode == "tn":
        a_spec = pl.BlockSpec((tk, tm), lambda i, j, kk: (kk, i))
    else:
        a_spec = pl.BlockSpec((tm, tk), lambda i, j, kk: (i, kk))
    b_block, b_index = ((tn, tk), lambda i, j, kk: (j, kk)) if mode == "nt" else ((tk, tn), lambda i, j, kk: (kk, j))
    b_spec = _slab_spec(b_block, b_index, b_slab)
    in_specs, operands, aliases = [a_spec, b_spec], [a, b], {}
    if into is None:
        out_spec = pl.BlockSpec((tm, tn), lambda i, j, kk: (i, j))
        out_shape = jax.ShapeDtypeStruct((m, n), out_dtype)
    else:
        buf, slab, count = into
        out_spec = pl.BlockSpec((None, tm, tn), lambda i, j, kk: (slab, i, j))
        out_shape = jax.ShapeDtypeStruct((count, m, n), out_dtype)
        if buf is not None:
            in_specs.append(pl.BlockSpec(memory_space=pl.ANY))
            operands.append(buf)
            aliases = {2: 0}
    return pl.pallas_call(
        body, name=name, grid=(m // tm, n // tn, nk), in_specs=in_specs, out_specs=out_spec, out_shape=out_shape,
        scratch_shapes=[pltpu.VMEM((tm, tn), F32)] if nk > 1 else [], input_output_aliases=aliases,
        compiler_params=_params(dimension_semantics=("arbitrary", "arbitrary", "arbitrary")),
    )(*operands)


def _ffn_in(name, h, w):
    w, slab = w if isinstance(w, tuple) else (w, None)
    s, d = h.shape
    f = w.shape[-1] // 2
    tm, tn = _pick(s, MM_TILES[2:]), _pick(f, MM_TILES)

    def body(h_ref, wg_ref, wu_ref, g_ref, u_ref, a_ref):
        hb = h_ref[...]
        g, u = _nn(hb, wg_ref[...]), _nn(hb, wu_ref[...])
        g_ref[...] = g.astype(BF16)
        u_ref[...] = u.astype(BF16)
        a_ref[...] = (g * jax.nn.sigmoid(g) * u).astype(BF16)

    out = pl.BlockSpec((tm, tn), lambda i, j: (i, j))
    return pl.pallas_call(
        body, name=name, grid=(s // tm, f // tn),
        in_specs=[pl.BlockSpec((tm, d), lambda i, j: (i, 0)), _slab_spec((d, tn), lambda i, j: (0, j), slab),
                  _slab_spec((d, tn), lambda i, j: (0, f // tn + j), slab)],
        out_specs=[out, out, out], out_shape=[jax.ShapeDtypeStruct((s, f), BF16)] * 3,
        compiler_params=_params(dimension_semantics=("arbitrary", "arbitrary")),
    )(h, w, w)


def _small_mm(name, a, b, mode):
    dims = {"nn": ((1,), (0,)), "tn": ((0,), (0,))}[mode]
    m = a.shape[0] if mode == "nn" else a.shape[1]

    def body(a_ref, b_ref, o_ref):
        o_ref[...] = lax.dot_general(a_ref[...], b_ref[...], (dims, ((), ())), precision=lax.Precision.HIGHEST,
                                     preferred_element_type=F32)

    return pl.pallas_call(body, name=name, out_shape=jax.ShapeDtypeStruct((m, b.shape[1]), F32),
                          compiler_params=_params())(a, b)


def _norm(x):
    mu = jnp.mean(x, axis=-1, keepdims=True)
    xc = x - mu
    rstd = lax.rsqrt(jnp.mean(xc * xc, axis=-1, keepdims=True) + LN_EPS)
    return xc * rstd, rstd


def _norm_bwd(dn, n, rstd):
    return rstd * (dn - jnp.mean(dn, axis=-1, keepdims=True) - n * jnp.mean(dn * n, axis=-1, keepdims=True))


def _colsum(v):
    return jnp.sum(v, axis=0, keepdims=True)


def _gelu(x):
    return 0.5 * x * (1.0 + jnp.tanh(GELU_K * (x + GELU_C * x * x * x)))


def _gelu_grad(x):
    t = jnp.tanh(GELU_K * (x + GELU_C * x * x * x))
    return 0.5 * (1.0 + t) + 0.5 * x * (1.0 - t * t) * GELU_K * (1.0 + 3.0 * GELU_C * x * x)


def _log_sigmoid_parts(z):
    lb = jnp.minimum(z, 0.0) - jnp.log(1.0 + jnp.exp(-jnp.abs(z)))
    return lb, lb - z


def _kv_transposed(proj, t):
    s = proj.shape[0]
    nb, nhp = s // t, SB_WIDTH // LANES

    def body(k_ref, v_ref, kt_ref, vt_ref):
        k, v = k_ref[...].astype(F32), v_ref[...].astype(F32)
        for hp in range(nhp):
            kt_ref[hp, 0] = k[:, hp * LANES:(hp + 1) * LANES].T.astype(BF16)
            vt_ref[hp, 0] = v[:, hp * LANES:(hp + 1) * LANES].T.astype(BF16)

    col = lambda cb: pl.BlockSpec((t, SB_WIDTH), lambda i, cb=cb: (i, cb))
    t_out = pl.BlockSpec((nhp, 1, LANES, t), lambda i: (0, i, 0, 0))
    return pl.pallas_call(
        body, name="kv_transposed", grid=(nb,), in_specs=[col(1), col(2)], out_specs=[t_out, t_out],
        out_shape=[jax.ShapeDtypeStruct((nhp, nb, LANES, t), BF16)] * 2,
        compiler_params=_params(dimension_semantics=("arbitrary",)),
    )(proj, proj)


def _tile_masks(t):
    row = lax.broadcasted_iota(jnp.int32, (t, t), 0)
    col = lax.broadcasted_iota(jnp.int32, (t, t), 1)
    return row, col


DEAD_LOG_WEIGHT = -110.0


def _walk_down(i, tiles, state, alive):
    st = lax.cond(i == 0, lambda s_: tiles([i], s_, [True]), lambda s_: tiles([i, i - 1], s_, [True, False]), state)
    n = jnp.maximum(i - 1, 0)

    def pair(c):
        return c[0] + 1, tiles([i - 2 - 2 * c[0], i - 3 - 2 * c[0]], c[1], [False, False])

    p, st = lax.while_loop(lambda c: (c[0] < n // 2) & alive(c[1]), pair, (jnp.int32(0), st))
    return lax.cond((n % 2 == 1) & (p == n // 2) & alive(st), lambda s_: tiles([0], s_, [False]), lambda s_: s_, st)


def _walk_up(i, first, tiles, state):
    n = jnp.maximum(i - 1 - first, 0)
    st = lax.fori_loop(0, n // 2, lambda p, s_: tiles([first + 2 * p, first + 2 * p + 1], s_, [False, False]), state)
    st = lax.cond(n % 2 == 1, lambda s_: tiles([i - 2], s_, [False]), lambda s_: s_, st)
    return lax.cond(i == 0, lambda s_: tiles([i], s_, [True]), lambda s_: tiles([i - 1, i], s_, [False, True]), st)


def _nt(a, b):
    return lax.dot_general(a, b, (((1,), (1,)), ((), ())), preferred_element_type=F32)


def _nn(a, b):
    return jnp.dot(a, b, preferred_element_type=F32)


def _attn_fwd(proj, vt3, t):
    s = proj.shape[0]
    nb, nhp = s // t, SB_WIDTH // LANES

    def body(q_ref, k_ref, vt_ref, o_ref, car_ref):
        i = pl.program_id(1)
        q2 = q_ref[...] * (1.0 / math.sqrt(HEAD_DIM))
        lane_q = lax.broadcasted_iota(jnp.int32, q2.shape, 1)
        row, col = _tile_masks(t)
        later = (col > row).astype(BF16)
        valid = row < col
        orow = lax.broadcasted_iota(jnp.int32, (LANES, t), 0)
        car_ref[...] = jnp.full(car_ref.shape, 2.0 * DEAD_LOG_WEIGHT, F32)
        qh = [jnp.where((lane_q < HEAD_DIM) == (hh == 0), q2, jnp.zeros_like(q2)) for hh in range(2)]

        def tiles(js, state, diagonal):
            chains = [(n, hh) for n in range(len(js)) for hh in range(2)]
            kb = [k_ref[pl.ds(pl.multiple_of(j * t, t), t), :] for j in js]
            z = {ch: _nt(kb[ch[0]], qh[ch[1]]) for ch in chains}
            lb, aft, csum = {}, {}, {}
            for ch in chains:
                lb[ch], l1m = _log_sigmoid_parts(z[ch])
                if diagonal[ch[0]]:
                    l1m = jnp.where(valid, l1m, 0.0)
                aft[ch] = _nn(later, l1m.astype(BF16))
                csum[ch] = _colsum(l1m)
            state = list(state)
            for ch in chains:
                n, hh = ch
                c_after, acc = state[hh]
                w = jnp.exp(lb[ch] + aft[ch] + c_after)
                if diagonal[ch[0]]:
                    w = jnp.where(valid, w, 0.0)
                car_ref[hh, pl.ds(js[n], 1), :] = c_after
                state[hh] = (c_after + csum[ch], acc + _nn(vt_ref[0, js[n]], w.astype(BF16)))
            return tuple(state)

        def alive(state):
            return jnp.max(jnp.maximum(state[0][0], state[1][0])) >= DEAD_LOG_WEIGHT

        zero = (jnp.zeros((1, t), F32), jnp.zeros((LANES, t), F32))
        (_, acc0), (_, acc1) = _walk_down(i, tiles, (zero, zero), alive)
        o_ref[...] = jnp.where(orow < HEAD_DIM, acc0, acc1).T.astype(o_ref.dtype)

    return pl.pallas_call(
        body, name="attn_fwd", grid=(nhp, nb),
        in_specs=[pl.BlockSpec((t, LANES), lambda hp, i: (i, hp)),
                  pl.BlockSpec((s, LANES), lambda hp, i: (0, nhp + hp)),
                  pl.BlockSpec((1, nb, LANES, t), lambda hp, i: (hp, 0, 0, 0))],
        out_specs=[pl.BlockSpec((t, LANES), lambda hp, i: (i, hp)),
                   pl.BlockSpec((2, nb, t), lambda hp, i: (hp, 0, i))],
        out_shape=[jax.ShapeDtypeStruct((s, nhp * LANES), BF16), jax.ShapeDtypeStruct((2 * nhp, nb, s), F32)],
        compiler_params=_params(dimension_semantics=("arbitrary", "arbitrary")),
    )(proj, proj, vt3)


def _attn_bwd(proj, do, kt3, car, t):
    s = proj.shape[0]
    nb, nhp = s // t, SB_WIDTH // LANES

    def body(q_ref, do_ref, k_ref, v_ref, kt_ref, car_ref, dq_ref, dk_ref, dv_ref):
        i = pl.program_id(1)

        @pl.when(i == 0)
        def _():
            dk_ref[...] = jnp.zeros_like(dk_ref)
            dv_ref[...] = jnp.zeros_like(dv_ref)

        q2, do2 = q_ref[...] * (1.0 / math.sqrt(HEAD_DIM)), do_ref[...]
        lane_q = lax.broadcasted_iota(jnp.int32, q2.shape, 1)
        row, col = _tile_masks(t)
        later = (col > row).astype(BF16)
        earlier = (col < row).astype(BF16)
        valid = row < col
        orow = lax.broadcasted_iota(jnp.int32, (LANES, t), 0)
        head = [(lane_q < HEAD_DIM) == (hh == 0) for hh in range(2)]
        qh = [jnp.where(hm, q2, jnp.zeros_like(q2)) for hm in head]
        doh = [jnp.where(hm, do2, jnp.zeros_like(do2)) for hm in head]

        def tiles(js, state, diagonal):
            chains = [(n, hh) for n in range(len(js)) for hh in range(2)]
            rows = [pl.ds(pl.multiple_of(j * t, t), t) for j in js]
            kb = [k_ref[r, :] for r in rows]
            vb = [v_ref[r, :] for r in rows]
            z = {ch: _nt(kb[ch[0]], qh[ch[1]]) for ch in chains}
            dw = {ch: _nt(vb[ch[0]], doh[ch[1]]) for ch in chains}
            lb, beta, aft = {}, {}, {}
            for ch in chains:
                lb[ch], l1m = _log_sigmoid_parts(z[ch])
                beta[ch] = jnp.exp(lb[ch])
                if diagonal[ch[0]]:
                    l1m = jnp.where(valid, l1m, 0.0)
                aft[ch] = _nn(later, l1m.astype(BF16))
            w, g, gsum, g_in = {}, {}, {}, {}
            for ch in chains:
                n, hh = ch
                w[ch] = jnp.exp(lb[ch] + aft[ch] + car_ref[hh, pl.ds(js[n], 1), :])
                if diagonal[ch[0]]:
                    w[ch] = jnp.where(valid, w[ch], 0.0)
                g[ch] = dw[ch] * w[ch]
                g_in[ch] = _nn(earlier, g[ch].astype(BF16))
                gsum[ch] = _colsum(g[ch])
            state = list(state)
            dk_t, dv_t = [None] * len(js), [None] * len(js)
            for ch in chains:
                n, hh = ch
                c_g, dqt = state[hh]
                dz = g[ch] - beta[ch] * (g[ch] + g_in[ch] + c_g)
                if diagonal[ch[0]]:
                    dz = jnp.where(valid, dz, 0.0)
                dzb, wb = dz.astype(BF16), w[ch].astype(BF16)
                dk_h, dv_h = _nn(dzb, qh[hh]), _nn(wb, doh[hh])
                dk_t[n] = dk_h if dk_t[n] is None else dk_t[n] + dk_h
                dv_t[n] = dv_h if dv_t[n] is None else dv_t[n] + dv_h
                state[hh] = (c_g + gsum[ch], dqt + _nn(kt_ref[0, js[n]], dzb))
            for n in range(len(js)):
                dk_ref[rows[n], :] += dk_t[n]
                dv_ref[rows[n], :] += dv_t[n]
            return tuple(state)

        reach = jnp.max(jnp.max(car_ref[...], axis=2, keepdims=True), axis=0)
        dead = (reach < DEAD_LOG_WEIGHT) & (lax.broadcasted_iota(jnp.int32, reach.shape, 0) < i)
        first = jnp.sum(jnp.where(dead, 1.0, 0.0)).astype(jnp.int32)
        zero = (jnp.zeros((1, t), F32), jnp.zeros((LANES, t), F32))
        (_, dq0), (_, dq1) = _walk_up(i, first, tiles, (zero, zero))
        dq_ref[...] = jnp.where(orow < HEAD_DIM, dq0, dq1).T

    tile_spec = pl.BlockSpec((t, LANES), lambda hp, i: (i, hp))
    whole = pl.BlockSpec((s, LANES), lambda hp, i: (0, hp))
    return pl.pallas_call(
        body, name="attn_bwd", grid=(nhp, nb),
        in_specs=[tile_spec, tile_spec, pl.BlockSpec((s, LANES), lambda hp, i: (0, nhp + hp)),
                  pl.BlockSpec((s, LANES), lambda hp, i: (0, 2 * nhp + hp)),
                  pl.BlockSpec((1, nb, LANES, t), lambda hp, i: (hp, 0, 0, 0)),
                  pl.BlockSpec((2, nb, t), lambda hp, i: (hp, 0, i))],
        out_specs=[tile_spec, whole, whole],
        out_shape=[jax.ShapeDtypeStruct((s, nhp * LANES), F32)] * 3,
        compiler_params=_params(dimension_semantics=("arbitrary", "arbitrary")),
    )(proj, do, proj, proj, kt3, car)


SCAN_LANES = 1024
SCAN_ROWS = 8
S5_CHUNKS = 4


def _scan_chunks(v):
    n = v.shape[1] // (2 * LANES)
    return [(v[:, c * 2 * LANES:c * 2 * LANES + LANES], v[:, c * 2 * LANES + LANES:(c + 1) * 2 * LANES]) for c in range(n)]


def _scan_tables(lr, li, reverse):
    if reverse:
        li = -li
    row = lax.broadcasted_iota(jnp.int32, (SCAN_ROWS, LANES), 0)
    powers = [(lr, li)]
    for _ in range(SCAN_ROWS - 1):
        pr, pi = powers[-1]
        powers.append((pr * lr - pi * li, pr * li + pi * lr))
    levels = []
    for d in (1, 2, 4):
        keep = (row < SCAN_ROWS - d) if reverse else (row >= d)
        levels.append((SCAN_ROWS - d if reverse else d,
                       (jnp.where(keep, powers[d - 1][0], 0.0), jnp.where(keep, powers[d - 1][1], 0.0))))
    pr = pi = jnp.zeros((SCAN_ROWS, LANES), F32)
    for r in range(SCAN_ROWS):
        steps = SCAN_ROWS - r if reverse else r + 1
        pr = jnp.where(row == r, powers[steps - 1][0], pr)
        pi = jnp.where(row == r, powers[steps - 1][1], pi)
    return levels, (pr, pi)


def _s5_fwd(proj, u_off, bmat, cmat, lam):
    s, w = proj.shape[0], SSM_BLOCKS * bmat.shape[1]
    tt = _pick(s, (512, 256, 128, 8))
    nt = s // tt
    cin = bmat.shape[0] // SSM_BLOCKS
    chunk = tt // S5_CHUNKS

    def body(u_ref, b_ref, c_ref, lam_ref, h_ref, y_ref, x_ref, st_ref):
        @pl.when(pl.program_id(1) == 0)
        def _():
            st_ref[...] = jnp.zeros_like(st_ref)

        tables = [_scan_tables(lr, li, reverse=False) for lr, li in _scan_chunks(lam_ref[...])]

        def project(k):
            x_ref[k * chunk:(k + 1) * chunk, :] = _nn(u_ref[k * chunk:(k + 1) * chunk, :].astype(BF16), b_ref[...])

        def tile(r0, last):
            last, parts = list(last), []
            for c, (xr, xi) in enumerate(_scan_chunks(x_ref[r0:r0 + SCAN_ROWS, :])):
                levels, (pr, pi) = tables[c]
                for d, (ar, ai) in levels:
                    sr, si = pltpu.roll(xr, d, 0), pltpu.roll(xi, d, 0)
                    xr, xi = xr + ar * sr - ai * si, xi + ar * si + ai * sr
                br, bi = last[2 * c], last[2 * c + 1]
                hr = xr + pr * br - pi * bi
                hi = xi + pr * bi + pi * br
                last[2 * c], last[2 * c + 1] = hr[SCAN_ROWS - 1:], hi[SCAN_ROWS - 1:]
                parts += [hr, hi]
            h_ref[r0:r0 + SCAN_ROWS, :] = jnp.concatenate(parts, axis=1)
            return tuple(last)

        st = st_ref[0:1, :]
        last = tuple(st[:, c * LANES:(c + 1) * LANES] for c in range(SCAN_LANES // LANES))
        project(0)
        for k in range(S5_CHUNKS):
            if k + 1 < S5_CHUNKS:
                project(k + 1)
            for r0 in range(k * chunk, (k + 1) * chunk, SCAN_ROWS):
                last = tile(r0, last)
            y_ref[k * chunk:(k + 1) * chunk, :] = _nn(h_ref[k * chunk:(k + 1) * chunk, :].astype(BF16), c_ref[...])
        st_ref[0:1, :] = jnp.concatenate(last, axis=1)

    return pl.pallas_call(
        body, name="s5_fwd", grid=(SSM_BLOCKS, nt),
        in_specs=[pl.BlockSpec((tt, cin), lambda kb, i: (i, u_off // cin + kb)),
                  pl.BlockSpec((cin, SCAN_LANES), lambda kb, i: (kb, 0)),
                  pl.BlockSpec((SCAN_LANES, cin), lambda kb, i: (kb, 0)),
                  pl.BlockSpec((1, SCAN_LANES), lambda kb, i: (0, kb))],
        out_specs=[pl.BlockSpec((tt, SCAN_LANES), lambda kb, i: (i, kb)), pl.BlockSpec((tt, cin), lambda kb, i: (i, kb))],
        out_shape=[jax.ShapeDtypeStruct((s, w), F32), jax.ShapeDtypeStruct((s, bmat.shape[0]), F32)],
        scratch_shapes=[pltpu.VMEM((tt, SCAN_LANES), F32), pltpu.VMEM((SCAN_ROWS, SCAN_LANES), F32)],
        compiler_params=_params(dimension_semantics=("arbitrary", "arbitrary")),
    )(proj, bmat, cmat, lam)


def _s5_bwd(dy, h, proj, u_off, bmat, cmat, lam):
    s, w = h.shape
    tt = _pick(s, (512, 256, 128, 8))
    nt = s // tt
    cin = bmat.shape[0] // SSM_BLOCKS
    chunk = tt // S5_CHUNKS

    def body(dy_ref, h_ref, u_ref, b_ref, c_ref, lam_ref, du_ref, dlam_ref, db_ref, dc_ref, e_ref, a_ref, st_ref):
        @pl.when(pl.program_id(1) == 0)
        def _():
            st_ref[...] = jnp.zeros_like(st_ref)
            dlam_ref[...] = jnp.zeros_like(dlam_ref)
            db_ref[...] = jnp.zeros_like(db_ref)
            dc_ref[...] = jnp.zeros_like(dc_ref)

        tables = [_scan_tables(lr, li, reverse=True) for lr, li in _scan_chunks(lam_ref[...])]
        nch = len(tables)
        row = lax.broadcasted_iota(jnp.int32, (SCAN_ROWS, LANES), 0)
        rows_first = (((0,), (0,)), ((), ()))

        def project(k):
            e_ref[k * chunk:(k + 1) * chunk, :] = _nt(dy_ref[k * chunk:(k + 1) * chunk, :], c_ref[...])

        def finish(k):
            rows = slice(k * chunk, (k + 1) * chunk)
            adj = a_ref[rows, :].astype(BF16)
            du_ref[rows, :] = _nt(adj, b_ref[...])
            db_ref[...] += lax.dot_general(u_ref[rows, :].astype(BF16), adj, rows_first, preferred_element_type=F32)
            dc_ref[...] += lax.dot_general(h_ref[rows, :].astype(BF16), dy_ref[rows, :], rows_first, preferred_element_type=F32)

        def tile(r0, carry):
            e_c = _scan_chunks(e_ref[r0:r0 + SCAN_ROWS, :])
            h_c = _scan_chunks(h_ref[r0:r0 + SCAN_ROWS, :])
            carry, parts = list(carry), []
            for c in range(nch):
                (yr, yi), (hr, hi) = e_c[c], h_c[c]
                levels, (pr, pi) = tables[c]
                for shift, (lr, li) in levels:
                    sr, si = pltpu.roll(yr, shift, 0), pltpu.roll(yi, shift, 0)
                    yr, yi = yr + lr * sr - li * si, yi + lr * si + li * sr
                nr, ni, dr, di = carry[4 * c:4 * c + 4]
                ar = yr + pr * nr - pi * ni
                ai = yi + pr * ni + pi * nr
                nxr = jnp.where(row == SCAN_ROWS - 1, nr, pltpu.roll(ar, SCAN_ROWS - 1, 0))
                nxi = jnp.where(row == SCAN_ROWS - 1, ni, pltpu.roll(ai, SCAN_ROWS - 1, 0))
                carry[4 * c:4 * c + 4] = [ar[0:1], ai[0:1], dr + nxr * hr + nxi * hi, di + nxi * hr - nxr * hi]
                parts += [ar, ai]
            a_ref[r0:r0 + SCAN_ROWS, :] = jnp.concatenate(parts, axis=1)
            return tuple(carry)

        st, dl = st_ref[0:1, :], dlam_ref[...]
        init = []
        for c in range(nch):
            lo = c * 2 * LANES
            init += [st[:, lo:lo + LANES], st[:, lo + LANES:lo + 2 * LANES],
                     dl[:, lo:lo + LANES], dl[:, lo + LANES:lo + 2 * LANES]]
        fin = tuple(init)
        project(S5_CHUNKS - 1)
        for k in reversed(range(S5_CHUNKS)):
            if k > 0:
                project(k - 1)
            for r0 in reversed(range(k * chunk, (k + 1) * chunk, SCAN_ROWS)):
                fin = tile(r0, fin)
            finish(k)
        st_ref[0:1, :] = jnp.concatenate([fin[4 * c + q] for c in range(nch) for q in (0, 1)], axis=1)
        dlam_ref[...] = jnp.concatenate([fin[4 * c + q] for c in range(nch) for q in (2, 3)], axis=1)

        @pl.when(pl.program_id(1) == nt - 1)
        def _():
            dlam_ref[0:1, :] = jnp.sum(dlam_ref[...], axis=0, keepdims=True)

    def rev(width, col):
        return pl.BlockSpec((tt, width), lambda kb, i: (nt - 1 - i, col(kb)))

    return pl.pallas_call(
        body, name="s5_bwd", grid=(SSM_BLOCKS, nt),
        in_specs=[rev(cin, lambda kb: kb), rev(SCAN_LANES, lambda kb: kb), rev(cin, lambda kb: u_off // cin + kb),
                  pl.BlockSpec((cin, SCAN_LANES), lambda kb, i: (kb, 0)),
                  pl.BlockSpec((SCAN_LANES, cin), lambda kb, i: (kb, 0)),
                  pl.BlockSpec((1, SCAN_LANES), lambda kb, i: (0, kb))],
        out_specs=[rev(cin, lambda kb: kb), pl.BlockSpec((SCAN_ROWS, SCAN_LANES), lambda kb, i: (0, kb)),
                   pl.BlockSpec((cin, SCAN_LANES), lambda kb, i: (kb, 0)), pl.BlockSpec((SCAN_LANES, cin), lambda kb, i: (kb, 0))],
        out_shape=[jax.ShapeDtypeStruct((s, bmat.shape[0]), F32), jax.ShapeDtypeStruct((SCAN_ROWS, w), F32),
                   jax.ShapeDtypeStruct((bmat.shape[0], SCAN_LANES), F32), jax.ShapeDtypeStruct((w, cin), F32)],
        scratch_shapes=[pltpu.VMEM((tt, SCAN_LANES), F32), pltpu.VMEM((tt, SCAN_LANES), F32),
                        pltpu.VMEM((SCAN_ROWS, SCAN_LANES), F32)],
        compiler_params=_params(dimension_semantics=("arbitrary", "arbitrary")),
    )(dy, h, proj, bmat, cmat, lam)


def _ssm_params_fwd(a_re, a_im, log_dt, b_re, b_im):
    def body(ar_ref, ai_ref, ldt_ref, br_ref, bi_ref, lr_ref, li_ref, bbr_ref, bbi_ref):
        ar, ai, dt = ar_ref[...], ai_ref[...], jnp.exp(ldt_ref[...])
        mag = jnp.exp(ar * dt)
        lr, li = mag * jnp.cos(ai * dt), mag * jnp.sin(ai * dt)
        den = ar * ar + ai * ai
        cr = ((lr - 1.0) * ar + li * ai) / den
        ci = (li * ar - (lr - 1.0) * ai) / den
        br, bi = br_ref[...], bi_ref[...]
        lr_ref[...], li_ref[...] = lr, li
        bbr_ref[...] = cr * br - ci * bi
        bbi_ref[...] = cr * bi + ci * br

    n = a_re.shape[0]
    v1, v16 = jax.ShapeDtypeStruct((n, 1), F32), jax.ShapeDtypeStruct((n, SSM_GROUP), F32)
    return pl.pallas_call(body, name="ssm_params_fwd", out_shape=[v1, v1, v16, v16],
                          compiler_params=_params())(a_re, a_im, log_dt, b_re, b_im)


def _ssm_params_bwd(a_re, a_im, log_dt, b_re, b_im, g_lr, g_li, g_bbr, g_bbi):
    n = a_re.shape[0]

    def body(ar_ref, ai_ref, ldt_ref, br_ref, bi_ref, glr_ref, gli_ref, gbr_ref, gbi_ref,
             dar_ref, dai_ref, dldt_ref, dbr_ref, dbi_ref):
        ar, ai, dt = ar_ref[...], ai_ref[...], jnp.exp(ldt_ref[...])
        mag = jnp.exp(ar * dt)
        lr, li = mag * jnp.cos(ai * dt), mag * jnp.sin(ai * dt)
        den = ar * ar + ai * ai
        cr = ((lr - 1.0) * ar + li * ai) / den
        ci = (li * ar - (lr - 1.0) * ai) / den
        br, bi, gbr, gbi = br_ref[...], bi_ref[...], gbr_ref[...], gbi_ref[...]
        dbr_ref[...] = gbr * cr + gbi * ci
        dbi_ref[...] = gbi * cr - gbr * ci
        gcr = jnp.sum(gbr * br + gbi * bi, axis=1, keepdims=True)
        gci = jnp.sum(gbi * br - gbr * bi, axis=1, keepdims=True)
        ir, ii = ar / den, -ai / den
        glr = glr_ref[...] + gcr * ir + gci * ii
        gli = gli_ref[...] + gci * ir - gcr * ii
        qr, qi = cr * ir - ci * ii, cr * ii + ci * ir
        gar = -(gcr * qr + gci * qi)
        gai = -(gci * qr - gcr * qi)
        gxr = glr * lr + gli * li
        gxi = gli * lr - glr * li
        dar_ref[...] = gar + gxr * dt
        dai_ref[...] = gai + gxi * dt
        gdt = (gxr * ar + gxi * ai) * dt
        rowg = lax.broadcasted_iota(jnp.int32, (n, SSM_GROUPS), 0) // SSM_STATE
        colg = lax.broadcasted_iota(jnp.int32, (n, SSM_GROUPS), 1)
        dldt_ref[...] = jnp.sum(jnp.where(rowg == colg, gdt, 0.0), axis=0, keepdims=True)

    v1, v16 = jax.ShapeDtypeStruct((n, 1), F32), jax.ShapeDtypeStruct((n, SSM_GROUP), F32)
    return pl.pallas_call(body, name="ssm_params_bwd",
                          out_shape=[v1, v1, jax.ShapeDtypeStruct((1, SSM_GROUPS), F32), v16, v16],
                          compiler_params=_params())(a_re, a_im, log_dt, b_re, b_im, g_lr, g_li, g_bbr, g_bbi)


def _interleave(re, im, axis):
    shp = list(re.shape)
    new = shp[:axis] + [shp[axis] // LANES, LANES] + shp[axis + 1:]
    st = jnp.stack([re.reshape(new), im.reshape(new)], axis=axis + 1)
    return st.reshape(shp[:axis] + [2 * shp[axis]] + shp[axis + 1:])


def _deinterleave(v, axis):
    shp = list(v.shape)
    r = v.reshape(shp[:axis] + [shp[axis] // (2 * LANES), 2, LANES] + shp[axis + 1:])
    out = shp[:axis] + [shp[axis] // 2] + shp[axis + 1:]
    return (lax.index_in_dim(r, 0, axis + 1, keepdims=False).reshape(out),
            lax.index_in_dim(r, 1, axis + 1, keepdims=False).reshape(out))


def _b_matrix(bbr, bbi):
    per = SSM_GROUPS // SSM_BLOCKS
    eye = jnp.eye(per, dtype=F32)

    def blockdiag(v):
        x = v.reshape(SSM_BLOCKS, per, SSM_STATE, SSM_GROUP).transpose(0, 1, 3, 2)
        return (eye[None, :, None, :, None] * x[:, :, :, None, :]).reshape(SSM_GROUPS * SSM_GROUP, per * SSM_STATE)

    return _interleave(blockdiag(bbr), blockdiag(bbi), 1)


def _diag_blocks(v, rows, cols):
    per = SSM_GROUPS // SSM_BLOCKS
    return jnp.stack([v[g * rows:(g + 1) * rows, (g % per) * cols:(g % per + 1) * cols] for g in range(SSM_GROUPS)])


def _b_matrix_grad(d):
    def diag(v):
        return _diag_blocks(v, SSM_GROUP, SSM_STATE).transpose(0, 2, 1).reshape(N_STATE, SSM_GROUP)

    dr, di = _deinterleave(d, 1)
    return diag(dr), diag(di)


def _c_matrix(c_re, c_im):
    per = SSM_GROUPS // SSM_BLOCKS
    eye = jnp.eye(per, dtype=F32)

    def blockdiag(v):
        x = v.reshape(SSM_BLOCKS, per, SSM_GROUP, SSM_STATE).transpose(0, 1, 3, 2)
        return (x[:, :, :, None, :] * eye[None, :, None, :, None]).reshape(N_STATE, per * SSM_GROUP)

    return _interleave(blockdiag(c_re), blockdiag(-c_im), 0)


def _c_matrix_grad(d):
    def diag(v):
        return _diag_blocks(v, SSM_STATE, SSM_GROUP).transpose(0, 2, 1)

    dr, di = _deinterleave(d, 0)
    return diag(dr), -diag(di)


def _row(v):
    return v.reshape(1, -1)


def _ssm_inputs(p):
    rows = lambda v: v.reshape(N_STATE, -1)
    ldt = jnp.repeat(p["ssm_log_dt"], SSM_STATE).reshape(N_STATE, 1)
    return rows(p["ssm_a_re"]), rows(p["ssm_a_im"]), ldt, rows(p["ssm_b_re"]), rows(p["ssm_b_im"])


def _lnmod(x, sc, sh):
    return _norm(x)[0] * (1.0 + sc) + sh


def _resid_ln(x, y, g, lg, lb):
    return _norm(ALPHA * x + (1.0 + g) * y)[0] * lg + lb


def _resid_ln_lnmod(x, y, g, lg, lb, sc, sh):
    xo = _resid_ln(x, y, g, lg, lb)
    return xo, _lnmod(xo, sc, sh)


def _layer_fwd(x, h1, mod, p, tag, next_mod):
    d = x.shape[1]
    sh_m, sc_m, g_m, sh_f, sc_f, g_f = [_row(mod[i]) for i in range(6)]
    nm = lambda s: f"{s}_{tag}"
    proj = _mm(nm("proj"), h1, p["w_in"], "nn", out_dtype=BF16)
    t = min(ATT_TILE, x.shape[0])
    kt3, vt3 = _kv_transposed(proj, t)
    att, car = _attn_fwd(proj, vt3, t)
    y_sb = _mm(nm("sb_up"), att, p["w_sb_up"], "nn", out_dtype=BF16)

    lam_r, lam_i, bbr, bbi = _ssm_params_fwd(*_ssm_inputs(p))
    lam = _interleave(lam_r.reshape(1, N_STATE), lam_i.reshape(1, N_STATE), 1)
    bmat = _b_matrix(bbr, bbi).astype(BF16)
    cmat = _c_matrix(p["ssm_c_re"], p["ssm_c_im"]).astype(BF16)
    hst, yc = _s5_fwd(proj, U_OFFSET, bmat, cmat, lam)

    def ssm_act(yc, u, dsk):
        y0 = yc + dsk * u
        return y0, _gelu(y0)

    y0, y1 = _rowwise(nm("ssm_act"), ssm_act, [(yc, 0, 512), (proj, 3, 512)], [_row(p["ssm_d"])], [(512, F32), (512, F32)])
    gl = _mm(nm("glu"), y1, p["w_glu"], "nn")
    y2 = _rowwise(nm("glu_act"), lambda y1, gl, b: y1 * jax.nn.sigmoid(gl + b), [(y1, 0, 512), (gl, 0, 512)],
                  [_row(p["b_glu"])], [(512, BF16)])
    y_ssm = _mm(nm("ssm_up"), y2, p["w_ssm_up"], "nn", out_dtype=BF16)

    def merge(gsb, gss, ysb, yss):
        return jax.nn.sigmoid(gsb) * ysb + jax.nn.sigmoid(gss) * yss

    merged = _rowwise(nm("merge"), merge, [(proj, 2, d), (proj, 3, d), (y_sb, 0, d), (y_ssm, 0, d)], [], [(d, BF16)])
    y = _mm(nm("out"), merged, p["w_out"], "nn")

    x1, h2 = _rowwise(nm("ln1"), _resid_ln_lnmod, [(x, 0, d), (y, 0, d)],
                      [g_m, _row(p["ln1_g"]), _row(p["ln1_b"]), sc_f, sh_f], [(d, F32), (d, BF16)])
    f_gate, f_up, act = _ffn_in(nm("ffn_in"), h2, p["w_ffn_in"])
    yf = _mm(nm("ffn_out"), act, p["w_ffn_out"], "nn")
    x2 = h1_next = None
    if next_mod is not None:
        x2, h1_next = _rowwise(nm("ln2"), _resid_ln_lnmod, [(x1, 0, d), (yf, 0, d)],
                               [g_f, _row(p["ln2_g"]), _row(p["ln2_b"]), next_mod[1], next_mod[0]], [(d, F32), (d, BF16)])
    saved = dict(x=x, h1=h1, proj=proj, kt3=kt3, car=car, att=att, y_sb=y_sb, lam=lam, bmat=bmat,
                 cmat=cmat, hst=hst, y0=y0, y1=y1, gl=gl, y2=y2, y_ssm=y_ssm, merged=merged, y=y, x1=x1, h2=h2, f_gate=f_gate, f_up=f_up,
                 act=act, yf=yf, t=t)
    return x2, h1_next, saved


def _resid_ln_bwd(x, y, dxo, g, lg):
    n, rstd = _norm(ALPHA * x + (1.0 + g) * y)
    dr = _norm_bwd(dxo * lg, n, rstd)
    return ALPHA * dr, (1.0 + g) * dr, _colsum(dxo * n), _colsum(dxo), _colsum(dr * y)


def _lnmod_bwd(x, dh, dxa, sc):
    n, rstd = _norm(x)
    return dxa + _norm_bwd(dh * (1.0 + sc), n, rstd), _colsum(dh * n), _colsum(dh)


def _lnmod_resid_ln_bwd(xo, dh, dxa, x, y, sc, g, lg):
    dxo, dsc, dsh = _lnmod_bwd(xo, dh, dxa, sc)
    dx, dy, dlg, dlb, dg = _resid_ln_bwd(x, y, dxo, g, lg)
    return dx, dy, dsc, dsh, dlg, dlb, dg


def _layer_bwd(dx1a, dyf, mod, p, sv, layer, depth, stacked):
    d = dx1a.shape[1]
    sh_m, sc_m, g_m, sh_f, sc_f, g_f = [_row(mod[i]) for i in range(6)]
    nm = lambda s: f"{s}_{layer}"
    grads = {}

    def weight_grad(n, a, b, **kw):
        grads[n] = _mm(nm("d" + n), a, b, "tn", out_dtype=BF16, into=(stacked.get(n), layer, depth), **kw)

    dact = _mm(nm("d_act"), dyf, p["w_ffn_out"], "nt", out_dtype=BF16)
    weight_grad("w_ffn_out", sv["act"], dyf)
    fh = sv["f_gate"].shape[1]

    def swiglu_bwd(g, u, da):
        sg = jax.nn.sigmoid(g)
        return jnp.concatenate([da * u * sg * (1.0 + g * (1.0 - sg)), da * g * sg], axis=1)

    df = _rowwise(nm("swiglu_bwd"), swiglu_bwd, [(sv["f_gate"], 0, fh), (sv["f_up"], 0, fh), (dact, 0, fh)], [], [(2 * fh, BF16)])
    dh2 = _mm(nm("d_h2"), df, p["w_ffn_in"], "nt")
    weight_grad("w_ffn_in", sv["h2"], df)
    dxa, dy, dsc_f, dsh_f, grads["ln1_g"], grads["ln1_b"], dg_m = _rowwise(
        nm("ln1_bwd"), _lnmod_resid_ln_bwd, [(sv["x1"], 0, d), (dh2, 0, d), (dx1a, 0, d), (sv["x"], 0, d), (sv["y"], 0, d)],
        [sc_f, g_m, _row(p["ln1_g"])], [(d, F32), (d, BF16)], [d] * 5, tm=128)
    dmerged = _mm(nm("d_merged"), dy, p["w_out"], "nt", out_dtype=BF16)
    weight_grad("w_out", sv["merged"], dy)

    def merge_bwd(gsb, gss, ysb, yss, dm):
        s1, s2 = jax.nn.sigmoid(gsb), jax.nn.sigmoid(gss)
        return s1 * dm, s2 * dm, dm * ysb * s1 * (1.0 - s1), dm * yss * s2 * (1.0 - s2)

    dy_sb, dy_ssm, dg_sb, dg_ssm = _rowwise(
        nm("merge_bwd"), merge_bwd, [(sv["proj"], 2, d), (sv["proj"], 3, d), (sv["y_sb"], 0, d), (sv["y_ssm"], 0, d),
                                     (dmerged, 0, d)], [], [(d, BF16)] * 4)
    dy2 = _mm(nm("d_y2"), dy_ssm, p["w_ssm_up"], "nt")
    weight_grad("w_ssm_up", sv["y2"], dy_ssm)

    def glu_act_bwd(y1, gl, dy2, b):
        sg = jax.nn.sigmoid(gl + b)
        dgl = dy2 * y1 * sg * (1.0 - sg)
        return dy2 * sg, dgl, _colsum(dgl)

    dy1a, dgl, grads["b_glu"] = _rowwise(nm("glu_act_bwd"), glu_act_bwd, [(sv["y1"], 0, 512), (sv["gl"], 0, 512), (dy2, 0, 512)],
                                         [_row(p["b_glu"])], [(512, F32), (512, BF16)], [512])
    dy1b = _mm(nm("d_y1"), dgl, p["w_glu"], "nt")
    weight_grad("w_glu", sv["y1"], dgl)

    def ssm_act_bwd(y0, u, dy1a, dy1b, dsk):
        dy0 = (dy1a + dy1b) * _gelu_grad(y0)
        return dy0, dsk * dy0, _colsum(dy0 * u)

    dy0, du_a, grads["ssm_d"] = _rowwise(nm("ssm_act_bwd"), ssm_act_bwd,
                                         [(sv["y0"], 0, 512), (sv["proj"], 3, 512), (dy1a, 0, 512), (dy1b, 0, 512)],
                                         [_row(p["ssm_d"])], [(512, BF16), (512, F32)], [512])
    du_b, dlam, d_bmat, d_cmat = _s5_bwd(dy0, sv["hst"], sv["proj"], U_OFFSET, sv["bmat"], sv["cmat"], sv["lam"])
    grads["ssm_c_re"], grads["ssm_c_im"] = _c_matrix_grad(d_cmat)
    g_bbr, g_bbi = _b_matrix_grad(d_bmat)
    g_lr, g_li = _deinterleave(dlam[0:1], 1)
    da_re, da_im, dldt, db_re, db_im = _ssm_params_bwd(*_ssm_inputs(p), g_lr.reshape(N_STATE, 1), g_li.reshape(N_STATE, 1),
                                                       g_bbr, g_bbi)
    grads["ssm_a_re"] = da_re.reshape(SSM_GROUPS, SSM_STATE)
    grads["ssm_a_im"] = da_im.reshape(SSM_GROUPS, SSM_STATE)
    grads["ssm_log_dt"] = dldt.reshape(SSM_GROUPS)
    grads["ssm_b_re"] = db_re.reshape(SSM_GROUPS, SSM_STATE, SSM_GROUP)
    grads["ssm_b_im"] = db_im.reshape(SSM_GROUPS, SSM_STATE, SSM_GROUP)
    datt = _mm(nm("d_att"), dy_sb, p["w_sb_up"], "nt", out_dtype=BF16)
    weight_grad("w_sb_up", sv["att"], dy_sb)
    dqs, dk, dv = _attn_bwd(sv["proj"], datt, sv["kt3"], sv["car"], sv["t"])

    def dproj_cols(dqs, dk, dv, dua, dub, dgsb, dgss):
        return jnp.concatenate([dqs * (1.0 / math.sqrt(HEAD_DIM)), dk, dv, dua + dub, dgsb.astype(F32), dgss.astype(F32)],
                               axis=1)

    dproj = _rowwise(nm("dproj"), dproj_cols, [(dqs, 0, 512), (dk, 0, 512), (dv, 0, 512), (du_a, 0, 512), (du_b, 0, 512),
                                               (dg_sb, 0, d), (dg_ssm, 0, d)], [], [(2048 + 2 * d, BF16)])
    dh1 = _mm(nm("d_h1"), dproj, p["w_in"], "nt")
    weight_grad("w_in", sv["h1"], dproj)
    for k in ("ln1_g", "ln1_b", "ssm_d", "b_glu"):
        grads[k] = grads[k].reshape(-1)
    return dh1, dxa, grads, (dg_m, dsh_f, dsc_f)


def _local_step(x, target, mod, layer_w):
    depth, d = len(layer_w), x.shape[1]
    rows = lambda l: [_row(mod[l][i]) for i in range(6)]
    h1 = _rowwise("lnmod1_0", _lnmod, [(x, 0, d)], [rows(0)[1], rows(0)[0]], [(d, BF16)])
    xs, saved = x, []
    for l in range(depth):
        xs, h1, sv = _layer_fwd(xs, h1, mod[l], layer_w[l], str(l), rows(l + 1)[:2] if l + 1 < depth else None)
        saved.append(sv)

    def head_bwd(x1, yf, tgt, g, lg, lb):
        err = _resid_ln(x1, yf, g, lg, lb) - tgt
        return _resid_ln_bwd(x1, yf, err * (1.0 / d), g, lg) + (_colsum(err * err) * (0.5 / d),)

    def boundary_bwd(dh, dxa, x1, yf, sc, g, lg, lb):
        dxo, dsc, dsh = _lnmod_bwd(_resid_ln(x1, yf, g, lg, lb), dh, dxa, sc)
        return _resid_ln_bwd(x1, yf, dxo, g, lg) + (dsc, dsh)

    lgrads, sums, stacked = [None] * depth, [dict() for _ in range(depth)], {}
    last, p = saved[-1], layer_w[-1]
    dx1a, dyf, dlg, dlb, dg_f, loss_cols = _rowwise(
        "head_bwd", head_bwd, [(last["x1"], 0, d), (last["yf"], 0, d), (target, 0, d)],
        [rows(depth - 1)[5], _row(p["ln2_g"]), _row(p["ln2_b"])], [(d, F32), (d, BF16)], [d] * 4)
    for l in reversed(range(depth)):
        sums[l]["g_f"] = dg_f
        dh1, dxa, lgrads[l], (sums[l]["g_m"], sums[l]["sh_f"], sums[l]["sc_f"]) = _layer_bwd(
            dx1a, dyf, mod[l], layer_w[l], saved[l], l, depth, stacked)
        lgrads[l]["ln2_g"], lgrads[l]["ln2_b"] = dlg.reshape(-1), dlb.reshape(-1)
        stacked = {n: lgrads[l][n] for n in COL_SPLIT + ROW_SPLIT}
        if l > 0:
            prev, p = saved[l - 1], layer_w[l - 1]
            dx1a, dyf, dlg, dlb, dg_f, sums[l]["sc_m"], sums[l]["sh_m"] = _rowwise(
                f"boundary_bwd_{l}", boundary_bwd, [(dh1, 0, d), (dxa, 0, d), (prev["x1"], 0, d), (prev["yf"], 0, d)],
                [rows(l)[1], rows(l - 1)[5], _row(p["ln2_g"]), _row(p["ln2_b"])], [(d, F32), (d, BF16)], [d] * 5, tm=128)
        else:
            dx, sums[l]["sc_m"], sums[l]["sh_m"] = _rowwise("lnmod1_bwd", _lnmod_bwd, [(x, 0, d), (dh1, 0, d), (dxa, 0, d)],
                                                            [rows(0)[1]], [(d, F32)], [d, d])
    dmod = jnp.stack([jnp.concatenate([sums[l][k] for k in ("sh_m", "sc_m", "g_m", "sh_f", "sc_f", "g_f")], axis=0)
                      for l in range(depth)])
    return loss_cols, dx, dmod, lgrads, stacked


def _place():
    return lax.axis_index("x"), lax.axis_index("y"), lax.axis_index("c")


def _all_gather8(name, block):
    m_per, n = block.shape

    def body(x_ref, out_ref, send_sems, recv_sems, local_sem):
        x, y, c = _place()
        me, sibling = (x, y, c), (x, y, 1 - c)
        chips = [(1 - x, y), (x, 1 - y), (1 - x, 1 - y)]

        def rows(px, py, pc):
            return out_ref.at[pl.ds(pl.multiple_of((4 * px + 2 * py + pc) * m_per, 8), m_per), :]

        def copy(k, blk, to, src=None):
            return pltpu.make_async_remote_copy(src_ref=rows(*blk) if src is None else src, dst_ref=rows(*blk),
                                                send_sem=send_sems.at[k], recv_sem=recv_sems.at[k],
                                                device_id=to, device_id_type=MESH)

        mine = pltpu.make_async_copy(x_ref, rows(*me), local_sem)
        mine.start()
        first = [copy(0, me, sibling, src=x_ref)] + [copy(1 + j, me, (*chip, c), src=x_ref) for j, chip in enumerate(chips)]
        for cp in first:
            cp.start()
        passed = [copy(4 + j, (*chip, c), sibling) for j, chip in enumerate(chips)]
        for j, chip in enumerate(chips):
            copy(1 + j, (*chip, c), me).wait_recv()
            passed[j].start()
        copy(0, sibling, me).wait_recv()
        for j, chip in enumerate(chips):
            copy(4 + j, (*chip, 1 - c), me).wait_recv()
        for cp in first + passed:
            cp.wait_send()
        mine.wait()

    return pl.pallas_call(
        body, name=name, out_shape=jax.ShapeDtypeStruct((8 * m_per, n), block.dtype),
        in_specs=[pl.BlockSpec(memory_space=pltpu.VMEM)], out_specs=pl.BlockSpec(memory_space=pltpu.VMEM),
        scratch_shapes=[pltpu.SemaphoreType.DMA((7,)), pltpu.SemaphoreType.DMA((7,)), pltpu.SemaphoreType.DMA],
        compiler_params=_params(),
    )(block)


def _other_chips(x, y):
    return [(1 - x, y), (x, 1 - y), (1 - x, 1 - y)]


def _gather_weights(whole, by_rows):
    n = len(whole)

    def body(*refs):
        dst = refs[n:2 * n]
        ici_send, ici_recv, d2d_send, d2d_recv = refs[2 * n:]
        x, y, c = _place()
        chips = _other_chips(x, y)

        def part(ref, k, px, py, pc):
            _, r, cols = whole[k].shape
            q = 2 * px + py
            if by_rows[k]:
                return ref[k].at[:, pl.ds(pl.multiple_of((2 * q + pc) * (r // 8), 16), r // 8), :]
            return ref[k].at[:, pl.ds(pl.multiple_of(pc * (r // 2), 16), r // 2),
                             pl.ds(pl.multiple_of(q * (cols // 4), LANES), cols // 4)]

        def ici(k, j, px, py, to):
            return pltpu.make_async_remote_copy(src_ref=part(dst, k, px, py, c), dst_ref=part(dst, k, px, py, c),
                                                send_sem=ici_send.at[k, j], recv_sem=ici_recv.at[k, j],
                                                device_id=(*to, c), device_id_type=MESH)

        def d2d(k, j, px, py, pc):
            return pltpu.make_async_remote_copy(src_ref=part(dst, k, px, py, pc), dst_ref=part(dst, k, px, py, pc),
                                                send_sem=d2d_send.at[k, j], recv_sem=d2d_recv.at[k, j],
                                                device_id=(x, y, 1 - c), device_id_type=MESH)

        for k in range(n):
            for j, chip in enumerate(chips):
                ici(k, j, x, y, chip).start()
        for k in range(n):
            for j, chip in enumerate(chips):
                ici(k, j, *chip, chip).wait_recv()
                d2d(k, j, *chip, c).start()
        for k in range(n):
            for j, chip in enumerate(chips):
                d2d(k, j, *chip, 1 - c).wait_recv()
        for k in range(n):
            for j, chip in enumerate(chips):
                ici(k, j, x, y, chip).wait_send()
                d2d(k, j, *chip, c).wait_send()

    any_spec = pl.BlockSpec(memory_space=pl.ANY)
    return pl.pallas_call(
        body, name="gather_weights", in_specs=[any_spec] * n, out_specs=[any_spec] * n,
        out_shape=[jax.ShapeDtypeStruct(a.shape, a.dtype) for a in whole], input_output_aliases={k: k for k in range(n)},
        scratch_shapes=[pltpu.SemaphoreType.DMA((n, 3))] * 4,
        compiler_params=_params(),
    )(*whole)


def _part_shape(shape, by_rows):
    l, r, c = shape
    return (l, r // 8, c) if by_rows else (l, r // 2, c // 4)


def _pair_exchange(grads, by_rows):
    n = len(grads)

    def body(*refs):
        src, dst = refs[:n], refs[n:2 * n]
        send_sems, recv_sems = refs[2 * n:]
        x, y, c = _place()

        def window(k, q, pc):
            _, hr, hc = _part_shape(grads[k].shape, by_rows[k])
            if by_rows[k]:
                return src[k].at[:, pl.ds(pl.multiple_of((2 * q + pc) * hr, 16), hr), :]
            return src[k].at[:, pl.ds(pl.multiple_of(pc * hr, 16), hr), pl.ds(q * hc, hc)]

        def copy(k, q, pc):
            return pltpu.make_async_remote_copy(src_ref=window(k, q, pc), dst_ref=dst[k].at[q], send_sem=send_sems.at[k, q],
                                                recv_sem=recv_sems.at[k, q], device_id=(x, y, 1 - c), device_id_type=MESH)

        for k in range(n):
            for q in range(4):
                copy(k, q, 1 - c).start()
        for k in range(n):
            for q in range(4):
                copy(k, q, c).wait_recv()
        for k in range(n):
            for q in range(4):
                copy(k, q, 1 - c).wait_send()

    any_spec = pl.BlockSpec(memory_space=pl.ANY)
    return pl.pallas_call(
        body, name="pair_exchange", in_specs=[any_spec] * n, out_specs=[any_spec] * n,
        out_shape=[jax.ShapeDtypeStruct((4, *_part_shape(g.shape, rows)), g.dtype) for g, rows in zip(grads, by_rows)],
        scratch_shapes=[pltpu.SemaphoreType.DMA((n, 4)), pltpu.SemaphoreType.DMA((n, 4))],
        compiler_params=_params(),
    )(*grads)


def _pair_sum(name, g, theirs, by_rows, c, chip):
    _, l, hr, hc = theirs.shape
    tr = _pick(hr, (256, 176, 128, 64, 32))

    def body(s_ref, g_ref, t_ref, p_ref, own_ref):
        v = (g_ref[...].astype(F32) + t_ref[0].astype(F32)).astype(BF16)
        p_ref[0] = v

        @pl.when(pl.program_id(2) == s_ref[1])
        def _():
            own_ref[0] = v

    if by_rows:
        g_spec = pl.BlockSpec((1, tr, hc), lambda li, i, q, s: (li, (2 * q + s[0]) * (hr // tr) + i, 0))
    else:
        g_spec = pl.BlockSpec((1, tr, hc), lambda li, i, q, s: (li, s[0] * (hr // tr) + i, q))
    slot = pl.BlockSpec((1, 1, tr, hc), lambda li, i, q, s: (q, li, i, 0))
    grid_spec = pltpu.PrefetchScalarGridSpec(
        num_scalar_prefetch=1, grid=(l, hr // tr, 4), in_specs=[g_spec, slot],
        out_specs=[slot, pl.BlockSpec((1, 1, tr, hc), lambda li, i, q, s: (s[1], li, i, 0))])
    return pl.pallas_call(
        body, name=name, grid_spec=grid_spec, out_shape=[jax.ShapeDtypeStruct(theirs.shape, BF16)] * 2,
        compiler_params=_params(dimension_semantics=("arbitrary", "arbitrary", "arbitrary")),
    )(jnp.stack([c, chip]).astype(jnp.int32), g, theirs)


def _chip_scatter(sums, landing):
    n = len(sums)

    def body(*refs):
        src, dst = refs[:n], refs[2 * n:3 * n]
        send_sems, recv_sems = refs[3 * n:]
        x, y, c = _place()
        mine = 2 * x + y

        def copy(k, j, src_slot, dst_slot, to):
            return pltpu.make_async_remote_copy(src_ref=src[k].at[src_slot], dst_ref=dst[k].at[dst_slot],
                                                send_sem=send_sems.at[k, j], recv_sem=recv_sems.at[k, j],
                                                device_id=(*to, c), device_id_type=MESH)

        chips = _other_chips(x, y)
        for k in range(n):
            for j, (px, py) in enumerate(chips):
                copy(k, j, 2 * px + py, mine, (px, py)).start()
        for k in range(n):
            for j, (px, py) in enumerate(chips):
                copy(k, j, mine, 2 * px + py, (px, py)).wait_recv()
        for k in range(n):
            for j, (px, py) in enumerate(chips):
                copy(k, j, 2 * px + py, mine, (px, py)).wait_send()

    any_spec = pl.BlockSpec(memory_space=pl.ANY)
    return pl.pallas_call(
        body, name="chip_scatter", in_specs=[any_spec] * (2 * n), out_specs=[any_spec] * n,
        out_shape=[jax.ShapeDtypeStruct(a.shape, a.dtype) for a in landing],
        input_output_aliases={n + k: k for k in range(n)},
        scratch_shapes=[pltpu.SemaphoreType.DMA((n, 3)), pltpu.SemaphoreType.DMA((n, 3))],
        compiler_params=_params(),
    )(*sums, *landing)


def _sum_slots(name, parts, half=None):
    slots, l, r, c = parts.shape
    tr = _pick(r, (256, 176, 128, 64, 32, 8))

    def body(*refs):
        p_ref, o_ref = refs[-2:]
        acc = p_ref[0].astype(F32)
        for i in range(1, slots):
            acc = acc + p_ref[i].astype(F32)
        o_ref[...] = acc

    if half is None:
        return pl.pallas_call(
            body, name=name, grid=(l, r // tr), in_specs=[pl.BlockSpec((slots, 1, tr, c), lambda li, i: (0, li, i, 0))],
            out_specs=pl.BlockSpec((1, tr, c), lambda li, i: (li, i, 0)), out_shape=jax.ShapeDtypeStruct((l, r, c), F32),
            compiler_params=_params(dimension_semantics=("arbitrary", "arbitrary")),
        )(parts)
    grid_spec = pltpu.PrefetchScalarGridSpec(
        num_scalar_prefetch=1, grid=(l, r // tr),
        in_specs=[pl.BlockSpec((slots, 1, tr, c), lambda li, i, h: (0, li, i, 0))],
        out_specs=pl.BlockSpec((1, tr, c), lambda li, i, h: (li, h[0] * (r // tr) + i, 0)))
    return pl.pallas_call(
        body, name=name, grid_spec=grid_spec, out_shape=jax.ShapeDtypeStruct((l, 2 * r, c), F32),
        compiler_params=_params(dimension_semantics=("arbitrary", "arbitrary")),
    )(jnp.reshape(half, (1,)).astype(jnp.int32), parts)


def _swap_halves(blocks):
    n = len(blocks)

    def body(*refs):
        src, dst = refs[:n], refs[n:2 * n]
        send_sems, recv_sems = refs[2 * n:]
        x, y, c = _place()

        def half(ref, k, pc):
            r = blocks[k].shape[1] // 2
            return ref[k].at[:, pl.ds(pl.multiple_of(pc * r, 8), r), :]

        def copy(k, pc):
            return pltpu.make_async_remote_copy(src_ref=half(src, k, pc), dst_ref=half(dst, k, pc), send_sem=send_sems.at[k],
                                                recv_sem=recv_sems.at[k], device_id=(x, y, 1 - c), device_id_type=MESH)

        for k in range(n):
            copy(k, c).start()
        for k in range(n):
            copy(k, 1 - c).wait_recv()
        for k in range(n):
            copy(k, c).wait_send()

    any_spec = pl.BlockSpec(memory_space=pl.ANY)
    return pl.pallas_call(
        body, name="swap_halves", in_specs=[any_spec] * n, out_specs=[any_spec] * n,
        out_shape=[jax.ShapeDtypeStruct(b.shape, b.dtype) for b in blocks], input_output_aliases={k: k for k in range(n)},
        scratch_shapes=[pltpu.SemaphoreType.DMA((n,)), pltpu.SemaphoreType.DMA((n,))],
        compiler_params=_params(),
    )(*blocks)


def _adamw(name, w, g, m, v):
    shape = w.shape
    cols = shape[-1] if w.ndim > 1 and shape[-1] % LANES == 0 else w.size if w.size % LANES else LANES
    flat = lambda a: a.reshape(-1, cols)
    rows = w.size // cols
    tr = _pick(rows, [r for r in (512, 256, 128, 64, 32, 16, 8) if r * cols <= 256 * 1024]) if rows % 8 == 0 else rows

    def body(w_ref, g_ref, m_ref, v_ref, go_ref, d_ref, nm_ref, nv_ref):
        gg = g_ref[...]
        go_ref[...] = gg
        nm = ADAM_B1 * m_ref[...] + (1.0 - ADAM_B1) * gg
        nv = ADAM_B2 * v_ref[...] + (1.0 - ADAM_B2) * (gg * gg)
        m_hat = nm / (1.0 - ADAM_B1 ** ADAM_STEP)
        v_hat = nv / (1.0 - ADAM_B2 ** ADAM_STEP)
        d_ref[...] = -ADAM_LR * (m_hat / (jnp.sqrt(v_hat) + ADAM_EPS) + ADAM_WD * w_ref[...])
        nm_ref[...] = nm
        nv_ref[...] = nv

    spec = pl.BlockSpec((tr, cols), lambda i: (i, 0))
    out = pl.pallas_call(
        body, name=name, grid=(rows // tr,), in_specs=[spec] * 4, out_specs=[spec] * 4,
        out_shape=[jax.ShapeDtypeStruct((rows, cols), F32)] * 4,
        compiler_params=_params(dimension_semantics=("arbitrary",)),
    )(flat(w), flat(g), flat(m), flat(v))
    return tuple(o.reshape(shape) for o in out)


WEIGHTS = ["w_ada", "b_ada", "w_in", "w_sb_up", "ssm_a_re", "ssm_a_im", "ssm_log_dt", "ssm_b_re", "ssm_b_im", "ssm_c_re",
           "ssm_c_im", "ssm_d", "w_glu", "b_glu", "w_ssm_up", "w_out", "ln1_g", "ln1_b", "w_ffn_in", "w_ffn_out", "ln2_g",
           "ln2_b"]
COL_SPLIT = ["w_in", "w_sb_up", "w_ssm_up", "w_ffn_in"]
ROW_SPLIT = ["w_glu", "w_out", "w_ffn_out"]
SMALL = ["ssm_a_re", "ssm_a_im", "ssm_log_dt", "ssm_b_re", "ssm_b_im", "ssm_c_re", "ssm_c_im", "ssm_d", "b_glu", "ln1_g",
         "ln1_b", "ln2_g", "ln2_b"]
SLAB_COLS = 1024


def _cast_into_whole(name, w, by_rows, chip):
    l, r, cols = w.shape
    tr = _pick(r, (512, 256, 128, 64, 16))

    def body(q_ref, w_ref, o_ref):
        o_ref[...] = w_ref[...].astype(BF16)

    if by_rows:
        out_map, shape = (lambda li, i, q: (li, q[0] * (r // tr) + i, 0)), (l, 4 * r, cols)
    else:
        out_map, shape = (lambda li, i, q: (li, i, q[0])), (l, r, 4 * cols)
    grid_spec = pltpu.PrefetchScalarGridSpec(
        num_scalar_prefetch=1, grid=(l, r // tr), in_specs=[pl.BlockSpec((1, tr, cols), lambda li, i, q: (li, i, 0))],
        out_specs=pl.BlockSpec((1, tr, cols), out_map))
    return pl.pallas_call(body, name=name, grid_spec=grid_spec, out_shape=jax.ShapeDtypeStruct(shape, BF16),
                          compiler_params=_params(dimension_semantics=("arbitrary", "arbitrary")),
                          )(jnp.reshape(chip, (1,)).astype(jnp.int32), w)


def _silu_rows(name, c):
    def body(c_ref, o_ref):
        v = c_ref[...]
        o_ref[...] = v * jax.nn.sigmoid(v)

    return pl.pallas_call(body, name=name, out_shape=jax.ShapeDtypeStruct(c.shape, F32), compiler_params=_params())(c)


def _pad_rows(v, mult=8):
    flat = v.reshape(-1)
    per = mult * SLAB_COLS
    total = -(-flat.size // per) * per
    return jnp.pad(flat, (0, total - flat.size)).reshape(-1, SLAB_COLS)


def kernel(x, c, w_ada, b_ada, w_in, w_sb_up, ssm_a_re, ssm_a_im, ssm_log_dt, ssm_b_re, ssm_b_im, ssm_c_re, ssm_c_im, ssm_d, w_glu, b_glu, w_ssm_up, w_out, ln1_g, ln1_b, w_ffn_in, w_ffn_out, ln2_g, ln2_b, loss_target, m_w_ada, m_b_ada, m_w_in, m_w_sb_up, m_ssm_a_re, m_ssm_a_im, m_ssm_log_dt, m_ssm_b_re, m_ssm_b_im, m_ssm_c_re, m_ssm_c_im, m_ssm_d, m_w_glu, m_b_glu, m_w_ssm_up, m_w_out, m_ln1_g, m_ln1_b, m_w_ffn_in, m_w_ffn_out, m_ln2_g, m_ln2_b, v_w_ada, v_b_ada, v_w_in, v_w_sb_up, v_ssm_a_re, v_ssm_a_im, v_ssm_log_dt, v_ssm_b_re, v_ssm_b_im, v_ssm_c_re, v_ssm_c_im, v_ssm_d, v_w_glu, v_b_glu, v_w_ssm_up, v_w_out, v_ln1_g, v_ln1_b, v_w_ffn_in, v_w_ffn_out, v_ln2_g, v_ln2_b):
    args = dict(locals())
    w = {n: args[n] for n in WEIGHTS}
    mom = {n: args["m_" + n] for n in WEIGHTS}
    var = {n: args["v_" + n] for n in WEIGHTS}
    depth, d = w_ada.shape[0], x.shape[-1]
    xi, yi, ci = _place()
    me, chip = 4 * xi + 2 * yi + ci, 2 * xi + yi
    ada_cols = w_ada.shape[2]

    big = COL_SPLIT + ROW_SPLIT
    by_rows = [n in ROW_SPLIT for n in big]
    full = dict(zip(big, _gather_weights([_cast_into_whole(f"cast_{n}", w[n], n in ROW_SPLIT, chip) for n in big], by_rows)))

    c_all = _all_gather8("gather_c", jnp.pad(c, ((0, 7), (0, 0))))[::8]
    c_act = _silu_rows("silu_c", c_all)
    b_cols = lax.dynamic_slice_in_dim(b_ada, chip * ada_cols, ada_cols, axis=1)
    mod_part = jnp.concatenate([_small_mm(f"mod_{l}", c_act, w_ada[l], "nn") + b_cols[l][None] for l in range(depth)], axis=0)
    mod_all = _all_gather8("gather_mod", mod_part).reshape(4, 2, depth, 8, ada_cols)[:, 0]
    mod_mine = lax.dynamic_index_in_dim(mod_all, me, axis=2, keepdims=False)
    mod = mod_mine.transpose(1, 0, 2).reshape(depth, 6, d)

    layer_w = [{**{n: (full[n], l) for n in big}, **{n: w[n][l] for n in SMALL}} for l in range(depth)]
    loss_cols, dx, dmods, lgrads, stacked = _local_step(x[0], loss_target[0], mod, layer_w)
    loss = lax.psum(jnp.sum(loss_cols), ("x", "y", "c"))
    grad_x = dx[None]

    theirs = _pair_exchange([stacked[n] for n in big], by_rows)
    pairs = [_pair_sum(f"pair_{n}", stacked[n], t, n in ROW_SPLIT, ci, chip) for n, t in zip(big, theirs)]
    landed = _chip_scatter([p[0] for p in pairs], [p[1] for p in pairs])
    halves = [_sum_slots(f"sum_{n}", p, half=ci) for n, p in zip(big, landed)]
    grad = dict(zip(big, _swap_halves(halves)))

    pieces = [dmods] + [jnp.stack([lgrads[l][n] for l in range(depth)]) for n in SMALL]
    slab = jnp.concatenate([_pad_rows(p) for p in pieces], axis=0)
    slabs = _all_gather8("gather_small", slab).reshape(8, 1, *slab.shape)
    total = _sum_slots("sum_small", slabs)[0]
    row = _pad_rows(pieces[0]).shape[0]
    for n, p in zip(SMALL, pieces[1:]):
        rows = _pad_rows(p).shape[0]
        grad[n] = total[row:row + rows].reshape(-1)[:p.size].reshape(p.shape)
        row += rows
    dmod_rows = _pad_rows(pieces[0]).shape[0]
    dmod_all = slabs[:, 0, :dmod_rows].reshape(8, -1)[:, :depth * 6 * d].reshape(8, depth, 4, ada_cols)
    dmod_cols = lax.dynamic_index_in_dim(dmod_all, chip, axis=2, keepdims=False)
    grad["w_ada"] = jnp.stack([_small_mm(f"dw_ada_{l}", c_act, dmod_cols[:, l], "tn") for l in range(depth)])
    dmod_sum = _sum_slots("sum_dmod", slabs[:, :, :dmod_rows])[0]
    grad["b_ada"] = dmod_sum.reshape(-1)[:depth * 6 * d].reshape(depth, 6 * d)

    delta, new_m, new_v = {}, {}, {}
    for n in WEIGHTS:
        grad[n], delta[n], new_m[n], new_v[n] = _adamw(f"adamw_{n}", w[n], grad[n], mom[n], var[n])
    return (loss, grad_x, *[grad[n] for n in WEIGHTS], *[delta[n] for n in WEIGHTS], *[new_m[n] for n in WEIGHTS],
            *[new_v[n] for n in WEIGHTS])
```

```python
import functools
import math

import jax
import jax.numpy as jnp
from jax import lax
from jax.experimental import pallas as pl
from jax.experimental.pallas import tpu as pltpu

F32 = jnp.float32
BF16 = jnp.bfloat16
MESH = pl.DeviceIdType.MESH

LANES = 128
HEAD_DIM = 64
SB_WIDTH = 512
ATT_TILE = 256
SSM_GROUPS, SSM_STATE, SSM_GROUP = 32, 64, 16
N_STATE = SSM_GROUPS * SSM_STATE
SSM_BLOCKS = SSM_GROUPS * SSM_GROUP // LANES
U_OFFSET = 3 * 512
LN_EPS = 1e-5
DEPTH = 2
ALPHA = (2 * DEPTH) ** 0.25
ADAM_LR, ADAM_B1, ADAM_B2, ADAM_EPS, ADAM_WD, ADAM_STEP = 0.001, 0.9, 0.999, 1e-08, 0.01, 10
VMEM_LIMIT = 56 * 1024 * 1024
GELU_K = math.sqrt(2.0 / math.pi)
GELU_C = 0.044715


def _params(**kw):
    return pltpu.CompilerParams(vmem_limit_bytes=VMEM_LIMIT, **kw)


def _pick(n, prefs):
    for p in prefs:
        if n % p == 0:
            return p
    return n


def _rowwise(name, fn, rows, vecs, outs, sums=(), tm=None):
    s = rows[0][0].shape[0]
    tm = tm or _pick(s, (256, 128, 64, 8))
    nin, no, ns = len(rows) + len(vecs), len(outs), len(sums)

    def body(*refs):
        res = fn(*[r[...].astype(F32) for r in refs[:nin]])
        res = res if isinstance(res, tuple) else (res,)
        for r, v in zip(refs[nin:nin + no], res[:no]):
            r[...] = v.astype(r.dtype)
        if ns:
            @pl.when(pl.program_id(0) == 0)
            def _():
                for r in refs[nin + no:]:
                    r[...] = jnp.zeros_like(r)
            for r, v in zip(refs[nin + no:], res[no:]):
                r[...] += v

    in_specs = [pl.BlockSpec((tm, w), lambda i, cb=cb: (i, cb)) for _, cb, w in rows]
    in_specs += [pl.BlockSpec(v.shape, lambda i: (0, 0)) for v in vecs]
    out_specs = [pl.BlockSpec((tm, w), lambda i: (i, 0)) for w, _ in outs]
    out_specs += [pl.BlockSpec((1, w), lambda i: (0, 0)) for w in sums]
    out_shape = [jax.ShapeDtypeStruct((s, w), dt) for w, dt in outs]
    out_shape += [jax.ShapeDtypeStruct((1, w), F32) for w in sums]
    res = pl.pallas_call(
        body, name=name, grid=(s // tm,), in_specs=in_specs, out_specs=out_specs, out_shape=out_shape,
        compiler_params=_params(dimension_semantics=("arbitrary",)),
    )(*[a for a, _, _ in rows], *vecs)
    return res[0] if len(res) == 1 else tuple(res)


MM_TILES = (1408, 1024, 512, 256, 128)


def _slab_spec(block, index, slab):
    if slab is None:
        return pl.BlockSpec(block, index)
    return pl.BlockSpec((None, *block), lambda *g: (slab, *index(*g)))


def _mm(name, a, b, mode, out_dtype=F32, into=None):
    b, b_slab = b if isinstance(b, tuple) else (b, None)
    if mode == "nn":
        m, k, n = a.shape[0], a.shape[1], b.shape[-1]
    elif mode == "nt":
        m, k, n = a.shape[0], a.shape[1], b.shape[-2]
    else:
        k, m, n = a.shape[0], a.shape[1], b.shape[-1]
    tm = _pick(m, MM_TILES if mode == "tn" else (2048,) + MM_TILES[1:])
    tn = _pick(n, MM_TILES)
    tk = _pick(k, (2048,) + MM_TILES if mode == "tn" else MM_TILES)
    nk = k // tk
    dims = {"nn": ((1,), (0,)), "nt": ((1,), (1,)), "tn": ((0,), (0,))}[mode]

    def body(a_ref, b_ref, *rest):
        o_ref = rest[-2] if nk > 1 else rest[-1]
        prod = lax.dot_general(a_ref[...].astype(BF16), b_ref[...].astype(BF16), (dims, ((), ())),
                               preferred_element_type=F32)
        if nk == 1:
            o_ref[...] = prod.astype(o_ref.dtype)
            return
        acc_ref = rest[-1]
        kk = pl.program_id(2)

        @pl.when(kk == 0)
        def _():
            acc_ref[...] = prod

        @pl.when(kk > 0)
        def _():
            acc_ref[...] += prod

        @pl.when(kk == nk - 1)
        def _():
            o_ref[...] = acc_ref[...].astype(o_ref.dtype)

    if mode == "tn":
        a_spec = pl.BlockSpec((tk, tm), lambda i, j, kk: (kk, i))
    else:
        a_spec = pl.BlockSpec((tm, tk), lambda i, j, kk: (i, kk))
    b_block, b_index = ((tn, tk), lambda i, j, kk: (j, kk)) if mode == "nt" else ((tk, tn), lambda i, j, kk: (kk, j))
    b_spec = _slab_spec(b_block, b_index, b_slab)
    in_specs, operands, aliases = [a_spec, b_spec], [a, b], {}
    if into is None:
        out_spec = pl.BlockSpec((tm, tn), lambda i, j, kk: (i, j))
        out_shape = jax.ShapeDtypeStruct((m, n), out_dtype)
    else:
        buf, slab, count = into
        out_spec = pl.BlockSpec((None, tm, tn), lambda i, j, kk: (slab, i, j))
        out_shape = jax.ShapeDtypeStruct((count, m, n), out_dtype)
        if buf is not None:
            in_specs.append(pl.BlockSpec(memory_space=pl.ANY))
            operands.append(buf)
            aliases = {2: 0}
    return pl.pallas_call(
        body, name=name, grid=(m // tm, n // tn, nk), in_specs=in_specs, out_specs=out_spec, out_shape=out_shape,
        scratch_shapes=[pltpu.VMEM((tm, tn), F32)] if nk > 1 else [], input_output_aliases=aliases,
        compiler_params=_params(dimension_semantics=("arbitrary", "arbitrary", "arbitrary")),
    )(*operands)


def _ffn_in(name, h, w):
    w, slab = w if isinstance(w, tuple) else (w, None)
    s, d = h.shape
    f = w.shape[-1] // 2
    tm, tn = _pick(s, MM_TILES[2:]), _pick(f, MM_TILES)

    def body(h_ref, wg_ref, wu_ref, g_ref, u_ref, a_ref):
        hb = h_ref[...]
        g, u = _nn(hb, wg_ref[...]), _nn(hb, wu_ref[...])
        g_ref[...] = g.astype(BF16)
        u_ref[...] = u.astype(BF16)
        a_ref[...] = (g * jax.nn.sigmoid(g) * u).astype(BF16)

    out = pl.BlockSpec((tm, tn), lambda i, j: (i, j))
    return pl.pallas_call(
        body, name=name, grid=(s // tm, f // tn),
        in_specs=[pl.BlockSpec((tm, d), lambda i, j: (i, 0)), _slab_spec((d, tn), lambda i, j: (0, j), slab),
                  _slab_spec((d, tn), lambda i, j: (0, f // tn + j), slab)],
        out_specs=[out, out, out], out_shape=[jax.ShapeDtypeStruct((s, f), BF16)] * 3,
        compiler_params=_params(dimension_semantics=("arbitrary", "arbitrary")),
    )(h, w, w)


def _small_mm(name, a, b, mode):
    dims = {"nn": ((1,), (0,)), "tn": ((0,), (0,))}[mode]
    m = a.shape[0] if mode == "nn" else a.shape[1]

    def body(a_ref, b_ref, o_ref):
        o_ref[...] = lax.dot_general(a_ref[...], b_ref[...], (dims, ((), ())), precision=lax.Precision.HIGHEST,
                                     preferred_element_type=F32)

    return pl.pallas_call(body, name=name, out_shape=jax.ShapeDtypeStruct((m, b.shape[1]), F32),
                          compiler_params=_params())(a, b)


def _norm(x):
    mu = jnp.mean(x, axis=-1, keepdims=True)
    xc = x - mu
    rstd = lax.rsqrt(jnp.mean(xc * xc, axis=-1, keepdims=True) + LN_EPS)
    return xc * rstd, rstd


def _norm_bwd(dn, n, rstd):
    return rstd * (dn - jnp.mean(dn, axis=-1, keepdims=True) - n * jnp.mean(dn * n, axis=-1, keepdims=True))


def _colsum(v):
    return jnp.sum(v, axis=0, keepdims=True)


def _gelu(x):
    return 0.5 * x * (1.0 + jnp.tanh(GELU_K * (x + GELU_C * x * x * x)))


def _gelu_grad(x):
    t = jnp.tanh(GELU_K * (x + GELU_C * x * x * x))
    return 0.5 * (1.0 + t) + 0.5 * x * (1.0 - t * t) * GELU_K * (1.0 + 3.0 * GELU_C * x * x)


def _log_sigmoid_parts(z):
    lb = jnp.minimum(z, 0.0) - jnp.log(1.0 + jnp.exp(-jnp.abs(z)))
    return lb, lb - z


def _kv_transposed(proj, t):
    s = proj.shape[0]
    nb, nhp = s // t, SB_WIDTH // LANES

    def body(k_ref, v_ref, kt_ref, vt_ref):
        k, v = k_ref[...].astype(F32), v_ref[...].astype(F32)
        for hp in range(nhp):
            kt_ref[hp, 0] = k[:, hp * LANES:(hp + 1) * LANES].T.astype(BF16)
            vt_ref[hp, 0] = v[:, hp * LANES:(hp + 1) * LANES].T.astype(BF16)

    col = lambda cb: pl.BlockSpec((t, SB_WIDTH), lambda i, cb=cb: (i, cb))
    t_out = pl.BlockSpec((nhp, 1, LANES, t), lambda i: (0, i, 0, 0))
    return pl.pallas_call(
        body, name="kv_transposed", grid=(nb,), in_specs=[col(1), col(2)], out_specs=[t_out, t_out],
        out_shape=[jax.ShapeDtypeStruct((nhp, nb, LANES, t), BF16)] * 2,
        compiler_params=_params(dimension_semantics=("arbitrary",)),
    )(proj, proj)


def _tile_masks(t):
    row = lax.broadcasted_iota(jnp.int32, (t, t), 0)
    col = lax.broadcasted_iota(jnp.int32, (t, t), 1)
    return row, col


DEAD_LOG_WEIGHT = -110.0


def _walk_down(i, tiles, state, alive):
    st = lax.cond(i == 0, lambda s_: tiles([i], s_, [True]), lambda s_: tiles([i, i - 1], s_, [True, False]), state)
    n = jnp.maximum(i - 1, 0)

    def pair(c):
        return c[0] + 1, tiles([i - 2 - 2 * c[0], i - 3 - 2 * c[0]], c[1], [False, False])

    p, st = lax.while_loop(lambda c: (c[0] < n // 2) & alive(c[1]), pair, (jnp.int32(0), st))
    return lax.cond((n % 2 == 1) & (p == n // 2) & alive(st), lambda s_: tiles([0], s_, [False]), lambda s_: s_, st)


def _walk_up(i, first, tiles, state):
    n = jnp.maximum(i - 1 - first, 0)
    st = lax.fori_loop(0, n // 2, lambda p, s_: tiles([first + 2 * p, first + 2 * p + 1], s_, [False, False]), state)
    st = lax.cond(n % 2 == 1, lambda s_: tiles([i - 2], s_, [False]), lambda s_: s_, st)
    return lax.cond(i == 0, lambda s_: tiles([i], s_, [True]), lambda s_: tiles([i - 1, i], s_, [False, True]), st)


def _nt(a, b):
    return lax.dot_general(a, b, (((1,), (1,)), ((), ())), preferred_element_type=F32)


def _nn(a, b):
    return jnp.dot(a, b, preferred_element_type=F32)


def _attn_fwd(proj, vt3, t):
    s = proj.shape[0]
    nb, nhp = s // t, SB_WIDTH // LANES

    def body(q_ref, k_ref, vt_ref, o_ref, car_ref):
        i = pl.program_id(1)
        q2 = q_ref[...] * (1.0 / math.sqrt(HEAD_DIM))
        lane_q = lax.broadcasted_iota(jnp.int32, q2.shape, 1)
        row, col = _tile_masks(t)
        later = (col > row).astype(BF16)
        valid = row < col
        orow = lax.broadcasted_iota(jnp.int32, (LANES, t), 0)
        car_ref[...] = jnp.full(car_ref.shape, 2.0 * DEAD_LOG_WEIGHT, F32)
        qh = [jnp.where((lane_q < HEAD_DIM) == (hh == 0), q2, jnp.zeros_like(q2)) for hh in range(2)]

        def tiles(js, state, diagonal):
            chains = [(n, hh) for n in range(len(js)) for hh in range(2)]
            kb = [k_ref[pl.ds(pl.multiple_of(j * t, t), t), :] for j in js]
            z = {ch: _nt(kb[ch[0]], qh[ch[1]]) for ch in chains}
            lb, aft, csum = {}, {}, {}
            for ch in chains:
                lb[ch], l1m = _log_sigmoid_parts(z[ch])
                if diagonal[ch[0]]:
                    l1m = jnp.where(valid, l1m, 0.0)
                aft[ch] = _nn(later, l1m.astype(BF16))
                csum[ch] = _colsum(l1m)
            state = list(state)
            for ch in chains:
                n, hh = ch
                c_after, acc = state[hh]
                w = jnp.exp(lb[ch] + aft[ch] + c_after)
                if diagonal[ch[0]]:
                    w = jnp.where(valid, w, 0.0)
                car_ref[hh, pl.ds(js[n], 1), :] = c_after
                state[hh] = (c_after + csum[ch], acc + _nn(vt_ref[0, js[n]], w.astype(BF16)))
            return tuple(state)

        def alive(state):
            return jnp.max(jnp.maximum(state[0][0], state[1][0])) >= DEAD_LOG_WEIGHT

        zero = (jnp.zeros((1, t), F32), jnp.zeros((LANES, t), F32))
        (_, acc0), (_, acc1) = _walk_down(i, tiles, (zero, zero), alive)
        o_ref[...] = jnp.where(orow < HEAD_DIM, acc0, acc1).T.astype(o_ref.dtype)

    return pl.pallas_call(
        body, name="attn_fwd", grid=(nhp, nb),
        in_specs=[pl.BlockSpec((t, LANES), lambda hp, i: (i, hp)),
                  pl.BlockSpec((s, LANES), lambda hp, i: (0, nhp + hp)),
                  pl.BlockSpec((1, nb, LANES, t), lambda hp, i: (hp, 0, 0, 0))],
        out_specs=[pl.BlockSpec((t, LANES), lambda hp, i: (i, hp)),
                   pl.BlockSpec((2, nb, t), lambda hp, i: (hp, 0, i))],
        out_shape=[jax.ShapeDtypeStruct((s, nhp * LANES), BF16), jax.ShapeDtypeStruct((2 * nhp, nb, s), F32)],
        compiler_params=_params(dimension_semantics=("arbitrary", "arbitrary")),
    )(proj, proj, vt3)


def _attn_bwd(proj, do, kt3, car, t):
    s = proj.shape[0]
    nb, nhp = s // t, SB_WIDTH // LANES

    def body(q_ref, do_ref, k_ref, v_ref, kt_ref, car_ref, dq_ref, dk_ref, dv_ref):
        i = pl.program_id(1)

        @pl.when(i == 0)
        def _():
            dk_ref[...] = jnp.zeros_like(dk_ref)
            dv_ref[...] = jnp.zeros_like(dv_ref)

        q2, do2 = q_ref[...] * (1.0 / math.sqrt(HEAD_DIM)), do_ref[...]
        lane_q = lax.broadcasted_iota(jnp.int32, q2.shape, 1)
        row, col = _tile_masks(t)
        later = (col > row).astype(BF16)
        earlier = (col < row).astype(BF16)
        valid = row < col
        orow = lax.broadcasted_iota(jnp.int32, (LANES, t), 0)
        head = [(lane_q < HEAD_DIM) == (hh == 0) for hh in range(2)]
        qh = [jnp.where(hm, q2, jnp.zeros_like(q2)) for hm in head]
        doh = [jnp.where(hm, do2, jnp.zeros_like(do2)) for hm in head]

        def tiles(js, state, diagonal):
            chains = [(n, hh) for n in range(len(js)) for hh in range(2)]
            rows = [pl.ds(pl.multiple_of(j * t, t), t) for j in js]
            kb = [k_ref[r, :] for r in rows]
            vb = [v_ref[r, :] for r in rows]
            z = {ch: _nt(kb[ch[0]], qh[ch[1]]) for ch in chains}
            dw = {ch: _nt(vb[ch[0]], doh[ch[1]]) for ch in chains}
            lb, beta, aft = {}, {}, {}
            for ch in chains:
                lb[ch], l1m = _log_sigmoid_parts(z[ch])
                beta[ch] = jnp.exp(lb[ch])
                if diagonal[ch[0]]:
                    l1m = jnp.where(valid, l1m, 0.0)
                aft[ch] = _nn(later, l1m.astype(BF16))
            w, g, gsum, g_in = {}, {}, {}, {}
            for ch in chains:
                n, hh = ch
                w[ch] = jnp.exp(lb[ch] + aft[ch] + car_ref[hh, pl.ds(js[n], 1), :])
                if diagonal[ch[0]]:
                    w[ch] = jnp.where(valid, w[ch], 0.0)
                g[ch] = dw[ch] * w[ch]
                g_in[ch] = _nn(earlier, g[ch].astype(BF16))
                gsum[ch] = _colsum(g[ch])
            state = list(state)
            dk_t, dv_t = [None] * len(js), [None] * len(js)
            for ch in chains:
                n, hh = ch
                c_g, dqt = state[hh]
                dz = g[ch] - beta[ch] * (g[ch] + g_in[ch] + c_g)
                if diagonal[ch[0]]:
                    dz = jnp.where(valid, dz, 0.0)
                dzb, wb = dz.astype(BF16), w[ch].astype(BF16)
                dk_h, dv_h = _nn(dzb, qh[hh]), _nn(wb, doh[hh])
                dk_t[n] = dk_h if dk_t[n] is None else dk_t[n] + dk_h
                dv_t[n] = dv_h if dv_t[n] is None else dv_t[n] + dv_h
                state[hh] = (c_g + gsum[ch], dqt + _nn(kt_ref[0, js[n]], dzb))
            for n in range(len(js)):
                dk_ref[rows[n], :] += dk_t[n]
                dv_ref[rows[n], :] += dv_t[n]
            return tuple(state)

        reach = jnp.max(jnp.max(car_ref[...], axis=2, keepdims=True), axis=0)
        dead = (reach < DEAD_LOG_WEIGHT) & (lax.broadcasted_iota(jnp.int32, reach.shape, 0) < i)
        first = jnp.sum(jnp.where(dead, 1.0, 0.0)).astype(jnp.int32)
        zero = (jnp.zeros((1, t), F32), jnp.zeros((LANES, t), F32))
        (_, dq0), (_, dq1) = _walk_up(i, first, tiles, (zero, zero))
        dq_ref[...] = jnp.where(orow < HEAD_DIM, dq0, dq1).T

    tile_spec = pl.BlockSpec((t, LANES), lambda hp, i: (i, hp))
    whole = pl.BlockSpec((s, LANES), lambda hp, i: (0, hp))
    return pl.pallas_call(
        body, name="attn_bwd", grid=(nhp, nb),
        in_specs=[tile_spec, tile_spec, pl.BlockSpec((s, LANES), lambda hp, i: (0, nhp + hp)),
                  pl.BlockSpec((s, LANES), lambda hp, i: (0, 2 * nhp + hp)),
                  pl.BlockSpec((1, nb, LANES, t), lambda hp, i: (hp, 0, 0, 0)),
                  pl.BlockSpec((2, nb, t), lambda hp, i: (hp, 0, i))],
        out_specs=[tile_spec, whole, whole],
        out_shape=[jax.ShapeDtypeStruct((s, nhp * LANES), F32)] * 3,
        compiler_params=_params(dimension_semantics=("arbitrary", "arbitrary")),
    )(proj, do, proj, proj, kt3, car)


SCAN_LANES = 1024
SCAN_ROWS = 8
S5_CHUNKS = 4


def _scan_chunks(v):
    n = v.shape[1] // (2 * LANES)
    return [(v[:, c * 2 * LANES:c * 2 * LANES + LANES], v[:, c * 2 * LANES + LANES:(c + 1) * 2 * LANES]) for c in range(n)]


def _scan_tables(lr, li, reverse):
    if reverse:
        li = -li
    row = lax.broadcasted_iota(jnp.int32, (SCAN_ROWS, LANES), 0)
    powers = [(lr, li)]
    for _ in range(SCAN_ROWS - 1):
        pr, pi = powers[-1]
        powers.append((pr * lr - pi * li, pr * li + pi * lr))
    levels = []
    for d in (1, 2, 4):
        keep = (row < SCAN_ROWS - d) if reverse else (row >= d)
        levels.append((SCAN_ROWS - d if reverse else d,
                       (jnp.where(keep, powers[d - 1][0], 0.0), jnp.where(keep, powers[d - 1][1], 0.0))))
    pr = pi = jnp.zeros((SCAN_ROWS, LANES), F32)
    for r in range(SCAN_ROWS):
        steps = SCAN_ROWS - r if reverse else r + 1
        pr = jnp.where(row == r, powers[steps - 1][0], pr)
        pi = jnp.where(row == r, powers[steps - 1][1], pi)
    return levels, (pr, pi)


def _s5_fwd(proj, u_off, bmat, cmat, lam):
    s, w = proj.shape[0], SSM_BLOCKS * bmat.shape[1]
    tt = _pick(s, (512, 256, 128, 8))
    nt = s // tt
    cin = bmat.shape[0] // SSM_BLOCKS
    chunk = tt // S5_CHUNKS

    def body(u_ref, b_ref, c_ref, lam_ref, h_ref, y_ref, x_ref, st_ref):
        @pl.when(pl.program_id(1) == 0)
        def _():
            st_ref[...] = jnp.zeros_like(st_ref)

        tables = [_scan_tables(lr, li, reverse=False) for lr, li in _scan_chunks(lam_ref[...])]

        def project(k):
            x_ref[k * chunk:(k + 1) * chunk, :] = _nn(u_ref[k * chunk:(k + 1) * chunk, :].astype(BF16), b_ref[...])

        def tile(r0, last):
            last, parts = list(last), []
            for c, (xr, xi) in enumerate(_scan_chunks(x_ref[r0:r0 + SCAN_ROWS, :])):
                levels, (pr, pi) = tables[c]
                for d, (ar, ai) in levels:
                    sr, si = pltpu.roll(xr, d, 0), pltpu.roll(xi, d, 0)
                    xr, xi = xr + ar * sr - ai * si, xi + ar * si + ai * sr
                br, bi = last[2 * c], last[2 * c + 1]
                hr = xr + pr * br - pi * bi
                hi = xi + pr * bi + pi * br
                last[2 * c], last[2 * c + 1] = hr[SCAN_ROWS - 1:], hi[SCAN_ROWS - 1:]
                parts += [hr, hi]
            h_ref[r0:r0 + SCAN_ROWS, :] = jnp.concatenate(parts, axis=1)
            return tuple(last)

        st = st_ref[0:1, :]
        last = tuple(st[:, c * LANES:(c + 1) * LANES] for c in range(SCAN_LANES // LANES))
        project(0)
        for k in range(S5_CHUNKS):
            if k + 1 < S5_CHUNKS:
                project(k + 1)
            for r0 in range(k * chunk, (k + 1) * chunk, SCAN_ROWS):
                last = tile(r0, last)
            y_ref[k * chunk:(k + 1) * chunk, :] = _nn(h_ref[k * chunk:(k + 1) * chunk, :].astype(BF16), c_ref[...])
        st_ref[0:1, :] = jnp.concatenate(last, axis=1)

    return pl.pallas_call(
        body, name="s5_fwd", grid=(SSM_BLOCKS, nt),
        in_specs=[pl.BlockSpec((tt, cin), lambda kb, i: (i, u_off // cin + kb)),
                  pl.BlockSpec((cin, SCAN_LANES), lambda kb, i: (kb, 0)),
                  pl.BlockSpec((SCAN_LANES, cin), lambda kb, i: (kb, 0)),
                  pl.BlockSpec((1, SCAN_LANES), lambda kb, i: (0, kb))],
        out_specs=[pl.BlockSpec((tt, SCAN_LANES), lambda kb, i: (i, kb)), pl.BlockSpec((tt, cin), lambda kb, i: (i, kb))],
        out_shape=[jax.ShapeDtypeStruct((s, w), F32), jax.ShapeDtypeStruct((s, bmat.shape[0]), F32)],
        scratch_shapes=[pltpu.VMEM((tt, SCAN_LANES), F32), pltpu.VMEM((SCAN_ROWS, SCAN_LANES), F32)],
        compiler_params=_params(dimension_semantics=("arbitrary", "arbitrary")),
    )(proj, bmat, cmat, lam)


def _s5_bwd(dy, h, proj, u_off, bmat, cmat, lam):
    s, w = h.shape
    tt = _pick(s, (512, 256, 128, 8))
    nt = s // tt
    cin = bmat.shape[0] // SSM_BLOCKS
    chunk = tt // S5_CHUNKS

    def body(dy_ref, h_ref, u_ref, b_ref, c_ref, lam_ref, du_ref, dlam_ref, db_ref, dc_ref, e_ref, a_ref, st_ref):
        @pl.when(pl.program_id(1) == 0)
        def _():
            st_ref[...] = jnp.zeros_like(st_ref)
            dlam_ref[...] = jnp.zeros_like(dlam_ref)
            db_ref[...] = jnp.zeros_like(db_ref)
            dc_ref[...] = jnp.zeros_like(dc_ref)

        tables = [_scan_tables(lr, li, reverse=True) for lr, li in _scan_chunks(lam_ref[...])]
        nch = len(tables)
        row = lax.broadcasted_iota(jnp.int32, (SCAN_ROWS, LANES), 0)
        rows_first = (((0,), (0,)), ((), ()))

        def project(k):
            e_ref[k * chunk:(k + 1) * chunk, :] = _nt(dy_ref[k * chunk:(k + 1) * chunk, :], c_ref[...])

        def finish(k):
            rows = slice(k * chunk, (k + 1) * chunk)
            adj = a_ref[rows, :].astype(BF16)
            du_ref[rows, :] = _nt(adj, b_ref[...])
            db_ref[...] += lax.dot_general(u_ref[rows, :].astype(BF16), adj, rows_first, preferred_element_type=F32)
            dc_ref[...] += lax.dot_general(h_ref[rows, :].astype(BF16), dy_ref[rows, :], rows_first, preferred_element_type=F32)

        def tile(r0, carry):
            e_c = _scan_chunks(e_ref[r0:r0 + SCAN_ROWS, :])
            h_c = _scan_chunks(h_ref[r0:r0 + SCAN_ROWS, :])
            carry, parts = list(carry), []
            for c in range(nch):
                (yr, yi), (hr, hi) = e_c[c], h_c[c]
                levels, (pr, pi) = tables[c]
                for shift, (lr, li) in levels:
                    sr, si = pltpu.roll(yr, shift, 0), pltpu.roll(yi, shift, 0)
                    yr, yi = yr + lr * sr - li * si, yi + lr * si + li * sr
                nr, ni, dr, di = carry[4 * c:4 * c + 4]
                ar = yr + pr * nr - pi * ni
                ai = yi + pr * ni + pi * nr
                nxr = jnp.where(row == SCAN_ROWS - 1, nr, pltpu.roll(ar, SCAN_ROWS - 1, 0))
                nxi = jnp.where(row == SCAN_ROWS - 1, ni, pltpu.roll(ai, SCAN_ROWS - 1, 0))
                carry[4 * c:4 * c + 4] = [ar[0:1], ai[0:1], dr + nxr * hr + nxi * hi, di + nxi * hr - nxr * hi]
                parts += [ar, ai]
            a_ref[r0:r0 + SCAN_ROWS, :] = jnp.concatenate(parts, axis=1)
            return tuple(carry)

        st, dl = st_ref[0:1, :], dlam_ref[...]
        init = []
        for c in range(nch):
            lo = c * 2 * LANES
            init += [st[:, lo:lo + LANES], st[:, lo + LANES:lo + 2 * LANES],
                     dl[:, lo:lo + LANES], dl[:, lo + LANES:lo + 2 * LANES]]
        fin = tuple(init)
        project(S5_CHUNKS - 1)
        for k in reversed(range(S5_CHUNKS)):
            if k > 0:
                project(k - 1)
            for r0 in reversed(range(k * chunk, (k + 1) * chunk, SCAN_ROWS)):
                fin = tile(r0, fin)
            finish(k)
        st_ref[0:1, :] = jnp.concatenate([fin[4 * c + q] for c in range(nch) for q in (0, 1)], axis=1)
        dlam_ref[...] = jnp.concatenate([fin[4 * c + q] for c in range(nch) for q in (2, 3)], axis=1)

        @pl.when(pl.program_id(1) == nt - 1)
        def _():
            dlam_ref[0:1, :] = jnp.sum(dlam_ref[...], axis=0, keepdims=True)

    def rev(width, col):
        return pl.BlockSpec((tt, width), lambda kb, i: (nt - 1 - i, col(kb)))

    return pl.pallas_call(
        body, name="s5_bwd", grid=(SSM_BLOCKS, nt),
        in_specs=[rev(cin, lambda kb: kb), rev(SCAN_LANES, lambda kb: kb), rev(cin, lambda kb: u_off // cin + kb),
                  pl.BlockSpec((cin, SCAN_LANES), lambda kb, i: (kb, 0)),
                  pl.BlockSpec((SCAN_LANES, cin), lambda kb, i: (kb, 0)),
                  pl.BlockSpec((1, SCAN_LANES), lambda kb, i: (0, kb))],
        out_specs=[rev(cin, lambda kb: kb), pl.BlockSpec((SCAN_ROWS, SCAN_LANES), lambda kb, i: (0, kb)),
                   pl.BlockSpec((cin, SCAN_LANES), lambda kb, i: (kb, 0)), pl.BlockSpec((SCAN_LANES, cin), lambda kb, i: (kb, 0))],
        out_shape=[jax.ShapeDtypeStruct((s, bmat.shape[0]), F32), jax.ShapeDtypeStruct((SCAN_ROWS, w), F32),
                   jax.ShapeDtypeStruct((bmat.shape[0], SCAN_LANES), F32), jax.ShapeDtypeStruct((w, cin), F32)],
        scratch_shapes=[pltpu.VMEM((tt, SCAN_LANES), F32), pltpu.VMEM((tt, SCAN_LANES), F32),
                        pltpu.VMEM((SCAN_ROWS, SCAN_LANES), F32)],
        compiler_params=_params(dimension_semantics=("arbitrary", "arbitrary")),
    )(dy, h, proj, bmat, cmat, lam)


def _ssm_params_fwd(a_re, a_im, log_dt, b_re, b_im):
    def body(ar_ref, ai_ref, ldt_ref, br_ref, bi_ref, lr_ref, li_ref, bbr_ref, bbi_ref):
        ar, ai, dt = ar_ref[...], ai_ref[...], jnp.exp(ldt_ref[...])
        mag = jnp.exp(ar * dt)
        lr, li = mag * jnp.cos(ai * dt), mag * jnp.sin(ai * dt)
        den = ar * ar + ai * ai
        cr = ((lr - 1.0) * ar + li * ai) / den
        ci = (li * ar - (lr - 1.0) * ai) / den
        br, bi = br_ref[...], bi_ref[...]
        lr_ref[...], li_ref[...] = lr, li
        bbr_ref[...] = cr * br - ci * bi
        bbi_ref[...] = cr * bi + ci * br

    n = a_re.shape[0]
    v1, v16 = jax.ShapeDtypeStruct((n, 1), F32), jax.ShapeDtypeStruct((n, SSM_GROUP), F32)
    return pl.pallas_call(body, name="ssm_params_fwd", out_shape=[v1, v1, v16, v16],
                          compiler_params=_params())(a_re, a_im, log_dt, b_re, b_im)


def _ssm_params_bwd(a_re, a_im, log_dt, b_re, b_im, g_lr, g_li, g_bbr, g_bbi):
    n = a_re.shape[0]

    def body(ar_ref, ai_ref, ldt_ref, br_ref, bi_ref, glr_ref, gli_ref, gbr_ref, gbi_ref,
             dar_ref, dai_ref, dldt_ref, dbr_ref, dbi_ref):
        ar, ai, dt = ar_ref[...], ai_ref[...], jnp.exp(ldt_ref[...])
        mag = jnp.exp(ar * dt)
        lr, li = mag * jnp.cos(ai * dt), mag * jnp.sin(ai * dt)
        den = ar * ar + ai * ai
        cr = ((lr - 1.0) * ar + li * ai) / den
        ci = (li * ar - (lr - 1.0) * ai) / den
        br, bi, gbr, gbi = br_ref[...], bi_ref[...], gbr_ref[...], gbi_ref[...]
        dbr_ref[...] = gbr * cr + gbi * ci
        dbi_ref[...] = gbi * cr - gbr * ci
        gcr = jnp.sum(gbr * br + gbi * bi, axis=1, keepdims=True)
        gci = jnp.sum(gbi * br - gbr * bi, axis=1, keepdims=True)
        ir, ii = ar / den, -ai / den
        glr = glr_ref[...] + gcr * ir + gci * ii
        gli = gli_ref[...] + gci * ir - gcr * ii
        qr, qi = cr * ir - ci * ii, cr * ii + ci * ir
        gar = -(gcr * qr + gci * qi)
        gai = -(gci * qr - gcr * qi)
        gxr = glr * lr + gli * li
        gxi = gli * lr - glr * li
        dar_ref[...] = gar + gxr * dt
        dai_ref[...] = gai + gxi * dt
        gdt = (gxr * ar + gxi * ai) * dt
        rowg = lax.broadcasted_iota(jnp.int32, (n, SSM_GROUPS), 0) // SSM_STATE
        colg = lax.broadcasted_iota(jnp.int32, (n, SSM_GROUPS), 1)
        dldt_ref[...] = jnp.sum(jnp.where(rowg == colg, gdt, 0.0), axis=0, keepdims=True)

    v1, v16 = jax.ShapeDtypeStruct((n, 1), F32), jax.ShapeDtypeStruct((n, SSM_GROUP), F32)
    return pl.pallas_call(body, name="ssm_params_bwd",
                          out_shape=[v1, v1, jax.ShapeDtypeStruct((1, SSM_GROUPS), F32), v16, v16],
                          compiler_params=_params())(a_re, a_im, log_dt, b_re, b_im, g_lr, g_li, g_bbr, g_bbi)


def _interleave(re, im, axis):
    shp = list(re.shape)
    new = shp[:axis] + [shp[axis] // LANES, LANES] + shp[axis + 1:]
    st = jnp.stack([re.reshape(new), im.reshape(new)], axis=axis + 1)
    return st.reshape(shp[:axis] + [2 * shp[axis]] + shp[axis + 1:])


def _deinterleave(v, axis):
    shp = list(v.shape)
    r = v.reshape(shp[:axis] + [shp[axis] // (2 * LANES), 2, LANES] + shp[axis + 1:])
    out = shp[:axis] + [shp[axis] // 2] + shp[axis + 1:]
    return (lax.index_in_dim(r, 0, axis + 1, keepdims=False).reshape(out),
            lax.index_in_dim(r, 1, axis + 1, keepdims=False).reshape(out))


def _b_matrix(bbr, bbi):
    per = SSM_GROUPS // SSM_BLOCKS
    eye = jnp.eye(per, dtype=F32)

    def blockdiag(v):
        x = v.reshape(SSM_BLOCKS, per, SSM_STATE, SSM_GROUP).transpose(0, 1, 3, 2)
        return (eye[None, :, None, :, None] * x[:, :, :, None, :]).reshape(SSM_GROUPS * SSM_GROUP, per * SSM_STATE)

    return _interleave(blockdiag(bbr), blockdiag(bbi), 1)


def _diag_blocks(v, rows, cols):
    per = SSM_GROUPS // SSM_BLOCKS
    x = v.reshape(SSM_BLOCKS, per, rows, per, cols) * jnp.eye(per, dtype=v.dtype)[None, :, None, :, None]
    return jnp.sum(x, axis=3).reshape(SSM_GROUPS, rows, cols)


def _b_matrix_grad(d):
    def diag(v):
        return _diag_blocks(v, SSM_GROUP, SSM_STATE).transpose(0, 2, 1).reshape(N_STATE, SSM_GROUP)

    dr, di = _deinterleave(d, 1)
    return diag(dr), diag(di)


def _c_matrix(c_re, c_im):
    per = SSM_GROUPS // SSM_BLOCKS
    eye = jnp.eye(per, dtype=F32)

    def blockdiag(v):
        x = v.reshape(SSM_BLOCKS, per, SSM_GROUP, SSM_STATE).transpose(0, 1, 3, 2)
        return (x[:, :, :, None, :] * eye[None, :, None, :, None]).reshape(N_STATE, per * SSM_GROUP)

    return _interleave(blockdiag(c_re), blockdiag(-c_im), 0)


def _c_matrix_grad(d):
    def diag(v):
        return _diag_blocks(v, SSM_STATE, SSM_GROUP).transpose(0, 2, 1)

    dr, di = _deinterleave(d, 0)
    return diag(dr), -diag(di)


def _row(v):
    return v.reshape(1, -1)


def _ssm_inputs(p):
    rows = lambda v: v.reshape(N_STATE, -1)
    ldt = jnp.repeat(p["ssm_log_dt"], SSM_STATE).reshape(N_STATE, 1)
    return rows(p["ssm_a_re"]), rows(p["ssm_a_im"]), ldt, rows(p["ssm_b_re"]), rows(p["ssm_b_im"])


def _lnmod(x, sc, sh):
    return _norm(x)[0] * (1.0 + sc) + sh


def _resid_ln(x, y, g, lg, lb):
    return _norm(ALPHA * x + (1.0 + g) * y)[0] * lg + lb


def _resid_ln_lnmod(x, y, g, lg, lb, sc, sh):
    xo = _resid_ln(x, y, g, lg, lb)
    return xo, _lnmod(xo, sc, sh)


def _layer_fwd(x, h1, mod, p, tag, next_mod):
    d = x.shape[1]
    sh_m, sc_m, g_m, sh_f, sc_f, g_f = [_row(mod[i]) for i in range(6)]
    nm = lambda s: f"{s}_{tag}"
    proj = _mm(nm("proj"), h1, p["w_in"], "nn", out_dtype=BF16)
    t = min(ATT_TILE, x.shape[0])
    kt3, vt3 = _kv_transposed(proj, t)
    att, car = _attn_fwd(proj, vt3, t)
    y_sb = _mm(nm("sb_up"), att, p["w_sb_up"], "nn", out_dtype=BF16)

    ssm_in = _ssm_inputs(p)
    lam_r, lam_i, bbr, bbi = _ssm_params_fwd(*ssm_in)
    lam = _interleave(lam_r.reshape(1, N_STATE), lam_i.reshape(1, N_STATE), 1)
    bmat = _b_matrix(bbr, bbi).astype(BF16)
    cmat = _c_matrix(p["ssm_c_re"], p["ssm_c_im"]).astype(BF16)
    hst, yc = _s5_fwd(proj, U_OFFSET, bmat, cmat, lam)

    def ssm_act(yc, u, dsk):
        y0 = yc + dsk * u
        return y0, _gelu(y0)

    y0, y1 = _rowwise(nm("ssm_act"), ssm_act, [(yc, 0, 512), (proj, 3, 512)], [_row(p["ssm_d"])], [(512, F32), (512, F32)])
    gl = _mm(nm("glu"), y1, p["w_glu"], "nn")
    y2 = _rowwise(nm("glu_act"), lambda y1, gl, b: y1 * jax.nn.sigmoid(gl + b), [(y1, 0, 512), (gl, 0, 512)],
                  [_row(p["b_glu"])], [(512, BF16)])
    y_ssm = _mm(nm("ssm_up"), y2, p["w_ssm_up"], "nn", out_dtype=BF16)

    def merge(gsb, gss, ysb, yss):
        return jax.nn.sigmoid(gsb) * ysb + jax.nn.sigmoid(gss) * yss

    merged = _rowwise(nm("merge"), merge, [(proj, 2, d), (proj, 3, d), (y_sb, 0, d), (y_ssm, 0, d)], [], [(d, BF16)])
    y = _mm(nm("out"), merged, p["w_out"], "nn")

    x1, h2 = _rowwise(nm("ln1"), _resid_ln_lnmod, [(x, 0, d), (y, 0, d)],
                      [g_m, _row(p["ln1_g"]), _row(p["ln1_b"]), sc_f, sh_f], [(d, F32), (d, BF16)])
    f_gate, f_up, act = _ffn_in(nm("ffn_in"), h2, p["w_ffn_in"])
    yf = _mm(nm("ffn_out"), act, p["w_ffn_out"], "nn")
    x2 = h1_next = None
    if next_mod is not None:
        x2, h1_next = _rowwise(nm("ln2"), _resid_ln_lnmod, [(x1, 0, d), (yf, 0, d)],
                               [g_f, _row(p["ln2_g"]), _row(p["ln2_b"]), next_mod[1], next_mod[0]], [(d, F32), (d, BF16)])
    saved = dict(x=x, h1=h1, proj=proj, ssm_in=ssm_in, kt3=kt3, car=car, att=att, y_sb=y_sb, lam=lam, bmat=bmat,
                 cmat=cmat, hst=hst, y0=y0, y1=y1, gl=gl, y2=y2, y_ssm=y_ssm, merged=merged, y=y, x1=x1, h2=h2, f_gate=f_gate, f_up=f_up,
                 act=act, yf=yf, t=t)
    return x2, h1_next, saved


def _resid_ln_bwd(x, y, dxo, g, lg):
    n, rstd = _norm(ALPHA * x + (1.0 + g) * y)
    dr = _norm_bwd(dxo * lg, n, rstd)
    return ALPHA * dr, (1.0 + g) * dr, _colsum(dxo * n), _colsum(dxo), _colsum(dr * y)


def _lnmod_bwd(x, dh, dxa, sc):
    n, rstd = _norm(x)
    return dxa + _norm_bwd(dh * (1.0 + sc), n, rstd), _colsum(dh * n), _colsum(dh)


def _lnmod_resid_ln_bwd(xo, dh, dxa, x, y, sc, g, lg):
    dxo, dsc, dsh = _lnmod_bwd(xo, dh, dxa, sc)
    dx, dy, dlg, dlb, dg = _resid_ln_bwd(x, y, dxo, g, lg)
    return dx, dy, dsc, dsh, dlg, dlb, dg


def _layer_bwd(dx1a, dyf, mod, p, sv, layer, depth, stacked):
    d = dx1a.shape[1]
    sh_m, sc_m, g_m, sh_f, sc_f, g_f = [_row(mod[i]) for i in range(6)]
    nm = lambda s: f"{s}_{layer}"
    grads = {}

    def weight_grad(n, a, b, **kw):
        grads[n] = _mm(nm("d" + n), a, b, "tn", out_dtype=BF16, into=(stacked.get(n), layer, depth), **kw)

    dact = _mm(nm("d_act"), dyf, p["w_ffn_out"], "nt", out_dtype=BF16)
    weight_grad("w_ffn_out", sv["act"], dyf)
    fh = sv["f_gate"].shape[1]

    def swiglu_bwd(g, u, da):
        sg = jax.nn.sigmoid(g)
        return jnp.concatenate([da * u * sg * (1.0 + g * (1.0 - sg)), da * g * sg], axis=1)

    df = _rowwise(nm("swiglu_bwd"), swiglu_bwd, [(sv["f_gate"], 0, fh), (sv["f_up"], 0, fh), (dact, 0, fh)], [], [(2 * fh, BF16)])
    dh2 = _mm(nm("d_h2"), df, p["w_ffn_in"], "nt")
    weight_grad("w_ffn_in", sv["h2"], df)
    dxa, dy, dsc_f, dsh_f, grads["ln1_g"], grads["ln1_b"], dg_m = _rowwise(
        nm("ln1_bwd"), _lnmod_resid_ln_bwd, [(sv["x1"], 0, d), (dh2, 0, d), (dx1a, 0, d), (sv["x"], 0, d), (sv["y"], 0, d)],
        [sc_f, g_m, _row(p["ln1_g"])], [(d, F32), (d, BF16)], [d] * 5)
    dmerged = _mm(nm("d_merged"), dy, p["w_out"], "nt", out_dtype=BF16)
    weight_grad("w_out", sv["merged"], dy)

    def merge_bwd(gsb, gss, ysb, yss, dm):
        s1, s2 = jax.nn.sigmoid(gsb), jax.nn.sigmoid(gss)
        return s1 * dm, s2 * dm, dm * ysb * s1 * (1.0 - s1), dm * yss * s2 * (1.0 - s2)

    dy_sb, dy_ssm, dg_sb, dg_ssm = _rowwise(
        nm("merge_bwd"), merge_bwd, [(sv["proj"], 2, d), (sv["proj"], 3, d), (sv["y_sb"], 0, d), (sv["y_ssm"], 0, d),
                                     (dmerged, 0, d)], [], [(d, BF16)] * 4)
    dy2 = _mm(nm("d_y2"), dy_ssm, p["w_ssm_up"], "nt")
    weight_grad("w_ssm_up", sv["y2"], dy_ssm)

    def glu_act_bwd(y1, gl, dy2, b):
        sg = jax.nn.sigmoid(gl + b)
        dgl = dy2 * y1 * sg * (1.0 - sg)
        return dy2 * sg, dgl, _colsum(dgl)

    dy1a, dgl, grads["b_glu"] = _rowwise(nm("glu_act_bwd"), glu_act_bwd, [(sv["y1"], 0, 512), (sv["gl"], 0, 512), (dy2, 0, 512)],
                                         [_row(p["b_glu"])], [(512, F32), (512, BF16)], [512])
    dy1b = _mm(nm("d_y1"), dgl, p["w_glu"], "nt")
    weight_grad("w_glu", sv["y1"], dgl)

    def ssm_act_bwd(y0, u, dy1a, dy1b, dsk):
        dy0 = (dy1a + dy1b) * _gelu_grad(y0)
        return dy0, dsk * dy0, _colsum(dy0 * u)

    dy0, du_a, grads["ssm_d"] = _rowwise(nm("ssm_act_bwd"), ssm_act_bwd,
                                         [(sv["y0"], 0, 512), (sv["proj"], 3, 512), (dy1a, 0, 512), (dy1b, 0, 512)],
                                         [_row(p["ssm_d"])], [(512, BF16), (512, F32)], [512])
    du_b, dlam, d_bmat, d_cmat = _s5_bwd(dy0, sv["hst"], sv["proj"], U_OFFSET, sv["bmat"], sv["cmat"], sv["lam"])
    grads["ssm_c_re"], grads["ssm_c_im"] = _c_matrix_grad(d_cmat)
    g_bbr, g_bbi = _b_matrix_grad(d_bmat)
    g_lr, g_li = _deinterleave(dlam[0:1], 1)
    da_re, da_im, dldt, db_re, db_im = _ssm_params_bwd(*sv["ssm_in"], g_lr.reshape(N_STATE, 1), g_li.reshape(N_STATE, 1),
                                                       g_bbr, g_bbi)
    grads["ssm_a_re"] = da_re.reshape(SSM_GROUPS, SSM_STATE)
    grads["ssm_a_im"] = da_im.reshape(SSM_GROUPS, SSM_STATE)
    grads["ssm_log_dt"] = dldt.reshape(SSM_GROUPS)
    grads["ssm_b_re"] = db_re.reshape(SSM_GROUPS, SSM_STATE, SSM_GROUP)
    grads["ssm_b_im"] = db_im.reshape(SSM_GROUPS, SSM_STATE, SSM_GROUP)
    datt = _mm(nm("d_att"), dy_sb, p["w_sb_up"], "nt", out_dtype=BF16)
    weight_grad("w_sb_up", sv["att"], dy_sb)
    dqs, dk, dv = _attn_bwd(sv["proj"], datt, sv["kt3"], sv["car"], sv["t"])

    def dproj_cols(dqs, dk, dv, dua, dub, dgsb, dgss):
        return jnp.concatenate([dqs * (1.0 / math.sqrt(HEAD_DIM)), dk, dv, dua + dub, dgsb.astype(F32), dgss.astype(F32)],
                               axis=1)

    dproj = _rowwise(nm("dproj"), dproj_cols, [(dqs, 0, 512), (dk, 0, 512), (dv, 0, 512), (du_a, 0, 512), (du_b, 0, 512),
                                               (dg_sb, 0, d), (dg_ssm, 0, d)], [], [(2048 + 2 * d, BF16)])
    dh1 = _mm(nm("d_h1"), dproj, p["w_in"], "nt")
    weight_grad("w_in", sv["h1"], dproj)
    for k in ("ln1_g", "ln1_b", "ssm_d", "b_glu"):
        grads[k] = grads[k].reshape(-1)
    return dh1, dxa, grads, (dg_m, dsh_f, dsc_f)


def _local_step(x, target, mod, layer_w):
    depth, d = len(layer_w), x.shape[1]
    rows = lambda l: [_row(mod[l][i]) for i in range(6)]
    h1 = _rowwise("lnmod1_0", _lnmod, [(x, 0, d)], [rows(0)[1], rows(0)[0]], [(d, BF16)])
    xs, saved = x, []
    for l in range(depth):
        xs, h1, sv = _layer_fwd(xs, h1, mod[l], layer_w[l], str(l), rows(l + 1)[:2] if l + 1 < depth else None)
        saved.append(sv)

    def head_bwd(x1, yf, tgt, g, lg, lb):
        err = _resid_ln(x1, yf, g, lg, lb) - tgt
        return _resid_ln_bwd(x1, yf, err * (1.0 / d), g, lg) + (_colsum(err * err) * (0.5 / d),)

    def boundary_bwd(dh, dxa, x1, yf, sc, g, lg, lb):
        dxo, dsc, dsh = _lnmod_bwd(_resid_ln(x1, yf, g, lg, lb), dh, dxa, sc)
        return _resid_ln_bwd(x1, yf, dxo, g, lg) + (dsc, dsh)

    lgrads, sums, stacked = [None] * depth, [dict() for _ in range(depth)], {}
    last, p = saved[-1], layer_w[-1]
    dx1a, dyf, dlg, dlb, dg_f, loss_cols = _rowwise(
        "head_bwd", head_bwd, [(last["x1"], 0, d), (last["yf"], 0, d), (target, 0, d)],
        [rows(depth - 1)[5], _row(p["ln2_g"]), _row(p["ln2_b"])], [(d, F32), (d, BF16)], [d] * 4)
    for l in reversed(range(depth)):
        sums[l]["g_f"] = dg_f
        dh1, dxa, lgrads[l], (sums[l]["g_m"], sums[l]["sh_f"], sums[l]["sc_f"]) = _layer_bwd(
            dx1a, dyf, mod[l], layer_w[l], saved[l], l, depth, stacked)
        lgrads[l]["ln2_g"], lgrads[l]["ln2_b"] = dlg.reshape(-1), dlb.reshape(-1)
        stacked = {n: lgrads[l][n] for n in COL_SPLIT + ROW_SPLIT}
        if l > 0:
            prev, p = saved[l - 1], layer_w[l - 1]
            dx1a, dyf, dlg, dlb, dg_f, sums[l]["sc_m"], sums[l]["sh_m"] = _rowwise(
                f"boundary_bwd_{l}", boundary_bwd, [(dh1, 0, d), (dxa, 0, d), (prev["x1"], 0, d), (prev["yf"], 0, d)],
                [rows(l)[1], rows(l - 1)[5], _row(p["ln2_g"]), _row(p["ln2_b"])], [(d, F32), (d, BF16)], [d] * 5)
        else:
            dx, sums[l]["sc_m"], sums[l]["sh_m"] = _rowwise("lnmod1_bwd", _lnmod_bwd, [(x, 0, d), (dh1, 0, d), (dxa, 0, d)],
                                                            [rows(0)[1]], [(d, F32)], [d, d])
    dmod = jnp.stack([jnp.concatenate([sums[l][k] for k in ("sh_m", "sc_m", "g_m", "sh_f", "sc_f", "g_f")], axis=0)
                      for l in range(depth)])
    return loss_cols, dx, dmod, lgrads, stacked


def _place():
    return lax.axis_index("x"), lax.axis_index("y"), lax.axis_index("c")


def _all_gather8(name, block):
    m_per, n = block.shape

    def body(x_ref, out_ref, send_sems, recv_sems, local_sem):
        x, y, c = _place()
        me, sibling = (x, y, c), (x, y, 1 - c)
        chips = [(1 - x, y), (x, 1 - y), (1 - x, 1 - y)]

        def rows(px, py, pc):
            return out_ref.at[pl.ds(pl.multiple_of((4 * px + 2 * py + pc) * m_per, 8), m_per), :]

        def copy(k, blk, to, src=None):
            return pltpu.make_async_remote_copy(src_ref=rows(*blk) if src is None else src, dst_ref=rows(*blk),
                                                send_sem=send_sems.at[k], recv_sem=recv_sems.at[k],
                                                device_id=to, device_id_type=MESH)

        mine = pltpu.make_async_copy(x_ref, rows(*me), local_sem)
        mine.start()
        first = [copy(0, me, sibling, src=x_ref)] + [copy(1 + j, me, (*chip, c), src=x_ref) for j, chip in enumerate(chips)]
        for cp in first:
            cp.start()
        passed = [copy(4 + j, (*chip, c), sibling) for j, chip in enumerate(chips)]
        for j, chip in enumerate(chips):
            copy(1 + j, (*chip, c), me).wait_recv()
            passed[j].start()
        copy(0, sibling, me).wait_recv()
        for j, chip in enumerate(chips):
            copy(4 + j, (*chip, 1 - c), me).wait_recv()
        for cp in first + passed:
            cp.wait_send()
        mine.wait()

    return pl.pallas_call(
        body, name=name, out_shape=jax.ShapeDtypeStruct((8 * m_per, n), block.dtype),
        in_specs=[pl.BlockSpec(memory_space=pltpu.VMEM)], out_specs=pl.BlockSpec(memory_space=pltpu.VMEM),
        scratch_shapes=[pltpu.SemaphoreType.DMA((7,)), pltpu.SemaphoreType.DMA((7,)), pltpu.SemaphoreType.DMA],
        compiler_params=_params(),
    )(block)


def _other_chips(x, y):
    return [(1 - x, y), (x, 1 - y), (1 - x, 1 - y)]


def _gather_weights(whole, by_rows):
    n = len(whole)

    def body(*refs):
        dst = refs[n:2 * n]
        ici_send, ici_recv, d2d_send, d2d_recv = refs[2 * n:]
        x, y, c = _place()
        chips = _other_chips(x, y)

        def part(ref, k, px, py, pc):
            _, r, cols = whole[k].shape
            q = 2 * px + py
            if by_rows[k]:
                return ref[k].at[:, pl.ds(pl.multiple_of((2 * q + pc) * (r // 8), 16), r // 8), :]
            return ref[k].at[:, pl.ds(pl.multiple_of(pc * (r // 2), 16), r // 2),
                             pl.ds(pl.multiple_of(q * (cols // 4), LANES), cols // 4)]

        def ici(k, j, px, py, to):
            return pltpu.make_async_remote_copy(src_ref=part(dst, k, px, py, c), dst_ref=part(dst, k, px, py, c),
                                                send_sem=ici_send.at[k, j], recv_sem=ici_recv.at[k, j],
                                                device_id=(*to, c), device_id_type=MESH)

        def d2d(k, j, px, py, pc):
            return pltpu.make_async_remote_copy(src_ref=part(dst, k, px, py, pc), dst_ref=part(dst, k, px, py, pc),
                                                send_sem=d2d_send.at[k, j], recv_sem=d2d_recv.at[k, j],
                                                device_id=(x, y, 1 - c), device_id_type=MESH)

        for k in range(n):
            for j, chip in enumerate(chips):
                ici(k, j, x, y, chip).start()
        for k in range(n):
            for j, chip in enumerate(chips):
                ici(k, j, *chip, chip).wait_recv()
                d2d(k, j, *chip, c).start()
        for k in range(n):
            for j, chip in enumerate(chips):
                d2d(k, j, *chip, 1 - c).wait_recv()
        for k in range(n):
            for j, chip in enumerate(chips):
                ici(k, j, x, y, chip).wait_send()
                d2d(k, j, *chip, c).wait_send()

    any_spec = pl.BlockSpec(memory_space=pl.ANY)
    return pl.pallas_call(
        body, name="gather_weights", in_specs=[any_spec] * n, out_specs=[any_spec] * n,
        out_shape=[jax.ShapeDtypeStruct(a.shape, a.dtype) for a in whole], input_output_aliases={k: k for k in range(n)},
        scratch_shapes=[pltpu.SemaphoreType.DMA((n, 3))] * 4,
        compiler_params=_params(),
    )(*whole)


def _part_shape(shape, by_rows):
    l, r, c = shape
    return (l, r // 8, c) if by_rows else (l, r // 2, c // 4)


def _pair_exchange(grads, by_rows):
    n = len(grads)

    def body(*refs):
        src, dst = refs[:n], refs[n:2 * n]
        send_sems, recv_sems = refs[2 * n:]
        x, y, c = _place()

        def window(k, q, pc):
            _, hr, hc = _part_shape(grads[k].shape, by_rows[k])
            if by_rows[k]:
                return src[k].at[:, pl.ds(pl.multiple_of((2 * q + pc) * hr, 16), hr), :]
            return src[k].at[:, pl.ds(pl.multiple_of(pc * hr, 16), hr), pl.ds(q * hc, hc)]

        def copy(k, q, pc):
            return pltpu.make_async_remote_copy(src_ref=window(k, q, pc), dst_ref=dst[k].at[q], send_sem=send_sems.at[k, q],
                                                recv_sem=recv_sems.at[k, q], device_id=(x, y, 1 - c), device_id_type=MESH)

        for k in range(n):
            for q in range(4):
                copy(k, q, 1 - c).start()
        for k in range(n):
            for q in range(4):
                copy(k, q, c).wait_recv()
        for k in range(n):
            for q in range(4):
                copy(k, q, 1 - c).wait_send()

    any_spec = pl.BlockSpec(memory_space=pl.ANY)
    return pl.pallas_call(
        body, name="pair_exchange", in_specs=[any_spec] * n, out_specs=[any_spec] * n,
        out_shape=[jax.ShapeDtypeStruct((4, *_part_shape(g.shape, rows)), g.dtype) for g, rows in zip(grads, by_rows)],
        scratch_shapes=[pltpu.SemaphoreType.DMA((n, 4)), pltpu.SemaphoreType.DMA((n, 4))],
        compiler_params=_params(),
    )(*grads)


def _pair_sum(name, g, theirs, by_rows, c, chip):
    _, l, hr, hc = theirs.shape
    tr = _pick(hr, (256, 176, 128, 64, 32))

    def body(s_ref, g_ref, t_ref, p_ref, own_ref):
        v = (g_ref[...].astype(F32) + t_ref[0].astype(F32)).astype(BF16)
        p_ref[0] = v

        @pl.when(pl.program_id(2) == s_ref[1])
        def _():
            own_ref[0] = v

    if by_rows:
        g_spec = pl.BlockSpec((1, tr, hc), lambda li, i, q, s: (li, (2 * q + s[0]) * (hr // tr) + i, 0))
    else:
        g_spec = pl.BlockSpec((1, tr, hc), lambda li, i, q, s: (li, s[0] * (hr // tr) + i, q))
    slot = pl.BlockSpec((1, 1, tr, hc), lambda li, i, q, s: (q, li, i, 0))
    grid_spec = pltpu.PrefetchScalarGridSpec(
        num_scalar_prefetch=1, grid=(l, hr // tr, 4), in_specs=[g_spec, slot],
        out_specs=[slot, pl.BlockSpec((1, 1, tr, hc), lambda li, i, q, s: (s[1], li, i, 0))])
    return pl.pallas_call(
        body, name=name, grid_spec=grid_spec, out_shape=[jax.ShapeDtypeStruct(theirs.shape, BF16)] * 2,
        compiler_params=_params(dimension_semantics=("arbitrary", "arbitrary", "arbitrary")),
    )(jnp.stack([c, chip]).astype(jnp.int32), g, theirs)


def _chip_scatter(sums, landing):
    n = len(sums)

    def body(*refs):
        src, dst = refs[:n], refs[2 * n:3 * n]
        send_sems, recv_sems = refs[3 * n:]
        x, y, c = _place()
        mine = 2 * x + y

        def copy(k, j, src_slot, dst_slot, to):
            return pltpu.make_async_remote_copy(src_ref=src[k].at[src_slot], dst_ref=dst[k].at[dst_slot],
                                                send_sem=send_sems.at[k, j], recv_sem=recv_sems.at[k, j],
                                                device_id=(*to, c), device_id_type=MESH)

        chips = _other_chips(x, y)
        for k in range(n):
            for j, (px, py) in enumerate(chips):
                copy(k, j, 2 * px + py, mine, (px, py)).start()
        for k in range(n):
            for j, (px, py) in enumerate(chips):
                copy(k, j, mine, 2 * px + py, (px, py)).wait_recv()
        for k in range(n):
            for j, (px, py) in enumerate(chips):
                copy(k, j, 2 * px + py, mine, (px, py)).wait_send()

    any_spec = pl.BlockSpec(memory_space=pl.ANY)
    return pl.pallas_call(
        body, name="chip_scatter", in_specs=[any_spec] * (2 * n), out_specs=[any_spec] * n,
        out_shape=[jax.ShapeDtypeStruct(a.shape, a.dtype) for a in landing],
        input_output_aliases={n + k: k for k in range(n)},
        scratch_shapes=[pltpu.SemaphoreType.DMA((n, 3)), pltpu.SemaphoreType.DMA((n, 3))],
        compiler_params=_params(),
    )(*sums, *landing)


def _sum_slots(name, parts, half=None):
    slots, l, r, c = parts.shape
    tr = _pick(r, (256, 176, 128, 64, 32, 8))

    def body(*refs):
        p_ref, o_ref = refs[-2:]
        acc = p_ref[0].astype(F32)
        for i in range(1, slots):
            acc = acc + p_ref[i].astype(F32)
        o_ref[...] = acc

    if half is None:
        return pl.pallas_call(
            body, name=name, grid=(l, r // tr), in_specs=[pl.BlockSpec((slots, 1, tr, c), lambda li, i: (0, li, i, 0))],
            out_specs=pl.BlockSpec((1, tr, c), lambda li, i: (li, i, 0)), out_shape=jax.ShapeDtypeStruct((l, r, c), F32),
            compiler_params=_params(dimension_semantics=("arbitrary", "arbitrary")),
        )(parts)
    grid_spec = pltpu.PrefetchScalarGridSpec(
        num_scalar_prefetch=1, grid=(l, r // tr),
        in_specs=[pl.BlockSpec((slots, 1, tr, c), lambda li, i, h: (0, li, i, 0))],
        out_specs=pl.BlockSpec((1, tr, c), lambda li, i, h: (li, h[0] * (r // tr) + i, 0)))
    return pl.pallas_call(
        body, name=name, grid_spec=grid_spec, out_shape=jax.ShapeDtypeStruct((l, 2 * r, c), F32),
        compiler_params=_params(dimension_semantics=("arbitrary", "arbitrary")),
    )(jnp.reshape(half, (1,)).astype(jnp.int32), parts)


def _swap_halves(blocks):
    n = len(blocks)

    def body(*refs):
        src, dst = refs[:n], refs[n:2 * n]
        send_sems, recv_sems = refs[2 * n:]
        x, y, c = _place()

        def half(ref, k, pc):
            r = blocks[k].shape[1] // 2
            return ref[k].at[:, pl.ds(pl.multiple_of(pc * r, 8), r), :]

        def copy(k, pc):
            return pltpu.make_async_remote_copy(src_ref=half(src, k, pc), dst_ref=half(dst, k, pc), send_sem=send_sems.at[k],
                                                recv_sem=recv_sems.at[k], device_id=(x, y, 1 - c), device_id_type=MESH)

        for k in range(n):
            copy(k, c).start()
        for k in range(n):
            copy(k, 1 - c).wait_recv()
        for k in range(n):
            copy(k, c).wait_send()

    any_spec = pl.BlockSpec(memory_space=pl.ANY)
    return pl.pallas_call(
        body, name="swap_halves", in_specs=[any_spec] * n, out_specs=[any_spec] * n,
        out_shape=[jax.ShapeDtypeStruct(b.shape, b.dtype) for b in blocks], input_output_aliases={k: k for k in range(n)},
        scratch_shapes=[pltpu.SemaphoreType.DMA((n,)), pltpu.SemaphoreType.DMA((n,))],
        compiler_params=_params(),
    )(*blocks)


def _adamw(name, w, g, m, v):
    shape = w.shape
    cols = shape[-1] if w.ndim > 1 and shape[-1] % LANES == 0 else w.size if w.size % LANES else LANES
    flat = lambda a: a.reshape(-1, cols)
    rows = w.size // cols
    tr = _pick(rows, [r for r in (512, 256, 128, 64, 32, 16, 8) if r * cols <= 256 * 1024]) if rows % 8 == 0 else rows

    def body(w_ref, g_ref, m_ref, v_ref, go_ref, d_ref, nm_ref, nv_ref):
        gg = g_ref[...]
        go_ref[...] = gg
        nm = ADAM_B1 * m_ref[...] + (1.0 - ADAM_B1) * gg
        nv = ADAM_B2 * v_ref[...] + (1.0 - ADAM_B2) * (gg * gg)
        m_hat = nm / (1.0 - ADAM_B1 ** ADAM_STEP)
        v_hat = nv / (1.0 - ADAM_B2 ** ADAM_STEP)
        d_ref[...] = -ADAM_LR * (m_hat / (jnp.sqrt(v_hat) + ADAM_EPS) + ADAM_WD * w_ref[...])
        nm_ref[...] = nm
        nv_ref[...] = nv

    spec = pl.BlockSpec((tr, cols), lambda i: (i, 0))
    out = pl.pallas_call(
        body, name=name, grid=(rows // tr,), in_specs=[spec] * 4, out_specs=[spec] * 4,
        out_shape=[jax.ShapeDtypeStruct((rows, cols), F32)] * 4,
        compiler_params=_params(dimension_semantics=("arbitrary",)),
    )(flat(w), flat(g), flat(m), flat(v))
    return tuple(o.reshape(shape) for o in out)


WEIGHTS = ["w_ada", "b_ada", "w_in", "w_sb_up", "ssm_a_re", "ssm_a_im", "ssm_log_dt", "ssm_b_re", "ssm_b_im", "ssm_c_re",
           "ssm_c_im", "ssm_d", "w_glu", "b_glu", "w_ssm_up", "w_out", "ln1_g", "ln1_b", "w_ffn_in", "w_ffn_out", "ln2_g",
           "ln2_b"]
COL_SPLIT = ["w_in", "w_sb_up", "w_ssm_up", "w_ffn_in"]
ROW_SPLIT = ["w_glu", "w_out", "w_ffn_out"]
SMALL = ["ssm_a_re", "ssm_a_im", "ssm_log_dt", "ssm_b_re", "ssm_b_im", "ssm_c_re", "ssm_c_im", "ssm_d", "b_glu", "ln1_g",
         "ln1_b", "ln2_g", "ln2_b"]
SLAB_COLS = 1024


def _cast_into_whole(name, w, by_rows, chip):
    l, r, cols = w.shape
    tr = _pick(r, (512, 256, 128, 64, 16))

    def body(q_ref, w_ref, o_ref):
        o_ref[...] = w_ref[...].astype(BF16)

    if by_rows:
        out_map, shape = (lambda li, i, q: (li, q[0] * (r // tr) + i, 0)), (l, 4 * r, cols)
    else:
        out_map, shape = (lambda li, i, q: (li, i, q[0])), (l, r, 4 * cols)
    grid_spec = pltpu.PrefetchScalarGridSpec(
        num_scalar_prefetch=1, grid=(l, r // tr), in_specs=[pl.BlockSpec((1, tr, cols), lambda li, i, q: (li, i, 0))],
        out_specs=pl.BlockSpec((1, tr, cols), out_map))
    return pl.pallas_call(body, name=name, grid_spec=grid_spec, out_shape=jax.ShapeDtypeStruct(shape, BF16),
                          compiler_params=_params(dimension_semantics=("arbitrary", "arbitrary")),
                          )(jnp.reshape(chip, (1,)).astype(jnp.int32), w)


def _silu_rows(name, c):
    def body(c_ref, o_ref):
        v = c_ref[...]
        o_ref[...] = v * jax.nn.sigmoid(v)

    return pl.pallas_call(body, name=name, out_shape=jax.ShapeDtypeStruct(c.shape, F32), compiler_params=_params())(c)


def _pad_rows(v, mult=8):
    flat = v.reshape(-1)
    per = mult * SLAB_COLS
    total = -(-flat.size // per) * per
    return jnp.pad(flat, (0, total - flat.size)).reshape(-1, SLAB_COLS)


def kernel(x, c, w_ada, b_ada, w_in, w_sb_up, ssm_a_re, ssm_a_im, ssm_log_dt, ssm_b_re, ssm_b_im, ssm_c_re, ssm_c_im, ssm_d, w_glu, b_glu, w_ssm_up, w_out, ln1_g, ln1_b, w_ffn_in, w_ffn_out, ln2_g, ln2_b, loss_target, m_w_ada, m_b_ada, m_w_in, m_w_sb_up, m_ssm_a_re, m_ssm_a_im, m_ssm_log_dt, m_ssm_b_re, m_ssm_b_im, m_ssm_c_re, m_ssm_c_im, m_ssm_d, m_w_glu, m_b_glu, m_w_ssm_up, m_w_out, m_ln1_g, m_ln1_b, m_w_ffn_in, m_w_ffn_out, m_ln2_g, m_ln2_b, v_w_ada, v_b_ada, v_w_in, v_w_sb_up, v_ssm_a_re, v_ssm_a_im, v_ssm_log_dt, v_ssm_b_re, v_ssm_b_im, v_ssm_c_re, v_ssm_c_im, v_ssm_d, v_w_glu, v_b_glu, v_w_ssm_up, v_w_out, v_ln1_g, v_ln1_b, v_w_ffn_in, v_w_ffn_out, v_ln2_g, v_ln2_b):
    args = dict(locals())
    w = {n: args[n] for n in WEIGHTS}
    mom = {n: args["m_" + n] for n in WEIGHTS}
    var = {n: args["v_" + n] for n in WEIGHTS}
    depth, d = w_ada.shape[0], x.shape[-1]
    xi, yi, ci = _place()
    me, chip = 4 * xi + 2 * yi + ci, 2 * xi + yi
    ada_cols = w_ada.shape[2]

    big = COL_SPLIT + ROW_SPLIT
    by_rows = [n in ROW_SPLIT for n in big]
    full = dict(zip(big, _gather_weights([_cast_into_whole(f"cast_{n}", w[n], n in ROW_SPLIT, chip) for n in big], by_rows)))

    c_all = _all_gather8("gather_c", jnp.pad(c, ((0, 7), (0, 0))))[::8]
    c_act = _silu_rows("silu_c", c_all)
    b_cols = lax.dynamic_slice_in_dim(b_ada, chip * ada_cols, ada_cols, axis=1)
    mod_part = jnp.concatenate([_small_mm(f"mod_{l}", c_act, w_ada[l], "nn") + b_cols[l][None] for l in range(depth)], axis=0)
    mod_all = _all_gather8("gather_mod", mod_part).reshape(4, 2, depth, 8, ada_cols)[:, 0]
    mod_mine = lax.dynamic_index_in_dim(mod_all, me, axis=2, keepdims=False)
    mod = mod_mine.transpose(1, 0, 2).reshape(depth, 6, d)

    layer_w = [{**{n: (full[n], l) for n in big}, **{n: w[n][l] for n in SMALL}} for l in range(depth)]
    loss_cols, dx, dmods, lgrads, stacked = _local_step(x[0], loss_target[0], mod, layer_w)
    loss = lax.psum(jnp.sum(loss_cols), ("x", "y", "c"))
    grad_x = dx[None]

    theirs = _pair_exchange([stacked[n] for n in big], by_rows)
    pairs = [_pair_sum(f"pair_{n}", stacked[n], t, n in ROW_SPLIT, ci, chip) for n, t in zip(big, theirs)]
    landed = _chip_scatter([p[0] for p in pairs], [p[1] for p in pairs])
    halves = [_sum_slots(f"sum_{n}", p, half=ci) for n, p in zip(big, landed)]
    grad = dict(zip(big, _swap_halves(halves)))

    pieces = [dmods] + [jnp.stack([lgrads[l][n] for l in range(depth)]) for n in SMALL]
    slab = _pad_rows(jnp.concatenate([p.reshape(-1) for p in pieces]))
    slabs = _all_gather8("gather_small", slab).reshape(8, 1, *slab.shape)
    total = _sum_slots("sum_small", slabs)[0].reshape(-1)
    at = dmods.size
    grad["b_ada"] = total[:at].reshape(depth, 6 * d)
    for n, p in zip(SMALL, pieces[1:]):
        grad[n] = total[at:at + p.size].reshape(p.shape)
        at += p.size
    dmod_all = slabs.reshape(8, -1)[:, :dmods.size].reshape(8, depth, 4, ada_cols)
    dmod_cols = lax.dynamic_index_in_dim(dmod_all, chip, axis=2, keepdims=False)
    grad["w_ada"] = jnp.stack([_small_mm(f"dw_ada_{l}", c_act, dmod_cols[:, l], "tn") for l in range(depth)])

    delta, new_m, new_v = {}, {}, {}
    for n in WEIGHTS:
        grad[n], delta[n], new_m[n], new_v[n] = _adamw(f"adamw_{n}", w[n], grad[n], mom[n], var[n])
    return (loss, grad_x, *[grad[n] for n in WEIGHTS], *[delta[n] for n in WEIGHTS], *[new_m[n] for n in WEIGHTS],
            *[new_v[n] for n in WEIGHTS])
```

```python
import functools
import math

import jax
import jax.numpy as jnp
from jax import lax
from jax.experimental import pallas as pl
from jax.experimental.pallas import tpu as pltpu

F32 = jnp.float32
BF16 = jnp.bfloat16
MESH = pl.DeviceIdType.MESH

LANES = 128
HEAD_DIM = 64
SB_WIDTH = 512
ATT_TILE = 256
SSM_GROUPS, SSM_STATE, SSM_GROUP = 32, 64, 16
N_STATE = SSM_GROUPS * SSM_STATE
SSM_BLOCKS = SSM_GROUPS * SSM_GROUP // LANES
U_OFFSET = 3 * 512
LN_EPS = 1e-5
DEPTH = 2
ALPHA = (2 * DEPTH) ** 0.25
ADAM_LR, ADAM_B1, ADAM_B2, ADAM_EPS, ADAM_WD, ADAM_STEP = 0.001, 0.9, 0.999, 1e-08, 0.01, 10
VMEM_LIMIT = 56 * 1024 * 1024
GELU_K = math.sqrt(2.0 / math.pi)
GELU_C = 0.044715


def _params(**kw):
    return pltpu.CompilerParams(vmem_limit_bytes=VMEM_LIMIT, **kw)


def _pick(n, prefs):
    for p in prefs:
        if n % p == 0:
            return p
    return n


def _rowwise(name, fn, rows, vecs, outs, sums=(), tm=None):
    s = rows[0][0].shape[0]
    tm = tm or _pick(s, (256, 128, 64, 8))
    nin, no, ns = len(rows) + len(vecs), len(outs), len(sums)

    def body(*refs):
        res = fn(*[r[...].astype(F32) for r in refs[:nin]])
        res = res if isinstance(res, tuple) else (res,)
        for r, v in zip(refs[nin:nin + no], res[:no]):
            r[...] = v.astype(r.dtype)
        if ns:
            @pl.when(pl.program_id(0) == 0)
            def _():
                for r in refs[nin + no:]:
                    r[...] = jnp.zeros_like(r)
            for r, v in zip(refs[nin + no:], res[no:]):
                r[...] += v

    in_specs = [pl.BlockSpec((tm, w), lambda i, cb=cb: (i, cb)) for _, cb, w in rows]
    in_specs += [pl.BlockSpec(v.shape, lambda i: (0, 0)) for v in vecs]
    out_specs = [pl.BlockSpec((tm, w), lambda i: (i, 0)) for w, _ in outs]
    out_specs += [pl.BlockSpec((1, w), lambda i: (0, 0)) for w in sums]
    out_shape = [jax.ShapeDtypeStruct((s, w), dt) for w, dt in outs]
    out_shape += [jax.ShapeDtypeStruct((1, w), F32) for w in sums]
    res = pl.pallas_call(
        body, name=name, grid=(s // tm,), in_specs=in_specs, out_specs=out_specs, out_shape=out_shape,
        compiler_params=_params(dimension_semantics=("arbitrary",)),
    )(*[a for a, _, _ in rows], *vecs)
    return res[0] if len(res) == 1 else tuple(res)


MM_TILES = (1408, 1024, 512, 256, 128)


def _slab_spec(block, index, slab):
    if slab is None:
        return pl.BlockSpec(block, index)
    return pl.BlockSpec((None, *block), lambda *g: (slab, *index(*g)))


def _mm(name, a, b, mode, out_dtype=F32, into=None):
    b, b_slab = b if isinstance(b, tuple) else (b, None)
    if mode == "nn":
        m, k, n = a.shape[0], a.shape[1], b.shape[-1]
    elif mode == "nt":
        m, k, n = a.shape[0], a.shape[1], b.shape[-2]
    else:
        k, m, n = a.shape[0], a.shape[1], b.shape[-1]
    tm = _pick(m, MM_TILES if mode == "tn" else (2048,) + MM_TILES[1:])
    tn = _pick(n, MM_TILES)
    tk = _pick(k, (2048,) + MM_TILES if mode == "tn" else MM_TILES)
    nk = k // tk
    dims = {"nn": ((1,), (0,)), "nt": ((1,), (1,)), "tn": ((0,), (0,))}[mode]

    def body(a_ref, b_ref, *rest):
        o_ref = rest[-2] if nk > 1 else rest[-1]
        prod = lax.dot_general(a_ref[...].astype(BF16), b_ref[...].astype(BF16), (dims, ((), ())),
                               preferred_element_type=F32)
        if nk == 1:
            o_ref[...] = prod.astype(o_ref.dtype)
            return
        acc_ref = rest[-1]
        kk = pl.program_id(2)

        @pl.when(kk == 0)
        def _():
            acc_ref[...] = prod

        @pl.when(kk > 0)
        def _():
            acc_ref[...] += prod

        @pl.when(kk == nk - 1)
        def _():
            o_ref[...] = acc_ref[...].astype(o_ref.dtype)

    if mode == "tn":
        a_spec = pl.BlockSpec((tk, tm), lambda i, j, kk: (kk, i))
    else:
        a_spec = pl.BlockSpec((tm, tk), lambda i, j, kk: (i, kk))
    b_block, b_index = ((tn, tk), lambda i, j, kk: (j, kk)) if mode == "nt" else ((tk, tn), lambda i, j, kk: (kk, j))
    b_spec = _slab_spec(b_block, b_index, b_slab)
    in_specs, operands, aliases = [a_spec, b_spec], [a, b], {}
    if into is None:
        out_spec = pl.BlockSpec((tm, tn), lambda i, j, kk: (i, j))
        out_shape = jax.ShapeDtypeStruct((m, n), out_dtype)
    else:
        buf, slab, count = into
        out_spec = pl.BlockSpec((None, tm, tn), lambda i, j, kk: (slab, i, j))
        out_shape = jax.ShapeDtypeStruct((count, m, n), out_dtype)
        if buf is not None:
            in_specs.append(pl.BlockSpec(memory_space=pl.ANY))
            operands.append(buf)
            aliases = {2: 0}
    return pl.pallas_call(
        body, name=name, grid=(m // tm, n // tn, nk), in_specs=in_specs, out_specs=out_spec, out_shape=out_shape,
        scratch_shapes=[pltpu.VMEM((tm, tn), F32)] if nk > 1 else [], input_output_aliases=aliases,
        compiler_params=_params(dimension_semantics=("arbitrary", "arbitrary", "arbitrary")),
    )(*operands)


def _ffn_in(name, h, w):
    w, slab = w if isinstance(w, tuple) else (w, None)
    s, d = h.shape
    f = w.shape[-1] // 2
    tm, tn = _pick(s, MM_TILES[2:]), _pick(f, MM_TILES)

    def body(h_ref, wg_ref, wu_ref, g_ref, u_ref, a_ref):
        hb = h_ref[...]
        g, u = _nn(hb, wg_ref[...]), _nn(hb, wu_ref[...])
        g_ref[...] = g.astype(BF16)
        u_ref[...] = u.astype(BF16)
        a_ref[...] = (g * jax.nn.sigmoid(g) * u).astype(BF16)

    out = pl.BlockSpec((tm, tn), lambda i, j: (i, j))
    return pl.pallas_call(
        body, name=name, grid=(s // tm, f // tn),
        in_specs=[pl.BlockSpec((tm, d), lambda i, j: (i, 0)), _slab_spec((d, tn), lambda i, j: (0, j), slab),
                  _slab_spec((d, tn), lambda i, j: (0, f // tn + j), slab)],
        out_specs=[out, out, out], out_shape=[jax.ShapeDtypeStruct((s, f), BF16)] * 3,
        compiler_params=_params(dimension_semantics=("arbitrary", "arbitrary")),
    )(h, w, w)


def _small_mm(name, a, b, mode):
    dims = {"nn": ((1,), (0,)), "tn": ((0,), (0,))}[mode]
    m = a.shape[0] if mode == "nn" else a.shape[1]

    def body(a_ref, b_ref, o_ref):
        o_ref[...] = lax.dot_general(a_ref[...], b_ref[...], (dims, ((), ())), precision=lax.Precision.HIGHEST,
                                     preferred_element_type=F32)

    return pl.pallas_call(body, name=name, out_shape=jax.ShapeDtypeStruct((m, b.shape[1]), F32),
                          compiler_params=_params())(a, b)


def _norm(x):
    mu = jnp.mean(x, axis=-1, keepdims=True)
    xc = x - mu
    rstd = lax.rsqrt(jnp.mean(xc * xc, axis=-1, keepdims=True) + LN_EPS)
    return xc * rstd, rstd


def _norm_bwd(dn, n, rstd):
    return rstd * (dn - jnp.mean(dn, axis=-1, keepdims=True) - n * jnp.mean(dn * n, axis=-1, keepdims=True))


def _colsum(v):
    return jnp.sum(v, axis=0, keepdims=True)


def _gelu(x):
    return 0.5 * x * (1.0 + jnp.tanh(GELU_K * (x + GELU_C * x * x * x)))


def _gelu_grad(x):
    t = jnp.tanh(GELU_K * (x + GELU_C * x * x * x))
    return 0.5 * (1.0 + t) + 0.5 * x * (1.0 - t * t) * GELU_K * (1.0 + 3.0 * GELU_C * x * x)


def _log_sigmoid_parts(z):
    lb = jnp.minimum(z, 0.0) - jnp.log(1.0 + jnp.exp(-jnp.abs(z)))
    return lb, lb - z


def _kv_transposed(proj, t):
    s = proj.shape[0]
    nb, nhp = s // t, SB_WIDTH // LANES

    def body(k_ref, v_ref, kt_ref, vt_ref):
        k, v = k_ref[...].astype(F32), v_ref[...].astype(F32)
        for hp in range(nhp):
            kt_ref[hp, 0] = k[:, hp * LANES:(hp + 1) * LANES].T.astype(BF16)
            vt_ref[hp, 0] = v[:, hp * LANES:(hp + 1) * LANES].T.astype(BF16)

    col = lambda cb: pl.BlockSpec((t, SB_WIDTH), lambda i, cb=cb: (i, cb))
    t_out = pl.BlockSpec((nhp, 1, LANES, t), lambda i: (0, i, 0, 0))
    return pl.pallas_call(
        body, name="kv_transposed", grid=(nb,), in_specs=[col(1), col(2)], out_specs=[t_out, t_out],
        out_shape=[jax.ShapeDtypeStruct((nhp, nb, LANES, t), BF16)] * 2,
        compiler_params=_params(dimension_semantics=("arbitrary",)),
    )(proj, proj)


def _tile_masks(t):
    row = lax.broadcasted_iota(jnp.int32, (t, t), 0)
    col = lax.broadcasted_iota(jnp.int32, (t, t), 1)
    return row, col


DEAD_LOG_WEIGHT = -110.0


def _walk_down(i, tiles, state, alive):
    st = lax.cond(i == 0, lambda s_: tiles([i], s_, [True]), lambda s_: tiles([i, i - 1], s_, [True, False]), state)
    n = jnp.maximum(i - 1, 0)

    def pair(c):
        nxt = tiles([i - 2 - 2 * c[0], i - 3 - 2 * c[0]], c[1], [False, False])
        return c[0] + 1, nxt, alive(nxt)

    p, st, go = lax.while_loop(lambda c: (c[0] < n // 2) & c[2], pair, (jnp.int32(0), st, alive(st)))
    return lax.cond((n % 2 == 1) & (p == n // 2) & go, lambda s_: tiles([0], s_, [False]), lambda s_: s_, st)


def _walk_up(i, first, tiles, state):
    n = jnp.maximum(i - 1 - first, 0)
    st = lax.fori_loop(0, n // 2, lambda p, s_: tiles([first + 2 * p, first + 2 * p + 1], s_, [False, False]), state)
    st = lax.cond(n % 2 == 1, lambda s_: tiles([i - 2], s_, [False]), lambda s_: s_, st)
    return lax.cond(i == 0, lambda s_: tiles([i], s_, [True]), lambda s_: tiles([i - 1, i], s_, [False, True]), st)


def _nt(a, b):
    return lax.dot_general(a, b, (((1,), (1,)), ((), ())), preferred_element_type=F32)


def _nn(a, b):
    return jnp.dot(a, b, preferred_element_type=F32)


def _attn_fwd(proj, vt3, t):
    s = proj.shape[0]
    nb, nhp = s // t, SB_WIDTH // LANES

    def body(q_ref, k_ref, vt_ref, o_ref, car_ref):
        i = pl.program_id(1)
        q2 = q_ref[...] * (1.0 / math.sqrt(HEAD_DIM))
        lane_q = lax.broadcasted_iota(jnp.int32, q2.shape, 1)
        row, col = _tile_masks(t)
        later = (col > row).astype(BF16)
        valid = row < col
        orow = lax.broadcasted_iota(jnp.int32, (LANES, t), 0)
        car_ref[...] = jnp.full(car_ref.shape, 2.0 * DEAD_LOG_WEIGHT, F32)
        qh = [jnp.where((lane_q < HEAD_DIM) == (hh == 0), q2, jnp.zeros_like(q2)) for hh in range(2)]

        def tiles(js, state, diagonal):
            chains = [(n, hh) for n in range(len(js)) for hh in range(2)]
            kb = [k_ref[pl.ds(pl.multiple_of(j * t, t), t), :] for j in js]
            z = {ch: _nt(kb[ch[0]], qh[ch[1]]) for ch in chains}
            lb, aft, csum = {}, {}, {}
            for ch in chains:
                lb[ch], l1m = _log_sigmoid_parts(z[ch])
                if diagonal[ch[0]]:
                    l1m = jnp.where(valid, l1m, 0.0)
                aft[ch] = _nn(later, l1m.astype(BF16))
                csum[ch] = _colsum(l1m)
            state = list(state)
            for ch in chains:
                n, hh = ch
                c_after, acc = state[hh]
                w = jnp.exp(lb[ch] + aft[ch] + c_after)
                if diagonal[ch[0]]:
                    w = jnp.where(valid, w, 0.0)
                car_ref[hh, pl.ds(js[n], 1), :] = c_after
                state[hh] = (c_after + csum[ch], acc + _nn(vt_ref[0, js[n]], w.astype(BF16)))
            return tuple(state)

        def alive(state):
            return jnp.max(jnp.maximum(state[0][0], state[1][0])) >= DEAD_LOG_WEIGHT

        zero = (jnp.zeros((1, t), F32), jnp.zeros((LANES, t), F32))
        (_, acc0), (_, acc1) = _walk_down(i, tiles, (zero, zero), alive)
        o_ref[...] = jnp.where(orow < HEAD_DIM, acc0, acc1).T.astype(o_ref.dtype)

    return pl.pallas_call(
        body, name="attn_fwd", grid=(nhp, nb),
        in_specs=[pl.BlockSpec((t, LANES), lambda hp, i: (i, hp)),
                  pl.BlockSpec((s, LANES), lambda hp, i: (0, nhp + hp)),
                  pl.BlockSpec((1, nb, LANES, t), lambda hp, i: (hp, 0, 0, 0))],
        out_specs=[pl.BlockSpec((t, LANES), lambda hp, i: (i, hp)),
                   pl.BlockSpec((2, nb, t), lambda hp, i: (hp, 0, i))],
        out_shape=[jax.ShapeDtypeStruct((s, nhp * LANES), BF16), jax.ShapeDtypeStruct((2 * nhp, nb, s), F32)],
        compiler_params=_params(dimension_semantics=("arbitrary", "arbitrary")),
    )(proj, proj, vt3)


def _attn_bwd(proj, do, kt3, car, t):
    s = proj.shape[0]
    nb, nhp = s // t, SB_WIDTH // LANES

    def body(q_ref, do_ref, k_ref, v_ref, kt_ref, car_ref, dq_ref, dk_ref, dv_ref):
        i = pl.program_id(1)

        @pl.when(i == 0)
        def _():
            dk_ref[...] = jnp.zeros_like(dk_ref)
            dv_ref[...] = jnp.zeros_like(dv_ref)

        q2, do2 = q_ref[...] * (1.0 / math.sqrt(HEAD_DIM)), do_ref[...]
        lane_q = lax.broadcasted_iota(jnp.int32, q2.shape, 1)
        row, col = _tile_masks(t)
        later = (col > row).astype(BF16)
        earlier = (col < row).astype(BF16)
        valid = row < col
        orow = lax.broadcasted_iota(jnp.int32, (LANES, t), 0)
        head = [(lane_q < HEAD_DIM) == (hh == 0) for hh in range(2)]
        qh = [jnp.where(hm, q2, jnp.zeros_like(q2)) for hm in head]
        doh = [jnp.where(hm, do2, jnp.zeros_like(do2)) for hm in head]

        def tiles(js, state, diagonal):
            chains = [(n, hh) for n in range(len(js)) for hh in range(2)]
            rows = [pl.ds(pl.multiple_of(j * t, t), t) for j in js]
            kb = [k_ref[r, :] for r in rows]
            vb = [v_ref[r, :] for r in rows]
            z = {ch: _nt(kb[ch[0]], qh[ch[1]]) for ch in chains}
            dw = {ch: _nt(vb[ch[0]], doh[ch[1]]) for ch in chains}
            lb, beta, aft = {}, {}, {}
            for ch in chains:
                lb[ch], l1m = _log_sigmoid_parts(z[ch])
                beta[ch] = jnp.exp(lb[ch])
                if diagonal[ch[0]]:
                    l1m = jnp.where(valid, l1m, 0.0)
                aft[ch] = _nn(later, l1m.astype(BF16))
            w, g, gsum, g_in = {}, {}, {}, {}
            for ch in chains:
                n, hh = ch
                w[ch] = jnp.exp(lb[ch] + aft[ch] + car_ref[hh, pl.ds(js[n], 1), :])
                if diagonal[ch[0]]:
                    w[ch] = jnp.where(valid, w[ch], 0.0)
                g[ch] = dw[ch] * w[ch]
                g_in[ch] = _nn(earlier, g[ch].astype(BF16))
                gsum[ch] = _colsum(g[ch])
            state = list(state)
            dk_t, dv_t = [None] * len(js), [None] * len(js)
            for ch in chains:
                n, hh = ch
                c_g, dqt = state[hh]
                dz = g[ch] - beta[ch] * (g[ch] + g_in[ch] + c_g)
                if diagonal[ch[0]]:
                    dz = jnp.where(valid, dz, 0.0)
                dzb, wb = dz.astype(BF16), w[ch].astype(BF16)
                dk_h, dv_h = _nn(dzb, qh[hh]), _nn(wb, doh[hh])
                dk_t[n] = dk_h if dk_t[n] is None else dk_t[n] + dk_h
                dv_t[n] = dv_h if dv_t[n] is None else dv_t[n] + dv_h
                state[hh] = (c_g + gsum[ch], dqt + _nn(kt_ref[0, js[n]], dzb))
            for n in range(len(js)):
                dk_ref[rows[n], :] += dk_t[n]
                dv_ref[rows[n], :] += dv_t[n]
            return tuple(state)

        reach = jnp.max(jnp.max(car_ref[...], axis=2, keepdims=True), axis=0)
        dead = (reach < DEAD_LOG_WEIGHT) & (lax.broadcasted_iota(jnp.int32, reach.shape, 0) < i)
        first = jnp.sum(jnp.where(dead, 1.0, 0.0)).astype(jnp.int32)
        zero = (jnp.zeros((1, t), F32), jnp.zeros((LANES, t), F32))
        (_, dq0), (_, dq1) = _walk_up(i, first, tiles, (zero, zero))
        dq_ref[...] = jnp.where(orow < HEAD_DIM, dq0, dq1).T

    tile_spec = pl.BlockSpec((t, LANES), lambda hp, i: (i, hp))
    whole = pl.BlockSpec((s, LANES), lambda hp, i: (0, hp))
    return pl.pallas_call(
        body, name="attn_bwd", grid=(nhp, nb),
        in_specs=[tile_spec, tile_spec, pl.BlockSpec((s, LANES), lambda hp, i: (0, nhp + hp)),
                  pl.BlockSpec((s, LANES), lambda hp, i: (0, 2 * nhp + hp)),
                  pl.BlockSpec((1, nb, LANES, t), lambda hp, i: (hp, 0, 0, 0)),
                  pl.BlockSpec((2, nb, t), lambda hp, i: (hp, 0, i))],
        out_specs=[tile_spec, whole, whole],
        out_shape=[jax.ShapeDtypeStruct((s, nhp * LANES), F32)] * 3,
        compiler_params=_params(dimension_semantics=("arbitrary", "arbitrary")),
    )(proj, do, proj, proj, kt3, car)


SCAN_LANES = 1024
SCAN_ROWS = 8
S5_CHUNKS = 4


def _scan_chunks(v):
    n = v.shape[1] // (2 * LANES)
    return [(v[:, c * 2 * LANES:c * 2 * LANES + LANES], v[:, c * 2 * LANES + LANES:(c + 1) * 2 * LANES]) for c in range(n)]


def _scan_tables(lr, li, reverse):
    if reverse:
        li = -li
    row = lax.broadcasted_iota(jnp.int32, (SCAN_ROWS, LANES), 0)
    powers = [(lr, li)]
    for _ in range(SCAN_ROWS - 1):
        pr, pi = powers[-1]
        powers.append((pr * lr - pi * li, pr * li + pi * lr))
    levels = []
    for d in (1, 2, 4):
        keep = (row < SCAN_ROWS - d) if reverse else (row >= d)
        levels.append((SCAN_ROWS - d if reverse else d,
                       (jnp.where(keep, powers[d - 1][0], 0.0), jnp.where(keep, powers[d - 1][1], 0.0))))
    pr = pi = jnp.zeros((SCAN_ROWS, LANES), F32)
    for r in range(SCAN_ROWS):
        steps = SCAN_ROWS - r if reverse else r + 1
        pr = jnp.where(row == r, powers[steps - 1][0], pr)
        pi = jnp.where(row == r, powers[steps - 1][1], pi)
    return levels, (pr, pi)


def _s5_fwd(proj, u_off, bmat, cmat, lam):
    s, w = proj.shape[0], SSM_BLOCKS * bmat.shape[1]
    tt = _pick(s, (512, 256, 128, 8))
    nt = s // tt
    cin = bmat.shape[0] // SSM_BLOCKS
    chunk = tt // S5_CHUNKS

    def body(u_ref, b_ref, c_ref, lam_ref, h_ref, y_ref, x_ref, st_ref):
        @pl.when(pl.program_id(1) == 0)
        def _():
            st_ref[...] = jnp.zeros_like(st_ref)

        tables = [_scan_tables(lr, li, reverse=False) for lr, li in _scan_chunks(lam_ref[...])]

        def project(k):
            x_ref[k * chunk:(k + 1) * chunk, :] = _nn(u_ref[k * chunk:(k + 1) * chunk, :].astype(BF16), b_ref[...])

        def tile(r0, last):
            last, parts = list(last), []
            for c, (xr, xi) in enumerate(_scan_chunks(x_ref[r0:r0 + SCAN_ROWS, :])):
                levels, (pr, pi) = tables[c]
                for d, (ar, ai) in levels:
                    sr, si = pltpu.roll(xr, d, 0), pltpu.roll(xi, d, 0)
                    xr, xi = xr + ar * sr - ai * si, xi + ar * si + ai * sr
                br, bi = last[2 * c], last[2 * c + 1]
                hr = xr + pr * br - pi * bi
                hi = xi + pr * bi + pi * br
                last[2 * c], last[2 * c + 1] = hr[SCAN_ROWS - 1:], hi[SCAN_ROWS - 1:]
                parts += [hr, hi]
            h_ref[r0:r0 + SCAN_ROWS, :] = jnp.concatenate(parts, axis=1)
            return tuple(last)

        st = st_ref[0:1, :]
        last = tuple(st[:, c * LANES:(c + 1) * LANES] for c in range(SCAN_LANES // LANES))
        project(0)
        for k in range(S5_CHUNKS):
            if k + 1 < S5_CHUNKS:
                project(k + 1)
            for r0 in range(k * chunk, (k + 1) * chunk, SCAN_ROWS):
                last = tile(r0, last)
            y_ref[k * chunk:(k + 1) * chunk, :] = _nn(h_ref[k * chunk:(k + 1) * chunk, :].astype(BF16), c_ref[...])
        st_ref[0:1, :] = jnp.concatenate(last, axis=1)

    return pl.pallas_call(
        body, name="s5_fwd", grid=(SSM_BLOCKS, nt),
        in_specs=[pl.BlockSpec((tt, cin), lambda kb, i: (i, u_off // cin + kb)),
                  pl.BlockSpec((cin, SCAN_LANES), lambda kb, i: (kb, 0)),
                  pl.BlockSpec((SCAN_LANES, cin), lambda kb, i: (kb, 0)),
                  pl.BlockSpec((1, SCAN_LANES), lambda kb, i: (0, kb))],
        out_specs=[pl.BlockSpec((tt, SCAN_LANES), lambda kb, i: (i, kb)), pl.BlockSpec((tt, cin), lambda kb, i: (i, kb))],
        out_shape=[jax.ShapeDtypeStruct((s, w), F32), jax.ShapeDtypeStruct((s, bmat.shape[0]), F32)],
        scratch_shapes=[pltpu.VMEM((tt, SCAN_LANES), F32), pltpu.VMEM((SCAN_ROWS, SCAN_LANES), F32)],
        compiler_params=_params(dimension_semantics=("arbitrary", "arbitrary")),
    )(proj, bmat, cmat, lam)


def _s5_bwd(dy, h, proj, u_off, bmat, cmat, lam):
    s, w = h.shape
    tt = _pick(s, (512, 256, 128, 8))
    nt = s // tt
    cin = bmat.shape[0] // SSM_BLOCKS
    chunk = tt // S5_CHUNKS

    def body(dy_ref, h_ref, u_ref, b_ref, c_ref, lam_ref, du_ref, dlam_ref, db_ref, dc_ref, e_ref, a_ref, st_ref):
        @pl.when(pl.program_id(1) == 0)
        def _():
            st_ref[...] = jnp.zeros_like(st_ref)
            dlam_ref[...] = jnp.zeros_like(dlam_ref)
            db_ref[...] = jnp.zeros_like(db_ref)
            dc_ref[...] = jnp.zeros_like(dc_ref)

        tables = [_scan_tables(lr, li, reverse=True) for lr, li in _scan_chunks(lam_ref[...])]
        nch = len(tables)
        row = lax.broadcasted_iota(jnp.int32, (SCAN_ROWS, LANES), 0)
        rows_first = (((0,), (0,)), ((), ()))

        def project(k):
            e_ref[k * chunk:(k + 1) * chunk, :] = _nt(dy_ref[k * chunk:(k + 1) * chunk, :], c_ref[...])

        def finish(k):
            rows = slice(k * chunk, (k + 1) * chunk)
            adj = a_ref[rows, :].astype(BF16)
            du_ref[rows, :] = _nt(adj, b_ref[...])
            db_ref[...] += lax.dot_general(u_ref[rows, :].astype(BF16), adj, rows_first, preferred_element_type=F32)
            dc_ref[...] += lax.dot_general(h_ref[rows, :].astype(BF16), dy_ref[rows, :], rows_first, preferred_element_type=F32)

        def tile(r0, carry):
            e_c = _scan_chunks(e_ref[r0:r0 + SCAN_ROWS, :])
            h_c = _scan_chunks(h_ref[r0:r0 + SCAN_ROWS, :])
            carry, parts = list(carry), []
            for c in range(nch):
                (yr, yi), (hr, hi) = e_c[c], h_c[c]
                levels, (pr, pi) = tables[c]
                for shift, (lr, li) in levels:
                    sr, si = pltpu.roll(yr, shift, 0), pltpu.roll(yi, shift, 0)
                    yr, yi = yr + lr * sr - li * si, yi + lr * si + li * sr
                nr, ni, dr, di = carry[4 * c:4 * c + 4]
                ar = yr + pr * nr - pi * ni
                ai = yi + pr * ni + pi * nr
                nxr = jnp.where(row == SCAN_ROWS - 1, nr, pltpu.roll(ar, SCAN_ROWS - 1, 0))
                nxi = jnp.where(row == SCAN_ROWS - 1, ni, pltpu.roll(ai, SCAN_ROWS - 1, 0))
                carry[4 * c:4 * c + 4] = [ar[0:1], ai[0:1], dr + nxr * hr + nxi * hi, di + nxi * hr - nxr * hi]
                parts += [ar, ai]
            a_ref[r0:r0 + SCAN_ROWS, :] = jnp.concatenate(parts, axis=1)
            return tuple(carry)

        st, dl = st_ref[0:1, :], dlam_ref[...]
        init = []
        for c in range(nch):
            lo = c * 2 * LANES
            init += [st[:, lo:lo + LANES], st[:, lo + LANES:lo + 2 * LANES],
                     dl[:, lo:lo + LANES], dl[:, lo + LANES:lo + 2 * LANES]]
        fin = tuple(init)
        project(S5_CHUNKS - 1)
        for k in reversed(range(S5_CHUNKS)):
            if k > 0:
                project(k - 1)
            for r0 in reversed(range(k * chunk, (k + 1) * chunk, SCAN_ROWS)):
                fin = tile(r0, fin)
            finish(k)
        st_ref[0:1, :] = jnp.concatenate([fin[4 * c + q] for c in range(nch) for q in (0, 1)], axis=1)
        dlam_ref[...] = jnp.concatenate([fin[4 * c + q] for c in range(nch) for q in (2, 3)], axis=1)

        @pl.when(pl.program_id(1) == nt - 1)
        def _():
            dlam_ref[0:1, :] = jnp.sum(dlam_ref[...], axis=0, keepdims=True)

    def rev(width, col):
        return pl.BlockSpec((tt, width), lambda kb, i: (nt - 1 - i, col(kb)))

    return pl.pallas_call(
        body, name="s5_bwd", grid=(SSM_BLOCKS, nt),
        in_specs=[rev(cin, lambda kb: kb), rev(SCAN_LANES, lambda kb: kb), rev(cin, lambda kb: u_off // cin + kb),
                  pl.BlockSpec((cin, SCAN_LANES), lambda kb, i: (kb, 0)),
                  pl.BlockSpec((SCAN_LANES, cin), lambda kb, i: (kb, 0)),
                  pl.BlockSpec((1, SCAN_LANES), lambda kb, i: (0, kb))],
        out_specs=[rev(cin, lambda kb: kb), pl.BlockSpec((SCAN_ROWS, SCAN_LANES), lambda kb, i: (0, kb)),
                   pl.BlockSpec((cin, SCAN_LANES), lambda kb, i: (kb, 0)), pl.BlockSpec((SCAN_LANES, cin), lambda kb, i: (kb, 0))],
        out_shape=[jax.ShapeDtypeStruct((s, bmat.shape[0]), F32), jax.ShapeDtypeStruct((SCAN_ROWS, w), F32),
                   jax.ShapeDtypeStruct((bmat.shape[0], SCAN_LANES), F32), jax.ShapeDtypeStruct((w, cin), F32)],
        scratch_shapes=[pltpu.VMEM((tt, SCAN_LANES), F32), pltpu.VMEM((tt, SCAN_LANES), F32),
                        pltpu.VMEM((SCAN_ROWS, SCAN_LANES), F32)],
        compiler_params=_params(dimension_semantics=("arbitrary", "arbitrary")),
    )(dy, h, proj, bmat, cmat, lam)


def _ssm_params_fwd(a_re, a_im, log_dt, b_re, b_im):
    def body(ar_ref, ai_ref, ldt_ref, br_ref, bi_ref, lr_ref, li_ref, bbr_ref, bbi_ref):
        ar, ai, dt = ar_ref[...], ai_ref[...], jnp.exp(ldt_ref[...])
        mag = jnp.exp(ar * dt)
        lr, li = mag * jnp.cos(ai * dt), mag * jnp.sin(ai * dt)
        den = ar * ar + ai * ai
        cr = ((lr - 1.0) * ar + li * ai) / den
        ci = (li * ar - (lr - 1.0) * ai) / den
        br, bi = br_ref[...], bi_ref[...]
        lr_ref[...], li_ref[...] = lr, li
        bbr_ref[...] = cr * br - ci * bi
        bbi_ref[...] = cr * bi + ci * br

    n = a_re.shape[0]
    v1, v16 = jax.ShapeDtypeStruct((n, 1), F32), jax.ShapeDtypeStruct((n, SSM_GROUP), F32)
    return pl.pallas_call(body, name="ssm_params_fwd", out_shape=[v1, v1, v16, v16],
                          compiler_params=_params())(a_re, a_im, log_dt, b_re, b_im)


def _ssm_params_bwd(a_re, a_im, log_dt, b_re, b_im, g_lr, g_li, g_bbr, g_bbi):
    n = a_re.shape[0]

    def body(ar_ref, ai_ref, ldt_ref, br_ref, bi_ref, glr_ref, gli_ref, gbr_ref, gbi_ref,
             dar_ref, dai_ref, dldt_ref, dbr_ref, dbi_ref):
        ar, ai, dt = ar_ref[...], ai_ref[...], jnp.exp(ldt_ref[...])
        mag = jnp.exp(ar * dt)
        lr, li = mag * jnp.cos(ai * dt), mag * jnp.sin(ai * dt)
        den = ar * ar + ai * ai
        cr = ((lr - 1.0) * ar + li * ai) / den
        ci = (li * ar - (lr - 1.0) * ai) / den
        br, bi, gbr, gbi = br_ref[...], bi_ref[...], gbr_ref[...], gbi_ref[...]
        dbr_ref[...] = gbr * cr + gbi * ci
        dbi_ref[...] = gbi * cr - gbr * ci
        gcr = jnp.sum(gbr * br + gbi * bi, axis=1, keepdims=True)
        gci = jnp.sum(gbi * br - gbr * bi, axis=1, keepdims=True)
        ir, ii = ar / den, -ai / den
        glr = glr_ref[...] + gcr * ir + gci * ii
        gli = gli_ref[...] + gci * ir - gcr * ii
        qr, qi = cr * ir - ci * ii, cr * ii + ci * ir
        gar = -(gcr * qr + gci * qi)
        gai = -(gci * qr - gcr * qi)
        gxr = glr * lr + gli * li
        gxi = gli * lr - glr * li
        dar_ref[...] = gar + gxr * dt
        dai_ref[...] = gai + gxi * dt
        gdt = (gxr * ar + gxi * ai) * dt
        rowg = lax.broadcasted_iota(jnp.int32, (n, SSM_GROUPS), 0) // SSM_STATE
        colg = lax.broadcasted_iota(jnp.int32, (n, SSM_GROUPS), 1)
        dldt_ref[...] = jnp.sum(jnp.where(rowg == colg, gdt, 0.0), axis=0, keepdims=True)

    v1, v16 = jax.ShapeDtypeStruct((n, 1), F32), jax.ShapeDtypeStruct((n, SSM_GROUP), F32)
    return pl.pallas_call(body, name="ssm_params_bwd",
                          out_shape=[v1, v1, jax.ShapeDtypeStruct((1, SSM_GROUPS), F32), v16, v16],
                          compiler_params=_params())(a_re, a_im, log_dt, b_re, b_im, g_lr, g_li, g_bbr, g_bbi)


def _interleave(re, im, axis):
    shp = list(re.shape)
    new = shp[:axis] + [shp[axis] // LANES, LANES] + shp[axis + 1:]
    st = jnp.stack([re.reshape(new), im.reshape(new)], axis=axis + 1)
    return st.reshape(shp[:axis] + [2 * shp[axis]] + shp[axis + 1:])


def _deinterleave(v, axis):
    shp = list(v.shape)
    r = v.reshape(shp[:axis] + [shp[axis] // (2 * LANES), 2, LANES] + shp[axis + 1:])
    out = shp[:axis] + [shp[axis] // 2] + shp[axis + 1:]
    return (lax.index_in_dim(r, 0, axis + 1, keepdims=False).reshape(out),
            lax.index_in_dim(r, 1, axis + 1, keepdims=False).reshape(out))


def _b_matrix(bbr, bbi):
    per = SSM_GROUPS // SSM_BLOCKS
    eye = jnp.eye(per, dtype=F32)

    def blockdiag(v):
        x = v.reshape(SSM_BLOCKS, per, SSM_STATE, SSM_GROUP).transpose(0, 1, 3, 2)
        return (eye[None, :, None, :, None] * x[:, :, :, None, :]).reshape(SSM_GROUPS * SSM_GROUP, per * SSM_STATE)

    return _interleave(blockdiag(bbr), blockdiag(bbi), 1)


def _diag_blocks(v, rows, cols):
    per = SSM_GROUPS // SSM_BLOCKS
    x = v.reshape(SSM_BLOCKS, per, rows, per, cols) * jnp.eye(per, dtype=v.dtype)[None, :, None, :, None]
    return jnp.sum(x, axis=3).reshape(SSM_GROUPS, rows, cols)


def _b_matrix_grad(d):
    def diag(v):
        return _diag_blocks(v, SSM_GROUP, SSM_STATE).transpose(0, 2, 1).reshape(N_STATE, SSM_GROUP)

    dr, di = _deinterleave(d, 1)
    return diag(dr), diag(di)


def _c_matrix(c_re, c_im):
    per = SSM_GROUPS // SSM_BLOCKS
    eye = jnp.eye(per, dtype=F32)

    def blockdiag(v):
        x = v.reshape(SSM_BLOCKS, per, SSM_GROUP, SSM_STATE).transpose(0, 1, 3, 2)
        return (x[:, :, :, None, :] * eye[None, :, None, :, None]).reshape(N_STATE, per * SSM_GROUP)

    return _interleave(blockdiag(c_re), blockdiag(-c_im), 0)


def _c_matrix_grad(d):
    def diag(v):
        return _diag_blocks(v, SSM_STATE, SSM_GROUP).transpose(0, 2, 1)

    dr, di = _deinterleave(d, 0)
    return diag(dr), -diag(di)


def _row(v):
    return v.reshape(1, -1)


def _ssm_inputs(p):
    rows = lambda v: v.reshape(N_STATE, -1)
    ldt = jnp.repeat(p["ssm_log_dt"], SSM_STATE).reshape(N_STATE, 1)
    return rows(p["ssm_a_re"]), rows(p["ssm_a_im"]), ldt, rows(p["ssm_b_re"]), rows(p["ssm_b_im"])


def _lnmod(x, sc, sh):
    return _norm(x)[0] * (1.0 + sc) + sh


def _resid_ln(x, y, g, lg, lb):
    return _norm(ALPHA * x + (1.0 + g) * y)[0] * lg + lb


def _resid_ln_lnmod(x, y, g, lg, lb, sc, sh):
    xo = _resid_ln(x, y, g, lg, lb)
    return xo, _lnmod(xo, sc, sh)


def _layer_fwd(x, h1, mod, p, tag, next_mod):
    d = x.shape[1]
    sh_m, sc_m, g_m, sh_f, sc_f, g_f = [_row(mod[i]) for i in range(6)]
    nm = lambda s: f"{s}_{tag}"
    proj = _mm(nm("proj"), h1, p["w_in"], "nn", out_dtype=BF16)
    t = min(ATT_TILE, x.shape[0])
    kt3, vt3 = _kv_transposed(proj, t)
    att, car = _attn_fwd(proj, vt3, t)
    y_sb = _mm(nm("sb_up"), att, p["w_sb_up"], "nn", out_dtype=BF16)

    ssm_in = _ssm_inputs(p)
    lam_r, lam_i, bbr, bbi = _ssm_params_fwd(*ssm_in)
    lam = _interleave(lam_r.reshape(1, N_STATE), lam_i.reshape(1, N_STATE), 1)
    bmat = _b_matrix(bbr, bbi).astype(BF16)
    cmat = _c_matrix(p["ssm_c_re"], p["ssm_c_im"]).astype(BF16)
    hst, yc = _s5_fwd(proj, U_OFFSET, bmat, cmat, lam)

    def ssm_act(yc, u, dsk):
        y0 = yc + dsk * u
        return y0, _gelu(y0)

    y0, y1 = _rowwise(nm("ssm_act"), ssm_act, [(yc, 0, 512), (proj, 3, 512)], [_row(p["ssm_d"])], [(512, F32), (512, F32)])
    gl = _mm(nm("glu"), y1, p["w_glu"], "nn")
    y2 = _rowwise(nm("glu_act"), lambda y1, gl, b: y1 * jax.nn.sigmoid(gl + b), [(y1, 0, 512), (gl, 0, 512)],
                  [_row(p["b_glu"])], [(512, BF16)])
    y_ssm = _mm(nm("ssm_up"), y2, p["w_ssm_up"], "nn", out_dtype=BF16)

    def merge(gsb, gss, ysb, yss):
        return jax.nn.sigmoid(gsb) * ysb + jax.nn.sigmoid(gss) * yss

    merged = _rowwise(nm("merge"), merge, [(proj, 2, d), (proj, 3, d), (y_sb, 0, d), (y_ssm, 0, d)], [], [(d, BF16)])
    y = _mm(nm("out"), merged, p["w_out"], "nn")

    x1, h2 = _rowwise(nm("ln1"), _resid_ln_lnmod, [(x, 0, d), (y, 0, d)],
                      [g_m, _row(p["ln1_g"]), _row(p["ln1_b"]), sc_f, sh_f], [(d, F32), (d, BF16)])
    f_gate, f_up, act = _ffn_in(nm("ffn_in"), h2, p["w_ffn_in"])
    yf = _mm(nm("ffn_out"), act, p["w_ffn_out"], "nn")
    x2 = h1_next = None
    if next_mod is not None:
        x2, h1_next = _rowwise(nm("ln2"), _resid_ln_lnmod, [(x1, 0, d), (yf, 0, d)],
                               [g_f, _row(p["ln2_g"]), _row(p["ln2_b"]), next_mod[1], next_mod[0]], [(d, F32), (d, BF16)])
    saved = dict(x=x, h1=h1, proj=proj, ssm_in=ssm_in, kt3=kt3, car=car, att=att, y_sb=y_sb, lam=lam, bmat=bmat,
                 cmat=cmat, hst=hst, y0=y0, y1=y1, gl=gl, y2=y2, y_ssm=y_ssm, merged=merged, y=y, x1=x1, h2=h2, f_gate=f_gate, f_up=f_up,
                 act=act, yf=yf, t=t)
    return x2, h1_next, saved


def _resid_ln_bwd(x, y, dxo, g, lg):
    n, rstd = _norm(ALPHA * x + (1.0 + g) * y)
    dr = _norm_bwd(dxo * lg, n, rstd)
    return ALPHA * dr, (1.0 + g) * dr, _colsum(dxo * n), _colsum(dxo), _colsum(dr * y)


def _lnmod_bwd(x, dh, dxa, sc):
    n, rstd = _norm(x)
    return dxa + _norm_bwd(dh * (1.0 + sc), n, rstd), _colsum(dh * n), _colsum(dh)


def _lnmod_resid_ln_bwd(xo, dh, dxa, x, y, sc, g, lg):
    dxo, dsc, dsh = _lnmod_bwd(xo, dh, dxa, sc)
    dx, dy, dlg, dlb, dg = _resid_ln_bwd(x, y, dxo, g, lg)
    return dx, dy, dsc, dsh, dlg, dlb, dg


def _layer_bwd(dx1a, dyf, mod, p, sv, layer, depth, stacked):
    d = dx1a.shape[1]
    sh_m, sc_m, g_m, sh_f, sc_f, g_f = [_row(mod[i]) for i in range(6)]
    nm = lambda s: f"{s}_{layer}"
    grads = {}

    def weight_grad(n, a, b, **kw):
        grads[n] = _mm(nm("d" + n), a, b, "tn", out_dtype=BF16, into=(stacked.get(n), layer, depth), **kw)

    dact = _mm(nm("d_act"), dyf, p["w_ffn_out"], "nt", out_dtype=BF16)
    weight_grad("w_ffn_out", sv["act"], dyf)
    fh = sv["f_gate"].shape[1]

    def swiglu_bwd(g, u, da):
        sg = jax.nn.sigmoid(g)
        return jnp.concatenate([da * u * sg * (1.0 + g * (1.0 - sg)), da * g * sg], axis=1)

    df = _rowwise(nm("swiglu_bwd"), swiglu_bwd, [(sv["f_gate"], 0, fh), (sv["f_up"], 0, fh), (dact, 0, fh)], [], [(2 * fh, BF16)])
    dh2 = _mm(nm("d_h2"), df, p["w_ffn_in"], "nt")
    weight_grad("w_ffn_in", sv["h2"], df)
    dxa, dy, dsc_f, dsh_f, grads["ln1_g"], grads["ln1_b"], dg_m = _rowwise(
        nm("ln1_bwd"), _lnmod_resid_ln_bwd, [(sv["x1"], 0, d), (dh2, 0, d), (dx1a, 0, d), (sv["x"], 0, d), (sv["y"], 0, d)],
        [sc_f, g_m, _row(p["ln1_g"])], [(d, F32), (d, BF16)], [d] * 5)
    dmerged = _mm(nm("d_merged"), dy, p["w_out"], "nt", out_dtype=BF16)
    weight_grad("w_out", sv["merged"], dy)

    def merge_bwd(gsb, gss, ysb, yss, dm):
        s1, s2 = jax.nn.sigmoid(gsb), jax.nn.sigmoid(gss)
        return s1 * dm, s2 * dm, dm * ysb * s1 * (1.0 - s1), dm * yss * s2 * (1.0 - s2)

    dy_sb, dy_ssm, dg_sb, dg_ssm = _rowwise(
        nm("merge_bwd"), merge_bwd, [(sv["proj"], 2, d), (sv["proj"], 3, d), (sv["y_sb"], 0, d), (sv["y_ssm"], 0, d),
                                     (dmerged, 0, d)], [], [(d, BF16)] * 4)
    dy2 = _mm(nm("d_y2"), dy_ssm, p["w_ssm_up"], "nt")
    weight_grad("w_ssm_up", sv["y2"], dy_ssm)

    def glu_act_bwd(y1, gl, dy2, b):
        sg = jax.nn.sigmoid(gl + b)
        dgl = dy2 * y1 * sg * (1.0 - sg)
        return dy2 * sg, dgl, _colsum(dgl)

    dy1a, dgl, grads["b_glu"] = _rowwise(nm("glu_act_bwd"), glu_act_bwd, [(sv["y1"], 0, 512), (sv["gl"], 0, 512), (dy2, 0, 512)],
                                         [_row(p["b_glu"])], [(512, F32), (512, BF16)], [512])
    dy1b = _mm(nm("d_y1"), dgl, p["w_glu"], "nt")
    weight_grad("w_glu", sv["y1"], dgl)

    def ssm_act_bwd(y0, u, dy1a, dy1b, dsk):
        dy0 = (dy1a + dy1b) * _gelu_grad(y0)
        return dy0, dsk * dy0, _colsum(dy0 * u)

    dy0, du_a, grads["ssm_d"] = _rowwise(nm("ssm_act_bwd"), ssm_act_bwd,
                                         [(sv["y0"], 0, 512), (sv["proj"], 3, 512), (dy1a, 0, 512), (dy1b, 0, 512)],
                                         [_row(p["ssm_d"])], [(512, BF16), (512, F32)], [512])
    du_b, dlam, d_bmat, d_cmat = _s5_bwd(dy0, sv["hst"], sv["proj"], U_OFFSET, sv["bmat"], sv["cmat"], sv["lam"])
    grads["ssm_c_re"], grads["ssm_c_im"] = _c_matrix_grad(d_cmat)
    g_bbr, g_bbi = _b_matrix_grad(d_bmat)
    g_lr, g_li = _deinterleave(dlam[0:1], 1)
    da_re, da_im, dldt, db_re, db_im = _ssm_params_bwd(*sv["ssm_in"], g_lr.reshape(N_STATE, 1), g_li.reshape(N_STATE, 1),
                                                       g_bbr, g_bbi)
    grads["ssm_a_re"] = da_re.reshape(SSM_GROUPS, SSM_STATE)
    grads["ssm_a_im"] = da_im.reshape(SSM_GROUPS, SSM_STATE)
    grads["ssm_log_dt"] = dldt.reshape(SSM_GROUPS)
    grads["ssm_b_re"] = db_re.reshape(SSM_GROUPS, SSM_STATE, SSM_GROUP)
    grads["ssm_b_im"] = db_im.reshape(SSM_GROUPS, SSM_STATE, SSM_GROUP)
    datt = _mm(nm("d_att"), dy_sb, p["w_sb_up"], "nt", out_dtype=BF16)
    weight_grad("w_sb_up", sv["att"], dy_sb)
    dqs, dk, dv = _attn_bwd(sv["proj"], datt, sv["kt3"], sv["car"], sv["t"])

    def dproj_cols(dqs, dk, dv, dua, dub, dgsb, dgss):
        return jnp.concatenate([dqs * (1.0 / math.sqrt(HEAD_DIM)), dk, dv, dua + dub, dgsb.astype(F32), dgss.astype(F32)],
                               axis=1)

    dproj = _rowwise(nm("dproj"), dproj_cols, [(dqs, 0, 512), (dk, 0, 512), (dv, 0, 512), (du_a, 0, 512), (du_b, 0, 512),
                                               (dg_sb, 0, d), (dg_ssm, 0, d)], [], [(2048 + 2 * d, BF16)])
    dh1 = _mm(nm("d_h1"), dproj, p["w_in"], "nt")
    weight_grad("w_in", sv["h1"], dproj)
    for k in ("ln1_g", "ln1_b", "ssm_d", "b_glu"):
        grads[k] = grads[k].reshape(-1)
    return dh1, dxa, grads, (dg_m, dsh_f, dsc_f)


def _local_step(x, target, mod, layer_w):
    depth, d = len(layer_w), x.shape[1]
    rows = lambda l: [_row(mod[l][i]) for i in range(6)]
    h1 = _rowwise("lnmod1_0", _lnmod, [(x, 0, d)], [rows(0)[1], rows(0)[0]], [(d, BF16)])
    xs, saved = x, []
    for l in range(depth):
        xs, h1, sv = _layer_fwd(xs, h1, mod[l], layer_w[l], str(l), rows(l + 1)[:2] if l + 1 < depth else None)
        saved.append(sv)

    def head_bwd(x1, yf, tgt, g, lg, lb):
        err = _resid_ln(x1, yf, g, lg, lb) - tgt
        return _resid_ln_bwd(x1, yf, err * (1.0 / d), g, lg) + (_colsum(err * err) * (0.5 / d),)

    def boundary_bwd(dh, dxa, x1, yf, sc, g, lg, lb):
        dxo, dsc, dsh = _lnmod_bwd(_resid_ln(x1, yf, g, lg, lb), dh, dxa, sc)
        return _resid_ln_bwd(x1, yf, dxo, g, lg) + (dsc, dsh)

    lgrads, sums, stacked = [None] * depth, [dict() for _ in range(depth)], {}
    last, p = saved[-1], layer_w[-1]
    dx1a, dyf, dlg, dlb, dg_f, loss_cols = _rowwise(
        "head_bwd", head_bwd, [(last["x1"], 0, d), (last["yf"], 0, d), (target, 0, d)],
        [rows(depth - 1)[5], _row(p["ln2_g"]), _row(p["ln2_b"])], [(d, F32), (d, BF16)], [d] * 4)
    for l in reversed(range(depth)):
        sums[l]["g_f"] = dg_f
        dh1, dxa, lgrads[l], (sums[l]["g_m"], sums[l]["sh_f"], sums[l]["sc_f"]) = _layer_bwd(
            dx1a, dyf, mod[l], layer_w[l], saved[l], l, depth, stacked)
        lgrads[l]["ln2_g"], lgrads[l]["ln2_b"] = dlg.reshape(-1), dlb.reshape(-1)
        stacked = {n: lgrads[l][n] for n in COL_SPLIT + ROW_SPLIT}
        if l > 0:
            prev, p = saved[l - 1], layer_w[l - 1]
            dx1a, dyf, dlg, dlb, dg_f, sums[l]["sc_m"], sums[l]["sh_m"] = _rowwise(
                f"boundary_bwd_{l}", boundary_bwd, [(dh1, 0, d), (dxa, 0, d), (prev["x1"], 0, d), (prev["yf"], 0, d)],
                [rows(l)[1], rows(l - 1)[5], _row(p["ln2_g"]), _row(p["ln2_b"])], [(d, F32), (d, BF16)], [d] * 5)
        else:
            dx, sums[l]["sc_m"], sums[l]["sh_m"] = _rowwise("lnmod1_bwd", _lnmod_bwd, [(x, 0, d), (dh1, 0, d), (dxa, 0, d)],
                                                            [rows(0)[1]], [(d, F32)], [d, d])
    dmod = jnp.stack([jnp.concatenate([sums[l][k] for k in ("sh_m", "sc_m", "g_m", "sh_f", "sc_f", "g_f")], axis=0)
                      for l in range(depth)])
    return loss_cols, dx, dmod, lgrads, stacked


def _place():
    return lax.axis_index("x"), lax.axis_index("y"), lax.axis_index("c")


def _all_gather8(name, block):
    m_per, n = block.shape

    def body(x_ref, out_ref, send_sems, recv_sems, local_sem):
        x, y, c = _place()
        me, sibling = (x, y, c), (x, y, 1 - c)
        chips = [(1 - x, y), (x, 1 - y), (1 - x, 1 - y)]

        def rows(px, py, pc):
            return out_ref.at[pl.ds(pl.multiple_of((4 * px + 2 * py + pc) * m_per, 8), m_per), :]

        def copy(k, blk, to, src=None):
            return pltpu.make_async_remote_copy(src_ref=rows(*blk) if src is None else src, dst_ref=rows(*blk),
                                                send_sem=send_sems.at[k], recv_sem=recv_sems.at[k],
                                                device_id=to, device_id_type=MESH)

        mine = pltpu.make_async_copy(x_ref, rows(*me), local_sem)
        mine.start()
        first = [copy(0, me, sibling, src=x_ref)] + [copy(1 + j, me, (*chip, c), src=x_ref) for j, chip in enumerate(chips)]
        for cp in first:
            cp.start()
        passed = [copy(4 + j, (*chip, c), sibling) for j, chip in enumerate(chips)]
        for j, chip in enumerate(chips):
            copy(1 + j, (*chip, c), me).wait_recv()
            passed[j].start()
        copy(0, sibling, me).wait_recv()
        for j, chip in enumerate(chips):
            copy(4 + j, (*chip, 1 - c), me).wait_recv()
        for cp in first + passed:
            cp.wait_send()
        mine.wait()

    return pl.pallas_call(
        body, name=name, out_shape=jax.ShapeDtypeStruct((8 * m_per, n), block.dtype),
        in_specs=[pl.BlockSpec(memory_space=pltpu.VMEM)], out_specs=pl.BlockSpec(memory_space=pltpu.VMEM),
        scratch_shapes=[pltpu.SemaphoreType.DMA((7,)), pltpu.SemaphoreType.DMA((7,)), pltpu.SemaphoreType.DMA],
        compiler_params=_params(),
    )(block)


def _other_chips(x, y):
    return [(1 - x, y), (x, 1 - y), (1 - x, 1 - y)]


def _gather_weights(whole, by_rows):
    n = len(whole)

    def body(*refs):
        dst = refs[n:2 * n]
        ici_send, ici_recv, d2d_send, d2d_recv = refs[2 * n:]
        x, y, c = _place()
        chips = _other_chips(x, y)

        def part(ref, k, px, py, pc):
            _, r, cols = whole[k].shape
            q = 2 * px + py
            if by_rows[k]:
                return ref[k].at[:, pl.ds(pl.multiple_of((2 * q + pc) * (r // 8), 16), r // 8), :]
            return ref[k].at[:, pl.ds(pl.multiple_of(pc * (r // 2), 16), r // 2),
                             pl.ds(pl.multiple_of(q * (cols // 4), LANES), cols // 4)]

        def ici(k, j, px, py, to):
            return pltpu.make_async_remote_copy(src_ref=part(dst, k, px, py, c), dst_ref=part(dst, k, px, py, c),
                                                send_sem=ici_send.at[k, j], recv_sem=ici_recv.at[k, j],
                                                device_id=(*to, c), device_id_type=MESH)

        def d2d(k, j, px, py, pc):
            return pltpu.make_async_remote_copy(src_ref=part(dst, k, px, py, pc), dst_ref=part(dst, k, px, py, pc),
                                                send_sem=d2d_send.at[k, j], recv_sem=d2d_recv.at[k, j],
                                                device_id=(x, y, 1 - c), device_id_type=MESH)

        for k in range(n):
            for j, chip in enumerate(chips):
                ici(k, j, x, y, chip).start()
        for k in range(n):
            for j, chip in enumerate(chips):
                ici(k, j, *chip, chip).wait_recv()
                d2d(k, j, *chip, c).start()
        for k in range(n):
            for j, chip in enumerate(chips):
                d2d(k, j, *chip, 1 - c).wait_recv()
        for k in range(n):
            for j, chip in enumerate(chips):
                ici(k, j, x, y, chip).wait_send()
                d2d(k, j, *chip, c).wait_send()

    any_spec = pl.BlockSpec(memory_space=pl.ANY)
    return pl.pallas_call(
        body, name="gather_weights", in_specs=[any_spec] * n, out_specs=[any_spec] * n,
        out_shape=[jax.ShapeDtypeStruct(a.shape, a.dtype) for a in whole], input_output_aliases={k: k for k in range(n)},
        scratch_shapes=[pltpu.SemaphoreType.DMA((n, 3))] * 4,
        compiler_params=_params(),
    )(*whole)


def _part_shape(shape, by_rows):
    l, r, c = shape
    return (l, r // 8, c) if by_rows else (l, r // 2, c // 4)


def _pair_exchange(grads, by_rows):
    n = len(grads)

    def body(*refs):
        src, dst = refs[:n], refs[n:2 * n]
        send_sems, recv_sems = refs[2 * n:]
        x, y, c = _place()

        def window(k, q, pc):
            _, hr, hc = _part_shape(grads[k].shape, by_rows[k])
            if by_rows[k]:
                return src[k].at[:, pl.ds(pl.multiple_of((2 * q + pc) * hr, 16), hr), :]
            return src[k].at[:, pl.ds(pl.multiple_of(pc * hr, 16), hr), pl.ds(q * hc, hc)]

        def copy(k, q, pc):
            return pltpu.make_async_remote_copy(src_ref=window(k, q, pc), dst_ref=dst[k].at[q], send_sem=send_sems.at[k, q],
                                                recv_sem=recv_sems.at[k, q], device_id=(x, y, 1 - c), device_id_type=MESH)

        for k in range(n):
            for q in range(4):
                copy(k, q, 1 - c).start()
        for k in range(n):
            for q in range(4):
                copy(k, q, c).wait_recv()
        for k in range(n):
            for q in range(4):
                copy(k, q, 1 - c).wait_send()

    any_spec = pl.BlockSpec(memory_space=pl.ANY)
    return pl.pallas_call(
        body, name="pair_exchange", in_specs=[any_spec] * n, out_specs=[any_spec] * n,
        out_shape=[jax.ShapeDtypeStruct((4, *_part_shape(g.shape, rows)), g.dtype) for g, rows in zip(grads, by_rows)],
        scratch_shapes=[pltpu.SemaphoreType.DMA((n, 4)), pltpu.SemaphoreType.DMA((n, 4))],
        compiler_params=_params(),
    )(*grads)


def _pair_sum(name, g, theirs, by_rows, c, chip):
    _, l, hr, hc = theirs.shape
    tr = _pick(hr, (256, 176, 128, 64, 32))

    def body(s_ref, g_ref, t_ref, p_ref, own_ref):
        v = (g_ref[...].astype(F32) + t_ref[0].astype(F32)).astype(BF16)
        p_ref[0] = v

        @pl.when(pl.program_id(2) == s_ref[1])
        def _():
            own_ref[0] = v

    if by_rows:
        g_spec = pl.BlockSpec((1, tr, hc), lambda li, i, q, s: (li, (2 * q + s[0]) * (hr // tr) + i, 0))
    else:
        g_spec = pl.BlockSpec((1, tr, hc), lambda li, i, q, s: (li, s[0] * (hr // tr) + i, q))
    slot = pl.BlockSpec((1, 1, tr, hc), lambda li, i, q, s: (q, li, i, 0))
    grid_spec = pltpu.PrefetchScalarGridSpec(
        num_scalar_prefetch=1, grid=(l, hr // tr, 4), in_specs=[g_spec, slot],
        out_specs=[slot, pl.BlockSpec((1, 1, tr, hc), lambda li, i, q, s: (s[1], li, i, 0))])
    return pl.pallas_call(
        body, name=name, grid_spec=grid_spec, out_shape=[jax.ShapeDtypeStruct(theirs.shape, BF16)] * 2,
        compiler_params=_params(dimension_semantics=("arbitrary", "arbitrary", "arbitrary")),
    )(jnp.stack([c, chip]).astype(jnp.int32), g, theirs)


def _chip_scatter(sums, landing):
    n = len(sums)

    def body(*refs):
        src, dst = refs[:n], refs[2 * n:3 * n]
        send_sems, recv_sems = refs[3 * n:]
        x, y, c = _place()
        mine = 2 * x + y

        def copy(k, j, src_slot, dst_slot, to):
            return pltpu.make_async_remote_copy(src_ref=src[k].at[src_slot], dst_ref=dst[k].at[dst_slot],
                                                send_sem=send_sems.at[k, j], recv_sem=recv_sems.at[k, j],
                                                device_id=(*to, c), device_id_type=MESH)

        chips = _other_chips(x, y)
        for k in range(n):
            for j, (px, py) in enumerate(chips):
                copy(k, j, 2 * px + py, mine, (px, py)).start()
        for k in range(n):
            for j, (px, py) in enumerate(chips):
                copy(k, j, mine, 2 * px + py, (px, py)).wait_recv()
        for k in range(n):
            for j, (px, py) in enumerate(chips):
                copy(k, j, 2 * px + py, mine, (px, py)).wait_send()

    any_spec = pl.BlockSpec(memory_space=pl.ANY)
    return pl.pallas_call(
        body, name="chip_scatter", in_specs=[any_spec] * (2 * n), out_specs=[any_spec] * n,
        out_shape=[jax.ShapeDtypeStruct(a.shape, a.dtype) for a in landing],
        input_output_aliases={n + k: k for k in range(n)},
        scratch_shapes=[pltpu.SemaphoreType.DMA((n, 3)), pltpu.SemaphoreType.DMA((n, 3))],
        compiler_params=_params(),
    )(*sums, *landing)


def _sum_slots(name, parts, half=None):
    slots, l, r, c = parts.shape
    tr = _pick(r, (256, 176, 128, 64, 32, 8))

    def body(*refs):
        p_ref, o_ref = refs[-2:]
        acc = p_ref[0].astype(F32)
        for i in range(1, slots):
            acc = acc + p_ref[i].astype(F32)
        o_ref[...] = acc

    if half is None:
        return pl.pallas_call(
            body, name=name, grid=(l, r // tr), in_specs=[pl.BlockSpec((slots, 1, tr, c), lambda li, i: (0, li, i, 0))],
            out_specs=pl.BlockSpec((1, tr, c), lambda li, i: (li, i, 0)), out_shape=jax.ShapeDtypeStruct((l, r, c), F32),
            compiler_params=_params(dimension_semantics=("arbitrary", "arbitrary")),
        )(parts)
    grid_spec = pltpu.PrefetchScalarGridSpec(
        num_scalar_prefetch=1, grid=(l, r // tr),
        in_specs=[pl.BlockSpec((slots, 1, tr, c), lambda li, i, h: (0, li, i, 0))],
        out_specs=pl.BlockSpec((1, tr, c), lambda li, i, h: (li, h[0] * (r // tr) + i, 0)))
    return pl.pallas_call(
        body, name=name, grid_spec=grid_spec, out_shape=jax.ShapeDtypeStruct((l, 2 * r, c), F32),
        compiler_params=_params(dimension_semantics=("arbitrary", "arbitrary")),
    )(jnp.reshape(half, (1,)).astype(jnp.int32), parts)


def _swap_halves(blocks):
    n = len(blocks)

    def body(*refs):
        src, dst = refs[:n], refs[n:2 * n]
        send_sems, recv_sems = refs[2 * n:]
        x, y, c = _place()

        def half(ref, k, pc):
            r = blocks[k].shape[1] // 2
            return ref[k].at[:, pl.ds(pl.multiple_of(pc * r, 8), r), :]

        def copy(k, pc):
            return pltpu.make_async_remote_copy(src_ref=half(src, k, pc), dst_ref=half(dst, k, pc), send_sem=send_sems.at[k],
                                                recv_sem=recv_sems.at[k], device_id=(x, y, 1 - c), device_id_type=MESH)

        for k in range(n):
            copy(k, c).start()
        for k in range(n):
            copy(k, 1 - c).wait_recv()
        for k in range(n):
            copy(k, c).wait_send()

    any_spec = pl.BlockSpec(memory_space=pl.ANY)
    return pl.pallas_call(
        body, name="swap_halves", in_specs=[any_spec] * n, out_specs=[any_spec] * n,
        out_shape=[jax.ShapeDtypeStruct(b.shape, b.dtype) for b in blocks], input_output_aliases={k: k for k in range(n)},
        scratch_shapes=[pltpu.SemaphoreType.DMA((n,)), pltpu.SemaphoreType.DMA((n,))],
        compiler_params=_params(),
    )(*blocks)


def _adamw(name, w, g, m, v):
    shape = w.shape
    cols = shape[-1]
    flat = lambda a: a.reshape(-1, cols)
    rows = w.size // cols
    tr = _pick(rows, [r for r in (512, 256, 128, 64, 32, 16, 8) if r * cols <= 256 * 1024]) if rows % 8 == 0 else rows

    def body(w_ref, g_ref, m_ref, v_ref, go_ref, d_ref, nm_ref, nv_ref):
        gg = g_ref[...]
        go_ref[...] = gg
        nm = ADAM_B1 * m_ref[...] + (1.0 - ADAM_B1) * gg
        nv = ADAM_B2 * v_ref[...] + (1.0 - ADAM_B2) * (gg * gg)
        m_hat = nm / (1.0 - ADAM_B1 ** ADAM_STEP)
        v_hat = nv / (1.0 - ADAM_B2 ** ADAM_STEP)
        d_ref[...] = -ADAM_LR * (m_hat / (jnp.sqrt(v_hat) + ADAM_EPS) + ADAM_WD * w_ref[...])
        nm_ref[...] = nm
        nv_ref[...] = nv

    spec = pl.BlockSpec((tr, cols), lambda i: (i, 0))
    out = pl.pallas_call(
        body, name=name, grid=(rows // tr,), in_specs=[spec] * 4, out_specs=[spec] * 4,
        out_shape=[jax.ShapeDtypeStruct((rows, cols), F32)] * 4,
        compiler_params=_params(dimension_semantics=("arbitrary",)),
    )(flat(w), flat(g), flat(m), flat(v))
    return tuple(o.reshape(shape) for o in out)


WEIGHTS = ["w_ada", "b_ada", "w_in", "w_sb_up", "ssm_a_re", "ssm_a_im", "ssm_log_dt", "ssm_b_re", "ssm_b_im", "ssm_c_re",
           "ssm_c_im", "ssm_d", "w_glu", "b_glu", "w_ssm_up", "w_out", "ln1_g", "ln1_b", "w_ffn_in", "w_ffn_out", "ln2_g",
           "ln2_b"]
COL_SPLIT = ["w_in", "w_sb_up", "w_ssm_up", "w_ffn_in"]
ROW_SPLIT = ["w_glu", "w_out", "w_ffn_out"]
SMALL = ["ssm_a_re", "ssm_a_im", "ssm_log_dt", "ssm_b_re", "ssm_b_im", "ssm_c_re", "ssm_c_im", "ssm_d", "b_glu", "ln1_g",
         "ln1_b", "ln2_g", "ln2_b"]
SLAB_COLS = 1024


def _cast_into_whole(name, w, by_rows, chip):
    l, r, cols = w.shape
    tr = _pick(r, (512, 256, 128, 64, 16))

    def body(q_ref, w_ref, o_ref):
        o_ref[...] = w_ref[...].astype(BF16)

    if by_rows:
        out_map, shape = (lambda li, i, q: (li, q[0] * (r // tr) + i, 0)), (l, 4 * r, cols)
    else:
        out_map, shape = (lambda li, i, q: (li, i, q[0])), (l, r, 4 * cols)
    grid_spec = pltpu.PrefetchScalarGridSpec(
        num_scalar_prefetch=1, grid=(l, r // tr), in_specs=[pl.BlockSpec((1, tr, cols), lambda li, i, q: (li, i, 0))],
        out_specs=pl.BlockSpec((1, tr, cols), out_map))
    return pl.pallas_call(body, name=name, grid_spec=grid_spec, out_shape=jax.ShapeDtypeStruct(shape, BF16),
                          compiler_params=_params(dimension_semantics=("arbitrary", "arbitrary")),
                          )(jnp.reshape(chip, (1,)).astype(jnp.int32), w)


def _silu_rows(name, c):
    def body(c_ref, o_ref):
        v = c_ref[...]
        o_ref[...] = v * jax.nn.sigmoid(v)

    return pl.pallas_call(body, name=name, out_shape=jax.ShapeDtypeStruct(c.shape, F32), compiler_params=_params())(c)


def _pad_rows(v, mult=8):
    flat = v.reshape(-1)
    per = mult * SLAB_COLS
    total = -(-flat.size // per) * per
    return jnp.pad(flat, (0, total - flat.size)).reshape(-1, SLAB_COLS)


def kernel(x, c, w_ada, b_ada, w_in, w_sb_up, ssm_a_re, ssm_a_im, ssm_log_dt, ssm_b_re, ssm_b_im, ssm_c_re, ssm_c_im, ssm_d, w_glu, b_glu, w_ssm_up, w_out, ln1_g, ln1_b, w_ffn_in, w_ffn_out, ln2_g, ln2_b, loss_target, m_w_ada, m_b_ada, m_w_in, m_w_sb_up, m_ssm_a_re, m_ssm_a_im, m_ssm_log_dt, m_ssm_b_re, m_ssm_b_im, m_ssm_c_re, m_ssm_c_im, m_ssm_d, m_w_glu, m_b_glu, m_w_ssm_up, m_w_out, m_ln1_g, m_ln1_b, m_w_ffn_in, m_w_ffn_out, m_ln2_g, m_ln2_b, v_w_ada, v_b_ada, v_w_in, v_w_sb_up, v_ssm_a_re, v_ssm_a_im, v_ssm_log_dt, v_ssm_b_re, v_ssm_b_im, v_ssm_c_re, v_ssm_c_im, v_ssm_d, v_w_glu, v_b_glu, v_w_ssm_up, v_w_out, v_ln1_g, v_ln1_b, v_w_ffn_in, v_w_ffn_out, v_ln2_g, v_ln2_b):
    args = dict(locals())
    w = {n: args[n] for n in WEIGHTS}
    mom = {n: args["m_" + n] for n in WEIGHTS}
    var = {n: args["v_" + n] for n in WEIGHTS}
    depth, d = w_ada.shape[0], x.shape[-1]
    xi, yi, ci = _place()
    me, chip = 4 * xi + 2 * yi + ci, 2 * xi + yi
    ada_cols = w_ada.shape[2]

    big = COL_SPLIT + ROW_SPLIT
    by_rows = [n in ROW_SPLIT for n in big]
    full = dict(zip(big, _gather_weights([_cast_into_whole(f"cast_{n}", w[n], n in ROW_SPLIT, chip) for n in big], by_rows)))

    c_all = _all_gather8("gather_c", jnp.pad(c, ((0, 7), (0, 0))))[::8]
    c_act = _silu_rows("silu_c", c_all)
    b_cols = lax.dynamic_slice_in_dim(b_ada, chip * ada_cols, ada_cols, axis=1)
    mod_part = jnp.concatenate([_small_mm(f"mod_{l}", c_act, w_ada[l], "nn") + b_cols[l][None] for l in range(depth)], axis=0)
    mod_all = _all_gather8("gather_mod", mod_part).reshape(4, 2, depth, 8, ada_cols)[:, 0]
    mod_mine = lax.dynamic_index_in_dim(mod_all, me, axis=2, keepdims=False)
    mod = mod_mine.transpose(1, 0, 2).reshape(depth, 6, d)

    layer_w = [{**{n: (full[n], l) for n in big}, **{n: w[n][l] for n in SMALL}} for l in range(depth)]
    loss_cols, dx, dmods, lgrads, stacked = _local_step(x[0], loss_target[0], mod, layer_w)
    loss = lax.psum(jnp.sum(loss_cols), ("x", "y", "c"))
    grad_x = dx[None]

    theirs = _pair_exchange([stacked[n] for n in big], by_rows)
    pairs = [_pair_sum(f"pair_{n}", stacked[n], t, n in ROW_SPLIT, ci, chip) for n, t in zip(big, theirs)]
    landed = _chip_scatter([p[0] for p in pairs], [p[1] for p in pairs])
    halves = [_sum_slots(f"sum_{n}", p, half=ci) for n, p in zip(big, landed)]
    grad = dict(zip(big, _swap_halves(halves)))

    pieces = [dmods] + [jnp.stack([lgrads[l][n] for l in range(depth)]) for n in SMALL]
    slab = _pad_rows(jnp.concatenate([p.reshape(-1) for p in pieces]))
    slabs = _all_gather8("gather_small", slab).reshape(8, 1, *slab.shape)
    total = _sum_slots("sum_small", slabs)[0].reshape(-1)
    at = dmods.size
    grad["b_ada"] = total[:at].reshape(depth, 6 * d)
    for n, p in zip(SMALL, pieces[1:]):
        grad[n] = total[at:at + p.size].reshape(p.shape)
        at += p.size
    dmod_all = slabs.reshape(8, -1)[:, :dmods.size].reshape(8, depth, 4, ada_cols)
    dmod_cols = lax.dynamic_index_in_dim(dmod_all, chip, axis=2, keepdims=False)
    grad["w_ada"] = jnp.stack([_small_mm(f"dw_ada_{l}", c_act, dmod_cols[:, l], "tn") for l in range(depth)])

    delta, new_m, new_v = {}, {}, {}
    for n in WEIGHTS:
        grad[n], delta[n], new_m[n], new_v[n] = _adamw(f"adamw_{n}", w[n], grad[n], mom[n], var[n])
    return (loss, grad_x, *[grad[n] for n in WEIGHTS], *[delta[n] for n in WEIGHTS], *[new_m[n] for n in WEIGHTS],
            *[new_v[n] for n in WEIGHTS])
```

```python
import functools
import math

import jax
import jax.numpy as jnp
from jax import lax
from jax.experimental import pallas as pl
from jax.experimental.pallas import tpu as pltpu

F32 = jnp.float32
BF16 = jnp.bfloat16
MESH = pl.DeviceIdType.MESH

LANES = 128
HEAD_DIM = 64
SB_WIDTH = 512
ATT_TILE = 256
SSM_GROUPS, SSM_STATE, SSM_GROUP = 32, 64, 16
N_STATE = SSM_GROUPS * SSM_STATE
SSM_BLOCKS = SSM_GROUPS * SSM_GROUP // LANES
U_OFFSET = 3 * 512
LN_EPS = 1e-5
DEPTH = 2
ALPHA = (2 * DEPTH) ** 0.25
ADAM_LR, ADAM_B1, ADAM_B2, ADAM_EPS, ADAM_WD, ADAM_STEP = 0.001, 0.9, 0.999, 1e-08, 0.01, 10
VMEM_LIMIT = 56 * 1024 * 1024
GELU_K = math.sqrt(2.0 / math.pi)
GELU_C = 0.044715


def _params(**kw):
    return pltpu.CompilerParams(vmem_limit_bytes=VMEM_LIMIT, **kw)


def _pick(n, prefs):
    for p in prefs:
        if n % p == 0:
            return p
    return n


ROWWISE_WIDE = 7 * 1024


def _rowwise(name, fn, rows, vecs, outs, sums=(), tm=None):
    s = rows[0][0].shape[0]
    wide = sum(w for _, _, w in rows) + sum(w for w, _ in outs) > ROWWISE_WIDE
    tm = tm or _pick(s, (256, 128, 64, 8) if wide else (512, 256, 128, 64, 8))
    nin, no, ns = len(rows) + len(vecs), len(outs), len(sums)

    def body(*refs):
        res = fn(*[r[...].astype(F32) for r in refs[:nin]])
        res = res if isinstance(res, tuple) else (res,)
        for r, v in zip(refs[nin:nin + no], res[:no]):
            r[...] = v.astype(r.dtype)
        if ns:
            @pl.when(pl.program_id(0) == 0)
            def _():
                for r in refs[nin + no:]:
                    r[...] = jnp.zeros_like(r)
            for r, v in zip(refs[nin + no:], res[no:]):
                r[...] += v

    in_specs = [pl.BlockSpec((tm, w), lambda i, cb=cb: (i, cb)) for _, cb, w in rows]
    in_specs += [pl.BlockSpec(v.shape, lambda i: (0, 0)) for v in vecs]
    out_specs = [pl.BlockSpec((tm, w), lambda i: (i, 0)) for w, _ in outs]
    out_specs += [pl.BlockSpec((1, w), lambda i: (0, 0)) for w in sums]
    out_shape = [jax.ShapeDtypeStruct((s, w), dt) for w, dt in outs]
    out_shape += [jax.ShapeDtypeStruct((1, w), F32) for w in sums]
    res = pl.pallas_call(
        body, name=name, grid=(s // tm,), in_specs=in_specs, out_specs=out_specs, out_shape=out_shape,
        compiler_params=_params(dimension_semantics=("arbitrary",)),
    )(*[a for a, _, _ in rows], *vecs)
    return res[0] if len(res) == 1 else tuple(res)


MM_TILES = (1408, 1024, 512, 256, 128)


def _slab_spec(block, index, slab):
    if slab is None:
        return pl.BlockSpec(block, index)
    return pl.BlockSpec((None, *block), lambda *g: (slab, *index(*g)))


def _mm(name, a, b, mode, out_dtype=F32, into=None):
    b, b_slab = b if isinstance(b, tuple) else (b, None)
    if mode == "nn":
        m, k, n = a.shape[0], a.shape[1], b.shape[-1]
    elif mode == "nt":
        m, k, n = a.shape[0], a.shape[1], b.shape[-2]
    else:
        k, m, n = a.shape[0], a.shape[1], b.shape[-1]
    tm = _pick(m, MM_TILES if mode == "tn" else (2048,) + MM_TILES[1:])
    tn = _pick(n, MM_TILES)
    tk = _pick(k, (2048,) + MM_TILES if mode == "tn" else MM_TILES)
    nk = k // tk
    dims = {"nn": ((1,), (0,)), "nt": ((1,), (1,)), "tn": ((0,), (0,))}[mode]

    def body(a_ref, b_ref, *rest):
        o_ref = rest[-2] if nk > 1 else rest[-1]
        prod = lax.dot_general(a_ref[...].astype(BF16), b_ref[...].astype(BF16), (dims, ((), ())),
                               preferred_element_type=F32)
        if nk == 1:
            o_ref[...] = prod.astype(o_ref.dtype)
            return
        acc_ref = rest[-1]
        kk = pl.program_id(2)

        @pl.when(kk == 0)
        def _():
            acc_ref[...] = prod

        @pl.when(kk > 0)
        def _():
            acc_ref[...] += prod

        @pl.when(kk == nk - 1)
        def _():
            o_ref[...] = acc_ref[...].astype(o_ref.dtype)

    if mode == "tn":
        a_spec = pl.BlockSpec((tk, tm), lambda i, j, kk: (kk, i))
    else:
        a_spec = pl.BlockSpec((tm, tk), lambda i, j, kk: (i, kk))
    b_block, b_index = ((tn, tk), lambda i, j, kk: (j, kk)) if mode == "nt" else ((tk, tn), lambda i, j, kk: (kk, j))
    b_spec = _slab_spec(b_block, b_index, b_slab)
    in_specs, operands, aliases = [a_spec, b_spec], [a, b], {}
    if into is None:
        out_spec = pl.BlockSpec((tm, tn), lambda i, j, kk: (i, j))
        out_shape = jax.ShapeDtypeStruct((m, n), out_dtype)
    else:
        buf, slab, count = into
        out_spec = pl.BlockSpec((None, tm, tn), lambda i, j, kk: (slab, i, j))
        out_shape = jax.ShapeDtypeStruct((count, m, n), out_dtype)
        if buf is not None:
            in_specs.append(pl.BlockSpec(memory_space=pl.ANY))
            operands.append(buf)
            aliases = {2: 0}
    return pl.pallas_call(
        body, name=name, grid=(m // tm, n // tn, nk), in_specs=in_specs, out_specs=out_spec, out_shape=out_shape,
        scratch_shapes=[pltpu.VMEM((tm, tn), F32)] if nk > 1 else [], input_output_aliases=aliases,
        compiler_params=_params(dimension_semantics=("arbitrary", "arbitrary", "arbitrary")),
    )(*operands)


def _ffn_in(name, h, w):
    w, slab = w if isinstance(w, tuple) else (w, None)
    s, d = h.shape
    f = w.shape[-1] // 2
    tm, tn = _pick(s, MM_TILES[2:]), _pick(f, MM_TILES)

    def body(h_ref, wg_ref, wu_ref, g_ref, u_ref, a_ref):
        hb = h_ref[...]
        g, u = _nn(hb, wg_ref[...]), _nn(hb, wu_ref[...])
        g_ref[...] = g.astype(BF16)
        u_ref[...] = u.astype(BF16)
        a_ref[...] = (g * jax.nn.sigmoid(g) * u).astype(BF16)

    out = pl.BlockSpec((tm, tn), lambda i, j: (i, j))
    return pl.pallas_call(
        body, name=name, grid=(s // tm, f // tn),
        in_specs=[pl.BlockSpec((tm, d), lambda i, j: (i, 0)), _slab_spec((d, tn), lambda i, j: (0, j), slab),
                  _slab_spec((d, tn), lambda i, j: (0, f // tn + j), slab)],
        out_specs=[out, out, out], out_shape=[jax.ShapeDtypeStruct((s, f), BF16)] * 3,
        compiler_params=_params(dimension_semantics=("arbitrary", "arbitrary")),
    )(h, w, w)


def _small_mm(name, a, b, mode):
    dims = {"nn": ((1,), (0,)), "tn": ((0,), (0,))}[mode]
    m = a.shape[0] if mode == "nn" else a.shape[1]

    def body(a_ref, b_ref, o_ref):
        o_ref[...] = lax.dot_general(a_ref[...], b_ref[...], (dims, ((), ())), precision=lax.Precision.HIGHEST,
                                     preferred_element_type=F32)

    return pl.pallas_call(body, name=name, out_shape=jax.ShapeDtypeStruct((m, b.shape[1]), F32),
                          compiler_params=_params())(a, b)


def _norm(x):
    mu = jnp.mean(x, axis=-1, keepdims=True)
    xc = x - mu
    rstd = lax.rsqrt(jnp.mean(xc * xc, axis=-1, keepdims=True) + LN_EPS)
    return xc * rstd, rstd


def _norm_bwd(dn, n, rstd):
    return rstd * (dn - jnp.mean(dn, axis=-1, keepdims=True) - n * jnp.mean(dn * n, axis=-1, keepdims=True))


def _colsum(v):
    return jnp.sum(v, axis=0, keepdims=True)


def _gelu(x):
    return 0.5 * x * (1.0 + jnp.tanh(GELU_K * (x + GELU_C * x * x * x)))


def _gelu_grad(x):
    t = jnp.tanh(GELU_K * (x + GELU_C * x * x * x))
    return 0.5 * (1.0 + t) + 0.5 * x * (1.0 - t * t) * GELU_K * (1.0 + 3.0 * GELU_C * x * x)


def _log_sigmoid_parts(z):
    lb = jnp.minimum(z, 0.0) - jnp.log(1.0 + jnp.exp(-jnp.abs(z)))
    return lb, lb - z


def _kv_transposed(proj, t):
    s = proj.shape[0]
    nb, nhp = s // t, SB_WIDTH // LANES

    def body(k_ref, v_ref, kt_ref, vt_ref):
        k, v = k_ref[...].astype(F32), v_ref[...].astype(F32)
        for hp in range(nhp):
            kt_ref[hp, 0] = k[:, hp * LANES:(hp + 1) * LANES].T.astype(BF16)
            vt_ref[hp, 0] = v[:, hp * LANES:(hp + 1) * LANES].T.astype(BF16)

    col = lambda cb: pl.BlockSpec((t, SB_WIDTH), lambda i, cb=cb: (i, cb))
    t_out = pl.BlockSpec((nhp, 1, LANES, t), lambda i: (0, i, 0, 0))
    return pl.pallas_call(
        body, name="kv_transposed", grid=(nb,), in_specs=[col(1), col(2)], out_specs=[t_out, t_out],
        out_shape=[jax.ShapeDtypeStruct((nhp, nb, LANES, t), BF16)] * 2,
        compiler_params=_params(dimension_semantics=("arbitrary",)),
    )(proj, proj)


def _tile_masks(t):
    row = lax.broadcasted_iota(jnp.int32, (t, t), 0)
    col = lax.broadcasted_iota(jnp.int32, (t, t), 1)
    return row, col


DEAD_LOG_WEIGHT = -110.0


def _walk_down(i, tiles, state, alive):
    st = lax.cond(i == 0, lambda s_: tiles([i], s_, [True]), lambda s_: tiles([i, i - 1], s_, [True, False]), state)
    n = jnp.maximum(i - 1, 0)

    def pair(c):
        nxt = tiles([i - 2 - 2 * c[0], i - 3 - 2 * c[0]], c[1], [False, False])
        return c[0] + 1, nxt, alive(nxt)

    p, st, go = lax.while_loop(lambda c: (c[0] < n // 2) & c[2], pair, (jnp.int32(0), st, alive(st)))
    return lax.cond((n % 2 == 1) & (p == n // 2) & go, lambda s_: tiles([0], s_, [False]), lambda s_: s_, st)


def _walk_up(i, first, tiles, state):
    n = jnp.maximum(i - 1 - first, 0)
    st = lax.fori_loop(0, n // 2, lambda p, s_: tiles([first + 2 * p, first + 2 * p + 1], s_, [False, False]), state)
    st = lax.cond(n % 2 == 1, lambda s_: tiles([i - 2], s_, [False]), lambda s_: s_, st)
    return lax.cond(i == 0, lambda s_: tiles([i], s_, [True]), lambda s_: tiles([i - 1, i], s_, [False, True]), st)


def _nt(a, b):
    return lax.dot_general(a, b, (((1,), (1,)), ((), ())), preferred_element_type=F32)


def _nn(a, b):
    return jnp.dot(a, b, preferred_element_type=F32)


def _attn_fwd(proj, vt3, t):
    s = proj.shape[0]
    nb, nhp = s // t, SB_WIDTH // LANES

    def body(q_ref, k_ref, vt_ref, o_ref, car_ref):
        i = pl.program_id(1)
        q2 = q_ref[...] * (1.0 / math.sqrt(HEAD_DIM))
        lane_q = lax.broadcasted_iota(jnp.int32, q2.shape, 1)
        row, col = _tile_masks(t)
        later = (col > row).astype(BF16)
        valid = row < col
        orow = lax.broadcasted_iota(jnp.int32, (LANES, t), 0)
        car_ref[...] = jnp.full(car_ref.shape, 2.0 * DEAD_LOG_WEIGHT, F32)
        qh = [jnp.where((lane_q < HEAD_DIM) == (hh == 0), q2, jnp.zeros_like(q2)) for hh in range(2)]

        def tiles(js, state, diagonal):
            chains = [(n, hh) for n in range(len(js)) for hh in range(2)]
            kb = [k_ref[pl.ds(pl.multiple_of(j * t, t), t), :] for j in js]
            z = {ch: _nt(kb[ch[0]], qh[ch[1]]) for ch in chains}
            lb, aft, csum = {}, {}, {}
            for ch in chains:
                lb[ch], l1m = _log_sigmoid_parts(z[ch])
                if diagonal[ch[0]]:
                    l1m = jnp.where(valid, l1m, 0.0)
                aft[ch] = _nn(later, l1m.astype(BF16))
                csum[ch] = _colsum(l1m)
            state = list(state)
            for ch in chains:
                n, hh = ch
                c_after, acc = state[hh]
                w = jnp.exp(lb[ch] + aft[ch] + c_after)
                if diagonal[ch[0]]:
                    w = jnp.where(valid, w, 0.0)
                car_ref[hh, pl.ds(js[n], 1), :] = c_after
                state[hh] = (c_after + csum[ch], acc + _nn(vt_ref[0, js[n]], w.astype(BF16)))
            return tuple(state)

        def alive(state):
            return jnp.max(jnp.maximum(state[0][0], state[1][0])) >= DEAD_LOG_WEIGHT

        zero = (jnp.zeros((1, t), F32), jnp.zeros((LANES, t), F32))
        (_, acc0), (_, acc1) = _walk_down(i, tiles, (zero, zero), alive)
        o_ref[...] = jnp.where(orow < HEAD_DIM, acc0, acc1).T.astype(o_ref.dtype)

    return pl.pallas_call(
        body, name="attn_fwd", grid=(nhp, nb),
        in_specs=[pl.BlockSpec((t, LANES), lambda hp, i: (i, hp)),
                  pl.BlockSpec((s, LANES), lambda hp, i: (0, nhp + hp)),
                  pl.BlockSpec((1, nb, LANES, t), lambda hp, i: (hp, 0, 0, 0))],
        out_specs=[pl.BlockSpec((t, LANES), lambda hp, i: (i, hp)),
                   pl.BlockSpec((2, nb, t), lambda hp, i: (hp, 0, i))],
        out_shape=[jax.ShapeDtypeStruct((s, nhp * LANES), BF16), jax.ShapeDtypeStruct((2 * nhp, nb, s), F32)],
        compiler_params=_params(dimension_semantics=("arbitrary", "arbitrary")),
    )(proj, proj, vt3)


def _attn_bwd(proj, do, kt3, car, t):
    s = proj.shape[0]
    nb, nhp = s // t, SB_WIDTH // LANES

    def body(q_ref, do_ref, k_ref, v_ref, kt_ref, car_ref, dq_ref, dk_ref, dv_ref):
        i = pl.program_id(1)

        @pl.when(i == 0)
        def _():
            dk_ref[...] = jnp.zeros_like(dk_ref)
            dv_ref[...] = jnp.zeros_like(dv_ref)

        q2, do2 = q_ref[...] * (1.0 / math.sqrt(HEAD_DIM)), do_ref[...]
        lane_q = lax.broadcasted_iota(jnp.int32, q2.shape, 1)
        row, col = _tile_masks(t)
        later = (col > row).astype(BF16)
        earlier = (col < row).astype(BF16)
        valid = row < col
        orow = lax.broadcasted_iota(jnp.int32, (LANES, t), 0)
        head = [(lane_q < HEAD_DIM) == (hh == 0) for hh in range(2)]
        qh = [jnp.where(hm, q2, jnp.zeros_like(q2)) for hm in head]
        doh = [jnp.where(hm, do2, jnp.zeros_like(do2)) for hm in head]

        def tiles(js, state, diagonal):
            chains = [(n, hh) for n in range(len(js)) for hh in range(2)]
            rows = [pl.ds(pl.multiple_of(j * t, t), t) for j in js]
            kb = [k_ref[r, :] for r in rows]
            vb = [v_ref[r, :] for r in rows]
            z = {ch: _nt(kb[ch[0]], qh[ch[1]]) for ch in chains}
            dw = {ch: _nt(vb[ch[0]], doh[ch[1]]) for ch in chains}
            lb, beta, aft = {}, {}, {}
            for ch in chains:
                lb[ch], l1m = _log_sigmoid_parts(z[ch])
                beta[ch] = jnp.exp(lb[ch])
                if diagonal[ch[0]]:
                    l1m = jnp.where(valid, l1m, 0.0)
                aft[ch] = _nn(later, l1m.astype(BF16))
            w, g, gsum, g_in = {}, {}, {}, {}
            for ch in chains:
                n, hh = ch
                w[ch] = jnp.exp(lb[ch] + aft[ch] + car_ref[hh, pl.ds(js[n], 1), :])
                if diagonal[ch[0]]:
                    w[ch] = jnp.where(valid, w[ch], 0.0)
                g[ch] = dw[ch] * w[ch]
                g_in[ch] = _nn(earlier, g[ch].astype(BF16))
                gsum[ch] = _colsum(g[ch])
            state = list(state)
            dk_t, dv_t = [None] * len(js), [None] * len(js)
            for ch in chains:
                n, hh = ch
                c_g, dqt = state[hh]
                dz = g[ch] - beta[ch] * (g[ch] + g_in[ch] + c_g)
                if diagonal[ch[0]]:
                    dz = jnp.where(valid, dz, 0.0)
                dzb, wb = dz.astype(BF16), w[ch].astype(BF16)
                dk_h, dv_h = _nn(dzb, qh[hh]), _nn(wb, doh[hh])
                dk_t[n] = dk_h if dk_t[n] is None else dk_t[n] + dk_h
                dv_t[n] = dv_h if dv_t[n] is None else dv_t[n] + dv_h
                state[hh] = (c_g + gsum[ch], dqt + _nn(kt_ref[0, js[n]], dzb))
            for n in range(len(js)):
                dk_ref[rows[n], :] += dk_t[n]
                dv_ref[rows[n], :] += dv_t[n]
            return tuple(state)

        reach = jnp.max(jnp.max(car_ref[...], axis=2, keepdims=True), axis=0)
        dead = (reach < DEAD_LOG_WEIGHT) & (lax.broadcasted_iota(jnp.int32, reach.shape, 0) < i)
        first = jnp.sum(jnp.where(dead, 1.0, 0.0)).astype(jnp.int32)
        zero = (jnp.zeros((1, t), F32), jnp.zeros((LANES, t), F32))
        (_, dq0), (_, dq1) = _walk_up(i, first, tiles, (zero, zero))
        dq_ref[...] = jnp.where(orow < HEAD_DIM, dq0, dq1).T

    tile_spec = pl.BlockSpec((t, LANES), lambda hp, i: (i, hp))
    whole = pl.BlockSpec((s, LANES), lambda hp, i: (0, hp))
    return pl.pallas_call(
        body, name="attn_bwd", grid=(nhp, nb),
        in_specs=[tile_spec, tile_spec, pl.BlockSpec((s, LANES), lambda hp, i: (0, nhp + hp)),
                  pl.BlockSpec((s, LANES), lambda hp, i: (0, 2 * nhp + hp)),
                  pl.BlockSpec((1, nb, LANES, t), lambda hp, i: (hp, 0, 0, 0)),
                  pl.BlockSpec((2, nb, t), lambda hp, i: (hp, 0, i))],
        out_specs=[tile_spec, whole, whole],
        out_shape=[jax.ShapeDtypeStruct((s, nhp * LANES), F32)] * 3,
        compiler_params=_params(dimension_semantics=("arbitrary", "arbitrary")),
    )(proj, do, proj, proj, kt3, car)


SCAN_LANES = 1024
SCAN_ROWS = 8
S5_CHUNKS = 4


def _scan_chunks(v):
    n = v.shape[1] // (2 * LANES)
    return [(v[:, c * 2 * LANES:c * 2 * LANES + LANES], v[:, c * 2 * LANES + LANES:(c + 1) * 2 * LANES]) for c in range(n)]


def _scan_tables(lr, li, reverse):
    if reverse:
        li = -li
    row = lax.broadcasted_iota(jnp.int32, (SCAN_ROWS, LANES), 0)
    powers = [(lr, li)]
    for _ in range(SCAN_ROWS - 1):
        pr, pi = powers[-1]
        powers.append((pr * lr - pi * li, pr * li + pi * lr))
    levels = []
    for d in (1, 2, 4):
        keep = (row < SCAN_ROWS - d) if reverse else (row >= d)
        levels.append((SCAN_ROWS - d if reverse else d,
                       (jnp.where(keep, powers[d - 1][0], 0.0), jnp.where(keep, powers[d - 1][1], 0.0))))
    pr = pi = jnp.zeros((SCAN_ROWS, LANES), F32)
    for r in range(SCAN_ROWS):
        steps = SCAN_ROWS - r if reverse else r + 1
        pr = jnp.where(row == r, powers[steps - 1][0], pr)
        pi = jnp.where(row == r, powers[steps - 1][1], pi)
    return levels, (pr, pi)


def _s5_fwd(proj, u_off, bmat, cmat, lam):
    s, w = proj.shape[0], SSM_BLOCKS * bmat.shape[1]
    tt = _pick(s, (512, 256, 128, 8))
    nt = s // tt
    cin = bmat.shape[0] // SSM_BLOCKS
    chunk = tt // S5_CHUNKS

    def body(u_ref, b_ref, c_ref, lam_ref, h_ref, y_ref, x_ref, st_ref):
        @pl.when(pl.program_id(1) == 0)
        def _():
            st_ref[...] = jnp.zeros_like(st_ref)

        tables = [_scan_tables(lr, li, reverse=False) for lr, li in _scan_chunks(lam_ref[...])]

        def project(k):
            x_ref[k * chunk:(k + 1) * chunk, :] = _nn(u_ref[k * chunk:(k + 1) * chunk, :].astype(BF16), b_ref[...])

        def tile(r0, last):
            last, parts = list(last), []
            for c, (xr, xi) in enumerate(_scan_chunks(x_ref[r0:r0 + SCAN_ROWS, :])):
                levels, (pr, pi) = tables[c]
                for d, (ar, ai) in levels:
                    sr, si = pltpu.roll(xr, d, 0), pltpu.roll(xi, d, 0)
                    xr, xi = xr + ar * sr - ai * si, xi + ar * si + ai * sr
                br, bi = last[2 * c], last[2 * c + 1]
                hr = xr + pr * br - pi * bi
                hi = xi + pr * bi + pi * br
                last[2 * c], last[2 * c + 1] = hr[SCAN_ROWS - 1:], hi[SCAN_ROWS - 1:]
                parts += [hr, hi]
            h_ref[r0:r0 + SCAN_ROWS, :] = jnp.concatenate(parts, axis=1)
            return tuple(last)

        st = st_ref[0:1, :]
        last = tuple(st[:, c * LANES:(c + 1) * LANES] for c in range(SCAN_LANES // LANES))
        project(0)
        for k in range(S5_CHUNKS):
            if k + 1 < S5_CHUNKS:
                project(k + 1)
            for r0 in range(k * chunk, (k + 1) * chunk, SCAN_ROWS):
                last = tile(r0, last)
            y_ref[k * chunk:(k + 1) * chunk, :] = _nn(h_ref[k * chunk:(k + 1) * chunk, :].astype(BF16), c_ref[...])
        st_ref[0:1, :] = jnp.concatenate(last, axis=1)

    return pl.pallas_call(
        body, name="s5_fwd", grid=(SSM_BLOCKS, nt),
        in_specs=[pl.BlockSpec((tt, cin), lambda kb, i: (i, u_off // cin + kb)),
                  pl.BlockSpec((cin, SCAN_LANES), lambda kb, i: (kb, 0)),
                  pl.BlockSpec((SCAN_LANES, cin), lambda kb, i: (kb, 0)),
                  pl.BlockSpec((1, SCAN_LANES), lambda kb, i: (0, kb))],
        out_specs=[pl.BlockSpec((tt, SCAN_LANES), lambda kb, i: (i, kb)), pl.BlockSpec((tt, cin), lambda kb, i: (i, kb))],
        out_shape=[jax.ShapeDtypeStruct((s, w), F32), jax.ShapeDtypeStruct((s, bmat.shape[0]), F32)],
        scratch_shapes=[pltpu.VMEM((tt, SCAN_LANES), F32), pltpu.VMEM((SCAN_ROWS, SCAN_LANES), F32)],
        compiler_params=_params(dimension_semantics=("arbitrary", "arbitrary")),
    )(proj, bmat, cmat, lam)


def _s5_bwd(dy, h, proj, u_off, bmat, cmat, lam):
    s, w = h.shape
    tt = _pick(s, (512, 256, 128, 8))
    nt = s // tt
    cin = bmat.shape[0] // SSM_BLOCKS
    chunk = tt // S5_CHUNKS

    def body(dy_ref, h_ref, u_ref, b_ref, c_ref, lam_ref, du_ref, dlam_ref, db_ref, dc_ref, e_ref, a_ref, st_ref):
        @pl.when(pl.program_id(1) == 0)
        def _():
            st_ref[...] = jnp.zeros_like(st_ref)
            dlam_ref[...] = jnp.zeros_like(dlam_ref)
            db_ref[...] = jnp.zeros_like(db_ref)
            dc_ref[...] = jnp.zeros_like(dc_ref)

        tables = [_scan_tables(lr, li, reverse=True) for lr, li in _scan_chunks(lam_ref[...])]
        nch = len(tables)
        row = lax.broadcasted_iota(jnp.int32, (SCAN_ROWS, LANES), 0)
        rows_first = (((0,), (0,)), ((), ()))

        def project(k):
            e_ref[k * chunk:(k + 1) * chunk, :] = _nt(dy_ref[k * chunk:(k + 1) * chunk, :], c_ref[...])

        def finish(k):
            rows = slice(k * chunk, (k + 1) * chunk)
            adj = a_ref[rows, :].astype(BF16)
            du_ref[rows, :] = _nt(adj, b_ref[...])
            db_ref[...] += lax.dot_general(u_ref[rows, :].astype(BF16), adj, rows_first, preferred_element_type=F32)
            dc_ref[...] += lax.dot_general(h_ref[rows, :].astype(BF16), dy_ref[rows, :], rows_first, preferred_element_type=F32)

        def tile(r0, carry):
            e_c = _scan_chunks(e_ref[r0:r0 + SCAN_ROWS, :])
            h_c = _scan_chunks(h_ref[r0:r0 + SCAN_ROWS, :])
            carry, parts = list(carry), []
            for c in range(nch):
                (yr, yi), (hr, hi) = e_c[c], h_c[c]
                levels, (pr, pi) = tables[c]
                for shift, (lr, li) in levels:
                    sr, si = pltpu.roll(yr, shift, 0), pltpu.roll(yi, shift, 0)
                    yr, yi = yr + lr * sr - li * si, yi + lr * si + li * sr
                nr, ni, dr, di = carry[4 * c:4 * c + 4]
                ar = yr + pr * nr - pi * ni
                ai = yi + pr * ni + pi * nr
                nxr = jnp.where(row == SCAN_ROWS - 1, nr, pltpu.roll(ar, SCAN_ROWS - 1, 0))
                nxi = jnp.where(row == SCAN_ROWS - 1, ni, pltpu.roll(ai, SCAN_ROWS - 1, 0))
                carry[4 * c:4 * c + 4] = [ar[0:1], ai[0:1], dr + nxr * hr + nxi * hi, di + nxi * hr - nxr * hi]
                parts += [ar, ai]
            a_ref[r0:r0 + SCAN_ROWS, :] = jnp.concatenate(parts, axis=1)
            return tuple(carry)

        st, dl = st_ref[0:1, :], dlam_ref[...]
        init = []
        for c in range(nch):
            lo = c * 2 * LANES
            init += [st[:, lo:lo + LANES], st[:, lo + LANES:lo + 2 * LANES],
                     dl[:, lo:lo + LANES], dl[:, lo + LANES:lo + 2 * LANES]]
        fin = tuple(init)
        project(S5_CHUNKS - 1)
        for k in reversed(range(S5_CHUNKS)):
            if k > 0:
                project(k - 1)
            for r0 in reversed(range(k * chunk, (k + 1) * chunk, SCAN_ROWS)):
                fin = tile(r0, fin)
            finish(k)
        st_ref[0:1, :] = jnp.concatenate([fin[4 * c + q] for c in range(nch) for q in (0, 1)], axis=1)
        dlam_ref[...] = jnp.concatenate([fin[4 * c + q] for c in range(nch) for q in (2, 3)], axis=1)

        @pl.when(pl.program_id(1) == nt - 1)
        def _():
            dlam_ref[0:1, :] = jnp.sum(dlam_ref[...], axis=0, keepdims=True)

    def rev(width, col):
        return pl.BlockSpec((tt, width), lambda kb, i: (nt - 1 - i, col(kb)))

    return pl.pallas_call(
        body, name="s5_bwd", grid=(SSM_BLOCKS, nt),
        in_specs=[rev(cin, lambda kb: kb), rev(SCAN_LANES, lambda kb: kb), rev(cin, lambda kb: u_off // cin + kb),
                  pl.BlockSpec((cin, SCAN_LANES), lambda kb, i: (kb, 0)),
                  pl.BlockSpec((SCAN_LANES, cin), lambda kb, i: (kb, 0)),
                  pl.BlockSpec((1, SCAN_LANES), lambda kb, i: (0, kb))],
        out_specs=[rev(cin, lambda kb: kb), pl.BlockSpec((SCAN_ROWS, SCAN_LANES), lambda kb, i: (0, kb)),
                   pl.BlockSpec((cin, SCAN_LANES), lambda kb, i: (kb, 0)), pl.BlockSpec((SCAN_LANES, cin), lambda kb, i: (kb, 0))],
        out_shape=[jax.ShapeDtypeStruct((s, bmat.shape[0]), F32), jax.ShapeDtypeStruct((SCAN_ROWS, w), F32),
                   jax.ShapeDtypeStruct((bmat.shape[0], SCAN_LANES), F32), jax.ShapeDtypeStruct((w, cin), F32)],
        scratch_shapes=[pltpu.VMEM((tt, SCAN_LANES), F32), pltpu.VMEM((tt, SCAN_LANES), F32),
                        pltpu.VMEM((SCAN_ROWS, SCAN_LANES), F32)],
        compiler_params=_params(dimension_semantics=("arbitrary", "arbitrary")),
    )(dy, h, proj, bmat, cmat, lam)


def _ssm_params_fwd(a_re, a_im, log_dt, b_re, b_im):
    def body(ar_ref, ai_ref, ldt_ref, br_ref, bi_ref, lr_ref, li_ref, bbr_ref, bbi_ref):
        ar, ai, dt = ar_ref[...], ai_ref[...], jnp.exp(ldt_ref[...])
        mag = jnp.exp(ar * dt)
        lr, li = mag * jnp.cos(ai * dt), mag * jnp.sin(ai * dt)
        den = ar * ar + ai * ai
        cr = ((lr - 1.0) * ar + li * ai) / den
        ci = (li * ar - (lr - 1.0) * ai) / den
        br, bi = br_ref[...], bi_ref[...]
        lr_ref[...], li_ref[...] = lr, li
        bbr_ref[...] = cr * br - ci * bi
        bbi_ref[...] = cr * bi + ci * br

    n = a_re.shape[0]
    v1, v16 = jax.ShapeDtypeStruct((n, 1), F32), jax.ShapeDtypeStruct((n, SSM_GROUP), F32)
    return pl.pallas_call(body, name="ssm_params_fwd", out_shape=[v1, v1, v16, v16],
                          compiler_params=_params())(a_re, a_im, log_dt, b_re, b_im)


def _ssm_params_bwd(a_re, a_im, log_dt, b_re, b_im, g_lr, g_li, g_bbr, g_bbi):
    n = a_re.shape[0]

    def body(ar_ref, ai_ref, ldt_ref, br_ref, bi_ref, glr_ref, gli_ref, gbr_ref, gbi_ref,
             dar_ref, dai_ref, dldt_ref, dbr_ref, dbi_ref):
        ar, ai, dt = ar_ref[...], ai_ref[...], jnp.exp(ldt_ref[...])
        mag = jnp.exp(ar * dt)
        lr, li = mag * jnp.cos(ai * dt), mag * jnp.sin(ai * dt)
        den = ar * ar + ai * ai
        cr = ((lr - 1.0) * ar + li * ai) / den
        ci = (li * ar - (lr - 1.0) * ai) / den
        br, bi, gbr, gbi = br_ref[...], bi_ref[...], gbr_ref[...], gbi_ref[...]
        dbr_ref[...] = gbr * cr + gbi * ci
        dbi_ref[...] = gbi * cr - gbr * ci
        gcr = jnp.sum(gbr * br + gbi * bi, axis=1, keepdims=True)
        gci = jnp.sum(gbi * br - gbr * bi, axis=1, keepdims=True)
        ir, ii = ar / den, -ai / den
        glr = glr_ref[...] + gcr * ir + gci * ii
        gli = gli_ref[...] + gci * ir - gcr * ii
        qr, qi = cr * ir - ci * ii, cr * ii + ci * ir
        gar = -(gcr * qr + gci * qi)
        gai = -(gci * qr - gcr * qi)
        gxr = glr * lr + gli * li
        gxi = gli * lr - glr * li
        dar_ref[...] = gar + gxr * dt
        dai_ref[...] = gai + gxi * dt
        gdt = (gxr * ar + gxi * ai) * dt
        rowg = lax.broadcasted_iota(jnp.int32, (n, SSM_GROUPS), 0) // SSM_STATE
        colg = lax.broadcasted_iota(jnp.int32, (n, SSM_GROUPS), 1)
        dldt_ref[...] = jnp.sum(jnp.where(rowg == colg, gdt, 0.0), axis=0, keepdims=True)

    v1, v16 = jax.ShapeDtypeStruct((n, 1), F32), jax.ShapeDtypeStruct((n, SSM_GROUP), F32)
    return pl.pallas_call(body, name="ssm_params_bwd",
                          out_shape=[v1, v1, jax.ShapeDtypeStruct((1, SSM_GROUPS), F32), v16, v16],
                          compiler_params=_params())(a_re, a_im, log_dt, b_re, b_im, g_lr, g_li, g_bbr, g_bbi)


def _interleave(re, im, axis):
    shp = list(re.shape)
    new = shp[:axis] + [shp[axis] // LANES, LANES] + shp[axis + 1:]
    st = jnp.stack([re.reshape(new), im.reshape(new)], axis=axis + 1)
    return st.reshape(shp[:axis] + [2 * shp[axis]] + shp[axis + 1:])


def _deinterleave(v, axis):
    shp = list(v.shape)
    r = v.reshape(shp[:axis] + [shp[axis] // (2 * LANES), 2, LANES] + shp[axis + 1:])
    out = shp[:axis] + [shp[axis] // 2] + shp[axis + 1:]
    return (lax.index_in_dim(r, 0, axis + 1, keepdims=False).reshape(out),
            lax.index_in_dim(r, 1, axis + 1, keepdims=False).reshape(out))


def _b_matrix(bbr, bbi):
    per = SSM_GROUPS // SSM_BLOCKS
    eye = jnp.eye(per, dtype=F32)

    def blockdiag(v):
        x = v.reshape(SSM_BLOCKS, per, SSM_STATE, SSM_GROUP).transpose(0, 1, 3, 2)
        return (eye[None, :, None, :, None] * x[:, :, :, None, :]).reshape(SSM_GROUPS * SSM_GROUP, per * SSM_STATE)

    return _interleave(blockdiag(bbr), blockdiag(bbi), 1)


def _diag_blocks(v, rows, cols):
    per = SSM_GROUPS // SSM_BLOCKS
    x = v.reshape(SSM_BLOCKS, per, rows, per, cols) * jnp.eye(per, dtype=v.dtype)[None, :, None, :, None]
    return jnp.sum(x, axis=3).reshape(SSM_GROUPS, rows, cols)


def _b_matrix_grad(d):
    def diag(v):
        return _diag_blocks(v, SSM_GROUP, SSM_STATE).transpose(0, 2, 1).reshape(N_STATE, SSM_GROUP)

    dr, di = _deinterleave(d, 1)
    return diag(dr), diag(di)


def _c_matrix(c_re, c_im):
    per = SSM_GROUPS // SSM_BLOCKS
    eye = jnp.eye(per, dtype=F32)

    def blockdiag(v):
        x = v.reshape(SSM_BLOCKS, per, SSM_GROUP, SSM_STATE).transpose(0, 1, 3, 2)
        return (x[:, :, :, None, :] * eye[None, :, None, :, None]).reshape(N_STATE, per * SSM_GROUP)

    return _interleave(blockdiag(c_re), blockdiag(-c_im), 0)


def _c_matrix_grad(d):
    def diag(v):
        return _diag_blocks(v, SSM_STATE, SSM_GROUP).transpose(0, 2, 1)

    dr, di = _deinterleave(d, 0)
    return diag(dr), -diag(di)


def _row(v):
    return v.reshape(1, -1)


def _ssm_inputs(p):
    rows = lambda v: v.reshape(N_STATE, -1)
    ldt = jnp.repeat(p["ssm_log_dt"], SSM_STATE).reshape(N_STATE, 1)
    return rows(p["ssm_a_re"]), rows(p["ssm_a_im"]), ldt, rows(p["ssm_b_re"]), rows(p["ssm_b_im"])


def _lnmod(x, sc, sh):
    return _norm(x)[0] * (1.0 + sc) + sh


def _resid_ln(x, y, g, lg, lb):
    return _norm(ALPHA * x + (1.0 + g) * y)[0] * lg + lb


def _resid_ln_lnmod(x, y, g, lg, lb, sc, sh):
    xo = _resid_ln(x, y, g, lg, lb)
    return xo, _lnmod(xo, sc, sh)


def _layer_fwd(x, h1, mod, p, tag, next_mod):
    d = x.shape[1]
    sh_m, sc_m, g_m, sh_f, sc_f, g_f = [_row(mod[i]) for i in range(6)]
    nm = lambda s: f"{s}_{tag}"
    proj = _mm(nm("proj"), h1, p["w_in"], "nn", out_dtype=BF16)
    t = min(ATT_TILE, x.shape[0])
    kt3, vt3 = _kv_transposed(proj, t)
    att, car = _attn_fwd(proj, vt3, t)
    y_sb = _mm(nm("sb_up"), att, p["w_sb_up"], "nn", out_dtype=BF16)

    ssm_in = _ssm_inputs(p)
    lam_r, lam_i, bbr, bbi = _ssm_params_fwd(*ssm_in)
    lam = _interleave(lam_r.reshape(1, N_STATE), lam_i.reshape(1, N_STATE), 1)
    bmat = _b_matrix(bbr, bbi).astype(BF16)
    cmat = _c_matrix(p["ssm_c_re"], p["ssm_c_im"]).astype(BF16)
    hst, yc = _s5_fwd(proj, U_OFFSET, bmat, cmat, lam)

    def ssm_act(yc, u, dsk):
        y0 = yc + dsk * u
        return y0, _gelu(y0)

    y0, y1 = _rowwise(nm("ssm_act"), ssm_act, [(yc, 0, 512), (proj, 3, 512)], [_row(p["ssm_d"])], [(512, F32), (512, F32)])
    gl = _mm(nm("glu"), y1, p["w_glu"], "nn")
    y2 = _rowwise(nm("glu_act"), lambda y1, gl, b: y1 * jax.nn.sigmoid(gl + b), [(y1, 0, 512), (gl, 0, 512)],
                  [_row(p["b_glu"])], [(512, BF16)])
    y_ssm = _mm(nm("ssm_up"), y2, p["w_ssm_up"], "nn", out_dtype=BF16)

    def merge(gsb, gss, ysb, yss):
        return jax.nn.sigmoid(gsb) * ysb + jax.nn.sigmoid(gss) * yss

    merged = _rowwise(nm("merge"), merge, [(proj, 2, d), (proj, 3, d), (y_sb, 0, d), (y_ssm, 0, d)], [], [(d, BF16)])
    y = _mm(nm("out"), merged, p["w_out"], "nn")

    x1, h2 = _rowwise(nm("ln1"), _resid_ln_lnmod, [(x, 0, d), (y, 0, d)],
                      [g_m, _row(p["ln1_g"]), _row(p["ln1_b"]), sc_f, sh_f], [(d, F32), (d, BF16)])
    f_gate, f_up, act = _ffn_in(nm("ffn_in"), h2, p["w_ffn_in"])
    yf = _mm(nm("ffn_out"), act, p["w_ffn_out"], "nn")
    x2 = h1_next = None
    if next_mod is not None:
        x2, h1_next = _rowwise(nm("ln2"), _resid_ln_lnmod, [(x1, 0, d), (yf, 0, d)],
                               [g_f, _row(p["ln2_g"]), _row(p["ln2_b"]), next_mod[1], next_mod[0]], [(d, F32), (d, BF16)])
    saved = dict(x=x, h1=h1, proj=proj, ssm_in=ssm_in, kt3=kt3, car=car, att=att, y_sb=y_sb, lam=lam, bmat=bmat,
                 cmat=cmat, hst=hst, y0=y0, y1=y1, gl=gl, y2=y2, y_ssm=y_ssm, merged=merged, y=y, x1=x1, h2=h2, f_gate=f_gate, f_up=f_up,
                 act=act, yf=yf, t=t)
    return x2, h1_next, saved


def _resid_ln_bwd(x, y, dxo, g, lg):
    n, rstd = _norm(ALPHA * x + (1.0 + g) * y)
    dr = _norm_bwd(dxo * lg, n, rstd)
    return ALPHA * dr, (1.0 + g) * dr, _colsum(dxo * n), _colsum(dxo), _colsum(dr * y)


def _lnmod_bwd(x, dh, dxa, sc):
    n, rstd = _norm(x)
    return dxa + _norm_bwd(dh * (1.0 + sc), n, rstd), _colsum(dh * n), _colsum(dh)


def _lnmod_resid_ln_bwd(xo, dh, dxa, x, y, sc, g, lg):
    dxo, dsc, dsh = _lnmod_bwd(xo, dh, dxa, sc)
    dx, dy, dlg, dlb, dg = _resid_ln_bwd(x, y, dxo, g, lg)
    return dx, dy, dsc, dsh, dlg, dlb, dg


def _layer_bwd(dx1a, dyf, mod, p, sv, layer, depth, stacked):
    d = dx1a.shape[1]
    sh_m, sc_m, g_m, sh_f, sc_f, g_f = [_row(mod[i]) for i in range(6)]
    nm = lambda s: f"{s}_{layer}"
    grads = {}

    def weight_grad(n, a, b, **kw):
        grads[n] = _mm(nm("d" + n), a, b, "tn", out_dtype=BF16, into=(stacked.get(n), layer, depth), **kw)

    dact = _mm(nm("d_act"), dyf, p["w_ffn_out"], "nt", out_dtype=BF16)
    weight_grad("w_ffn_out", sv["act"], dyf)
    fh = sv["f_gate"].shape[1]

    def swiglu_bwd(g, u, da):
        sg = jax.nn.sigmoid(g)
        return jnp.concatenate([da * u * sg * (1.0 + g * (1.0 - sg)), da * g * sg], axis=1)

    df = _rowwise(nm("swiglu_bwd"), swiglu_bwd, [(sv["f_gate"], 0, fh), (sv["f_up"], 0, fh), (dact, 0, fh)], [], [(2 * fh, BF16)])
    dh2 = _mm(nm("d_h2"), df, p["w_ffn_in"], "nt")
    weight_grad("w_ffn_in", sv["h2"], df)
    dxa, dy, dsc_f, dsh_f, grads["ln1_g"], grads["ln1_b"], dg_m = _rowwise(
        nm("ln1_bwd"), _lnmod_resid_ln_bwd, [(sv["x1"], 0, d), (dh2, 0, d), (dx1a, 0, d), (sv["x"], 0, d), (sv["y"], 0, d)],
        [sc_f, g_m, _row(p["ln1_g"])], [(d, F32), (d, BF16)], [d] * 5)
    dmerged = _mm(nm("d_merged"), dy, p["w_out"], "nt", out_dtype=BF16)
    weight_grad("w_out", sv["merged"], dy)

    def merge_bwd(gsb, gss, ysb, yss, dm):
        s1, s2 = jax.nn.sigmoid(gsb), jax.nn.sigmoid(gss)
        return s1 * dm, s2 * dm, dm * ysb * s1 * (1.0 - s1), dm * yss * s2 * (1.0 - s2)

    dy_sb, dy_ssm, dg_sb, dg_ssm = _rowwise(
        nm("merge_bwd"), merge_bwd, [(sv["proj"], 2, d), (sv["proj"], 3, d), (sv["y_sb"], 0, d), (sv["y_ssm"], 0, d),
                                     (dmerged, 0, d)], [], [(d, BF16)] * 4)
    dy2 = _mm(nm("d_y2"), dy_ssm, p["w_ssm_up"], "nt")
    weight_grad("w_ssm_up", sv["y2"], dy_ssm)

    def glu_act_bwd(y1, gl, dy2, b):
        sg = jax.nn.sigmoid(gl + b)
        dgl = dy2 * y1 * sg * (1.0 - sg)
        return dy2 * sg, dgl, _colsum(dgl)

    dy1a, dgl, grads["b_glu"] = _rowwise(nm("glu_act_bwd"), glu_act_bwd, [(sv["y1"], 0, 512), (sv["gl"], 0, 512), (dy2, 0, 512)],
                                         [_row(p["b_glu"])], [(512, F32), (512, BF16)], [512])
    dy1b = _mm(nm("d_y1"), dgl, p["w_glu"], "nt")
    weight_grad("w_glu", sv["y1"], dgl)

    def ssm_act_bwd(y0, u, dy1a, dy1b, dsk):
        dy0 = (dy1a + dy1b) * _gelu_grad(y0)
        return dy0, dsk * dy0, _colsum(dy0 * u)

    dy0, du_a, grads["ssm_d"] = _rowwise(nm("ssm_act_bwd"), ssm_act_bwd,
                                         [(sv["y0"], 0, 512), (sv["proj"], 3, 512), (dy1a, 0, 512), (dy1b, 0, 512)],
                                         [_row(p["ssm_d"])], [(512, BF16), (512, F32)], [512])
    du_b, dlam, d_bmat, d_cmat = _s5_bwd(dy0, sv["hst"], sv["proj"], U_OFFSET, sv["bmat"], sv["cmat"], sv["lam"])
    grads["ssm_c_re"], grads["ssm_c_im"] = _c_matrix_grad(d_cmat)
    g_bbr, g_bbi = _b_matrix_grad(d_bmat)
    g_lr, g_li = _deinterleave(dlam[0:1], 1)
    da_re, da_im, dldt, db_re, db_im = _ssm_params_bwd(*sv["ssm_in"], g_lr.reshape(N_STATE, 1), g_li.reshape(N_STATE, 1),
                                                       g_bbr, g_bbi)
    grads["ssm_a_re"] = da_re.reshape(SSM_GROUPS, SSM_STATE)
    grads["ssm_a_im"] = da_im.reshape(SSM_GROUPS, SSM_STATE)
    grads["ssm_log_dt"] = dldt.reshape(SSM_GROUPS)
    grads["ssm_b_re"] = db_re.reshape(SSM_GROUPS, SSM_STATE, SSM_GROUP)
    grads["ssm_b_im"] = db_im.reshape(SSM_GROUPS, SSM_STATE, SSM_GROUP)
    datt = _mm(nm("d_att"), dy_sb, p["w_sb_up"], "nt", out_dtype=BF16)
    weight_grad("w_sb_up", sv["att"], dy_sb)
    dqs, dk, dv = _attn_bwd(sv["proj"], datt, sv["kt3"], sv["car"], sv["t"])

    def dproj_cols(dqs, dk, dv, dua, dub, dgsb, dgss):
        return jnp.concatenate([dqs * (1.0 / math.sqrt(HEAD_DIM)), dk, dv, dua + dub, dgsb.astype(F32), dgss.astype(F32)],
                               axis=1)

    dproj = _rowwise(nm("dproj"), dproj_cols, [(dqs, 0, 512), (dk, 0, 512), (dv, 0, 512), (du_a, 0, 512), (du_b, 0, 512),
                                               (dg_sb, 0, d), (dg_ssm, 0, d)], [], [(2048 + 2 * d, BF16)])
    dh1 = _mm(nm("d_h1"), dproj, p["w_in"], "nt")
    weight_grad("w_in", sv["h1"], dproj)
    for k in ("ln1_g", "ln1_b", "ssm_d", "b_glu"):
        grads[k] = grads[k].reshape(-1)
    return dh1, dxa, grads, (dg_m, dsh_f, dsc_f)


def _local_step(x, target, mod, layer_w):
    depth, d = len(layer_w), x.shape[1]
    rows = lambda l: [_row(mod[l][i]) for i in range(6)]
    h1 = _rowwise("lnmod1_0", _lnmod, [(x, 0, d)], [rows(0)[1], rows(0)[0]], [(d, BF16)])
    xs, saved = x, []
    for l in range(depth):
        xs, h1, sv = _layer_fwd(xs, h1, mod[l], layer_w[l], str(l), rows(l + 1)[:2] if l + 1 < depth else None)
        saved.append(sv)

    def head_bwd(x1, yf, tgt, g, lg, lb):
        err = _resid_ln(x1, yf, g, lg, lb) - tgt
        return _resid_ln_bwd(x1, yf, err * (1.0 / d), g, lg) + (_colsum(err * err) * (0.5 / d),)

    def boundary_bwd(dh, dxa, x1, yf, sc, g, lg, lb):
        dxo, dsc, dsh = _lnmod_bwd(_resid_ln(x1, yf, g, lg, lb), dh, dxa, sc)
        return _resid_ln_bwd(x1, yf, dxo, g, lg) + (dsc, dsh)

    lgrads, sums, stacked = [None] * depth, [dict() for _ in range(depth)], {}
    last, p = saved[-1], layer_w[-1]
    dx1a, dyf, dlg, dlb, dg_f, loss_cols = _rowwise(
        "head_bwd", head_bwd, [(last["x1"], 0, d), (last["yf"], 0, d), (target, 0, d)],
        [rows(depth - 1)[5], _row(p["ln2_g"]), _row(p["ln2_b"])], [(d, F32), (d, BF16)], [d] * 4)
    for l in reversed(range(depth)):
        sums[l]["g_f"] = dg_f
        dh1, dxa, lgrads[l], (sums[l]["g_m"], sums[l]["sh_f"], sums[l]["sc_f"]) = _layer_bwd(
            dx1a, dyf, mod[l], layer_w[l], saved[l], l, depth, stacked)
        lgrads[l]["ln2_g"], lgrads[l]["ln2_b"] = dlg.reshape(-1), dlb.reshape(-1)
        stacked = {n: lgrads[l][n] for n in COL_SPLIT + ROW_SPLIT}
        if l > 0:
            prev, p = saved[l - 1], layer_w[l - 1]
            dx1a, dyf, dlg, dlb, dg_f, sums[l]["sc_m"], sums[l]["sh_m"] = _rowwise(
                f"boundary_bwd_{l}", boundary_bwd, [(dh1, 0, d), (dxa, 0, d), (prev["x1"], 0, d), (prev["yf"], 0, d)],
                [rows(l)[1], rows(l - 1)[5], _row(p["ln2_g"]), _row(p["ln2_b"])], [(d, F32), (d, BF16)], [d] * 5)
        else:
            dx, sums[l]["sc_m"], sums[l]["sh_m"] = _rowwise("lnmod1_bwd", _lnmod_bwd, [(x, 0, d), (dh1, 0, d), (dxa, 0, d)],
                                                            [rows(0)[1]], [(d, F32)], [d, d])
    dmod = jnp.stack([jnp.concatenate([sums[l][k] for k in ("sh_m", "sc_m", "g_m", "sh_f", "sc_f", "g_f")], axis=0)
                      for l in range(depth)])
    return loss_cols, dx, dmod, lgrads, stacked


def _place():
    return lax.axis_index("x"), lax.axis_index("y"), lax.axis_index("c")


def _all_gather8(name, block):
    m_per, n = block.shape

    def body(x_ref, out_ref, send_sems, recv_sems, local_sem):
        x, y, c = _place()
        me, sibling = (x, y, c), (x, y, 1 - c)
        chips = [(1 - x, y), (x, 1 - y), (1 - x, 1 - y)]

        def rows(px, py, pc):
            return out_ref.at[pl.ds(pl.multiple_of((4 * px + 2 * py + pc) * m_per, 8), m_per), :]

        def copy(k, blk, to, src=None):
            return pltpu.make_async_remote_copy(src_ref=rows(*blk) if src is None else src, dst_ref=rows(*blk),
                                                send_sem=send_sems.at[k], recv_sem=recv_sems.at[k],
                                                device_id=to, device_id_type=MESH)

        mine = pltpu.make_async_copy(x_ref, rows(*me), local_sem)
        mine.start()
        first = [copy(0, me, sibling, src=x_ref)] + [copy(1 + j, me, (*chip, c), src=x_ref) for j, chip in enumerate(chips)]
        for cp in first:
            cp.start()
        passed = [copy(4 + j, (*chip, c), sibling) for j, chip in enumerate(chips)]
        for j, chip in enumerate(chips):
            copy(1 + j, (*chip, c), me).wait_recv()
            passed[j].start()
        copy(0, sibling, me).wait_recv()
        for j, chip in enumerate(chips):
            copy(4 + j, (*chip, 1 - c), me).wait_recv()
        for cp in first + passed:
            cp.wait_send()
        mine.wait()

    return pl.pallas_call(
        body, name=name, out_shape=jax.ShapeDtypeStruct((8 * m_per, n), block.dtype),
        in_specs=[pl.BlockSpec(memory_space=pltpu.VMEM)], out_specs=pl.BlockSpec(memory_space=pltpu.VMEM),
        scratch_shapes=[pltpu.SemaphoreType.DMA((7,)), pltpu.SemaphoreType.DMA((7,)), pltpu.SemaphoreType.DMA],
        compiler_params=_params(),
    )(block)


def _other_chips(x, y):
    return [(1 - x, y), (x, 1 - y), (1 - x, 1 - y)]


def _gather_weights(whole, by_rows):
    n = len(whole)

    def body(*refs):
        dst = refs[n:2 * n]
        ici_send, ici_recv, d2d_send, d2d_recv = refs[2 * n:]
        x, y, c = _place()
        chips = _other_chips(x, y)

        def part(ref, k, px, py, pc):
            _, r, cols = whole[k].shape
            q = 2 * px + py
            if by_rows[k]:
                return ref[k].at[:, pl.ds(pl.multiple_of((2 * q + pc) * (r // 8), 16), r // 8), :]
            return ref[k].at[:, pl.ds(pl.multiple_of(pc * (r // 2), 16), r // 2),
                             pl.ds(pl.multiple_of(q * (cols // 4), LANES), cols // 4)]

        def ici(k, j, px, py, to):
            return pltpu.make_async_remote_copy(src_ref=part(dst, k, px, py, c), dst_ref=part(dst, k, px, py, c),
                                                send_sem=ici_send.at[k, j], recv_sem=ici_recv.at[k, j],
                                                device_id=(*to, c), device_id_type=MESH)

        def d2d(k, j, px, py, pc):
            return pltpu.make_async_remote_copy(src_ref=part(dst, k, px, py, pc), dst_ref=part(dst, k, px, py, pc),
                                                send_sem=d2d_send.at[k, j], recv_sem=d2d_recv.at[k, j],
                                                device_id=(x, y, 1 - c), device_id_type=MESH)

        for k in range(n):
            for j, chip in enumerate(chips):
                ici(k, j, x, y, chip).start()
        for k in range(n):
            for j, chip in enumerate(chips):
                ici(k, j, *chip, chip).wait_recv()
                d2d(k, j, *chip, c).start()
        for k in range(n):
            for j, chip in enumerate(chips):
                d2d(k, j, *chip, 1 - c).wait_recv()
        for k in range(n):
            for j, chip in enumerate(chips):
                ici(k, j, x, y, chip).wait_send()
                d2d(k, j, *chip, c).wait_send()

    any_spec = pl.BlockSpec(memory_space=pl.ANY)
    return pl.pallas_call(
        body, name="gather_weights", in_specs=[any_spec] * n, out_specs=[any_spec] * n,
        out_shape=[jax.ShapeDtypeStruct(a.shape, a.dtype) for a in whole], input_output_aliases={k: k for k in range(n)},
        scratch_shapes=[pltpu.SemaphoreType.DMA((n, 3))] * 4,
        compiler_params=_params(),
    )(*whole)


def _part_shape(shape, by_rows):
    l, r, c = shape
    return (l, r // 8, c) if by_rows else (l, r // 2, c // 4)


def _pair_exchange(grads, by_rows):
    n = len(grads)

    def body(*refs):
        src, dst = refs[:n], refs[n:2 * n]
        send_sems, recv_sems = refs[2 * n:]
        x, y, c = _place()

        def window(k, q, pc):
            _, hr, hc = _part_shape(grads[k].shape, by_rows[k])
            if by_rows[k]:
                return src[k].at[:, pl.ds(pl.multiple_of((2 * q + pc) * hr, 16), hr), :]
            return src[k].at[:, pl.ds(pl.multiple_of(pc * hr, 16), hr), pl.ds(q * hc, hc)]

        def copy(k, q, pc):
            return pltpu.make_async_remote_copy(src_ref=window(k, q, pc), dst_ref=dst[k].at[q], send_sem=send_sems.at[k, q],
                                                recv_sem=recv_sems.at[k, q], device_id=(x, y, 1 - c), device_id_type=MESH)

        for k in range(n):
            for q in range(4):
                copy(k, q, 1 - c).start()
        for k in range(n):
            for q in range(4):
                copy(k, q, c).wait_recv()
        for k in range(n):
            for q in range(4):
                copy(k, q, 1 - c).wait_send()

    any_spec = pl.BlockSpec(memory_space=pl.ANY)
    return pl.pallas_call(
        body, name="pair_exchange", in_specs=[any_spec] * n, out_specs=[any_spec] * n,
        out_shape=[jax.ShapeDtypeStruct((4, *_part_shape(g.shape, rows)), g.dtype) for g, rows in zip(grads, by_rows)],
        scratch_shapes=[pltpu.SemaphoreType.DMA((n, 4)), pltpu.SemaphoreType.DMA((n, 4))],
        compiler_params=_params(),
    )(*grads)


def _pair_sum(name, g, theirs, by_rows, c, chip):
    _, l, hr, hc = theirs.shape
    tr = _pick(hr, (256, 176, 128, 64, 32))

    def body(s_ref, g_ref, t_ref, p_ref, own_ref):
        v = (g_ref[...].astype(F32) + t_ref[0].astype(F32)).astype(BF16)
        p_ref[0] = v

        @pl.when(pl.program_id(2) == s_ref[1])
        def _():
            own_ref[0] = v

    if by_rows:
        g_spec = pl.BlockSpec((1, tr, hc), lambda li, i, q, s: (li, (2 * q + s[0]) * (hr // tr) + i, 0))
    else:
        g_spec = pl.BlockSpec((1, tr, hc), lambda li, i, q, s: (li, s[0] * (hr // tr) + i, q))
    slot = pl.BlockSpec((1, 1, tr, hc), lambda li, i, q, s: (q, li, i, 0))
    grid_spec = pltpu.PrefetchScalarGridSpec(
        num_scalar_prefetch=1, grid=(l, hr // tr, 4), in_specs=[g_spec, slot],
        out_specs=[slot, pl.BlockSpec((1, 1, tr, hc), lambda li, i, q, s: (s[1], li, i, 0))])
    return pl.pallas_call(
        body, name=name, grid_spec=grid_spec, out_shape=[jax.ShapeDtypeStruct(theirs.shape, BF16)] * 2,
        compiler_params=_params(dimension_semantics=("arbitrary", "arbitrary", "arbitrary")),
    )(jnp.stack([c, chip]).astype(jnp.int32), g, theirs)


def _chip_scatter(sums, landing):
    n = len(sums)

    def body(*refs):
        src, dst = refs[:n], refs[2 * n:3 * n]
        send_sems, recv_sems = refs[3 * n:]
        x, y, c = _place()
        mine = 2 * x + y

        def copy(k, j, src_slot, dst_slot, to):
            return pltpu.make_async_remote_copy(src_ref=src[k].at[src_slot], dst_ref=dst[k].at[dst_slot],
                                                send_sem=send_sems.at[k, j], recv_sem=recv_sems.at[k, j],
                                                device_id=(*to, c), device_id_type=MESH)

        chips = _other_chips(x, y)
        for k in range(n):
            for j, (px, py) in enumerate(chips):
                copy(k, j, 2 * px + py, mine, (px, py)).start()
        for k in range(n):
            for j, (px, py) in enumerate(chips):
                copy(k, j, mine, 2 * px + py, (px, py)).wait_recv()
        for k in range(n):
            for j, (px, py) in enumerate(chips):
                copy(k, j, 2 * px + py, mine, (px, py)).wait_send()

    any_spec = pl.BlockSpec(memory_space=pl.ANY)
    return pl.pallas_call(
        body, name="chip_scatter", in_specs=[any_spec] * (2 * n), out_specs=[any_spec] * n,
        out_shape=[jax.ShapeDtypeStruct(a.shape, a.dtype) for a in landing],
        input_output_aliases={n + k: k for k in range(n)},
        scratch_shapes=[pltpu.SemaphoreType.DMA((n, 3)), pltpu.SemaphoreType.DMA((n, 3))],
        compiler_params=_params(),
    )(*sums, *landing)


def _sum_slots(name, parts, half=None):
    slots, l, r, c = parts.shape
    tr = _pick(r, (256, 176, 128, 64, 32, 8))

    def body(*refs):
        p_ref, o_ref = refs[-2:]
        acc = p_ref[0].astype(F32)
        for i in range(1, slots):
            acc = acc + p_ref[i].astype(F32)
        o_ref[...] = acc

    if half is None:
        return pl.pallas_call(
            body, name=name, grid=(l, r // tr), in_specs=[pl.BlockSpec((slots, 1, tr, c), lambda li, i: (0, li, i, 0))],
            out_specs=pl.BlockSpec((1, tr, c), lambda li, i: (li, i, 0)), out_shape=jax.ShapeDtypeStruct((l, r, c), F32),
            compiler_params=_params(dimension_semantics=("arbitrary", "arbitrary")),
        )(parts)
    grid_spec = pltpu.PrefetchScalarGridSpec(
        num_scalar_prefetch=1, grid=(l, r // tr),
        in_specs=[pl.BlockSpec((slots, 1, tr, c), lambda li, i, h: (0, li, i, 0))],
        out_specs=pl.BlockSpec((1, tr, c), lambda li, i, h: (li, h[0] * (r // tr) + i, 0)))
    return pl.pallas_call(
        body, name=name, grid_spec=grid_spec, out_shape=jax.ShapeDtypeStruct((l, 2 * r, c), F32),
        compiler_params=_params(dimension_semantics=("arbitrary", "arbitrary")),
    )(jnp.reshape(half, (1,)).astype(jnp.int32), parts)


def _swap_halves(blocks):
    n = len(blocks)

    def body(*refs):
        src, dst = refs[:n], refs[n:2 * n]
        send_sems, recv_sems = refs[2 * n:]
        x, y, c = _place()

        def half(ref, k, pc):
            r = blocks[k].shape[1] // 2
            return ref[k].at[:, pl.ds(pl.multiple_of(pc * r, 8), r), :]

        def copy(k, pc):
            return pltpu.make_async_remote_copy(src_ref=half(src, k, pc), dst_ref=half(dst, k, pc), send_sem=send_sems.at[k],
                                                recv_sem=recv_sems.at[k], device_id=(x, y, 1 - c), device_id_type=MESH)

        for k in range(n):
            copy(k, c).start()
        for k in range(n):
            copy(k, 1 - c).wait_recv()
        for k in range(n):
            copy(k, c).wait_send()

    any_spec = pl.BlockSpec(memory_space=pl.ANY)
    return pl.pallas_call(
        body, name="swap_halves", in_specs=[any_spec] * n, out_specs=[any_spec] * n,
        out_shape=[jax.ShapeDtypeStruct(b.shape, b.dtype) for b in blocks], input_output_aliases={k: k for k in range(n)},
        scratch_shapes=[pltpu.SemaphoreType.DMA((n,)), pltpu.SemaphoreType.DMA((n,))],
        compiler_params=_params(),
    )(*blocks)


def _adamw(name, w, g, m, v):
    shape = w.shape
    cols = shape[-1]
    flat = lambda a: a.reshape(-1, cols)
    rows = w.size // cols
    tr = _pick(rows, [r for r in (512, 256, 128, 64, 32, 16, 8) if r * cols <= 256 * 1024]) if rows % 8 == 0 else rows

    def body(w_ref, g_ref, m_ref, v_ref, go_ref, d_ref, nm_ref, nv_ref):
        gg = g_ref[...]
        go_ref[...] = gg
        nm = ADAM_B1 * m_ref[...] + (1.0 - ADAM_B1) * gg
        nv = ADAM_B2 * v_ref[...] + (1.0 - ADAM_B2) * (gg * gg)
        m_hat = nm / (1.0 - ADAM_B1 ** ADAM_STEP)
        v_hat = nv / (1.0 - ADAM_B2 ** ADAM_STEP)
        d_ref[...] = -ADAM_LR * (m_hat / (jnp.sqrt(v_hat) + ADAM_EPS) + ADAM_WD * w_ref[...])
        nm_ref[...] = nm
        nv_ref[...] = nv

    spec = pl.BlockSpec((tr, cols), lambda i: (i, 0))
    out = pl.pallas_call(
        body, name=name, grid=(rows // tr,), in_specs=[spec] * 4, out_specs=[spec] * 4,
        out_shape=[jax.ShapeDtypeStruct((rows, cols), F32)] * 4,
        compiler_params=_params(dimension_semantics=("arbitrary",)),
    )(flat(w), flat(g), flat(m), flat(v))
    return tuple(o.reshape(shape) for o in out)


WEIGHTS = ["w_ada", "b_ada", "w_in", "w_sb_up", "ssm_a_re", "ssm_a_im", "ssm_log_dt", "ssm_b_re", "ssm_b_im", "ssm_c_re",
           "ssm_c_im", "ssm_d", "w_glu", "b_glu", "w_ssm_up", "w_out", "ln1_g", "ln1_b", "w_ffn_in", "w_ffn_out", "ln2_g",
           "ln2_b"]
COL_SPLIT = ["w_in", "w_sb_up", "w_ssm_up", "w_ffn_in"]
ROW_SPLIT = ["w_glu", "w_out", "w_ffn_out"]
SMALL = ["ssm_a_re", "ssm_a_im", "ssm_log_dt", "ssm_b_re", "ssm_b_im", "ssm_c_re", "ssm_c_im", "ssm_d", "b_glu", "ln1_g",
         "ln1_b", "ln2_g", "ln2_b"]
SLAB_COLS = 1024


def _cast_into_whole(name, w, by_rows, chip):
    l, r, cols = w.shape
    tr = _pick(r, (512, 256, 128, 64, 16))

    def body(q_ref, w_ref, o_ref):
        o_ref[...] = w_ref[...].astype(BF16)

    if by_rows:
        out_map, shape = (lambda li, i, q: (li, q[0] * (r // tr) + i, 0)), (l, 4 * r, cols)
    else:
        out_map, shape = (lambda li, i, q: (li, i, q[0])), (l, r, 4 * cols)
    grid_spec = pltpu.PrefetchScalarGridSpec(
        num_scalar_prefetch=1, grid=(l, r // tr), in_specs=[pl.BlockSpec((1, tr, cols), lambda li, i, q: (li, i, 0))],
        out_specs=pl.BlockSpec((1, tr, cols), out_map))
    return pl.pallas_call(body, name=name, grid_spec=grid_spec, out_shape=jax.ShapeDtypeStruct(shape, BF16),
                          compiler_params=_params(dimension_semantics=("arbitrary", "arbitrary")),
                          )(jnp.reshape(chip, (1,)).astype(jnp.int32), w)


def _silu_rows(name, c):
    def body(c_ref, o_ref):
        v = c_ref[...]
        o_ref[...] = v * jax.nn.sigmoid(v)

    return pl.pallas_call(body, name=name, out_shape=jax.ShapeDtypeStruct(c.shape, F32), compiler_params=_params())(c)


def _pad_rows(v, mult=8):
    flat = v.reshape(-1)
    per = mult * SLAB_COLS
    total = -(-flat.size // per) * per
    return jnp.pad(flat, (0, total - flat.size)).reshape(-1, SLAB_COLS)


def kernel(x, c, w_ada, b_ada, w_in, w_sb_up, ssm_a_re, ssm_a_im, ssm_log_dt, ssm_b_re, ssm_b_im, ssm_c_re, ssm_c_im, ssm_d, w_glu, b_glu, w_ssm_up, w_out, ln1_g, ln1_b, w_ffn_in, w_ffn_out, ln2_g, ln2_b, loss_target, m_w_ada, m_b_ada, m_w_in, m_w_sb_up, m_ssm_a_re, m_ssm_a_im, m_ssm_log_dt, m_ssm_b_re, m_ssm_b_im, m_ssm_c_re, m_ssm_c_im, m_ssm_d, m_w_glu, m_b_glu, m_w_ssm_up, m_w_out, m_ln1_g, m_ln1_b, m_w_ffn_in, m_w_ffn_out, m_ln2_g, m_ln2_b, v_w_ada, v_b_ada, v_w_in, v_w_sb_up, v_ssm_a_re, v_ssm_a_im, v_ssm_log_dt, v_ssm_b_re, v_ssm_b_im, v_ssm_c_re, v_ssm_c_im, v_ssm_d, v_w_glu, v_b_glu, v_w_ssm_up, v_w_out, v_ln1_g, v_ln1_b, v_w_ffn_in, v_w_ffn_out, v_ln2_g, v_ln2_b):
    args = dict(locals())
    w = {n: args[n] for n in WEIGHTS}
    mom = {n: args["m_" + n] for n in WEIGHTS}
    var = {n: args["v_" + n] for n in WEIGHTS}
    depth, d = w_ada.shape[0], x.shape[-1]
    xi, yi, ci = _place()
    me, chip = 4 * xi + 2 * yi + ci, 2 * xi + yi
    ada_cols = w_ada.shape[2]

    big = COL_SPLIT + ROW_SPLIT
    by_rows = [n in ROW_SPLIT for n in big]
    full = dict(zip(big, _gather_weights([_cast_into_whole(f"cast_{n}", w[n], n in ROW_SPLIT, chip) for n in big], by_rows)))

    c_all = _all_gather8("gather_c", jnp.pad(c, ((0, 7), (0, 0))))[::8]
    c_act = _silu_rows("silu_c", c_all)
    b_cols = lax.dynamic_slice_in_dim(b_ada, chip * ada_cols, ada_cols, axis=1)
    mod_part = jnp.concatenate([_small_mm(f"mod_{l}", c_act, w_ada[l], "nn") + b_cols[l][None] for l in range(depth)], axis=0)
    mod_all = _all_gather8("gather_mod", mod_part).reshape(4, 2, depth, 8, ada_cols)[:, 0]
    mod_mine = lax.dynamic_index_in_dim(mod_all, me, axis=2, keepdims=False)
    mod = mod_mine.transpose(1, 0, 2).reshape(depth, 6, d)

    layer_w = [{**{n: (full[n], l) for n in big}, **{n: w[n][l] for n in SMALL}} for l in range(depth)]
    loss_cols, dx, dmods, lgrads, stacked = _local_step(x[0], loss_target[0], mod, layer_w)
    loss = lax.psum(jnp.sum(loss_cols), ("x", "y", "c"))
    grad_x = dx[None]

    theirs = _pair_exchange([stacked[n] for n in big], by_rows)
    pairs = [_pair_sum(f"pair_{n}", stacked[n], t, n in ROW_SPLIT, ci, chip) for n, t in zip(big, theirs)]
    landed = _chip_scatter([p[0] for p in pairs], [p[1] for p in pairs])
    halves = [_sum_slots(f"sum_{n}", p, half=ci) for n, p in zip(big, landed)]
    grad = dict(zip(big, _swap_halves(halves)))

    pieces = [dmods] + [jnp.stack([lgrads[l][n] for l in range(depth)]) for n in SMALL]
    slab = _pad_rows(jnp.concatenate([p.reshape(-1) for p in pieces]))
    slabs = _all_gather8("gather_small", slab).reshape(8, 1, *slab.shape)
    total = _sum_slots("sum_small", slabs)[0].reshape(-1)
    at = dmods.size
    grad["b_ada"] = total[:at].reshape(depth, 6 * d)
    for n, p in zip(SMALL, pieces[1:]):
        grad[n] = total[at:at + p.size].reshape(p.shape)
        at += p.size
    dmod_all = slabs.reshape(8, -1)[:, :dmods.size].reshape(8, depth, 4, ada_cols)
    dmod_cols = lax.dynamic_index_in_dim(dmod_all, chip, axis=2, keepdims=False)
    grad["w_ada"] = jnp.stack([_small_mm(f"dw_ada_{l}", c_act, dmod_cols[:, l], "tn") for l in range(depth)])

    delta, new_m, new_v = {}, {}, {}
    for n in WEIGHTS:
        grad[n], delta[n], new_m[n], new_v[n] = _adamw(f"adamw_{n}", w[n], grad[n], mom[n], var[n])
    return (loss, grad_x, *[grad[n] for n in WEIGHTS], *[delta[n] for n in WEIGHTS], *[new_m[n] for n in WEIGHTS],
            *[new_v[n] for n in WEIGHTS])
```

```python
import functools
import math

import jax
import jax.numpy as jnp
from jax import lax
from jax.experimental import pallas as pl
from jax.experimental.pallas import tpu as pltpu

F32 = jnp.float32
BF16 = jnp.bfloat16
MESH = pl.DeviceIdType.MESH

LANES = 128
HEAD_DIM = 64
SB_WIDTH = 512
ATT_TILE = 256
SSM_GROUPS, SSM_STATE, SSM_GROUP = 32, 64, 16
N_STATE = SSM_GROUPS * SSM_STATE
SSM_BLOCKS = SSM_GROUPS * SSM_GROUP // LANES
U_OFFSET = 3 * 512
LN_EPS = 1e-5
DEPTH = 2
ALPHA = (2 * DEPTH) ** 0.25
ADAM_LR, ADAM_B1, ADAM_B2, ADAM_EPS, ADAM_WD, ADAM_STEP = 0.001, 0.9, 0.999, 1e-08, 0.01, 10
VMEM_LIMIT = 56 * 1024 * 1024
GELU_K = math.sqrt(2.0 / math.pi)
GELU_C = 0.044715


def _params(**kw):
    return pltpu.CompilerParams(vmem_limit_bytes=VMEM_LIMIT, **kw)


def _pick(n, prefs):
    for p in prefs:
        if n % p == 0:
            return p
    return n


ROWWISE_TILES = ((1024, 3 * 1024), (512, 10 * 1024), (256, 1 << 30), (128, 1 << 30), (64, 1 << 30), (8, 1 << 30))


def _rowwise(name, fn, rows, vecs, outs, sums=(), tm=None):
    s = rows[0][0].shape[0]
    width = sum(w for _, _, w in rows) + sum(w for w, _ in outs)
    tm = tm or _pick(s, [t for t, most in ROWWISE_TILES if width <= most])
    nin, no, ns = len(rows) + len(vecs), len(outs), len(sums)

    def body(*refs):
        res = fn(*[r[...].astype(F32) for r in refs[:nin]])
        res = res if isinstance(res, tuple) else (res,)
        for r, v in zip(refs[nin:nin + no], res[:no]):
            r[...] = v.astype(r.dtype)
        if ns:
            @pl.when(pl.program_id(0) == 0)
            def _():
                for r in refs[nin + no:]:
                    r[...] = jnp.zeros_like(r)
            for r, v in zip(refs[nin + no:], res[no:]):
                r[...] += v

    in_specs = [pl.BlockSpec((tm, w), lambda i, cb=cb: (i, cb)) for _, cb, w in rows]
    in_specs += [pl.BlockSpec(v.shape, lambda i: (0, 0)) for v in vecs]
    out_specs = [pl.BlockSpec((tm, w), lambda i: (i, 0)) for w, _ in outs]
    out_specs += [pl.BlockSpec((1, w), lambda i: (0, 0)) for w in sums]
    out_shape = [jax.ShapeDtypeStruct((s, w), dt) for w, dt in outs]
    out_shape += [jax.ShapeDtypeStruct((1, w), F32) for w in sums]
    res = pl.pallas_call(
        body, name=name, grid=(s // tm,), in_specs=in_specs, out_specs=out_specs, out_shape=out_shape,
        compiler_params=_params(dimension_semantics=("arbitrary",)),
    )(*[a for a, _, _ in rows], *vecs)
    return res[0] if len(res) == 1 else tuple(res)


MM_TILES = (1408, 1024, 512, 256, 128)


def _slab_spec(block, index, slab):
    if slab is None:
        return pl.BlockSpec(block, index)
    return pl.BlockSpec((None, *block), lambda *g: (slab, *index(*g)))


def _mm(name, a, b, mode, out_dtype=F32, into=None):
    b, b_slab = b if isinstance(b, tuple) else (b, None)
    if mode == "nn":
        m, k, n = a.shape[0], a.shape[1], b.shape[-1]
    elif mode == "nt":
        m, k, n = a.shape[0], a.shape[1], b.shape[-2]
    else:
        k, m, n = a.shape[0], a.shape[1], b.shape[-1]
    tm = _pick(m, MM_TILES if mode == "tn" else (2048,) + MM_TILES[1:])
    tn = _pick(n, MM_TILES)
    tk = _pick(k, (2048,) + MM_TILES if mode == "tn" else MM_TILES)
    nk = k // tk
    dims = {"nn": ((1,), (0,)), "nt": ((1,), (1,)), "tn": ((0,), (0,))}[mode]

    def body(a_ref, b_ref, *rest):
        o_ref = rest[-2] if nk > 1 else rest[-1]
        prod = lax.dot_general(a_ref[...].astype(BF16), b_ref[...].astype(BF16), (dims, ((), ())),
                               preferred_element_type=F32)
        if nk == 1:
            o_ref[...] = prod.astype(o_ref.dtype)
            return
        acc_ref = rest[-1]
        kk = pl.program_id(2)

        @pl.when(kk == 0)
        def _():
            acc_ref[...] = prod

        @pl.when(kk > 0)
        def _():
            acc_ref[...] += prod

        @pl.when(kk == nk - 1)
        def _():
            o_ref[...] = acc_ref[...].astype(o_ref.dtype)

    if mode == "tn":
        a_spec = pl.BlockSpec((tk, tm), lambda i, j, kk: (kk, i))
    else:
        a_spec = pl.BlockSpec((tm, tk), lambda i, j, kk: (i, kk))
    b_block, b_index = ((tn, tk), lambda i, j, kk: (j, kk)) if mode == "nt" else ((tk, tn), lambda i, j, kk: (kk, j))
    b_spec = _slab_spec(b_block, b_index, b_slab)
    in_specs, operands, aliases = [a_spec, b_spec], [a, b], {}
    if into is None:
        out_spec = pl.BlockSpec((tm, tn), lambda i, j, kk: (i, j))
        out_shape = jax.ShapeDtypeStruct((m, n), out_dtype)
    else:
        buf, slab, count = into
        out_spec = pl.BlockSpec((None, tm, tn), lambda i, j, kk: (slab, i, j))
        out_shape = jax.ShapeDtypeStruct((count, m, n), out_dtype)
        if buf is not None:
            in_specs.append(pl.BlockSpec(memory_space=pl.ANY))
            operands.append(buf)
            aliases = {2: 0}
    return pl.pallas_call(
        body, name=name, grid=(m // tm, n // tn, nk), in_specs=in_specs, out_specs=out_spec, out_shape=out_shape,
        scratch_shapes=[pltpu.VMEM((tm, tn), F32)] if nk > 1 else [], input_output_aliases=aliases,
        compiler_params=_params(dimension_semantics=("arbitrary", "arbitrary", "arbitrary")),
    )(*operands)


def _ffn_in(name, h, w):
    w, slab = w if isinstance(w, tuple) else (w, None)
    s, d = h.shape
    f = w.shape[-1] // 2
    tm, tn = _pick(s, MM_TILES[2:]), _pick(f, MM_TILES)

    def body(h_ref, wg_ref, wu_ref, g_ref, u_ref, a_ref):
        hb = h_ref[...]
        g, u = _nn(hb, wg_ref[...]), _nn(hb, wu_ref[...])
        g_ref[...] = g.astype(BF16)
        u_ref[...] = u.astype(BF16)
        a_ref[...] = (g * jax.nn.sigmoid(g) * u).astype(BF16)

    out = pl.BlockSpec((tm, tn), lambda i, j: (i, j))
    return pl.pallas_call(
        body, name=name, grid=(s // tm, f // tn),
        in_specs=[pl.BlockSpec((tm, d), lambda i, j: (i, 0)), _slab_spec((d, tn), lambda i, j: (0, j), slab),
                  _slab_spec((d, tn), lambda i, j: (0, f // tn + j), slab)],
        out_specs=[out, out, out], out_shape=[jax.ShapeDtypeStruct((s, f), BF16)] * 3,
        compiler_params=_params(dimension_semantics=("arbitrary", "arbitrary")),
    )(h, w, w)


def _small_mm(name, a, b, mode):
    dims = {"nn": ((1,), (0,)), "tn": ((0,), (0,))}[mode]
    m = a.shape[0] if mode == "nn" else a.shape[1]

    def body(a_ref, b_ref, o_ref):
        o_ref[...] = lax.dot_general(a_ref[...], b_ref[...], (dims, ((), ())), precision=lax.Precision.HIGHEST,
                                     preferred_element_type=F32)

    return pl.pallas_call(body, name=name, out_shape=jax.ShapeDtypeStruct((m, b.shape[1]), F32),
                          compiler_params=_params())(a, b)


def _norm(x):
    mu = jnp.mean(x, axis=-1, keepdims=True)
    xc = x - mu
    rstd = lax.rsqrt(jnp.mean(xc * xc, axis=-1, keepdims=True) + LN_EPS)
    return xc * rstd, rstd


def _norm_bwd(dn, n, rstd):
    return rstd * (dn - jnp.mean(dn, axis=-1, keepdims=True) - n * jnp.mean(dn * n, axis=-1, keepdims=True))


def _colsum(v):
    return jnp.sum(v, axis=0, keepdims=True)


def _gelu(x):
    return 0.5 * x * (1.0 + jnp.tanh(GELU_K * (x + GELU_C * x * x * x)))


def _gelu_grad(x):
    t = jnp.tanh(GELU_K * (x + GELU_C * x * x * x))
    return 0.5 * (1.0 + t) + 0.5 * x * (1.0 - t * t) * GELU_K * (1.0 + 3.0 * GELU_C * x * x)


def _log_sigmoid_parts(z):
    lb = jnp.minimum(z, 0.0) - jnp.log(1.0 + jnp.exp(-jnp.abs(z)))
    return lb, lb - z


def _kv_transposed(proj, t):
    s = proj.shape[0]
    nb, nhp = s // t, SB_WIDTH // LANES

    def body(k_ref, v_ref, kt_ref, vt_ref):
        k, v = k_ref[...].astype(F32), v_ref[...].astype(F32)
        for hp in range(nhp):
            kt_ref[hp, 0] = k[:, hp * LANES:(hp + 1) * LANES].T.astype(BF16)
            vt_ref[hp, 0] = v[:, hp * LANES:(hp + 1) * LANES].T.astype(BF16)

    col = lambda cb: pl.BlockSpec((t, SB_WIDTH), lambda i, cb=cb: (i, cb))
    t_out = pl.BlockSpec((nhp, 1, LANES, t), lambda i: (0, i, 0, 0))
    return pl.pallas_call(
        body, name="kv_transposed", grid=(nb,), in_specs=[col(1), col(2)], out_specs=[t_out, t_out],
        out_shape=[jax.ShapeDtypeStruct((nhp, nb, LANES, t), BF16)] * 2,
        compiler_params=_params(dimension_semantics=("arbitrary",)),
    )(proj, proj)


def _tile_masks(t):
    row = lax.broadcasted_iota(jnp.int32, (t, t), 0)
    col = lax.broadcasted_iota(jnp.int32, (t, t), 1)
    return row, col


DEAD_LOG_WEIGHT = -110.0


def _walk_down(i, tiles, state, alive):
    st = lax.cond(i == 0, lambda s_: tiles([i], s_, [True]), lambda s_: tiles([i, i - 1], s_, [True, False]), state)
    n = jnp.maximum(i - 1, 0)

    def pair(c):
        nxt = tiles([i - 2 - 2 * c[0], i - 3 - 2 * c[0]], c[1], [False, False])
        return c[0] + 1, nxt, alive(nxt)

    p, st, go = lax.while_loop(lambda c: (c[0] < n // 2) & c[2], pair, (jnp.int32(0), st, alive(st)))
    return lax.cond((n % 2 == 1) & (p == n // 2) & go, lambda s_: tiles([0], s_, [False]), lambda s_: s_, st)


def _walk_up(i, first, tiles, state):
    n = jnp.maximum(i - 1 - first, 0)
    st = lax.fori_loop(0, n // 2, lambda p, s_: tiles([first + 2 * p, first + 2 * p + 1], s_, [False, False]), state)
    st = lax.cond(n % 2 == 1, lambda s_: tiles([i - 2], s_, [False]), lambda s_: s_, st)
    return lax.cond(i == 0, lambda s_: tiles([i], s_, [True]), lambda s_: tiles([i - 1, i], s_, [False, True]), st)


def _nt(a, b):
    return lax.dot_general(a, b, (((1,), (1,)), ((), ())), preferred_element_type=F32)


def _nn(a, b):
    return jnp.dot(a, b, preferred_element_type=F32)


def _attn_fwd(proj, vt3, t):
    s = proj.shape[0]
    nb, nhp = s // t, SB_WIDTH // LANES

    def body(q_ref, k_ref, vt_ref, o_ref, car_ref):
        i = pl.program_id(1)
        q2 = q_ref[...] * (1.0 / math.sqrt(HEAD_DIM))
        lane_q = lax.broadcasted_iota(jnp.int32, q2.shape, 1)
        row, col = _tile_masks(t)
        later = (col > row).astype(BF16)
        valid = row < col
        orow = lax.broadcasted_iota(jnp.int32, (LANES, t), 0)
        car_ref[...] = jnp.full(car_ref.shape, 2.0 * DEAD_LOG_WEIGHT, F32)
        qh = [jnp.where((lane_q < HEAD_DIM) == (hh == 0), q2, jnp.zeros_like(q2)) for hh in range(2)]

        def tiles(js, state, diagonal):
            chains = [(n, hh) for n in range(len(js)) for hh in range(2)]
            kb = [k_ref[pl.ds(pl.multiple_of(j * t, t), t), :] for j in js]
            z = {ch: _nt(kb[ch[0]], qh[ch[1]]) for ch in chains}
            lb, aft, csum = {}, {}, {}
            for ch in chains:
                lb[ch], l1m = _log_sigmoid_parts(z[ch])
                if diagonal[ch[0]]:
                    l1m = jnp.where(valid, l1m, 0.0)
                aft[ch] = _nn(later, l1m.astype(BF16))
                csum[ch] = _colsum(l1m)
            state = list(state)
            for ch in chains:
                n, hh = ch
                c_after, acc = state[hh]
                w = jnp.exp(lb[ch] + aft[ch] + c_after)
                if diagonal[ch[0]]:
                    w = jnp.where(valid, w, 0.0)
                car_ref[hh, pl.ds(js[n], 1), :] = c_after
                state[hh] = (c_after + csum[ch], acc + _nn(vt_ref[0, js[n]], w.astype(BF16)))
            return tuple(state)

        def alive(state):
            return jnp.max(jnp.maximum(state[0][0], state[1][0])) >= DEAD_LOG_WEIGHT

        zero = (jnp.zeros((1, t), F32), jnp.zeros((LANES, t), F32))
        (_, acc0), (_, acc1) = _walk_down(i, tiles, (zero, zero), alive)
        o_ref[...] = jnp.where(orow < HEAD_DIM, acc0, acc1).T.astype(o_ref.dtype)

    return pl.pallas_call(
        body, name="attn_fwd", grid=(nhp, nb),
        in_specs=[pl.BlockSpec((t, LANES), lambda hp, i: (i, hp)),
                  pl.BlockSpec((s, LANES), lambda hp, i: (0, nhp + hp)),
                  pl.BlockSpec((1, nb, LANES, t), lambda hp, i: (hp, 0, 0, 0))],
        out_specs=[pl.BlockSpec((t, LANES), lambda hp, i: (i, hp)),
                   pl.BlockSpec((2, nb, t), lambda hp, i: (hp, 0, i))],
        out_shape=[jax.ShapeDtypeStruct((s, nhp * LANES), BF16), jax.ShapeDtypeStruct((2 * nhp, nb, s), F32)],
        compiler_params=_params(dimension_semantics=("arbitrary", "arbitrary")),
    )(proj, proj, vt3)


def _attn_bwd(proj, do, kt3, car, t):
    s = proj.shape[0]
    nb, nhp = s // t, SB_WIDTH // LANES

    def body(q_ref, do_ref, k_ref, v_ref, kt_ref, car_ref, dq_ref, dk_ref, dv_ref):
        i = pl.program_id(1)

        @pl.when(i == 0)
        def _():
            dk_ref[...] = jnp.zeros_like(dk_ref)
            dv_ref[...] = jnp.zeros_like(dv_ref)

        q2, do2 = q_ref[...] * (1.0 / math.sqrt(HEAD_DIM)), do_ref[...]
        lane_q = lax.broadcasted_iota(jnp.int32, q2.shape, 1)
        row, col = _tile_masks(t)
        later = (col > row).astype(BF16)
        earlier = (col < row).astype(BF16)
        valid = row < col
        orow = lax.broadcasted_iota(jnp.int32, (LANES, t), 0)
        head = [(lane_q < HEAD_DIM) == (hh == 0) for hh in range(2)]
        qh = [jnp.where(hm, q2, jnp.zeros_like(q2)) for hm in head]
        doh = [jnp.where(hm, do2, jnp.zeros_like(do2)) for hm in head]

        def tiles(js, state, diagonal):
            chains = [(n, hh) for n in range(len(js)) for hh in range(2)]
            rows = [pl.ds(pl.multiple_of(j * t, t), t) for j in js]
            kb = [k_ref[r, :] for r in rows]
            vb = [v_ref[r, :] for r in rows]
            z = {ch: _nt(kb[ch[0]], qh[ch[1]]) for ch in chains}
            dw = {ch: _nt(vb[ch[0]], doh[ch[1]]) for ch in chains}
            lb, beta, aft = {}, {}, {}
            for ch in chains:
                lb[ch], l1m = _log_sigmoid_parts(z[ch])
                beta[ch] = jnp.exp(lb[ch])
                if diagonal[ch[0]]:
                    l1m = jnp.where(valid, l1m, 0.0)
                aft[ch] = _nn(later, l1m.astype(BF16))
            w, g, gsum, g_in = {}, {}, {}, {}
            for ch in chains:
                n, hh = ch
                w[ch] = jnp.exp(lb[ch] + aft[ch] + car_ref[hh, pl.ds(js[n], 1), :])
                if diagonal[ch[0]]:
                    w[ch] = jnp.where(valid, w[ch], 0.0)
                g[ch] = dw[ch] * w[ch]
                g_in[ch] = _nn(earlier, g[ch].astype(BF16))
                gsum[ch] = _colsum(g[ch])
            state = list(state)
            dk_t, dv_t = [None] * len(js), [None] * len(js)
            for ch in chains:
                n, hh = ch
                c_g, dqt = state[hh]
                dz = g[ch] - beta[ch] * (g[ch] + g_in[ch] + c_g)
                if diagonal[ch[0]]:
                    dz = jnp.where(valid, dz, 0.0)
                dzb, wb = dz.astype(BF16), w[ch].astype(BF16)
                dk_h, dv_h = _nn(dzb, qh[hh]), _nn(wb, doh[hh])
                dk_t[n] = dk_h if dk_t[n] is None else dk_t[n] + dk_h
                dv_t[n] = dv_h if dv_t[n] is None else dv_t[n] + dv_h
                state[hh] = (c_g + gsum[ch], dqt + _nn(kt_ref[0, js[n]], dzb))
            for n in range(len(js)):
                dk_ref[rows[n], :] += dk_t[n]
                dv_ref[rows[n], :] += dv_t[n]
            return tuple(state)

        reach = jnp.max(jnp.max(car_ref[...], axis=2, keepdims=True), axis=0)
        dead = (reach < DEAD_LOG_WEIGHT) & (lax.broadcasted_iota(jnp.int32, reach.shape, 0) < i)
        first = jnp.sum(jnp.where(dead, 1.0, 0.0)).astype(jnp.int32)
        zero = (jnp.zeros((1, t), F32), jnp.zeros((LANES, t), F32))
        (_, dq0), (_, dq1) = _walk_up(i, first, tiles, (zero, zero))
        dq_ref[...] = jnp.where(orow < HEAD_DIM, dq0, dq1).T

    tile_spec = pl.BlockSpec((t, LANES), lambda hp, i: (i, hp))
    whole = pl.BlockSpec((s, LANES), lambda hp, i: (0, hp))
    return pl.pallas_call(
        body, name="attn_bwd", grid=(nhp, nb),
        in_specs=[tile_spec, tile_spec, pl.BlockSpec((s, LANES), lambda hp, i: (0, nhp + hp)),
                  pl.BlockSpec((s, LANES), lambda hp, i: (0, 2 * nhp + hp)),
                  pl.BlockSpec((1, nb, LANES, t), lambda hp, i: (hp, 0, 0, 0)),
                  pl.BlockSpec((2, nb, t), lambda hp, i: (hp, 0, i))],
        out_specs=[tile_spec, whole, whole],
        out_shape=[jax.ShapeDtypeStruct((s, nhp * LANES), F32)] * 3,
        compiler_params=_params(dimension_semantics=("arbitrary", "arbitrary")),
    )(proj, do, proj, proj, kt3, car)


SCAN_LANES = 1024
SCAN_ROWS = 8
S5_CHUNKS = 4


def _scan_chunks(v):
    n = v.shape[1] // (2 * LANES)
    return [(v[:, c * 2 * LANES:c * 2 * LANES + LANES], v[:, c * 2 * LANES + LANES:(c + 1) * 2 * LANES]) for c in range(n)]


def _scan_tables(lr, li, reverse):
    if reverse:
        li = -li
    row = lax.broadcasted_iota(jnp.int32, (SCAN_ROWS, LANES), 0)
    powers = [(lr, li)]
    for _ in range(SCAN_ROWS - 1):
        pr, pi = powers[-1]
        powers.append((pr * lr - pi * li, pr * li + pi * lr))
    levels = []
    for d in (1, 2, 4):
        keep = (row < SCAN_ROWS - d) if reverse else (row >= d)
        levels.append((SCAN_ROWS - d if reverse else d,
                       (jnp.where(keep, powers[d - 1][0], 0.0), jnp.where(keep, powers[d - 1][1], 0.0))))
    pr = pi = jnp.zeros((SCAN_ROWS, LANES), F32)
    for r in range(SCAN_ROWS):
        steps = SCAN_ROWS - r if reverse else r + 1
        pr = jnp.where(row == r, powers[steps - 1][0], pr)
        pi = jnp.where(row == r, powers[steps - 1][1], pi)
    return levels, (pr, pi)


def _s5_fwd(proj, u_off, bmat, cmat, lam):
    s, w = proj.shape[0], SSM_BLOCKS * bmat.shape[1]
    tt = _pick(s, (512, 256, 128, 8))
    nt = s // tt
    cin = bmat.shape[0] // SSM_BLOCKS
    chunk = tt // S5_CHUNKS

    def body(u_ref, b_ref, c_ref, lam_ref, h_ref, y_ref, x_ref, st_ref):
        @pl.when(pl.program_id(1) == 0)
        def _():
            st_ref[...] = jnp.zeros_like(st_ref)

        tables = [_scan_tables(lr, li, reverse=False) for lr, li in _scan_chunks(lam_ref[...])]

        def project(k):
            x_ref[k * chunk:(k + 1) * chunk, :] = _nn(u_ref[k * chunk:(k + 1) * chunk, :].astype(BF16), b_ref[...])

        def tile(r0, last):
            last, parts = list(last), []
            for c, (xr, xi) in enumerate(_scan_chunks(x_ref[r0:r0 + SCAN_ROWS, :])):
                levels, (pr, pi) = tables[c]
                for d, (ar, ai) in levels:
                    sr, si = pltpu.roll(xr, d, 0), pltpu.roll(xi, d, 0)
                    xr, xi = xr + ar * sr - ai * si, xi + ar * si + ai * sr
                br, bi = last[2 * c], last[2 * c + 1]
                hr = xr + pr * br - pi * bi
                hi = xi + pr * bi + pi * br
                last[2 * c], last[2 * c + 1] = hr[SCAN_ROWS - 1:], hi[SCAN_ROWS - 1:]
                parts += [hr, hi]
            h_ref[r0:r0 + SCAN_ROWS, :] = jnp.concatenate(parts, axis=1)
            return tuple(last)

        st = st_ref[0:1, :]
        last = tuple(st[:, c * LANES:(c + 1) * LANES] for c in range(SCAN_LANES // LANES))
        project(0)
        for k in range(S5_CHUNKS):
            if k + 1 < S5_CHUNKS:
                project(k + 1)
            for r0 in range(k * chunk, (k + 1) * chunk, SCAN_ROWS):
                last = tile(r0, last)
            y_ref[k * chunk:(k + 1) * chunk, :] = _nn(h_ref[k * chunk:(k + 1) * chunk, :].astype(BF16), c_ref[...])
        st_ref[0:1, :] = jnp.concatenate(last, axis=1)

    return pl.pallas_call(
        body, name="s5_fwd", grid=(SSM_BLOCKS, nt),
        in_specs=[pl.BlockSpec((tt, cin), lambda kb, i: (i, u_off // cin + kb)),
                  pl.BlockSpec((cin, SCAN_LANES), lambda kb, i: (kb, 0)),
                  pl.BlockSpec((SCAN_LANES, cin), lambda kb, i: (kb, 0)),
                  pl.BlockSpec((1, SCAN_LANES), lambda kb, i: (0, kb))],
        out_specs=[pl.BlockSpec((tt, SCAN_LANES), lambda kb, i: (i, kb)), pl.BlockSpec((tt, cin), lambda kb, i: (i, kb))],
        out_shape=[jax.ShapeDtypeStruct((s, w), F32), jax.ShapeDtypeStruct((s, bmat.shape[0]), F32)],
        scratch_shapes=[pltpu.VMEM((tt, SCAN_LANES), F32), pltpu.VMEM((SCAN_ROWS, SCAN_LANES), F32)],
        compiler_params=_params(dimension_semantics=("arbitrary", "arbitrary")),
    )(proj, bmat, cmat, lam)


def _s5_bwd(dy, h, proj, u_off, bmat, cmat, lam):
    s, w = h.shape
    tt = _pick(s, (512, 256, 128, 8))
    nt = s // tt
    cin = bmat.shape[0] // SSM_BLOCKS
    chunk = tt // S5_CHUNKS

    def body(dy_ref, h_ref, u_ref, b_ref, c_ref, lam_ref, du_ref, dlam_ref, db_ref, dc_ref, e_ref, a_ref, st_ref):
        @pl.when(pl.program_id(1) == 0)
        def _():
            st_ref[...] = jnp.zeros_like(st_ref)
            dlam_ref[...] = jnp.zeros_like(dlam_ref)
            db_ref[...] = jnp.zeros_like(db_ref)
            dc_ref[...] = jnp.zeros_like(dc_ref)

        tables = [_scan_tables(lr, li, reverse=True) for lr, li in _scan_chunks(lam_ref[...])]
        nch = len(tables)
        row = lax.broadcasted_iota(jnp.int32, (SCAN_ROWS, LANES), 0)
        rows_first = (((0,), (0,)), ((), ()))

        def project(k):
            e_ref[k * chunk:(k + 1) * chunk, :] = _nt(dy_ref[k * chunk:(k + 1) * chunk, :], c_ref[...])

        def finish(k):
            rows = slice(k * chunk, (k + 1) * chunk)
            adj = a_ref[rows, :].astype(BF16)
            du_ref[rows, :] = _nt(adj, b_ref[...])
            db_ref[...] += lax.dot_general(u_ref[rows, :].astype(BF16), adj, rows_first, preferred_element_type=F32)
            dc_ref[...] += lax.dot_general(h_ref[rows, :].astype(BF16), dy_ref[rows, :], rows_first, preferred_element_type=F32)

        def tile(r0, carry):
            e_c = _scan_chunks(e_ref[r0:r0 + SCAN_ROWS, :])
            h_c = _scan_chunks(h_ref[r0:r0 + SCAN_ROWS, :])
            carry, parts = list(carry), []
            for c in range(nch):
                (yr, yi), (hr, hi) = e_c[c], h_c[c]
                levels, (pr, pi) = tables[c]
                for shift, (lr, li) in levels:
                    sr, si = pltpu.roll(yr, shift, 0), pltpu.roll(yi, shift, 0)
                    yr, yi = yr + lr * sr - li * si, yi + lr * si + li * sr
                nr, ni, dr, di = carry[4 * c:4 * c + 4]
                ar = yr + pr * nr - pi * ni
                ai = yi + pr * ni + pi * nr
                nxr = jnp.where(row == SCAN_ROWS - 1, nr, pltpu.roll(ar, SCAN_ROWS - 1, 0))
                nxi = jnp.where(row == SCAN_ROWS - 1, ni, pltpu.roll(ai, SCAN_ROWS - 1, 0))
                carry[4 * c:4 * c + 4] = [ar[0:1], ai[0:1], dr + nxr * hr + nxi * hi, di + nxi * hr - nxr * hi]
                parts += [ar, ai]
            a_ref[r0:r0 + SCAN_ROWS, :] = jnp.concatenate(parts, axis=1)
            return tuple(carry)

        st, dl = st_ref[0:1, :], dlam_ref[...]
        init = []
        for c in range(nch):
            lo = c * 2 * LANES
            init += [st[:, lo:lo + LANES], st[:, lo + LANES:lo + 2 * LANES],
                     dl[:, lo:lo + LANES], dl[:, lo + LANES:lo + 2 * LANES]]
        fin = tuple(init)
        project(S5_CHUNKS - 1)
        for k in reversed(range(S5_CHUNKS)):
            if k > 0:
                project(k - 1)
            for r0 in reversed(range(k * chunk, (k + 1) * chunk, SCAN_ROWS)):
                fin = tile(r0, fin)
            finish(k)
        st_ref[0:1, :] = jnp.concatenate([fin[4 * c + q] for c in range(nch) for q in (0, 1)], axis=1)
        dlam_ref[...] = jnp.concatenate([fin[4 * c + q] for c in range(nch) for q in (2, 3)], axis=1)

        @pl.when(pl.program_id(1) == nt - 1)
        def _():
            dlam_ref[0:1, :] = jnp.sum(dlam_ref[...], axis=0, keepdims=True)

    def rev(width, col):
        return pl.BlockSpec((tt, width), lambda kb, i: (nt - 1 - i, col(kb)))

    return pl.pallas_call(
        body, name="s5_bwd", grid=(SSM_BLOCKS, nt),
        in_specs=[rev(cin, lambda kb: kb), rev(SCAN_LANES, lambda kb: kb), rev(cin, lambda kb: u_off // cin + kb),
                  pl.BlockSpec((cin, SCAN_LANES), lambda kb, i: (kb, 0)),
                  pl.BlockSpec((SCAN_LANES, cin), lambda kb, i: (kb, 0)),
                  pl.BlockSpec((1, SCAN_LANES), lambda kb, i: (0, kb))],
        out_specs=[rev(cin, lambda kb: kb), pl.BlockSpec((SCAN_ROWS, SCAN_LANES), lambda kb, i: (0, kb)),
                   pl.BlockSpec((cin, SCAN_LANES), lambda kb, i: (kb, 0)), pl.BlockSpec((SCAN_LANES, cin), lambda kb, i: (kb, 0))],
        out_shape=[jax.ShapeDtypeStruct((s, bmat.shape[0]), F32), jax.ShapeDtypeStruct((SCAN_ROWS, w), F32),
                   jax.ShapeDtypeStruct((bmat.shape[0], SCAN_LANES), F32), jax.ShapeDtypeStruct((w, cin), F32)],
        scratch_shapes=[pltpu.VMEM((tt, SCAN_LANES), F32), pltpu.VMEM((tt, SCAN_LANES), F32),
                        pltpu.VMEM((SCAN_ROWS, SCAN_LANES), F32)],
        compiler_params=_params(dimension_semantics=("arbitrary", "arbitrary")),
    )(dy, h, proj, bmat, cmat, lam)


def _ssm_params_fwd(a_re, a_im, log_dt, b_re, b_im):
    def body(ar_ref, ai_ref, ldt_ref, br_ref, bi_ref, lr_ref, li_ref, bbr_ref, bbi_ref):
        ar, ai, dt = ar_ref[...], ai_ref[...], jnp.exp(ldt_ref[...])
        mag = jnp.exp(ar * dt)
        lr, li = mag * jnp.cos(ai * dt), mag * jnp.sin(ai * dt)
        den = ar * ar + ai * ai
        cr = ((lr - 1.0) * ar + li * ai) / den
        ci = (li * ar - (lr - 1.0) * ai) / den
        br, bi = br_ref[...], bi_ref[...]
        lr_ref[...], li_ref[...] = lr, li
        bbr_ref[...] = cr * br - ci * bi
        bbi_ref[...] = cr * bi + ci * br

    n = a_re.shape[0]
    v1, v16 = jax.ShapeDtypeStruct((n, 1), F32), jax.ShapeDtypeStruct((n, SSM_GROUP), F32)
    return pl.pallas_call(body, name="ssm_params_fwd", out_shape=[v1, v1, v16, v16],
                          compiler_params=_params())(a_re, a_im, log_dt, b_re, b_im)


def _ssm_params_bwd(a_re, a_im, log_dt, b_re, b_im, g_lr, g_li, g_bbr, g_bbi):
    n = a_re.shape[0]

    def body(ar_ref, ai_ref, ldt_ref, br_ref, bi_ref, glr_ref, gli_ref, gbr_ref, gbi_ref,
             dar_ref, dai_ref, dldt_ref, dbr_ref, dbi_ref):
        ar, ai, dt = ar_ref[...], ai_ref[...], jnp.exp(ldt_ref[...])
        mag = jnp.exp(ar * dt)
        lr, li = mag * jnp.cos(ai * dt), mag * jnp.sin(ai * dt)
        den = ar * ar + ai * ai
        cr = ((lr - 1.0) * ar + li * ai) / den
        ci = (li * ar - (lr - 1.0) * ai) / den
        br, bi, gbr, gbi = br_ref[...], bi_ref[...], gbr_ref[...], gbi_ref[...]
        dbr_ref[...] = gbr * cr + gbi * ci
        dbi_ref[...] = gbi * cr - gbr * ci
        gcr = jnp.sum(gbr * br + gbi * bi, axis=1, keepdims=True)
        gci = jnp.sum(gbi * br - gbr * bi, axis=1, keepdims=True)
        ir, ii = ar / den, -ai / den
        glr = glr_ref[...] + gcr * ir + gci * ii
        gli = gli_ref[...] + gci * ir - gcr * ii
        qr, qi = cr * ir - ci * ii, cr * ii + ci * ir
        gar = -(gcr * qr + gci * qi)
        gai = -(gci * qr - gcr * qi)
        gxr = glr * lr + gli * li
        gxi = gli * lr - glr * li
        dar_ref[...] = gar + gxr * dt
        dai_ref[...] = gai + gxi * dt
        gdt = (gxr * ar + gxi * ai) * dt
        rowg = lax.broadcasted_iota(jnp.int32, (n, SSM_GROUPS), 0) // SSM_STATE
        colg = lax.broadcasted_iota(jnp.int32, (n, SSM_GROUPS), 1)
        dldt_ref[...] = jnp.sum(jnp.where(rowg == colg, gdt, 0.0), axis=0, keepdims=True)

    v1, v16 = jax.ShapeDtypeStruct((n, 1), F32), jax.ShapeDtypeStruct((n, SSM_GROUP), F32)
    return pl.pallas_call(body, name="ssm_params_bwd",
                          out_shape=[v1, v1, jax.ShapeDtypeStruct((1, SSM_GROUPS), F32), v16, v16],
                          compiler_params=_params())(a_re, a_im, log_dt, b_re, b_im, g_lr, g_li, g_bbr, g_bbi)


def _interleave(re, im, axis):
    shp = list(re.shape)
    new = shp[:axis] + [shp[axis] // LANES, LANES] + shp[axis + 1:]
    st = jnp.stack([re.reshape(new), im.reshape(new)], axis=axis + 1)
    return st.reshape(shp[:axis] + [2 * shp[axis]] + shp[axis + 1:])


def _deinterleave(v, axis):
    shp = list(v.shape)
    r = v.reshape(shp[:axis] + [shp[axis] // (2 * LANES), 2, LANES] + shp[axis + 1:])
    out = shp[:axis] + [shp[axis] // 2] + shp[axis + 1:]
    return (lax.index_in_dim(r, 0, axis + 1, keepdims=False).reshape(out),
            lax.index_in_dim(r, 1, axis + 1, keepdims=False).reshape(out))


def _b_matrix(bbr, bbi):
    per = SSM_GROUPS // SSM_BLOCKS
    eye = jnp.eye(per, dtype=F32)

    def blockdiag(v):
        x = v.reshape(SSM_BLOCKS, per, SSM_STATE, SSM_GROUP).transpose(0, 1, 3, 2)
        return (eye[None, :, None, :, None] * x[:, :, :, None, :]).reshape(SSM_GROUPS * SSM_GROUP, per * SSM_STATE)

    return _interleave(blockdiag(bbr), blockdiag(bbi), 1)


def _diag_blocks(v, rows, cols):
    per = SSM_GROUPS // SSM_BLOCKS
    x = v.reshape(SSM_BLOCKS, per, rows, per, cols) * jnp.eye(per, dtype=v.dtype)[None, :, None, :, None]
    return jnp.sum(x, axis=3).reshape(SSM_GROUPS, rows, cols)


def _b_matrix_grad(d):
    def diag(v):
        return _diag_blocks(v, SSM_GROUP, SSM_STATE).transpose(0, 2, 1).reshape(N_STATE, SSM_GROUP)

    dr, di = _deinterleave(d, 1)
    return diag(dr), diag(di)


def _c_matrix(c_re, c_im):
    per = SSM_GROUPS // SSM_BLOCKS
    eye = jnp.eye(per, dtype=F32)

    def blockdiag(v):
        x = v.reshape(SSM_BLOCKS, per, SSM_GROUP, SSM_STATE).transpose(0, 1, 3, 2)
        return (x[:, :, :, None, :] * eye[None, :, None, :, None]).reshape(N_STATE, per * SSM_GROUP)

    return _interleave(blockdiag(c_re), blockdiag(-c_im), 0)


def _c_matrix_grad(d):
    def diag(v):
        return _diag_blocks(v, SSM_STATE, SSM_GROUP).transpose(0, 2, 1)

    dr, di = _deinterleave(d, 0)
    return diag(dr), -diag(di)


def _row(v):
    return v.reshape(1, -1)


def _ssm_inputs(p):
    rows = lambda v: v.reshape(N_STATE, -1)
    ldt = jnp.repeat(p["ssm_log_dt"], SSM_STATE).reshape(N_STATE, 1)
    return rows(p["ssm_a_re"]), rows(p["ssm_a_im"]), ldt, rows(p["ssm_b_re"]), rows(p["ssm_b_im"])


def _lnmod(x, sc, sh):
    return _norm(x)[0] * (1.0 + sc) + sh


def _resid_ln(x, y, g, lg, lb):
    return _norm(ALPHA * x + (1.0 + g) * y)[0] * lg + lb


def _resid_ln_lnmod(x, y, g, lg, lb, sc, sh):
    xo = _resid_ln(x, y, g, lg, lb)
    return xo, _lnmod(xo, sc, sh)


def _layer_fwd(x, h1, mod, p, tag, next_mod):
    d = x.shape[1]
    sh_m, sc_m, g_m, sh_f, sc_f, g_f = [_row(mod[i]) for i in range(6)]
    nm = lambda s: f"{s}_{tag}"
    proj = _mm(nm("proj"), h1, p["w_in"], "nn", out_dtype=BF16)
    t = min(ATT_TILE, x.shape[0])
    kt3, vt3 = _kv_transposed(proj, t)
    att, car = _attn_fwd(proj, vt3, t)
    y_sb = _mm(nm("sb_up"), att, p["w_sb_up"], "nn", out_dtype=BF16)

    ssm_in = _ssm_inputs(p)
    lam_r, lam_i, bbr, bbi = _ssm_params_fwd(*ssm_in)
    lam = _interleave(lam_r.reshape(1, N_STATE), lam_i.reshape(1, N_STATE), 1)
    bmat = _b_matrix(bbr, bbi).astype(BF16)
    cmat = _c_matrix(p["ssm_c_re"], p["ssm_c_im"]).astype(BF16)
    hst, yc = _s5_fwd(proj, U_OFFSET, bmat, cmat, lam)

    def ssm_act(yc, u, dsk):
        y0 = yc + dsk * u
        return y0, _gelu(y0)

    y0, y1 = _rowwise(nm("ssm_act"), ssm_act, [(yc, 0, 512), (proj, 3, 512)], [_row(p["ssm_d"])], [(512, F32), (512, F32)])
    gl = _mm(nm("glu"), y1, p["w_glu"], "nn")
    y2 = _rowwise(nm("glu_act"), lambda y1, gl, b: y1 * jax.nn.sigmoid(gl + b), [(y1, 0, 512), (gl, 0, 512)],
                  [_row(p["b_glu"])], [(512, BF16)])
    y_ssm = _mm(nm("ssm_up"), y2, p["w_ssm_up"], "nn", out_dtype=BF16)

    def merge(gsb, gss, ysb, yss):
        return jax.nn.sigmoid(gsb) * ysb + jax.nn.sigmoid(gss) * yss

    merged = _rowwise(nm("merge"), merge, [(proj, 2, d), (proj, 3, d), (y_sb, 0, d), (y_ssm, 0, d)], [], [(d, BF16)])
    y = _mm(nm("out"), merged, p["w_out"], "nn")

    x1, h2 = _rowwise(nm("ln1"), _resid_ln_lnmod, [(x, 0, d), (y, 0, d)],
                      [g_m, _row(p["ln1_g"]), _row(p["ln1_b"]), sc_f, sh_f], [(d, F32), (d, BF16)])
    f_gate, f_up, act = _ffn_in(nm("ffn_in"), h2, p["w_ffn_in"])
    yf = _mm(nm("ffn_out"), act, p["w_ffn_out"], "nn")
    x2 = h1_next = None
    if next_mod is not None:
        x2, h1_next = _rowwise(nm("ln2"), _resid_ln_lnmod, [(x1, 0, d), (yf, 0, d)],
                               [g_f, _row(p["ln2_g"]), _row(p["ln2_b"]), next_mod[1], next_mod[0]], [(d, F32), (d, BF16)])
    saved = dict(x=x, h1=h1, proj=proj, ssm_in=ssm_in, kt3=kt3, car=car, att=att, y_sb=y_sb, lam=lam, bmat=bmat,
                 cmat=cmat, hst=hst, y0=y0, y1=y1, gl=gl, y2=y2, y_ssm=y_ssm, merged=merged, y=y, x1=x1, h2=h2, f_gate=f_gate, f_up=f_up,
                 act=act, yf=yf, t=t)
    return x2, h1_next, saved


def _resid_ln_bwd(x, y, dxo, g, lg):
    n, rstd = _norm(ALPHA * x + (1.0 + g) * y)
    dr = _norm_bwd(dxo * lg, n, rstd)
    return ALPHA * dr, (1.0 + g) * dr, _colsum(dxo * n), _colsum(dxo), _colsum(dr * y)


def _lnmod_bwd(x, dh, dxa, sc):
    n, rstd = _norm(x)
    return dxa + _norm_bwd(dh * (1.0 + sc), n, rstd), _colsum(dh * n), _colsum(dh)


def _lnmod_resid_ln_bwd(xo, dh, dxa, x, y, sc, g, lg):
    dxo, dsc, dsh = _lnmod_bwd(xo, dh, dxa, sc)
    dx, dy, dlg, dlb, dg = _resid_ln_bwd(x, y, dxo, g, lg)
    return dx, dy, dsc, dsh, dlg, dlb, dg


def _layer_bwd(dx1a, dyf, mod, p, sv, layer, depth, stacked):
    d = dx1a.shape[1]
    sh_m, sc_m, g_m, sh_f, sc_f, g_f = [_row(mod[i]) for i in range(6)]
    nm = lambda s: f"{s}_{layer}"
    grads = {}

    def weight_grad(n, a, b, **kw):
        grads[n] = _mm(nm("d" + n), a, b, "tn", out_dtype=BF16, into=(stacked.get(n), layer, depth), **kw)

    dact = _mm(nm("d_act"), dyf, p["w_ffn_out"], "nt", out_dtype=BF16)
    weight_grad("w_ffn_out", sv["act"], dyf)
    fh = sv["f_gate"].shape[1]

    def swiglu_bwd(g, u, da):
        sg = jax.nn.sigmoid(g)
        return jnp.concatenate([da * u * sg * (1.0 + g * (1.0 - sg)), da * g * sg], axis=1)

    df = _rowwise(nm("swiglu_bwd"), swiglu_bwd, [(sv["f_gate"], 0, fh), (sv["f_up"], 0, fh), (dact, 0, fh)], [], [(2 * fh, BF16)])
    dh2 = _mm(nm("d_h2"), df, p["w_ffn_in"], "nt")
    weight_grad("w_ffn_in", sv["h2"], df)
    dxa, dy, dsc_f, dsh_f, grads["ln1_g"], grads["ln1_b"], dg_m = _rowwise(
        nm("ln1_bwd"), _lnmod_resid_ln_bwd, [(sv["x1"], 0, d), (dh2, 0, d), (dx1a, 0, d), (sv["x"], 0, d), (sv["y"], 0, d)],
        [sc_f, g_m, _row(p["ln1_g"])], [(d, F32), (d, BF16)], [d] * 5)
    dmerged = _mm(nm("d_merged"), dy, p["w_out"], "nt", out_dtype=BF16)
    weight_grad("w_out", sv["merged"], dy)

    def merge_bwd(gsb, gss, ysb, yss, dm):
        s1, s2 = jax.nn.sigmoid(gsb), jax.nn.sigmoid(gss)
        return s1 * dm, s2 * dm, dm * ysb * s1 * (1.0 - s1), dm * yss * s2 * (1.0 - s2)

    dy_sb, dy_ssm, dg_sb, dg_ssm = _rowwise(
        nm("merge_bwd"), merge_bwd, [(sv["proj"], 2, d), (sv["proj"], 3, d), (sv["y_sb"], 0, d), (sv["y_ssm"], 0, d),
                                     (dmerged, 0, d)], [], [(d, BF16)] * 4)
    dy2 = _mm(nm("d_y2"), dy_ssm, p["w_ssm_up"], "nt")
    weight_grad("w_ssm_up", sv["y2"], dy_ssm)

    def glu_act_bwd(y1, gl, dy2, b):
        sg = jax.nn.sigmoid(gl + b)
        dgl = dy2 * y1 * sg * (1.0 - sg)
        return dy2 * sg, dgl, _colsum(dgl)

    dy1a, dgl, grads["b_glu"] = _rowwise(nm("glu_act_bwd"), glu_act_bwd, [(sv["y1"], 0, 512), (sv["gl"], 0, 512), (dy2, 0, 512)],
                                         [_row(p["b_glu"])], [(512, F32), (512, BF16)], [512])
    dy1b = _mm(nm("d_y1"), dgl, p["w_glu"], "nt")
    weight_grad("w_glu", sv["y1"], dgl)

    def ssm_act_bwd(y0, u, dy1a, dy1b, dsk):
        dy0 = (dy1a + dy1b) * _gelu_grad(y0)
        return dy0, dsk * dy0, _colsum(dy0 * u)

    dy0, du_a, grads["ssm_d"] = _rowwise(nm("ssm_act_bwd"), ssm_act_bwd,
                                         [(sv["y0"], 0, 512), (sv["proj"], 3, 512), (dy1a, 0, 512), (dy1b, 0, 512)],
                                         [_row(p["ssm_d"])], [(512, BF16), (512, F32)], [512])
    du_b, dlam, d_bmat, d_cmat = _s5_bwd(dy0, sv["hst"], sv["proj"], U_OFFSET, sv["bmat"], sv["cmat"], sv["lam"])
    grads["ssm_c_re"], grads["ssm_c_im"] = _c_matrix_grad(d_cmat)
    g_bbr, g_bbi = _b_matrix_grad(d_bmat)
    g_lr, g_li = _deinterleave(dlam[0:1], 1)
    da_re, da_im, dldt, db_re, db_im = _ssm_params_bwd(*sv["ssm_in"], g_lr.reshape(N_STATE, 1), g_li.reshape(N_STATE, 1),
                                                       g_bbr, g_bbi)
    grads["ssm_a_re"] = da_re.reshape(SSM_GROUPS, SSM_STATE)
    grads["ssm_a_im"] = da_im.reshape(SSM_GROUPS, SSM_STATE)
    grads["ssm_log_dt"] = dldt.reshape(SSM_GROUPS)
    grads["ssm_b_re"] = db_re.reshape(SSM_GROUPS, SSM_STATE, SSM_GROUP)
    grads["ssm_b_im"] = db_im.reshape(SSM_GROUPS, SSM_STATE, SSM_GROUP)
    datt = _mm(nm("d_att"), dy_sb, p["w_sb_up"], "nt", out_dtype=BF16)
    weight_grad("w_sb_up", sv["att"], dy_sb)
    dqs, dk, dv = _attn_bwd(sv["proj"], datt, sv["kt3"], sv["car"], sv["t"])

    def dproj_cols(dqs, dk, dv, dua, dub, dgsb, dgss):
        return jnp.concatenate([dqs * (1.0 / math.sqrt(HEAD_DIM)), dk, dv, dua + dub, dgsb.astype(F32), dgss.astype(F32)],
                               axis=1)

    dproj = _rowwise(nm("dproj"), dproj_cols, [(dqs, 0, 512), (dk, 0, 512), (dv, 0, 512), (du_a, 0, 512), (du_b, 0, 512),
                                               (dg_sb, 0, d), (dg_ssm, 0, d)], [], [(2048 + 2 * d, BF16)])
    dh1 = _mm(nm("d_h1"), dproj, p["w_in"], "nt")
    weight_grad("w_in", sv["h1"], dproj)
    for k in ("ln1_g", "ln1_b", "ssm_d", "b_glu"):
        grads[k] = grads[k].reshape(-1)
    return dh1, dxa, grads, (dg_m, dsh_f, dsc_f)


def _local_step(x, target, mod, layer_w):
    depth, d = len(layer_w), x.shape[1]
    rows = lambda l: [_row(mod[l][i]) for i in range(6)]
    h1 = _rowwise("lnmod1_0", _lnmod, [(x, 0, d)], [rows(0)[1], rows(0)[0]], [(d, BF16)])
    xs, saved = x, []
    for l in range(depth):
        xs, h1, sv = _layer_fwd(xs, h1, mod[l], layer_w[l], str(l), rows(l + 1)[:2] if l + 1 < depth else None)
        saved.append(sv)

    def head_bwd(x1, yf, tgt, g, lg, lb):
        err = _resid_ln(x1, yf, g, lg, lb) - tgt
        return _resid_ln_bwd(x1, yf, err * (1.0 / d), g, lg) + (_colsum(err * err) * (0.5 / d),)

    def boundary_bwd(dh, dxa, x1, yf, sc, g, lg, lb):
        dxo, dsc, dsh = _lnmod_bwd(_resid_ln(x1, yf, g, lg, lb), dh, dxa, sc)
        return _resid_ln_bwd(x1, yf, dxo, g, lg) + (dsc, dsh)

    lgrads, sums, stacked = [None] * depth, [dict() for _ in range(depth)], {}
    last, p = saved[-1], layer_w[-1]
    dx1a, dyf, dlg, dlb, dg_f, loss_cols = _rowwise(
        "head_bwd", head_bwd, [(last["x1"], 0, d), (last["yf"], 0, d), (target, 0, d)],
        [rows(depth - 1)[5], _row(p["ln2_g"]), _row(p["ln2_b"])], [(d, F32), (d, BF16)], [d] * 4)
    for l in reversed(range(depth)):
        sums[l]["g_f"] = dg_f
        dh1, dxa, lgrads[l], (sums[l]["g_m"], sums[l]["sh_f"], sums[l]["sc_f"]) = _layer_bwd(
            dx1a, dyf, mod[l], layer_w[l], saved[l], l, depth, stacked)
        lgrads[l]["ln2_g"], lgrads[l]["ln2_b"] = dlg.reshape(-1), dlb.reshape(-1)
        stacked = {n: lgrads[l][n] for n in COL_SPLIT + ROW_SPLIT}
        if l > 0:
            prev, p = saved[l - 1], layer_w[l - 1]
            dx1a, dyf, dlg, dlb, dg_f, sums[l]["sc_m"], sums[l]["sh_m"] = _rowwise(
                f"boundary_bwd_{l}", boundary_bwd, [(dh1, 0, d), (dxa, 0, d), (prev["x1"], 0, d), (prev["yf"], 0, d)],
                [rows(l)[1], rows(l - 1)[5], _row(p["ln2_g"]), _row(p["ln2_b"])], [(d, F32), (d, BF16)], [d] * 5)
        else:
            dx, sums[l]["sc_m"], sums[l]["sh_m"] = _rowwise("lnmod1_bwd", _lnmod_bwd, [(x, 0, d), (dh1, 0, d), (dxa, 0, d)],
                                                            [rows(0)[1]], [(d, F32)], [d, d])
    dmod = jnp.stack([jnp.concatenate([sums[l][k] for k in ("sh_m", "sc_m", "g_m", "sh_f", "sc_f", "g_f")], axis=0)
                      for l in range(depth)])
    return loss_cols, dx, dmod, lgrads, stacked


def _place():
    return lax.axis_index("x"), lax.axis_index("y"), lax.axis_index("c")


def _all_gather8(name, block):
    m_per, n = block.shape

    def body(x_ref, out_ref, send_sems, recv_sems, local_sem):
        x, y, c = _place()
        me, sibling = (x, y, c), (x, y, 1 - c)
        chips = [(1 - x, y), (x, 1 - y), (1 - x, 1 - y)]

        def rows(px, py, pc):
            return out_ref.at[pl.ds(pl.multiple_of((4 * px + 2 * py + pc) * m_per, 8), m_per), :]

        def copy(k, blk, to, src=None):
            return pltpu.make_async_remote_copy(src_ref=rows(*blk) if src is None else src, dst_ref=rows(*blk),
                                                send_sem=send_sems.at[k], recv_sem=recv_sems.at[k],
                                                device_id=to, device_id_type=MESH)

        mine = pltpu.make_async_copy(x_ref, rows(*me), local_sem)
        mine.start()
        first = [copy(0, me, sibling, src=x_ref)] + [copy(1 + j, me, (*chip, c), src=x_ref) for j, chip in enumerate(chips)]
        for cp in first:
            cp.start()
        passed = [copy(4 + j, (*chip, c), sibling) for j, chip in enumerate(chips)]
        for j, chip in enumerate(chips):
            copy(1 + j, (*chip, c), me).wait_recv()
            passed[j].start()
        copy(0, sibling, me).wait_recv()
        for j, chip in enumerate(chips):
            copy(4 + j, (*chip, 1 - c), me).wait_recv()
        for cp in first + passed:
            cp.wait_send()
        mine.wait()

    return pl.pallas_call(
        body, name=name, out_shape=jax.ShapeDtypeStruct((8 * m_per, n), block.dtype),
        in_specs=[pl.BlockSpec(memory_space=pltpu.VMEM)], out_specs=pl.BlockSpec(memory_space=pltpu.VMEM),
        scratch_shapes=[pltpu.SemaphoreType.DMA((7,)), pltpu.SemaphoreType.DMA((7,)), pltpu.SemaphoreType.DMA],
        compiler_params=_params(),
    )(block)


def _other_chips(x, y):
    return [(1 - x, y), (x, 1 - y), (1 - x, 1 - y)]


def _gather_weights(whole, by_rows):
    n = len(whole)

    def body(*refs):
        dst = refs[n:2 * n]
        ici_send, ici_recv, d2d_send, d2d_recv = refs[2 * n:]
        x, y, c = _place()
        chips = _other_chips(x, y)

        def part(ref, k, px, py, pc):
            _, r, cols = whole[k].shape
            q = 2 * px + py
            if by_rows[k]:
                return ref[k].at[:, pl.ds(pl.multiple_of((2 * q + pc) * (r // 8), 16), r // 8), :]
            return ref[k].at[:, pl.ds(pl.multiple_of(pc * (r // 2), 16), r // 2),
                             pl.ds(pl.multiple_of(q * (cols // 4), LANES), cols // 4)]

        def ici(k, j, px, py, to):
            return pltpu.make_async_remote_copy(src_ref=part(dst, k, px, py, c), dst_ref=part(dst, k, px, py, c),
                                                send_sem=ici_send.at[k, j], recv_sem=ici_recv.at[k, j],
                                                device_id=(*to, c), device_id_type=MESH)

        def d2d(k, j, px, py, pc):
            return pltpu.make_async_remote_copy(src_ref=part(dst, k, px, py, pc), dst_ref=part(dst, k, px, py, pc),
                                                send_sem=d2d_send.at[k, j], recv_sem=d2d_recv.at[k, j],
                                                device_id=(x, y, 1 - c), device_id_type=MESH)

        for k in range(n):
            for j, chip in enumerate(chips):
                ici(k, j, x, y, chip).start()
        for k in range(n):
            for j, chip in enumerate(chips):
                ici(k, j, *chip, chip).wait_recv()
                d2d(k, j, *chip, c).start()
        for k in range(n):
            for j, chip in enumerate(chips):
                d2d(k, j, *chip, 1 - c).wait_recv()
        for k in range(n):
            for j, chip in enumerate(chips):
                ici(k, j, x, y, chip).wait_send()
                d2d(k, j, *chip, c).wait_send()

    any_spec = pl.BlockSpec(memory_space=pl.ANY)
    return pl.pallas_call(
        body, name="gather_weights", in_specs=[any_spec] * n, out_specs=[any_spec] * n,
        out_shape=[jax.ShapeDtypeStruct(a.shape, a.dtype) for a in whole], input_output_aliases={k: k for k in range(n)},
        scratch_shapes=[pltpu.SemaphoreType.DMA((n, 3))] * 4,
        compiler_params=_params(),
    )(*whole)


def _part_shape(shape, by_rows):
    l, r, c = shape
    return (l, r // 8, c) if by_rows else (l, r // 2, c // 4)


def _pair_exchange(grads, by_rows):
    n = len(grads)

    def body(*refs):
        src, dst = refs[:n], refs[n:2 * n]
        send_sems, recv_sems = refs[2 * n:]
        x, y, c = _place()

        def window(k, q, pc):
            _, hr, hc = _part_shape(grads[k].shape, by_rows[k])
            if by_rows[k]:
                return src[k].at[:, pl.ds(pl.multiple_of((2 * q + pc) * hr, 16), hr), :]
            return src[k].at[:, pl.ds(pl.multiple_of(pc * hr, 16), hr), pl.ds(q * hc, hc)]

        def copy(k, q, pc):
            return pltpu.make_async_remote_copy(src_ref=window(k, q, pc), dst_ref=dst[k].at[q], send_sem=send_sems.at[k, q],
                                                recv_sem=recv_sems.at[k, q], device_id=(x, y, 1 - c), device_id_type=MESH)

        for k in range(n):
            for q in range(4):
                copy(k, q, 1 - c).start()
        for k in range(n):
            for q in range(4):
                copy(k, q, c).wait_recv()
        for k in range(n):
            for q in range(4):
                copy(k, q, 1 - c).wait_send()

    any_spec = pl.BlockSpec(memory_space=pl.ANY)
    return pl.pallas_call(
        body, name="pair_exchange", in_specs=[any_spec] * n, out_specs=[any_spec] * n,
        out_shape=[jax.ShapeDtypeStruct((4, *_part_shape(g.shape, rows)), g.dtype) for g, rows in zip(grads, by_rows)],
        scratch_shapes=[pltpu.SemaphoreType.DMA((n, 4)), pltpu.SemaphoreType.DMA((n, 4))],
        compiler_params=_params(),
    )(*grads)


def _pair_sum(name, g, theirs, by_rows, c, chip):
    _, l, hr, hc = theirs.shape
    tr = _pick(hr, (256, 176, 128, 64, 32))

    def body(s_ref, g_ref, t_ref, p_ref, own_ref):
        v = (g_ref[...].astype(F32) + t_ref[0].astype(F32)).astype(BF16)
        p_ref[0] = v

        @pl.when(pl.program_id(2) == s_ref[1])
        def _():
            own_ref[0] = v

    if by_rows:
        g_spec = pl.BlockSpec((1, tr, hc), lambda li, i, q, s: (li, (2 * q + s[0]) * (hr // tr) + i, 0))
    else:
        g_spec = pl.BlockSpec((1, tr, hc), lambda li, i, q, s: (li, s[0] * (hr // tr) + i, q))
    slot = pl.BlockSpec((1, 1, tr, hc), lambda li, i, q, s: (q, li, i, 0))
    grid_spec = pltpu.PrefetchScalarGridSpec(
        num_scalar_prefetch=1, grid=(l, hr // tr, 4), in_specs=[g_spec, slot],
        out_specs=[slot, pl.BlockSpec((1, 1, tr, hc), lambda li, i, q, s: (s[1], li, i, 0))])
    return pl.pallas_call(
        body, name=name, grid_spec=grid_spec, out_shape=[jax.ShapeDtypeStruct(theirs.shape, BF16)] * 2,
        compiler_params=_params(dimension_semantics=("arbitrary", "arbitrary", "arbitrary")),
    )(jnp.stack([c, chip]).astype(jnp.int32), g, theirs)


def _chip_scatter(sums, landing):
    n = len(sums)

    def body(*refs):
        src, dst = refs[:n], refs[2 * n:3 * n]
        send_sems, recv_sems = refs[3 * n:]
        x, y, c = _place()
        mine = 2 * x + y

        def copy(k, j, src_slot, dst_slot, to):
            return pltpu.make_async_remote_copy(src_ref=src[k].at[src_slot], dst_ref=dst[k].at[dst_slot],
                                                send_sem=send_sems.at[k, j], recv_sem=recv_sems.at[k, j],
                                                device_id=(*to, c), device_id_type=MESH)

        chips = _other_chips(x, y)
        for k in range(n):
            for j, (px, py) in enumerate(chips):
                copy(k, j, 2 * px + py, mine, (px, py)).start()
        for k in range(n):
            for j, (px, py) in enumerate(chips):
                copy(k, j, mine, 2 * px + py, (px, py)).wait_recv()
        for k in range(n):
            for j, (px, py) in enumerate(chips):
                copy(k, j, 2 * px + py, mine, (px, py)).wait_send()

    any_spec = pl.BlockSpec(memory_space=pl.ANY)
    return pl.pallas_call(
        body, name="chip_scatter", in_specs=[any_spec] * (2 * n), out_specs=[any_spec] * n,
        out_shape=[jax.ShapeDtypeStruct(a.shape, a.dtype) for a in landing],
        input_output_aliases={n + k: k for k in range(n)},
        scratch_shapes=[pltpu.SemaphoreType.DMA((n, 3)), pltpu.SemaphoreType.DMA((n, 3))],
        compiler_params=_params(),
    )(*sums, *landing)


def _sum_slots(name, parts, half=None):
    slots, l, r, c = parts.shape
    tr = _pick(r, (256, 176, 128, 64, 32, 8))

    def body(*refs):
        p_ref, o_ref = refs[-2:]
        acc = p_ref[0].astype(F32)
        for i in range(1, slots):
            acc = acc + p_ref[i].astype(F32)
        o_ref[...] = acc

    if half is None:
        return pl.pallas_call(
            body, name=name, grid=(l, r // tr), in_specs=[pl.BlockSpec((slots, 1, tr, c), lambda li, i: (0, li, i, 0))],
            out_specs=pl.BlockSpec((1, tr, c), lambda li, i: (li, i, 0)), out_shape=jax.ShapeDtypeStruct((l, r, c), F32),
            compiler_params=_params(dimension_semantics=("arbitrary", "arbitrary")),
        )(parts)
    grid_spec = pltpu.PrefetchScalarGridSpec(
        num_scalar_prefetch=1, grid=(l, r // tr),
        in_specs=[pl.BlockSpec((slots, 1, tr, c), lambda li, i, h: (0, li, i, 0))],
        out_specs=pl.BlockSpec((1, tr, c), lambda li, i, h: (li, h[0] * (r // tr) + i, 0)))
    return pl.pallas_call(
        body, name=name, grid_spec=grid_spec, out_shape=jax.ShapeDtypeStruct((l, 2 * r, c), F32),
        compiler_params=_params(dimension_semantics=("arbitrary", "arbitrary")),
    )(jnp.reshape(half, (1,)).astype(jnp.int32), parts)


def _swap_halves(blocks):
    n = len(blocks)

    def body(*refs):
        src, dst = refs[:n], refs[n:2 * n]
        send_sems, recv_sems = refs[2 * n:]
        x, y, c = _place()

        def half(ref, k, pc):
            r = blocks[k].shape[1] // 2
            return ref[k].at[:, pl.ds(pl.multiple_of(pc * r, 8), r), :]

        def copy(k, pc):
            return pltpu.make_async_remote_copy(src_ref=half(src, k, pc), dst_ref=half(dst, k, pc), send_sem=send_sems.at[k],
                                                recv_sem=recv_sems.at[k], device_id=(x, y, 1 - c), device_id_type=MESH)

        for k in range(n):
            copy(k, c).start()
        for k in range(n):
            copy(k, 1 - c).wait_recv()
        for k in range(n):
            copy(k, c).wait_send()

    any_spec = pl.BlockSpec(memory_space=pl.ANY)
    return pl.pallas_call(
        body, name="swap_halves", in_specs=[any_spec] * n, out_specs=[any_spec] * n,
        out_shape=[jax.ShapeDtypeStruct(b.shape, b.dtype) for b in blocks], input_output_aliases={k: k for k in range(n)},
        scratch_shapes=[pltpu.SemaphoreType.DMA((n,)), pltpu.SemaphoreType.DMA((n,))],
        compiler_params=_params(),
    )(*blocks)


def _adamw(name, w, g, m, v):
    shape = w.shape
    cols = shape[-1]
    flat = lambda a: a.reshape(-1, cols)
    rows = w.size // cols
    tr = _pick(rows, [r for r in (512, 256, 128, 64, 32, 16, 8) if r * cols <= 256 * 1024]) if rows % 8 == 0 else rows

    def body(w_ref, g_ref, m_ref, v_ref, go_ref, d_ref, nm_ref, nv_ref):
        gg = g_ref[...]
        go_ref[...] = gg
        nm = ADAM_B1 * m_ref[...] + (1.0 - ADAM_B1) * gg
        nv = ADAM_B2 * v_ref[...] + (1.0 - ADAM_B2) * (gg * gg)
        m_hat = nm / (1.0 - ADAM_B1 ** ADAM_STEP)
        v_hat = nv / (1.0 - ADAM_B2 ** ADAM_STEP)
        d_ref[...] = -ADAM_LR * (m_hat / (jnp.sqrt(v_hat) + ADAM_EPS) + ADAM_WD * w_ref[...])
        nm_ref[...] = nm
        nv_ref[...] = nv

    spec = pl.BlockSpec((tr, cols), lambda i: (i, 0))
    out = pl.pallas_call(
        body, name=name, grid=(rows // tr,), in_specs=[spec] * 4, out_specs=[spec] * 4,
        out_shape=[jax.ShapeDtypeStruct((rows, cols), F32)] * 4,
        compiler_params=_params(dimension_semantics=("arbitrary",)),
    )(flat(w), flat(g), flat(m), flat(v))
    return tuple(o.reshape(shape) for o in out)


WEIGHTS = ["w_ada", "b_ada", "w_in", "w_sb_up", "ssm_a_re", "ssm_a_im", "ssm_log_dt", "ssm_b_re", "ssm_b_im", "ssm_c_re",
           "ssm_c_im", "ssm_d", "w_glu", "b_glu", "w_ssm_up", "w_out", "ln1_g", "ln1_b", "w_ffn_in", "w_ffn_out", "ln2_g",
           "ln2_b"]
COL_SPLIT = ["w_in", "w_sb_up", "w_ssm_up", "w_ffn_in"]
ROW_SPLIT = ["w_glu", "w_out", "w_ffn_out"]
SMALL = ["ssm_a_re", "ssm_a_im", "ssm_log_dt", "ssm_b_re", "ssm_b_im", "ssm_c_re", "ssm_c_im", "ssm_d", "b_glu", "ln1_g",
         "ln1_b", "ln2_g", "ln2_b"]
SLAB_COLS = 1024


def _cast_into_whole(name, w, by_rows, chip):
    l, r, cols = w.shape
    tr = _pick(r, (512, 256, 128, 64, 16))

    def body(q_ref, w_ref, o_ref):
        o_ref[...] = w_ref[...].astype(BF16)

    if by_rows:
        out_map, shape = (lambda li, i, q: (li, q[0] * (r // tr) + i, 0)), (l, 4 * r, cols)
    else:
        out_map, shape = (lambda li, i, q: (li, i, q[0])), (l, r, 4 * cols)
    grid_spec = pltpu.PrefetchScalarGridSpec(
        num_scalar_prefetch=1, grid=(l, r // tr), in_specs=[pl.BlockSpec((1, tr, cols), lambda li, i, q: (li, i, 0))],
        out_specs=pl.BlockSpec((1, tr, cols), out_map))
    return pl.pallas_call(body, name=name, grid_spec=grid_spec, out_shape=jax.ShapeDtypeStruct(shape, BF16),
                          compiler_params=_params(dimension_semantics=("arbitrary", "arbitrary")),
                          )(jnp.reshape(chip, (1,)).astype(jnp.int32), w)


def _silu_rows(name, c):
    def body(c_ref, o_ref):
        v = c_ref[...]
        o_ref[...] = v * jax.nn.sigmoid(v)

    return pl.pallas_call(body, name=name, out_shape=jax.ShapeDtypeStruct(c.shape, F32), compiler_params=_params())(c)


def _pad_rows(v, mult=8):
    flat = v.reshape(-1)
    per = mult * SLAB_COLS
    total = -(-flat.size // per) * per
    return jnp.pad(flat, (0, total - flat.size)).reshape(-1, SLAB_COLS)


def kernel(x, c, w_ada, b_ada, w_in, w_sb_up, ssm_a_re, ssm_a_im, ssm_log_dt, ssm_b_re, ssm_b_im, ssm_c_re, ssm_c_im, ssm_d, w_glu, b_glu, w_ssm_up, w_out, ln1_g, ln1_b, w_ffn_in, w_ffn_out, ln2_g, ln2_b, loss_target, m_w_ada, m_b_ada, m_w_in, m_w_sb_up, m_ssm_a_re, m_ssm_a_im, m_ssm_log_dt, m_ssm_b_re, m_ssm_b_im, m_ssm_c_re, m_ssm_c_im, m_ssm_d, m_w_glu, m_b_glu, m_w_ssm_up, m_w_out, m_ln1_g, m_ln1_b, m_w_ffn_in, m_w_ffn_out, m_ln2_g, m_ln2_b, v_w_ada, v_b_ada, v_w_in, v_w_sb_up, v_ssm_a_re, v_ssm_a_im, v_ssm_log_dt, v_ssm_b_re, v_ssm_b_im, v_ssm_c_re, v_ssm_c_im, v_ssm_d, v_w_glu, v_b_glu, v_w_ssm_up, v_w_out, v_ln1_g, v_ln1_b, v_w_ffn_in, v_w_ffn_out, v_ln2_g, v_ln2_b):
    args = dict(locals())
    w = {n: args[n] for n in WEIGHTS}
    mom = {n: args["m_" + n] for n in WEIGHTS}
    var = {n: args["v_" + n] for n in WEIGHTS}
    depth, d = w_ada.shape[0], x.shape[-1]
    xi, yi, ci = _place()
    me, chip = 4 * xi + 2 * yi + ci, 2 * xi + yi
    ada_cols = w_ada.shape[2]

    big = COL_SPLIT + ROW_SPLIT
    by_rows = [n in ROW_SPLIT for n in big]
    full = dict(zip(big, _gather_weights([_cast_into_whole(f"cast_{n}", w[n], n in ROW_SPLIT, chip) for n in big], by_rows)))

    c_all = _all_gather8("gather_c", jnp.pad(c, ((0, 7), (0, 0))))[::8]
    c_act = _silu_rows("silu_c", c_all)
    b_cols = lax.dynamic_slice_in_dim(b_ada, chip * ada_cols, ada_cols, axis=1)
    mod_part = jnp.concatenate([_small_mm(f"mod_{l}", c_act, w_ada[l], "nn") + b_cols[l][None] for l in range(depth)], axis=0)
    mod_all = _all_gather8("gather_mod", mod_part).reshape(4, 2, depth, 8, ada_cols)[:, 0]
    mod_mine = lax.dynamic_index_in_dim(mod_all, me, axis=2, keepdims=False)
    mod = mod_mine.transpose(1, 0, 2).reshape(depth, 6, d)

    layer_w = [{**{n: (full[n], l) for n in big}, **{n: w[n][l] for n in SMALL}} for l in range(depth)]
    loss_cols, dx, dmods, lgrads, stacked = _local_step(x[0], loss_target[0], mod, layer_w)
    loss = lax.psum(jnp.sum(loss_cols), ("x", "y", "c"))
    grad_x = dx[None]

    theirs = _pair_exchange([stacked[n] for n in big], by_rows)
    pairs = [_pair_sum(f"pair_{n}", stacked[n], t, n in ROW_SPLIT, ci, chip) for n, t in zip(big, theirs)]
    landed = _chip_scatter([p[0] for p in pairs], [p[1] for p in pairs])
    halves = [_sum_slots(f"sum_{n}", p, half=ci) for n, p in zip(big, landed)]
    grad = dict(zip(big, _swap_halves(halves)))

    pieces = [dmods] + [jnp.stack([lgrads[l][n] for l in range(depth)]) for n in SMALL]
    slab = _pad_rows(jnp.concatenate([p.reshape(-1) for p in pieces]))
    slabs = _all_gather8("gather_small", slab).reshape(8, 1, *slab.shape)
    total = _sum_slots("sum_small", slabs)[0].reshape(-1)
    at = dmods.size
    grad["b_ada"] = total[:at].reshape(depth, 6 * d)
    for n, p in zip(SMALL, pieces[1:]):
        grad[n] = total[at:at + p.size].reshape(p.shape)
        at += p.size
    dmod_all = slabs.reshape(8, -1)[:, :dmods.size].reshape(8, depth, 4, ada_cols)
    dmod_cols = lax.dynamic_index_in_dim(dmod_all, chip, axis=2, keepdims=False)
    grad["w_ada"] = jnp.stack([_small_mm(f"dw_ada_{l}", c_act, dmod_cols[:, l], "tn") for l in range(depth)])

    delta, new_m, new_v = {}, {}, {}
    for n in WEIGHTS:
        grad[n], delta[n], new_m[n], new_v[n] = _adamw(f"adamw_{n}", w[n], grad[n], mom[n], var[n])
    return (loss, grad_x, *[grad[n] for n in WEIGHTS], *[delta[n] for n in WEIGHTS], *[new_m[n] for n in WEIGHTS],
            *[new_v[n] for n in WEIGHTS])
```

```python
import functools
import math

import jax
import jax.numpy as jnp
from jax import lax
from jax.experimental import pallas as pl
from jax.experimental.pallas import tpu as pltpu

F32 = jnp.float32
BF16 = jnp.bfloat16
MESH = pl.DeviceIdType.MESH

LANES = 128
HEAD_DIM = 64
SB_WIDTH = 512
ATT_TILE = 256
SSM_GROUPS, SSM_STATE, SSM_GROUP = 32, 64, 16
N_STATE = SSM_GROUPS * SSM_STATE
SSM_BLOCKS = SSM_GROUPS * SSM_GROUP // LANES
U_OFFSET = 3 * 512
LN_EPS = 1e-5
DEPTH = 2
ALPHA = (2 * DEPTH) ** 0.25
ADAM_LR, ADAM_B1, ADAM_B2, ADAM_EPS, ADAM_WD, ADAM_STEP = 0.001, 0.9, 0.999, 1e-08, 0.01, 10
VMEM_LIMIT = 56 * 1024 * 1024
GELU_K = math.sqrt(2.0 / math.pi)
GELU_C = 0.044715


def _params(**kw):
    return pltpu.CompilerParams(vmem_limit_bytes=VMEM_LIMIT, **kw)


def _pick(n, prefs):
    for p in prefs:
        if n % p == 0:
            return p
    return n


ROWWISE_TILES = ((1024, 3 * 1024), (512, 10 * 1024), (256, 1 << 30), (128, 1 << 30), (64, 1 << 30), (8, 1 << 30))


def _rowwise(name, fn, rows, vecs, outs, sums=(), tm=None):
    s = rows[0][0].shape[0]
    width = sum(w for _, _, w in rows) + sum(w for w, _ in outs)
    tm = tm or _pick(s, [t for t, most in ROWWISE_TILES if width <= most])
    nin, no, ns = len(rows) + len(vecs), len(outs), len(sums)

    def body(*refs):
        res = fn(*[r[...].astype(F32) for r in refs[:nin]])
        res = res if isinstance(res, tuple) else (res,)
        for r, v in zip(refs[nin:nin + no], res[:no]):
            r[...] = v.astype(r.dtype)
        if ns:
            @pl.when(pl.program_id(0) == 0)
            def _():
                for r in refs[nin + no:]:
                    r[...] = jnp.zeros_like(r)
            for r, v in zip(refs[nin + no:], res[no:]):
                r[...] += v

    in_specs = [pl.BlockSpec((tm, w), lambda i, cb=cb: (i, cb)) for _, cb, w in rows]
    in_specs += [pl.BlockSpec(v.shape, lambda i: (0, 0)) for v in vecs]
    out_specs = [pl.BlockSpec((tm, w), lambda i: (i, 0)) for w, _ in outs]
    out_specs += [pl.BlockSpec((1, w), lambda i: (0, 0)) for w in sums]
    out_shape = [jax.ShapeDtypeStruct((s, w), dt) for w, dt in outs]
    out_shape += [jax.ShapeDtypeStruct((1, w), F32) for w in sums]
    res = pl.pallas_call(
        body, name=name, grid=(s // tm,), in_specs=in_specs, out_specs=out_specs, out_shape=out_shape,
        compiler_params=_params(dimension_semantics=("arbitrary",)),
    )(*[a for a, _, _ in rows], *vecs)
    return res[0] if len(res) == 1 else tuple(res)


MM_TILES = (1408, 1024, 512, 256, 128)


def _slab_spec(block, index, slab):
    if slab is None:
        return pl.BlockSpec(block, index)
    return pl.BlockSpec((None, *block), lambda *g: (slab, *index(*g)))


def _mm(name, a, b, mode, out_dtype=F32, into=None):
    b, b_slab = b if isinstance(b, tuple) else (b, None)
    if mode == "nn":
        m, k, n = a.shape[0], a.shape[1], b.shape[-1]
    elif mode == "nt":
        m, k, n = a.shape[0], a.shape[1], b.shape[-2]
    else:
        k, m, n = a.shape[0], a.shape[1], b.shape[-1]
    tm = _pick(m, MM_TILES if mode == "tn" else (2048,) + MM_TILES[1:])
    tn = _pick(n, MM_TILES)
    tk = _pick(k, (2048,) + MM_TILES if mode == "tn" else MM_TILES)
    nk = k // tk
    dims = {"nn": ((1,), (0,)), "nt": ((1,), (1,)), "tn": ((0,), (0,))}[mode]

    def body(a_ref, b_ref, *rest):
        o_ref = rest[-2] if nk > 1 else rest[-1]
        prod = lax.dot_general(a_ref[...].astype(BF16), b_ref[...].astype(BF16), (dims, ((), ())),
                               preferred_element_type=F32)
        if nk == 1:
            o_ref[...] = prod.astype(o_ref.dtype)
            return
        acc_ref = rest[-1]
        kk = pl.program_id(2)

        @pl.when(kk == 0)
        def _():
            acc_ref[...] = prod

        @pl.when(kk > 0)
        def _():
            acc_ref[...] += prod

        @pl.when(kk == nk - 1)
        def _():
            o_ref[...] = acc_ref[...].astype(o_ref.dtype)

    if mode == "tn":
        a_spec = pl.BlockSpec((tk, tm), lambda i, j, kk: (kk, i))
    else:
        a_spec = pl.BlockSpec((tm, tk), lambda i, j, kk: (i, kk))
    b_block, b_index = ((tn, tk), lambda i, j, kk: (j, kk)) if mode == "nt" else ((tk, tn), lambda i, j, kk: (kk, j))
    b_spec = _slab_spec(b_block, b_index, b_slab)
    in_specs, operands, aliases = [a_spec, b_spec], [a, b], {}
    if into is None:
        out_spec = pl.BlockSpec((tm, tn), lambda i, j, kk: (i, j))
        out_shape = jax.ShapeDtypeStruct((m, n), out_dtype)
    else:
        buf, slab, count = into
        out_spec = pl.BlockSpec((None, tm, tn), lambda i, j, kk: (slab, i, j))
        out_shape = jax.ShapeDtypeStruct((count, m, n), out_dtype)
        if buf is not None:
            in_specs.append(pl.BlockSpec(memory_space=pl.ANY))
            operands.append(buf)
            aliases = {2: 0}
    return pl.pallas_call(
        body, name=name, grid=(m // tm, n // tn, nk), in_specs=in_specs, out_specs=out_spec, out_shape=out_shape,
        scratch_shapes=[pltpu.VMEM((tm, tn), F32)] if nk > 1 else [], input_output_aliases=aliases,
        compiler_params=_params(dimension_semantics=("arbitrary", "arbitrary", "arbitrary")),
    )(*operands)


def _ffn_in(name, h, w):
    w, slab = w if isinstance(w, tuple) else (w, None)
    s, d = h.shape
    f = w.shape[-1] // 2
    tm, tn = _pick(s, MM_TILES[2:]), _pick(f, MM_TILES)

    def body(h_ref, wg_ref, wu_ref, g_ref, u_ref, a_ref):
        hb = h_ref[...]
        g, u = _nn(hb, wg_ref[...]), _nn(hb, wu_ref[...])
        g_ref[...] = g.astype(BF16)
        u_ref[...] = u.astype(BF16)
        a_ref[...] = (g * jax.nn.sigmoid(g) * u).astype(BF16)

    out = pl.BlockSpec((tm, tn), lambda i, j: (i, j))
    return pl.pallas_call(
        body, name=name, grid=(s // tm, f // tn),
        in_specs=[pl.BlockSpec((tm, d), lambda i, j: (i, 0)), _slab_spec((d, tn), lambda i, j: (0, j), slab),
                  _slab_spec((d, tn), lambda i, j: (0, f // tn + j), slab)],
        out_specs=[out, out, out], out_shape=[jax.ShapeDtypeStruct((s, f), BF16)] * 3,
        compiler_params=_params(dimension_semantics=("arbitrary", "arbitrary")),
    )(h, w, w)


def _small_mm(name, a, b, mode):
    dims = {"nn": ((1,), (0,)), "tn": ((0,), (0,))}[mode]
    m = a.shape[0] if mode == "nn" else a.shape[1]

    def body(a_ref, b_ref, o_ref):
        o_ref[...] = lax.dot_general(a_ref[...], b_ref[...], (dims, ((), ())), precision=lax.Precision.HIGHEST,
                                     preferred_element_type=F32)

    return pl.pallas_call(body, name=name, out_shape=jax.ShapeDtypeStruct((m, b.shape[1]), F32),
                          compiler_params=_params())(a, b)


def _norm(x):
    mu = jnp.mean(x, axis=-1, keepdims=True)
    xc = x - mu
    rstd = lax.rsqrt(jnp.mean(xc * xc, axis=-1, keepdims=True) + LN_EPS)
    return xc * rstd, rstd


def _norm_bwd(dn, n, rstd):
    return rstd * (dn - jnp.mean(dn, axis=-1, keepdims=True) - n * jnp.mean(dn * n, axis=-1, keepdims=True))


def _colsum(v):
    return jnp.sum(v, axis=0, keepdims=True)


def _gelu(x):
    return 0.5 * x * (1.0 + jnp.tanh(GELU_K * (x + GELU_C * x * x * x)))


def _gelu_grad(x):
    t = jnp.tanh(GELU_K * (x + GELU_C * x * x * x))
    return 0.5 * (1.0 + t) + 0.5 * x * (1.0 - t * t) * GELU_K * (1.0 + 3.0 * GELU_C * x * x)


def _log_sigmoid_parts(z):
    lb = jnp.minimum(z, 0.0) - jnp.log(1.0 + jnp.exp(-jnp.abs(z)))
    return lb, lb - z


def _kv_transposed(proj, t):
    s = proj.shape[0]
    nb, nhp = s // t, SB_WIDTH // LANES

    def body(k_ref, v_ref, kt_ref, vt_ref):
        k, v = k_ref[...].astype(F32), v_ref[...].astype(F32)
        for hp in range(nhp):
            kt_ref[hp, 0] = k[:, hp * LANES:(hp + 1) * LANES].T.astype(BF16)
            vt_ref[hp, 0] = v[:, hp * LANES:(hp + 1) * LANES].T.astype(BF16)

    col = lambda cb: pl.BlockSpec((t, SB_WIDTH), lambda i, cb=cb: (i, cb))
    t_out = pl.BlockSpec((nhp, 1, LANES, t), lambda i: (0, i, 0, 0))
    return pl.pallas_call(
        body, name="kv_transposed", grid=(nb,), in_specs=[col(1), col(2)], out_specs=[t_out, t_out],
        out_shape=[jax.ShapeDtypeStruct((nhp, nb, LANES, t), BF16)] * 2,
        compiler_params=_params(dimension_semantics=("arbitrary",)),
    )(proj, proj)


def _key_order(t, sign):
    ones = jnp.ones((t, t), BF16)
    return jnp.triu(ones, 1) if sign > 0 else jnp.tril(ones, -1)


DEAD_LOG_WEIGHT = -110.0


def _walk_down(i, tiles, state, alive):
    st = lax.cond(i == 0, lambda s_: tiles([i], s_, [True]), lambda s_: tiles([i, i - 1], s_, [True, False]), state)
    n = jnp.maximum(i - 1, 0)

    def pair(c):
        nxt = tiles([i - 2 - 2 * c[0], i - 3 - 2 * c[0]], c[1], [False, False])
        return c[0] + 1, nxt, alive(nxt)

    p, st, go = lax.while_loop(lambda c: (c[0] < n // 2) & c[2], pair, (jnp.int32(0), st, alive(st)))
    return lax.cond((n % 2 == 1) & (p == n // 2) & go, lambda s_: tiles([0], s_, [False]), lambda s_: s_, st)


def _walk_up(i, first, tiles, state):
    n = jnp.maximum(i - 1 - first, 0)
    st = lax.fori_loop(0, n // 2, lambda p, s_: tiles([first + 2 * p, first + 2 * p + 1], s_, [False, False]), state)
    st = lax.cond(n % 2 == 1, lambda s_: tiles([i - 2], s_, [False]), lambda s_: s_, st)
    return lax.cond(i == 0, lambda s_: tiles([i], s_, [True]), lambda s_: tiles([i - 1, i], s_, [False, True]), st)


def _nt(a, b):
    return lax.dot_general(a, b, (((1,), (1,)), ((), ())), preferred_element_type=F32)


def _nn(a, b):
    return jnp.dot(a, b, preferred_element_type=F32)


def _attn_fwd(proj, vt3, t):
    s = proj.shape[0]
    nb, nhp = s // t, SB_WIDTH // LANES

    def body(q_ref, k_ref, vt_ref, later_ref, o_ref, car_ref):
        i = pl.program_id(1)
        q2 = q_ref[...] * (1.0 / math.sqrt(HEAD_DIM))
        lane_q = lax.broadcasted_iota(jnp.int32, q2.shape, 1)
        later = later_ref[...]
        valid = later > 0
        orow = lax.broadcasted_iota(jnp.int32, (LANES, t), 0)
        car_ref[...] = jnp.full(car_ref.shape, 2.0 * DEAD_LOG_WEIGHT, F32)
        qh = [jnp.where((lane_q < HEAD_DIM) == (hh == 0), q2, jnp.zeros_like(q2)) for hh in range(2)]

        def tiles(js, state, diagonal):
            chains = [(n, hh) for n in range(len(js)) for hh in range(2)]
            kb = [k_ref[pl.ds(pl.multiple_of(j * t, t), t), :] for j in js]
            z = {ch: _nt(kb[ch[0]], qh[ch[1]]) for ch in chains}
            lb, aft, csum = {}, {}, {}
            for ch in chains:
                lb[ch], l1m = _log_sigmoid_parts(z[ch])
                if diagonal[ch[0]]:
                    l1m = jnp.where(valid, l1m, 0.0)
                aft[ch] = _nn(later, l1m.astype(BF16))
                csum[ch] = _colsum(l1m)
            state = list(state)
            for ch in chains:
                n, hh = ch
                c_after, acc = state[hh]
                w = jnp.exp(lb[ch] + aft[ch] + c_after)
                if diagonal[ch[0]]:
                    w = jnp.where(valid, w, 0.0)
                car_ref[hh, pl.ds(js[n], 1), :] = c_after
                state[hh] = (c_after + csum[ch], acc + _nn(vt_ref[0, js[n]], w.astype(BF16)))
            return tuple(state)

        def alive(state):
            return jnp.max(jnp.maximum(state[0][0], state[1][0])) >= DEAD_LOG_WEIGHT

        zero = (jnp.zeros((1, t), F32), jnp.zeros((LANES, t), F32))
        (_, acc0), (_, acc1) = _walk_down(i, tiles, (zero, zero), alive)
        o_ref[...] = jnp.where(orow < HEAD_DIM, acc0, acc1).T.astype(o_ref.dtype)

    return pl.pallas_call(
        body, name="attn_fwd", grid=(nhp, nb),
        in_specs=[pl.BlockSpec((t, LANES), lambda hp, i: (i, hp)),
                  pl.BlockSpec((s, LANES), lambda hp, i: (0, nhp + hp)),
                  pl.BlockSpec((1, nb, LANES, t), lambda hp, i: (hp, 0, 0, 0)),
                  pl.BlockSpec((t, t), lambda hp, i: (0, 0))],
        out_specs=[pl.BlockSpec((t, LANES), lambda hp, i: (i, hp)),
                   pl.BlockSpec((2, nb, t), lambda hp, i: (hp, 0, i))],
        out_shape=[jax.ShapeDtypeStruct((s, nhp * LANES), BF16), jax.ShapeDtypeStruct((2 * nhp, nb, s), F32)],
        compiler_params=_params(dimension_semantics=("arbitrary", "arbitrary")),
    )(proj, proj, vt3, _key_order(t, 1))


def _attn_bwd(proj, do, kt3, car, t):
    s = proj.shape[0]
    nb, nhp = s // t, SB_WIDTH // LANES

    def body(q_ref, do_ref, k_ref, v_ref, kt_ref, car_ref, later_ref, earlier_ref, dq_ref, dk_ref, dv_ref):
        i = pl.program_id(1)

        @pl.when(i == 0)
        def _():
            dk_ref[...] = jnp.zeros_like(dk_ref)
            dv_ref[...] = jnp.zeros_like(dv_ref)

        q2, do2 = q_ref[...] * (1.0 / math.sqrt(HEAD_DIM)), do_ref[...]
        lane_q = lax.broadcasted_iota(jnp.int32, q2.shape, 1)
        later = later_ref[...]
        earlier = earlier_ref[...]
        valid = later > 0
        orow = lax.broadcasted_iota(jnp.int32, (LANES, t), 0)
        head = [(lane_q < HEAD_DIM) == (hh == 0) for hh in range(2)]
        qh = [jnp.where(hm, q2, jnp.zeros_like(q2)) for hm in head]
        doh = [jnp.where(hm, do2, jnp.zeros_like(do2)) for hm in head]

        def tiles(js, state, diagonal):
            chains = [(n, hh) for n in range(len(js)) for hh in range(2)]
            rows = [pl.ds(pl.multiple_of(j * t, t), t) for j in js]
            kb = [k_ref[r, :] for r in rows]
            vb = [v_ref[r, :] for r in rows]
            z = {ch: _nt(kb[ch[0]], qh[ch[1]]) for ch in chains}
            dw = {ch: _nt(vb[ch[0]], doh[ch[1]]) for ch in chains}
            lb, beta, aft = {}, {}, {}
            for ch in chains:
                lb[ch], l1m = _log_sigmoid_parts(z[ch])
                beta[ch] = jnp.exp(lb[ch])
                if diagonal[ch[0]]:
                    l1m = jnp.where(valid, l1m, 0.0)
                aft[ch] = _nn(later, l1m.astype(BF16))
            w, g, gsum, g_in = {}, {}, {}, {}
            for ch in chains:
                n, hh = ch
                w[ch] = jnp.exp(lb[ch] + aft[ch] + car_ref[hh, pl.ds(js[n], 1), :])
                if diagonal[ch[0]]:
                    w[ch] = jnp.where(valid, w[ch], 0.0)
                g[ch] = dw[ch] * w[ch]
                g_in[ch] = _nn(earlier, g[ch].astype(BF16))
                gsum[ch] = _colsum(g[ch])
            state = list(state)
            dk_t, dv_t = [None] * len(js), [None] * len(js)
            for ch in chains:
                n, hh = ch
                c_g, dqt = state[hh]
                dz = g[ch] - beta[ch] * (g[ch] + g_in[ch] + c_g)
                if diagonal[ch[0]]:
                    dz = jnp.where(valid, dz, 0.0)
                dzb, wb = dz.astype(BF16), w[ch].astype(BF16)
                dk_h, dv_h = _nn(dzb, qh[hh]), _nn(wb, doh[hh])
                dk_t[n] = dk_h if dk_t[n] is None else dk_t[n] + dk_h
                dv_t[n] = dv_h if dv_t[n] is None else dv_t[n] + dv_h
                state[hh] = (c_g + gsum[ch], dqt + _nn(kt_ref[0, js[n]], dzb))
            for n in range(len(js)):
                dk_ref[rows[n], :] += dk_t[n]
                dv_ref[rows[n], :] += dv_t[n]
            return tuple(state)

        reach = jnp.max(jnp.max(car_ref[...], axis=2, keepdims=True), axis=0)
        dead = (reach < DEAD_LOG_WEIGHT) & (lax.broadcasted_iota(jnp.int32, reach.shape, 0) < i)
        first = jnp.sum(jnp.where(dead, 1.0, 0.0)).astype(jnp.int32)
        zero = (jnp.zeros((1, t), F32), jnp.zeros((LANES, t), F32))
        (_, dq0), (_, dq1) = _walk_up(i, first, tiles, (zero, zero))
        dq_ref[...] = jnp.where(orow < HEAD_DIM, dq0, dq1).T

    tile_spec = pl.BlockSpec((t, LANES), lambda hp, i: (i, hp))
    whole = pl.BlockSpec((s, LANES), lambda hp, i: (0, hp))
    return pl.pallas_call(
        body, name="attn_bwd", grid=(nhp, nb),
        in_specs=[tile_spec, tile_spec, pl.BlockSpec((s, LANES), lambda hp, i: (0, nhp + hp)),
                  pl.BlockSpec((s, LANES), lambda hp, i: (0, 2 * nhp + hp)),
                  pl.BlockSpec((1, nb, LANES, t), lambda hp, i: (hp, 0, 0, 0)),
                  pl.BlockSpec((2, nb, t), lambda hp, i: (hp, 0, i)),
                  pl.BlockSpec((t, t), lambda hp, i: (0, 0)), pl.BlockSpec((t, t), lambda hp, i: (0, 0))],
        out_specs=[tile_spec, whole, whole],
        out_shape=[jax.ShapeDtypeStruct((s, nhp * LANES), F32)] * 3,
        compiler_params=_params(dimension_semantics=("arbitrary", "arbitrary")),
    )(proj, do, proj, proj, kt3, car, _key_order(t, 1), _key_order(t, -1))


SCAN_LANES = 1024
SCAN_ROWS = 8
S5_CHUNKS = 4


def _scan_chunks(v):
    n = v.shape[1] // (2 * LANES)
    return [(v[:, c * 2 * LANES:c * 2 * LANES + LANES], v[:, c * 2 * LANES + LANES:(c + 1) * 2 * LANES]) for c in range(n)]


def _scan_tables(lr, li, reverse):
    if reverse:
        li = -li
    row = lax.broadcasted_iota(jnp.int32, (SCAN_ROWS, LANES), 0)
    powers = [(lr, li)]
    for _ in range(SCAN_ROWS - 1):
        pr, pi = powers[-1]
        powers.append((pr * lr - pi * li, pr * li + pi * lr))
    levels = []
    for d in (1, 2, 4):
        keep = (row < SCAN_ROWS - d) if reverse else (row >= d)
        levels.append((SCAN_ROWS - d if reverse else d,
                       (jnp.where(keep, powers[d - 1][0], 0.0), jnp.where(keep, powers[d - 1][1], 0.0))))
    pr = pi = jnp.zeros((SCAN_ROWS, LANES), F32)
    for r in range(SCAN_ROWS):
        steps = SCAN_ROWS - r if reverse else r + 1
        pr = jnp.where(row == r, powers[steps - 1][0], pr)
        pi = jnp.where(row == r, powers[steps - 1][1], pi)
    return levels, (pr, pi)


def _s5_fwd(proj, u_off, bmat, cmat, lam):
    s, w = proj.shape[0], SSM_BLOCKS * bmat.shape[1]
    tt = _pick(s, (512, 256, 128, 8))
    nt = s // tt
    cin = bmat.shape[0] // SSM_BLOCKS
    chunk = tt // S5_CHUNKS

    def body(u_ref, b_ref, c_ref, lam_ref, h_ref, y_ref, x_ref, st_ref):
        @pl.when(pl.program_id(1) == 0)
        def _():
            st_ref[...] = jnp.zeros_like(st_ref)

        tables = [_scan_tables(lr, li, reverse=False) for lr, li in _scan_chunks(lam_ref[...])]

        def project(k):
            x_ref[k * chunk:(k + 1) * chunk, :] = _nn(u_ref[k * chunk:(k + 1) * chunk, :].astype(BF16), b_ref[...])

        def tile(r0, last):
            last, parts = list(last), []
            for c, (xr, xi) in enumerate(_scan_chunks(x_ref[r0:r0 + SCAN_ROWS, :])):
                levels, (pr, pi) = tables[c]
                for d, (ar, ai) in levels:
                    sr, si = pltpu.roll(xr, d, 0), pltpu.roll(xi, d, 0)
                    xr, xi = xr + ar * sr - ai * si, xi + ar * si + ai * sr
                br, bi = last[2 * c], last[2 * c + 1]
                hr = xr + pr * br - pi * bi
                hi = xi + pr * bi + pi * br
                last[2 * c], last[2 * c + 1] = hr[SCAN_ROWS - 1:], hi[SCAN_ROWS - 1:]
                parts += [hr, hi]
            h_ref[r0:r0 + SCAN_ROWS, :] = jnp.concatenate(parts, axis=1)
            return tuple(last)

        st = st_ref[0:1, :]
        last = tuple(st[:, c * LANES:(c + 1) * LANES] for c in range(SCAN_LANES // LANES))
        project(0)
        for k in range(S5_CHUNKS):
            if k + 1 < S5_CHUNKS:
                project(k + 1)
            for r0 in range(k * chunk, (k + 1) * chunk, SCAN_ROWS):
                last = tile(r0, last)
            y_ref[k * chunk:(k + 1) * chunk, :] = _nn(h_ref[k * chunk:(k + 1) * chunk, :].astype(BF16), c_ref[...])
        st_ref[0:1, :] = jnp.concatenate(last, axis=1)

    return pl.pallas_call(
        body, name="s5_fwd", grid=(SSM_BLOCKS, nt),
        in_specs=[pl.BlockSpec((tt, cin), lambda kb, i: (i, u_off // cin + kb)),
                  pl.BlockSpec((cin, SCAN_LANES), lambda kb, i: (kb, 0)),
                  pl.BlockSpec((SCAN_LANES, cin), lambda kb, i: (kb, 0)),
                  pl.BlockSpec((1, SCAN_LANES), lambda kb, i: (0, kb))],
        out_specs=[pl.BlockSpec((tt, SCAN_LANES), lambda kb, i: (i, kb)), pl.BlockSpec((tt, cin), lambda kb, i: (i, kb))],
        out_shape=[jax.ShapeDtypeStruct((s, w), F32), jax.ShapeDtypeStruct((s, bmat.shape[0]), F32)],
        scratch_shapes=[pltpu.VMEM((tt, SCAN_LANES), F32), pltpu.VMEM((SCAN_ROWS, SCAN_LANES), F32)],
        compiler_params=_params(dimension_semantics=("arbitrary", "arbitrary")),
    )(proj, bmat, cmat, lam)


def _s5_bwd(dy, h, proj, u_off, bmat, cmat, lam):
    s, w = h.shape
    tt = _pick(s, (512, 256, 128, 8))
    nt = s // tt
    cin = bmat.shape[0] // SSM_BLOCKS
    chunk = tt // S5_CHUNKS

    def body(dy_ref, h_ref, u_ref, b_ref, c_ref, lam_ref, du_ref, dlam_ref, db_ref, dc_ref, e_ref, a_ref, st_ref):
        @pl.when(pl.program_id(1) == 0)
        def _():
            st_ref[...] = jnp.zeros_like(st_ref)
            dlam_ref[...] = jnp.zeros_like(dlam_ref)
            db_ref[...] = jnp.zeros_like(db_ref)
            dc_ref[...] = jnp.zeros_like(dc_ref)

        tables = [_scan_tables(lr, li, reverse=True) for lr, li in _scan_chunks(lam_ref[...])]
        nch = len(tables)
        row = lax.broadcasted_iota(jnp.int32, (SCAN_ROWS, LANES), 0)
        rows_first = (((0,), (0,)), ((), ()))

        def project(k):
            e_ref[k * chunk:(k + 1) * chunk, :] = _nt(dy_ref[k * chunk:(k + 1) * chunk, :], c_ref[...])

        def finish(k):
            rows = slice(k * chunk, (k + 1) * chunk)
            adj = a_ref[rows, :].astype(BF16)
            du_ref[rows, :] = _nt(adj, b_ref[...])
            db_ref[...] += lax.dot_general(u_ref[rows, :].astype(BF16), adj, rows_first, preferred_element_type=F32)
            dc_ref[...] += lax.dot_general(h_ref[rows, :].astype(BF16), dy_ref[rows, :], rows_first, preferred_element_type=F32)

        def tile(r0, carry):
            e_c = _scan_chunks(e_ref[r0:r0 + SCAN_ROWS, :])
            h_c = _scan_chunks(h_ref[r0:r0 + SCAN_ROWS, :])
            carry, parts = list(carry), []
            for c in range(nch):
                (yr, yi), (hr, hi) = e_c[c], h_c[c]
                levels, (pr, pi) = tables[c]
                for shift, (lr, li) in levels:
                    sr, si = pltpu.roll(yr, shift, 0), pltpu.roll(yi, shift, 0)
                    yr, yi = yr + lr * sr - li * si, yi + lr * si + li * sr
                nr, ni, dr, di = carry[4 * c:4 * c + 4]
                ar = yr + pr * nr - pi * ni
                ai = yi + pr * ni + pi * nr
                nxr = jnp.where(row == SCAN_ROWS - 1, nr, pltpu.roll(ar, SCAN_ROWS - 1, 0))
                nxi = jnp.where(row == SCAN_ROWS - 1, ni, pltpu.roll(ai, SCAN_ROWS - 1, 0))
                carry[4 * c:4 * c + 4] = [ar[0:1], ai[0:1], dr + nxr * hr + nxi * hi, di + nxi * hr - nxr * hi]
                parts += [ar, ai]
            a_ref[r0:r0 + SCAN_ROWS, :] = jnp.concatenate(parts, axis=1)
            return tuple(carry)

        st, dl = st_ref[0:1, :], dlam_ref[...]
        init = []
        for c in range(nch):
            lo = c * 2 * LANES
            init += [st[:, lo:lo + LANES], st[:, lo + LANES:lo + 2 * LANES],
                     dl[:, lo:lo + LANES], dl[:, lo + LANES:lo + 2 * LANES]]
        fin = tuple(init)
        project(S5_CHUNKS - 1)
        for k in reversed(range(S5_CHUNKS)):
            if k > 0:
                project(k - 1)
            for r0 in reversed(range(k * chunk, (k + 1) * chunk, SCAN_ROWS)):
                fin = tile(r0, fin)
            finish(k)
        st_ref[0:1, :] = jnp.concatenate([fin[4 * c + q] for c in range(nch) for q in (0, 1)], axis=1)
        dlam_ref[...] = jnp.concatenate([fin[4 * c + q] for c in range(nch) for q in (2, 3)], axis=1)

        @pl.when(pl.program_id(1) == nt - 1)
        def _():
            dlam_ref[0:1, :] = jnp.sum(dlam_ref[...], axis=0, keepdims=True)

    def rev(width, col):
        return pl.BlockSpec((tt, width), lambda kb, i: (nt - 1 - i, col(kb)))

    return pl.pallas_call(
        body, name="s5_bwd", grid=(SSM_BLOCKS, nt),
        in_specs=[rev(cin, lambda kb: kb), rev(SCAN_LANES, lambda kb: kb), rev(cin, lambda kb: u_off // cin + kb),
                  pl.BlockSpec((cin, SCAN_LANES), lambda kb, i: (kb, 0)),
                  pl.BlockSpec((SCAN_LANES, cin), lambda kb, i: (kb, 0)),
                  pl.BlockSpec((1, SCAN_LANES), lambda kb, i: (0, kb))],
        out_specs=[rev(cin, lambda kb: kb), pl.BlockSpec((SCAN_ROWS, SCAN_LANES), lambda kb, i: (0, kb)),
                   pl.BlockSpec((cin, SCAN_LANES), lambda kb, i: (kb, 0)), pl.BlockSpec((SCAN_LANES, cin), lambda kb, i: (kb, 0))],
        out_shape=[jax.ShapeDtypeStruct((s, bmat.shape[0]), F32), jax.ShapeDtypeStruct((SCAN_ROWS, w), F32),
                   jax.ShapeDtypeStruct((bmat.shape[0], SCAN_LANES), F32), jax.ShapeDtypeStruct((w, cin), F32)],
        scratch_shapes=[pltpu.VMEM((tt, SCAN_LANES), F32), pltpu.VMEM((tt, SCAN_LANES), F32),
                        pltpu.VMEM((SCAN_ROWS, SCAN_LANES), F32)],
        compiler_params=_params(dimension_semantics=("arbitrary", "arbitrary")),
    )(dy, h, proj, bmat, cmat, lam)


def _ssm_params_fwd(a_re, a_im, log_dt, b_re, b_im):
    def body(ar_ref, ai_ref, ldt_ref, br_ref, bi_ref, lr_ref, li_ref, bbr_ref, bbi_ref):
        ar, ai, dt = ar_ref[...], ai_ref[...], jnp.exp(ldt_ref[...])
        mag = jnp.exp(ar * dt)
        lr, li = mag * jnp.cos(ai * dt), mag * jnp.sin(ai * dt)
        den = ar * ar + ai * ai
        cr = ((lr - 1.0) * ar + li * ai) / den
        ci = (li * ar - (lr - 1.0) * ai) / den
        br, bi = br_ref[...], bi_ref[...]
        lr_ref[...], li_ref[...] = lr, li
        bbr_ref[...] = cr * br - ci * bi
        bbi_ref[...] = cr * bi + ci * br

    n = a_re.shape[0]
    v1, v16 = jax.ShapeDtypeStruct((n, 1), F32), jax.ShapeDtypeStruct((n, SSM_GROUP), F32)
    return pl.pallas_call(body, name="ssm_params_fwd", out_shape=[v1, v1, v16, v16],
                          compiler_params=_params())(a_re, a_im, log_dt, b_re, b_im)


def _ssm_params_bwd(a_re, a_im, log_dt, b_re, b_im, g_lr, g_li, g_bbr, g_bbi):
    n = a_re.shape[0]

    def body(ar_ref, ai_ref, ldt_ref, br_ref, bi_ref, glr_ref, gli_ref, gbr_ref, gbi_ref,
             dar_ref, dai_ref, dldt_ref, dbr_ref, dbi_ref):
        ar, ai, dt = ar_ref[...], ai_ref[...], jnp.exp(ldt_ref[...])
        mag = jnp.exp(ar * dt)
        lr, li = mag * jnp.cos(ai * dt), mag * jnp.sin(ai * dt)
        den = ar * ar + ai * ai
        cr = ((lr - 1.0) * ar + li * ai) / den
        ci = (li * ar - (lr - 1.0) * ai) / den
        br, bi, gbr, gbi = br_ref[...], bi_ref[...], gbr_ref[...], gbi_ref[...]
        dbr_ref[...] = gbr * cr + gbi * ci
        dbi_ref[...] = gbi * cr - gbr * ci
        gcr = jnp.sum(gbr * br + gbi * bi, axis=1, keepdims=True)
        gci = jnp.sum(gbi * br - gbr * bi, axis=1, keepdims=True)
        ir, ii = ar / den, -ai / den
        glr = glr_ref[...] + gcr * ir + gci * ii
        gli = gli_ref[...] + gci * ir - gcr * ii
        qr, qi = cr * ir - ci * ii, cr * ii + ci * ir
        gar = -(gcr * qr + gci * qi)
        gai = -(gci * qr - gcr * qi)
        gxr = glr * lr + gli * li
        gxi = gli * lr - glr * li
        dar_ref[...] = gar + gxr * dt
        dai_ref[...] = gai + gxi * dt
        gdt = (gxr * ar + gxi * ai) * dt
        rowg = lax.broadcasted_iota(jnp.int32, (n, SSM_GROUPS), 0) // SSM_STATE
        colg = lax.broadcasted_iota(jnp.int32, (n, SSM_GROUPS), 1)
        dldt_ref[...] = jnp.sum(jnp.where(rowg == colg, gdt, 0.0), axis=0, keepdims=True)

    v1, v16 = jax.ShapeDtypeStruct((n, 1), F32), jax.ShapeDtypeStruct((n, SSM_GROUP), F32)
    return pl.pallas_call(body, name="ssm_params_bwd",
                          out_shape=[v1, v1, jax.ShapeDtypeStruct((1, SSM_GROUPS), F32), v16, v16],
                          compiler_params=_params())(a_re, a_im, log_dt, b_re, b_im, g_lr, g_li, g_bbr, g_bbi)


def _interleave(re, im, axis):
    shp = list(re.shape)
    new = shp[:axis] + [shp[axis] // LANES, LANES] + shp[axis + 1:]
    st = jnp.stack([re.reshape(new), im.reshape(new)], axis=axis + 1)
    return st.reshape(shp[:axis] + [2 * shp[axis]] + shp[axis + 1:])


def _deinterleave(v, axis):
    shp = list(v.shape)
    r = v.reshape(shp[:axis] + [shp[axis] // (2 * LANES), 2, LANES] + shp[axis + 1:])
    out = shp[:axis] + [shp[axis] // 2] + shp[axis + 1:]
    return (lax.index_in_dim(r, 0, axis + 1, keepdims=False).reshape(out),
            lax.index_in_dim(r, 1, axis + 1, keepdims=False).reshape(out))


def _b_matrix(bbr, bbi):
    per = SSM_GROUPS // SSM_BLOCKS
    eye = jnp.eye(per, dtype=F32)

    def blockdiag(v):
        x = v.reshape(SSM_BLOCKS, per, SSM_STATE, SSM_GROUP).transpose(0, 1, 3, 2)
        return (eye[None, :, None, :, None] * x[:, :, :, None, :]).reshape(SSM_GROUPS * SSM_GROUP, per * SSM_STATE)

    return _interleave(blockdiag(bbr), blockdiag(bbi), 1)


def _diag_blocks(v, rows, cols):
    per = SSM_GROUPS // SSM_BLOCKS
    x = v.reshape(SSM_BLOCKS, per, rows, per, cols) * jnp.eye(per, dtype=v.dtype)[None, :, None, :, None]
    return jnp.sum(x, axis=3).reshape(SSM_GROUPS, rows, cols)


def _b_matrix_grad(d):
    def diag(v):
        return _diag_blocks(v, SSM_GROUP, SSM_STATE).transpose(0, 2, 1).reshape(N_STATE, SSM_GROUP)

    dr, di = _deinterleave(d, 1)
    return diag(dr), diag(di)


def _c_matrix(c_re, c_im):
    per = SSM_GROUPS // SSM_BLOCKS
    eye = jnp.eye(per, dtype=F32)

    def blockdiag(v):
        x = v.reshape(SSM_BLOCKS, per, SSM_GROUP, SSM_STATE).transpose(0, 1, 3, 2)
        return (x[:, :, :, None, :] * eye[None, :, None, :, None]).reshape(N_STATE, per * SSM_GROUP)

    return _interleave(blockdiag(c_re), blockdiag(-c_im), 0)


def _c_matrix_grad(d):
    def diag(v):
        return _diag_blocks(v, SSM_STATE, SSM_GROUP).transpose(0, 2, 1)

    dr, di = _deinterleave(d, 0)
    return diag(dr), -diag(di)


def _row(v):
    return v.reshape(1, -1)


def _ssm_inputs(p):
    rows = lambda v: v.reshape(N_STATE, -1)
    ldt = jnp.repeat(p["ssm_log_dt"], SSM_STATE).reshape(N_STATE, 1)
    return rows(p["ssm_a_re"]), rows(p["ssm_a_im"]), ldt, rows(p["ssm_b_re"]), rows(p["ssm_b_im"])


def _lnmod(x, sc, sh):
    return _norm(x)[0] * (1.0 + sc) + sh


def _resid_ln(x, y, g, lg, lb):
    return _norm(ALPHA * x + (1.0 + g) * y)[0] * lg + lb


def _resid_ln_lnmod(x, y, g, lg, lb, sc, sh):
    xo = _resid_ln(x, y, g, lg, lb)
    return xo, _lnmod(xo, sc, sh)


def _layer_fwd(x, h1, mod, p, tag, next_mod):
    d = x.shape[1]
    sh_m, sc_m, g_m, sh_f, sc_f, g_f = [_row(mod[i]) for i in range(6)]
    nm = lambda s: f"{s}_{tag}"
    proj = _mm(nm("proj"), h1, p["w_in"], "nn", out_dtype=BF16)
    t = min(ATT_TILE, x.shape[0])
    kt3, vt3 = _kv_transposed(proj, t)
    att, car = _attn_fwd(proj, vt3, t)
    y_sb = _mm(nm("sb_up"), att, p["w_sb_up"], "nn", out_dtype=BF16)

    ssm_in = _ssm_inputs(p)
    lam_r, lam_i, bbr, bbi = _ssm_params_fwd(*ssm_in)
    lam = _interleave(lam_r.reshape(1, N_STATE), lam_i.reshape(1, N_STATE), 1)
    bmat = _b_matrix(bbr, bbi).astype(BF16)
    cmat = _c_matrix(p["ssm_c_re"], p["ssm_c_im"]).astype(BF16)
    hst, yc = _s5_fwd(proj, U_OFFSET, bmat, cmat, lam)

    def ssm_act(yc, u, dsk):
        y0 = yc + dsk * u
        return y0, _gelu(y0)

    y0, y1 = _rowwise(nm("ssm_act"), ssm_act, [(yc, 0, 512), (proj, 3, 512)], [_row(p["ssm_d"])], [(512, F32), (512, F32)])
    gl = _mm(nm("glu"), y1, p["w_glu"], "nn")
    y2 = _rowwise(nm("glu_act"), lambda y1, gl, b: y1 * jax.nn.sigmoid(gl + b), [(y1, 0, 512), (gl, 0, 512)],
                  [_row(p["b_glu"])], [(512, BF16)])
    y_ssm = _mm(nm("ssm_up"), y2, p["w_ssm_up"], "nn", out_dtype=BF16)

    def merge(gsb, gss, ysb, yss):
        return jax.nn.sigmoid(gsb) * ysb + jax.nn.sigmoid(gss) * yss

    merged = _rowwise(nm("merge"), merge, [(proj, 2, d), (proj, 3, d), (y_sb, 0, d), (y_ssm, 0, d)], [], [(d, BF16)])
    y = _mm(nm("out"), merged, p["w_out"], "nn")

    x1, h2 = _rowwise(nm("ln1"), _resid_ln_lnmod, [(x, 0, d), (y, 0, d)],
                      [g_m, _row(p["ln1_g"]), _row(p["ln1_b"]), sc_f, sh_f], [(d, F32), (d, BF16)])
    f_gate, f_up, act = _ffn_in(nm("ffn_in"), h2, p["w_ffn_in"])
    yf = _mm(nm("ffn_out"), act, p["w_ffn_out"], "nn")
    x2 = h1_next = None
    if next_mod is not None:
        x2, h1_next = _rowwise(nm("ln2"), _resid_ln_lnmod, [(x1, 0, d), (yf, 0, d)],
                               [g_f, _row(p["ln2_g"]), _row(p["ln2_b"]), next_mod[1], next_mod[0]], [(d, F32), (d, BF16)])
    saved = dict(x=x, h1=h1, proj=proj, ssm_in=ssm_in, kt3=kt3, car=car, att=att, y_sb=y_sb, lam=lam, bmat=bmat,
                 cmat=cmat, hst=hst, y0=y0, y1=y1, gl=gl, y2=y2, y_ssm=y_ssm, merged=merged, y=y, x1=x1, h2=h2, f_gate=f_gate, f_up=f_up,
                 act=act, yf=yf, t=t)
    return x2, h1_next, saved


def _resid_ln_bwd(x, y, dxo, g, lg):
    n, rstd = _norm(ALPHA * x + (1.0 + g) * y)
    dr = _norm_bwd(dxo * lg, n, rstd)
    return ALPHA * dr, (1.0 + g) * dr, _colsum(dxo * n), _colsum(dxo), _colsum(dr * y)


def _lnmod_bwd(x, dh, dxa, sc):
    n, rstd = _norm(x)
    return dxa + _norm_bwd(dh * (1.0 + sc), n, rstd), _colsum(dh * n), _colsum(dh)


def _lnmod_resid_ln_bwd(xo, dh, dxa, x, y, sc, g, lg):
    dxo, dsc, dsh = _lnmod_bwd(xo, dh, dxa, sc)
    dx, dy, dlg, dlb, dg = _resid_ln_bwd(x, y, dxo, g, lg)
    return dx, dy, dsc, dsh, dlg, dlb, dg


def _layer_bwd(dx1a, dyf, mod, p, sv, layer, depth, stacked):
    d = dx1a.shape[1]
    sh_m, sc_m, g_m, sh_f, sc_f, g_f = [_row(mod[i]) for i in range(6)]
    nm = lambda s: f"{s}_{layer}"
    grads = {}

    def weight_grad(n, a, b, **kw):
        grads[n] = _mm(nm("d" + n), a, b, "tn", out_dtype=BF16, into=(stacked.get(n), layer, depth), **kw)

    dact = _mm(nm("d_act"), dyf, p["w_ffn_out"], "nt", out_dtype=BF16)
    weight_grad("w_ffn_out", sv["act"], dyf)
    fh = sv["f_gate"].shape[1]

    def swiglu_bwd(g, u, da):
        sg = jax.nn.sigmoid(g)
        return jnp.concatenate([da * u * sg * (1.0 + g * (1.0 - sg)), da * g * sg], axis=1)

    df = _rowwise(nm("swiglu_bwd"), swiglu_bwd, [(sv["f_gate"], 0, fh), (sv["f_up"], 0, fh), (dact, 0, fh)], [], [(2 * fh, BF16)])
    dh2 = _mm(nm("d_h2"), df, p["w_ffn_in"], "nt")
    weight_grad("w_ffn_in", sv["h2"], df)
    dxa, dy, dsc_f, dsh_f, grads["ln1_g"], grads["ln1_b"], dg_m = _rowwise(
        nm("ln1_bwd"), _lnmod_resid_ln_bwd, [(sv["x1"], 0, d), (dh2, 0, d), (dx1a, 0, d), (sv["x"], 0, d), (sv["y"], 0, d)],
        [sc_f, g_m, _row(p["ln1_g"])], [(d, F32), (d, BF16)], [d] * 5)
    dmerged = _mm(nm("d_merged"), dy, p["w_out"], "nt", out_dtype=BF16)
    weight_grad("w_out", sv["merged"], dy)

    def merge_bwd(gsb, gss, ysb, yss, dm):
        s1, s2 = jax.nn.sigmoid(gsb), jax.nn.sigmoid(gss)
        return s1 * dm, s2 * dm, dm * ysb * s1 * (1.0 - s1), dm * yss * s2 * (1.0 - s2)

    dy_sb, dy_ssm, dg_sb, dg_ssm = _rowwise(
        nm("merge_bwd"), merge_bwd, [(sv["proj"], 2, d), (sv["proj"], 3, d), (sv["y_sb"], 0, d), (sv["y_ssm"], 0, d),
                                     (dmerged, 0, d)], [], [(d, BF16)] * 4)
    dy2 = _mm(nm("d_y2"), dy_ssm, p["w_ssm_up"], "nt")
    weight_grad("w_ssm_up", sv["y2"], dy_ssm)

    def glu_act_bwd(y1, gl, dy2, b):
        sg = jax.nn.sigmoid(gl + b)
        dgl = dy2 * y1 * sg * (1.0 - sg)
        return dy2 * sg, dgl, _colsum(dgl)

    dy1a, dgl, grads["b_glu"] = _rowwise(nm("glu_act_bwd"), glu_act_bwd, [(sv["y1"], 0, 512), (sv["gl"], 0, 512), (dy2, 0, 512)],
                                         [_row(p["b_glu"])], [(512, F32), (512, BF16)], [512])
    dy1b = _mm(nm("d_y1"), dgl, p["w_glu"], "nt")
    weight_grad("w_glu", sv["y1"], dgl)

    def ssm_act_bwd(y0, u, dy1a, dy1b, dsk):
        dy0 = (dy1a + dy1b) * _gelu_grad(y0)
        return dy0, dsk * dy0, _colsum(dy0 * u)

    dy0, du_a, grads["ssm_d"] = _rowwise(nm("ssm_act_bwd"), ssm_act_bwd,
                                         [(sv["y0"], 0, 512), (sv["proj"], 3, 512), (dy1a, 0, 512), (dy1b, 0, 512)],
                                         [_row(p["ssm_d"])], [(512, BF16), (512, F32)], [512])
    du_b, dlam, d_bmat, d_cmat = _s5_bwd(dy0, sv["hst"], sv["proj"], U_OFFSET, sv["bmat"], sv["cmat"], sv["lam"])
    grads["ssm_c_re"], grads["ssm_c_im"] = _c_matrix_grad(d_cmat)
    g_bbr, g_bbi = _b_matrix_grad(d_bmat)
    g_lr, g_li = _deinterleave(dlam[0:1], 1)
    da_re, da_im, dldt, db_re, db_im = _ssm_params_bwd(*sv["ssm_in"], g_lr.reshape(N_STATE, 1), g_li.reshape(N_STATE, 1),
                                                       g_bbr, g_bbi)
    grads["ssm_a_re"] = da_re.reshape(SSM_GROUPS, SSM_STATE)
    grads["ssm_a_im"] = da_im.reshape(SSM_GROUPS, SSM_STATE)
    grads["ssm_log_dt"] = dldt.reshape(SSM_GROUPS)
    grads["ssm_b_re"] = db_re.reshape(SSM_GROUPS, SSM_STATE, SSM_GROUP)
    grads["ssm_b_im"] = db_im.reshape(SSM_GROUPS, SSM_STATE, SSM_GROUP)
    datt = _mm(nm("d_att"), dy_sb, p["w_sb_up"], "nt", out_dtype=BF16)
    weight_grad("w_sb_up", sv["att"], dy_sb)
    dqs, dk, dv = _attn_bwd(sv["proj"], datt, sv["kt3"], sv["car"], sv["t"])

    def dproj_cols(dqs, dk, dv, dua, dub, dgsb, dgss):
        return jnp.concatenate([dqs * (1.0 / math.sqrt(HEAD_DIM)), dk, dv, dua + dub, dgsb.astype(F32), dgss.astype(F32)],
                               axis=1)

    dproj = _rowwise(nm("dproj"), dproj_cols, [(dqs, 0, 512), (dk, 0, 512), (dv, 0, 512), (du_a, 0, 512), (du_b, 0, 512),
                                               (dg_sb, 0, d), (dg_ssm, 0, d)], [], [(2048 + 2 * d, BF16)])
    dh1 = _mm(nm("d_h1"), dproj, p["w_in"], "nt")
    weight_grad("w_in", sv["h1"], dproj)
    for k in ("ln1_g", "ln1_b", "ssm_d", "b_glu"):
        grads[k] = grads[k].reshape(-1)
    return dh1, dxa, grads, (dg_m, dsh_f, dsc_f)


def _local_step(x, target, mod, layer_w):
    depth, d = len(layer_w), x.shape[1]
    rows = lambda l: [_row(mod[l][i]) for i in range(6)]
    h1 = _rowwise("lnmod1_0", _lnmod, [(x, 0, d)], [rows(0)[1], rows(0)[0]], [(d, BF16)])
    xs, saved = x, []
    for l in range(depth):
        xs, h1, sv = _layer_fwd(xs, h1, mod[l], layer_w[l], str(l), rows(l + 1)[:2] if l + 1 < depth else None)
        saved.append(sv)

    def head_bwd(x1, yf, tgt, g, lg, lb):
        err = _resid_ln(x1, yf, g, lg, lb) - tgt
        return _resid_ln_bwd(x1, yf, err * (1.0 / d), g, lg) + (_colsum(err * err) * (0.5 / d),)

    def boundary_bwd(dh, dxa, x1, yf, sc, g, lg, lb):
        dxo, dsc, dsh = _lnmod_bwd(_resid_ln(x1, yf, g, lg, lb), dh, dxa, sc)
        return _resid_ln_bwd(x1, yf, dxo, g, lg) + (dsc, dsh)

    lgrads, sums, stacked = [None] * depth, [dict() for _ in range(depth)], {}
    last, p = saved[-1], layer_w[-1]
    dx1a, dyf, dlg, dlb, dg_f, loss_cols = _rowwise(
        "head_bwd", head_bwd, [(last["x1"], 0, d), (last["yf"], 0, d), (target, 0, d)],
        [rows(depth - 1)[5], _row(p["ln2_g"]), _row(p["ln2_b"])], [(d, F32), (d, BF16)], [d] * 4)
    for l in reversed(range(depth)):
        sums[l]["g_f"] = dg_f
        dh1, dxa, lgrads[l], (sums[l]["g_m"], sums[l]["sh_f"], sums[l]["sc_f"]) = _layer_bwd(
            dx1a, dyf, mod[l], layer_w[l], saved[l], l, depth, stacked)
        lgrads[l]["ln2_g"], lgrads[l]["ln2_b"] = dlg.reshape(-1), dlb.reshape(-1)
        stacked = {n: lgrads[l][n] for n in COL_SPLIT + ROW_SPLIT}
        if l > 0:
            prev, p = saved[l - 1], layer_w[l - 1]
            dx1a, dyf, dlg, dlb, dg_f, sums[l]["sc_m"], sums[l]["sh_m"] = _rowwise(
                f"boundary_bwd_{l}", boundary_bwd, [(dh1, 0, d), (dxa, 0, d), (prev["x1"], 0, d), (prev["yf"], 0, d)],
                [rows(l)[1], rows(l - 1)[5], _row(p["ln2_g"]), _row(p["ln2_b"])], [(d, F32), (d, BF16)], [d] * 5)
        else:
            dx, sums[l]["sc_m"], sums[l]["sh_m"] = _rowwise("lnmod1_bwd", _lnmod_bwd, [(x, 0, d), (dh1, 0, d), (dxa, 0, d)],
                                                            [rows(0)[1]], [(d, F32)], [d, d])
    dmod = jnp.stack([jnp.concatenate([sums[l][k] for k in ("sh_m", "sc_m", "g_m", "sh_f", "sc_f", "g_f")], axis=0)
                      for l in range(depth)])
    return loss_cols, dx, dmod, lgrads, stacked


def _place():
    return lax.axis_index("x"), lax.axis_index("y"), lax.axis_index("c")


def _all_gather8(name, block):
    m_per, n = block.shape

    def body(x_ref, out_ref, send_sems, recv_sems, local_sem):
        x, y, c = _place()
        me, sibling = (x, y, c), (x, y, 1 - c)
        chips = [(1 - x, y), (x, 1 - y), (1 - x, 1 - y)]

        def rows(px, py, pc):
            return out_ref.at[pl.ds(pl.multiple_of((4 * px + 2 * py + pc) * m_per, 8), m_per), :]

        def copy(k, blk, to, src=None):
            return pltpu.make_async_remote_copy(src_ref=rows(*blk) if src is None else src, dst_ref=rows(*blk),
                                                send_sem=send_sems.at[k], recv_sem=recv_sems.at[k],
                                                device_id=to, device_id_type=MESH)

        mine = pltpu.make_async_copy(x_ref, rows(*me), local_sem)
        mine.start()
        first = [copy(0, me, sibling, src=x_ref)] + [copy(1 + j, me, (*chip, c), src=x_ref) for j, chip in enumerate(chips)]
        for cp in first:
            cp.start()
        passed = [copy(4 + j, (*chip, c), sibling) for j, chip in enumerate(chips)]
        for j, chip in enumerate(chips):
            copy(1 + j, (*chip, c), me).wait_recv()
            passed[j].start()
        copy(0, sibling, me).wait_recv()
        for j, chip in enumerate(chips):
            copy(4 + j, (*chip, 1 - c), me).wait_recv()
        for cp in first + passed:
            cp.wait_send()
        mine.wait()

    return pl.pallas_call(
        body, name=name, out_shape=jax.ShapeDtypeStruct((8 * m_per, n), block.dtype),
        in_specs=[pl.BlockSpec(memory_space=pltpu.VMEM)], out_specs=pl.BlockSpec(memory_space=pltpu.VMEM),
        scratch_shapes=[pltpu.SemaphoreType.DMA((7,)), pltpu.SemaphoreType.DMA((7,)), pltpu.SemaphoreType.DMA],
        compiler_params=_params(),
    )(block)


def _other_chips(x, y):
    return [(1 - x, y), (x, 1 - y), (1 - x, 1 - y)]


def _gather_weights(whole, by_rows):
    n = len(whole)

    def body(*refs):
        dst = refs[n:2 * n]
        ici_send, ici_recv, d2d_send, d2d_recv = refs[2 * n:]
        x, y, c = _place()
        chips = _other_chips(x, y)

        def part(ref, k, px, py, pc):
            _, r, cols = whole[k].shape
            q = 2 * px + py
            if by_rows[k]:
                return ref[k].at[:, pl.ds(pl.multiple_of((2 * q + pc) * (r // 8), 16), r // 8), :]
            return ref[k].at[:, pl.ds(pl.multiple_of(pc * (r // 2), 16), r // 2),
                             pl.ds(pl.multiple_of(q * (cols // 4), LANES), cols // 4)]

        def ici(k, j, px, py, to):
            return pltpu.make_async_remote_copy(src_ref=part(dst, k, px, py, c), dst_ref=part(dst, k, px, py, c),
                                                send_sem=ici_send.at[k, j], recv_sem=ici_recv.at[k, j],
                                                device_id=(*to, c), device_id_type=MESH)

        def d2d(k, j, px, py, pc):
            return pltpu.make_async_remote_copy(src_ref=part(dst, k, px, py, pc), dst_ref=part(dst, k, px, py, pc),
                                                send_sem=d2d_send.at[k, j], recv_sem=d2d_recv.at[k, j],
                                                device_id=(x, y, 1 - c), device_id_type=MESH)

        for k in range(n):
            for j, chip in enumerate(chips):
                ici(k, j, x, y, chip).start()
        for k in range(n):
            for j, chip in enumerate(chips):
                ici(k, j, *chip, chip).wait_recv()
                d2d(k, j, *chip, c).start()
        for k in range(n):
            for j, chip in enumerate(chips):
                d2d(k, j, *chip, 1 - c).wait_recv()
        for k in range(n):
            for j, chip in enumerate(chips):
                ici(k, j, x, y, chip).wait_send()
                d2d(k, j, *chip, c).wait_send()

    any_spec = pl.BlockSpec(memory_space=pl.ANY)
    return pl.pallas_call(
        body, name="gather_weights", in_specs=[any_spec] * n, out_specs=[any_spec] * n,
        out_shape=[jax.ShapeDtypeStruct(a.shape, a.dtype) for a in whole], input_output_aliases={k: k for k in range(n)},
        scratch_shapes=[pltpu.SemaphoreType.DMA((n, 3))] * 4,
        compiler_params=_params(),
    )(*whole)


def _part_shape(shape, by_rows):
    l, r, c = shape
    return (l, r // 8, c) if by_rows else (l, r // 2, c // 4)


def _pair_exchange(grads, by_rows):
    n = len(grads)

    def body(*refs):
        src, dst = refs[:n], refs[n:2 * n]
        send_sems, recv_sems = refs[2 * n:]
        x, y, c = _place()

        def window(k, q, pc):
            _, hr, hc = _part_shape(grads[k].shape, by_rows[k])
            if by_rows[k]:
                return src[k].at[:, pl.ds(pl.multiple_of((2 * q + pc) * hr, 16), hr), :]
            return src[k].at[:, pl.ds(pl.multiple_of(pc * hr, 16), hr), pl.ds(q * hc, hc)]

        def copy(k, q, pc):
            return pltpu.make_async_remote_copy(src_ref=window(k, q, pc), dst_ref=dst[k].at[q], send_sem=send_sems.at[k, q],
                                                recv_sem=recv_sems.at[k, q], device_id=(x, y, 1 - c), device_id_type=MESH)

        for k in range(n):
            for q in range(4):
                copy(k, q, 1 - c).start()
        for k in range(n):
            for q in range(4):
                copy(k, q, c).wait_recv()
        for k in range(n):
            for q in range(4):
                copy(k, q, 1 - c).wait_send()

    any_spec = pl.BlockSpec(memory_space=pl.ANY)
    return pl.pallas_call(
        body, name="pair_exchange", in_specs=[any_spec] * n, out_specs=[any_spec] * n,
        out_shape=[jax.ShapeDtypeStruct((4, *_part_shape(g.shape, rows)), g.dtype) for g, rows in zip(grads, by_rows)],
        scratch_shapes=[pltpu.SemaphoreType.DMA((n, 4)), pltpu.SemaphoreType.DMA((n, 4))],
        compiler_params=_params(),
    )(*grads)


def _pair_sum(name, g, theirs, by_rows, c, chip):
    _, l, hr, hc = theirs.shape
    tr = _pick(hr, (256, 176, 128, 64, 32))

    def body(s_ref, g_ref, t_ref, p_ref, own_ref):
        v = (g_ref[...].astype(F32) + t_ref[0].astype(F32)).astype(BF16)
        p_ref[0] = v

        @pl.when(pl.program_id(2) == s_ref[1])
        def _():
            own_ref[0] = v

    if by_rows:
        g_spec = pl.BlockSpec((1, tr, hc), lambda li, i, q, s: (li, (2 * q + s[0]) * (hr // tr) + i, 0))
    else:
        g_spec = pl.BlockSpec((1, tr, hc), lambda li, i, q, s: (li, s[0] * (hr // tr) + i, q))
    slot = pl.BlockSpec((1, 1, tr, hc), lambda li, i, q, s: (q, li, i, 0))
    grid_spec = pltpu.PrefetchScalarGridSpec(
        num_scalar_prefetch=1, grid=(l, hr // tr, 4), in_specs=[g_spec, slot],
        out_specs=[slot, pl.BlockSpec((1, 1, tr, hc), lambda li, i, q, s: (s[1], li, i, 0))])
    return pl.pallas_call(
        body, name=name, grid_spec=grid_spec, out_shape=[jax.ShapeDtypeStruct(theirs.shape, BF16)] * 2,
        compiler_params=_params(dimension_semantics=("arbitrary", "arbitrary", "arbitrary")),
    )(jnp.stack([c, chip]).astype(jnp.int32), g, theirs)


def _chip_scatter(sums, landing):
    n = len(sums)

    def body(*refs):
        src, dst = refs[:n], refs[2 * n:3 * n]
        send_sems, recv_sems = refs[3 * n:]
        x, y, c = _place()
        mine = 2 * x + y

        def copy(k, j, src_slot, dst_slot, to):
            return pltpu.make_async_remote_copy(src_ref=src[k].at[src_slot], dst_ref=dst[k].at[dst_slot],
                                                send_sem=send_sems.at[k, j], recv_sem=recv_sems.at[k, j],
                                                device_id=(*to, c), device_id_type=MESH)

        chips = _other_chips(x, y)
        for k in range(n):
            for j, (px, py) in enumerate(chips):
                copy(k, j, 2 * px + py, mine, (px, py)).start()
        for k in range(n):
            for j, (px, py) in enumerate(chips):
                copy(k, j, mine, 2 * px + py, (px, py)).wait_recv()
        for k in range(n):
            for j, (px, py) in enumerate(chips):
                copy(k, j, 2 * px + py, mine, (px, py)).wait_send()

    any_spec = pl.BlockSpec(memory_space=pl.ANY)
    return pl.pallas_call(
        body, name="chip_scatter", in_specs=[any_spec] * (2 * n), out_specs=[any_spec] * n,
        out_shape=[jax.ShapeDtypeStruct(a.shape, a.dtype) for a in landing],
        input_output_aliases={n + k: k for k in range(n)},
        scratch_shapes=[pltpu.SemaphoreType.DMA((n, 3)), pltpu.SemaphoreType.DMA((n, 3))],
        compiler_params=_params(),
    )(*sums, *landing)


def _sum_slots(name, parts, half=None):
    slots, l, r, c = parts.shape
    tr = _pick(r, (256, 176, 128, 64, 32, 8))

    def body(*refs):
        p_ref, o_ref = refs[-2:]
        acc = p_ref[0].astype(F32)
        for i in range(1, slots):
            acc = acc + p_ref[i].astype(F32)
        o_ref[...] = acc

    if half is None:
        return pl.pallas_call(
            body, name=name, grid=(l, r // tr), in_specs=[pl.BlockSpec((slots, 1, tr, c), lambda li, i: (0, li, i, 0))],
            out_specs=pl.BlockSpec((1, tr, c), lambda li, i: (li, i, 0)), out_shape=jax.ShapeDtypeStruct((l, r, c), F32),
            compiler_params=_params(dimension_semantics=("arbitrary", "arbitrary")),
        )(parts)
    grid_spec = pltpu.PrefetchScalarGridSpec(
        num_scalar_prefetch=1, grid=(l, r // tr),
        in_specs=[pl.BlockSpec((slots, 1, tr, c), lambda li, i, h: (0, li, i, 0))],
        out_specs=pl.BlockSpec((1, tr, c), lambda li, i, h: (li, h[0] * (r // tr) + i, 0)))
    return pl.pallas_call(
        body, name=name, grid_spec=grid_spec, out_shape=jax.ShapeDtypeStruct((l, 2 * r, c), F32),
        compiler_params=_params(dimension_semantics=("arbitrary", "arbitrary")),
    )(jnp.reshape(half, (1,)).astype(jnp.int32), parts)


def _swap_halves(blocks):
    n = len(blocks)

    def body(*refs):
        src, dst = refs[:n], refs[n:2 * n]
        send_sems, recv_sems = refs[2 * n:]
        x, y, c = _place()

        def half(ref, k, pc):
            r = blocks[k].shape[1] // 2
            return ref[k].at[:, pl.ds(pl.multiple_of(pc * r, 8), r), :]

        def copy(k, pc):
            return pltpu.make_async_remote_copy(src_ref=half(src, k, pc), dst_ref=half(dst, k, pc), send_sem=send_sems.at[k],
                                                recv_sem=recv_sems.at[k], device_id=(x, y, 1 - c), device_id_type=MESH)

        for k in range(n):
            copy(k, c).start()
        for k in range(n):
            copy(k, 1 - c).wait_recv()
        for k in range(n):
            copy(k, c).wait_send()

    any_spec = pl.BlockSpec(memory_space=pl.ANY)
    return pl.pallas_call(
        body, name="swap_halves", in_specs=[any_spec] * n, out_specs=[any_spec] * n,
        out_shape=[jax.ShapeDtypeStruct(b.shape, b.dtype) for b in blocks], input_output_aliases={k: k for k in range(n)},
        scratch_shapes=[pltpu.SemaphoreType.DMA((n,)), pltpu.SemaphoreType.DMA((n,))],
        compiler_params=_params(),
    )(*blocks)


def _adamw(name, w, g, m, v):
    shape = w.shape
    cols = shape[-1]
    flat = lambda a: a.reshape(-1, cols)
    rows = w.size // cols
    tr = _pick(rows, [r for r in (512, 256, 128, 64, 32, 16, 8) if r * cols <= 256 * 1024]) if rows % 8 == 0 else rows

    def body(w_ref, g_ref, m_ref, v_ref, go_ref, d_ref, nm_ref, nv_ref):
        gg = g_ref[...]
        go_ref[...] = gg
        nm = ADAM_B1 * m_ref[...] + (1.0 - ADAM_B1) * gg
        nv = ADAM_B2 * v_ref[...] + (1.0 - ADAM_B2) * (gg * gg)
        m_hat = nm / (1.0 - ADAM_B1 ** ADAM_STEP)
        v_hat = nv / (1.0 - ADAM_B2 ** ADAM_STEP)
        d_ref[...] = -ADAM_LR * (m_hat / (jnp.sqrt(v_hat) + ADAM_EPS) + ADAM_WD * w_ref[...])
        nm_ref[...] = nm
        nv_ref[...] = nv

    spec = pl.BlockSpec((tr, cols), lambda i: (i, 0))
    out = pl.pallas_call(
        body, name=name, grid=(rows // tr,), in_specs=[spec] * 4, out_specs=[spec] * 4,
        out_shape=[jax.ShapeDtypeStruct((rows, cols), F32)] * 4,
        compiler_params=_params(dimension_semantics=("arbitrary",)),
    )(flat(w), flat(g), flat(m), flat(v))
    return tuple(o.reshape(shape) for o in out)


WEIGHTS = ["w_ada", "b_ada", "w_in", "w_sb_up", "ssm_a_re", "ssm_a_im", "ssm_log_dt", "ssm_b_re", "ssm_b_im", "ssm_c_re",
           "ssm_c_im", "ssm_d", "w_glu", "b_glu", "w_ssm_up", "w_out", "ln1_g", "ln1_b", "w_ffn_in", "w_ffn_out", "ln2_g",
           "ln2_b"]
COL_SPLIT = ["w_in", "w_sb_up", "w_ssm_up", "w_ffn_in"]
ROW_SPLIT = ["w_glu", "w_out", "w_ffn_out"]
SMALL = ["ssm_a_re", "ssm_a_im", "ssm_log_dt", "ssm_b_re", "ssm_b_im", "ssm_c_re", "ssm_c_im", "ssm_d", "b_glu", "ln1_g",
         "ln1_b", "ln2_g", "ln2_b"]
SLAB_COLS = 1024


def _cast_into_whole(name, w, by_rows, chip):
    l, r, cols = w.shape
    tr = _pick(r, (512, 256, 128, 64, 16))

    def body(q_ref, w_ref, o_ref):
        o_ref[...] = w_ref[...].astype(BF16)

    if by_rows:
        out_map, shape = (lambda li, i, q: (li, q[0] * (r // tr) + i, 0)), (l, 4 * r, cols)
    else:
        out_map, shape = (lambda li, i, q: (li, i, q[0])), (l, r, 4 * cols)
    grid_spec = pltpu.PrefetchScalarGridSpec(
        num_scalar_prefetch=1, grid=(l, r // tr), in_specs=[pl.BlockSpec((1, tr, cols), lambda li, i, q: (li, i, 0))],
        out_specs=pl.BlockSpec((1, tr, cols), out_map))
    return pl.pallas_call(body, name=name, grid_spec=grid_spec, out_shape=jax.ShapeDtypeStruct(shape, BF16),
                          compiler_params=_params(dimension_semantics=("arbitrary", "arbitrary")),
                          )(jnp.reshape(chip, (1,)).astype(jnp.int32), w)


def _silu_rows(name, c):
    def body(c_ref, o_ref):
        v = c_ref[...]
        o_ref[...] = v * jax.nn.sigmoid(v)

    return pl.pallas_call(body, name=name, out_shape=jax.ShapeDtypeStruct(c.shape, F32), compiler_params=_params())(c)


def _pad_rows(v, mult=8):
    flat = v.reshape(-1)
    per = mult * SLAB_COLS
    total = -(-flat.size // per) * per
    return jnp.pad(flat, (0, total - flat.size)).reshape(-1, SLAB_COLS)


def kernel(x, c, w_ada, b_ada, w_in, w_sb_up, ssm_a_re, ssm_a_im, ssm_log_dt, ssm_b_re, ssm_b_im, ssm_c_re, ssm_c_im, ssm_d, w_glu, b_glu, w_ssm_up, w_out, ln1_g, ln1_b, w_ffn_in, w_ffn_out, ln2_g, ln2_b, loss_target, m_w_ada, m_b_ada, m_w_in, m_w_sb_up, m_ssm_a_re, m_ssm_a_im, m_ssm_log_dt, m_ssm_b_re, m_ssm_b_im, m_ssm_c_re, m_ssm_c_im, m_ssm_d, m_w_glu, m_b_glu, m_w_ssm_up, m_w_out, m_ln1_g, m_ln1_b, m_w_ffn_in, m_w_ffn_out, m_ln2_g, m_ln2_b, v_w_ada, v_b_ada, v_w_in, v_w_sb_up, v_ssm_a_re, v_ssm_a_im, v_ssm_log_dt, v_ssm_b_re, v_ssm_b_im, v_ssm_c_re, v_ssm_c_im, v_ssm_d, v_w_glu, v_b_glu, v_w_ssm_up, v_w_out, v_ln1_g, v_ln1_b, v_w_ffn_in, v_w_ffn_out, v_ln2_g, v_ln2_b):
    args = dict(locals())
    w = {n: args[n] for n in WEIGHTS}
    mom = {n: args["m_" + n] for n in WEIGHTS}
    var = {n: args["v_" + n] for n in WEIGHTS}
    depth, d = w_ada.shape[0], x.shape[-1]
    xi, yi, ci = _place()
    me, chip = 4 * xi + 2 * yi + ci, 2 * xi + yi
    ada_cols = w_ada.shape[2]

    big = COL_SPLIT + ROW_SPLIT
    by_rows = [n in ROW_SPLIT for n in big]
    full = dict(zip(big, _gather_weights([_cast_into_whole(f"cast_{n}", w[n], n in ROW_SPLIT, chip) for n in big], by_rows)))

    c_all = _all_gather8("gather_c", jnp.pad(c, ((0, 7), (0, 0))))[::8]
    c_act = _silu_rows("silu_c", c_all)
    b_cols = lax.dynamic_slice_in_dim(b_ada, chip * ada_cols, ada_cols, axis=1)
    mod_part = jnp.concatenate([_small_mm(f"mod_{l}", c_act, w_ada[l], "nn") + b_cols[l][None] for l in range(depth)], axis=0)
    mod_all = _all_gather8("gather_mod", mod_part).reshape(4, 2, depth, 8, ada_cols)[:, 0]
    mod_mine = lax.dynamic_index_in_dim(mod_all, me, axis=2, keepdims=False)
    mod = mod_mine.transpose(1, 0, 2).reshape(depth, 6, d)

    layer_w = [{**{n: (full[n], l) for n in big}, **{n: w[n][l] for n in SMALL}} for l in range(depth)]
    loss_cols, dx, dmods, lgrads, stacked = _local_step(x[0], loss_target[0], mod, layer_w)
    loss = lax.psum(jnp.sum(loss_cols), ("x", "y", "c"))
    grad_x = dx[None]

    theirs = _pair_exchange([stacked[n] for n in big], by_rows)
    pairs = [_pair_sum(f"pair_{n}", stacked[n], t, n in ROW_SPLIT, ci, chip) for n, t in zip(big, theirs)]
    landed = _chip_scatter([p[0] for p in pairs], [p[1] for p in pairs])
    halves = [_sum_slots(f"sum_{n}", p, half=ci) for n, p in zip(big, landed)]
    grad = dict(zip(big, _swap_halves(halves)))

    pieces = [dmods] + [jnp.stack([lgrads[l][n] for l in range(depth)]) for n in SMALL]
    slab = _pad_rows(jnp.concatenate([p.reshape(-1) for p in pieces]))
    slabs = _all_gather8("gather_small", slab).reshape(8, 1, *slab.shape)
    total = _sum_slots("sum_small", slabs)[0].reshape(-1)
    at = dmods.size
    grad["b_ada"] = total[:at].reshape(depth, 6 * d)
    for n, p in zip(SMALL, pieces[1:]):
        grad[n] = total[at:at + p.size].reshape(p.shape)
        at += p.size
    dmod_all = slabs.reshape(8, -1)[:, :dmods.size].reshape(8, depth, 4, ada_cols)
    dmod_cols = lax.dynamic_index_in_dim(dmod_all, chip, axis=2, keepdims=False)
    grad["w_ada"] = jnp.stack([_small_mm(f"dw_ada_{l}", c_act, dmod_cols[:, l], "tn") for l in range(depth)])

    delta, new_m, new_v = {}, {}, {}
    for n in WEIGHTS:
        grad[n], delta[n], new_m[n], new_v[n] = _adamw(f"adamw_{n}", w[n], grad[n], mom[n], var[n])
    return (loss, grad_x, *[grad[n] for n in WEIGHTS], *[delta[n] for n in WEIGHTS], *[new_m[n] for n in WEIGHTS],
            *[new_v[n] for n in WEIGHTS])
```
